```python
import jax
import jax.numpy as jnp
from jax import lax
import numpy as np

D_MODEL = 1024
BATCH = 1
SEQ = 16384
DEPTH = 1

GRID_W = 64
CTX_LEN = 256
D_MIX = D_MODEL
D_MLSTM = D_MIX // 2
MLSTM_HEADS = 4
MLSTM_DH = D_MLSTM // MLSTM_HEADS
QKV_BLOCK = 4
N_QKV_BLOCKS = D_MLSTM // QKV_BLOCK
CONV_K = 3
CHUNK = 128
D_FOURIER = D_MIX - D_MLSTM
FOURIER_GROUPS = 4
FOURIER_CG = D_FOURIER // FOURIER_GROUPS
D_IN_PROJ = 2 * D_MLSTM + D_FOURIER
N_EXPERT_GROUPS = 4
EXPERTS_PER_GROUP = 4
N_EXPERTS = N_EXPERT_GROUPS * EXPERTS_PER_GROUP
TOP_K = 2
D_EXPERT = D_MODEL // 2
N_MOD = 6
EPS = 1e-6
POS_BASE = 10000.0
F32 = jnp.float32

kernel_name = 'hybrid_mlstm_fourier_hmoe_prefix_block'


def rmsnorm(x, g):
    x32 = x.astype(F32)
    y = x32 * lax.rsqrt(jnp.mean(x32 * x32, axis=-1, keepdims=True) + EPS)
    return (y * g.astype(F32)).astype(x.dtype)


def modulate(h, shift, scale):
    return h * (1 + scale) + shift


def sincos_2d(rows, cols):
    quarter = D_MODEL // 4
    freq = 1.0 / (POS_BASE ** (jnp.arange(quarter, dtype=F32) / quarter))
    r = jnp.arange(rows, dtype=F32)[:, None] * freq
    cl = jnp.arange(cols, dtype=F32)[:, None] * freq
    er = jnp.concatenate([jnp.sin(r), jnp.cos(r)], axis=-1)
    ec = jnp.concatenate([jnp.sin(cl), jnp.cos(cl)], axis=-1)
    half = D_MODEL // 2
    emb = jnp.concatenate([jnp.broadcast_to(er[:, None, :], (rows, cols, half)),
                           jnp.broadcast_to(ec[None, :, :], (rows, cols, half))], axis=-1)
    return emb.reshape(rows * cols, D_MODEL)


def short_conv(x, w, b):
    y = lax.conv_general_dilated(x, w[:, None, :], window_strides=(1,), padding='SAME',
                                 dimension_numbers=('NWC', 'WIO', 'NWC'),
                                 feature_group_count=x.shape[-1])
    return y + b


def headwise(x, w):
    y = jnp.einsum('btni,nio->btno', x.reshape(x.shape[:-1] + (N_QKV_BLOCKS, QKV_BLOCK)), w)
    return y.reshape(x.shape)


def to_heads(t):
    B, T, _ = t.shape
    return t.astype(F32).reshape(B, T, MLSTM_HEADS, MLSTM_DH).transpose(0, 2, 1, 3)


def flip_seq(t):
    return jnp.flip(t, axis=2)


def mixer_features(h, lp):
    proj = h @ lp['w_in']
    x_m = proj[..., :D_MLSTM]
    z = proj[..., D_MLSTM:2 * D_MLSTM]
    u = proj[..., 2 * D_MLSTM:]
    act = jax.nn.silu(short_conv(x_m, lp['conv_w'], lp['conv_b']))
    q = headwise(act, lp['w_q'])
    k = headwise(act, lp['w_k'])
    v = headwise(x_m, lp['w_v'])
    qkv = jnp.concatenate([q, k, v], axis=-1)

    def gates(w, b):
        g = (qkv @ w + b).astype(F32).transpose(0, 2, 1)
        return g[:, :MLSTM_HEADS], jax.nn.log_sigmoid(g[:, MLSTM_HEADS:])

    heads = (to_heads(q) * (MLSTM_DH ** -0.5), to_heads(k), to_heads(v))
    return heads, gates(lp['w_if_fwd'], lp['b_if_fwd']), gates(lp['w_if_bwd'], lp['b_if_bwd']), act, z, u


def mlstm_scan(k, v, ig, lf, state):
    B, H, T, DH = k.shape
    nc = T // CHUNK
    kc = k.reshape(B, H, nc, CHUNK, DH)
    vc = v.reshape(B, H, nc, CHUNK, DH)
    b = jnp.cumsum(lf.reshape(B, H, nc, CHUNK), axis=-1)
    btot = b[..., -1]
    a = btot[..., None] - b + ig.reshape(B, H, nc, CHUNK)
    amax = jnp.max(a, axis=-1)

    def step(carry, xs):
        C, n, m = carry
        k_, v_, a_, amax_, bt_ = xs
        m_new = jnp.maximum(bt_ + m, amax_)
        decay = jnp.exp(bt_ + m - m_new)
        kw = k_ * jnp.exp(a_ - m_new[..., None])[..., None]
        C_new = decay[..., None, None] * C + jnp.einsum('bhlk,bhlv->bhkv', kw, v_)
        n_new = decay[..., None] * n + jnp.sum(kw, axis=2)
        return (C_new, n_new, m_new), (C, n, m)

    xs = tuple(jnp.moveaxis(t, 2, 0) for t in (kc, vc, a, amax, btot))
    final, starts = lax.scan(step, state, xs)
    starts = tuple(jnp.moveaxis(s, 0, 2) for s in starts)
    return starts, final


def mlstm_out(q, k, v, ig, lf, starts):
    C0, n0, m0 = starts
    B, H, T, DH = q.shape
    nc = T // CHUNK
    qc = q.reshape(B, H, nc, CHUNK, DH)
    kc = k.reshape(B, H, nc, CHUNK, DH)
    vc = v.reshape(B, H, nc, CHUNK, DH)
    igc = ig.reshape(B, H, nc, CHUNK)
    b = jnp.cumsum(lf.reshape(B, H, nc, CHUNK), axis=-1)
    tri = jnp.tril(jnp.ones((CHUNK, CHUNK), dtype=bool))
    logd = jnp.where(tri, b[..., :, None] - b[..., None, :] + igc[..., None, :], -jnp.inf)
    m_inter = b + m0[..., None]
    m_t = jnp.maximum(m_inter, jnp.max(logd, axis=-1))
    w_inter = jnp.exp(m_inter - m_t)
    s = jnp.einsum('bhctd,bhcsd->bhcts', qc, kc) * jnp.exp(logd - m_t[..., None])
    num = jnp.einsum('bhcts,bhcsd->bhctd', s, vc) + w_inter[..., None] * jnp.einsum('bhctk,bhckv->bhctv', qc, C0)
    nq = jnp.sum(s, axis=-1) + w_inter * jnp.einsum('bhctk,bhck->bhct', qc, n0)
    h = num / jnp.maximum(jnp.abs(nq), jnp.exp(-m_t))[..., None]
    return h.reshape(B, H, T, DH)


def mlstm_direction(ctx_in, lat_in, with_ctx):
    qc, kc, vc, igc, lfc = ctx_in
    ql, kl, vl, igl, lfl = lat_in
    B = kl.shape[0]
    zero = (jnp.zeros((B, MLSTM_HEADS, MLSTM_DH, MLSTM_DH), F32),
            jnp.zeros((B, MLSTM_HEADS, MLSTM_DH), F32),
            jnp.zeros((B, MLSTM_HEADS), F32))
    ctx_starts, ctx_final = mlstm_scan(kc, vc, igc, lfc, zero)
    lat_starts, _ = mlstm_scan(kl, vl, igl, lfl, ctx_final)
    h_lat = mlstm_out(ql, kl, vl, igl, lfl, lat_starts)
    h_ctx = mlstm_out(qc, kc, vc, igc, lfc, ctx_starts) if with_ctx else None
    return h_lat, h_ctx


def mlstm_readout(hf, hb, act, z, norm_w, skip):
    h = hf + hb
    mu = jnp.mean(h, axis=-1, keepdims=True)
    var = jnp.mean(jnp.square(h - mu), axis=-1, keepdims=True)
    hn = (h - mu) * lax.rsqrt(var + EPS)
    B, H, T, DH = hn.shape
    hn = hn.transpose(0, 2, 1, 3).reshape(B, T, D_MLSTM) * norm_w.astype(F32)
    out = (hn + skip.astype(F32) * act.astype(F32)) * jax.nn.silu(z.astype(F32))
    return out.astype(z.dtype)


def fourier_mix(u, w_fourier):
    B, T, _ = u.shape
    ug = u.astype(F32).reshape(B, T, FOURIER_GROUPS, FOURIER_CG)
    f = jnp.fft.fft2(ug, axes=(1, 3), norm='ortho').real
    y = jnp.einsum('btgc,gcd->btgd', f, w_fourier.astype(F32))
    return y.reshape(B, T, D_FOURIER).astype(u.dtype)


def token_mix(h, hc, lp, with_ctx):
    qkv_l, gf_l, gb_l, act_l, z_l, u_l = mixer_features(h, lp)
    qkv_c, gf_c, gb_c, act_c, z_c, u_c = mixer_features(hc, lp)
    hf_l, hf_c = mlstm_direction(qkv_c + gf_c, qkv_l + gf_l, with_ctx)
    hb_l, hb_c = mlstm_direction(tuple(flip_seq(t) for t in qkv_c + gb_c),
                                 tuple(flip_seq(t) for t in qkv_l + gb_l), with_ctx)

    def merge(hf, hb, act, z, u):
        m = mlstm_readout(hf, flip_seq(hb), act, z, lp['mlstm_norm_w'], lp['mlstm_skip'])
        f = fourier_mix(u, lp['w_fourier'])
        return jnp.concatenate([m, f], axis=-1) @ lp['w_out']

    y_l = merge(hf_l, hb_l, act_l, z_l, u_l)
    y_c = merge(hf_c, hb_c, act_c, z_c, u_c) if with_ctx else None
    return y_l, y_c


def hier_moe(h, lp):
    B, T, _ = h.shape
    g_logits = (h @ lp['w_router_group'] + lp['b_router_group']).astype(F32)
    p_group = jax.nn.softmax(g_logits, axis=-1)
    g_sel = jnp.argmax(g_logits, axis=-1)
    p_sel = jnp.take_along_axis(p_group, g_sel[..., None], axis=-1)
    e_logits = (h @ lp['w_router_expert'] + lp['b_router_expert']).astype(F32)
    e_logits = e_logits.reshape(B, T, N_EXPERT_GROUPS, EXPERTS_PER_GROUP)
    e_sel = jnp.take_along_axis(e_logits, g_sel[..., None, None], axis=2)[..., 0, :]
    top_v, top_i = lax.top_k(e_sel, TOP_K)
    w_top = jax.nn.softmax(top_v, axis=-1) * p_sel
    expert_id = g_sel[..., None] * EXPERTS_PER_GROUP + top_i
    combine = jnp.sum(jax.nn.one_hot(expert_id, N_EXPERTS, dtype=F32) * w_top[..., None], axis=-2)
    combine = combine.astype(h.dtype)
    y = jnp.zeros_like(h)
    for e in range(N_EXPERTS):
        a = jax.nn.silu(h @ lp['w_gate'][e]) * (h @ lp['w_up'][e])
        y = y + combine[..., e, None] * (a @ lp['w_down'][e])
    return y


def setup_inputs(seed: int = 0) -> dict:
    key = jax.random.key(seed)
    ks = jax.random.split(key, 40)
    L = DEPTH
    H = MLSTM_HEADS

    def nrm(k, shape, s):
        return jax.random.normal(k, shape, F32) * s

    def b_if(k1, k2):
        return jnp.concatenate([nrm(k1, (L, H), 0.1),
                                jnp.linspace(3.0, 6.0, H, dtype=F32)[None, :] + nrm(k2, (L, H), 0.1)], axis=-1)

    return {
        'x': nrm(ks[0], (BATCH, SEQ, D_MODEL), 1.0),
        'c': nrm(ks[1], (BATCH, D_MODEL), 1.0),
        'ctx': nrm(ks[2], (BATCH, CTX_LEN, D_MODEL), 1.0),
        'c_ctx': nrm(ks[3], (D_MODEL,), 1.0),
        'w_ada': nrm(ks[4], (L, D_MODEL, N_MOD * D_MODEL), 0.5 * D_MODEL ** -0.5),
        'b_ada': nrm(ks[5], (L, N_MOD * D_MODEL), 0.01),
        'g_pre_mix': 1.0 + nrm(ks[6], (L, D_MODEL), 0.05),
        'g_post_mix': 1.0 + nrm(ks[7], (L, D_MODEL), 0.05),
        'g_pre_ffn': 1.0 + nrm(ks[8], (L, D_MODEL), 0.05),
        'g_post_ffn': 1.0 + nrm(ks[9], (L, D_MODEL), 0.05),
        'w_in': nrm(ks[10], (L, D_MODEL, D_IN_PROJ), D_MODEL ** -0.5),
        'conv_w': nrm(ks[11], (L, CONV_K, D_MLSTM), CONV_K ** -0.5),
        'conv_b': nrm(ks[12], (L, D_MLSTM), 0.01),
        'w_q': nrm(ks[13], (L, N_QKV_BLOCKS, QKV_BLOCK, QKV_BLOCK), QKV_BLOCK ** -0.5),
        'w_k': nrm(ks[14], (L, N_QKV_BLOCKS, QKV_BLOCK, QKV_BLOCK), QKV_BLOCK ** -0.5),
        'w_v': nrm(ks[15], (L, N_QKV_BLOCKS, QKV_BLOCK, QKV_BLOCK), QKV_BLOCK ** -0.5),
        'w_if_fwd': nrm(ks[16], (L, 3 * D_MLSTM, 2 * H), 0.02),
        'b_if_fwd': b_if(ks[17], ks[18]),
        'w_if_bwd': nrm(ks[19], (L, 3 * D_MLSTM, 2 * H), 0.02),
        'b_if_bwd': b_if(ks[20], ks[21]),
        'mlstm_norm_w': 1.0 + nrm(ks[22], (L, D_MLSTM), 0.05),
        'mlstm_skip': 1.0 + nrm(ks[23], (L, D_MLSTM), 0.05),
        'w_fourier': nrm(ks[24], (L, FOURIER_GROUPS, FOURIER_CG, FOURIER_CG), FOURIER_CG ** -0.5),
        'w_out': nrm(ks[25], (L, D_MIX, D_MODEL), D_MIX ** -0.5),
        'w_router_group': nrm(ks[26], (L, D_MODEL, N_EXPERT_GROUPS), D_MODEL ** -0.5),
        'b_router_group': nrm(ks[27], (L, N_EXPERT_GROUPS), 0.01),
        'w_router_expert': nrm(ks[28], (L, D_MODEL, N_EXPERTS), D_MODEL ** -0.5),
        'b_router_expert': nrm(ks[29], (L, N_EXPERTS), 0.01),
        'w_gate': nrm(ks[30], (L, N_EXPERTS, D_MODEL, D_EXPERT), D_MODEL ** -0.5),
        'w_up': nrm(ks[31], (L, N_EXPERTS, D_MODEL, D_EXPERT), D_MODEL ** -0.5),
        'w_down': nrm(ks[32], (L, N_EXPERTS, D_EXPERT, D_MODEL), D_EXPERT ** -0.5),
    }


def reference(x, c, ctx, c_ctx, w_ada, b_ada, g_pre_mix, g_post_mix, g_pre_ffn, g_post_ffn,
              w_in, conv_w, conv_b, w_q, w_k, w_v, w_if_fwd, b_if_fwd, w_if_bwd, b_if_bwd,
              mlstm_norm_w, mlstm_skip, w_fourier, w_out, w_router_group, b_router_group,
              w_router_expert, b_router_expert, w_gate, w_up, w_down):
    T = x.shape[1]
    rows = T // GRID_W
    x = x + sincos_2d(rows, GRID_W).astype(x.dtype)[None]
    xc = ctx
    for l in range(DEPTH):
        with_ctx = l + 1 < DEPTH
        lp = {
            'w_in': w_in[l], 'conv_w': conv_w[l], 'conv_b': conv_b[l],
            'w_q': w_q[l], 'w_k': w_k[l], 'w_v': w_v[l],
            'w_if_fwd': w_if_fwd[l], 'b_if_fwd': b_if_fwd[l],
            'w_if_bwd': w_if_bwd[l], 'b_if_bwd': b_if_bwd[l],
            'mlstm_norm_w': mlstm_norm_w[l], 'mlstm_skip': mlstm_skip[l],
            'w_fourier': w_fourier[l], 'w_out': w_out[l],
            'w_router_group': w_router_group[l], 'b_router_group': b_router_group[l],
            'w_router_expert': w_router_expert[l], 'b_router_expert': b_router_expert[l],
            'w_gate': w_gate[l], 'w_up': w_up[l], 'w_down': w_down[l],
        }
        mod = [m[:, None, :] for m in jnp.split(jax.nn.silu(c) @ w_ada[l] + b_ada[l], N_MOD, axis=-1)]
        mod_c = [m[None, :] for m in jnp.split(jax.nn.silu(c_ctx) @ w_ada[l] + b_ada[l], N_MOD, axis=-1)]
        shift1, scale1, gate1, shift2, scale2, gate2 = mod
        h = modulate(rmsnorm(x, g_pre_mix[l]), shift1, scale1)
        hc = modulate(rmsnorm(xc, g_pre_mix[l]), mod_c[0], mod_c[1])
        y, yc = token_mix(h, hc, lp, with_ctx)
        x = x + gate1 * rmsnorm(y, g_post_mix[l])
        h2 = modulate(rmsnorm(x, g_pre_ffn[l]), shift2, scale2)
        x = x + gate2 * rmsnorm(hier_moe(h2, lp), g_post_ffn[l])
        if with_ctx:
            xc = xc + mod_c[2] * rmsnorm(yc, g_post_mix[l])
            hc2 = modulate(rmsnorm(xc, g_pre_ffn[l]), mod_c[3], mod_c[4])
            xc = xc + mod_c[5] * rmsnorm(hier_moe(hc2, lp), g_post_ffn[l])
    return x
```

```python
import functools

import numpy as np
import jax
import jax.numpy as jnp
from jax import lax
from jax.experimental import pallas as pl
from jax.experimental.pallas import tpu as pltpu

F32 = jnp.float32
BF16 = jnp.bfloat16

D_MODEL = 1024
SEQ = 16384
GRID_W = 64
CTX_LEN = 256
D_MLSTM = 512
HEADS = 4
DH = 128
QKV_BLOCK = 4
CONV_K = 3
CHUNK = 128
D_FOURIER = 512
FGROUPS = 4
FCG = 128
N_GROUPS = 4
EPG = 4
N_EXPERTS = 16
D_EXPERT = 512
N_MOD = 6
EPS = 1e-6
POS_BASE = 10000.0
LANES = 128
NEG_BIG = -3.0e38

VMEM_LIMIT = 52 * 1024 * 1024


def _cparams(*sem):
    return pltpu.CompilerParams(dimension_semantics=sem, vmem_limit_bytes=VMEM_LIMIT)


def _dot(a, b):
    return jnp.dot(a, b, preferred_element_type=F32)


def _dot_nt(a, b):
    return lax.dot_general(a, b, (((1,), (1,)), ((), ())), preferred_element_type=F32)


def _split_bf16(a):
    hi = a.astype(BF16)
    lo = (a - hi.astype(F32)).astype(BF16)
    return hi, lo


def _dot3(a, b):
    a_hi, a_lo = _split_bf16(a)
    b_hi, b_lo = _split_bf16(b)
    return _dot(a_hi, b_hi) + (_dot(a_hi, b_lo) + _dot(a_lo, b_hi))


def _sigmoid(x):
    return 1.0 / (1.0 + jnp.exp(-x))


def _rms(x):
    return x * lax.rsqrt(jnp.mean(x * x, axis=-1, keepdims=True) + EPS)


def _ada_kernel(c_ref, w_ref, b_ref, o_ref):
    c = c_ref[...]
    s = c * _sigmoid(c)
    o_ref[...] = _dot3(s, w_ref[...]) + b_ref[...]


def _ada(c8, w, b):
    n = w.shape[1]
    tn = 768
    return pl.pallas_call(
        _ada_kernel,
        grid=(n // tn,),
        in_specs=[pl.BlockSpec((8, D_MODEL), lambda j: (0, 0)),
                  pl.BlockSpec((D_MODEL, tn), lambda j: (0, j)),
                  pl.BlockSpec((1, tn), lambda j: (0, j))],
        out_specs=pl.BlockSpec((8, tn), lambda j: (0, j)),
        out_shape=jax.ShapeDtypeStruct((8, n), F32),
        compiler_params=_cparams("parallel"),
        name="ada",
    )(c8, w, b)


def _add_pos(x3, er_ref, ec_ref):
    r = x3.shape[0]
    pr = jnp.broadcast_to(er_ref[...], (r, GRID_W, D_MODEL // 2))
    pc = jnp.broadcast_to(ec_ref[...], (r, GRID_W, D_MODEL // 2))
    return x3 + jnp.concatenate([pr, pc], axis=-1)


def _inproj_kernel(x_ref, er_ref, ec_ref, g_ref, sh_ref, w_ref, xm_ref, z_ref, u_ref, *, add_pos):
    x3 = x_ref[...]
    if add_pos:
        x3 = _add_pos(x3, er_ref, ec_ref)
    x = x3.reshape(x3.shape[0] * GRID_W, D_MODEL)
    h = _rms(x) * g_ref[...] + sh_ref[...]
    proj = _dot(h.astype(BF16), w_ref[...])
    xm_ref[...] = proj[:, :D_MLSTM].astype(BF16)
    z_ref[...] = proj[:, D_MLSTM:2 * D_MLSTM].astype(BF16)
    u_ref[...] = proj[:, 2 * D_MLSTM:].astype(BF16)


def _inproj(x3, er3, ec3, g_eff, shift, w_in, *, rows, add_pos):
    nr = x3.shape[0]
    t = nr * GRID_W
    tm = rows * GRID_W
    out = jax.ShapeDtypeStruct((t, D_MLSTM), BF16)
    ospec = pl.BlockSpec((tm, D_MLSTM), lambda i: (i, 0))
    vec = pl.BlockSpec((1, D_MODEL), lambda i: (0, 0))
    return pl.pallas_call(
        functools.partial(_inproj_kernel, add_pos=add_pos),
        grid=(nr // rows,),
        in_specs=[pl.BlockSpec((rows, GRID_W, D_MODEL), lambda i: (i, 0, 0)),
                  pl.BlockSpec((rows, 1, D_MODEL // 2), lambda i: (i, 0, 0)),
                  pl.BlockSpec((1, GRID_W, D_MODEL // 2), lambda i: (0, 0, 0)),
                  vec, vec,
                  pl.BlockSpec(w_in.shape, lambda i: (0, 0))],
        out_specs=[ospec, ospec, ospec],
        out_shape=[out, out, out],
        compiler_params=_cparams("parallel"),
        name="inproj",
    )(x3, er3, ec3, g_eff, shift, w_in)


def _feat_kernel(xm_ref, prev_ref, next_ref, cw_ref, cb_ref, wq_ref, wkt_ref, wv_ref,
                 wiq_ref, wik_ref, wiv_ref, bi_ref,
                 q_ref, kt_ref, v_ref, act_ref, g_ref):
    i = pl.program_id(0)
    n = pl.num_programs(0)
    xm_bf = xm_ref[...]
    xm = xm_bf.astype(F32)
    tm = xm.shape[0]
    prev_row = prev_ref[...].astype(F32)[15:16, :] * jnp.where(i > 0, 1.0, 0.0)
    next_row = next_ref[...].astype(F32)[0:1, :] * jnp.where(i < n - 1, 1.0, 0.0)
    rid = lax.broadcasted_iota(jnp.int32, (tm, 1), 0)
    x_left = jnp.where(rid == 0, prev_row, pltpu.roll(xm, 1, 0))
    x_right = jnp.where(rid == tm - 1, next_row, pltpu.roll(xm, tm - 1, 0))
    cw = cw_ref[...]
    y = cw[0:1] * x_left + cw[1:2] * xm + cw[2:3] * x_right + cb_ref[...]
    act = (y * _sigmoid(y)).astype(BF16)
    act_ref[...] = act
    q = _dot(act, wq_ref[...])
    kt = _dot_nt(wkt_ref[...], act)
    v = _dot(xm_bf, wv_ref[...])
    q_bf = q.astype(BF16)
    kt_bf = kt.astype(BF16)
    v_bf = v.astype(BF16)
    q_ref[...] = (q * (DH ** -0.5)).astype(BF16)
    kt_ref[...] = kt_bf
    v_ref[...] = v_bf
    g = _dot_nt(wiq_ref[...], q_bf) + _dot(wik_ref[...], kt_bf) + _dot_nt(wiv_ref[...], v_bf)
    g_ref[...] = g + bi_ref[...]


def _feat(xm, conv_w, conv_b, wq, wkt, wv, wiq, wik, wiv, bi, *, tm):
    t = xm.shape[0]
    nb16 = t // 16
    k16 = tm // 16
    full = lambda a: pl.BlockSpec(a.shape, lambda i: (0,) * a.ndim)
    tok = pl.BlockSpec((tm, D_MLSTM), lambda i: (i, 0))
    return pl.pallas_call(
        _feat_kernel,
        grid=(t // tm,),
        in_specs=[tok,
                  pl.BlockSpec((16, D_MLSTM), lambda i: (jnp.maximum(i * k16 - 1, 0), 0)),
                  pl.BlockSpec((16, D_MLSTM), lambda i: (jnp.minimum((i + 1) * k16, nb16 - 1), 0)),
                  full(conv_w), full(conv_b), full(wq), full(wkt), full(wv),
                  full(wiq), full(wik), full(wiv), full(bi)],
        out_specs=[tok,
                   pl.BlockSpec((D_MLSTM, tm), lambda i: (0, i)),
                   tok, tok,
                   pl.BlockSpec((16, tm), lambda i: (0, i))],
        out_shape=[jax.ShapeDtypeStruct((t, D_MLSTM), BF16),
                   jax.ShapeDtypeStruct((D_MLSTM, t), BF16),
                   jax.ShapeDtypeStruct((t, D_MLSTM), BF16),
                   jax.ShapeDtypeStruct((t, D_MLSTM), BF16),
                   jax.ShapeDtypeStruct((16, t), F32)],
        compiler_params=_cparams("parallel"),
        name="feat",
    )(xm, xm, xm, conv_w, conv_b, wq, wkt, wv, wiq, wik, wiv, bi)


def _gates_kernel(g_ref, d_ref, gc_ref):
    g = g_ref[...]
    tl = g.shape[1]
    ig = g[0:8]
    fg = g[8:16]
    lf = jnp.minimum(fg, 0.0) - jnp.log(1.0 + jnp.exp(-jnp.abs(fg)))
    pos = lax.broadcasted_iota(jnp.int32, (8, tl), 1) & (CHUNK - 1)
    is_fwd = lax.broadcasted_iota(jnp.int32, (8, tl), 0) < HEADS

    def scan(x, op, ident):
        xf = x
        xb = x
        k = 1
        while k < CHUNK:
            xf = op(xf, jnp.where(pos >= k, pltpu.roll(xf, k, 1), ident))
            xb = op(xb, jnp.where(pos < CHUNK - k, pltpu.roll(xb, tl - k, 1), ident))
            k *= 2
        return jnp.where(is_fwd, xf, xb)

    b = scan(lf, jnp.add, 0.0)
    d = ig - b
    mloc = b + scan(d, jnp.maximum, NEG_BIG)
    d_ref[...] = d
    stack = jnp.concatenate([b, mloc, jnp.zeros((LANES - 16, tl), F32)], axis=0)
    gc_ref[...] = stack.T


def _gates(g, *, tl):
    t = g.shape[1]
    return pl.pallas_call(
        _gates_kernel,
        grid=(t // tl,),
        in_specs=[pl.BlockSpec((16, tl), lambda i: (0, i))],
        out_specs=[pl.BlockSpec((8, tl), lambda i: (0, i)),
                   pl.BlockSpec((tl, LANES), lambda i: (i, 0))],
        out_shape=[jax.ShapeDtypeStruct((8, t), F32),
                   jax.ShapeDtypeStruct((t, LANES), F32)],
        compiler_params=_cparams("parallel"),
        name="gates",
    )(g)


def _mlstm_kernel(*refs, cps, emit):
    (qf_ref, kf_ref, vf_ref, gcf_ref, drf_ref,
     qb_ref, kb_ref, vb_ref, gcb_ref, drb_ref, c0_ref, m0_ref) = refs[:12]
    if emit:
        hf_ref, hb_ref, cfin_ref, mfin_ref, c_scr, m_scr = refs[12:]
    else:
        cfin_ref, mfin_ref, c_scr, m_scr = refs[12:]
        hf_ref = hb_ref = None
    step = pl.program_id(0)

    @pl.when(step == 0)
    def _():
        c_scr[...] = c0_ref[...]
        m_scr[...] = m0_ref[...]

    ti = lax.broadcasted_iota(jnp.int32, (CHUNK, CHUNK), 0)
    si = lax.broadcasted_iota(jnp.int32, (CHUNK, CHUNK), 1)
    ones_col = jnp.where(si == 0, 1.0, 0.0).astype(BF16)

    for j in range(cps):
        for d in range(2):
            if d == 0:
                q_ref, k_ref, v_ref, gc_ref, dr_ref, h_ref = qf_ref, kf_ref, vf_ref, gcf_ref, drf_ref, hf_ref
                r0, mask, last = j * CHUNK, si <= ti, CHUNK - 1
            else:
                q_ref, k_ref, v_ref, gc_ref, dr_ref, h_ref = qb_ref, kb_ref, vb_ref, gcb_ref, drb_ref, hb_ref
                r0, mask, last = (cps - 1 - j) * CHUNK, si >= ti, 0
            rows = slice(r0, r0 + CHUNK)
            gc = gc_ref[rows, :]
            dr = dr_ref[:, rows]
            for hd in range(HEADS):
                r = d * HEADS + hd
                cols = slice(hd * DH, (hd + 1) * DH)
                b = gc[:, r:r + 1]
                mloc = gc[:, 8 + r:9 + r]
                drow = dr[r:r + 1, :]
                m0 = m_scr[r][0:1, 0:1]
                c_aug = c_scr[r]
                kt = k_ref[cols, rows]
                vaug = jnp.concatenate([v_ref[rows, cols], ones_col], axis=1)
                if emit:
                    qh = q_ref[rows, cols]
                    bm = b + m0
                    m_t = jnp.maximum(bm, mloc)
                    dmat = jnp.where(mask, jnp.exp((b - m_t) + drow), 0.0)
                    smat = (_dot(qh, kt) * dmat).astype(BF16)
                    num = _dot(smat, vaug) + jnp.exp(bm - m_t) * _dot(qh, c_aug.astype(BF16))
                    den = jnp.maximum(jnp.abs(num[:, DH:DH + 1]), jnp.exp(-m_t))
                    h_ref[rows, cols] = (num[:, :DH] / den).astype(h_ref.dtype)
                btot = gc[last:last + 1, r:r + 1]
                amax = gc[last:last + 1, 8 + r:9 + r]
                m_new = jnp.maximum(btot + m0, amax)
                decay = jnp.exp(btot + m0 - m_new)
                kw = (kt.astype(F32) * jnp.exp(btot + drow - m_new)).astype(BF16)
                c_scr[r] = decay * c_aug + _dot(kw, vaug)
                m_scr[r] = jnp.broadcast_to(m_new, (8, LANES))

    @pl.when(step == pl.num_programs(0) - 1)
    def _():
        cfin_ref[...] = c_scr[...]
        mfin_ref[...] = m_scr[...]


def _mlstm(q, kt, v, gc, dr, c0, m0, *, cps, emit):
    t = q.shape[0]
    cb = cps * CHUNK
    nb = t // cb
    fwd_r = lambda i: (i, 0)
    bwd_r = lambda i: (nb - 1 - i, 0)
    fwd_c = lambda i: (0, i)
    bwd_c = lambda i: (0, nb - 1 - i)
    tok = lambda f: pl.BlockSpec((cb, D_MLSTM), f)
    in_specs = []
    for fr, fc in ((fwd_r, fwd_c), (bwd_r, bwd_c)):
        in_specs += [tok(fr), pl.BlockSpec((D_MLSTM, cb), fc), tok(fr),
                     pl.BlockSpec((cb, LANES), fr), pl.BlockSpec((8, cb), fc)]
    cshape = (2 * HEADS, DH, 2 * DH)
    mshape = (2 * HEADS, 8, LANES)
    cspec = pl.BlockSpec(cshape, lambda i: (0, 0, 0))
    mspec = pl.BlockSpec(mshape, lambda i: (0, 0, 0))
    in_specs += [cspec, mspec]
    out_specs = [cspec, mspec]
    out_shape = [jax.ShapeDtypeStruct(cshape, F32), jax.ShapeDtypeStruct(mshape, F32)]
    if emit:
        out_specs = [tok(fwd_r), tok(bwd_r)] + out_specs
        out_shape = [jax.ShapeDtypeStruct((t, D_MLSTM), BF16)] * 2 + out_shape
    return pl.pallas_call(
        functools.partial(_mlstm_kernel, cps=cps, emit=emit),
        grid=(nb,),
        in_specs=in_specs,
        out_specs=out_specs,
        out_shape=out_shape,
        scratch_shapes=[pltpu.VMEM(cshape, F32), pltpu.VMEM(mshape, F32)],
        compiler_params=_cparams("arbitrary"),
        name="mlstm",
    )(q, kt, v, gc, dr, q, kt, v, gc, dr, c0, m0)


def _dft1_kernel(u_ref, f_ref, yc_ref, ys_ref):
    y = _dot(f_ref[...], u_ref[...])
    yc_ref[...] = y[:CHUNK].astype(BF16)
    ys_ref[...] = y[CHUNK:].astype(BF16)


def _dft1(u2d, f1, *, tn):
    n = u2d.shape[1]
    blk = pl.BlockSpec((CHUNK, tn), lambda j: (0, j))
    out = jax.ShapeDtypeStruct((CHUNK, n), BF16)
    return pl.pallas_call(
        _dft1_kernel,
        grid=(n // tn,),
        in_specs=[blk, pl.BlockSpec(f1.shape, lambda j: (0, 0))],
        out_specs=[blk, blk],
        out_shape=[out, out],
        compiler_params=_cparams("parallel"),
        name="dft1",
    )(u2d, f1)


def _dft2_kernel(yc_ref, ys_ref, cw_ref, sw_ref, a2_ref, cs_ref, wf_ref, o_ref, *, kb, scale):
    a2 = a2_ref[...]
    cs = cs_ref[...]
    for kk in range(kb):
        rows = slice(kk * CHUNK, (kk + 1) * CHUNK)
        yc = yc_ref[rows, :].astype(F32)
        ys = ys_ref[rows, :].astype(F32)
        cw = cw_ref[kk]
        sw = sw_ref[kk]
        p = jnp.concatenate([yc * cw - ys * sw, yc * sw + ys * cw], axis=0).astype(BF16)
        x = _dot(a2, p)
        for g in range(FGROUPS):
            cols = slice(g * FCG, (g + 1) * FCG)
            cat = jnp.concatenate([x[:CHUNK, cols], x[CHUNK:, cols]], axis=1).astype(BF16)
            f = _dot(cat, cs) * scale
            o_ref[:, kk * D_FOURIER + g * FCG:kk * D_FOURIER + (g + 1) * FCG] = (
                _dot(f.astype(BF16), wf_ref[g]).astype(o_ref.dtype))


def _dft2(yc, ys, cw3, sw3, a2, cs, wf, *, kb):
    t = yc.shape[0]
    n1 = t // CHUNK
    scale = float(1.0 / np.sqrt(float(t) * FCG))
    tok = pl.BlockSpec((kb * CHUNK, D_FOURIER), lambda i: (i, 0))
    tw = pl.BlockSpec((kb, CHUNK, 1), lambda i: (i, 0, 0))
    full = lambda a: pl.BlockSpec(a.shape, lambda i: (0,) * a.ndim)
    return pl.pallas_call(
        functools.partial(_dft2_kernel, kb=kb, scale=scale),
        grid=(n1 // kb,),
        in_specs=[tok, tok, tw, tw, full(a2), full(cs), full(wf)],
        out_specs=pl.BlockSpec((CHUNK, kb * D_FOURIER), lambda i: (0, i)),
        out_shape=jax.ShapeDtypeStruct((CHUNK, n1 * D_FOURIER), BF16),
        compiler_params=_cparams("parallel"),
        name="dft2",
    )(yc, ys, cw3, sw3, a2, cs, wf)


def _merge_kernel(hf_ref, hb_ref, act_ref, z_ref, yf_ref, x_ref, er_ref, ec_ref,
                  nw_ref, sk_ref, wout_ref, gg1_ref, g2_ref, sh2_ref, wr_ref, br_ref,
                  x1_ref, h2_ref, lg_ref):
    h = hf_ref[...].astype(F32) + hb_ref[...].astype(F32)
    parts = []
    for hd in range(HEADS):
        hh = h[:, hd * DH:(hd + 1) * DH]
        dl = hh - jnp.mean(hh, axis=-1, keepdims=True)
        var = jnp.mean(dl * dl, axis=-1, keepdims=True)
        parts.append(dl * lax.rsqrt(var + EPS))
    hn = jnp.concatenate(parts, axis=-1)
    z = z_ref[...].astype(F32)
    m = (hn * nw_ref[...] + sk_ref[...] * act_ref[...].astype(F32)) * (z * _sigmoid(z))
    cat = jnp.concatenate([m.astype(BF16), yf_ref[...]], axis=-1)
    y = _dot(cat, wout_ref[...])
    x3 = _add_pos(x_ref[...], er_ref, ec_ref)
    xp = x3.reshape(x3.shape[0] * GRID_W, D_MODEL)
    x1 = xp + _rms(y) * gg1_ref[...]
    x1_ref[...] = x1
    h2 = _rms(x1) * g2_ref[...] + sh2_ref[...]
    h2_ref[...] = h2.astype(BF16)
    lg = _dot3(h2, wr_ref[...]) + br_ref[...]
    lg_ref[...] = lg.T[:32]


def _merge(hf, hb, act, z, yf, x3, er3, ec3, nw, sk, wout, gg1, g2, sh2, wr, br, *, rows):
    nr = x3.shape[0]
    t = nr * GRID_W
    tm = rows * GRID_W
    tok = pl.BlockSpec((tm, D_MLSTM), lambda i: (i, 0))
    full = lambda a: pl.BlockSpec(a.shape, lambda i: (0,) * a.ndim)
    return pl.pallas_call(
        _merge_kernel,
        grid=(nr // rows,),
        in_specs=[tok, tok, tok, tok, tok,
                  pl.BlockSpec((rows, GRID_W, D_MODEL), lambda i: (i, 0, 0)),
                  pl.BlockSpec((rows, 1, D_MODEL // 2), lambda i: (i, 0, 0)),
                  pl.BlockSpec((1, GRID_W, D_MODEL // 2), lambda i: (0, 0, 0)),
                  full(nw), full(sk), full(wout), full(gg1), full(g2), full(sh2), full(wr), full(br)],
        out_specs=[pl.BlockSpec((tm, D_MODEL), lambda i: (i, 0)),
                   pl.BlockSpec((tm, D_MODEL), lambda i: (i, 0)),
                   pl.BlockSpec((32, tm), lambda i: (0, i))],
        out_shape=[jax.ShapeDtypeStruct((t, D_MODEL), F32),
                   jax.ShapeDtypeStruct((t, D_MODEL), BF16),
                   jax.ShapeDtypeStruct((32, t), F32)],
        compiler_params=_cparams("parallel"),
        name="merge",
    )(hf, hb, act, z, yf, x3, er3, ec3, nw, sk, wout, gg1, g2, sh2, wr, br)


def _route_kernel(lg_ref, comb_ref):
    lg = lg_ref[...]
    tl = lg.shape[1]
    g = [lg[j:j + 1] for j in range(N_GROUPS)]
    e = [lg[N_GROUPS + j:N_GROUPS + j + 1] for j in range(N_EXPERTS)]
    gmax = jnp.maximum(jnp.maximum(g[0], g[1]), jnp.maximum(g[2], g[3]))
    den = jnp.exp(g[0] - gmax) + jnp.exp(g[1] - gmax) + jnp.exp(g[2] - gmax) + jnp.exp(g[3] - gmax)
    p_sel = 1.0 / den
    sel = []
    free = jnp.ones((1, tl), F32)
    for j in range(N_GROUPS):
        s = jnp.where(g[j] >= gmax, free, 0.0)
        sel.append(s)
        free = free - s
    es = []
    for j in range(EPG):
        es.append(sel[0] * e[j] + sel[1] * e[EPG + j] + sel[2] * e[2 * EPG + j] + sel[3] * e[3 * EPG + j])
    rank = []
    for j in range(EPG):
        rj = jnp.zeros((1, tl), F32)
        for i in range(EPG):
            if i == j:
                continue
            beats = (es[i] >= es[j]) if i < j else (es[i] > es[j])
            rj = rj + jnp.where(beats, 1.0, 0.0)
        rank.append(rj)
    v1 = jnp.maximum(jnp.maximum(es[0], es[1]), jnp.maximum(es[2], es[3]))
    v2 = sum(jnp.where(rank[j] == 1.0, es[j], 0.0) for j in range(EPG))
    tt = jnp.exp(v2 - v1)
    w1 = p_sel / (1.0 + tt)
    w2 = w1 * tt
    w = [jnp.where(rank[j] == 0.0, w1, jnp.where(rank[j] == 1.0, w2, 0.0)) for j in range(EPG)]
    rows = [sel[gi] * w[j] for gi in range(N_GROUPS) for j in range(EPG)]
    stack = jnp.concatenate(rows + [jnp.zeros((LANES - N_EXPERTS, tl), F32)], axis=0)
    comb_ref[...] = stack.T


def _route(lg, *, tl):
    t = lg.shape[1]
    return pl.pallas_call(
        _route_kernel,
        grid=(t // tl,),
        in_specs=[pl.BlockSpec((32, tl), lambda i: (0, i))],
        out_specs=pl.BlockSpec((tl, LANES), lambda i: (i, 0)),
        out_shape=jax.ShapeDtypeStruct((t, LANES), F32),
        compiler_params=_cparams("parallel"),
        name="route",
    )(lg)


def _moe_kernel(h_ref, comb_ref, wg_ref, wu_ref, wd_ref, x1_ref, gg2_ref, o_ref, acc_ref):
    e = pl.program_id(1)
    h = h_ref[...]
    lane = lax.broadcasted_iota(jnp.int32, comb_ref.shape, 1)
    cw = jnp.sum(jnp.where(lane == e, comb_ref[...], 0.0), axis=1, keepdims=True)
    gt = _dot(h, wg_ref[0])
    a = (gt * _sigmoid(gt)) * _dot(h, wu_ref[0]) * cw
    y = _dot(a.astype(BF16), wd_ref[0])

    @pl.when(e == 0)
    def _():
        acc_ref[...] = y

    @pl.when(e > 0)
    def _():
        acc_ref[...] += y

    @pl.when(e == pl.num_programs(1) - 1)
    def _():
        o_ref[...] = x1_ref[...] + _rms(acc_ref[...]) * gg2_ref[...]


def _moe(h2, comb, wg, wu, wd, x1, gg2, *, tm):
    t = h2.shape[0]
    tok = pl.BlockSpec((tm, D_MODEL), lambda i, e: (i, 0))
    return pl.pallas_call(
        _moe_kernel,
        grid=(t // tm, N_EXPERTS),
        in_specs=[tok,
                  pl.BlockSpec((tm, LANES), lambda i, e: (i, 0)),
                  pl.BlockSpec((1, D_MODEL, D_EXPERT), lambda i, e: (e, 0, 0)),
                  pl.BlockSpec((1, D_MODEL, D_EXPERT), lambda i, e: (e, 0, 0)),
                  pl.BlockSpec((1, D_EXPERT, D_MODEL), lambda i, e: (e, 0, 0)),
                  tok,
                  pl.BlockSpec((1, D_MODEL), lambda i, e: (0, 0))],
        out_specs=tok,
        out_shape=jax.ShapeDtypeStruct((t, D_MODEL), F32),
        scratch_shapes=[pltpu.VMEM((tm, D_MODEL), F32)],
        compiler_params=_cparams("parallel", "arbitrary"),
        name="moe",
    )(h2, comb, wg, wu, wd, x1, gg2)


def _pos_tables(rows):
    quarter = D_MODEL // 4
    freq = 1.0 / (POS_BASE ** (jnp.arange(quarter, dtype=F32) / quarter))
    r = jnp.arange(rows, dtype=F32)[:, None] * freq
    cl = jnp.arange(GRID_W, dtype=F32)[:, None] * freq
    er = jnp.concatenate([jnp.sin(r), jnp.cos(r)], axis=-1)
    ec = jnp.concatenate([jnp.sin(cl), jnp.cos(cl)], axis=-1)
    return er[:, None, :], ec[None, :, :]


def _dft_tables(t):
    n = jnp.arange(CHUNK, dtype=jnp.int32)
    prod = n[:, None] * n[None, :]
    ang = (prod % CHUNK).astype(F32) * (2.0 * np.pi / CHUNK)
    c, s = jnp.cos(ang), jnp.sin(ang)
    f1 = jnp.concatenate([c, s], axis=0)
    a2 = jnp.concatenate([jnp.concatenate([c, -s], axis=1), jnp.concatenate([s, c], axis=1)], axis=0)
    cs = jnp.concatenate([c, -s], axis=0)
    angw = prod.astype(F32) * (2.0 * np.pi / t)
    return f1, a2, cs, jnp.cos(angw)[:, :, None], jnp.sin(angw)[:, :, None]


def _blockdiag(w):
    n = w.shape[0]
    eye = jnp.eye(n, dtype=w.dtype)
    return (eye[:, None, :, None] * w[:, :, None, :]).reshape(n * QKV_BLOCK, n * QKV_BLOCK)


def _gate_weights(w_f, b_f, w_b, b_b):
    w = jnp.concatenate([w_f[:, :HEADS], w_b[:, :HEADS], w_f[:, HEADS:], w_b[:, HEADS:]], axis=1).T
    b = jnp.concatenate([b_f[:HEADS], b_b[:HEADS], b_f[HEADS:], b_b[HEADS:]])
    return w.astype(BF16), b[:, None]


def kernel(x, c, ctx, c_ctx, w_ada, b_ada, g_pre_mix, g_post_mix, g_pre_ffn, g_post_ffn,
           w_in, conv_w, conv_b, w_q, w_k, w_v, w_if_fwd, b_if_fwd, w_if_bwd, b_if_bwd,
           mlstm_norm_w, mlstm_skip, w_fourier, w_out, w_router_group, b_router_group,
           w_router_expert, b_router_expert, w_gate, w_up, w_down):
    t = x.shape[1]
    rows = t // GRID_W

    c8 = jnp.concatenate([c, c_ctx[None, :], jnp.zeros((6, D_MODEL), F32)], axis=0)
    mod = _ada(c8, w_ada[0], b_ada[0][None, :])
    shift1, scale1, gate1, shift2, scale2, gate2 = [mod[0:1, k * D_MODEL:(k + 1) * D_MODEL] for k in range(N_MOD)]
    shift1c, scale1c = mod[1:2, 0:D_MODEL], mod[1:2, D_MODEL:2 * D_MODEL]
    g1 = g_pre_mix[0][None, :] * (1.0 + scale1)
    g1c = g_pre_mix[0][None, :] * (1.0 + scale1c)
    gg1 = g_post_mix[0][None, :] * gate1
    g2 = g_pre_ffn[0][None, :] * (1.0 + scale2)
    gg2 = g_post_ffn[0][None, :] * gate2

    er3, ec3 = _pos_tables(rows)
    x3 = x.reshape(rows, GRID_W, D_MODEL)
    ctx3 = ctx.reshape(CTX_LEN // GRID_W, GRID_W, D_MODEL)
    w_in_bf = w_in[0].astype(BF16)

    xm_l, z_l, u_l = _inproj(x3, er3, ec3, g1, shift1, w_in_bf, rows=8, add_pos=True)
    xm_c, _, _ = _inproj(ctx3, er3, ec3, g1c, shift1c, w_in_bf, rows=CTX_LEN // GRID_W, add_pos=False)

    wq = _blockdiag(w_q[0]).astype(BF16)
    wkt = _blockdiag(w_k[0]).T.astype(BF16)
    wv = _blockdiag(w_v[0]).astype(BF16)
    wi, bi = _gate_weights(w_if_fwd[0], b_if_fwd[0], w_if_bwd[0], b_if_bwd[0])
    wiq, wik, wiv = wi[:, :D_MLSTM], wi[:, D_MLSTM:2 * D_MLSTM], wi[:, 2 * D_MLSTM:]
    cb = conv_b[0][None, :]
    q_l, kt_l, v_l, act_l, gp_l = _feat(xm_l, conv_w[0], cb, wq, wkt, wv, wiq, wik, wiv, bi, tm=512)
    q_c, kt_c, v_c, _, gp_c = _feat(xm_c, conv_w[0], cb, wq, wkt, wv, wiq, wik, wiv, bi, tm=CTX_LEN)

    dr_l, gc_l = _gates(gp_l, tl=2048)
    dr_c, gc_c = _gates(gp_c, tl=CTX_LEN)

    c0 = jnp.zeros((2 * HEADS, DH, 2 * DH), F32)
    m0 = jnp.zeros((2 * HEADS, 8, LANES), F32)
    c_ctx_fin, m_ctx_fin = _mlstm(q_c, kt_c, v_c, gc_c, dr_c, c0, m0, cps=CTX_LEN // CHUNK, emit=False)
    hf, hb, _, _ = _mlstm(q_l, kt_l, v_l, gc_l, dr_l, c_ctx_fin, m_ctx_fin, cps=4, emit=True)

    f1, a2, cs, cw3, sw3 = _dft_tables(t)
    yc, ys = _dft1(u_l.reshape(CHUNK, (t // CHUNK) * D_FOURIER), f1.astype(BF16), tn=2048)
    yf = _dft2(yc.reshape(t, D_FOURIER), ys.reshape(t, D_FOURIER), cw3, sw3,
               a2.astype(BF16), cs.astype(BF16), w_fourier[0].astype(BF16), kb=4)
    yf = yf.reshape(t, D_FOURIER)

    wr = jnp.concatenate([w_router_group[0], w_router_expert[0],
                          jnp.zeros((D_MODEL, LANES - N_GROUPS - N_EXPERTS), F32)], axis=1)
    br = jnp.concatenate([b_router_group[0], b_router_expert[0],
                          jnp.zeros((LANES - N_GROUPS - N_EXPERTS,), F32)])[None, :]
    x1, h2, lg = _merge(hf, hb, act_l, z_l, yf, x3, er3, ec3,
                        mlstm_norm_w[0][None, :], mlstm_skip[0][None, :], w_out[0].astype(BF16),
                        gg1, g2, shift2, wr, br, rows=8)
    comb = _route(lg, tl=2048)
    out = _moe(h2, comb, w_gate[0].astype(BF16), w_up[0].astype(BF16), w_down[0].astype(BF16),
               x1, gg2, tm=1024)
    return out[None]
```

```python
import functools

import numpy as np
import jax
import jax.numpy as jnp
from jax import lax
from jax.experimental import pallas as pl
from jax.experimental.pallas import tpu as pltpu

F32 = jnp.float32
BF16 = jnp.bfloat16

D_MODEL = 1024
SEQ = 16384
GRID_W = 64
CTX_LEN = 256
D_MLSTM = 512
HEADS = 4
DH = 128
QKV_BLOCK = 4
CONV_K = 3
CHUNK = 128
D_FOURIER = 512
FGROUPS = 4
FCG = 128
N_GROUPS = 4
EPG = 4
N_EXPERTS = 16
D_EXPERT = 512
N_MOD = 6
EPS = 1e-6
POS_BASE = 10000.0
LANES = 128
NEG_BIG = -3.0e38

VMEM_LIMIT = 52 * 1024 * 1024
MOE_VMEM_LIMIT = 56 * 1024 * 1024
MOE_TB = 2048
MOE_SBK = 1024
MOE_R = 256


def _cparams(*sem):
    return pltpu.CompilerParams(dimension_semantics=sem, vmem_limit_bytes=VMEM_LIMIT)


def _dot(a, b):
    return jnp.dot(a, b, preferred_element_type=F32)


def _dot_nt(a, b):
    return lax.dot_general(a, b, (((1,), (1,)), ((), ())), preferred_element_type=F32)


def _split_bf16(a):
    hi = a.astype(BF16)
    lo = (a - hi.astype(F32)).astype(BF16)
    return hi, lo


def _dot3(a, b):
    a_hi, a_lo = _split_bf16(a)
    b_hi, b_lo = _split_bf16(b)
    return _dot(a_hi, b_hi) + (_dot(a_hi, b_lo) + _dot(a_lo, b_hi))


def _sigmoid(x):
    return 1.0 / (1.0 + jnp.exp(-x))


def _rms(x):
    return x * lax.rsqrt(jnp.mean(x * x, axis=-1, keepdims=True) + EPS)


def _ada_kernel(c_ref, w_ref, b_ref, o_ref):
    c = c_ref[...]
    s = c * _sigmoid(c)
    o_ref[...] = _dot3(s, w_ref[...]) + b_ref[...]


def _ada(c8, w, b):
    n = w.shape[1]
    tn = 768
    return pl.pallas_call(
        _ada_kernel,
        grid=(n // tn,),
        in_specs=[pl.BlockSpec((8, D_MODEL), lambda j: (0, 0)),
                  pl.BlockSpec((D_MODEL, tn), lambda j: (0, j)),
                  pl.BlockSpec((1, tn), lambda j: (0, j))],
        out_specs=pl.BlockSpec((8, tn), lambda j: (0, j)),
        out_shape=jax.ShapeDtypeStruct((8, n), F32),
        compiler_params=_cparams("parallel"),
        name="ada",
    )(c8, w, b)


def _add_pos(x3, er_ref, ec_ref):
    r = x3.shape[0]
    pr = jnp.broadcast_to(er_ref[...], (r, GRID_W, D_MODEL // 2))
    pc = jnp.broadcast_to(ec_ref[...], (r, GRID_W, D_MODEL // 2))
    return x3 + jnp.concatenate([pr, pc], axis=-1)


def _inproj_kernel(x_ref, er_ref, ec_ref, g_ref, sh_ref, w_ref, xm_ref, z_ref, u_ref, *, add_pos):
    x3 = x_ref[...]
    if add_pos:
        x3 = _add_pos(x3, er_ref, ec_ref)
    x = x3.reshape(x3.shape[0] * GRID_W, D_MODEL)
    h = _rms(x) * g_ref[...] + sh_ref[...]
    proj = _dot(h.astype(BF16), w_ref[...])
    xm_ref[...] = proj[:, :D_MLSTM].astype(BF16)
    z_ref[...] = proj[:, D_MLSTM:2 * D_MLSTM].astype(BF16)
    u_ref[...] = proj[:, 2 * D_MLSTM:].astype(BF16)


def _inproj(x3, er3, ec3, g_eff, shift, w_in, *, rows, add_pos):
    nr = x3.shape[0]
    t = nr * GRID_W
    tm = rows * GRID_W
    out = jax.ShapeDtypeStruct((t, D_MLSTM), BF16)
    ospec = pl.BlockSpec((tm, D_MLSTM), lambda i: (i, 0))
    vec = pl.BlockSpec((1, D_MODEL), lambda i: (0, 0))
    return pl.pallas_call(
        functools.partial(_inproj_kernel, add_pos=add_pos),
        grid=(nr // rows,),
        in_specs=[pl.BlockSpec((rows, GRID_W, D_MODEL), lambda i: (i, 0, 0)),
                  pl.BlockSpec((rows, 1, D_MODEL // 2), lambda i: (i, 0, 0)),
                  pl.BlockSpec((1, GRID_W, D_MODEL // 2), lambda i: (0, 0, 0)),
                  vec, vec,
                  pl.BlockSpec(w_in.shape, lambda i: (0, 0))],
        out_specs=[ospec, ospec, ospec],
        out_shape=[out, out, out],
        compiler_params=_cparams("parallel"),
        name="inproj",
    )(x3, er3, ec3, g_eff, shift, w_in)


def _feat_kernel(xm_ref, prev_ref, next_ref, cw_ref, cb_ref, wq_ref, wkt_ref, wv_ref,
                 wiq_ref, wik_ref, wiv_ref, bi_ref,
                 q_ref, kt_ref, v_ref, act_ref, g_ref):
    i = pl.program_id(0)
    n = pl.num_programs(0)
    xm_bf = xm_ref[...]
    xm = xm_bf.astype(F32)
    tm = xm.shape[0]
    prev_row = prev_ref[...].astype(F32)[15:16, :] * jnp.where(i > 0, 1.0, 0.0)
    next_row = next_ref[...].astype(F32)[0:1, :] * jnp.where(i < n - 1, 1.0, 0.0)
    rid = lax.broadcasted_iota(jnp.int32, (tm, 1), 0)
    x_left = jnp.where(rid == 0, prev_row, pltpu.roll(xm, 1, 0))
    x_right = jnp.where(rid == tm - 1, next_row, pltpu.roll(xm, tm - 1, 0))
    cw = cw_ref[...]
    y = cw[0:1] * x_left + cw[1:2] * xm + cw[2:3] * x_right + cb_ref[...]
    act = (y * _sigmoid(y)).astype(BF16)
    act_ref[...] = act
    q = _dot(act, wq_ref[...])
    kt = _dot_nt(wkt_ref[...], act)
    v = _dot(xm_bf, wv_ref[...])
    q_bf = q.astype(BF16)
    kt_bf = kt.astype(BF16)
    v_bf = v.astype(BF16)
    q_ref[...] = (q * (DH ** -0.5)).astype(BF16)
    kt_ref[...] = kt_bf
    v_ref[...] = v_bf
    g = _dot_nt(wiq_ref[...], q_bf) + _dot(wik_ref[...], kt_bf) + _dot_nt(wiv_ref[...], v_bf)
    g_ref[...] = g + bi_ref[...]


def _feat(xm, conv_w, conv_b, wq, wkt, wv, wiq, wik, wiv, bi, *, tm):
    t = xm.shape[0]
    nb16 = t // 16
    k16 = tm // 16
    full = lambda a: pl.BlockSpec(a.shape, lambda i: (0,) * a.ndim)
    tok = pl.BlockSpec((tm, D_MLSTM), lambda i: (i, 0))
    return pl.pallas_call(
        _feat_kernel,
        grid=(t // tm,),
        in_specs=[tok,
                  pl.BlockSpec((16, D_MLSTM), lambda i: (jnp.maximum(i * k16 - 1, 0), 0)),
                  pl.BlockSpec((16, D_MLSTM), lambda i: (jnp.minimum((i + 1) * k16, nb16 - 1), 0)),
                  full(conv_w), full(conv_b), full(wq), full(wkt), full(wv),
                  full(wiq), full(wik), full(wiv), full(bi)],
        out_specs=[tok,
                   pl.BlockSpec((D_MLSTM, tm), lambda i: (0, i)),
                   tok, tok,
                   pl.BlockSpec((16, tm), lambda i: (0, i))],
        out_shape=[jax.ShapeDtypeStruct((t, D_MLSTM), BF16),
                   jax.ShapeDtypeStruct((D_MLSTM, t), BF16),
                   jax.ShapeDtypeStruct((t, D_MLSTM), BF16),
                   jax.ShapeDtypeStruct((t, D_MLSTM), BF16),
                   jax.ShapeDtypeStruct((16, t), F32)],
        compiler_params=_cparams("parallel"),
        name="feat",
    )(xm, xm, xm, conv_w, conv_b, wq, wkt, wv, wiq, wik, wiv, bi)


def _gates_kernel(g_ref, d_ref, gc_ref):
    g = g_ref[...]
    tl = g.shape[1]
    ig = g[0:8]
    fg = g[8:16]
    lf = jnp.minimum(fg, 0.0) - jnp.log(1.0 + jnp.exp(-jnp.abs(fg)))
    pos = lax.broadcasted_iota(jnp.int32, (8, tl), 1) & (CHUNK - 1)
    is_fwd = lax.broadcasted_iota(jnp.int32, (8, tl), 0) < HEADS

    def scan(x, op, ident):
        xf = x
        xb = x
        k = 1
        while k < CHUNK:
            xf = op(xf, jnp.where(pos >= k, pltpu.roll(xf, k, 1), ident))
            xb = op(xb, jnp.where(pos < CHUNK - k, pltpu.roll(xb, tl - k, 1), ident))
            k *= 2
        return jnp.where(is_fwd, xf, xb)

    b = scan(lf, jnp.add, 0.0)
    d = ig - b
    mloc = b + scan(d, jnp.maximum, NEG_BIG)
    d_ref[...] = d
    stack = jnp.concatenate([b, mloc, jnp.zeros((LANES - 16, tl), F32)], axis=0)
    gc_ref[...] = stack.T


def _gates(g, *, tl):
    t = g.shape[1]
    return pl.pallas_call(
        _gates_kernel,
        grid=(t // tl,),
        in_specs=[pl.BlockSpec((16, tl), lambda i: (0, i))],
        out_specs=[pl.BlockSpec((8, tl), lambda i: (0, i)),
                   pl.BlockSpec((tl, LANES), lambda i: (i, 0))],
        out_shape=[jax.ShapeDtypeStruct((8, t), F32),
                   jax.ShapeDtypeStruct((t, LANES), F32)],
        compiler_params=_cparams("parallel"),
        name="gates",
    )(g)


def _mlstm_kernel(*refs, cps, emit):
    (qf_ref, kf_ref, vf_ref, gcf_ref, drf_ref,
     qb_ref, kb_ref, vb_ref, gcb_ref, drb_ref, c0_ref, m0_ref) = refs[:12]
    if emit:
        hf_ref, hb_ref, cfin_ref, mfin_ref, c_scr, m_scr = refs[12:]
    else:
        cfin_ref, mfin_ref, c_scr, m_scr = refs[12:]
        hf_ref = hb_ref = None
    step = pl.program_id(0)

    @pl.when(step == 0)
    def _():
        c_scr[...] = c0_ref[...]
        m_scr[...] = m0_ref[...]

    ti = lax.broadcasted_iota(jnp.int32, (CHUNK, CHUNK), 0)
    si = lax.broadcasted_iota(jnp.int32, (CHUNK, CHUNK), 1)
    ones_col = jnp.where(si == 0, 1.0, 0.0).astype(BF16)

    for j in range(cps):
        for d in range(2):
            if d == 0:
                q_ref, k_ref, v_ref, gc_ref, dr_ref, h_ref = qf_ref, kf_ref, vf_ref, gcf_ref, drf_ref, hf_ref
                r0, mask, last = j * CHUNK, si <= ti, CHUNK - 1
            else:
                q_ref, k_ref, v_ref, gc_ref, dr_ref, h_ref = qb_ref, kb_ref, vb_ref, gcb_ref, drb_ref, hb_ref
                r0, mask, last = (cps - 1 - j) * CHUNK, si >= ti, 0
            rows = slice(r0, r0 + CHUNK)
            gc = gc_ref[rows, :]
            dr = dr_ref[:, rows]
            for hd in range(HEADS):
                r = d * HEADS + hd
                cols = slice(hd * DH, (hd + 1) * DH)
                b = gc[:, r:r + 1]
                mloc = gc[:, 8 + r:9 + r]
                drow = dr[r:r + 1, :]
                m0 = m_scr[r][0:1, 0:1]
                c_aug = c_scr[r]
                kt = k_ref[cols, rows]
                vaug = jnp.concatenate([v_ref[rows, cols], ones_col], axis=1)
                if emit:
                    qh = q_ref[rows, cols]
                    bm = b + m0
                    m_t = jnp.maximum(bm, mloc)
                    dmat = jnp.where(mask, jnp.exp((b - m_t) + drow), 0.0)
                    smat = (_dot(qh, kt) * dmat).astype(BF16)
                    num = _dot(smat, vaug) + jnp.exp(bm - m_t) * _dot(qh, c_aug.astype(BF16))
                    den = jnp.maximum(jnp.abs(num[:, DH:DH + 1]), jnp.exp(-m_t))
                    h_ref[rows, cols] = (num[:, :DH] / den).astype(h_ref.dtype)
                btot = gc[last:last + 1, r:r + 1]
                amax = gc[last:last + 1, 8 + r:9 + r]
                m_new = jnp.maximum(btot + m0, amax)
                decay = jnp.exp(btot + m0 - m_new)
                kw = (kt.astype(F32) * jnp.exp(btot + drow - m_new)).astype(BF16)
                c_scr[r] = decay * c_aug + _dot(kw, vaug)
                m_scr[r] = jnp.broadcast_to(m_new, (8, LANES))

    @pl.when(step == pl.num_programs(0) - 1)
    def _():
        cfin_ref[...] = c_scr[...]
        mfin_ref[...] = m_scr[...]


def _mlstm(q, kt, v, gc, dr, c0, m0, *, cps, emit):
    t = q.shape[0]
    cb = cps * CHUNK
    nb = t // cb
    fwd_r = lambda i: (i, 0)
    bwd_r = lambda i: (nb - 1 - i, 0)
    fwd_c = lambda i: (0, i)
    bwd_c = lambda i: (0, nb - 1 - i)
    tok = lambda f: pl.BlockSpec((cb, D_MLSTM), f)
    in_specs = []
    for fr, fc in ((fwd_r, fwd_c), (bwd_r, bwd_c)):
        in_specs += [tok(fr), pl.BlockSpec((D_MLSTM, cb), fc), tok(fr),
                     pl.BlockSpec((cb, LANES), fr), pl.BlockSpec((8, cb), fc)]
    cshape = (2 * HEADS, DH, 2 * DH)
    mshape = (2 * HEADS, 8, LANES)
    cspec = pl.BlockSpec(cshape, lambda i: (0, 0, 0))
    mspec = pl.BlockSpec(mshape, lambda i: (0, 0, 0))
    in_specs += [cspec, mspec]
    out_specs = [cspec, mspec]
    out_shape = [jax.ShapeDtypeStruct(cshape, F32), jax.ShapeDtypeStruct(mshape, F32)]
    if emit:
        out_specs = [tok(fwd_r), tok(bwd_r)] + out_specs
        out_shape = [jax.ShapeDtypeStruct((t, D_MLSTM), BF16)] * 2 + out_shape
    return pl.pallas_call(
        functools.partial(_mlstm_kernel, cps=cps, emit=emit),
        grid=(nb,),
        in_specs=in_specs,
        out_specs=out_specs,
        out_shape=out_shape,
        scratch_shapes=[pltpu.VMEM(cshape, F32), pltpu.VMEM(mshape, F32)],
        compiler_params=_cparams("arbitrary"),
        name="mlstm",
    )(q, kt, v, gc, dr, q, kt, v, gc, dr, c0, m0)


def _dft1_kernel(u_ref, f_ref, yc_ref, ys_ref):
    y = _dot(f_ref[...], u_ref[...])
    yc_ref[...] = y[:CHUNK].astype(BF16)
    ys_ref[...] = y[CHUNK:].astype(BF16)


def _dft1(u2d, f1, *, tn):
    n = u2d.shape[1]
    blk = pl.BlockSpec((CHUNK, tn), lambda j: (0, j))
    out = jax.ShapeDtypeStruct((CHUNK, n), BF16)
    return pl.pallas_call(
        _dft1_kernel,
        grid=(n // tn,),
        in_specs=[blk, pl.BlockSpec(f1.shape, lambda j: (0, 0))],
        out_specs=[blk, blk],
        out_shape=[out, out],
        compiler_params=_cparams("parallel"),
        name="dft1",
    )(u2d, f1)


def _dft2_kernel(yc_ref, ys_ref, cw_ref, sw_ref, a2_ref, cs_ref, wf_ref, o_ref, *, kb, scale):
    a2 = a2_ref[...]
    cs = cs_ref[...]
    for kk in range(kb):
        rows = slice(kk * CHUNK, (kk + 1) * CHUNK)
        yc = yc_ref[rows, :].astype(F32)
        ys = ys_ref[rows, :].astype(F32)
        cw = cw_ref[kk]
        sw = sw_ref[kk]
        p = jnp.concatenate([yc * cw - ys * sw, yc * sw + ys * cw], axis=0).astype(BF16)
        x = _dot(a2, p)
        for g in range(FGROUPS):
            cols = slice(g * FCG, (g + 1) * FCG)
            cat = jnp.concatenate([x[:CHUNK, cols], x[CHUNK:, cols]], axis=1).astype(BF16)
            f = _dot(cat, cs) * scale
            o_ref[:, kk * D_FOURIER + g * FCG:kk * D_FOURIER + (g + 1) * FCG] = (
                _dot(f.astype(BF16), wf_ref[g]).astype(o_ref.dtype))


def _dft2(yc, ys, cw3, sw3, a2, cs, wf, *, kb):
    t = yc.shape[0]
    n1 = t // CHUNK
    scale = float(1.0 / np.sqrt(float(t) * FCG))
    tok = pl.BlockSpec((kb * CHUNK, D_FOURIER), lambda i: (i, 0))
    tw = pl.BlockSpec((kb, CHUNK, 1), lambda i: (i, 0, 0))
    full = lambda a: pl.BlockSpec(a.shape, lambda i: (0,) * a.ndim)
    return pl.pallas_call(
        functools.partial(_dft2_kernel, kb=kb, scale=scale),
        grid=(n1 // kb,),
        in_specs=[tok, tok, tw, tw, full(a2), full(cs), full(wf)],
        out_specs=pl.BlockSpec((CHUNK, kb * D_FOURIER), lambda i: (0, i)),
        out_shape=jax.ShapeDtypeStruct((CHUNK, n1 * D_FOURIER), BF16),
        compiler_params=_cparams("parallel"),
        name="dft2",
    )(yc, ys, cw3, sw3, a2, cs, wf)


def _merge_kernel(hf_ref, hb_ref, act_ref, z_ref, yf_ref, x_ref, er_ref, ec_ref,
                  nw_ref, sk_ref, wout_ref, gg1_ref, g2_ref, sh2_ref, wr_ref, br_ref,
                  x1_ref, h2_ref, lg_ref):
    h = hf_ref[...].astype(F32) + hb_ref[...].astype(F32)
    parts = []
    for hd in range(HEADS):
        hh = h[:, hd * DH:(hd + 1) * DH]
        dl = hh - jnp.mean(hh, axis=-1, keepdims=True)
        var = jnp.mean(dl * dl, axis=-1, keepdims=True)
        parts.append(dl * lax.rsqrt(var + EPS))
    hn = jnp.concatenate(parts, axis=-1)
    z = z_ref[...].astype(F32)
    m = (hn * nw_ref[...] + sk_ref[...] * act_ref[...].astype(F32)) * (z * _sigmoid(z))
    cat = jnp.concatenate([m.astype(BF16), yf_ref[...]], axis=-1)
    y = _dot(cat, wout_ref[...])
    x3 = _add_pos(x_ref[...], er_ref, ec_ref)
    xp = x3.reshape(x3.shape[0] * GRID_W, D_MODEL)
    x1 = xp + _rms(y) * gg1_ref[...]
    x1_ref[...] = x1
    h2 = _rms(x1) * g2_ref[...] + sh2_ref[...]
    h2_ref[...] = h2.astype(BF16)
    lg = _dot3(h2, wr_ref[...]) + br_ref[...]
    lg_ref[...] = lg.T[:32]


def _merge(hf, hb, act, z, yf, x3, er3, ec3, nw, sk, wout, gg1, g2, sh2, wr, br, *, rows):
    nr = x3.shape[0]
    t = nr * GRID_W
    tm = rows * GRID_W
    tok = pl.BlockSpec((tm, D_MLSTM), lambda i: (i, 0))
    full = lambda a: pl.BlockSpec(a.shape, lambda i: (0,) * a.ndim)
    return pl.pallas_call(
        _merge_kernel,
        grid=(nr // rows,),
        in_specs=[tok, tok, tok, tok, tok,
                  pl.BlockSpec((rows, GRID_W, D_MODEL), lambda i: (i, 0, 0)),
                  pl.BlockSpec((rows, 1, D_MODEL // 2), lambda i: (i, 0, 0)),
                  pl.BlockSpec((1, GRID_W, D_MODEL // 2), lambda i: (0, 0, 0)),
                  full(nw), full(sk), full(wout), full(gg1), full(g2), full(sh2), full(wr), full(br)],
        out_specs=[pl.BlockSpec((tm, D_MODEL), lambda i: (i, 0)),
                   pl.BlockSpec((tm, D_MODEL), lambda i: (i, 0)),
                   pl.BlockSpec((32, tm), lambda i: (0, i))],
        out_shape=[jax.ShapeDtypeStruct((t, D_MODEL), F32),
                   jax.ShapeDtypeStruct((t, D_MODEL), BF16),
                   jax.ShapeDtypeStruct((32, t), F32)],
        compiler_params=_cparams("parallel"),
        name="merge",
    )(hf, hb, act, z, yf, x3, er3, ec3, nw, sk, wout, gg1, g2, sh2, wr, br)


def _route_kernel(lg_ref, pos_ref, w_ref, cnt_ref, *, sbk):
    lg = lg_ref[...]
    tl = lg.shape[1]
    g = [lg[j:j + 1] for j in range(N_GROUPS)]
    e = [lg[N_GROUPS + j:N_GROUPS + j + 1] for j in range(N_EXPERTS)]
    gmax = jnp.maximum(jnp.maximum(g[0], g[1]), jnp.maximum(g[2], g[3]))
    den = jnp.exp(g[0] - gmax) + jnp.exp(g[1] - gmax) + jnp.exp(g[2] - gmax) + jnp.exp(g[3] - gmax)
    p_sel = 1.0 / den
    sel = []
    free = jnp.ones((1, tl), F32)
    for j in range(N_GROUPS):
        s = jnp.where(g[j] >= gmax, free, 0.0)
        sel.append(s)
        free = free - s
    es = []
    for j in range(EPG):
        es.append(sel[0] * e[j] + sel[1] * e[EPG + j] + sel[2] * e[2 * EPG + j] + sel[3] * e[3 * EPG + j])
    rank = []
    for j in range(EPG):
        rj = jnp.zeros((1, tl), F32)
        for i in range(EPG):
            if i == j:
                continue
            beats = (es[i] >= es[j]) if i < j else (es[i] > es[j])
            rj = rj + jnp.where(beats, 1.0, 0.0)
        rank.append(rj)
    v1 = jnp.maximum(jnp.maximum(es[0], es[1]), jnp.maximum(es[2], es[3]))
    v2 = sum(jnp.where(rank[j] == 1.0, es[j], 0.0) for j in range(EPG))
    tt = jnp.exp(v2 - v1)
    w1 = p_sel / (1.0 + tt)
    w2 = w1 * tt
    w = [jnp.where(rank[j] == 0.0, w1, jnp.where(rank[j] == 1.0, w2, 0.0)) for j in range(EPG)]
    top2 = [jnp.where(rank[j] < 2.0, 1.0, 0.0) for j in range(EPG)]
    mem = jnp.concatenate([sel[gi] * top2[j] for gi in range(N_GROUPS) for j in range(EPG)], axis=0)
    w_ref[...] = jnp.concatenate([sel[gi] * w[j] for gi in range(N_GROUPS) for j in range(EPG)], axis=0)
    lane = lax.broadcasted_iota(jnp.int32, (N_EXPERTS, tl), 1) & (sbk - 1)
    c = mem
    k = 1
    while k < sbk:
        c = c + jnp.where(lane >= k, pltpu.roll(c, k, 1), 0.0)
        k *= 2
    pos_ref[...] = jnp.where(mem > 0.0, c - 1.0, -1.0)
    lane128 = lax.broadcasted_iota(jnp.int32, (N_EXPERTS, LANES), 1)
    cnt = jnp.zeros((N_EXPERTS, LANES), F32)
    for kb in range(tl // sbk):
        tot = jnp.sum(mem[:, kb * sbk:(kb + 1) * sbk], axis=1, keepdims=True)
        cnt = cnt + jnp.where(lane128 == kb, tot, 0.0)
    cnt_ref[...] = cnt


def _route(lg, *, tl, sbk):
    t = lg.shape[1]
    row = pl.BlockSpec((N_EXPERTS, tl), lambda i: (0, i))
    return pl.pallas_call(
        functools.partial(_route_kernel, sbk=sbk),
        grid=(t // tl,),
        in_specs=[pl.BlockSpec((32, tl), lambda i: (0, i))],
        out_specs=[row, row, pl.BlockSpec((N_EXPERTS, LANES), lambda i: (i, 0))],
        out_shape=[jax.ShapeDtypeStruct((N_EXPERTS, t), F32),
                   jax.ShapeDtypeStruct((N_EXPERTS, t), F32),
                   jax.ShapeDtypeStruct((t // tl * N_EXPERTS, LANES), F32)],
        compiler_params=_cparams("parallel"),
        name="route",
    )(lg)


def _moe_kernel(cnt_ref, h_ref, pos_ref, w_ref, wg_ref, wu_ref, wd_ref, x1_ref, gg2_ref, o_ref, *, sbk, r):
    i = pl.program_id(0)
    e = pl.program_id(1)
    nsb = h_ref.shape[0] // sbk

    @pl.when(e == 0)
    def _():
        o_ref[...] = jnp.zeros(o_ref.shape, F32)

    wg, wu, wd = wg_ref[0], wu_ref[0], wd_ref[0]
    pos_all = pos_ref[pl.ds(e, 1), :]
    w_all = w_ref[pl.ds(e, 1), :]
    rid0 = lax.broadcasted_iota(jnp.int32, (r, sbk), 0).astype(F32)

    def subtile(sb, base):
        cols = slice(sb * sbk, (sb + 1) * sbk)
        hit = pos_all[:, cols] == (rid0 + base)
        s = jnp.where(hit, 1.0, 0.0).astype(BF16)
        xs = _dot(s, h_ref[cols, :]).astype(BF16)
        gt = _dot(xs, wg)
        a = ((gt * _sigmoid(gt)) * _dot(xs, wu)).astype(BF16)
        y = _dot(a, wd).astype(BF16)
        wct = jnp.where(hit, w_all[:, cols], 0.0).T.astype(BF16)
        o_ref[cols, :] += _dot(wct, y)

    for sb in range(nsb):
        subtile(sb, 0.0)

    nmax = 0
    for sb in range(nsb):
        nmax = jnp.maximum(nmax, (cnt_ref[(i * nsb + sb) * N_EXPERTS + e] + (r - 1)) // r)

    def overflow(s, carry):
        for sb in range(nsb):
            subtile(sb, (s * r).astype(F32))
        return carry

    lax.fori_loop(1, nmax, overflow, 0)

    @pl.when(e == pl.num_programs(1) - 1)
    def _():
        o_ref[...] = x1_ref[...] + _rms(o_ref[...]) * gg2_ref[...]


def _moe(cnt, h2, pos, w, wg, wu, wd, x1, gg2, *, tb, sbk, r):
    t = h2.shape[0]
    tok = pl.BlockSpec((tb, D_MODEL), lambda i, e, c: (i, 0))
    row = pl.BlockSpec((N_EXPERTS, tb), lambda i, e, c: (0, i))
    return pl.pallas_call(
        functools.partial(_moe_kernel, sbk=sbk, r=r),
        grid_spec=pltpu.PrefetchScalarGridSpec(
            num_scalar_prefetch=1,
            grid=(t // tb, N_EXPERTS),
            in_specs=[tok, row, row,
                      pl.BlockSpec((1, D_MODEL, D_EXPERT), lambda i, e, c: (e, 0, 0)),
                      pl.BlockSpec((1, D_MODEL, D_EXPERT), lambda i, e, c: (e, 0, 0)),
                      pl.BlockSpec((1, D_EXPERT, D_MODEL), lambda i, e, c: (e, 0, 0)),
                      tok,
                      pl.BlockSpec((1, D_MODEL), lambda i, e, c: (0, 0))],
            out_specs=tok),
        out_shape=jax.ShapeDtypeStruct((t, D_MODEL), F32),
        compiler_params=pltpu.CompilerParams(dimension_semantics=("parallel", "arbitrary"),
                                             vmem_limit_bytes=MOE_VMEM_LIMIT),
        name="moe",
    )(cnt, h2, pos, w, wg, wu, wd, x1, gg2)


def _pos_tables(rows):
    quarter = D_MODEL // 4
    freq = 1.0 / (POS_BASE ** (np.arange(quarter, dtype=np.float64) / quarter))
    r = np.arange(rows, dtype=np.float64)[:, None] * freq
    cl = np.arange(GRID_W, dtype=np.float64)[:, None] * freq
    er = np.concatenate([np.sin(r), np.cos(r)], axis=-1).astype(np.float32)
    ec = np.concatenate([np.sin(cl), np.cos(cl)], axis=-1).astype(np.float32)
    return jnp.asarray(er[:, None, :]), jnp.asarray(ec[None, :, :])


def _dft_tables(t):
    n = np.arange(CHUNK, dtype=np.int64)
    prod = n[:, None] * n[None, :]
    ang = (prod % CHUNK).astype(np.float64) * (2.0 * np.pi / CHUNK)
    c, s = np.cos(ang), np.sin(ang)
    f1 = np.concatenate([c, s], axis=0)
    a2 = np.concatenate([np.concatenate([c, -s], axis=1), np.concatenate([s, c], axis=1)], axis=0)
    cs = np.concatenate([c, -s], axis=0)
    angw = prod.astype(np.float64) * (2.0 * np.pi / t)
    f32 = lambda a: jnp.asarray(a.astype(np.float32))
    return f32(f1), f32(a2), f32(cs), f32(np.cos(angw)[:, :, None]), f32(np.sin(angw)[:, :, None])


def _blockdiag(w):
    n = w.shape[0]
    size = n * QKV_BLOCK
    spread = np.tile(np.eye(QKV_BLOCK, dtype=np.float32), (1, n))
    rows = jnp.dot(w.reshape(size, QKV_BLOCK), jnp.asarray(spread), precision=lax.Precision.HIGHEST)
    blk = np.arange(size) // QKV_BLOCK
    mask = (blk[:, None] == blk[None, :]).astype(np.float32)
    return rows * jnp.asarray(mask)


def _gate_weights(w_f, b_f, w_b, b_b):
    w = jnp.concatenate([w_f[:, :HEADS], w_b[:, :HEADS], w_f[:, HEADS:], w_b[:, HEADS:]], axis=1).T
    b = jnp.concatenate([b_f[:HEADS], b_b[:HEADS], b_f[HEADS:], b_b[HEADS:]])
    return w.astype(BF16), b[:, None]


def kernel(x, c, ctx, c_ctx, w_ada, b_ada, g_pre_mix, g_post_mix, g_pre_ffn, g_post_ffn,
           w_in, conv_w, conv_b, w_q, w_k, w_v, w_if_fwd, b_if_fwd, w_if_bwd, b_if_bwd,
           mlstm_norm_w, mlstm_skip, w_fourier, w_out, w_router_group, b_router_group,
           w_router_expert, b_router_expert, w_gate, w_up, w_down):
    t = x.shape[1]
    rows = t // GRID_W

    c8 = jnp.concatenate([c, c_ctx[None, :], jnp.zeros((6, D_MODEL), F32)], axis=0)
    mod = _ada(c8, w_ada[0], b_ada[0][None, :])
    shift1, scale1, gate1, shift2, scale2, gate2 = [mod[0:1, k * D_MODEL:(k + 1) * D_MODEL] for k in range(N_MOD)]
    shift1c, scale1c = mod[1:2, 0:D_MODEL], mod[1:2, D_MODEL:2 * D_MODEL]
    g1 = g_pre_mix[0][None, :] * (1.0 + scale1)
    g1c = g_pre_mix[0][None, :] * (1.0 + scale1c)
    gg1 = g_post_mix[0][None, :] * gate1
    g2 = g_pre_ffn[0][None, :] * (1.0 + scale2)
    gg2 = g_post_ffn[0][None, :] * gate2

    er3, ec3 = _pos_tables(rows)
    x3 = x.reshape(rows, GRID_W, D_MODEL)
    ctx3 = ctx.reshape(CTX_LEN // GRID_W, GRID_W, D_MODEL)
    w_in_bf = w_in[0].astype(BF16)

    xm_l, z_l, u_l = _inproj(x3, er3, ec3, g1, shift1, w_in_bf, rows=8, add_pos=True)
    xm_c, _, _ = _inproj(ctx3, er3, ec3, g1c, shift1c, w_in_bf, rows=CTX_LEN // GRID_W, add_pos=False)

    wq = _blockdiag(w_q[0]).astype(BF16)
    wkt = _blockdiag(w_k[0]).T.astype(BF16)
    wv = _blockdiag(w_v[0]).astype(BF16)
    wi, bi = _gate_weights(w_if_fwd[0], b_if_fwd[0], w_if_bwd[0], b_if_bwd[0])
    wiq, wik, wiv = wi[:, :D_MLSTM], wi[:, D_MLSTM:2 * D_MLSTM], wi[:, 2 * D_MLSTM:]
    cb = conv_b[0][None, :]
    q_l, kt_l, v_l, act_l, gp_l = _feat(xm_l, conv_w[0], cb, wq, wkt, wv, wiq, wik, wiv, bi, tm=512)
    q_c, kt_c, v_c, _, gp_c = _feat(xm_c, conv_w[0], cb, wq, wkt, wv, wiq, wik, wiv, bi, tm=CTX_LEN)

    dr_l, gc_l = _gates(gp_l, tl=2048)
    dr_c, gc_c = _gates(gp_c, tl=CTX_LEN)

    c0 = jnp.zeros((2 * HEADS, DH, 2 * DH), F32)
    m0 = jnp.zeros((2 * HEADS, 8, LANES), F32)
    c_ctx_fin, m_ctx_fin = _mlstm(q_c, kt_c, v_c, gc_c, dr_c, c0, m0, cps=CTX_LEN // CHUNK, emit=False)
    hf, hb, _, _ = _mlstm(q_l, kt_l, v_l, gc_l, dr_l, c_ctx_fin, m_ctx_fin, cps=4, emit=True)

    f1, a2, cs, cw3, sw3 = _dft_tables(t)
    yc, ys = _dft1(u_l.reshape(CHUNK, (t // CHUNK) * D_FOURIER), f1.astype(BF16), tn=2048)
    yf = _dft2(yc.reshape(t, D_FOURIER), ys.reshape(t, D_FOURIER), cw3, sw3,
               a2.astype(BF16), cs.astype(BF16), w_fourier[0].astype(BF16), kb=4)
    yf = yf.reshape(t, D_FOURIER)

    wr = jnp.concatenate([w_router_group[0], w_router_expert[0],
                          jnp.zeros((D_MODEL, LANES - N_GROUPS - N_EXPERTS), F32)], axis=1)
    br = jnp.concatenate([b_router_group[0], b_router_expert[0],
                          jnp.zeros((LANES - N_GROUPS - N_EXPERTS,), F32)])[None, :]
    x1, h2, lg = _merge(hf, hb, act_l, z_l, yf, x3, er3, ec3,
                        mlstm_norm_w[0][None, :], mlstm_skip[0][None, :], w_out[0].astype(BF16),
                        gg1, g2, shift2, wr, br, rows=8)
    pos, wts, cnt = _route(lg, tl=MOE_TB, sbk=MOE_SBK)
    nsb = MOE_TB // MOE_SBK
    cnt = cnt.reshape(t // MOE_TB, N_EXPERTS, LANES)[:, :, :nsb]
    cnt = jnp.transpose(cnt, (0, 2, 1)).reshape(-1).astype(jnp.int32)
    out = _moe(cnt, h2, pos, wts, w_gate[0].astype(BF16), w_up[0].astype(BF16), w_down[0].astype(BF16),
               x1, gg2, tb=MOE_TB, sbk=MOE_SBK, r=MOE_R)
    return out[None]
```

```python
import functools

import numpy as np
import jax
import jax.numpy as jnp
from jax import lax
from jax.experimental import pallas as pl
from jax.experimental.pallas import tpu as pltpu

F32 = jnp.float32
BF16 = jnp.bfloat16

D_MODEL = 1024
SEQ = 16384
GRID_W = 64
CTX_LEN = 256
D_MLSTM = 512
HEADS = 4
DH = 128
QKV_BLOCK = 4
CONV_K = 3
CHUNK = 128
D_FOURIER = 512
FGROUPS = 4
FCG = 128
N_GROUPS = 4
EPG = 4
N_EXPERTS = 16
D_EXPERT = 512
N_MOD = 6
EPS = 1e-6
POS_BASE = 10000.0
LANES = 128
NEG_BIG = -3.0e38

VMEM_LIMIT = 52 * 1024 * 1024
MOE_VMEM_LIMIT = 56 * 1024 * 1024
MOE_TB = 2048
MOE_SBK = 256
MOE_R = 48
MOE_EPS = 2
MOE_CSB = 1


def _cparams(*sem):
    return pltpu.CompilerParams(dimension_semantics=sem, vmem_limit_bytes=VMEM_LIMIT)


def _dot(a, b):
    return jnp.dot(a, b, preferred_element_type=F32)


def _dot_nt(a, b):
    return lax.dot_general(a, b, (((1,), (1,)), ((), ())), preferred_element_type=F32)


def _split_bf16(a):
    hi = a.astype(BF16)
    lo = (a - hi.astype(F32)).astype(BF16)
    return hi, lo


def _dot3(a, b):
    a_hi, a_lo = _split_bf16(a)
    b_hi, b_lo = _split_bf16(b)
    return _dot(a_hi, b_hi) + (_dot(a_hi, b_lo) + _dot(a_lo, b_hi))


def _sigmoid(x):
    return 1.0 / (1.0 + jnp.exp(-x))


def _rms(x):
    return x * lax.rsqrt(jnp.mean(x * x, axis=-1, keepdims=True) + EPS)


def _ada_kernel(c_ref, w_ref, b_ref, o_ref):
    c = c_ref[...]
    s = c * _sigmoid(c)
    o_ref[...] = _dot3(s, w_ref[...]) + b_ref[...]


def _ada(c8, w, b):
    n = w.shape[1]
    tn = 768
    return pl.pallas_call(
        _ada_kernel,
        grid=(n // tn,),
        in_specs=[pl.BlockSpec((8, D_MODEL), lambda j: (0, 0)),
                  pl.BlockSpec((D_MODEL, tn), lambda j: (0, j)),
                  pl.BlockSpec((1, tn), lambda j: (0, j))],
        out_specs=pl.BlockSpec((8, tn), lambda j: (0, j)),
        out_shape=jax.ShapeDtypeStruct((8, n), F32),
        compiler_params=_cparams("parallel"),
        name="ada",
    )(c8, w, b)


def _add_pos(x3, er_ref, ec_ref):
    r = x3.shape[0]
    pr = jnp.broadcast_to(er_ref[...], (r, GRID_W, D_MODEL // 2))
    pc = jnp.broadcast_to(ec_ref[...], (r, GRID_W, D_MODEL // 2))
    return x3 + jnp.concatenate([pr, pc], axis=-1)


def _inproj_kernel(x_ref, er_ref, ec_ref, g_ref, sh_ref, w_ref, xm_ref, z_ref, u_ref, *, add_pos):
    x3 = x_ref[...]
    if add_pos:
        x3 = _add_pos(x3, er_ref, ec_ref)
    x = x3.reshape(x3.shape[0] * GRID_W, D_MODEL)
    h = _rms(x) * g_ref[...] + sh_ref[...]
    proj = _dot(h.astype(BF16), w_ref[...])
    xm_ref[...] = proj[:, :D_MLSTM].astype(BF16)
    z_ref[...] = proj[:, D_MLSTM:2 * D_MLSTM].astype(BF16)
    u_ref[...] = proj[:, 2 * D_MLSTM:].astype(BF16)


def _inproj(x3, er3, ec3, g_eff, shift, w_in, *, rows, add_pos):
    nr = x3.shape[0]
    t = nr * GRID_W
    tm = rows * GRID_W
    out = jax.ShapeDtypeStruct((t, D_MLSTM), BF16)
    ospec = pl.BlockSpec((tm, D_MLSTM), lambda i: (i, 0))
    vec = pl.BlockSpec((1, D_MODEL), lambda i: (0, 0))
    return pl.pallas_call(
        functools.partial(_inproj_kernel, add_pos=add_pos),
        grid=(nr // rows,),
        in_specs=[pl.BlockSpec((rows, GRID_W, D_MODEL), lambda i: (i, 0, 0)),
                  pl.BlockSpec((rows, 1, D_MODEL // 2), lambda i: (i, 0, 0)),
                  pl.BlockSpec((1, GRID_W, D_MODEL // 2), lambda i: (0, 0, 0)),
                  vec, vec,
                  pl.BlockSpec(w_in.shape, lambda i: (0, 0))],
        out_specs=[ospec, ospec, ospec],
        out_shape=[out, out, out],
        compiler_params=_cparams("parallel"),
        name="inproj",
    )(x3, er3, ec3, g_eff, shift, w_in)


def _feat_kernel(xm_ref, prev_ref, next_ref, cw_ref, cb_ref, wq_ref, wkt_ref, wv_ref,
                 wiq_ref, wik_ref, wiv_ref, bi_ref,
                 q_ref, kt_ref, v_ref, act_ref, g_ref):
    i = pl.program_id(0)
    n = pl.num_programs(0)
    xm_bf = xm_ref[...]
    xm = xm_bf.astype(F32)
    tm = xm.shape[0]
    prev_row = prev_ref[...].astype(F32)[15:16, :] * jnp.where(i > 0, 1.0, 0.0)
    next_row = next_ref[...].astype(F32)[0:1, :] * jnp.where(i < n - 1, 1.0, 0.0)
    rid = lax.broadcasted_iota(jnp.int32, (tm, 1), 0)
    x_left = jnp.where(rid == 0, prev_row, pltpu.roll(xm, 1, 0))
    x_right = jnp.where(rid == tm - 1, next_row, pltpu.roll(xm, tm - 1, 0))
    cw = cw_ref[...]
    y = cw[0:1] * x_left + cw[1:2] * xm + cw[2:3] * x_right + cb_ref[...]
    act = (y * _sigmoid(y)).astype(BF16)
    act_ref[...] = act
    q = _dot(act, wq_ref[...])
    kt = _dot_nt(wkt_ref[...], act)
    v = _dot(xm_bf, wv_ref[...])
    q_bf = q.astype(BF16)
    kt_bf = kt.astype(BF16)
    v_bf = v.astype(BF16)
    q_ref[...] = (q * (DH ** -0.5)).astype(BF16)
    kt_ref[...] = kt_bf
    v_ref[...] = v_bf
    g = _dot_nt(wiq_ref[...], q_bf) + _dot(wik_ref[...], kt_bf) + _dot_nt(wiv_ref[...], v_bf)
    g_ref[...] = g + bi_ref[...]


def _feat(xm, conv_w, conv_b, wq, wkt, wv, wiq, wik, wiv, bi, *, tm):
    t = xm.shape[0]
    nb16 = t // 16
    k16 = tm // 16
    full = lambda a: pl.BlockSpec(a.shape, lambda i: (0,) * a.ndim)
    tok = pl.BlockSpec((tm, D_MLSTM), lambda i: (i, 0))
    return pl.pallas_call(
        _feat_kernel,
        grid=(t // tm,),
        in_specs=[tok,
                  pl.BlockSpec((16, D_MLSTM), lambda i: (jnp.maximum(i * k16 - 1, 0), 0)),
                  pl.BlockSpec((16, D_MLSTM), lambda i: (jnp.minimum((i + 1) * k16, nb16 - 1), 0)),
                  full(conv_w), full(conv_b), full(wq), full(wkt), full(wv),
                  full(wiq), full(wik), full(wiv), full(bi)],
        out_specs=[tok,
                   pl.BlockSpec((D_MLSTM, tm), lambda i: (0, i)),
                   tok, tok,
                   pl.BlockSpec((16, tm), lambda i: (0, i))],
        out_shape=[jax.ShapeDtypeStruct((t, D_MLSTM), BF16),
                   jax.ShapeDtypeStruct((D_MLSTM, t), BF16),
                   jax.ShapeDtypeStruct((t, D_MLSTM), BF16),
                   jax.ShapeDtypeStruct((t, D_MLSTM), BF16),
                   jax.ShapeDtypeStruct((16, t), F32)],
        compiler_params=_cparams("parallel"),
        name="feat",
    )(xm, xm, xm, conv_w, conv_b, wq, wkt, wv, wiq, wik, wiv, bi)


def _gates_kernel(g_ref, d_ref, gc_ref):
    g = g_ref[...]
    tl = g.shape[1]
    ig = g[0:8]
    fg = g[8:16]
    lf = jnp.minimum(fg, 0.0) - jnp.log(1.0 + jnp.exp(-jnp.abs(fg)))
    pos = lax.broadcasted_iota(jnp.int32, (8, tl), 1) & (CHUNK - 1)
    is_fwd = lax.broadcasted_iota(jnp.int32, (8, tl), 0) < HEADS

    def scan(x, op, ident):
        xf = x
        xb = x
        k = 1
        while k < CHUNK:
            xf = op(xf, jnp.where(pos >= k, pltpu.roll(xf, k, 1), ident))
            xb = op(xb, jnp.where(pos < CHUNK - k, pltpu.roll(xb, tl - k, 1), ident))
            k *= 2
        return jnp.where(is_fwd, xf, xb)

    b = scan(lf, jnp.add, 0.0)
    d = ig - b
    mloc = b + scan(d, jnp.maximum, NEG_BIG)
    d_ref[...] = d
    stack = jnp.concatenate([b, mloc, jnp.zeros((LANES - 16, tl), F32)], axis=0)
    gc_ref[...] = stack.T


def _gates(g, *, tl):
    t = g.shape[1]
    return pl.pallas_call(
        _gates_kernel,
        grid=(t // tl,),
        in_specs=[pl.BlockSpec((16, tl), lambda i: (0, i))],
        out_specs=[pl.BlockSpec((8, tl), lambda i: (0, i)),
                   pl.BlockSpec((tl, LANES), lambda i: (i, 0))],
        out_shape=[jax.ShapeDtypeStruct((8, t), F32),
                   jax.ShapeDtypeStruct((t, LANES), F32)],
        compiler_params=_cparams("parallel"),
        name="gates",
    )(g)


def _mlstm_kernel(*refs, cps, emit):
    (qf_ref, kf_ref, vf_ref, gcf_ref, drf_ref,
     qb_ref, kb_ref, vb_ref, gcb_ref, drb_ref, c0_ref, m0_ref) = refs[:12]
    if emit:
        hf_ref, hb_ref, cfin_ref, mfin_ref, c_scr, m_scr = refs[12:]
    else:
        cfin_ref, mfin_ref, c_scr, m_scr = refs[12:]
        hf_ref = hb_ref = None
    step = pl.program_id(0)

    @pl.when(step == 0)
    def _():
        c_scr[...] = c0_ref[...]
        m_scr[...] = m0_ref[...]

    ti = lax.broadcasted_iota(jnp.int32, (CHUNK, CHUNK), 0)
    si = lax.broadcasted_iota(jnp.int32, (CHUNK, CHUNK), 1)
    ones_col = jnp.where(si == 0, 1.0, 0.0).astype(BF16)

    for j in range(cps):
        for d in range(2):
            if d == 0:
                q_ref, k_ref, v_ref, gc_ref, dr_ref, h_ref = qf_ref, kf_ref, vf_ref, gcf_ref, drf_ref, hf_ref
                r0, mask, last = j * CHUNK, si <= ti, CHUNK - 1
            else:
                q_ref, k_ref, v_ref, gc_ref, dr_ref, h_ref = qb_ref, kb_ref, vb_ref, gcb_ref, drb_ref, hb_ref
                r0, mask, last = (cps - 1 - j) * CHUNK, si >= ti, 0
            rows = slice(r0, r0 + CHUNK)
            gc = gc_ref[rows, :]
            dr = dr_ref[:, rows]
            for hd in range(HEADS):
                r = d * HEADS + hd
                cols = slice(hd * DH, (hd + 1) * DH)
                b = gc[:, r:r + 1]
                mloc = gc[:, 8 + r:9 + r]
                drow = dr[r:r + 1, :]
                m0 = m_scr[r][0:1, 0:1]
                c_aug = c_scr[r]
                kt = k_ref[cols, rows]
                vaug = jnp.concatenate([v_ref[rows, cols], ones_col], axis=1)
                if emit:
                    qh = q_ref[rows, cols]
                    bm = b + m0
                    m_t = jnp.maximum(bm, mloc)
                    dmat = jnp.where(mask, jnp.exp((b - m_t) + drow), 0.0)
                    smat = (_dot(qh, kt) * dmat).astype(BF16)
                    num = _dot(smat, vaug) + jnp.exp(bm - m_t) * _dot(qh, c_aug.astype(BF16))
                    den = jnp.maximum(jnp.abs(num[:, DH:DH + 1]), jnp.exp(-m_t))
                    h_ref[rows, cols] = (num[:, :DH] / den).astype(h_ref.dtype)
                btot = gc[last:last + 1, r:r + 1]
                amax = gc[last:last + 1, 8 + r:9 + r]
                m_new = jnp.maximum(btot + m0, amax)
                decay = jnp.exp(btot + m0 - m_new)
                kw = (kt.astype(F32) * jnp.exp(btot + drow - m_new)).astype(BF16)
                c_scr[r] = decay * c_aug + _dot(kw, vaug)
                m_scr[r] = jnp.broadcast_to(m_new, (8, LANES))

    @pl.when(step == pl.num_programs(0) - 1)
    def _():
        cfin_ref[...] = c_scr[...]
        mfin_ref[...] = m_scr[...]


def _mlstm(q, kt, v, gc, dr, c0, m0, *, cps, emit):
    t = q.shape[0]
    cb = cps * CHUNK
    nb = t // cb
    fwd_r = lambda i: (i, 0)
    bwd_r = lambda i: (nb - 1 - i, 0)
    fwd_c = lambda i: (0, i)
    bwd_c = lambda i: (0, nb - 1 - i)
    tok = lambda f: pl.BlockSpec((cb, D_MLSTM), f)
    in_specs = []
    for fr, fc in ((fwd_r, fwd_c), (bwd_r, bwd_c)):
        in_specs += [tok(fr), pl.BlockSpec((D_MLSTM, cb), fc), tok(fr),
                     pl.BlockSpec((cb, LANES), fr), pl.BlockSpec((8, cb), fc)]
    cshape = (2 * HEADS, DH, 2 * DH)
    mshape = (2 * HEADS, 8, LANES)
    cspec = pl.BlockSpec(cshape, lambda i: (0, 0, 0))
    mspec = pl.BlockSpec(mshape, lambda i: (0, 0, 0))
    in_specs += [cspec, mspec]
    out_specs = [cspec, mspec]
    out_shape = [jax.ShapeDtypeStruct(cshape, F32), jax.ShapeDtypeStruct(mshape, F32)]
    if emit:
        out_specs = [tok(fwd_r), tok(bwd_r)] + out_specs
        out_shape = [jax.ShapeDtypeStruct((t, D_MLSTM), BF16)] * 2 + out_shape
    return pl.pallas_call(
        functools.partial(_mlstm_kernel, cps=cps, emit=emit),
        grid=(nb,),
        in_specs=in_specs,
        out_specs=out_specs,
        out_shape=out_shape,
        scratch_shapes=[pltpu.VMEM(cshape, F32), pltpu.VMEM(mshape, F32)],
        compiler_params=_cparams("arbitrary"),
        name="mlstm",
    )(q, kt, v, gc, dr, q, kt, v, gc, dr, c0, m0)


def _dft1_kernel(u_ref, f_ref, yc_ref, ys_ref):
    y = _dot(f_ref[...], u_ref[...])
    yc_ref[...] = y[:CHUNK].astype(BF16)
    ys_ref[...] = y[CHUNK:].astype(BF16)


def _dft1(u2d, f1, *, tn):
    n = u2d.shape[1]
    blk = pl.BlockSpec((CHUNK, tn), lambda j: (0, j))
    out = jax.ShapeDtypeStruct((CHUNK, n), BF16)
    return pl.pallas_call(
        _dft1_kernel,
        grid=(n // tn,),
        in_specs=[blk, pl.BlockSpec(f1.shape, lambda j: (0, 0))],
        out_specs=[blk, blk],
        out_shape=[out, out],
        compiler_params=_cparams("parallel"),
        name="dft1",
    )(u2d, f1)


def _dft2_kernel(yc_ref, ys_ref, cw_ref, sw_ref, a2_ref, cs_ref, wf_ref, o_ref, *, kb, scale):
    a2 = a2_ref[...]
    cs = cs_ref[...]
    for kk in range(kb):
        rows = slice(kk * CHUNK, (kk + 1) * CHUNK)
        yc = yc_ref[rows, :].astype(F32)
        ys = ys_ref[rows, :].astype(F32)
        cw = cw_ref[kk]
        sw = sw_ref[kk]
        p = jnp.concatenate([yc * cw - ys * sw, yc * sw + ys * cw], axis=0).astype(BF16)
        x = _dot(a2, p)
        for g in range(FGROUPS):
            cols = slice(g * FCG, (g + 1) * FCG)
            cat = jnp.concatenate([x[:CHUNK, cols], x[CHUNK:, cols]], axis=1).astype(BF16)
            f = _dot(cat, cs) * scale
            o_ref[:, kk * D_FOURIER + g * FCG:kk * D_FOURIER + (g + 1) * FCG] = (
                _dot(f.astype(BF16), wf_ref[g]).astype(o_ref.dtype))


def _dft2(yc, ys, cw3, sw3, a2, cs, wf, *, kb):
    t = yc.shape[0]
    n1 = t // CHUNK
    scale = float(1.0 / np.sqrt(float(t) * FCG))
    tok = pl.BlockSpec((kb * CHUNK, D_FOURIER), lambda i: (i, 0))
    tw = pl.BlockSpec((kb, CHUNK, 1), lambda i: (i, 0, 0))
    full = lambda a: pl.BlockSpec(a.shape, lambda i: (0,) * a.ndim)
    return pl.pallas_call(
        functools.partial(_dft2_kernel, kb=kb, scale=scale),
        grid=(n1 // kb,),
        in_specs=[tok, tok, tw, tw, full(a2), full(cs), full(wf)],
        out_specs=pl.BlockSpec((CHUNK, kb * D_FOURIER), lambda i: (0, i)),
        out_shape=jax.ShapeDtypeStruct((CHUNK, n1 * D_FOURIER), BF16),
        compiler_params=_cparams("parallel"),
        name="dft2",
    )(yc, ys, cw3, sw3, a2, cs, wf)


def _merge_kernel(hf_ref, hb_ref, act_ref, z_ref, yf_ref, x_ref, er_ref, ec_ref,
                  nw_ref, sk_ref, wout_ref, gg1_ref, g2_ref, sh2_ref, wr_ref, br_ref,
                  x1_ref, h2_ref, lg_ref):
    h = hf_ref[...].astype(F32) + hb_ref[...].astype(F32)
    parts = []
    for hd in range(HEADS):
        hh = h[:, hd * DH:(hd + 1) * DH]
        dl = hh - jnp.mean(hh, axis=-1, keepdims=True)
        var = jnp.mean(dl * dl, axis=-1, keepdims=True)
        parts.append(dl * lax.rsqrt(var + EPS))
    hn = jnp.concatenate(parts, axis=-1)
    z = z_ref[...].astype(F32)
    m = (hn * nw_ref[...] + sk_ref[...] * act_ref[...].astype(F32)) * (z * _sigmoid(z))
    cat = jnp.concatenate([m.astype(BF16), yf_ref[...]], axis=-1)
    y = _dot(cat, wout_ref[...])
    x3 = _add_pos(x_ref[...], er_ref, ec_ref)
    xp = x3.reshape(x3.shape[0] * GRID_W, D_MODEL)
    x1 = xp + _rms(y) * gg1_ref[...]
    x1_ref[...] = x1
    h2 = _rms(x1) * g2_ref[...] + sh2_ref[...]
    h2_ref[...] = h2.astype(BF16)
    lg = _dot3(h2, wr_ref[...]) + br_ref[...]
    lg_ref[...] = lg.T[:32]


def _merge(hf, hb, act, z, yf, x3, er3, ec3, nw, sk, wout, gg1, g2, sh2, wr, br, *, rows):
    nr = x3.shape[0]
    t = nr * GRID_W
    tm = rows * GRID_W
    tok = pl.BlockSpec((tm, D_MLSTM), lambda i: (i, 0))
    full = lambda a: pl.BlockSpec(a.shape, lambda i: (0,) * a.ndim)
    return pl.pallas_call(
        _merge_kernel,
        grid=(nr // rows,),
        in_specs=[tok, tok, tok, tok, tok,
                  pl.BlockSpec((rows, GRID_W, D_MODEL), lambda i: (i, 0, 0)),
                  pl.BlockSpec((rows, 1, D_MODEL // 2), lambda i: (i, 0, 0)),
                  pl.BlockSpec((1, GRID_W, D_MODEL // 2), lambda i: (0, 0, 0)),
                  full(nw), full(sk), full(wout), full(gg1), full(g2), full(sh2), full(wr), full(br)],
        out_specs=[pl.BlockSpec((tm, D_MODEL), lambda i: (i, 0)),
                   pl.BlockSpec((tm, D_MODEL), lambda i: (i, 0)),
                   pl.BlockSpec((32, tm), lambda i: (0, i))],
        out_shape=[jax.ShapeDtypeStruct((t, D_MODEL), F32),
                   jax.ShapeDtypeStruct((t, D_MODEL), BF16),
                   jax.ShapeDtypeStruct((32, t), F32)],
        compiler_params=_cparams("parallel"),
        name="merge",
    )(hf, hb, act, z, yf, x3, er3, ec3, nw, sk, wout, gg1, g2, sh2, wr, br)


def _route_kernel(lg_ref, pos_ref, w_ref, qrow_ref, qcol_ref, cnt_ref, *, sbk, r):
    lg = lg_ref[...]
    tl = lg.shape[1]
    g = [lg[j:j + 1] for j in range(N_GROUPS)]
    e = [lg[N_GROUPS + j:N_GROUPS + j + 1] for j in range(N_EXPERTS)]
    gmax = jnp.maximum(jnp.maximum(g[0], g[1]), jnp.maximum(g[2], g[3]))
    den = jnp.exp(g[0] - gmax) + jnp.exp(g[1] - gmax) + jnp.exp(g[2] - gmax) + jnp.exp(g[3] - gmax)
    p_sel = 1.0 / den
    sel = []
    free = jnp.ones((1, tl), F32)
    for j in range(N_GROUPS):
        s = jnp.where(g[j] >= gmax, free, 0.0)
        sel.append(s)
        free = free - s
    es = []
    for j in range(EPG):
        es.append(sel[0] * e[j] + sel[1] * e[EPG + j] + sel[2] * e[2 * EPG + j] + sel[3] * e[3 * EPG + j])
    rank = []
    for j in range(EPG):
        rj = jnp.zeros((1, tl), F32)
        for i in range(EPG):
            if i == j:
                continue
            beats = (es[i] >= es[j]) if i < j else (es[i] > es[j])
            rj = rj + jnp.where(beats, 1.0, 0.0)
        rank.append(rj)
    v1 = jnp.maximum(jnp.maximum(es[0], es[1]), jnp.maximum(es[2], es[3]))
    v2 = sum(jnp.where(rank[j] == 1.0, es[j], 0.0) for j in range(EPG))
    tt = jnp.exp(v2 - v1)
    w1 = p_sel / (1.0 + tt)
    w2 = w1 * tt
    w = [jnp.where(rank[j] == 0.0, w1, jnp.where(rank[j] == 1.0, w2, 0.0)) for j in range(EPG)]
    top2 = [jnp.where(rank[j] < 2.0, 1.0, 0.0) for j in range(EPG)]
    mem = jnp.concatenate([sel[gi] * top2[j] for gi in range(N_GROUPS) for j in range(EPG)], axis=0)
    wts = jnp.concatenate([sel[gi] * w[j] for gi in range(N_GROUPS) for j in range(EPG)], axis=0)
    w_ref[...] = wts
    lane = lax.broadcasted_iota(jnp.int32, (N_EXPERTS, tl), 1) & (sbk - 1)
    c = mem
    k = 1
    while k < sbk:
        c = c + jnp.where(lane >= k, pltpu.roll(c, k, 1), 0.0)
        k *= 2
    rnk = c - 1.0
    pos_ref[...] = jnp.where(mem > 0.0, rnk, -1.0)
    q0 = jnp.full((1, tl), -1.0, F32)
    q1 = jnp.full((1, tl), -1.0, F32)
    w0 = jnp.zeros((1, tl), F32)
    w1 = jnp.zeros((1, tl), F32)
    seen = jnp.zeros((1, tl), F32)
    for ex in range(N_EXPERTS):
        m = mem[ex:ex + 1]
        rk = rnk[ex:ex + 1]
        val = jnp.where(rk < r, rk + float(ex * r), -1.0)
        first = (m * (1.0 - seen)) > 0.0
        second = (m * seen) > 0.0
        q0 = jnp.where(first, val, q0)
        w0 = jnp.where(first, wts[ex:ex + 1], w0)
        q1 = jnp.where(second, val, q1)
        w1 = jnp.where(second, wts[ex:ex + 1], w1)
        seen = seen + m
    qrow_ref[...] = jnp.concatenate([q0, q1, jnp.zeros((6, tl), F32)], axis=0)
    qcol_ref[...] = jnp.concatenate([q0, q1, w0, w1, jnp.zeros((LANES - 4, tl), F32)], axis=0).T
    lane128 = lax.broadcasted_iota(jnp.int32, (N_EXPERTS, LANES), 1)
    cnt = jnp.zeros((N_EXPERTS, LANES), F32)
    for kb in range(tl // sbk):
        tot = jnp.sum(mem[:, kb * sbk:(kb + 1) * sbk], axis=1, keepdims=True)
        cnt = cnt + jnp.where(lane128 == kb, tot, 0.0)
    cnt_ref[...] = cnt


def _route(lg, *, tl, sbk, r):
    t = lg.shape[1]
    row = pl.BlockSpec((N_EXPERTS, tl), lambda i: (0, i))
    return pl.pallas_call(
        functools.partial(_route_kernel, sbk=sbk, r=r),
        grid=(t // tl,),
        in_specs=[pl.BlockSpec((32, tl), lambda i: (0, i))],
        out_specs=[row, row,
                   pl.BlockSpec((8, tl), lambda i: (0, i)),
                   pl.BlockSpec((tl, LANES), lambda i: (i, 0)),
                   pl.BlockSpec((N_EXPERTS, LANES), lambda i: (i, 0))],
        out_shape=[jax.ShapeDtypeStruct((N_EXPERTS, t), F32),
                   jax.ShapeDtypeStruct((N_EXPERTS, t), F32),
                   jax.ShapeDtypeStruct((8, t), F32),
                   jax.ShapeDtypeStruct((t, LANES), F32),
                   jax.ShapeDtypeStruct((t // tl * N_EXPERTS, LANES), F32)],
        compiler_params=_cparams("parallel"),
        name="route",
    )(lg)


def _mlp(x, wg, wu, wd):
    gt = _dot(x, wg)
    a = ((gt * _sigmoid(gt)) * _dot(x, wu)).astype(BF16)
    return _dot(a, wd).astype(BF16)


def _moe_kernel(cnt_ref, h_ref, qrow_ref, pos_ref, w_ref, wg_ref, wu_ref, wd_ref,
                qcol_ref, x1_ref, gg2_ref, o_ref, xs_ref, ovf_ref, *, sbk, r, eps, csb):
    i = pl.program_id(0)
    step = pl.program_id(1)
    nsb = h_ref.shape[0] // sbk
    exp_steps = N_EXPERTS // eps
    slots = N_EXPERTS * r

    @pl.when(step == 0)
    def _():
        ovf_ref[...] = jnp.zeros(ovf_ref.shape, F32)
        pid = lax.broadcasted_iota(jnp.int32, (slots, sbk), 0).astype(F32)
        for sb in range(nsb):
            cols = slice(sb * sbk, (sb + 1) * sbk)
            q0 = qrow_ref[0:1, cols]
            q1 = qrow_ref[1:2, cols]
            s = jnp.where(q0 == pid, 1.0, jnp.where(q1 == pid, 1.0, 0.0)).astype(BF16)
            xs_ref[sb] = _dot(s, h_ref[cols, :]).astype(BF16)

    @pl.when(step < exp_steps)
    def _():
        rid = lax.broadcasted_iota(jnp.int32, (LANES, sbk), 0).astype(F32)
        for k in range(eps):
            e = step * eps + k
            wg, wu, wd = wg_ref[k], wu_ref[k], wd_ref[k]
            off = pl.multiple_of(e * r, 16)
            x = jnp.concatenate([xs_ref[sb, pl.ds(off, r), :] for sb in range(nsb)], axis=0)
            y = _mlp(x, wg, wu, wd)
            for sb in range(nsb):
                xs_ref[sb, pl.ds(off, r), :] = y[sb * r:(sb + 1) * r]

            def sub_block(sb, carry):
                n_over = jnp.maximum(cnt_ref[(i * nsb + sb) * N_EXPERTS + e] - r, 0)
                row0 = pl.multiple_of(sb * sbk, sbk)

                def one_pass(s, c2):
                    base = (r + s * LANES).astype(F32)
                    hit = pos_ref[pl.ds(e, 1), pl.ds(sb, 1), :].reshape(1, sbk) == (rid + base)
                    sel = jnp.where(hit, 1.0, 0.0).astype(BF16)
                    yo = _mlp(_dot(sel, h_ref[pl.ds(row0, sbk), :]).astype(BF16), wg, wu, wd)
                    wct = jnp.where(hit, w_ref[pl.ds(e, 1), pl.ds(sb, 1), :].reshape(1, sbk), 0.0).T
                    ovf_ref[pl.ds(row0, sbk), :] += _dot(wct.astype(BF16), yo)
                    return c2

                return lax.fori_loop(0, (n_over + LANES - 1) // LANES, one_pass, carry)

            lax.fori_loop(0, nsb, sub_block, 0)

    @pl.when(step >= exp_steps)
    def _():
        lane = lax.broadcasted_iota(jnp.int32, (sbk, slots), 1).astype(F32)
        for k in range(csb):
            sb = (step - exp_steps) * csb + k
            rows = slice(k * sbk, (k + 1) * sbk)
            qc = qcol_ref[rows, :]
            wmat = (jnp.where(lane == qc[:, 0:1], qc[:, 2:3], 0.0)
                    + jnp.where(lane == qc[:, 1:2], qc[:, 3:4], 0.0)).astype(BF16)
            y = _dot(wmat, xs_ref[sb]) + ovf_ref[pl.ds(pl.multiple_of(sb * sbk, sbk), sbk), :]
            o_ref[rows, :] = x1_ref[rows, :] + _rms(y) * gg2_ref[...]


def _moe(cnt, h2, qrow, pos3, w3, wg, wu, wd, qcol, x1, gg2, *, tb, sbk, r, eps, csb):
    t = h2.shape[0]
    nsb = tb // sbk
    exp_steps = N_EXPERTS // eps
    comb_steps = nsb // csb
    wblk = lambda i, s, c: (jnp.minimum(s, exp_steps - 1), 0, 0)
    oblk = lambda i, s, c: (i * comb_steps + jnp.maximum(s - exp_steps, 0), 0)
    r3 = pl.BlockSpec((N_EXPERTS, nsb, sbk), lambda i, s, c: (0, i, 0))
    return pl.pallas_call(
        functools.partial(_moe_kernel, sbk=sbk, r=r, eps=eps, csb=csb),
        grid_spec=pltpu.PrefetchScalarGridSpec(
            num_scalar_prefetch=1,
            grid=(t // tb, exp_steps + comb_steps),
            in_specs=[pl.BlockSpec((tb, D_MODEL), lambda i, s, c: (i, 0)),
                      pl.BlockSpec((8, tb), lambda i, s, c: (0, i)),
                      r3, r3,
                      pl.BlockSpec((eps, D_MODEL, D_EXPERT), wblk),
                      pl.BlockSpec((eps, D_MODEL, D_EXPERT), wblk),
                      pl.BlockSpec((eps, D_EXPERT, D_MODEL), wblk),
                      pl.BlockSpec((csb * sbk, LANES), oblk),
                      pl.BlockSpec((csb * sbk, D_MODEL), oblk),
                      pl.BlockSpec((1, D_MODEL), lambda i, s, c: (0, 0))],
            out_specs=pl.BlockSpec((csb * sbk, D_MODEL), oblk),
            scratch_shapes=[pltpu.VMEM((nsb, N_EXPERTS * r, D_MODEL), BF16),
                            pltpu.VMEM((tb, D_MODEL), F32)]),
        out_shape=jax.ShapeDtypeStruct((t, D_MODEL), F32),
        compiler_params=pltpu.CompilerParams(dimension_semantics=("parallel", "arbitrary"),
                                             vmem_limit_bytes=MOE_VMEM_LIMIT),
        name="moe",
    )(cnt, h2, qrow, pos3, w3, wg, wu, wd, qcol, x1, gg2)


def _pos_tables(rows):
    quarter = D_MODEL // 4
    freq = 1.0 / (POS_BASE ** (np.arange(quarter, dtype=np.float64) / quarter))
    r = np.arange(rows, dtype=np.float64)[:, None] * freq
    cl = np.arange(GRID_W, dtype=np.float64)[:, None] * freq
    er = np.concatenate([np.sin(r), np.cos(r)], axis=-1).astype(np.float32)
    ec = np.concatenate([np.sin(cl), np.cos(cl)], axis=-1).astype(np.float32)
    return jnp.asarray(er[:, None, :]), jnp.asarray(ec[None, :, :])


def _dft_tables(t):
    n = np.arange(CHUNK, dtype=np.int64)
    prod = n[:, None] * n[None, :]
    ang = (prod % CHUNK).astype(np.float64) * (2.0 * np.pi / CHUNK)
    c, s = np.cos(ang), np.sin(ang)
    f1 = np.concatenate([c, s], axis=0)
    a2 = np.concatenate([np.concatenate([c, -s], axis=1), np.concatenate([s, c], axis=1)], axis=0)
    cs = np.concatenate([c, -s], axis=0)
    angw = prod.astype(np.float64) * (2.0 * np.pi / t)
    f32 = lambda a: jnp.asarray(a.astype(np.float32))
    return f32(f1), f32(a2), f32(cs), f32(np.cos(angw)[:, :, None]), f32(np.sin(angw)[:, :, None])


def _blockdiag(w):
    n = w.shape[0]
    size = n * QKV_BLOCK
    spread = np.tile(np.eye(QKV_BLOCK, dtype=np.float32), (1, n))
    rows = jnp.dot(w.reshape(size, QKV_BLOCK), jnp.asarray(spread), precision=lax.Precision.HIGHEST)
    blk = np.arange(size) // QKV_BLOCK
    mask = (blk[:, None] == blk[None, :]).astype(np.float32)
    return rows * jnp.asarray(mask)


def _gate_weights(w_f, b_f, w_b, b_b):
    w = jnp.concatenate([w_f[:, :HEADS], w_b[:, :HEADS], w_f[:, HEADS:], w_b[:, HEADS:]], axis=1).T
    b = jnp.concatenate([b_f[:HEADS], b_b[:HEADS], b_f[HEADS:], b_b[HEADS:]])
    return w.astype(BF16), b[:, None]


def kernel(x, c, ctx, c_ctx, w_ada, b_ada, g_pre_mix, g_post_mix, g_pre_ffn, g_post_ffn,
           w_in, conv_w, conv_b, w_q, w_k, w_v, w_if_fwd, b_if_fwd, w_if_bwd, b_if_bwd,
           mlstm_norm_w, mlstm_skip, w_fourier, w_out, w_router_group, b_router_group,
           w_router_expert, b_router_expert, w_gate, w_up, w_down):
    t = x.shape[1]
    rows = t // GRID_W

    c8 = jnp.concatenate([c, c_ctx[None, :], jnp.zeros((6, D_MODEL), F32)], axis=0)
    mod = _ada(c8, w_ada[0], b_ada[0][None, :])
    shift1, scale1, gate1, shift2, scale2, gate2 = [mod[0:1, k * D_MODEL:(k + 1) * D_MODEL] for k in range(N_MOD)]
    shift1c, scale1c = mod[1:2, 0:D_MODEL], mod[1:2, D_MODEL:2 * D_MODEL]
    g1 = g_pre_mix[0][None, :] * (1.0 + scale1)
    g1c = g_pre_mix[0][None, :] * (1.0 + scale1c)
    gg1 = g_post_mix[0][None, :] * gate1
    g2 = g_pre_ffn[0][None, :] * (1.0 + scale2)
    gg2 = g_post_ffn[0][None, :] * gate2

    er3, ec3 = _pos_tables(rows)
    x3 = x.reshape(rows, GRID_W, D_MODEL)
    ctx3 = ctx.reshape(CTX_LEN // GRID_W, GRID_W, D_MODEL)
    w_in_bf = w_in[0].astype(BF16)

    xm_l, z_l, u_l = _inproj(x3, er3, ec3, g1, shift1, w_in_bf, rows=8, add_pos=True)
    xm_c, _, _ = _inproj(ctx3, er3, ec3, g1c, shift1c, w_in_bf, rows=CTX_LEN // GRID_W, add_pos=False)

    wq = _blockdiag(w_q[0]).astype(BF16)
    wkt = _blockdiag(w_k[0]).T.astype(BF16)
    wv = _blockdiag(w_v[0]).astype(BF16)
    wi, bi = _gate_weights(w_if_fwd[0], b_if_fwd[0], w_if_bwd[0], b_if_bwd[0])
    wiq, wik, wiv = wi[:, :D_MLSTM], wi[:, D_MLSTM:2 * D_MLSTM], wi[:, 2 * D_MLSTM:]
    cb = conv_b[0][None, :]
    q_l, kt_l, v_l, act_l, gp_l = _feat(xm_l, conv_w[0], cb, wq, wkt, wv, wiq, wik, wiv, bi, tm=512)
    q_c, kt_c, v_c, _, gp_c = _feat(xm_c, conv_w[0], cb, wq, wkt, wv, wiq, wik, wiv, bi, tm=CTX_LEN)

    dr_l, gc_l = _gates(gp_l, tl=2048)
    dr_c, gc_c = _gates(gp_c, tl=CTX_LEN)

    c0 = jnp.zeros((2 * HEADS, DH, 2 * DH), F32)
    m0 = jnp.zeros((2 * HEADS, 8, LANES), F32)
    c_ctx_fin, m_ctx_fin = _mlstm(q_c, kt_c, v_c, gc_c, dr_c, c0, m0, cps=CTX_LEN // CHUNK, emit=False)
    hf, hb, _, _ = _mlstm(q_l, kt_l, v_l, gc_l, dr_l, c_ctx_fin, m_ctx_fin, cps=4, emit=True)

    f1, a2, cs, cw3, sw3 = _dft_tables(t)
    yc, ys = _dft1(u_l.reshape(CHUNK, (t // CHUNK) * D_FOURIER), f1.astype(BF16), tn=2048)
    yf = _dft2(yc.reshape(t, D_FOURIER), ys.reshape(t, D_FOURIER), cw3, sw3,
               a2.astype(BF16), cs.astype(BF16), w_fourier[0].astype(BF16), kb=4)
    yf = yf.reshape(t, D_FOURIER)

    wr = jnp.concatenate([w_router_group[0], w_router_expert[0],
                          jnp.zeros((D_MODEL, LANES - N_GROUPS - N_EXPERTS), F32)], axis=1)
    br = jnp.concatenate([b_router_group[0], b_router_expert[0],
                          jnp.zeros((LANES - N_GROUPS - N_EXPERTS,), F32)])[None, :]
    x1, h2, lg = _merge(hf, hb, act_l, z_l, yf, x3, er3, ec3,
                        mlstm_norm_w[0][None, :], mlstm_skip[0][None, :], w_out[0].astype(BF16),
                        gg1, g2, shift2, wr, br, rows=8)
    pos, wts, qrow, qcol, cnt = _route(lg, tl=MOE_TB, sbk=MOE_SBK, r=MOE_R)
    nsb = MOE_TB // MOE_SBK
    cnt = cnt.reshape(t // MOE_TB, N_EXPERTS, LANES)[:, :, :nsb]
    cnt = jnp.transpose(cnt, (0, 2, 1)).reshape(-1).astype(jnp.int32)
    pos3 = pos.reshape(N_EXPERTS, t // MOE_SBK, MOE_SBK)
    wts3 = wts.reshape(N_EXPERTS, t // MOE_SBK, MOE_SBK)
    out = _moe(cnt, h2, qrow, pos3, wts3,
               w_gate[0].astype(BF16), w_up[0].astype(BF16), w_down[0].astype(BF16),
               qcol, x1, gg2, tb=MOE_TB, sbk=MOE_SBK, r=MOE_R, eps=MOE_EPS, csb=MOE_CSB)
    return out[None]
```

```python
import functools

import numpy as np
import jax
import jax.numpy as jnp
from jax import lax
from jax.experimental import pallas as pl
from jax.experimental.pallas import tpu as pltpu

F32 = jnp.float32
BF16 = jnp.bfloat16

D_MODEL = 1024
SEQ = 16384
GRID_W = 64
CTX_LEN = 256
D_MLSTM = 512
HEADS = 4
DH = 128
QKV_BLOCK = 4
CONV_K = 3
CHUNK = 128
D_FOURIER = 512
FGROUPS = 4
FCG = 128
N_GROUPS = 4
EPG = 4
N_EXPERTS = 16
D_EXPERT = 512
N_MOD = 6
EPS = 1e-6
POS_BASE = 10000.0
LANES = 128
NEG_BIG = -3.0e38

VMEM_LIMIT = 52 * 1024 * 1024
MOE_VMEM_LIMIT = 56 * 1024 * 1024
MOE_TB = 2048
MOE_SBK = 256
MOE_R = 48
MOE_EPS = 2
MOE_CSB = 1


def _cparams(*sem):
    return pltpu.CompilerParams(dimension_semantics=sem, vmem_limit_bytes=VMEM_LIMIT)


def _dot(a, b):
    return jnp.dot(a, b, preferred_element_type=F32)


def _dot_nt(a, b):
    return lax.dot_general(a, b, (((1,), (1,)), ((), ())), preferred_element_type=F32)


def _split_bf16(a):
    hi = a.astype(BF16)
    lo = (a - hi.astype(F32)).astype(BF16)
    return hi, lo


def _dot3(a, b):
    a_hi, a_lo = _split_bf16(a)
    b_hi, b_lo = _split_bf16(b)
    return _dot(a_hi, b_hi) + (_dot(a_hi, b_lo) + _dot(a_lo, b_hi))


def _sigmoid(x):
    return 1.0 / (1.0 + jnp.exp(-x))


def _rms(x):
    return x * lax.rsqrt(jnp.mean(x * x, axis=-1, keepdims=True) + EPS)


def _ada_kernel(c_ref, w_ref, b_ref, o_ref):
    c = c_ref[...]
    s = c * _sigmoid(c)
    o_ref[...] = _dot3(s, w_ref[...]) + b_ref[...]


def _ada(c8, w, b):
    n = w.shape[1]
    tn = 768
    return pl.pallas_call(
        _ada_kernel,
        grid=(n // tn,),
        in_specs=[pl.BlockSpec((8, D_MODEL), lambda j: (0, 0)),
                  pl.BlockSpec((D_MODEL, tn), lambda j: (0, j)),
                  pl.BlockSpec((1, tn), lambda j: (0, j))],
        out_specs=pl.BlockSpec((8, tn), lambda j: (0, j)),
        out_shape=jax.ShapeDtypeStruct((8, n), F32),
        compiler_params=_cparams("parallel"),
        name="ada",
    )(c8, w, b)


def _add_pos(x3, er_ref, ec_ref):
    r = x3.shape[0]
    pr = jnp.broadcast_to(er_ref[...], (r, GRID_W, D_MODEL // 2))
    pc = jnp.broadcast_to(ec_ref[...], (r, GRID_W, D_MODEL // 2))
    return x3 + jnp.concatenate([pr, pc], axis=-1)


def _inproj_kernel(x_ref, er_ref, ec_ref, g_ref, sh_ref, w_ref, xm_ref, z_ref, u_ref, *, add_pos):
    x3 = x_ref[...]
    if add_pos:
        x3 = _add_pos(x3, er_ref, ec_ref)
    x = x3.reshape(x3.shape[0] * GRID_W, D_MODEL)
    h = _rms(x) * g_ref[...] + sh_ref[...]
    proj = _dot(h.astype(BF16), w_ref[...])
    xm_ref[...] = proj[:, :D_MLSTM].astype(BF16)
    z_ref[...] = proj[:, D_MLSTM:2 * D_MLSTM].astype(BF16)
    u_ref[...] = proj[:, 2 * D_MLSTM:].astype(BF16)


def _inproj(x3, er3, ec3, g_eff, shift, w_in, *, rows, add_pos):
    nr = x3.shape[0]
    t = nr * GRID_W
    tm = rows * GRID_W
    out = jax.ShapeDtypeStruct((t, D_MLSTM), BF16)
    ospec = pl.BlockSpec((tm, D_MLSTM), lambda i: (i, 0))
    vec = pl.BlockSpec((1, D_MODEL), lambda i: (0, 0))
    return pl.pallas_call(
        functools.partial(_inproj_kernel, add_pos=add_pos),
        grid=(nr // rows,),
        in_specs=[pl.BlockSpec((rows, GRID_W, D_MODEL), lambda i: (i, 0, 0)),
                  pl.BlockSpec((rows, 1, D_MODEL // 2), lambda i: (i, 0, 0)),
                  pl.BlockSpec((1, GRID_W, D_MODEL // 2), lambda i: (0, 0, 0)),
                  vec, vec,
                  pl.BlockSpec(w_in.shape, lambda i: (0, 0))],
        out_specs=[ospec, ospec, ospec],
        out_shape=[out, out, out],
        compiler_params=_cparams("parallel"),
        name="inproj",
    )(x3, er3, ec3, g_eff, shift, w_in)


def _feat_kernel(xm_ref, prev_ref, next_ref, cw_ref, cb_ref, wq_ref, wkt_ref, wv_ref,
                 wiq_ref, wik_ref, wiv_ref, bi_ref,
                 q_ref, kt_ref, v_ref, act_ref, g_ref):
    i = pl.program_id(0)
    n = pl.num_programs(0)
    xm_bf = xm_ref[...]
    xm = xm_bf.astype(F32)
    tm = xm.shape[0]
    prev_row = prev_ref[...].astype(F32)[15:16, :] * jnp.where(i > 0, 1.0, 0.0)
    next_row = next_ref[...].astype(F32)[0:1, :] * jnp.where(i < n - 1, 1.0, 0.0)
    rid = lax.broadcasted_iota(jnp.int32, (tm, 1), 0)
    x_left = jnp.where(rid == 0, prev_row, pltpu.roll(xm, 1, 0))
    x_right = jnp.where(rid == tm - 1, next_row, pltpu.roll(xm, tm - 1, 0))
    cw = cw_ref[...]
    y = cw[0:1] * x_left + cw[1:2] * xm + cw[2:3] * x_right + cb_ref[...]
    act = (y * _sigmoid(y)).astype(BF16)
    act_ref[...] = act
    q = _dot(act, wq_ref[...])
    kt = _dot_nt(wkt_ref[...], act)
    v = _dot(xm_bf, wv_ref[...])
    q_bf = q.astype(BF16)
    kt_bf = kt.astype(BF16)
    v_bf = v.astype(BF16)
    q_ref[...] = (q * (DH ** -0.5)).astype(BF16)
    kt_ref[...] = kt_bf
    v_ref[...] = v_bf
    g = _dot_nt(wiq_ref[...], q_bf) + _dot(wik_ref[...], kt_bf) + _dot_nt(wiv_ref[...], v_bf)
    g_ref[...] = g + bi_ref[...]


def _feat(xm, conv_w, conv_b, wq, wkt, wv, wiq, wik, wiv, bi, *, tm):
    t = xm.shape[0]
    nb16 = t // 16
    k16 = tm // 16
    full = lambda a: pl.BlockSpec(a.shape, lambda i: (0,) * a.ndim)
    tok = pl.BlockSpec((tm, D_MLSTM), lambda i: (i, 0))
    return pl.pallas_call(
        _feat_kernel,
        grid=(t // tm,),
        in_specs=[tok,
                  pl.BlockSpec((16, D_MLSTM), lambda i: (jnp.maximum(i * k16 - 1, 0), 0)),
                  pl.BlockSpec((16, D_MLSTM), lambda i: (jnp.minimum((i + 1) * k16, nb16 - 1), 0)),
                  full(conv_w), full(conv_b), full(wq), full(wkt), full(wv),
                  full(wiq), full(wik), full(wiv), full(bi)],
        out_specs=[tok,
                   pl.BlockSpec((D_MLSTM, tm), lambda i: (0, i)),
                   tok, tok,
                   pl.BlockSpec((16, tm), lambda i: (0, i))],
        out_shape=[jax.ShapeDtypeStruct((t, D_MLSTM), BF16),
                   jax.ShapeDtypeStruct((D_MLSTM, t), BF16),
                   jax.ShapeDtypeStruct((t, D_MLSTM), BF16),
                   jax.ShapeDtypeStruct((t, D_MLSTM), BF16),
                   jax.ShapeDtypeStruct((16, t), F32)],
        compiler_params=_cparams("parallel"),
        name="feat",
    )(xm, xm, xm, conv_w, conv_b, wq, wkt, wv, wiq, wik, wiv, bi)


def _gates_kernel(g_ref, d_ref, gc_ref):
    g = g_ref[...]
    tl = g.shape[1]
    ig = g[0:8]
    fg = g[8:16]
    lf = jnp.minimum(fg, 0.0) - jnp.log(1.0 + jnp.exp(-jnp.abs(fg)))
    pos = lax.broadcasted_iota(jnp.int32, (8, tl), 1) & (CHUNK - 1)
    is_fwd = lax.broadcasted_iota(jnp.int32, (8, tl), 0) < HEADS

    def scan(x, op, ident):
        xf = x
        xb = x
        k = 1
        while k < CHUNK:
            xf = op(xf, jnp.where(pos >= k, pltpu.roll(xf, k, 1), ident))
            xb = op(xb, jnp.where(pos < CHUNK - k, pltpu.roll(xb, tl - k, 1), ident))
            k *= 2
        return jnp.where(is_fwd, xf, xb)

    b = scan(lf, jnp.add, 0.0)
    d = ig - b
    mloc = b + scan(d, jnp.maximum, NEG_BIG)
    d_ref[...] = d
    stack = jnp.concatenate([b, mloc, jnp.zeros((LANES - 16, tl), F32)], axis=0)
    gc_ref[...] = stack.T


def _gates(g, *, tl):
    t = g.shape[1]
    return pl.pallas_call(
        _gates_kernel,
        grid=(t // tl,),
        in_specs=[pl.BlockSpec((16, tl), lambda i: (0, i))],
        out_specs=[pl.BlockSpec((8, tl), lambda i: (0, i)),
                   pl.BlockSpec((tl, LANES), lambda i: (i, 0))],
        out_shape=[jax.ShapeDtypeStruct((8, t), F32),
                   jax.ShapeDtypeStruct((t, LANES), F32)],
        compiler_params=_cparams("parallel"),
        name="gates",
    )(g)


def _mlstm_kernel(*refs, cps, emit):
    (qf_ref, kf_ref, vf_ref, gcf_ref, drf_ref,
     qb_ref, kb_ref, vb_ref, gcb_ref, drb_ref, c0_ref, m0_ref) = refs[:12]
    if emit:
        hf_ref, hb_ref, cfin_ref, mfin_ref, c_scr, m_scr = refs[12:]
    else:
        cfin_ref, mfin_ref, c_scr, m_scr = refs[12:]
        hf_ref = hb_ref = None
    step = pl.program_id(0)

    @pl.when(step == 0)
    def _():
        c_scr[...] = c0_ref[...]
        m_scr[...] = m0_ref[...]

    ti = lax.broadcasted_iota(jnp.int32, (CHUNK, CHUNK), 0)
    si = lax.broadcasted_iota(jnp.int32, (CHUNK, CHUNK), 1)
    ones_col = jnp.where(si == 0, 1.0, 0.0).astype(BF16)

    for j in range(cps):
        for d in range(2):
            if d == 0:
                q_ref, k_ref, v_ref, gc_ref, dr_ref, h_ref = qf_ref, kf_ref, vf_ref, gcf_ref, drf_ref, hf_ref
                r0, mask, last = j * CHUNK, si <= ti, CHUNK - 1
            else:
                q_ref, k_ref, v_ref, gc_ref, dr_ref, h_ref = qb_ref, kb_ref, vb_ref, gcb_ref, drb_ref, hb_ref
                r0, mask, last = (cps - 1 - j) * CHUNK, si >= ti, 0
            rows = slice(r0, r0 + CHUNK)
            gc = gc_ref[rows, :]
            dr = dr_ref[:, rows]
            for hd in range(HEADS):
                r = d * HEADS + hd
                cols = slice(hd * DH, (hd + 1) * DH)
                b = gc[:, r:r + 1]
                mloc = gc[:, 8 + r:9 + r]
                drow = dr[r:r + 1, :]
                m0 = m_scr[r][0:1, 0:1]
                c_aug = c_scr[r]
                kt = k_ref[cols, rows]
                vaug = jnp.concatenate([v_ref[rows, cols], ones_col], axis=1)
                if emit:
                    qh = q_ref[rows, cols]
                    bm = b + m0
                    m_t = jnp.maximum(bm, mloc)
                    dmat = jnp.where(mask, jnp.exp((b - m_t) + drow), 0.0)
                    smat = (_dot(qh, kt) * dmat).astype(BF16)
                    num = _dot(smat, vaug) + jnp.exp(bm - m_t) * _dot(qh, c_aug.astype(BF16))
                    den = jnp.maximum(jnp.abs(num[:, DH:DH + 1]), jnp.exp(-m_t))
                    h_ref[rows, cols] = (num[:, :DH] / den).astype(h_ref.dtype)
                btot = gc[last:last + 1, r:r + 1]
                amax = gc[last:last + 1, 8 + r:9 + r]
                m_new = jnp.maximum(btot + m0, amax)
                decay = jnp.exp(btot + m0 - m_new)
                kw = (kt.astype(F32) * jnp.exp(btot + drow - m_new)).astype(BF16)
                c_scr[r] = decay * c_aug + _dot(kw, vaug)
                m_scr[r] = jnp.broadcast_to(m_new, (8, LANES))

    @pl.when(step == pl.num_programs(0) - 1)
    def _():
        cfin_ref[...] = c_scr[...]
        mfin_ref[...] = m_scr[...]


def _mlstm(q, kt, v, gc, dr, c0, m0, *, cps, emit):
    t = q.shape[0]
    cb = cps * CHUNK
    nb = t // cb
    fwd_r = lambda i: (i, 0)
    bwd_r = lambda i: (nb - 1 - i, 0)
    fwd_c = lambda i: (0, i)
    bwd_c = lambda i: (0, nb - 1 - i)
    tok = lambda f: pl.BlockSpec((cb, D_MLSTM), f)
    in_specs = []
    for fr, fc in ((fwd_r, fwd_c), (bwd_r, bwd_c)):
        in_specs += [tok(fr), pl.BlockSpec((D_MLSTM, cb), fc), tok(fr),
                     pl.BlockSpec((cb, LANES), fr), pl.BlockSpec((8, cb), fc)]
    cshape = (2 * HEADS, DH, 2 * DH)
    mshape = (2 * HEADS, 8, LANES)
    cspec = pl.BlockSpec(cshape, lambda i: (0, 0, 0))
    mspec = pl.BlockSpec(mshape, lambda i: (0, 0, 0))
    in_specs += [cspec, mspec]
    out_specs = [cspec, mspec]
    out_shape = [jax.ShapeDtypeStruct(cshape, F32), jax.ShapeDtypeStruct(mshape, F32)]
    if emit:
        out_specs = [tok(fwd_r), tok(bwd_r)] + out_specs
        out_shape = [jax.ShapeDtypeStruct((t, D_MLSTM), BF16)] * 2 + out_shape
    return pl.pallas_call(
        functools.partial(_mlstm_kernel, cps=cps, emit=emit),
        grid=(nb,),
        in_specs=in_specs,
        out_specs=out_specs,
        out_shape=out_shape,
        scratch_shapes=[pltpu.VMEM(cshape, F32), pltpu.VMEM(mshape, F32)],
        compiler_params=_cparams("arbitrary"),
        name="mlstm",
    )(q, kt, v, gc, dr, q, kt, v, gc, dr, c0, m0)


def _dft1_kernel(u_ref, f_ref, yc_ref, ys_ref):
    y = _dot(f_ref[...], u_ref[...])
    yc_ref[...] = y[:CHUNK].astype(BF16)
    ys_ref[...] = y[CHUNK:].astype(BF16)


def _dft1(u2d, f1, *, tn):
    n = u2d.shape[1]
    blk = pl.BlockSpec((CHUNK, tn), lambda j: (0, j))
    out = jax.ShapeDtypeStruct((CHUNK, n), BF16)
    return pl.pallas_call(
        _dft1_kernel,
        grid=(n // tn,),
        in_specs=[blk, pl.BlockSpec(f1.shape, lambda j: (0, 0))],
        out_specs=[blk, blk],
        out_shape=[out, out],
        compiler_params=_cparams("parallel"),
        name="dft1",
    )(u2d, f1)


def _dft2_kernel(yc_ref, ys_ref, cw_ref, sw_ref, a2_ref, cs_ref, wf_ref, o_ref, *, kb, scale):
    a2 = a2_ref[...]
    cs = cs_ref[...]
    for kk in range(kb):
        rows = slice(kk * CHUNK, (kk + 1) * CHUNK)
        yc = yc_ref[rows, :].astype(F32)
        ys = ys_ref[rows, :].astype(F32)
        cw = cw_ref[kk]
        sw = sw_ref[kk]
        p = jnp.concatenate([yc * cw - ys * sw, yc * sw + ys * cw], axis=0).astype(BF16)
        x = _dot(a2, p)
        for g in range(FGROUPS):
            cols = slice(g * FCG, (g + 1) * FCG)
            cat = jnp.concatenate([x[:CHUNK, cols], x[CHUNK:, cols]], axis=1).astype(BF16)
            f = _dot(cat, cs) * scale
            o_ref[:, kk * D_FOURIER + g * FCG:kk * D_FOURIER + (g + 1) * FCG] = (
                _dot(f.astype(BF16), wf_ref[g]).astype(o_ref.dtype))


def _dft2(yc, ys, cw3, sw3, a2, cs, wf, *, kb):
    t = yc.shape[0]
    n1 = t // CHUNK
    scale = float(1.0 / np.sqrt(float(t) * FCG))
    tok = pl.BlockSpec((kb * CHUNK, D_FOURIER), lambda i: (i, 0))
    tw = pl.BlockSpec((kb, CHUNK, 1), lambda i: (i, 0, 0))
    full = lambda a: pl.BlockSpec(a.shape, lambda i: (0,) * a.ndim)
    return pl.pallas_call(
        functools.partial(_dft2_kernel, kb=kb, scale=scale),
        grid=(n1 // kb,),
        in_specs=[tok, tok, tw, tw, full(a2), full(cs), full(wf)],
        out_specs=pl.BlockSpec((CHUNK, kb * D_FOURIER), lambda i: (0, i)),
        out_shape=jax.ShapeDtypeStruct((CHUNK, n1 * D_FOURIER), BF16),
        compiler_params=_cparams("parallel"),
        name="dft2",
    )(yc, ys, cw3, sw3, a2, cs, wf)


def _merge_kernel(hf_ref, hb_ref, act_ref, z_ref, yf_ref, x_ref, er_ref, ec_ref,
                  nw_ref, sk_ref, wout_ref, gg1_ref, g2_ref, sh2_ref, wr_ref, br_ref,
                  x1_ref, h2_ref, lg_ref):
    h = hf_ref[...].astype(F32) + hb_ref[...].astype(F32)
    parts = []
    for hd in range(HEADS):
        hh = h[:, hd * DH:(hd + 1) * DH]
        dl = hh - jnp.mean(hh, axis=-1, keepdims=True)
        var = jnp.mean(dl * dl, axis=-1, keepdims=True)
        parts.append(dl * lax.rsqrt(var + EPS))
    hn = jnp.concatenate(parts, axis=-1)
    z = z_ref[...].astype(F32)
    m = (hn * nw_ref[...] + sk_ref[...] * act_ref[...].astype(F32)) * (z * _sigmoid(z))
    cat = jnp.concatenate([m.astype(BF16), yf_ref[...]], axis=-1)
    y = _dot(cat, wout_ref[...])
    x3 = _add_pos(x_ref[...], er_ref, ec_ref)
    xp = x3.reshape(x3.shape[0] * GRID_W, D_MODEL)
    x1 = xp + _rms(y) * gg1_ref[...]
    x1_ref[...] = x1
    h2 = _rms(x1) * g2_ref[...] + sh2_ref[...]
    h2_ref[...] = h2.astype(BF16)
    lg = _dot3(h2, wr_ref[...]) + br_ref[...]
    lg_ref[...] = lg.T[:32]


def _merge(hf, hb, act, z, yf, x3, er3, ec3, nw, sk, wout, gg1, g2, sh2, wr, br, *, rows):
    nr = x3.shape[0]
    t = nr * GRID_W
    tm = rows * GRID_W
    tok = pl.BlockSpec((tm, D_MLSTM), lambda i: (i, 0))
    full = lambda a: pl.BlockSpec(a.shape, lambda i: (0,) * a.ndim)
    return pl.pallas_call(
        _merge_kernel,
        grid=(nr // rows,),
        in_specs=[tok, tok, tok, tok, tok,
                  pl.BlockSpec((rows, GRID_W, D_MODEL), lambda i: (i, 0, 0)),
                  pl.BlockSpec((rows, 1, D_MODEL // 2), lambda i: (i, 0, 0)),
                  pl.BlockSpec((1, GRID_W, D_MODEL // 2), lambda i: (0, 0, 0)),
                  full(nw), full(sk), full(wout), full(gg1), full(g2), full(sh2), full(wr), full(br)],
        out_specs=[pl.BlockSpec((tm, D_MODEL), lambda i: (i, 0)),
                   pl.BlockSpec((tm, D_MODEL), lambda i: (i, 0)),
                   pl.BlockSpec((32, tm), lambda i: (0, i))],
        out_shape=[jax.ShapeDtypeStruct((t, D_MODEL), F32),
                   jax.ShapeDtypeStruct((t, D_MODEL), BF16),
                   jax.ShapeDtypeStruct((32, t), F32)],
        compiler_params=_cparams("parallel"),
        name="merge",
    )(hf, hb, act, z, yf, x3, er3, ec3, nw, sk, wout, gg1, g2, sh2, wr, br)


def _route_kernel(lg_ref, pos_ref, w_ref, qrow_ref, qcol_ref, cnt_ref, *, sbk, r):
    lg = lg_ref[...]
    tl = lg.shape[1]
    g = [lg[j:j + 1] for j in range(N_GROUPS)]
    e = [lg[N_GROUPS + j:N_GROUPS + j + 1] for j in range(N_EXPERTS)]
    gmax = jnp.maximum(jnp.maximum(g[0], g[1]), jnp.maximum(g[2], g[3]))
    den = jnp.exp(g[0] - gmax) + jnp.exp(g[1] - gmax) + jnp.exp(g[2] - gmax) + jnp.exp(g[3] - gmax)
    p_sel = 1.0 / den
    sel = []
    free = jnp.ones((1, tl), F32)
    for j in range(N_GROUPS):
        s = jnp.where(g[j] >= gmax, free, 0.0)
        sel.append(s)
        free = free - s
    es = []
    for j in range(EPG):
        es.append(sel[0] * e[j] + sel[1] * e[EPG + j] + sel[2] * e[2 * EPG + j] + sel[3] * e[3 * EPG + j])
    rank = []
    for j in range(EPG):
        rj = jnp.zeros((1, tl), F32)
        for i in range(EPG):
            if i == j:
                continue
            beats = (es[i] >= es[j]) if i < j else (es[i] > es[j])
            rj = rj + jnp.where(beats, 1.0, 0.0)
        rank.append(rj)
    v1 = jnp.maximum(jnp.maximum(es[0], es[1]), jnp.maximum(es[2], es[3]))
    v2 = sum(jnp.where(rank[j] == 1.0, es[j], 0.0) for j in range(EPG))
    tt = jnp.exp(v2 - v1)
    w1 = p_sel / (1.0 + tt)
    w2 = w1 * tt
    w = [jnp.where(rank[j] == 0.0, w1, jnp.where(rank[j] == 1.0, w2, 0.0)) for j in range(EPG)]
    top2 = [jnp.where(rank[j] < 2.0, 1.0, 0.0) for j in range(EPG)]
    mem = jnp.concatenate([sel[gi] * top2[j] for gi in range(N_GROUPS) for j in range(EPG)], axis=0)
    wts = jnp.concatenate([sel[gi] * w[j] for gi in range(N_GROUPS) for j in range(EPG)], axis=0)
    w_ref[...] = wts
    lane = lax.broadcasted_iota(jnp.int32, (N_EXPERTS, tl), 1) & (sbk - 1)
    c = mem
    k = 1
    while k < sbk:
        c = c + jnp.where(lane >= k, pltpu.roll(c, k, 1), 0.0)
        k *= 2
    rnk = c - 1.0
    pos_ref[...] = jnp.where(mem > 0.0, rnk, -1.0)
    q0 = jnp.full((1, tl), -1.0, F32)
    q1 = jnp.full((1, tl), -1.0, F32)
    w0 = jnp.zeros((1, tl), F32)
    w1 = jnp.zeros((1, tl), F32)
    seen = jnp.zeros((1, tl), F32)
    for ex in range(N_EXPERTS):
        m = mem[ex:ex + 1]
        rk = rnk[ex:ex + 1]
        val = jnp.where(rk < r, rk + float(ex * r), -1.0)
        first = (m * (1.0 - seen)) > 0.0
        second = (m * seen) > 0.0
        q0 = jnp.where(first, val, q0)
        w0 = jnp.where(first, wts[ex:ex + 1], w0)
        q1 = jnp.where(second, val, q1)
        w1 = jnp.where(second, wts[ex:ex + 1], w1)
        seen = seen + m
    qrow_ref[...] = jnp.concatenate([q0, q1, jnp.zeros((6, tl), F32)], axis=0)
    qcol_ref[...] = jnp.concatenate([q0, q1, w0, w1, jnp.zeros((LANES - 4, tl), F32)], axis=0).T
    lane128 = lax.broadcasted_iota(jnp.int32, (N_EXPERTS, LANES), 1)
    cnt = jnp.zeros((N_EXPERTS, LANES), F32)
    for kb in range(tl // sbk):
        tot = jnp.sum(mem[:, kb * sbk:(kb + 1) * sbk], axis=1, keepdims=True)
        cnt = cnt + jnp.where(lane128 == kb, tot, 0.0)
    cnt_ref[...] = cnt


def _route(lg, *, tl, sbk, r):
    t = lg.shape[1]
    row = pl.BlockSpec((N_EXPERTS, tl), lambda i: (0, i))
    return pl.pallas_call(
        functools.partial(_route_kernel, sbk=sbk, r=r),
        grid=(t // tl,),
        in_specs=[pl.BlockSpec((32, tl), lambda i: (0, i))],
        out_specs=[row, row,
                   pl.BlockSpec((8, tl), lambda i: (0, i)),
                   pl.BlockSpec((tl, LANES), lambda i: (i, 0)),
                   pl.BlockSpec((N_EXPERTS, LANES), lambda i: (i, 0))],
        out_shape=[jax.ShapeDtypeStruct((N_EXPERTS, t), F32),
                   jax.ShapeDtypeStruct((N_EXPERTS, t), F32),
                   jax.ShapeDtypeStruct((8, t), F32),
                   jax.ShapeDtypeStruct((t, LANES), F32),
                   jax.ShapeDtypeStruct((t // tl * N_EXPERTS, LANES), F32)],
        compiler_params=_cparams("parallel"),
        name="route",
    )(lg)


def _mlp(x, wg, wu, wd):
    gt = _dot(x, wg)
    a = ((gt * _sigmoid(gt)) * _dot(x, wu)).astype(BF16)
    return _dot(a, wd).astype(BF16)


def _moe_kernel(cnt_ref, h_ref, qrow_ref, pos_ref, w_ref, wg_ref, wu_ref, wd_ref,
                qcol_ref, x1_ref, gg2_ref, o_ref, xs_ref, ovf_ref, *, sbk, r, eps, csb):
    i = pl.program_id(0)
    step = pl.program_id(1)
    nsb = h_ref.shape[0] // sbk
    exp_steps = N_EXPERTS // eps
    slots = N_EXPERTS * r

    @pl.when(step == 0)
    def _():
        ovf_ref[...] = jnp.zeros(ovf_ref.shape, F32)
        pid = lax.broadcasted_iota(jnp.int32, (slots, sbk), 0).astype(F32)

        def select(sb, carry):
            q0 = qrow_ref[0, pl.ds(sb, 1), :]
            q1 = qrow_ref[1, pl.ds(sb, 1), :]
            s = jnp.where(q0 == pid, 1.0, jnp.where(q1 == pid, 1.0, 0.0)).astype(BF16)
            row0 = pl.multiple_of(sb * sbk, sbk)
            xs_ref[sb] = _dot(s, h_ref[pl.ds(row0, sbk), :]).astype(BF16)
            return carry

        lax.fori_loop(0, nsb, select, 0)

    @pl.when(step < exp_steps)
    def _():
        rid = lax.broadcasted_iota(jnp.int32, (LANES, sbk), 0).astype(F32)

        def expert(k, carry):
            e = step * eps + k
            wg, wu, wd = wg_ref[k], wu_ref[k], wd_ref[k]
            off = pl.multiple_of(e * r, 16)
            x = jnp.concatenate([xs_ref[sb, pl.ds(off, r), :] for sb in range(nsb)], axis=0)
            y = _mlp(x, wg, wu, wd)
            for sb in range(nsb):
                xs_ref[sb, pl.ds(off, r), :] = y[sb * r:(sb + 1) * r]

            def sub_block(sb, c1):
                n_over = jnp.maximum(cnt_ref[(i * nsb + sb) * N_EXPERTS + e] - r, 0)
                row0 = pl.multiple_of(sb * sbk, sbk)

                def one_pass(s, c2):
                    base = (r + s * LANES).astype(F32)
                    hit = pos_ref[pl.ds(e, 1), pl.ds(sb, 1), :].reshape(1, sbk) == (rid + base)
                    sel = jnp.where(hit, 1.0, 0.0).astype(BF16)
                    yo = _mlp(_dot(sel, h_ref[pl.ds(row0, sbk), :]).astype(BF16), wg, wu, wd)
                    wct = jnp.where(hit, w_ref[pl.ds(e, 1), pl.ds(sb, 1), :].reshape(1, sbk), 0.0).T
                    ovf_ref[pl.ds(row0, sbk), :] += _dot(wct.astype(BF16), yo)
                    return c2

                return lax.fori_loop(0, (n_over + LANES - 1) // LANES, one_pass, c1)

            return lax.fori_loop(0, nsb, sub_block, carry)

        lax.fori_loop(0, eps, expert, 0)

    @pl.when(step >= exp_steps)
    def _():
        lane = lax.broadcasted_iota(jnp.int32, (sbk, slots), 1).astype(F32)
        for k in range(csb):
            sb = (step - exp_steps) * csb + k
            rows = slice(k * sbk, (k + 1) * sbk)
            qc = qcol_ref[rows, :]
            wmat = (jnp.where(lane == qc[:, 0:1], qc[:, 2:3], 0.0)
                    + jnp.where(lane == qc[:, 1:2], qc[:, 3:4], 0.0)).astype(BF16)
            y = _dot(wmat, xs_ref[sb]) + ovf_ref[pl.ds(pl.multiple_of(sb * sbk, sbk), sbk), :]
            o_ref[rows, :] = x1_ref[rows, :] + _rms(y) * gg2_ref[...]


def _moe(cnt, h2, qrow, pos3, w3, wg, wu, wd, qcol, x1, gg2, *, tb, sbk, r, eps, csb):
    t = h2.shape[0]
    nsb = tb // sbk
    exp_steps = N_EXPERTS // eps
    comb_steps = nsb // csb
    wblk = lambda i, s, c: (jnp.minimum(s, exp_steps - 1), 0, 0)
    oblk = lambda i, s, c: (i * comb_steps + jnp.maximum(s - exp_steps, 0), 0)
    r3 = pl.BlockSpec((N_EXPERTS, nsb, sbk), lambda i, s, c: (0, i, 0))
    return pl.pallas_call(
        functools.partial(_moe_kernel, sbk=sbk, r=r, eps=eps, csb=csb),
        grid_spec=pltpu.PrefetchScalarGridSpec(
            num_scalar_prefetch=1,
            grid=(t // tb, exp_steps + comb_steps),
            in_specs=[pl.BlockSpec((tb, D_MODEL), lambda i, s, c: (i, 0)),
                      pl.BlockSpec((8, nsb, sbk), lambda i, s, c: (0, i, 0)),
                      r3, r3,
                      pl.BlockSpec((eps, D_MODEL, D_EXPERT), wblk),
                      pl.BlockSpec((eps, D_MODEL, D_EXPERT), wblk),
                      pl.BlockSpec((eps, D_EXPERT, D_MODEL), wblk),
                      pl.BlockSpec((csb * sbk, LANES), oblk),
                      pl.BlockSpec((csb * sbk, D_MODEL), oblk),
                      pl.BlockSpec((1, D_MODEL), lambda i, s, c: (0, 0))],
            out_specs=pl.BlockSpec((csb * sbk, D_MODEL), oblk),
            scratch_shapes=[pltpu.VMEM((nsb, N_EXPERTS * r, D_MODEL), BF16),
                            pltpu.VMEM((tb, D_MODEL), F32)]),
        out_shape=jax.ShapeDtypeStruct((t, D_MODEL), F32),
        compiler_params=pltpu.CompilerParams(dimension_semantics=("parallel", "arbitrary"),
                                             vmem_limit_bytes=MOE_VMEM_LIMIT),
        name="moe",
    )(cnt, h2, qrow, pos3, w3, wg, wu, wd, qcol, x1, gg2)


def _pos_tables(rows):
    quarter = D_MODEL // 4
    freq = 1.0 / (POS_BASE ** (np.arange(quarter, dtype=np.float64) / quarter))
    r = np.arange(rows, dtype=np.float64)[:, None] * freq
    cl = np.arange(GRID_W, dtype=np.float64)[:, None] * freq
    er = np.concatenate([np.sin(r), np.cos(r)], axis=-1).astype(np.float32)
    ec = np.concatenate([np.sin(cl), np.cos(cl)], axis=-1).astype(np.float32)
    return jnp.asarray(er[:, None, :]), jnp.asarray(ec[None, :, :])


def _dft_tables(t):
    n = np.arange(CHUNK, dtype=np.int64)
    prod = n[:, None] * n[None, :]
    ang = (prod % CHUNK).astype(np.float64) * (2.0 * np.pi / CHUNK)
    c, s = np.cos(ang), np.sin(ang)
    f1 = np.concatenate([c, s], axis=0)
    a2 = np.concatenate([np.concatenate([c, -s], axis=1), np.concatenate([s, c], axis=1)], axis=0)
    cs = np.concatenate([c, -s], axis=0)
    angw = prod.astype(np.float64) * (2.0 * np.pi / t)
    f32 = lambda a: jnp.asarray(a.astype(np.float32))
    return f32(f1), f32(a2), f32(cs), f32(np.cos(angw)[:, :, None]), f32(np.sin(angw)[:, :, None])


def _blockdiag(w):
    n = w.shape[0]
    size = n * QKV_BLOCK
    spread = np.tile(np.eye(QKV_BLOCK, dtype=np.float32), (1, n))
    rows = jnp.dot(w.reshape(size, QKV_BLOCK), jnp.asarray(spread), precision=lax.Precision.HIGHEST)
    blk = np.arange(size) // QKV_BLOCK
    mask = (blk[:, None] == blk[None, :]).astype(np.float32)
    return rows * jnp.asarray(mask)


def _gate_weights(w_f, b_f, w_b, b_b):
    w = jnp.concatenate([w_f[:, :HEADS], w_b[:, :HEADS], w_f[:, HEADS:], w_b[:, HEADS:]], axis=1).T
    b = jnp.concatenate([b_f[:HEADS], b_b[:HEADS], b_f[HEADS:], b_b[HEADS:]])
    return w.astype(BF16), b[:, None]


def kernel(x, c, ctx, c_ctx, w_ada, b_ada, g_pre_mix, g_post_mix, g_pre_ffn, g_post_ffn,
           w_in, conv_w, conv_b, w_q, w_k, w_v, w_if_fwd, b_if_fwd, w_if_bwd, b_if_bwd,
           mlstm_norm_w, mlstm_skip, w_fourier, w_out, w_router_group, b_router_group,
           w_router_expert, b_router_expert, w_gate, w_up, w_down):
    t = x.shape[1]
    rows = t // GRID_W

    c8 = jnp.concatenate([c, c_ctx[None, :], jnp.zeros((6, D_MODEL), F32)], axis=0)
    mod = _ada(c8, w_ada[0], b_ada[0][None, :])
    shift1, scale1, gate1, shift2, scale2, gate2 = [mod[0:1, k * D_MODEL:(k + 1) * D_MODEL] for k in range(N_MOD)]
    shift1c, scale1c = mod[1:2, 0:D_MODEL], mod[1:2, D_MODEL:2 * D_MODEL]
    g1 = g_pre_mix[0][None, :] * (1.0 + scale1)
    g1c = g_pre_mix[0][None, :] * (1.0 + scale1c)
    gg1 = g_post_mix[0][None, :] * gate1
    g2 = g_pre_ffn[0][None, :] * (1.0 + scale2)
    gg2 = g_post_ffn[0][None, :] * gate2

    er3, ec3 = _pos_tables(rows)
    x3 = x.reshape(rows, GRID_W, D_MODEL)
    ctx3 = ctx.reshape(CTX_LEN // GRID_W, GRID_W, D_MODEL)
    w_in_bf = w_in[0].astype(BF16)

    xm_l, z_l, u_l = _inproj(x3, er3, ec3, g1, shift1, w_in_bf, rows=8, add_pos=True)
    xm_c, _, _ = _inproj(ctx3, er3, ec3, g1c, shift1c, w_in_bf, rows=CTX_LEN // GRID_W, add_pos=False)

    wq = _blockdiag(w_q[0]).astype(BF16)
    wkt = _blockdiag(w_k[0]).T.astype(BF16)
    wv = _blockdiag(w_v[0]).astype(BF16)
    wi, bi = _gate_weights(w_if_fwd[0], b_if_fwd[0], w_if_bwd[0], b_if_bwd[0])
    wiq, wik, wiv = wi[:, :D_MLSTM], wi[:, D_MLSTM:2 * D_MLSTM], wi[:, 2 * D_MLSTM:]
    cb = conv_b[0][None, :]
    q_l, kt_l, v_l, act_l, gp_l = _feat(xm_l, conv_w[0], cb, wq, wkt, wv, wiq, wik, wiv, bi, tm=512)
    q_c, kt_c, v_c, _, gp_c = _feat(xm_c, conv_w[0], cb, wq, wkt, wv, wiq, wik, wiv, bi, tm=CTX_LEN)

    dr_l, gc_l = _gates(gp_l, tl=2048)
    dr_c, gc_c = _gates(gp_c, tl=CTX_LEN)

    c0 = jnp.zeros((2 * HEADS, DH, 2 * DH), F32)
    m0 = jnp.zeros((2 * HEADS, 8, LANES), F32)
    c_ctx_fin, m_ctx_fin = _mlstm(q_c, kt_c, v_c, gc_c, dr_c, c0, m0, cps=CTX_LEN // CHUNK, emit=False)
    hf, hb, _, _ = _mlstm(q_l, kt_l, v_l, gc_l, dr_l, c_ctx_fin, m_ctx_fin, cps=4, emit=True)

    f1, a2, cs, cw3, sw3 = _dft_tables(t)
    yc, ys = _dft1(u_l.reshape(CHUNK, (t // CHUNK) * D_FOURIER), f1.astype(BF16), tn=2048)
    yf = _dft2(yc.reshape(t, D_FOURIER), ys.reshape(t, D_FOURIER), cw3, sw3,
               a2.astype(BF16), cs.astype(BF16), w_fourier[0].astype(BF16), kb=4)
    yf = yf.reshape(t, D_FOURIER)

    wr = jnp.concatenate([w_router_group[0], w_router_expert[0],
                          jnp.zeros((D_MODEL, LANES - N_GROUPS - N_EXPERTS), F32)], axis=1)
    br = jnp.concatenate([b_router_group[0], b_router_expert[0],
                          jnp.zeros((LANES - N_GROUPS - N_EXPERTS,), F32)])[None, :]
    x1, h2, lg = _merge(hf, hb, act_l, z_l, yf, x3, er3, ec3,
                        mlstm_norm_w[0][None, :], mlstm_skip[0][None, :], w_out[0].astype(BF16),
                        gg1, g2, shift2, wr, br, rows=8)
    pos, wts, qrow, qcol, cnt = _route(lg, tl=MOE_TB, sbk=MOE_SBK, r=MOE_R)
    nsb = MOE_TB // MOE_SBK
    cnt = cnt.reshape(t // MOE_TB, N_EXPERTS, LANES)[:, :, :nsb]
    cnt = jnp.transpose(cnt, (0, 2, 1)).reshape(-1).astype(jnp.int32)
    pos3 = pos.reshape(N_EXPERTS, t // MOE_SBK, MOE_SBK)
    wts3 = wts.reshape(N_EXPERTS, t // MOE_SBK, MOE_SBK)
    out = _moe(cnt, h2, qrow.reshape(8, t // MOE_SBK, MOE_SBK), pos3, wts3,
               w_gate[0].astype(BF16), w_up[0].astype(BF16), w_down[0].astype(BF16),
               qcol, x1, gg2, tb=MOE_TB, sbk=MOE_SBK, r=MOE_R, eps=MOE_EPS, csb=MOE_CSB)
    return out[None]
```

```python
import functools

import numpy as np
import jax
import jax.numpy as jnp
from jax import lax
from jax.experimental import pallas as pl
from jax.experimental.pallas import tpu as pltpu

F32 = jnp.float32
BF16 = jnp.bfloat16

D_MODEL = 1024
SEQ = 16384
GRID_W = 64
CTX_LEN = 256
D_MLSTM = 512
HEADS = 4
DH = 128
QKV_BLOCK = 4
CONV_K = 3
CHUNK = 128
D_FOURIER = 512
FGROUPS = 4
FCG = 128
N_GROUPS = 4
EPG = 4
N_EXPERTS = 16
D_EXPERT = 512
N_MOD = 6
EPS = 1e-6
POS_BASE = 10000.0
LANES = 128
NEG_BIG = -3.0e38

VMEM_LIMIT = 52 * 1024 * 1024
MOE_VMEM_LIMIT = 56 * 1024 * 1024
MOE_TB = 2048
MOE_SBK = 256
MOE_R = 48
MOE_EPS = 2
MOE_CSB = 1


def _cparams(*sem):
    return pltpu.CompilerParams(dimension_semantics=sem, vmem_limit_bytes=VMEM_LIMIT)


def _dot(a, b):
    return jnp.dot(a, b, preferred_element_type=F32)


def _dot_nt(a, b):
    return lax.dot_general(a, b, (((1,), (1,)), ((), ())), preferred_element_type=F32)


def _split_bf16(a):
    hi = a.astype(BF16)
    lo = (a - hi.astype(F32)).astype(BF16)
    return hi, lo


def _dot3(a, b):
    a_hi, a_lo = _split_bf16(a)
    b_hi, b_lo = _split_bf16(b)
    return _dot(a_hi, b_hi) + (_dot(a_hi, b_lo) + _dot(a_lo, b_hi))


def _sigmoid(x):
    return 1.0 / (1.0 + jnp.exp(-x))


def _rms(x):
    return x * lax.rsqrt(jnp.mean(x * x, axis=-1, keepdims=True) + EPS)


def _ada_kernel(c_ref, w_ref, b_ref, o_ref):
    c = c_ref[...]
    s = c * _sigmoid(c)
    o_ref[...] = _dot3(s, w_ref[...]) + b_ref[...]


def _ada(c8, w, b):
    n = w.shape[1]
    tn = 768
    return pl.pallas_call(
        _ada_kernel,
        grid=(n // tn,),
        in_specs=[pl.BlockSpec((8, D_MODEL), lambda j: (0, 0)),
                  pl.BlockSpec((D_MODEL, tn), lambda j: (0, j)),
                  pl.BlockSpec((1, tn), lambda j: (0, j))],
        out_specs=pl.BlockSpec((8, tn), lambda j: (0, j)),
        out_shape=jax.ShapeDtypeStruct((8, n), F32),
        compiler_params=_cparams("parallel"),
        name="ada",
    )(c8, w, b)


def _add_pos(x3, er_ref, ec_ref):
    r = x3.shape[0]
    pr = jnp.broadcast_to(er_ref[...], (r, GRID_W, D_MODEL // 2))
    pc = jnp.broadcast_to(ec_ref[...], (r, GRID_W, D_MODEL // 2))
    return x3 + jnp.concatenate([pr, pc], axis=-1)


def _inproj_kernel(x_ref, er_ref, ec_ref, g_ref, sh_ref, w_ref, xm_ref, z_ref, u_ref, *, add_pos):
    x3 = x_ref[...]
    if add_pos:
        x3 = _add_pos(x3, er_ref, ec_ref)
    x = x3.reshape(x3.shape[0] * GRID_W, D_MODEL)
    h = _rms(x) * g_ref[...] + sh_ref[...]
    proj = _dot(h.astype(BF16), w_ref[...])
    xm_ref[...] = proj[:, :D_MLSTM].astype(BF16)
    z_ref[...] = proj[:, D_MLSTM:2 * D_MLSTM].astype(BF16)
    u_ref[...] = proj[:, 2 * D_MLSTM:].astype(BF16)


def _inproj(x3, er3, ec3, g_eff, shift, w_in, *, rows, add_pos):
    nr = x3.shape[0]
    t = nr * GRID_W
    tm = rows * GRID_W
    out = jax.ShapeDtypeStruct((t, D_MLSTM), BF16)
    ospec = pl.BlockSpec((tm, D_MLSTM), lambda i: (i, 0))
    vec = pl.BlockSpec((1, D_MODEL), lambda i: (0, 0))
    return pl.pallas_call(
        functools.partial(_inproj_kernel, add_pos=add_pos),
        grid=(nr // rows,),
        in_specs=[pl.BlockSpec((rows, GRID_W, D_MODEL), lambda i: (i, 0, 0)),
                  pl.BlockSpec((rows, 1, D_MODEL // 2), lambda i: (i, 0, 0)),
                  pl.BlockSpec((1, GRID_W, D_MODEL // 2), lambda i: (0, 0, 0)),
                  vec, vec,
                  pl.BlockSpec(w_in.shape, lambda i: (0, 0))],
        out_specs=[ospec, ospec, ospec],
        out_shape=[out, out, out],
        compiler_params=_cparams("parallel"),
        name="inproj",
    )(x3, er3, ec3, g_eff, shift, w_in)


def _feat_kernel(xm_ref, prev_ref, next_ref, cw_ref, cb_ref, wq_ref, wkt_ref, wv_ref,
                 wiq_ref, wik_ref, wiv_ref, bi_ref,
                 q_ref, kt_ref, v_ref, act_ref, g_ref):
    i = pl.program_id(0)
    n = pl.num_programs(0)
    xm_bf = xm_ref[...]
    xm = xm_bf.astype(F32)
    tm = xm.shape[0]
    prev_row = prev_ref[...].astype(F32)[15:16, :] * jnp.where(i > 0, 1.0, 0.0)
    next_row = next_ref[...].astype(F32)[0:1, :] * jnp.where(i < n - 1, 1.0, 0.0)
    rid = lax.broadcasted_iota(jnp.int32, (tm, 1), 0)
    x_left = jnp.where(rid == 0, prev_row, pltpu.roll(xm, 1, 0))
    x_right = jnp.where(rid == tm - 1, next_row, pltpu.roll(xm, tm - 1, 0))
    cw = cw_ref[...]
    y = cw[0:1] * x_left + cw[1:2] * xm + cw[2:3] * x_right + cb_ref[...]
    act = (y * _sigmoid(y)).astype(BF16)
    act_ref[...] = act
    q = _dot(act, wq_ref[...])
    kt = _dot_nt(wkt_ref[...], act)
    v = _dot(xm_bf, wv_ref[...])
    q_bf = q.astype(BF16)
    kt_bf = kt.astype(BF16)
    v_bf = v.astype(BF16)
    q_ref[...] = (q * (DH ** -0.5)).astype(BF16)
    kt_ref[...] = kt_bf
    v_ref[...] = v_bf
    g = _dot_nt(wiq_ref[...], q_bf) + _dot(wik_ref[...], kt_bf) + _dot_nt(wiv_ref[...], v_bf)
    g_ref[...] = g + bi_ref[...]


def _feat(xm, conv_w, conv_b, wq, wkt, wv, wiq, wik, wiv, bi, *, tm):
    t = xm.shape[0]
    nb16 = t // 16
    k16 = tm // 16
    full = lambda a: pl.BlockSpec(a.shape, lambda i: (0,) * a.ndim)
    tok = pl.BlockSpec((tm, D_MLSTM), lambda i: (i, 0))
    return pl.pallas_call(
        _feat_kernel,
        grid=(t // tm,),
        in_specs=[tok,
                  pl.BlockSpec((16, D_MLSTM), lambda i: (jnp.maximum(i * k16 - 1, 0), 0)),
                  pl.BlockSpec((16, D_MLSTM), lambda i: (jnp.minimum((i + 1) * k16, nb16 - 1), 0)),
                  full(conv_w), full(conv_b), full(wq), full(wkt), full(wv),
                  full(wiq), full(wik), full(wiv), full(bi)],
        out_specs=[tok,
                   pl.BlockSpec((D_MLSTM, tm), lambda i: (0, i)),
                   tok, tok,
                   pl.BlockSpec((16, tm), lambda i: (0, i))],
        out_shape=[jax.ShapeDtypeStruct((t, D_MLSTM), BF16),
                   jax.ShapeDtypeStruct((D_MLSTM, t), BF16),
                   jax.ShapeDtypeStruct((t, D_MLSTM), BF16),
                   jax.ShapeDtypeStruct((t, D_MLSTM), BF16),
                   jax.ShapeDtypeStruct((16, t), F32)],
        compiler_params=_cparams("parallel"),
        name="feat",
    )(xm, xm, xm, conv_w, conv_b, wq, wkt, wv, wiq, wik, wiv, bi)


def _gates_kernel(g_ref, d_ref, gc_ref):
    g = g_ref[...]
    tl = g.shape[1]
    ig = g[0:8]
    fg = g[8:16]
    lf = jnp.minimum(fg, 0.0) - jnp.log(1.0 + jnp.exp(-jnp.abs(fg)))
    pos = lax.broadcasted_iota(jnp.int32, (8, tl), 1) & (CHUNK - 1)
    is_fwd = lax.broadcasted_iota(jnp.int32, (8, tl), 0) < HEADS

    def scan(x, op, ident):
        xf = x
        xb = x
        k = 1
        while k < CHUNK:
            xf = op(xf, jnp.where(pos >= k, pltpu.roll(xf, k, 1), ident))
            xb = op(xb, jnp.where(pos < CHUNK - k, pltpu.roll(xb, tl - k, 1), ident))
            k *= 2
        return jnp.where(is_fwd, xf, xb)

    b = scan(lf, jnp.add, 0.0)
    d = ig - b
    mloc = b + scan(d, jnp.maximum, NEG_BIG)
    d_ref[...] = d
    stack = jnp.concatenate([b, mloc, jnp.zeros((LANES - 16, tl), F32)], axis=0)
    gc_ref[...] = stack.T


def _gates(g, *, tl):
    t = g.shape[1]
    return pl.pallas_call(
        _gates_kernel,
        grid=(t // tl,),
        in_specs=[pl.BlockSpec((16, tl), lambda i: (0, i))],
        out_specs=[pl.BlockSpec((8, tl), lambda i: (0, i)),
                   pl.BlockSpec((tl, LANES), lambda i: (i, 0))],
        out_shape=[jax.ShapeDtypeStruct((8, t), F32),
                   jax.ShapeDtypeStruct((t, LANES), F32)],
        compiler_params=_cparams("parallel"),
        name="gates",
    )(g)


def _mlstm_kernel(*refs, cps, emit):
    (qf_ref, kf_ref, vf_ref, gcf_ref, drf_ref,
     qb_ref, kb_ref, vb_ref, gcb_ref, drb_ref, c0_ref, m0_ref) = refs[:12]
    if emit:
        hf_ref, hb_ref, cfin_ref, mfin_ref, c_scr, m_scr = refs[12:]
    else:
        cfin_ref, mfin_ref, c_scr, m_scr = refs[12:]
        hf_ref = hb_ref = None
    step = pl.program_id(0)

    @pl.when(step == 0)
    def _():
        c_scr[...] = c0_ref[...]
        m_scr[...] = m0_ref[...]

    ti = lax.broadcasted_iota(jnp.int32, (CHUNK, CHUNK), 0)
    si = lax.broadcasted_iota(jnp.int32, (CHUNK, CHUNK), 1)
    ones_col = jnp.where(si == 0, 1.0, 0.0).astype(BF16)

    for j in range(cps):
        for d in range(2):
            if d == 0:
                q_ref, k_ref, v_ref, gc_ref, dr_ref, h_ref = qf_ref, kf_ref, vf_ref, gcf_ref, drf_ref, hf_ref
                r0, mask, last = j * CHUNK, si <= ti, CHUNK - 1
            else:
                q_ref, k_ref, v_ref, gc_ref, dr_ref, h_ref = qb_ref, kb_ref, vb_ref, gcb_ref, drb_ref, hb_ref
                r0, mask, last = (cps - 1 - j) * CHUNK, si >= ti, 0
            rows = slice(r0, r0 + CHUNK)
            gc = gc_ref[rows, :]
            dr = dr_ref[:, rows]
            for hd in range(HEADS):
                r = d * HEADS + hd
                cols = slice(hd * DH, (hd + 1) * DH)
                b = gc[:, r:r + 1]
                mloc = gc[:, 8 + r:9 + r]
                drow = dr[r:r + 1, :]
                m0 = m_scr[r][0:1, 0:1]
                c_aug = c_scr[r]
                kt = k_ref[cols, rows]
                vaug = jnp.concatenate([v_ref[rows, cols], ones_col], axis=1)
                if emit:
                    qh = q_ref[rows, cols]
                    bm = b + m0
                    m_t = jnp.maximum(bm, mloc)
                    dmat = jnp.where(mask, jnp.exp((b - m_t) + drow), 0.0)
                    smat = (_dot(qh, kt) * dmat).astype(BF16)
                    num = _dot(smat, vaug) + jnp.exp(bm - m_t) * _dot(qh, c_aug.astype(BF16))
                    den = jnp.maximum(jnp.abs(num[:, DH:DH + 1]), jnp.exp(-m_t))
                    h_ref[rows, cols] = (num[:, :DH] / den).astype(h_ref.dtype)
                btot = gc[last:last + 1, r:r + 1]
                amax = gc[last:last + 1, 8 + r:9 + r]
                m_new = jnp.maximum(btot + m0, amax)
                decay = jnp.exp(btot + m0 - m_new)
                kw = (kt.astype(F32) * jnp.exp(btot + drow - m_new)).astype(BF16)
                c_scr[r] = decay * c_aug + _dot(kw, vaug)
                m_scr[r] = jnp.broadcast_to(m_new, (8, LANES))

    @pl.when(step == pl.num_programs(0) - 1)
    def _():
        cfin_ref[...] = c_scr[...]
        mfin_ref[...] = m_scr[...]


def _mlstm(q, kt, v, gc, dr, c0, m0, *, cps, emit):
    t = q.shape[0]
    cb = cps * CHUNK
    nb = t // cb
    fwd_r = lambda i: (i, 0)
    bwd_r = lambda i: (nb - 1 - i, 0)
    fwd_c = lambda i: (0, i)
    bwd_c = lambda i: (0, nb - 1 - i)
    tok = lambda f: pl.BlockSpec((cb, D_MLSTM), f)
    in_specs = []
    for fr, fc in ((fwd_r, fwd_c), (bwd_r, bwd_c)):
        in_specs += [tok(fr), pl.BlockSpec((D_MLSTM, cb), fc), tok(fr),
                     pl.BlockSpec((cb, LANES), fr), pl.BlockSpec((8, cb), fc)]
    cshape = (2 * HEADS, DH, 2 * DH)
    mshape = (2 * HEADS, 8, LANES)
    cspec = pl.BlockSpec(cshape, lambda i: (0, 0, 0))
    mspec = pl.BlockSpec(mshape, lambda i: (0, 0, 0))
    in_specs += [cspec, mspec]
    out_specs = [cspec, mspec]
    out_shape = [jax.ShapeDtypeStruct(cshape, F32), jax.ShapeDtypeStruct(mshape, F32)]
    if emit:
        out_specs = [tok(fwd_r), tok(bwd_r)] + out_specs
        out_shape = [jax.ShapeDtypeStruct((t, D_MLSTM), BF16)] * 2 + out_shape
    return pl.pallas_call(
        functools.partial(_mlstm_kernel, cps=cps, emit=emit),
        grid=(nb,),
        in_specs=in_specs,
        out_specs=out_specs,
        out_shape=out_shape,
        scratch_shapes=[pltpu.VMEM(cshape, F32), pltpu.VMEM(mshape, F32)],
        compiler_params=_cparams("arbitrary"),
        name="mlstm",
    )(q, kt, v, gc, dr, q, kt, v, gc, dr, c0, m0)


def _dft1_kernel(u_ref, f_ref, yc_ref, ys_ref):
    y = _dot(f_ref[...], u_ref[...])
    yc_ref[...] = y[:CHUNK].astype(BF16)
    ys_ref[...] = y[CHUNK:].astype(BF16)


def _dft1(u2d, f1, *, tn):
    n = u2d.shape[1]
    blk = pl.BlockSpec((CHUNK, tn), lambda j: (0, j))
    out = jax.ShapeDtypeStruct((CHUNK, n), BF16)
    return pl.pallas_call(
        _dft1_kernel,
        grid=(n // tn,),
        in_specs=[blk, pl.BlockSpec(f1.shape, lambda j: (0, 0))],
        out_specs=[blk, blk],
        out_shape=[out, out],
        compiler_params=_cparams("parallel"),
        name="dft1",
    )(u2d, f1)


def _dft2_kernel(yc_ref, ys_ref, cw_ref, sw_ref, a2_ref, cs_ref, wf_ref, o_ref, *, kb, scale):
    a2 = a2_ref[...]
    cs = cs_ref[...]
    for kk in range(kb):
        rows = slice(kk * CHUNK, (kk + 1) * CHUNK)
        yc = yc_ref[rows, :].astype(F32)
        ys = ys_ref[rows, :].astype(F32)
        cw = cw_ref[kk]
        sw = sw_ref[kk]
        p = jnp.concatenate([yc * cw - ys * sw, yc * sw + ys * cw], axis=0).astype(BF16)
        x = _dot(a2, p)
        for g in range(FGROUPS):
            cols = slice(g * FCG, (g + 1) * FCG)
            cat = jnp.concatenate([x[:CHUNK, cols], x[CHUNK:, cols]], axis=1).astype(BF16)
            f = _dot(cat, cs) * scale
            o_ref[:, kk * D_FOURIER + g * FCG:kk * D_FOURIER + (g + 1) * FCG] = (
                _dot(f.astype(BF16), wf_ref[g]).astype(o_ref.dtype))


def _dft2(yc, ys, cw3, sw3, a2, cs, wf, *, kb):
    t = yc.shape[0]
    n1 = t // CHUNK
    scale = float(1.0 / np.sqrt(float(t) * FCG))
    tok = pl.BlockSpec((kb * CHUNK, D_FOURIER), lambda i: (i, 0))
    tw = pl.BlockSpec((kb, CHUNK, 1), lambda i: (i, 0, 0))
    full = lambda a: pl.BlockSpec(a.shape, lambda i: (0,) * a.ndim)
    return pl.pallas_call(
        functools.partial(_dft2_kernel, kb=kb, scale=scale),
        grid=(n1 // kb,),
        in_specs=[tok, tok, tw, tw, full(a2), full(cs), full(wf)],
        out_specs=pl.BlockSpec((CHUNK, kb * D_FOURIER), lambda i: (0, i)),
        out_shape=jax.ShapeDtypeStruct((CHUNK, n1 * D_FOURIER), BF16),
        compiler_params=_cparams("parallel"),
        name="dft2",
    )(yc, ys, cw3, sw3, a2, cs, wf)


def _merge_kernel(hf_ref, hb_ref, act_ref, z_ref, yf_ref, x_ref, er_ref, ec_ref,
                  nw_ref, sk_ref, wout_ref, gg1_ref, g2_ref, sh2_ref, wr_ref, br_ref,
                  x1_ref, h2_ref, lg_ref):
    h = hf_ref[...].astype(F32) + hb_ref[...].astype(F32)
    parts = []
    for hd in range(HEADS):
        hh = h[:, hd * DH:(hd + 1) * DH]
        dl = hh - jnp.mean(hh, axis=-1, keepdims=True)
        var = jnp.mean(dl * dl, axis=-1, keepdims=True)
        parts.append(dl * lax.rsqrt(var + EPS))
    hn = jnp.concatenate(parts, axis=-1)
    z = z_ref[...].astype(F32)
    m = (hn * nw_ref[...] + sk_ref[...] * act_ref[...].astype(F32)) * (z * _sigmoid(z))
    cat = jnp.concatenate([m.astype(BF16), yf_ref[...]], axis=-1)
    y = _dot(cat, wout_ref[...])
    x3 = _add_pos(x_ref[...], er_ref, ec_ref)
    xp = x3.reshape(x3.shape[0] * GRID_W, D_MODEL)
    x1 = xp + _rms(y) * gg1_ref[...]
    x1_ref[...] = x1
    h2 = _rms(x1) * g2_ref[...] + sh2_ref[...]
    h2_ref[...] = h2.astype(BF16)
    lg = _dot3(h2, wr_ref[...]) + br_ref[...]
    lg_ref[...] = lg.T[:32]


def _merge(hf, hb, act, z, yf, x3, er3, ec3, nw, sk, wout, gg1, g2, sh2, wr, br, *, rows):
    nr = x3.shape[0]
    t = nr * GRID_W
    tm = rows * GRID_W
    tok = pl.BlockSpec((tm, D_MLSTM), lambda i: (i, 0))
    full = lambda a: pl.BlockSpec(a.shape, lambda i: (0,) * a.ndim)
    return pl.pallas_call(
        _merge_kernel,
        grid=(nr // rows,),
        in_specs=[tok, tok, tok, tok, tok,
                  pl.BlockSpec((rows, GRID_W, D_MODEL), lambda i: (i, 0, 0)),
                  pl.BlockSpec((rows, 1, D_MODEL // 2), lambda i: (i, 0, 0)),
                  pl.BlockSpec((1, GRID_W, D_MODEL // 2), lambda i: (0, 0, 0)),
                  full(nw), full(sk), full(wout), full(gg1), full(g2), full(sh2), full(wr), full(br)],
        out_specs=[pl.BlockSpec((tm, D_MODEL), lambda i: (i, 0)),
                   pl.BlockSpec((tm, D_MODEL), lambda i: (i, 0)),
                   pl.BlockSpec((32, tm), lambda i: (0, i))],
        out_shape=[jax.ShapeDtypeStruct((t, D_MODEL), F32),
                   jax.ShapeDtypeStruct((t, D_MODEL), BF16),
                   jax.ShapeDtypeStruct((32, t), F32)],
        compiler_params=_cparams("parallel"),
        name="merge",
    )(hf, hb, act, z, yf, x3, er3, ec3, nw, sk, wout, gg1, g2, sh2, wr, br)


def _route_kernel(lg_ref, pos_ref, w_ref, qrow_ref, qcol_ref, cnt_ref, *, sbk, r):
    lg = lg_ref[...]
    tl = lg.shape[1]
    g = [lg[j:j + 1] for j in range(N_GROUPS)]
    e = [lg[N_GROUPS + j:N_GROUPS + j + 1] for j in range(N_EXPERTS)]
    gmax = jnp.maximum(jnp.maximum(g[0], g[1]), jnp.maximum(g[2], g[3]))
    den = jnp.exp(g[0] - gmax) + jnp.exp(g[1] - gmax) + jnp.exp(g[2] - gmax) + jnp.exp(g[3] - gmax)
    p_sel = 1.0 / den
    sel = []
    free = jnp.ones((1, tl), F32)
    for j in range(N_GROUPS):
        s = jnp.where(g[j] >= gmax, free, 0.0)
        sel.append(s)
        free = free - s
    es = []
    for j in range(EPG):
        es.append(sel[0] * e[j] + sel[1] * e[EPG + j] + sel[2] * e[2 * EPG + j] + sel[3] * e[3 * EPG + j])
    rank = []
    for j in range(EPG):
        rj = jnp.zeros((1, tl), F32)
        for i in range(EPG):
            if i == j:
                continue
            beats = (es[i] >= es[j]) if i < j else (es[i] > es[j])
            rj = rj + jnp.where(beats, 1.0, 0.0)
        rank.append(rj)
    v1 = jnp.maximum(jnp.maximum(es[0], es[1]), jnp.maximum(es[2], es[3]))
    v2 = sum(jnp.where(rank[j] == 1.0, es[j], 0.0) for j in range(EPG))
    tt = jnp.exp(v2 - v1)
    w1 = p_sel / (1.0 + tt)
    w2 = w1 * tt
    w = [jnp.where(rank[j] == 0.0, w1, jnp.where(rank[j] == 1.0, w2, 0.0)) for j in range(EPG)]
    top2 = [jnp.where(rank[j] < 2.0, 1.0, 0.0) for j in range(EPG)]
    mem = jnp.concatenate([sel[gi] * top2[j] for gi in range(N_GROUPS) for j in range(EPG)], axis=0)
    wts = jnp.concatenate([sel[gi] * w[j] for gi in range(N_GROUPS) for j in range(EPG)], axis=0)
    w_ref[...] = wts
    lane = lax.broadcasted_iota(jnp.int32, (N_EXPERTS, tl), 1) & (sbk - 1)
    c = mem
    k = 1
    while k < sbk:
        c = c + jnp.where(lane >= k, pltpu.roll(c, k, 1), 0.0)
        k *= 2
    rnk = c - 1.0
    pos_ref[...] = jnp.where(mem > 0.0, rnk, -1.0)
    q0 = jnp.full((1, tl), -1.0, F32)
    q1 = jnp.full((1, tl), -1.0, F32)
    w0 = jnp.zeros((1, tl), F32)
    w1 = jnp.zeros((1, tl), F32)
    seen = jnp.zeros((1, tl), F32)
    for ex in range(N_EXPERTS):
        m = mem[ex:ex + 1]
        rk = rnk[ex:ex + 1]
        val = jnp.where(rk < r, rk + float(ex * r), -1.0)
        first = (m * (1.0 - seen)) > 0.0
        second = (m * seen) > 0.0
        q0 = jnp.where(first, val, q0)
        w0 = jnp.where(first, wts[ex:ex + 1], w0)
        q1 = jnp.where(second, val, q1)
        w1 = jnp.where(second, wts[ex:ex + 1], w1)
        seen = seen + m
    qrow_ref[...] = jnp.concatenate([q0, q1, jnp.zeros((6, tl), F32)], axis=0)
    qcol_ref[...] = jnp.concatenate([q0, q1, w0, w1, jnp.zeros((LANES - 4, tl), F32)], axis=0).T
    lane128 = lax.broadcasted_iota(jnp.int32, (N_EXPERTS, LANES), 1)
    cnt = jnp.zeros((N_EXPERTS, LANES), F32)
    for kb in range(tl // sbk):
        tot = jnp.sum(mem[:, kb * sbk:(kb + 1) * sbk], axis=1, keepdims=True)
        cnt = cnt + jnp.where(lane128 == kb, tot, 0.0)
    cnt_ref[...] = cnt


def _route(lg, *, tl, sbk, r):
    t = lg.shape[1]
    row = pl.BlockSpec((N_EXPERTS, tl), lambda i: (0, i))
    return pl.pallas_call(
        functools.partial(_route_kernel, sbk=sbk, r=r),
        grid=(t // tl,),
        in_specs=[pl.BlockSpec((32, tl), lambda i: (0, i))],
        out_specs=[row, row,
                   pl.BlockSpec((8, tl), lambda i: (0, i)),
                   pl.BlockSpec((tl, LANES), lambda i: (i, 0)),
                   pl.BlockSpec((N_EXPERTS, LANES), lambda i: (i, 0))],
        out_shape=[jax.ShapeDtypeStruct((N_EXPERTS, t), F32),
                   jax.ShapeDtypeStruct((N_EXPERTS, t), F32),
                   jax.ShapeDtypeStruct((8, t), F32),
                   jax.ShapeDtypeStruct((t, LANES), F32),
                   jax.ShapeDtypeStruct((t // tl * N_EXPERTS, LANES), F32)],
        compiler_params=_cparams("parallel"),
        name="route",
    )(lg)


def _mlp(x, wg, wu, wd):
    gt = _dot(x, wg)
    a = ((gt * _sigmoid(gt)) * _dot(x, wu)).astype(BF16)
    return _dot(a, wd).astype(BF16)


def _moe_kernel(cnt_ref, h_ref, qrow_ref, pos_ref, w_ref, wg_ref, wu_ref, wd_ref,
                qcol_ref, x1_ref, gg2_ref, o_ref, xs_ref, ovf_ref, *, sbk, r, eps, csb):
    i = pl.program_id(0)
    step = pl.program_id(1)
    nsb = h_ref.shape[0] // sbk
    exp_steps = N_EXPERTS // eps
    slots = N_EXPERTS * r

    @pl.when(step == 0)
    def _():
        ovf_ref[...] = jnp.zeros(ovf_ref.shape, F32)
        pid = lax.broadcasted_iota(jnp.int32, (slots, sbk), 0).astype(F32)

        def select(sb, carry):
            q0 = qrow_ref[0, pl.ds(sb, 1), :]
            q1 = qrow_ref[1, pl.ds(sb, 1), :]
            s = jnp.where(q0 == pid, 1.0, jnp.where(q1 == pid, 1.0, 0.0)).astype(BF16)
            row0 = pl.multiple_of(sb * sbk, sbk)
            xs_ref[sb] = _dot(s, h_ref[pl.ds(row0, sbk), :]).astype(BF16)
            return carry

        lax.fori_loop(0, nsb, select, 0)

    @pl.when(step < exp_steps)
    def _():
        def expert(k, carry):
            e = step * eps + k
            off = pl.multiple_of(e * r, 16)
            x = jnp.concatenate([xs_ref[sb, pl.ds(off, r), :] for sb in range(nsb)], axis=0)
            y = _mlp(x, wg_ref[k], wu_ref[k], wd_ref[k])
            for sb in range(nsb):
                xs_ref[sb, pl.ds(off, r), :] = y[sb * r:(sb + 1) * r]
            return carry

        lax.fori_loop(0, eps, expert, 0)

    n_pairs = (pl.num_programs(0) * nsb) * N_EXPERTS

    over = cnt_ref[n_pairs + i * exp_steps + jnp.minimum(step, exp_steps - 1)]

    @pl.when(jnp.logical_and(step < exp_steps, over > 0))
    def _():
        rid = lax.broadcasted_iota(jnp.int32, (LANES, sbk), 0).astype(F32)

        def expert(k, carry):
            e = step * eps + k
            wg, wu, wd = wg_ref[k], wu_ref[k], wd_ref[k]

            def sub_block(sb, c1):
                n_over = jnp.maximum(cnt_ref[(i * nsb + sb) * N_EXPERTS + e] - r, 0)
                row0 = pl.multiple_of(sb * sbk, sbk)

                def one_pass(s, c2):
                    base = (r + s * LANES).astype(F32)
                    hit = pos_ref[pl.ds(e, 1), pl.ds(sb, 1), :].reshape(1, sbk) == (rid + base)
                    sel = jnp.where(hit, 1.0, 0.0).astype(BF16)
                    yo = _mlp(_dot(sel, h_ref[pl.ds(row0, sbk), :]).astype(BF16), wg, wu, wd)
                    wct = jnp.where(hit, w_ref[pl.ds(e, 1), pl.ds(sb, 1), :].reshape(1, sbk), 0.0).T
                    ovf_ref[pl.ds(row0, sbk), :] += _dot(wct.astype(BF16), yo)
                    return c2

                return lax.fori_loop(0, (n_over + LANES - 1) // LANES, one_pass, c1)

            return lax.fori_loop(0, nsb, sub_block, carry)

        lax.fori_loop(0, eps, expert, 0)

    @pl.when(step >= exp_steps)
    def _():
        lane = lax.broadcasted_iota(jnp.int32, (sbk, slots), 1).astype(F32)
        for k in range(csb):
            sb = (step - exp_steps) * csb + k
            rows = slice(k * sbk, (k + 1) * sbk)
            qc = qcol_ref[rows, :]
            wmat = (jnp.where(lane == qc[:, 0:1], qc[:, 2:3], 0.0)
                    + jnp.where(lane == qc[:, 1:2], qc[:, 3:4], 0.0)).astype(BF16)
            y = _dot(wmat, xs_ref[sb]) + ovf_ref[pl.ds(pl.multiple_of(sb * sbk, sbk), sbk), :]
            o_ref[rows, :] = x1_ref[rows, :] + _rms(y) * gg2_ref[...]


def _moe(cnt, h2, qrow, pos3, w3, wg, wu, wd, qcol, x1, gg2, *, tb, sbk, r, eps, csb):
    t = h2.shape[0]
    nsb = tb // sbk
    exp_steps = N_EXPERTS // eps
    comb_steps = nsb // csb
    wblk = lambda i, s, c: (jnp.minimum(s, exp_steps - 1), 0, 0)
    oblk = lambda i, s, c: (i * comb_steps + jnp.maximum(s - exp_steps, 0), 0)
    r3 = pl.BlockSpec((N_EXPERTS, nsb, sbk), lambda i, s, c: (0, i, 0))
    return pl.pallas_call(
        functools.partial(_moe_kernel, sbk=sbk, r=r, eps=eps, csb=csb),
        grid_spec=pltpu.PrefetchScalarGridSpec(
            num_scalar_prefetch=1,
            grid=(t // tb, exp_steps + comb_steps),
            in_specs=[pl.BlockSpec((tb, D_MODEL), lambda i, s, c: (i, 0)),
                      pl.BlockSpec((8, nsb, sbk), lambda i, s, c: (0, i, 0)),
                      r3, r3,
                      pl.BlockSpec((eps, D_MODEL, D_EXPERT), wblk),
                      pl.BlockSpec((eps, D_MODEL, D_EXPERT), wblk),
                      pl.BlockSpec((eps, D_EXPERT, D_MODEL), wblk),
                      pl.BlockSpec((csb * sbk, LANES), oblk),
                      pl.BlockSpec((csb * sbk, D_MODEL), oblk),
                      pl.BlockSpec((1, D_MODEL), lambda i, s, c: (0, 0))],
            out_specs=pl.BlockSpec((csb * sbk, D_MODEL), oblk),
            scratch_shapes=[pltpu.VMEM((nsb, N_EXPERTS * r, D_MODEL), BF16),
                            pltpu.VMEM((tb, D_MODEL), F32)]),
        out_shape=jax.ShapeDtypeStruct((t, D_MODEL), F32),
        compiler_params=pltpu.CompilerParams(dimension_semantics=("parallel", "arbitrary"),
                                             vmem_limit_bytes=MOE_VMEM_LIMIT),
        name="moe",
    )(cnt, h2, qrow, pos3, w3, wg, wu, wd, qcol, x1, gg2)


def _pos_tables(rows):
    quarter = D_MODEL // 4
    freq = 1.0 / (POS_BASE ** (np.arange(quarter, dtype=np.float64) / quarter))
    r = np.arange(rows, dtype=np.float64)[:, None] * freq
    cl = np.arange(GRID_W, dtype=np.float64)[:, None] * freq
    er = np.concatenate([np.sin(r), np.cos(r)], axis=-1).astype(np.float32)
    ec = np.concatenate([np.sin(cl), np.cos(cl)], axis=-1).astype(np.float32)
    return jnp.asarray(er[:, None, :]), jnp.asarray(ec[None, :, :])


def _dft_tables(t):
    n = np.arange(CHUNK, dtype=np.int64)
    prod = n[:, None] * n[None, :]
    ang = (prod % CHUNK).astype(np.float64) * (2.0 * np.pi / CHUNK)
    c, s = np.cos(ang), np.sin(ang)
    f1 = np.concatenate([c, s], axis=0)
    a2 = np.concatenate([np.concatenate([c, -s], axis=1), np.concatenate([s, c], axis=1)], axis=0)
    cs = np.concatenate([c, -s], axis=0)
    angw = prod.astype(np.float64) * (2.0 * np.pi / t)
    f32 = lambda a: jnp.asarray(a.astype(np.float32))
    return f32(f1), f32(a2), f32(cs), f32(np.cos(angw)[:, :, None]), f32(np.sin(angw)[:, :, None])


def _blockdiag(w):
    n = w.shape[0]
    size = n * QKV_BLOCK
    spread = np.tile(np.eye(QKV_BLOCK, dtype=np.float32), (1, n))
    rows = jnp.dot(w.reshape(size, QKV_BLOCK), jnp.asarray(spread), precision=lax.Precision.HIGHEST)
    blk = np.arange(size) // QKV_BLOCK
    mask = (blk[:, None] == blk[None, :]).astype(np.float32)
    return rows * jnp.asarray(mask)


def _gate_weights(w_f, b_f, w_b, b_b):
    w = jnp.concatenate([w_f[:, :HEADS], w_b[:, :HEADS], w_f[:, HEADS:], w_b[:, HEADS:]], axis=1).T
    b = jnp.concatenate([b_f[:HEADS], b_b[:HEADS], b_f[HEADS:], b_b[HEADS:]])
    return w.astype(BF16), b[:, None]


def kernel(x, c, ctx, c_ctx, w_ada, b_ada, g_pre_mix, g_post_mix, g_pre_ffn, g_post_ffn,
           w_in, conv_w, conv_b, w_q, w_k, w_v, w_if_fwd, b_if_fwd, w_if_bwd, b_if_bwd,
           mlstm_norm_w, mlstm_skip, w_fourier, w_out, w_router_group, b_router_group,
           w_router_expert, b_router_expert, w_gate, w_up, w_down):
    t = x.shape[1]
    rows = t // GRID_W

    c8 = jnp.concatenate([c, c_ctx[None, :], jnp.zeros((6, D_MODEL), F32)], axis=0)
    mod = _ada(c8, w_ada[0], b_ada[0][None, :])
    shift1, scale1, gate1, shift2, scale2, gate2 = [mod[0:1, k * D_MODEL:(k + 1) * D_MODEL] for k in range(N_MOD)]
    shift1c, scale1c = mod[1:2, 0:D_MODEL], mod[1:2, D_MODEL:2 * D_MODEL]
    g1 = g_pre_mix[0][None, :] * (1.0 + scale1)
    g1c = g_pre_mix[0][None, :] * (1.0 + scale1c)
    gg1 = g_post_mix[0][None, :] * gate1
    g2 = g_pre_ffn[0][None, :] * (1.0 + scale2)
    gg2 = g_post_ffn[0][None, :] * gate2

    er3, ec3 = _pos_tables(rows)
    x3 = x.reshape(rows, GRID_W, D_MODEL)
    ctx3 = ctx.reshape(CTX_LEN // GRID_W, GRID_W, D_MODEL)
    w_in_bf = w_in[0].astype(BF16)

    xm_l, z_l, u_l = _inproj(x3, er3, ec3, g1, shift1, w_in_bf, rows=8, add_pos=True)
    xm_c, _, _ = _inproj(ctx3, er3, ec3, g1c, shift1c, w_in_bf, rows=CTX_LEN // GRID_W, add_pos=False)

    wq = _blockdiag(w_q[0]).astype(BF16)
    wkt = _blockdiag(w_k[0]).T.astype(BF16)
    wv = _blockdiag(w_v[0]).astype(BF16)
    wi, bi = _gate_weights(w_if_fwd[0], b_if_fwd[0], w_if_bwd[0], b_if_bwd[0])
    wiq, wik, wiv = wi[:, :D_MLSTM], wi[:, D_MLSTM:2 * D_MLSTM], wi[:, 2 * D_MLSTM:]
    cb = conv_b[0][None, :]
    q_l, kt_l, v_l, act_l, gp_l = _feat(xm_l, conv_w[0], cb, wq, wkt, wv, wiq, wik, wiv, bi, tm=512)
    q_c, kt_c, v_c, _, gp_c = _feat(xm_c, conv_w[0], cb, wq, wkt, wv, wiq, wik, wiv, bi, tm=CTX_LEN)

    dr_l, gc_l = _gates(gp_l, tl=2048)
    dr_c, gc_c = _gates(gp_c, tl=CTX_LEN)

    c0 = jnp.zeros((2 * HEADS, DH, 2 * DH), F32)
    m0 = jnp.zeros((2 * HEADS, 8, LANES), F32)
    c_ctx_fin, m_ctx_fin = _mlstm(q_c, kt_c, v_c, gc_c, dr_c, c0, m0, cps=CTX_LEN // CHUNK, emit=False)
    hf, hb, _, _ = _mlstm(q_l, kt_l, v_l, gc_l, dr_l, c_ctx_fin, m_ctx_fin, cps=4, emit=True)

    f1, a2, cs, cw3, sw3 = _dft_tables(t)
    yc, ys = _dft1(u_l.reshape(CHUNK, (t // CHUNK) * D_FOURIER), f1.astype(BF16), tn=2048)
    yf = _dft2(yc.reshape(t, D_FOURIER), ys.reshape(t, D_FOURIER), cw3, sw3,
               a2.astype(BF16), cs.astype(BF16), w_fourier[0].astype(BF16), kb=4)
    yf = yf.reshape(t, D_FOURIER)

    wr = jnp.concatenate([w_router_group[0], w_router_expert[0],
                          jnp.zeros((D_MODEL, LANES - N_GROUPS - N_EXPERTS), F32)], axis=1)
    br = jnp.concatenate([b_router_group[0], b_router_expert[0],
                          jnp.zeros((LANES - N_GROUPS - N_EXPERTS,), F32)])[None, :]
    x1, h2, lg = _merge(hf, hb, act_l, z_l, yf, x3, er3, ec3,
                        mlstm_norm_w[0][None, :], mlstm_skip[0][None, :], w_out[0].astype(BF16),
                        gg1, g2, shift2, wr, br, rows=8)
    pos, wts, qrow, qcol, cnt = _route(lg, tl=MOE_TB, sbk=MOE_SBK, r=MOE_R)
    nsb = MOE_TB // MOE_SBK
    cnt = cnt.reshape(t // MOE_TB, N_EXPERTS, LANES)[:, :, :nsb]
    cnt = jnp.transpose(cnt, (0, 2, 1)).astype(jnp.int32)
    over = jnp.maximum(cnt - MOE_R, 0).reshape(t // MOE_TB, nsb, N_EXPERTS // MOE_EPS, MOE_EPS)
    cnt = jnp.concatenate([cnt.reshape(-1), jnp.max(over, axis=(1, 3)).reshape(-1)])
    pos3 = pos.reshape(N_EXPERTS, t // MOE_SBK, MOE_SBK)
    wts3 = wts.reshape(N_EXPERTS, t // MOE_SBK, MOE_SBK)
    out = _moe(cnt, h2, qrow.reshape(8, t // MOE_SBK, MOE_SBK), pos3, wts3,
               w_gate[0].astype(BF16), w_up[0].astype(BF16), w_down[0].astype(BF16),
               qcol, x1, gg2, tb=MOE_TB, sbk=MOE_SBK, r=MOE_R, eps=MOE_EPS, csb=MOE_CSB)
    return out[None]
```

```python
import functools

import numpy as np
import jax
import jax.numpy as jnp
from jax import lax
from jax.experimental import pallas as pl
from jax.experimental.pallas import tpu as pltpu

F32 = jnp.float32
BF16 = jnp.bfloat16

D_MODEL = 1024
SEQ = 16384
GRID_W = 64
CTX_LEN = 256
D_MLSTM = 512
HEADS = 4
DH = 128
QKV_BLOCK = 4
CONV_K = 3
CHUNK = 128
D_FOURIER = 512
FGROUPS = 4
FCG = 128
N_GROUPS = 4
EPG = 4
N_EXPERTS = 16
D_EXPERT = 512
N_MOD = 6
EPS = 1e-6
POS_BASE = 10000.0
LANES = 128
NEG_BIG = -3.0e38

VMEM_LIMIT = 52 * 1024 * 1024
MOE_VMEM_LIMIT = 56 * 1024 * 1024
MOE_TB = 2048
MOE_SBK = 256
MOE_R = 48
MOE_EPS = 2
MOE_CSB = 1


def _cparams(*sem):
    return pltpu.CompilerParams(dimension_semantics=sem, vmem_limit_bytes=VMEM_LIMIT)


def _dot(a, b):
    return jnp.dot(a, b, preferred_element_type=F32)


def _dot_nt(a, b):
    return lax.dot_general(a, b, (((1,), (1,)), ((), ())), preferred_element_type=F32)


def _split_bf16(a):
    hi = a.astype(BF16)
    lo = (a - hi.astype(F32)).astype(BF16)
    return hi, lo


def _dot3(a, b):
    a_hi, a_lo = _split_bf16(a)
    b_hi, b_lo = _split_bf16(b)
    return _dot(a_hi, b_hi) + (_dot(a_hi, b_lo) + _dot(a_lo, b_hi))


def _sigmoid(x):
    return 1.0 / (1.0 + jnp.exp(-x))


def _rms(x):
    return x * lax.rsqrt(jnp.mean(x * x, axis=-1, keepdims=True) + EPS)


def _ada_kernel(c_ref, w_ref, b_ref, o_ref):
    c = c_ref[...]
    s = c * _sigmoid(c)
    o_ref[...] = _dot3(s, w_ref[...]) + b_ref[...]


def _ada(c8, w, b):
    n = w.shape[1]
    tn = 768
    return pl.pallas_call(
        _ada_kernel,
        grid=(n // tn,),
        in_specs=[pl.BlockSpec((8, D_MODEL), lambda j: (0, 0)),
                  pl.BlockSpec((D_MODEL, tn), lambda j: (0, j)),
                  pl.BlockSpec((1, tn), lambda j: (0, j))],
        out_specs=pl.BlockSpec((8, tn), lambda j: (0, j)),
        out_shape=jax.ShapeDtypeStruct((8, n), F32),
        compiler_params=_cparams("parallel"),
        name="ada",
    )(c8, w, b)


def _add_pos(x3, er_ref, ec_ref):
    r = x3.shape[0]
    pr = jnp.broadcast_to(er_ref[...], (r, GRID_W, D_MODEL // 2))
    pc = jnp.broadcast_to(ec_ref[...], (r, GRID_W, D_MODEL // 2))
    return x3 + jnp.concatenate([pr, pc], axis=-1)


def _inproj_kernel(x_ref, er_ref, ec_ref, g_ref, sh_ref, w_ref, xm_ref, z_ref, u_ref, *, add_pos):
    x3 = x_ref[...]
    if add_pos:
        x3 = _add_pos(x3, er_ref, ec_ref)
    x = x3.reshape(x3.shape[0] * GRID_W, D_MODEL)
    h = _rms(x) * g_ref[...] + sh_ref[...]
    proj = _dot(h.astype(BF16), w_ref[...])
    xm_ref[...] = proj[:, :D_MLSTM].astype(BF16)
    z_ref[...] = proj[:, D_MLSTM:2 * D_MLSTM].astype(BF16)
    u_ref[...] = proj[:, 2 * D_MLSTM:].astype(BF16)


def _inproj(x3, er3, ec3, g_eff, shift, w_in, *, rows, add_pos):
    nr = x3.shape[0]
    t = nr * GRID_W
    tm = rows * GRID_W
    out = jax.ShapeDtypeStruct((t, D_MLSTM), BF16)
    ospec = pl.BlockSpec((tm, D_MLSTM), lambda i: (i, 0))
    vec = pl.BlockSpec((1, D_MODEL), lambda i: (0, 0))
    return pl.pallas_call(
        functools.partial(_inproj_kernel, add_pos=add_pos),
        grid=(nr // rows,),
        in_specs=[pl.BlockSpec((rows, GRID_W, D_MODEL), lambda i: (i, 0, 0)),
                  pl.BlockSpec((rows, 1, D_MODEL // 2), lambda i: (i, 0, 0)),
                  pl.BlockSpec((1, GRID_W, D_MODEL // 2), lambda i: (0, 0, 0)),
                  vec, vec,
                  pl.BlockSpec(w_in.shape, lambda i: (0, 0))],
        out_specs=[ospec, ospec, ospec],
        out_shape=[out, out, out],
        compiler_params=_cparams("parallel"),
        name="inproj",
    )(x3, er3, ec3, g_eff, shift, w_in)


def _feat_kernel(xm_ref, prev_ref, next_ref, cw_ref, cb_ref, wq_ref, wkt_ref, wv_ref,
                 wiq_ref, wik_ref, wiv_ref, bi_ref,
                 q_ref, kt_ref, v_ref, act_ref, g_ref):
    i = pl.program_id(0)
    n = pl.num_programs(0)
    xm_bf = xm_ref[...]
    xm = xm_bf.astype(F32)
    tm = xm.shape[0]
    prev_row = prev_ref[...].astype(F32)[15:16, :] * jnp.where(i > 0, 1.0, 0.0)
    next_row = next_ref[...].astype(F32)[0:1, :] * jnp.where(i < n - 1, 1.0, 0.0)
    rid = lax.broadcasted_iota(jnp.int32, (tm, 1), 0)
    x_left = jnp.where(rid == 0, prev_row, pltpu.roll(xm, 1, 0))
    x_right = jnp.where(rid == tm - 1, next_row, pltpu.roll(xm, tm - 1, 0))
    cw = cw_ref[...]
    y = cw[0:1] * x_left + cw[1:2] * xm + cw[2:3] * x_right + cb_ref[...]
    act = (y * _sigmoid(y)).astype(BF16)
    act_ref[...] = act
    q = _dot(act, wq_ref[...])
    kt = _dot_nt(wkt_ref[...], act)
    v = _dot(xm_bf, wv_ref[...])
    q_bf = q.astype(BF16)
    kt_bf = kt.astype(BF16)
    v_bf = v.astype(BF16)
    q_ref[...] = (q * (DH ** -0.5)).astype(BF16)
    kt_ref[...] = kt_bf
    v_ref[...] = v_bf
    g = _dot_nt(wiq_ref[...], q_bf) + _dot(wik_ref[...], kt_bf) + _dot_nt(wiv_ref[...], v_bf)
    g_ref[...] = g + bi_ref[...]


def _feat(xm, conv_w, conv_b, wq, wkt, wv, wiq, wik, wiv, bi, *, tm):
    t = xm.shape[0]
    nb16 = t // 16
    k16 = tm // 16
    full = lambda a: pl.BlockSpec(a.shape, lambda i: (0,) * a.ndim)
    tok = pl.BlockSpec((tm, D_MLSTM), lambda i: (i, 0))
    return pl.pallas_call(
        _feat_kernel,
        grid=(t // tm,),
        in_specs=[tok,
                  pl.BlockSpec((16, D_MLSTM), lambda i: (jnp.maximum(i * k16 - 1, 0), 0)),
                  pl.BlockSpec((16, D_MLSTM), lambda i: (jnp.minimum((i + 1) * k16, nb16 - 1), 0)),
                  full(conv_w), full(conv_b), full(wq), full(wkt), full(wv),
                  full(wiq), full(wik), full(wiv), full(bi)],
        out_specs=[tok,
                   pl.BlockSpec((D_MLSTM, tm), lambda i: (0, i)),
                   tok, tok,
                   pl.BlockSpec((16, tm), lambda i: (0, i))],
        out_shape=[jax.ShapeDtypeStruct((t, D_MLSTM), BF16),
                   jax.ShapeDtypeStruct((D_MLSTM, t), BF16),
                   jax.ShapeDtypeStruct((t, D_MLSTM), BF16),
                   jax.ShapeDtypeStruct((t, D_MLSTM), BF16),
                   jax.ShapeDtypeStruct((16, t), F32)],
        compiler_params=_cparams("parallel"),
        name="feat",
    )(xm, xm, xm, conv_w, conv_b, wq, wkt, wv, wiq, wik, wiv, bi)


def _gates_kernel(g_ref, d_ref, gc_ref):
    g = g_ref[...]
    tl = g.shape[1]
    ig = g[0:8]
    fg = g[8:16]
    lf = jnp.minimum(fg, 0.0) - jnp.log(1.0 + jnp.exp(-jnp.abs(fg)))
    pos = lax.broadcasted_iota(jnp.int32, (8, tl), 1) & (CHUNK - 1)
    is_fwd = lax.broadcasted_iota(jnp.int32, (8, tl), 0) < HEADS

    def scan(x, op, ident):
        xf = x
        xb = x
        k = 1
        while k < CHUNK:
            xf = op(xf, jnp.where(pos >= k, pltpu.roll(xf, k, 1), ident))
            xb = op(xb, jnp.where(pos < CHUNK - k, pltpu.roll(xb, tl - k, 1), ident))
            k *= 2
        return jnp.where(is_fwd, xf, xb)

    b = scan(lf, jnp.add, 0.0)
    d = ig - b
    mloc = b + scan(d, jnp.maximum, NEG_BIG)
    d_ref[...] = d
    stack = jnp.concatenate([b, mloc, jnp.zeros((LANES - 16, tl), F32)], axis=0)
    gc_ref[...] = stack.T


def _gates(g, *, tl):
    t = g.shape[1]
    return pl.pallas_call(
        _gates_kernel,
        grid=(t // tl,),
        in_specs=[pl.BlockSpec((16, tl), lambda i: (0, i))],
        out_specs=[pl.BlockSpec((8, tl), lambda i: (0, i)),
                   pl.BlockSpec((tl, LANES), lambda i: (i, 0))],
        out_shape=[jax.ShapeDtypeStruct((8, t), F32),
                   jax.ShapeDtypeStruct((t, LANES), F32)],
        compiler_params=_cparams("parallel"),
        name="gates",
    )(g)


def _mlstm_kernel(*refs, cps, emit):
    (qf_ref, kf_ref, vf_ref, gcf_ref, drf_ref,
     qb_ref, kb_ref, vb_ref, gcb_ref, drb_ref, c0_ref, m0_ref) = refs[:12]
    if emit:
        hf_ref, hb_ref, cfin_ref, mfin_ref, c_scr, m_scr = refs[12:]
    else:
        cfin_ref, mfin_ref, c_scr, m_scr = refs[12:]
        hf_ref = hb_ref = None
    step = pl.program_id(0)

    @pl.when(step == 0)
    def _():
        c_scr[...] = c0_ref[...]
        m_scr[...] = m0_ref[...]

    ti = lax.broadcasted_iota(jnp.int32, (CHUNK, CHUNK), 0)
    si = lax.broadcasted_iota(jnp.int32, (CHUNK, CHUNK), 1)
    ones_col = jnp.where(si == 0, 1.0, 0.0).astype(BF16)

    for j in range(cps):
        for d in range(2):
            if d == 0:
                q_ref, k_ref, v_ref, gc_ref, dr_ref, h_ref = qf_ref, kf_ref, vf_ref, gcf_ref, drf_ref, hf_ref
                r0, mask, last = j * CHUNK, si <= ti, CHUNK - 1
            else:
                q_ref, k_ref, v_ref, gc_ref, dr_ref, h_ref = qb_ref, kb_ref, vb_ref, gcb_ref, drb_ref, hb_ref
                r0, mask, last = (cps - 1 - j) * CHUNK, si >= ti, 0
            rows = slice(r0, r0 + CHUNK)
            gc = gc_ref[rows, :]
            dr = dr_ref[:, rows]
            for hd in range(HEADS):
                r = d * HEADS + hd
                cols = slice(hd * DH, (hd + 1) * DH)
                b = gc[:, r:r + 1]
                mloc = gc[:, 8 + r:9 + r]
                drow = dr[r:r + 1, :]
                m0 = m_scr[r][0:1, 0:1]
                c_aug = c_scr[r]
                kt = k_ref[cols, rows]
                vaug = jnp.concatenate([v_ref[rows, cols], ones_col], axis=1)
                if emit:
                    qh = q_ref[rows, cols]
                    bm = b + m0
                    m_t = jnp.maximum(bm, mloc)
                    dmat = jnp.where(mask, jnp.exp((b - m_t) + drow), 0.0)
                    smat = (_dot(qh, kt) * dmat).astype(BF16)
                    num = _dot(smat, vaug) + jnp.exp(bm - m_t) * _dot(qh, c_aug.astype(BF16))
                    den = jnp.maximum(jnp.abs(num[:, DH:DH + 1]), jnp.exp(-m_t))
                    h_ref[rows, cols] = (num[:, :DH] / den).astype(h_ref.dtype)
                btot = gc[last:last + 1, r:r + 1]
                amax = gc[last:last + 1, 8 + r:9 + r]
                m_new = jnp.maximum(btot + m0, amax)
                decay = jnp.exp(btot + m0 - m_new)
                kw = (kt.astype(F32) * jnp.exp(btot + drow - m_new)).astype(BF16)
                c_scr[r] = decay * c_aug + _dot(kw, vaug)
                m_scr[r] = jnp.broadcast_to(m_new, (8, LANES))

    @pl.when(step == pl.num_programs(0) - 1)
    def _():
        cfin_ref[...] = c_scr[...]
        mfin_ref[...] = m_scr[...]


def _mlstm(q, kt, v, gc, dr, c0, m0, *, cps, emit):
    t = q.shape[0]
    cb = cps * CHUNK
    nb = t // cb
    fwd_r = lambda i: (i, 0)
    bwd_r = lambda i: (nb - 1 - i, 0)
    fwd_c = lambda i: (0, i)
    bwd_c = lambda i: (0, nb - 1 - i)
    tok = lambda f: pl.BlockSpec((cb, D_MLSTM), f)
    in_specs = []
    for fr, fc in ((fwd_r, fwd_c), (bwd_r, bwd_c)):
        in_specs += [tok(fr), pl.BlockSpec((D_MLSTM, cb), fc), tok(fr),
                     pl.BlockSpec((cb, LANES), fr), pl.BlockSpec((8, cb), fc)]
    cshape = (2 * HEADS, DH, 2 * DH)
    mshape = (2 * HEADS, 8, LANES)
    cspec = pl.BlockSpec(cshape, lambda i: (0, 0, 0))
    mspec = pl.BlockSpec(mshape, lambda i: (0, 0, 0))
    in_specs += [cspec, mspec]
    out_specs = [cspec, mspec]
    out_shape = [jax.ShapeDtypeStruct(cshape, F32), jax.ShapeDtypeStruct(mshape, F32)]
    if emit:
        out_specs = [tok(fwd_r), tok(bwd_r)] + out_specs
        out_shape = [jax.ShapeDtypeStruct((t, D_MLSTM), BF16)] * 2 + out_shape
    return pl.pallas_call(
        functools.partial(_mlstm_kernel, cps=cps, emit=emit),
        grid=(nb,),
        in_specs=in_specs,
        out_specs=out_specs,
        out_shape=out_shape,
        scratch_shapes=[pltpu.VMEM(cshape, F32), pltpu.VMEM(mshape, F32)],
        compiler_params=_cparams("arbitrary"),
        name="mlstm",
    )(q, kt, v, gc, dr, q, kt, v, gc, dr, c0, m0)


def _dft1_kernel(u_ref, f_ref, yc_ref, ys_ref):
    y = _dot(f_ref[...], u_ref[...])
    yc_ref[...] = y[:CHUNK].astype(BF16)
    ys_ref[...] = y[CHUNK:].astype(BF16)


def _dft1(u2d, f1, *, tn):
    n = u2d.shape[1]
    blk = pl.BlockSpec((CHUNK, tn), lambda j: (0, j))
    out = jax.ShapeDtypeStruct((CHUNK, n), BF16)
    return pl.pallas_call(
        _dft1_kernel,
        grid=(n // tn,),
        in_specs=[blk, pl.BlockSpec(f1.shape, lambda j: (0, 0))],
        out_specs=[blk, blk],
        out_shape=[out, out],
        compiler_params=_cparams("parallel"),
        name="dft1",
    )(u2d, f1)


def _dft2_kernel(yc_ref, ys_ref, cw_ref, sw_ref, a2_ref, cs_ref, wf_ref, o_ref, *, kb, scale):
    a2 = a2_ref[...]
    cs = cs_ref[...]
    for kk in range(kb):
        rows = slice(kk * CHUNK, (kk + 1) * CHUNK)
        yc = yc_ref[rows, :].astype(F32)
        ys = ys_ref[rows, :].astype(F32)
        cw = cw_ref[kk]
        sw = sw_ref[kk]
        p = jnp.concatenate([yc * cw - ys * sw, yc * sw + ys * cw], axis=0).astype(BF16)
        x = _dot(a2, p)
        for g in range(FGROUPS):
            cols = slice(g * FCG, (g + 1) * FCG)
            cat = jnp.concatenate([x[:CHUNK, cols], x[CHUNK:, cols]], axis=1).astype(BF16)
            f = _dot(cat, cs) * scale
            o_ref[:, kk * D_FOURIER + g * FCG:kk * D_FOURIER + (g + 1) * FCG] = (
                _dot(f.astype(BF16), wf_ref[g]).astype(o_ref.dtype))


def _dft2(yc, ys, cw3, sw3, a2, cs, wf, *, kb):
    t = yc.shape[0]
    n1 = t // CHUNK
    scale = float(1.0 / np.sqrt(float(t) * FCG))
    tok = pl.BlockSpec((kb * CHUNK, D_FOURIER), lambda i: (i, 0))
    tw = pl.BlockSpec((kb, CHUNK, 1), lambda i: (i, 0, 0))
    full = lambda a: pl.BlockSpec(a.shape, lambda i: (0,) * a.ndim)
    return pl.pallas_call(
        functools.partial(_dft2_kernel, kb=kb, scale=scale),
        grid=(n1 // kb,),
        in_specs=[tok, tok, tw, tw, full(a2), full(cs), full(wf)],
        out_specs=pl.BlockSpec((CHUNK, kb * D_FOURIER), lambda i: (0, i)),
        out_shape=jax.ShapeDtypeStruct((CHUNK, n1 * D_FOURIER), BF16),
        compiler_params=_cparams("parallel"),
        name="dft2",
    )(yc, ys, cw3, sw3, a2, cs, wf)


def _merge_kernel(hf_ref, hb_ref, act_ref, z_ref, yf_ref, x_ref, er_ref, ec_ref,
                  nw_ref, sk_ref, wout_ref, gg1_ref, g2_ref, sh2_ref, wr_ref, br_ref,
                  x1_ref, h2_ref, lg_ref):
    h = hf_ref[...].astype(F32) + hb_ref[...].astype(F32)
    parts = []
    for hd in range(HEADS):
        hh = h[:, hd * DH:(hd + 1) * DH]
        dl = hh - jnp.mean(hh, axis=-1, keepdims=True)
        var = jnp.mean(dl * dl, axis=-1, keepdims=True)
        parts.append(dl * lax.rsqrt(var + EPS))
    hn = jnp.concatenate(parts, axis=-1)
    z = z_ref[...].astype(F32)
    m = (hn * nw_ref[...] + sk_ref[...] * act_ref[...].astype(F32)) * (z * _sigmoid(z))
    cat = jnp.concatenate([m.astype(BF16), yf_ref[...]], axis=-1)
    y = _dot(cat, wout_ref[...])
    x3 = _add_pos(x_ref[...], er_ref, ec_ref)
    xp = x3.reshape(x3.shape[0] * GRID_W, D_MODEL)
    x1 = xp + _rms(y) * gg1_ref[...]
    x1_ref[...] = x1
    h2 = _rms(x1) * g2_ref[...] + sh2_ref[...]
    h2_ref[...] = h2.astype(BF16)
    lg = _dot3(h2, wr_ref[...]) + br_ref[...]
    lg_ref[...] = lg.T[:32]


def _merge(hf, hb, act, z, yf, x3, er3, ec3, nw, sk, wout, gg1, g2, sh2, wr, br, *, rows):
    nr = x3.shape[0]
    t = nr * GRID_W
    tm = rows * GRID_W
    tok = pl.BlockSpec((tm, D_MLSTM), lambda i: (i, 0))
    full = lambda a: pl.BlockSpec(a.shape, lambda i: (0,) * a.ndim)
    return pl.pallas_call(
        _merge_kernel,
        grid=(nr // rows,),
        in_specs=[tok, tok, tok, tok, tok,
                  pl.BlockSpec((rows, GRID_W, D_MODEL), lambda i: (i, 0, 0)),
                  pl.BlockSpec((rows, 1, D_MODEL // 2), lambda i: (i, 0, 0)),
                  pl.BlockSpec((1, GRID_W, D_MODEL // 2), lambda i: (0, 0, 0)),
                  full(nw), full(sk), full(wout), full(gg1), full(g2), full(sh2), full(wr), full(br)],
        out_specs=[pl.BlockSpec((tm, D_MODEL), lambda i: (i, 0)),
                   pl.BlockSpec((tm, D_MODEL), lambda i: (i, 0)),
                   pl.BlockSpec((32, tm), lambda i: (0, i))],
        out_shape=[jax.ShapeDtypeStruct((t, D_MODEL), F32),
                   jax.ShapeDtypeStruct((t, D_MODEL), BF16),
                   jax.ShapeDtypeStruct((32, t), F32)],
        compiler_params=_cparams("parallel"),
        name="merge",
    )(hf, hb, act, z, yf, x3, er3, ec3, nw, sk, wout, gg1, g2, sh2, wr, br)


def _route_kernel(lg_ref, pos_ref, w_ref, qrow_ref, qcol_ref, cnt_ref, *, sbk, r):
    lg = lg_ref[...]
    tl = lg.shape[1]
    g = [lg[j:j + 1] for j in range(N_GROUPS)]
    e = [lg[N_GROUPS + j:N_GROUPS + j + 1] for j in range(N_EXPERTS)]
    gmax = jnp.maximum(jnp.maximum(g[0], g[1]), jnp.maximum(g[2], g[3]))
    den = jnp.exp(g[0] - gmax) + jnp.exp(g[1] - gmax) + jnp.exp(g[2] - gmax) + jnp.exp(g[3] - gmax)
    p_sel = 1.0 / den
    sel = []
    free = jnp.ones((1, tl), F32)
    for j in range(N_GROUPS):
        s = jnp.where(g[j] >= gmax, free, 0.0)
        sel.append(s)
        free = free - s
    es = []
    for j in range(EPG):
        es.append(sel[0] * e[j] + sel[1] * e[EPG + j] + sel[2] * e[2 * EPG + j] + sel[3] * e[3 * EPG + j])
    rank = []
    for j in range(EPG):
        rj = jnp.zeros((1, tl), F32)
        for i in range(EPG):
            if i == j:
                continue
            beats = (es[i] >= es[j]) if i < j else (es[i] > es[j])
            rj = rj + jnp.where(beats, 1.0, 0.0)
        rank.append(rj)
    v1 = jnp.maximum(jnp.maximum(es[0], es[1]), jnp.maximum(es[2], es[3]))
    v2 = sum(jnp.where(rank[j] == 1.0, es[j], 0.0) for j in range(EPG))
    tt = jnp.exp(v2 - v1)
    w1 = p_sel / (1.0 + tt)
    w2 = w1 * tt
    w = [jnp.where(rank[j] == 0.0, w1, jnp.where(rank[j] == 1.0, w2, 0.0)) for j in range(EPG)]
    top2 = [jnp.where(rank[j] < 2.0, 1.0, 0.0) for j in range(EPG)]
    mem = jnp.concatenate([sel[gi] * top2[j] for gi in range(N_GROUPS) for j in range(EPG)], axis=0)
    wts = jnp.concatenate([sel[gi] * w[j] for gi in range(N_GROUPS) for j in range(EPG)], axis=0)
    w_ref[...] = wts
    lane = lax.broadcasted_iota(jnp.int32, (N_EXPERTS, tl), 1) & (sbk - 1)
    c = mem
    k = 1
    while k < sbk:
        c = c + jnp.where(lane >= k, pltpu.roll(c, k, 1), 0.0)
        k *= 2
    rnk = c - 1.0
    pos_ref[...] = jnp.where(mem > 0.0, rnk, -1.0)
    q0 = jnp.full((1, tl), -1.0, F32)
    q1 = jnp.full((1, tl), -1.0, F32)
    w0 = jnp.zeros((1, tl), F32)
    w1 = jnp.zeros((1, tl), F32)
    seen = jnp.zeros((1, tl), F32)
    for ex in range(N_EXPERTS):
        m = mem[ex:ex + 1]
        rk = rnk[ex:ex + 1]
        val = jnp.where(rk < r, rk + float(ex * r), -1.0)
        first = (m * (1.0 - seen)) > 0.0
        second = (m * seen) > 0.0
        q0 = jnp.where(first, val, q0)
        w0 = jnp.where(first, wts[ex:ex + 1], w0)
        q1 = jnp.where(second, val, q1)
        w1 = jnp.where(second, wts[ex:ex + 1], w1)
        seen = seen + m
    qrow_ref[...] = jnp.concatenate([q0, q1, jnp.zeros((6, tl), F32)], axis=0)
    qcol_ref[...] = jnp.concatenate([q0, q1, w0, w1, jnp.zeros((LANES - 4, tl), F32)], axis=0).T
    lane128 = lax.broadcasted_iota(jnp.int32, (N_EXPERTS, LANES), 1)
    cnt = jnp.zeros((N_EXPERTS, LANES), F32)
    for kb in range(tl // sbk):
        tot = jnp.sum(mem[:, kb * sbk:(kb + 1) * sbk], axis=1, keepdims=True)
        cnt = cnt + jnp.where(lane128 == kb, tot, 0.0)
    cnt_ref[...] = cnt


def _route(lg, *, tl, sbk, r):
    t = lg.shape[1]
    row = pl.BlockSpec((N_EXPERTS, tl), lambda i: (0, i))
    return pl.pallas_call(
        functools.partial(_route_kernel, sbk=sbk, r=r),
        grid=(t // tl,),
        in_specs=[pl.BlockSpec((32, tl), lambda i: (0, i))],
        out_specs=[row, row,
                   pl.BlockSpec((8, tl), lambda i: (0, i)),
                   pl.BlockSpec((tl, LANES), lambda i: (i, 0)),
                   pl.BlockSpec((N_EXPERTS, LANES), lambda i: (i, 0))],
        out_shape=[jax.ShapeDtypeStruct((N_EXPERTS, t), F32),
                   jax.ShapeDtypeStruct((N_EXPERTS, t), F32),
                   jax.ShapeDtypeStruct((8, t), F32),
                   jax.ShapeDtypeStruct((t, LANES), F32),
                   jax.ShapeDtypeStruct((t // tl * N_EXPERTS, LANES), F32)],
        compiler_params=_cparams("parallel"),
        name="route",
    )(lg)


def _mlp(x, wg, wu, wd):
    gt = _dot(x, wg)
    a = ((gt * _sigmoid(gt)) * _dot(x, wu)).astype(BF16)
    return _dot(a, wd).astype(BF16)


def _moe_kernel(cnt_ref, h_ref, qrow_ref, pos_ref, w_ref, wg_ref, wu_ref, wd_ref,
                qcol_ref, x1_ref, gg2_ref, o_ref, xs_ref, ovf_ref, *, sbk, r, eps, csb):
    i = pl.program_id(0)
    step = pl.program_id(1)
    nsb = h_ref.shape[0] // sbk
    exp_steps = N_EXPERTS // eps
    slots = N_EXPERTS * r

    grp = 4
    ngrp = nsb // grp

    @pl.when(step == 0)
    def _():
        def zero(sb, carry):
            ovf_ref[pl.ds(pl.multiple_of(sb * sbk, sbk), sbk), :] = jnp.zeros((sbk, D_MODEL), F32)
            return carry

        lax.fori_loop(0, nsb, zero, 0)
        pid =lax.broadcasted_iota(jnp.int32, (slots, sbk), 0).astype(F32)

        def select(sb, carry):
            q0 = qrow_ref[0, pl.ds(sb, 1), :]
            q1 = qrow_ref[1, pl.ds(sb, 1), :]
            s = jnp.where(q0 == pid, 1.0, jnp.where(q1 == pid, 1.0, 0.0)).astype(BF16)
            row0 = pl.multiple_of(sb * sbk, sbk)
            xs_ref[sb] = _dot(s, h_ref[pl.ds(row0, sbk), :]).astype(BF16)
            return carry

        lax.fori_loop(0, nsb, select, 0)

    @pl.when(step < exp_steps)
    def _():
        def expert(it, carry):
            k = it // ngrp
            sb0 = (it % ngrp) * grp
            off = pl.multiple_of((step * eps + k) * r, 16)
            x = jnp.concatenate([xs_ref[sb0 + j, pl.ds(off, r), :] for j in range(grp)], axis=0)
            y = _mlp(x, wg_ref[k], wu_ref[k], wd_ref[k])
            for j in range(grp):
                xs_ref[sb0 + j, pl.ds(off, r), :] = y[j * r:(j + 1) * r]
            return carry

        lax.fori_loop(0, eps * ngrp, expert, 0)

    n_pairs = (pl.num_programs(0) * nsb) * N_EXPERTS

    over = cnt_ref[n_pairs + i * exp_steps + jnp.minimum(step, exp_steps - 1)]

    @pl.when(jnp.logical_and(step < exp_steps, over > 0))
    def _():
        rid = lax.broadcasted_iota(jnp.int32, (LANES, sbk), 0).astype(F32)

        def expert(k, carry):
            e = step * eps + k
            wg, wu, wd = wg_ref[k], wu_ref[k], wd_ref[k]

            def sub_block(sb, c1):
                n_over = jnp.maximum(cnt_ref[(i * nsb + sb) * N_EXPERTS + e] - r, 0)
                row0 = pl.multiple_of(sb * sbk, sbk)

                def one_pass(s, c2):
                    base = (r + s * LANES).astype(F32)
                    hit = pos_ref[pl.ds(e, 1), pl.ds(sb, 1), :].reshape(1, sbk) == (rid + base)
                    sel = jnp.where(hit, 1.0, 0.0).astype(BF16)
                    yo = _mlp(_dot(sel, h_ref[pl.ds(row0, sbk), :]).astype(BF16), wg, wu, wd)
                    wct = jnp.where(hit, w_ref[pl.ds(e, 1), pl.ds(sb, 1), :].reshape(1, sbk), 0.0).T
                    ovf_ref[pl.ds(row0, sbk), :] += _dot(wct.astype(BF16), yo)
                    return c2

                return lax.fori_loop(0, (n_over + LANES - 1) // LANES, one_pass, c1)

            return lax.fori_loop(0, nsb, sub_block, carry)

        lax.fori_loop(0, eps, expert, 0)

    @pl.when(step >= exp_steps)
    def _():
        lane = lax.broadcasted_iota(jnp.int32, (sbk, slots), 1).astype(F32)
        for k in range(csb):
            sb = (step - exp_steps) * csb + k
            rows = slice(k * sbk, (k + 1) * sbk)
            qc = qcol_ref[rows, :]
            wmat = (jnp.where(lane == qc[:, 0:1], qc[:, 2:3], 0.0)
                    + jnp.where(lane == qc[:, 1:2], qc[:, 3:4], 0.0)).astype(BF16)
            y = _dot(wmat, xs_ref[sb]) + ovf_ref[pl.ds(pl.multiple_of(sb * sbk, sbk), sbk), :]
            o_ref[rows, :] = x1_ref[rows, :] + _rms(y) * gg2_ref[...]


def _moe(cnt, h2, qrow, pos3, w3, wg, wu, wd, qcol, x1, gg2, *, tb, sbk, r, eps, csb):
    t = h2.shape[0]
    nsb = tb // sbk
    exp_steps = N_EXPERTS // eps
    comb_steps = nsb // csb
    wblk = lambda i, s, c: (jnp.minimum(s, exp_steps - 1), 0, 0)
    oblk = lambda i, s, c: (i * comb_steps + jnp.maximum(s - exp_steps, 0), 0)
    r3 = pl.BlockSpec((N_EXPERTS, nsb, sbk), lambda i, s, c: (0, i, 0))
    return pl.pallas_call(
        functools.partial(_moe_kernel, sbk=sbk, r=r, eps=eps, csb=csb),
        grid_spec=pltpu.PrefetchScalarGridSpec(
            num_scalar_prefetch=1,
            grid=(t // tb, exp_steps + comb_steps),
            in_specs=[pl.BlockSpec((tb, D_MODEL), lambda i, s, c: (i, 0)),
                      pl.BlockSpec((8, nsb, sbk), lambda i, s, c: (0, i, 0)),
                      r3, r3,
                      pl.BlockSpec((eps, D_MODEL, D_EXPERT), wblk),
                      pl.BlockSpec((eps, D_MODEL, D_EXPERT), wblk),
                      pl.BlockSpec((eps, D_EXPERT, D_MODEL), wblk),
                      pl.BlockSpec((csb * sbk, LANES), oblk),
                      pl.BlockSpec((csb * sbk, D_MODEL), oblk),
                      pl.BlockSpec((1, D_MODEL), lambda i, s, c: (0, 0))],
            out_specs=pl.BlockSpec((csb * sbk, D_MODEL), oblk),
            scratch_shapes=[pltpu.VMEM((nsb, N_EXPERTS * r, D_MODEL), BF16),
                            pltpu.VMEM((tb, D_MODEL), F32)]),
        out_shape=jax.ShapeDtypeStruct((t, D_MODEL), F32),
        compiler_params=pltpu.CompilerParams(dimension_semantics=("parallel", "arbitrary"),
                                             vmem_limit_bytes=MOE_VMEM_LIMIT),
        name="moe",
    )(cnt, h2, qrow, pos3, w3, wg, wu, wd, qcol, x1, gg2)


def _pos_tables(rows):
    quarter = D_MODEL // 4
    freq = 1.0 / (POS_BASE ** (np.arange(quarter, dtype=np.float64) / quarter))
    r = np.arange(rows, dtype=np.float64)[:, None] * freq
    cl = np.arange(GRID_W, dtype=np.float64)[:, None] * freq
    er = np.concatenate([np.sin(r), np.cos(r)], axis=-1).astype(np.float32)
    ec = np.concatenate([np.sin(cl), np.cos(cl)], axis=-1).astype(np.float32)
    return jnp.asarray(er[:, None, :]), jnp.asarray(ec[None, :, :])


def _dft_tables(t):
    n = np.arange(CHUNK, dtype=np.int64)
    prod = n[:, None] * n[None, :]
    ang = (prod % CHUNK).astype(np.float64) * (2.0 * np.pi / CHUNK)
    c, s = np.cos(ang), np.sin(ang)
    f1 = np.concatenate([c, s], axis=0)
    a2 = np.concatenate([np.concatenate([c, -s], axis=1), np.concatenate([s, c], axis=1)], axis=0)
    cs = np.concatenate([c, -s], axis=0)
    angw = prod.astype(np.float64) * (2.0 * np.pi / t)
    f32 = lambda a: jnp.asarray(a.astype(np.float32))
    return f32(f1), f32(a2), f32(cs), f32(np.cos(angw)[:, :, None]), f32(np.sin(angw)[:, :, None])


def _blockdiag(w):
    n = w.shape[0]
    size = n * QKV_BLOCK
    spread = np.tile(np.eye(QKV_BLOCK, dtype=np.float32), (1, n))
    rows = jnp.dot(w.reshape(size, QKV_BLOCK), jnp.asarray(spread), precision=lax.Precision.HIGHEST)
    blk = np.arange(size) // QKV_BLOCK
    mask = (blk[:, None] == blk[None, :]).astype(np.float32)
    return rows * jnp.asarray(mask)


def _gate_weights(w_f, b_f, w_b, b_b):
    w = jnp.concatenate([w_f[:, :HEADS], w_b[:, :HEADS], w_f[:, HEADS:], w_b[:, HEADS:]], axis=1).T
    b = jnp.concatenate([b_f[:HEADS], b_b[:HEADS], b_f[HEADS:], b_b[HEADS:]])
    return w.astype(BF16), b[:, None]


def kernel(x, c, ctx, c_ctx, w_ada, b_ada, g_pre_mix, g_post_mix, g_pre_ffn, g_post_ffn,
           w_in, conv_w, conv_b, w_q, w_k, w_v, w_if_fwd, b_if_fwd, w_if_bwd, b_if_bwd,
           mlstm_norm_w, mlstm_skip, w_fourier, w_out, w_router_group, b_router_group,
           w_router_expert, b_router_expert, w_gate, w_up, w_down):
    t = x.shape[1]
    rows = t // GRID_W

    c8 = jnp.concatenate([c, c_ctx[None, :], jnp.zeros((6, D_MODEL), F32)], axis=0)
    mod = _ada(c8, w_ada[0], b_ada[0][None, :])
    shift1, scale1, gate1, shift2, scale2, gate2 = [mod[0:1, k * D_MODEL:(k + 1) * D_MODEL] for k in range(N_MOD)]
    shift1c, scale1c = mod[1:2, 0:D_MODEL], mod[1:2, D_MODEL:2 * D_MODEL]
    g1 = g_pre_mix[0][None, :] * (1.0 + scale1)
    g1c = g_pre_mix[0][None, :] * (1.0 + scale1c)
    gg1 = g_post_mix[0][None, :] * gate1
    g2 = g_pre_ffn[0][None, :] * (1.0 + scale2)
    gg2 = g_post_ffn[0][None, :] * gate2

    er3, ec3 = _pos_tables(rows)
    x3 = x.reshape(rows, GRID_W, D_MODEL)
    ctx3 = ctx.reshape(CTX_LEN // GRID_W, GRID_W, D_MODEL)
    w_in_bf = w_in[0].astype(BF16)

    xm_l, z_l, u_l = _inproj(x3, er3, ec3, g1, shift1, w_in_bf, rows=8, add_pos=True)
    xm_c, _, _ = _inproj(ctx3, er3, ec3, g1c, shift1c, w_in_bf, rows=CTX_LEN // GRID_W, add_pos=False)

    wq = _blockdiag(w_q[0]).astype(BF16)
    wkt = _blockdiag(w_k[0]).T.astype(BF16)
    wv = _blockdiag(w_v[0]).astype(BF16)
    wi, bi = _gate_weights(w_if_fwd[0], b_if_fwd[0], w_if_bwd[0], b_if_bwd[0])
    wiq, wik, wiv = wi[:, :D_MLSTM], wi[:, D_MLSTM:2 * D_MLSTM], wi[:, 2 * D_MLSTM:]
    cb = conv_b[0][None, :]
    q_l, kt_l, v_l, act_l, gp_l = _feat(xm_l, conv_w[0], cb, wq, wkt, wv, wiq, wik, wiv, bi, tm=512)
    q_c, kt_c, v_c, _, gp_c = _feat(xm_c, conv_w[0], cb, wq, wkt, wv, wiq, wik, wiv, bi, tm=CTX_LEN)

    dr_l, gc_l = _gates(gp_l, tl=2048)
    dr_c, gc_c = _gates(gp_c, tl=CTX_LEN)

    c0 = jnp.zeros((2 * HEADS, DH, 2 * DH), F32)
    m0 = jnp.zeros((2 * HEADS, 8, LANES), F32)
    c_ctx_fin, m_ctx_fin = _mlstm(q_c, kt_c, v_c, gc_c, dr_c, c0, m0, cps=CTX_LEN // CHUNK, emit=False)
    hf, hb, _, _ = _mlstm(q_l, kt_l, v_l, gc_l, dr_l, c_ctx_fin, m_ctx_fin, cps=2, emit=True)

    f1, a2, cs, cw3, sw3 = _dft_tables(t)
    yc, ys = _dft1(u_l.reshape(CHUNK, (t // CHUNK) * D_FOURIER), f1.astype(BF16), tn=2048)
    yf = _dft2(yc.reshape(t, D_FOURIER), ys.reshape(t, D_FOURIER), cw3, sw3,
               a2.astype(BF16), cs.astype(BF16), w_fourier[0].astype(BF16), kb=4)
    yf = yf.reshape(t, D_FOURIER)

    wr = jnp.concatenate([w_router_group[0], w_router_expert[0],
                          jnp.zeros((D_MODEL, LANES - N_GROUPS - N_EXPERTS), F32)], axis=1)
    br = jnp.concatenate([b_router_group[0], b_router_expert[0],
                          jnp.zeros((LANES - N_GROUPS - N_EXPERTS,), F32)])[None, :]
    x1, h2, lg = _merge(hf, hb, act_l, z_l, yf, x3, er3, ec3,
                        mlstm_norm_w[0][None, :], mlstm_skip[0][None, :], w_out[0].astype(BF16),
                        gg1, g2, shift2, wr, br, rows=8)
    pos, wts, qrow, qcol, cnt = _route(lg, tl=MOE_TB, sbk=MOE_SBK, r=MOE_R)
    nsb = MOE_TB // MOE_SBK
    cnt = cnt.reshape(t // MOE_TB, N_EXPERTS, LANES)[:, :, :nsb]
    cnt = jnp.transpose(cnt, (0, 2, 1)).astype(jnp.int32)
    over = jnp.maximum(cnt - MOE_R, 0).reshape(t // MOE_TB, nsb, N_EXPERTS // MOE_EPS, MOE_EPS)
    cnt = jnp.concatenate([cnt.reshape(-1), jnp.max(over, axis=(1, 3)).reshape(-1)])
    pos3 = pos.reshape(N_EXPERTS, t // MOE_SBK, MOE_SBK)
    wts3 = wts.reshape(N_EXPERTS, t // MOE_SBK, MOE_SBK)
    out = _moe(cnt, h2, qrow.reshape(8, t // MOE_SBK, MOE_SBK), pos3, wts3,
               w_gate[0].astype(BF16), w_up[0].astype(BF16), w_down[0].astype(BF16),
               qcol, x1, gg2, tb=MOE_TB, sbk=MOE_SBK, r=MOE_R, eps=MOE_EPS, csb=MOE_CSB)
    return out[None]
```

```python
import functools

import numpy as np
import jax
import jax.numpy as jnp
from jax import lax
from jax.experimental import pallas as pl
from jax.experimental.pallas import tpu as pltpu

F32 = jnp.float32
BF16 = jnp.bfloat16

D_MODEL = 1024
SEQ = 16384
GRID_W = 64
CTX_LEN = 256
D_MLSTM = 512
HEADS = 4
DH = 128
QKV_BLOCK = 4
CONV_K = 3
CHUNK = 128
D_FOURIER = 512
FGROUPS = 4
FCG = 128
N_GROUPS = 4
EPG = 4
N_EXPERTS = 16
D_EXPERT = 512
N_MOD = 6
EPS = 1e-6
POS_BASE = 10000.0
LANES = 128
NEG_BIG = -3.0e38

VMEM_LIMIT = 52 * 1024 * 1024
MOE_VMEM_LIMIT = 56 * 1024 * 1024
MOE_TB = 2048
MOE_SBK = 256
MOE_SLOTS = 1024
MOE_CSB = 1


def _cparams(*sem):
    return pltpu.CompilerParams(dimension_semantics=sem, vmem_limit_bytes=VMEM_LIMIT)


def _dot(a, b):
    return jnp.dot(a, b, preferred_element_type=F32)


def _dot_nt(a, b):
    return lax.dot_general(a, b, (((1,), (1,)), ((), ())), preferred_element_type=F32)


def _split_bf16(a):
    hi = a.astype(BF16)
    lo = (a - hi.astype(F32)).astype(BF16)
    return hi, lo


def _dot3(a, b):
    a_hi, a_lo = _split_bf16(a)
    b_hi, b_lo = _split_bf16(b)
    return _dot(a_hi, b_hi) + (_dot(a_hi, b_lo) + _dot(a_lo, b_hi))


def _sigmoid(x):
    return 1.0 / (1.0 + jnp.exp(-x))


def _rms(x):
    return x * lax.rsqrt(jnp.mean(x * x, axis=-1, keepdims=True) + EPS)


def _ada_kernel(c_ref, w_ref, b_ref, o_ref):
    c = c_ref[...]
    s = c * _sigmoid(c)
    o_ref[...] = _dot3(s, w_ref[...]) + b_ref[...]


def _ada(c8, w, b):
    n = w.shape[1]
    tn = 768
    return pl.pallas_call(
        _ada_kernel,
        grid=(n // tn,),
        in_specs=[pl.BlockSpec((8, D_MODEL), lambda j: (0, 0)),
                  pl.BlockSpec((D_MODEL, tn), lambda j: (0, j)),
                  pl.BlockSpec((1, tn), lambda j: (0, j))],
        out_specs=pl.BlockSpec((8, tn), lambda j: (0, j)),
        out_shape=jax.ShapeDtypeStruct((8, n), F32),
        compiler_params=_cparams("parallel"),
        name="ada",
    )(c8, w, b)


def _add_pos(x3, er_ref, ec_ref):
    r = x3.shape[0]
    pr = jnp.broadcast_to(er_ref[...], (r, GRID_W, D_MODEL // 2))
    pc = jnp.broadcast_to(ec_ref[...], (r, GRID_W, D_MODEL // 2))
    return x3 + jnp.concatenate([pr, pc], axis=-1)


def _inproj_kernel(x_ref, er_ref, ec_ref, g_ref, sh_ref, w_ref, xm_ref, z_ref, u_ref, *, add_pos):
    x3 = x_ref[...]
    if add_pos:
        x3 = _add_pos(x3, er_ref, ec_ref)
    x = x3.reshape(x3.shape[0] * GRID_W, D_MODEL)
    h = _rms(x) * g_ref[...] + sh_ref[...]
    proj = _dot(h.astype(BF16), w_ref[...])
    xm_ref[...] = proj[:, :D_MLSTM].astype(BF16)
    z_ref[...] = proj[:, D_MLSTM:2 * D_MLSTM].astype(BF16)
    u_ref[...] = proj[:, 2 * D_MLSTM:].astype(BF16)


def _inproj(x3, er3, ec3, g_eff, shift, w_in, *, rows, add_pos):
    nr = x3.shape[0]
    t = nr * GRID_W
    tm = rows * GRID_W
    out = jax.ShapeDtypeStruct((t, D_MLSTM), BF16)
    ospec = pl.BlockSpec((tm, D_MLSTM), lambda i: (i, 0))
    vec = pl.BlockSpec((1, D_MODEL), lambda i: (0, 0))
    return pl.pallas_call(
        functools.partial(_inproj_kernel, add_pos=add_pos),
        grid=(nr // rows,),
        in_specs=[pl.BlockSpec((rows, GRID_W, D_MODEL), lambda i: (i, 0, 0)),
                  pl.BlockSpec((rows, 1, D_MODEL // 2), lambda i: (i, 0, 0)),
                  pl.BlockSpec((1, GRID_W, D_MODEL // 2), lambda i: (0, 0, 0)),
                  vec, vec,
                  pl.BlockSpec(w_in.shape, lambda i: (0, 0))],
        out_specs=[ospec, ospec, ospec],
        out_shape=[out, out, out],
        compiler_params=_cparams("parallel"),
        name="inproj",
    )(x3, er3, ec3, g_eff, shift, w_in)


def _feat_kernel(xm_ref, prev_ref, next_ref, cw_ref, cb_ref, wq_ref, wkt_ref, wv_ref,
                 wiq_ref, wik_ref, wiv_ref, bi_ref,
                 q_ref, kt_ref, v_ref, act_ref, g_ref):
    i = pl.program_id(0)
    n = pl.num_programs(0)
    xm_bf = xm_ref[...]
    xm = xm_bf.astype(F32)
    tm = xm.shape[0]
    prev_row = prev_ref[...].astype(F32)[15:16, :] * jnp.where(i > 0, 1.0, 0.0)
    next_row = next_ref[...].astype(F32)[0:1, :] * jnp.where(i < n - 1, 1.0, 0.0)
    rid = lax.broadcasted_iota(jnp.int32, (tm, 1), 0)
    x_left = jnp.where(rid == 0, prev_row, pltpu.roll(xm, 1, 0))
    x_right = jnp.where(rid == tm - 1, next_row, pltpu.roll(xm, tm - 1, 0))
    cw = cw_ref[...]
    y = cw[0:1] * x_left + cw[1:2] * xm + cw[2:3] * x_right + cb_ref[...]
    act = (y * _sigmoid(y)).astype(BF16)
    act_ref[...] = act
    q = _dot(act, wq_ref[...])
    kt = _dot_nt(wkt_ref[...], act)
    v = _dot(xm_bf, wv_ref[...])
    q_bf = q.astype(BF16)
    kt_bf = kt.astype(BF16)
    v_bf = v.astype(BF16)
    q_ref[...] = (q * (DH ** -0.5)).astype(BF16)
    kt_ref[...] = kt_bf
    v_ref[...] = v_bf
    g = _dot_nt(wiq_ref[...], q_bf) + _dot(wik_ref[...], kt_bf) + _dot_nt(wiv_ref[...], v_bf)
    g_ref[...] = g + bi_ref[...]


def _feat(xm, conv_w, conv_b, wq, wkt, wv, wiq, wik, wiv, bi, *, tm):
    t = xm.shape[0]
    nb16 = t // 16
    k16 = tm // 16
    full = lambda a: pl.BlockSpec(a.shape, lambda i: (0,) * a.ndim)
    tok = pl.BlockSpec((tm, D_MLSTM), lambda i: (i, 0))
    return pl.pallas_call(
        _feat_kernel,
        grid=(t // tm,),
        in_specs=[tok,
                  pl.BlockSpec((16, D_MLSTM), lambda i: (jnp.maximum(i * k16 - 1, 0), 0)),
                  pl.BlockSpec((16, D_MLSTM), lambda i: (jnp.minimum((i + 1) * k16, nb16 - 1), 0)),
                  full(conv_w), full(conv_b), full(wq), full(wkt), full(wv),
                  full(wiq), full(wik), full(wiv), full(bi)],
        out_specs=[tok,
                   pl.BlockSpec((D_MLSTM, tm), lambda i: (0, i)),
                   tok, tok,
                   pl.BlockSpec((16, tm), lambda i: (0, i))],
        out_shape=[jax.ShapeDtypeStruct((t, D_MLSTM), BF16),
                   jax.ShapeDtypeStruct((D_MLSTM, t), BF16),
                   jax.ShapeDtypeStruct((t, D_MLSTM), BF16),
                   jax.ShapeDtypeStruct((t, D_MLSTM), BF16),
                   jax.ShapeDtypeStruct((16, t), F32)],
        compiler_params=_cparams("parallel"),
        name="feat",
    )(xm, xm, xm, conv_w, conv_b, wq, wkt, wv, wiq, wik, wiv, bi)


def _gates_kernel(g_ref, d_ref, gc_ref):
    g = g_ref[...]
    tl = g.shape[1]
    ig = g[0:8]
    fg = g[8:16]
    lf = jnp.minimum(fg, 0.0) - jnp.log(1.0 + jnp.exp(-jnp.abs(fg)))
    pos = lax.broadcasted_iota(jnp.int32, (8, tl), 1) & (CHUNK - 1)
    is_fwd = lax.broadcasted_iota(jnp.int32, (8, tl), 0) < HEADS

    def scan(x, op, ident):
        xf = x
        xb = x
        k = 1
        while k < CHUNK:
            xf = op(xf, jnp.where(pos >= k, pltpu.roll(xf, k, 1), ident))
            xb = op(xb, jnp.where(pos < CHUNK - k, pltpu.roll(xb, tl - k, 1), ident))
            k *= 2
        return jnp.where(is_fwd, xf, xb)

    b = scan(lf, jnp.add, 0.0)
    d = ig - b
    mloc = b + scan(d, jnp.maximum, NEG_BIG)
    d_ref[...] = d
    stack = jnp.concatenate([b, mloc, jnp.zeros((LANES - 16, tl), F32)], axis=0)
    gc_ref[...] = stack.T


def _gates(g, *, tl):
    t = g.shape[1]
    return pl.pallas_call(
        _gates_kernel,
        grid=(t // tl,),
        in_specs=[pl.BlockSpec((16, tl), lambda i: (0, i))],
        out_specs=[pl.BlockSpec((8, tl), lambda i: (0, i)),
                   pl.BlockSpec((tl, LANES), lambda i: (i, 0))],
        out_shape=[jax.ShapeDtypeStruct((8, t), F32),
                   jax.ShapeDtypeStruct((t, LANES), F32)],
        compiler_params=_cparams("parallel"),
        name="gates",
    )(g)


def _mlstm_kernel(*refs, cps, emit):
    (qf_ref, kf_ref, vf_ref, gcf_ref, drf_ref,
     qb_ref, kb_ref, vb_ref, gcb_ref, drb_ref, c0_ref, m0_ref) = refs[:12]
    if emit:
        hf_ref, hb_ref, cfin_ref, mfin_ref, c_scr, m_scr = refs[12:]
    else:
        cfin_ref, mfin_ref, c_scr, m_scr = refs[12:]
        hf_ref = hb_ref = None
    step = pl.program_id(0)

    @pl.when(step == 0)
    def _():
        c_scr[...] = c0_ref[...]
        m_scr[...] = m0_ref[...]

    ti = lax.broadcasted_iota(jnp.int32, (CHUNK, CHUNK), 0)
    si = lax.broadcasted_iota(jnp.int32, (CHUNK, CHUNK), 1)
    ones_col = jnp.where(si == 0, 1.0, 0.0).astype(BF16)

    for j in range(cps):
        for d in range(2):
            if d == 0:
                q_ref, k_ref, v_ref, gc_ref, dr_ref, h_ref = qf_ref, kf_ref, vf_ref, gcf_ref, drf_ref, hf_ref
                r0, mask, last = j * CHUNK, si <= ti, CHUNK - 1
            else:
                q_ref, k_ref, v_ref, gc_ref, dr_ref, h_ref = qb_ref, kb_ref, vb_ref, gcb_ref, drb_ref, hb_ref
                r0, mask, last = (cps - 1 - j) * CHUNK, si >= ti, 0
            rows = slice(r0, r0 + CHUNK)
            gc = gc_ref[rows, :]
            dr = dr_ref[:, rows]
            for hd in range(HEADS):
                r = d * HEADS + hd
                cols = slice(hd * DH, (hd + 1) * DH)
                b = gc[:, r:r + 1]
                mloc = gc[:, 8 + r:9 + r]
                drow = dr[r:r + 1, :]
                m0 = m_scr[r][0:1, 0:1]
                c_aug = c_scr[r]
                kt = k_ref[cols, rows]
                vaug = jnp.concatenate([v_ref[rows, cols], ones_col], axis=1)
                if emit:
                    qh = q_ref[rows, cols]
                    bm = b + m0
                    m_t = jnp.maximum(bm, mloc)
                    dmat = jnp.where(mask, jnp.exp((b - m_t) + drow), 0.0)
                    smat = (_dot(qh, kt) * dmat).astype(BF16)
                    num = _dot(smat, vaug) + jnp.exp(bm - m_t) * _dot(qh, c_aug.astype(BF16))
                    den = jnp.maximum(jnp.abs(num[:, DH:DH + 1]), jnp.exp(-m_t))
                    h_ref[rows, cols] = (num[:, :DH] / den).astype(h_ref.dtype)
                btot = gc[last:last + 1, r:r + 1]
                amax = gc[last:last + 1, 8 + r:9 + r]
                m_new = jnp.maximum(btot + m0, amax)
                decay = jnp.exp(btot + m0 - m_new)
                kw = (kt.astype(F32) * jnp.exp(btot + drow - m_new)).astype(BF16)
                c_scr[r] = decay * c_aug + _dot(kw, vaug)
                m_scr[r] = jnp.broadcast_to(m_new, (8, LANES))

    @pl.when(step == pl.num_programs(0) - 1)
    def _():
        cfin_ref[...] = c_scr[...]
        mfin_ref[...] = m_scr[...]


def _mlstm(q, kt, v, gc, dr, c0, m0, *, cps, emit):
    t = q.shape[0]
    cb = cps * CHUNK
    nb = t // cb
    fwd_r = lambda i: (i, 0)
    bwd_r = lambda i: (nb - 1 - i, 0)
    fwd_c = lambda i: (0, i)
    bwd_c = lambda i: (0, nb - 1 - i)
    tok = lambda f: pl.BlockSpec((cb, D_MLSTM), f)
    in_specs = []
    for fr, fc in ((fwd_r, fwd_c), (bwd_r, bwd_c)):
        in_specs += [tok(fr), pl.BlockSpec((D_MLSTM, cb), fc), tok(fr),
                     pl.BlockSpec((cb, LANES), fr), pl.BlockSpec((8, cb), fc)]
    cshape = (2 * HEADS, DH, 2 * DH)
    mshape = (2 * HEADS, 8, LANES)
    cspec = pl.BlockSpec(cshape, lambda i: (0, 0, 0))
    mspec = pl.BlockSpec(mshape, lambda i: (0, 0, 0))
    in_specs += [cspec, mspec]
    out_specs = [cspec, mspec]
    out_shape = [jax.ShapeDtypeStruct(cshape, F32), jax.ShapeDtypeStruct(mshape, F32)]
    if emit:
        out_specs = [tok(fwd_r), tok(bwd_r)] + out_specs
        out_shape = [jax.ShapeDtypeStruct((t, D_MLSTM), BF16)] * 2 + out_shape
    return pl.pallas_call(
        functools.partial(_mlstm_kernel, cps=cps, emit=emit),
        grid=(nb,),
        in_specs=in_specs,
        out_specs=out_specs,
        out_shape=out_shape,
        scratch_shapes=[pltpu.VMEM(cshape, F32), pltpu.VMEM(mshape, F32)],
        compiler_params=_cparams("arbitrary"),
        name="mlstm",
    )(q, kt, v, gc, dr, q, kt, v, gc, dr, c0, m0)


def _dft1_kernel(u_ref, f_ref, yc_ref, ys_ref):
    y = _dot(f_ref[...], u_ref[...])
    yc_ref[...] = y[:CHUNK].astype(BF16)
    ys_ref[...] = y[CHUNK:].astype(BF16)


def _dft1(u2d, f1, *, tn):
    n = u2d.shape[1]
    blk = pl.BlockSpec((CHUNK, tn), lambda j: (0, j))
    out = jax.ShapeDtypeStruct((CHUNK, n), BF16)
    return pl.pallas_call(
        _dft1_kernel,
        grid=(n // tn,),
        in_specs=[blk, pl.BlockSpec(f1.shape, lambda j: (0, 0))],
        out_specs=[blk, blk],
        out_shape=[out, out],
        compiler_params=_cparams("parallel"),
        name="dft1",
    )(u2d, f1)


def _dft2_kernel(yc_ref, ys_ref, cw_ref, sw_ref, a2_ref, cs_ref, wf_ref, o_ref, *, kb, scale):
    a2 = a2_ref[...]
    cs = cs_ref[...]
    for kk in range(kb):
        rows = slice(kk * CHUNK, (kk + 1) * CHUNK)
        yc = yc_ref[rows, :].astype(F32)
        ys = ys_ref[rows, :].astype(F32)
        cw = cw_ref[kk]
        sw = sw_ref[kk]
        p = jnp.concatenate([yc * cw - ys * sw, yc * sw + ys * cw], axis=0).astype(BF16)
        x = _dot(a2, p)
        for g in range(FGROUPS):
            cols = slice(g * FCG, (g + 1) * FCG)
            cat = jnp.concatenate([x[:CHUNK, cols], x[CHUNK:, cols]], axis=1).astype(BF16)
            f = _dot(cat, cs) * scale
            o_ref[:, kk * D_FOURIER + g * FCG:kk * D_FOURIER + (g + 1) * FCG] = (
                _dot(f.astype(BF16), wf_ref[g]).astype(o_ref.dtype))


def _dft2(yc, ys, cw3, sw3, a2, cs, wf, *, kb):
    t = yc.shape[0]
    n1 = t // CHUNK
    scale = float(1.0 / np.sqrt(float(t) * FCG))
    tok = pl.BlockSpec((kb * CHUNK, D_FOURIER), lambda i: (i, 0))
    tw = pl.BlockSpec((kb, CHUNK, 1), lambda i: (i, 0, 0))
    full = lambda a: pl.BlockSpec(a.shape, lambda i: (0,) * a.ndim)
    return pl.pallas_call(
        functools.partial(_dft2_kernel, kb=kb, scale=scale),
        grid=(n1 // kb,),
        in_specs=[tok, tok, tw, tw, full(a2), full(cs), full(wf)],
        out_specs=pl.BlockSpec((CHUNK, kb * D_FOURIER), lambda i: (0, i)),
        out_shape=jax.ShapeDtypeStruct((CHUNK, n1 * D_FOURIER), BF16),
        compiler_params=_cparams("parallel"),
        name="dft2",
    )(yc, ys, cw3, sw3, a2, cs, wf)


def _merge_kernel(hf_ref, hb_ref, act_ref, z_ref, yf_ref, x_ref, er_ref, ec_ref,
                  nw_ref, sk_ref, wout_ref, gg1_ref, g2_ref, sh2_ref, wr_ref, br_ref,
                  x1_ref, h2_ref, lg_ref):
    h = hf_ref[...].astype(F32) + hb_ref[...].astype(F32)
    parts = []
    for hd in range(HEADS):
        hh = h[:, hd * DH:(hd + 1) * DH]
        dl = hh - jnp.mean(hh, axis=-1, keepdims=True)
        var = jnp.mean(dl * dl, axis=-1, keepdims=True)
        parts.append(dl * lax.rsqrt(var + EPS))
    hn = jnp.concatenate(parts, axis=-1)
    z = z_ref[...].astype(F32)
    m = (hn * nw_ref[...] + sk_ref[...] * act_ref[...].astype(F32)) * (z * _sigmoid(z))
    cat = jnp.concatenate([m.astype(BF16), yf_ref[...]], axis=-1)
    y = _dot(cat, wout_ref[...])
    x3 = _add_pos(x_ref[...], er_ref, ec_ref)
    xp = x3.reshape(x3.shape[0] * GRID_W, D_MODEL)
    x1 = xp + _rms(y) * gg1_ref[...]
    x1_ref[...] = x1
    h2 = _rms(x1) * g2_ref[...] + sh2_ref[...]
    h2_ref[...] = h2.astype(BF16)
    lg = _dot3(h2, wr_ref[...]) + br_ref[...]
    lg_ref[...] = lg.T[:32]


def _merge(hf, hb, act, z, yf, x3, er3, ec3, nw, sk, wout, gg1, g2, sh2, wr, br, *, rows):
    nr = x3.shape[0]
    t = nr * GRID_W
    tm = rows * GRID_W
    tok = pl.BlockSpec((tm, D_MLSTM), lambda i: (i, 0))
    full = lambda a: pl.BlockSpec(a.shape, lambda i: (0,) * a.ndim)
    return pl.pallas_call(
        _merge_kernel,
        grid=(nr // rows,),
        in_specs=[tok, tok, tok, tok, tok,
                  pl.BlockSpec((rows, GRID_W, D_MODEL), lambda i: (i, 0, 0)),
                  pl.BlockSpec((rows, 1, D_MODEL // 2), lambda i: (i, 0, 0)),
                  pl.BlockSpec((1, GRID_W, D_MODEL // 2), lambda i: (0, 0, 0)),
                  full(nw), full(sk), full(wout), full(gg1), full(g2), full(sh2), full(wr), full(br)],
        out_specs=[pl.BlockSpec((tm, D_MODEL), lambda i: (i, 0)),
                   pl.BlockSpec((tm, D_MODEL), lambda i: (i, 0)),
                   pl.BlockSpec((32, tm), lambda i: (0, i))],
        out_shape=[jax.ShapeDtypeStruct((t, D_MODEL), F32),
                   jax.ShapeDtypeStruct((t, D_MODEL), BF16),
                   jax.ShapeDtypeStruct((32, t), F32)],
        compiler_params=_cparams("parallel"),
        name="merge",
    )(hf, hb, act, z, yf, x3, er3, ec3, nw, sk, wout, gg1, g2, sh2, wr, br)


def _route_kernel(lg_ref, pos_ref, w_ref, cnt_ref, *, sbk):
    lg = lg_ref[...]
    tl = lg.shape[1]
    g = [lg[j:j + 1] for j in range(N_GROUPS)]
    e = [lg[N_GROUPS + j:N_GROUPS + j + 1] for j in range(N_EXPERTS)]
    gmax = jnp.maximum(jnp.maximum(g[0], g[1]), jnp.maximum(g[2], g[3]))
    den = jnp.exp(g[0] - gmax) + jnp.exp(g[1] - gmax) + jnp.exp(g[2] - gmax) + jnp.exp(g[3] - gmax)
    p_sel = 1.0 / den
    sel = []
    free = jnp.ones((1, tl), F32)
    for j in range(N_GROUPS):
        s = jnp.where(g[j] >= gmax, free, 0.0)
        sel.append(s)
        free = free - s
    es = []
    for j in range(EPG):
        es.append(sel[0] * e[j] + sel[1] * e[EPG + j] + sel[2] * e[2 * EPG + j] + sel[3] * e[3 * EPG + j])
    rank = []
    for j in range(EPG):
        rj = jnp.zeros((1, tl), F32)
        for i in range(EPG):
            if i == j:
                continue
            beats = (es[i] >= es[j]) if i < j else (es[i] > es[j])
            rj = rj + jnp.where(beats, 1.0, 0.0)
        rank.append(rj)
    v1 = jnp.maximum(jnp.maximum(es[0], es[1]), jnp.maximum(es[2], es[3]))
    v2 = sum(jnp.where(rank[j] == 1.0, es[j], 0.0) for j in range(EPG))
    tt = jnp.exp(v2 - v1)
    w1 = p_sel / (1.0 + tt)
    w2 = w1 * tt
    w = [jnp.where(rank[j] == 0.0, w1, jnp.where(rank[j] == 1.0, w2, 0.0)) for j in range(EPG)]
    top2 = [jnp.where(rank[j] < 2.0, 1.0, 0.0) for j in range(EPG)]
    mem = jnp.concatenate([sel[gi] * top2[j] for gi in range(N_GROUPS) for j in range(EPG)], axis=0)
    wts = jnp.concatenate([sel[gi] * w[j] for gi in range(N_GROUPS) for j in range(EPG)], axis=0)
    w_ref[...] = wts
    lane = lax.broadcasted_iota(jnp.int32, (N_EXPERTS, tl), 1) & (sbk - 1)
    c = mem
    k = 1
    while k < sbk:
        c = c + jnp.where(lane >= k, pltpu.roll(c, k, 1), 0.0)
        k *= 2
    pos_ref[...] = jnp.where(mem > 0.0, c - 1.0, -1.0)
    lane128 = lax.broadcasted_iota(jnp.int32, (N_EXPERTS, LANES), 1)
    cnt = jnp.zeros((N_EXPERTS, LANES), F32)
    for kb in range(tl // sbk):
        tot = jnp.sum(mem[:, kb * sbk:(kb + 1) * sbk], axis=1, keepdims=True)
        cnt = cnt + jnp.where(lane128 == kb, tot, 0.0)
    cnt_ref[...] = cnt


def _route(lg, *, tl, sbk):
    t = lg.shape[1]
    row = pl.BlockSpec((N_EXPERTS, tl), lambda i: (0, i))
    return pl.pallas_call(
        functools.partial(_route_kernel, sbk=sbk),
        grid=(t // tl,),
        in_specs=[pl.BlockSpec((32, tl), lambda i: (0, i))],
        out_specs=[row, row, pl.BlockSpec((N_EXPERTS, LANES), lambda i: (i, 0))],
        out_shape=[jax.ShapeDtypeStruct((N_EXPERTS, t), F32),
                   jax.ShapeDtypeStruct((N_EXPERTS, t), F32),
                   jax.ShapeDtypeStruct((t // tl * N_EXPERTS, LANES), F32)],
        compiler_params=_cparams("parallel"),
        name="route",
    )(lg)


def _slots_kernel(tab_ref, pos_ref, w_ref, qrow_ref, qcol_ref, *, off_r, off_o):
    i = pl.program_id(0)
    tl = pos_ref.shape[1]
    q0 = jnp.full((1, tl), -1.0, F32)
    q1 = jnp.full((1, tl), -1.0, F32)
    w0 = jnp.zeros((1, tl), F32)
    w1 = jnp.zeros((1, tl), F32)
    seen = jnp.zeros((1, tl), F32)
    for ex in range(N_EXPERTS):
        rk = pos_ref[ex:ex + 1, :]
        wt = w_ref[ex:ex + 1, :]
        rows = tab_ref[off_r + i * N_EXPERTS + ex].astype(F32)
        first_row = tab_ref[off_o + i * N_EXPERTS + ex].astype(F32)
        m = jnp.where(rk >= 0.0, 1.0, 0.0)
        val = jnp.where(rk < rows, rk + first_row, -1.0)
        first = (m * (1.0 - seen)) > 0.0
        second = (m * seen) > 0.0
        q0 = jnp.where(first, val, q0)
        w0 = jnp.where(first, wt, w0)
        q1 = jnp.where(second, val, q1)
        w1 = jnp.where(second, wt, w1)
        seen = seen + m
    qrow_ref[...] = jnp.concatenate([q0, q1, jnp.zeros((6, tl), F32)], axis=0)
    qcol_ref[...] = jnp.concatenate([q0, q1, w0, w1, jnp.zeros((LANES - 4, tl), F32)], axis=0).T


def _slots(tab, pos, w, *, tl, off_r, off_o):
    t = pos.shape[1]
    row = pl.BlockSpec((N_EXPERTS, tl), lambda i, c: (0, i))
    return pl.pallas_call(
        functools.partial(_slots_kernel, off_r=off_r, off_o=off_o),
        grid_spec=pltpu.PrefetchScalarGridSpec(
            num_scalar_prefetch=1,
            grid=(t // tl,),
            in_specs=[row, row],
            out_specs=[pl.BlockSpec((8, tl), lambda i, c: (0, i)),
                       pl.BlockSpec((tl, LANES), lambda i, c: (i, 0))]),
        out_shape=[jax.ShapeDtypeStruct((8, t), F32),
                   jax.ShapeDtypeStruct((t, LANES), F32)],
        compiler_params=_cparams("parallel"),
        name="slots",
    )(tab, pos, w)


def _mlp(x, wg, wu, wd):
    gt = _dot(x, wg)
    a = ((gt * _sigmoid(gt)) * _dot(x, wu)).astype(BF16)
    return _dot(a, wd).astype(BF16)


def _moe_kernel(tab_ref, h_ref, qrow_ref, pos_ref, w_ref, wg_ref, wu_ref, wd_ref,
                qcol_ref, x1_ref, gg2_ref, o_ref, xs_ref, ovf_ref, *, sbk, slots, csb, off_r, off_o, off_f):
    i = pl.program_id(0)
    step = pl.program_id(1)
    nsb = h_ref.shape[0] // sbk
    e = jnp.minimum(step, N_EXPERTS - 1)
    n_rows = tab_ref[off_r + i * N_EXPERTS + e]
    row_a = tab_ref[off_o + i * N_EXPERTS + e]
    over = tab_ref[off_f + i * N_EXPERTS + e]

    @pl.when(step == 0)
    def _():
        def zero(sb, carry):
            ovf_ref[pl.ds(pl.multiple_of(sb * sbk, sbk), sbk), :] = jnp.zeros((sbk, D_MODEL), F32)
            return carry

        lax.fori_loop(0, nsb, zero, 0)
        pid = lax.broadcasted_iota(jnp.int32, (slots, sbk), 0).astype(F32)

        def select(sb, carry):
            q0 = qrow_ref[0, pl.ds(sb, 1), :]
            q1 = qrow_ref[1, pl.ds(sb, 1), :]
            s = jnp.where(q0 == pid, 1.0, jnp.where(q1 == pid, 1.0, 0.0)).astype(BF16)
            row0 = pl.multiple_of(sb * sbk, sbk)
            xs_ref[sb] = _dot(s, h_ref[pl.ds(row0, sbk), :]).astype(BF16)
            return carry

        lax.fori_loop(0, nsb, select, 0)

    def run_rows(first, n):
        first = pl.multiple_of(first, 16)
        x = jnp.concatenate([xs_ref[sb, pl.ds(first, n), :] for sb in range(nsb)], axis=0)
        y = _mlp(x, wg_ref[0], wu_ref[0], wd_ref[0])
        for sb in range(nsb):
            xs_ref[sb, pl.ds(first, n), :] = y[sb * n:(sb + 1) * n]

    @pl.when(step < N_EXPERTS)
    def _():
        n_pairs = n_rows // 32

        def pair(t, carry):
            run_rows(row_a + t * 32, 32)
            return carry

        lax.fori_loop(0, n_pairs, pair, 0)

        @pl.when(n_rows % 32 != 0)
        def _():
            run_rows(row_a + n_pairs * 32, 16)

    @pl.when(jnp.logical_and(step < N_EXPERTS, over > 0))
    def _():
        rid = lax.broadcasted_iota(jnp.int32, (LANES, sbk), 0).astype(F32)
        wg, wu, wd = wg_ref[0], wu_ref[0], wd_ref[0]

        def sub_block(sb, c1):
            n_over = jnp.maximum(tab_ref[(i * nsb + sb) * N_EXPERTS + e] - n_rows, 0)
            row0 = pl.multiple_of(sb * sbk, sbk)

            def one_pass(s, c2):
                base = (n_rows + s * LANES).astype(F32)
                hit = pos_ref[pl.ds(e, 1), pl.ds(sb, 1), :].reshape(1, sbk) == (rid + base)
                sel = jnp.where(hit, 1.0, 0.0).astype(BF16)
                yo = _mlp(_dot(sel, h_ref[pl.ds(row0, sbk), :]).astype(BF16), wg, wu, wd)
                wct = jnp.where(hit, w_ref[pl.ds(e, 1), pl.ds(sb, 1), :].reshape(1, sbk), 0.0).T
                ovf_ref[pl.ds(row0, sbk), :] += _dot(wct.astype(BF16), yo)
                return c2

            return lax.fori_loop(0, (n_over + LANES - 1) // LANES, one_pass, c1)

        lax.fori_loop(0, nsb, sub_block, 0)

    @pl.when(step >= N_EXPERTS)
    def _():
        lane = lax.broadcasted_iota(jnp.int32, (sbk, slots), 1).astype(F32)
        for k in range(csb):
            sb = (step - N_EXPERTS) * csb + k
            rows = slice(k * sbk, (k + 1) * sbk)
            qc = qcol_ref[rows, :]
            wmat = (jnp.where(lane == qc[:, 0:1], qc[:, 2:3], 0.0)
                    + jnp.where(lane == qc[:, 1:2], qc[:, 3:4], 0.0)).astype(BF16)
            y = _dot(wmat, xs_ref[sb]) + ovf_ref[pl.ds(pl.multiple_of(sb * sbk, sbk), sbk), :]
            o_ref[rows, :] = x1_ref[rows, :] + _rms(y) * gg2_ref[...]


def _moe(tab, h2, qrow3, pos3, w3, wg, wu, wd, qcol, x1, gg2, *, tb, sbk, slots, csb, off_r, off_o, off_f):
    t = h2.shape[0]
    nsb = tb // sbk
    comb_steps = nsb // csb
    wblk = lambda i, s, c: (jnp.minimum(s, N_EXPERTS - 1), 0, 0)
    oblk = lambda i, s, c: (i * comb_steps + jnp.maximum(s - N_EXPERTS, 0), 0)
    r3 = pl.BlockSpec((N_EXPERTS, nsb, sbk), lambda i, s, c: (0, i, 0))
    return pl.pallas_call(
        functools.partial(_moe_kernel, sbk=sbk, slots=slots, csb=csb,
                          off_r=off_r, off_o=off_o, off_f=off_f),
        grid_spec=pltpu.PrefetchScalarGridSpec(
            num_scalar_prefetch=1,
            grid=(t // tb, N_EXPERTS + comb_steps),
            in_specs=[pl.BlockSpec((tb, D_MODEL), lambda i, s, c: (i, 0)),
                      pl.BlockSpec((8, nsb, sbk), lambda i, s, c: (0, i, 0)),
                      r3, r3,
                      pl.BlockSpec((1, D_MODEL, D_EXPERT), wblk),
                      pl.BlockSpec((1, D_MODEL, D_EXPERT), wblk),
                      pl.BlockSpec((1, D_EXPERT, D_MODEL), wblk),
                      pl.BlockSpec((csb * sbk, LANES), oblk),
                      pl.BlockSpec((csb * sbk, D_MODEL), oblk),
                      pl.BlockSpec((1, D_MODEL), lambda i, s, c: (0, 0))],
            out_specs=pl.BlockSpec((csb * sbk, D_MODEL), oblk),
            scratch_shapes=[pltpu.VMEM((nsb, slots, D_MODEL), BF16),
                            pltpu.VMEM((tb, D_MODEL), F32)]),
        out_shape=jax.ShapeDtypeStruct((t, D_MODEL), F32),
        compiler_params=pltpu.CompilerParams(dimension_semantics=("parallel", "arbitrary"),
                                             vmem_limit_bytes=MOE_VMEM_LIMIT),
        name="moe",
    )(tab, h2, qrow3, pos3, w3, wg, wu, wd, qcol, x1, gg2)


def _slot_tables(cnt, *, slots):
    need = ((jnp.max(cnt, axis=1) + 15) // 16) * 16
    start = jnp.cumsum(need, axis=1) - need
    rows = jnp.clip(slots - start, 0, need)
    over = jnp.max(jnp.maximum(cnt - rows[:, None, :], 0), axis=1)
    return rows, start, over


def _pos_tables(rows):
    quarter = D_MODEL // 4
    freq = 1.0 / (POS_BASE ** (np.arange(quarter, dtype=np.float64) / quarter))
    r = np.arange(rows, dtype=np.float64)[:, None] * freq
    cl = np.arange(GRID_W, dtype=np.float64)[:, None] * freq
    er = np.concatenate([np.sin(r), np.cos(r)], axis=-1).astype(np.float32)
    ec = np.concatenate([np.sin(cl), np.cos(cl)], axis=-1).astype(np.float32)
    return jnp.asarray(er[:, None, :]), jnp.asarray(ec[None, :, :])


def _dft_tables(t):
    n = np.arange(CHUNK, dtype=np.int64)
    prod = n[:, None] * n[None, :]
    ang = (prod % CHUNK).astype(np.float64) * (2.0 * np.pi / CHUNK)
    c, s = np.cos(ang), np.sin(ang)
    f1 = np.concatenate([c, s], axis=0)
    a2 = np.concatenate([np.concatenate([c, -s], axis=1), np.concatenate([s, c], axis=1)], axis=0)
    cs = np.concatenate([c, -s], axis=0)
    angw = prod.astype(np.float64) * (2.0 * np.pi / t)
    f32 = lambda a: jnp.asarray(a.astype(np.float32))
    return f32(f1), f32(a2), f32(cs), f32(np.cos(angw)[:, :, None]), f32(np.sin(angw)[:, :, None])


def _blockdiag(w):
    n = w.shape[0]
    size = n * QKV_BLOCK
    spread = np.tile(np.eye(QKV_BLOCK, dtype=np.float32), (1, n))
    rows = jnp.dot(w.reshape(size, QKV_BLOCK), jnp.asarray(spread), precision=lax.Precision.HIGHEST)
    blk = np.arange(size) // QKV_BLOCK
    mask = (blk[:, None] == blk[None, :]).astype(np.float32)
    return rows * jnp.asarray(mask)


def _gate_weights(w_f, b_f, w_b, b_b):
    w = jnp.concatenate([w_f[:, :HEADS], w_b[:, :HEADS], w_f[:, HEADS:], w_b[:, HEADS:]], axis=1).T
    b = jnp.concatenate([b_f[:HEADS], b_b[:HEADS], b_f[HEADS:], b_b[HEADS:]])
    return w.astype(BF16), b[:, None]


def kernel(x, c, ctx, c_ctx, w_ada, b_ada, g_pre_mix, g_post_mix, g_pre_ffn, g_post_ffn,
           w_in, conv_w, conv_b, w_q, w_k, w_v, w_if_fwd, b_if_fwd, w_if_bwd, b_if_bwd,
           mlstm_norm_w, mlstm_skip, w_fourier, w_out, w_router_group, b_router_group,
           w_router_expert, b_router_expert, w_gate, w_up, w_down):
    t = x.shape[1]
    rows = t // GRID_W

    c8 = jnp.concatenate([c, c_ctx[None, :], jnp.zeros((6, D_MODEL), F32)], axis=0)
    mod = _ada(c8, w_ada[0], b_ada[0][None, :])
    shift1, scale1, gate1, shift2, scale2, gate2 = [mod[0:1, k * D_MODEL:(k + 1) * D_MODEL] for k in range(N_MOD)]
    shift1c, scale1c = mod[1:2, 0:D_MODEL], mod[1:2, D_MODEL:2 * D_MODEL]
    g1 = g_pre_mix[0][None, :] * (1.0 + scale1)
    g1c = g_pre_mix[0][None, :] * (1.0 + scale1c)
    gg1 = g_post_mix[0][None, :] * gate1
    g2 = g_pre_ffn[0][None, :] * (1.0 + scale2)
    gg2 = g_post_ffn[0][None, :] * gate2

    er3, ec3 = _pos_tables(rows)
    x3 = x.reshape(rows, GRID_W, D_MODEL)
    ctx3 = ctx.reshape(CTX_LEN // GRID_W, GRID_W, D_MODEL)
    w_in_bf = w_in[0].astype(BF16)

    xm_l, z_l, u_l = _inproj(x3, er3, ec3, g1, shift1, w_in_bf, rows=8, add_pos=True)
    xm_c, _, _ = _inproj(ctx3, er3, ec3, g1c, shift1c, w_in_bf, rows=CTX_LEN // GRID_W, add_pos=False)

    wq = _blockdiag(w_q[0]).astype(BF16)
    wkt = _blockdiag(w_k[0]).T.astype(BF16)
    wv = _blockdiag(w_v[0]).astype(BF16)
    wi, bi = _gate_weights(w_if_fwd[0], b_if_fwd[0], w_if_bwd[0], b_if_bwd[0])
    wiq, wik, wiv = wi[:, :D_MLSTM], wi[:, D_MLSTM:2 * D_MLSTM], wi[:, 2 * D_MLSTM:]
    cb = conv_b[0][None, :]
    q_l, kt_l, v_l, act_l, gp_l = _feat(xm_l, conv_w[0], cb, wq, wkt, wv, wiq, wik, wiv, bi, tm=512)
    q_c, kt_c, v_c, _, gp_c = _feat(xm_c, conv_w[0], cb, wq, wkt, wv, wiq, wik, wiv, bi, tm=CTX_LEN)

    dr_l, gc_l = _gates(gp_l, tl=2048)
    dr_c, gc_c = _gates(gp_c, tl=CTX_LEN)

    c0 = jnp.zeros((2 * HEADS, DH, 2 * DH), F32)
    m0 = jnp.zeros((2 * HEADS, 8, LANES), F32)
    c_ctx_fin, m_ctx_fin = _mlstm(q_c, kt_c, v_c, gc_c, dr_c, c0, m0, cps=CTX_LEN // CHUNK, emit=False)
    hf, hb, _, _ = _mlstm(q_l, kt_l, v_l, gc_l, dr_l, c_ctx_fin, m_ctx_fin, cps=2, emit=True)

    f1, a2, cs, cw3, sw3 = _dft_tables(t)
    yc, ys = _dft1(u_l.reshape(CHUNK, (t // CHUNK) * D_FOURIER), f1.astype(BF16), tn=2048)
    yf = _dft2(yc.reshape(t, D_FOURIER), ys.reshape(t, D_FOURIER), cw3, sw3,
               a2.astype(BF16), cs.astype(BF16), w_fourier[0].astype(BF16), kb=4)
    yf = yf.reshape(t, D_FOURIER)

    wr = jnp.concatenate([w_router_group[0], w_router_expert[0],
                          jnp.zeros((D_MODEL, LANES - N_GROUPS - N_EXPERTS), F32)], axis=1)
    br = jnp.concatenate([b_router_group[0], b_router_expert[0],
                          jnp.zeros((LANES - N_GROUPS - N_EXPERTS,), F32)])[None, :]
    x1, h2, lg = _merge(hf, hb, act_l, z_l, yf, x3, er3, ec3,
                        mlstm_norm_w[0][None, :], mlstm_skip[0][None, :], w_out[0].astype(BF16),
                        gg1, g2, shift2, wr, br, rows=8)
    pos, wts, cnt = _route(lg, tl=MOE_TB, sbk=MOE_SBK)
    nblk, nsb = t // MOE_TB, MOE_TB // MOE_SBK
    cnt = cnt.reshape(nblk, N_EXPERTS, LANES)[:, :, :nsb]
    cnt = jnp.transpose(cnt, (0, 2, 1)).astype(jnp.int32)
    rows, start, over = _slot_tables(cnt, slots=MOE_SLOTS)
    tab = jnp.concatenate([a.reshape(-1) for a in (cnt, rows, start, over)])
    off_r = nblk * nsb * N_EXPERTS
    off_o = off_r + nblk * N_EXPERTS
    off_f = off_o + nblk * N_EXPERTS
    qrow, qcol = _slots(tab, pos, wts, tl=MOE_TB, off_r=off_r, off_o=off_o)
    sub = (t // MOE_SBK, MOE_SBK)
    out = _moe(tab, h2, qrow.reshape(8, *sub), pos.reshape(N_EXPERTS, *sub), wts.reshape(N_EXPERTS, *sub),
               w_gate[0].astype(BF16), w_up[0].astype(BF16), w_down[0].astype(BF16),
               qcol, x1, gg2, tb=MOE_TB, sbk=MOE_SBK, slots=MOE_SLOTS, csb=MOE_CSB,
               off_r=off_r, off_o=off_o, off_f=off_f)
    return out[None]
```

```python
import functools

import numpy as np
import jax
import jax.numpy as jnp
from jax import lax
from jax.experimental import pallas as pl
from jax.experimental.pallas import tpu as pltpu

F32 = jnp.float32
BF16 = jnp.bfloat16

D_MODEL = 1024
SEQ = 16384
GRID_W = 64
CTX_LEN = 256
D_MLSTM = 512
HEADS = 4
DH = 128
QKV_BLOCK = 4
CONV_K = 3
CHUNK = 128
D_FOURIER = 512
FGROUPS = 4
FCG = 128
N_GROUPS = 4
EPG = 4
N_EXPERTS = 16
D_EXPERT = 512
N_MOD = 6
EPS = 1e-6
POS_BASE = 10000.0
LANES = 128
NEG_BIG = -3.0e38

VMEM_LIMIT = 52 * 1024 * 1024
MOE_VMEM_LIMIT = 56 * 1024 * 1024
MOE_TB = 2048
MOE_SBK = 256
MOE_SLOTS = 1024
MOE_CSB = 1


def _cparams(*sem):
    return pltpu.CompilerParams(dimension_semantics=sem, vmem_limit_bytes=VMEM_LIMIT)


def _dot(a, b):
    return jnp.dot(a, b, preferred_element_type=F32)


def _dot_nt(a, b):
    return lax.dot_general(a, b, (((1,), (1,)), ((), ())), preferred_element_type=F32)


def _split_bf16(a):
    hi = a.astype(BF16)
    lo = (a - hi.astype(F32)).astype(BF16)
    return hi, lo


def _dot3(a, b):
    a_hi, a_lo = _split_bf16(a)
    b_hi, b_lo = _split_bf16(b)
    return _dot(a_hi, b_hi) + (_dot(a_hi, b_lo) + _dot(a_lo, b_hi))


def _sigmoid(x):
    return 1.0 / (1.0 + jnp.exp(-x))


def _rms(x):
    return x * lax.rsqrt(jnp.mean(x * x, axis=-1, keepdims=True) + EPS)


def _ada_kernel(c_ref, w_ref, b_ref, o_ref):
    c = c_ref[...]
    s = c * _sigmoid(c)
    o_ref[...] = _dot3(s, w_ref[...]) + b_ref[...]


def _ada(c8, w, b):
    n = w.shape[1]
    tn = 768
    return pl.pallas_call(
        _ada_kernel,
        grid=(n // tn,),
        in_specs=[pl.BlockSpec((8, D_MODEL), lambda j: (0, 0)),
                  pl.BlockSpec((D_MODEL, tn), lambda j: (0, j)),
                  pl.BlockSpec((1, tn), lambda j: (0, j))],
        out_specs=pl.BlockSpec((8, tn), lambda j: (0, j)),
        out_shape=jax.ShapeDtypeStruct((8, n), F32),
        compiler_params=_cparams("parallel"),
        name="ada",
    )(c8, w, b)


def _add_pos(x3, er_ref, ec_ref):
    r = x3.shape[0]
    pr = jnp.broadcast_to(er_ref[...], (r, GRID_W, D_MODEL // 2))
    pc = jnp.broadcast_to(ec_ref[...], (r, GRID_W, D_MODEL // 2))
    return x3 + jnp.concatenate([pr, pc], axis=-1)


def _inproj_kernel(x_ref, er_ref, ec_ref, g_ref, sh_ref, w_ref, xm_ref, z_ref, u_ref, *, add_pos):
    x3 = x_ref[...]
    if add_pos:
        x3 = _add_pos(x3, er_ref, ec_ref)
    x = x3.reshape(x3.shape[0] * GRID_W, D_MODEL)
    h = _rms(x) * g_ref[...] + sh_ref[...]
    proj = _dot(h.astype(BF16), w_ref[...])
    xm_ref[...] = proj[:, :D_MLSTM].astype(BF16)
    z_ref[...] = proj[:, D_MLSTM:2 * D_MLSTM].astype(BF16)
    for g in range(FGROUPS):
        u_ref[g] = proj[:, 2 * D_MLSTM + g * FCG:2 * D_MLSTM + (g + 1) * FCG]


def _inproj(x3, er3, ec3, g_eff, shift, w_in, *, rows, add_pos):
    nr = x3.shape[0]
    t = nr * GRID_W
    tm = rows * GRID_W
    out = jax.ShapeDtypeStruct((t, D_MLSTM), BF16)
    ospec = pl.BlockSpec((tm, D_MLSTM), lambda i: (i, 0))
    vec = pl.BlockSpec((1, D_MODEL), lambda i: (0, 0))
    return pl.pallas_call(
        functools.partial(_inproj_kernel, add_pos=add_pos),
        grid=(nr // rows,),
        in_specs=[pl.BlockSpec((rows, GRID_W, D_MODEL), lambda i: (i, 0, 0)),
                  pl.BlockSpec((rows, 1, D_MODEL // 2), lambda i: (i, 0, 0)),
                  pl.BlockSpec((1, GRID_W, D_MODEL // 2), lambda i: (0, 0, 0)),
                  vec, vec,
                  pl.BlockSpec(w_in.shape, lambda i: (0, 0))],
        out_specs=[ospec, ospec, pl.BlockSpec((FGROUPS, tm, FCG), lambda i: (0, i, 0))],
        out_shape=[out, out, jax.ShapeDtypeStruct((FGROUPS, t, FCG), F32)],
        compiler_params=_cparams("parallel"),
        name="inproj",
    )(x3, er3, ec3, g_eff, shift, w_in)


def _feat_kernel(xm_ref, prev_ref, next_ref, cw_ref, cb_ref, wq_ref, wkt_ref, wv_ref,
                 wiq_ref, wik_ref, wiv_ref, bi_ref,
                 q_ref, kt_ref, v_ref, act_ref, g_ref):
    i = pl.program_id(0)
    n = pl.num_programs(0)
    xm_bf = xm_ref[...]
    xm = xm_bf.astype(F32)
    tm = xm.shape[0]
    prev_row = prev_ref[...].astype(F32)[15:16, :] * jnp.where(i > 0, 1.0, 0.0)
    next_row = next_ref[...].astype(F32)[0:1, :] * jnp.where(i < n - 1, 1.0, 0.0)
    rid = lax.broadcasted_iota(jnp.int32, (tm, 1), 0)
    x_left = jnp.where(rid == 0, prev_row, pltpu.roll(xm, 1, 0))
    x_right = jnp.where(rid == tm - 1, next_row, pltpu.roll(xm, tm - 1, 0))
    cw = cw_ref[...]
    y = cw[0:1] * x_left + cw[1:2] * xm + cw[2:3] * x_right + cb_ref[...]
    act = (y * _sigmoid(y)).astype(BF16)
    act_ref[...] = act
    q = _dot(act, wq_ref[...])
    kt = _dot_nt(wkt_ref[...], act)
    v = _dot(xm_bf, wv_ref[...])
    q_bf = q.astype(BF16)
    kt_bf = kt.astype(BF16)
    v_bf = v.astype(BF16)
    q_ref[...] = (q * (DH ** -0.5)).astype(BF16)
    kt_ref[...] = kt_bf
    v_ref[...] = v_bf
    g = _dot_nt(wiq_ref[...], q_bf) + _dot(wik_ref[...], kt_bf) + _dot_nt(wiv_ref[...], v_bf)
    g_ref[...] = g + bi_ref[...]


def _feat(xm, conv_w, conv_b, wq, wkt, wv, wiq, wik, wiv, bi, *, tm):
    t = xm.shape[0]
    nb16 = t // 16
    k16 = tm // 16
    full = lambda a: pl.BlockSpec(a.shape, lambda i: (0,) * a.ndim)
    tok = pl.BlockSpec((tm, D_MLSTM), lambda i: (i, 0))
    return pl.pallas_call(
        _feat_kernel,
        grid=(t // tm,),
        in_specs=[tok,
                  pl.BlockSpec((16, D_MLSTM), lambda i: (jnp.maximum(i * k16 - 1, 0), 0)),
                  pl.BlockSpec((16, D_MLSTM), lambda i: (jnp.minimum((i + 1) * k16, nb16 - 1), 0)),
                  full(conv_w), full(conv_b), full(wq), full(wkt), full(wv),
                  full(wiq), full(wik), full(wiv), full(bi)],
        out_specs=[tok,
                   pl.BlockSpec((D_MLSTM, tm), lambda i: (0, i)),
                   tok, tok,
                   pl.BlockSpec((16, tm), lambda i: (0, i))],
        out_shape=[jax.ShapeDtypeStruct((t, D_MLSTM), BF16),
                   jax.ShapeDtypeStruct((D_MLSTM, t), BF16),
                   jax.ShapeDtypeStruct((t, D_MLSTM), BF16),
                   jax.ShapeDtypeStruct((t, D_MLSTM), BF16),
                   jax.ShapeDtypeStruct((16, t), F32)],
        compiler_params=_cparams("parallel"),
        name="feat",
    )(xm, xm, xm, conv_w, conv_b, wq, wkt, wv, wiq, wik, wiv, bi)


def _gates_kernel(g_ref, d_ref, gc_ref):
    g = g_ref[...]
    tl = g.shape[1]
    ig = g[0:8]
    fg = g[8:16]
    lf = jnp.minimum(fg, 0.0) - jnp.log(1.0 + jnp.exp(-jnp.abs(fg)))
    pos = lax.broadcasted_iota(jnp.int32, (8, tl), 1) & (CHUNK - 1)
    is_fwd = lax.broadcasted_iota(jnp.int32, (8, tl), 0) < HEADS

    def scan(x, op, ident):
        xf = x
        xb = x
        k = 1
        while k < CHUNK:
            xf = op(xf, jnp.where(pos >= k, pltpu.roll(xf, k, 1), ident))
            xb = op(xb, jnp.where(pos < CHUNK - k, pltpu.roll(xb, tl - k, 1), ident))
            k *= 2
        return jnp.where(is_fwd, xf, xb)

    b = scan(lf, jnp.add, 0.0)
    d = ig - b
    mloc = b + scan(d, jnp.maximum, NEG_BIG)
    d_ref[...] = d
    stack = jnp.concatenate([b, mloc, jnp.zeros((LANES - 16, tl), F32)], axis=0)
    gc_ref[...] = stack.T


def _gates(g, *, tl):
    t = g.shape[1]
    return pl.pallas_call(
        _gates_kernel,
        grid=(t // tl,),
        in_specs=[pl.BlockSpec((16, tl), lambda i: (0, i))],
        out_specs=[pl.BlockSpec((8, tl), lambda i: (0, i)),
                   pl.BlockSpec((tl, LANES), lambda i: (i, 0))],
        out_shape=[jax.ShapeDtypeStruct((8, t), F32),
                   jax.ShapeDtypeStruct((t, LANES), F32)],
        compiler_params=_cparams("parallel"),
        name="gates",
    )(g)


def _mlstm_kernel(*refs, cps, emit):
    (qf_ref, kf_ref, vf_ref, gcf_ref, drf_ref,
     qb_ref, kb_ref, vb_ref, gcb_ref, drb_ref, c0_ref, m0_ref) = refs[:12]
    if emit:
        hf_ref, hb_ref, cfin_ref, mfin_ref, c_scr, m_scr = refs[12:]
    else:
        cfin_ref, mfin_ref, c_scr, m_scr = refs[12:]
        hf_ref = hb_ref = None
    step = pl.program_id(0)

    @pl.when(step == 0)
    def _():
        c_scr[...] = c0_ref[...]
        m_scr[...] = m0_ref[...]

    ti = lax.broadcasted_iota(jnp.int32, (CHUNK, CHUNK), 0)
    si = lax.broadcasted_iota(jnp.int32, (CHUNK, CHUNK), 1)
    ones_col = jnp.where(si == 0, 1.0, 0.0).astype(BF16)

    for j in range(cps):
        for d in range(2):
            if d == 0:
                q_ref, k_ref, v_ref, gc_ref, dr_ref, h_ref = qf_ref, kf_ref, vf_ref, gcf_ref, drf_ref, hf_ref
                r0, mask, last = j * CHUNK, si <= ti, CHUNK - 1
            else:
                q_ref, k_ref, v_ref, gc_ref, dr_ref, h_ref = qb_ref, kb_ref, vb_ref, gcb_ref, drb_ref, hb_ref
                r0, mask, last = (cps - 1 - j) * CHUNK, si >= ti, 0
            rows = slice(r0, r0 + CHUNK)
            gc = gc_ref[rows, :]
            dr = dr_ref[:, rows]
            for hd in range(HEADS):
                r = d * HEADS + hd
                cols = slice(hd * DH, (hd + 1) * DH)
                b = gc[:, r:r + 1]
                mloc = gc[:, 8 + r:9 + r]
                drow = dr[r:r + 1, :]
                m0 = m_scr[r][0:1, 0:1]
                c_aug = c_scr[r]
                kt = k_ref[cols, rows]
                vaug = jnp.concatenate([v_ref[rows, cols], ones_col], axis=1)
                if emit:
                    qh = q_ref[rows, cols]
                    bm = b + m0
                    m_t = jnp.maximum(bm, mloc)
                    dmat = jnp.where(mask, jnp.exp((b - m_t) + drow), 0.0)
                    smat = (_dot(qh, kt) * dmat).astype(BF16)
                    num = _dot(smat, vaug) + jnp.exp(bm - m_t) * _dot(qh, c_aug.astype(BF16))
                    den = jnp.maximum(jnp.abs(num[:, DH:DH + 1]), jnp.exp(-m_t))
                    h_ref[rows, cols] = (num[:, :DH] / den).astype(h_ref.dtype)
                btot = gc[last:last + 1, r:r + 1]
                amax = gc[last:last + 1, 8 + r:9 + r]
                m_new = jnp.maximum(btot + m0, amax)
                decay = jnp.exp(btot + m0 - m_new)
                kw = (kt.astype(F32) * jnp.exp(btot + drow - m_new)).astype(BF16)
                c_scr[r] = decay * c_aug + _dot(kw, vaug)
                m_scr[r] = jnp.broadcast_to(m_new, (8, LANES))

    @pl.when(step == pl.num_programs(0) - 1)
    def _():
        cfin_ref[...] = c_scr[...]
        mfin_ref[...] = m_scr[...]


def _mlstm(q, kt, v, gc, dr, c0, m0, *, cps, emit):
    t = q.shape[0]
    cb = cps * CHUNK
    nb = t // cb
    fwd_r = lambda i: (i, 0)
    bwd_r = lambda i: (nb - 1 - i, 0)
    fwd_c = lambda i: (0, i)
    bwd_c = lambda i: (0, nb - 1 - i)
    tok = lambda f: pl.BlockSpec((cb, D_MLSTM), f)
    in_specs = []
    for fr, fc in ((fwd_r, fwd_c), (bwd_r, bwd_c)):
        in_specs += [tok(fr), pl.BlockSpec((D_MLSTM, cb), fc), tok(fr),
                     pl.BlockSpec((cb, LANES), fr), pl.BlockSpec((8, cb), fc)]
    cshape = (2 * HEADS, DH, 2 * DH)
    mshape = (2 * HEADS, 8, LANES)
    cspec = pl.BlockSpec(cshape, lambda i: (0, 0, 0))
    mspec = pl.BlockSpec(mshape, lambda i: (0, 0, 0))
    in_specs += [cspec, mspec]
    out_specs = [cspec, mspec]
    out_shape = [jax.ShapeDtypeStruct(cshape, F32), jax.ShapeDtypeStruct(mshape, F32)]
    if emit:
        out_specs = [tok(fwd_r), tok(bwd_r)] + out_specs
        out_shape = [jax.ShapeDtypeStruct((t, D_MLSTM), BF16)] * 2 + out_shape
    return pl.pallas_call(
        functools.partial(_mlstm_kernel, cps=cps, emit=emit),
        grid=(nb,),
        in_specs=in_specs,
        out_specs=out_specs,
        out_shape=out_shape,
        scratch_shapes=[pltpu.VMEM(cshape, F32), pltpu.VMEM(mshape, F32)],
        compiler_params=_cparams("arbitrary"),
        name="mlstm",
    )(q, kt, v, gc, dr, q, kt, v, gc, dr, c0, m0)


FFT_ROWS = 8


def _dft1_kernel(u_ref, f_ref, yc_ref, ys_ref):
    f = f_ref[...]
    rows = CHUNK * FFT_ROWS
    u2 = u_ref.reshape(FGROUPS * rows, FCG)
    yc2 = yc_ref.reshape(FGROUPS * rows, FCG)
    ys2 = ys_ref.reshape(FGROUPS * rows, FCG)
    for s in range(FFT_ROWS):
        pick = [pl.ds(g * rows + s, CHUNK, stride=FFT_ROWS) for g in range(FGROUPS)]
        x = jnp.concatenate([u2[p, :] for p in pick], axis=1).astype(BF16)
        y = _dot(f, x)
        for g in range(FGROUPS):
            yc2[pick[g], :] = y[:CHUNK, g * FCG:(g + 1) * FCG]
            ys2[pick[g], :] = y[CHUNK:, g * FCG:(g + 1) * FCG]


def _dft1(u4, f1):
    blk = pl.BlockSpec((FGROUPS, CHUNK, FFT_ROWS, FCG), lambda j: (0, 0, j, 0))
    out = jax.ShapeDtypeStruct(u4.shape, F32)
    return pl.pallas_call(
        _dft1_kernel,
        grid=(u4.shape[2] // FFT_ROWS,),
        in_specs=[blk, pl.BlockSpec(f1.shape, lambda j: (0, 0))],
        out_specs=[blk, blk],
        out_shape=[out, out],
        compiler_params=_cparams("parallel"),
        name="dft1",
    )(u4, f1)


def _dft2_kernel(yc_ref, ys_ref, cw_ref, sw_ref, a2_ref, mix_ref, o_ref):
    a2 = a2_ref[...]
    rows = CHUNK * FFT_ROWS
    o2 = o_ref.reshape(FGROUPS * rows, FCG)
    for kk in range(FFT_ROWS):
        blk = slice(kk * CHUNK, (kk + 1) * CHUNK)
        yc = jnp.concatenate([yc_ref[g, blk, :] for g in range(FGROUPS)], axis=1)
        ys = jnp.concatenate([ys_ref[g, blk, :] for g in range(FGROUPS)], axis=1)
        cw = cw_ref[kk]
        sw = sw_ref[kk]
        p = jnp.concatenate([yc * cw - ys * sw, yc * sw + ys * cw], axis=0).astype(BF16)
        x = _dot(a2, p).astype(BF16)
        for g in range(FGROUPS):
            cols = slice(g * FCG, (g + 1) * FCG)
            cat = jnp.concatenate([x[:CHUNK, cols], x[CHUNK:, cols]], axis=1)
            o2[pl.ds(g * rows + kk, CHUNK, stride=FFT_ROWS), :] = _dot(cat, mix_ref[g])


def _dft2(yc, ys, cw3, sw3, a2, mix):
    t = yc.shape[1]
    n1 = t // CHUNK
    tok = pl.BlockSpec((FGROUPS, FFT_ROWS * CHUNK, FCG), lambda i: (0, i, 0))
    tw = pl.BlockSpec((FFT_ROWS, CHUNK, 1), lambda i: (i, 0, 0))
    full = lambda a: pl.BlockSpec(a.shape, lambda i: (0,) * a.ndim)
    return pl.pallas_call(
        _dft2_kernel,
        grid=(n1 // FFT_ROWS,),
        in_specs=[tok, tok, tw, tw, full(a2), full(mix)],
        out_specs=pl.BlockSpec((FGROUPS, CHUNK, FFT_ROWS, FCG), lambda i: (0, 0, i, 0)),
        out_shape=jax.ShapeDtypeStruct((FGROUPS, CHUNK, n1, FCG), F32),
        compiler_params=_cparams("parallel"),
        name="dft2",
    )(yc, ys, cw3, sw3, a2, mix)


def _merge_kernel(hf_ref, hb_ref, act_ref, z_ref, yf_ref, x_ref, er_ref, ec_ref,
                  nw_ref, sk_ref, wout_ref, gg1_ref, g2_ref, sh2_ref, wr_ref, br_ref,
                  x1_ref, h2_ref, lg_ref):
    h = hf_ref[...].astype(F32) + hb_ref[...].astype(F32)
    parts = []
    for hd in range(HEADS):
        hh = h[:, hd * DH:(hd + 1) * DH]
        dl = hh - jnp.mean(hh, axis=-1, keepdims=True)
        var = jnp.mean(dl * dl, axis=-1, keepdims=True)
        parts.append(dl * lax.rsqrt(var + EPS))
    hn = jnp.concatenate(parts, axis=-1)
    z = z_ref[...].astype(F32)
    m = (hn * nw_ref[...] + sk_ref[...] * act_ref[...].astype(F32)) * (z * _sigmoid(z))
    cat = jnp.concatenate([m.astype(BF16)] + [yf_ref[g].astype(BF16) for g in range(FGROUPS)], axis=-1)
    y = _dot(cat, wout_ref[...])
    x3 = _add_pos(x_ref[...], er_ref, ec_ref)
    xp = x3.reshape(x3.shape[0] * GRID_W, D_MODEL)
    x1 = xp + _rms(y) * gg1_ref[...]
    x1_ref[...] = x1
    h2 = _rms(x1) * g2_ref[...] + sh2_ref[...]
    h2_ref[...] = h2.astype(BF16)
    lg = _dot3(h2, wr_ref[...]) + br_ref[...]
    lg_ref[...] = lg.T[:32]


def _merge(hf, hb, act, z, yf, x3, er3, ec3, nw, sk, wout, gg1, g2, sh2, wr, br, *, rows):
    nr = x3.shape[0]
    t = nr * GRID_W
    tm = rows * GRID_W
    tok = pl.BlockSpec((tm, D_MLSTM), lambda i: (i, 0))
    full = lambda a: pl.BlockSpec(a.shape, lambda i: (0,) * a.ndim)
    return pl.pallas_call(
        _merge_kernel,
        grid=(nr // rows,),
        in_specs=[tok, tok, tok, tok,
                  pl.BlockSpec((FGROUPS, tm, FCG), lambda i: (0, i, 0)),
                  pl.BlockSpec((rows, GRID_W, D_MODEL), lambda i: (i, 0, 0)),
                  pl.BlockSpec((rows, 1, D_MODEL // 2), lambda i: (i, 0, 0)),
                  pl.BlockSpec((1, GRID_W, D_MODEL // 2), lambda i: (0, 0, 0)),
                  full(nw), full(sk), full(wout), full(gg1), full(g2), full(sh2), full(wr), full(br)],
        out_specs=[pl.BlockSpec((tm, D_MODEL), lambda i: (i, 0)),
                   pl.BlockSpec((tm, D_MODEL), lambda i: (i, 0)),
                   pl.BlockSpec((32, tm), lambda i: (0, i))],
        out_shape=[jax.ShapeDtypeStruct((t, D_MODEL), F32),
                   jax.ShapeDtypeStruct((t, D_MODEL), BF16),
                   jax.ShapeDtypeStruct((32, t), F32)],
        compiler_params=_cparams("parallel"),
        name="merge",
    )(hf, hb, act, z, yf, x3, er3, ec3, nw, sk, wout, gg1, g2, sh2, wr, br)


def _route_kernel(lg_ref, pos_ref, w_ref, cnt_ref, *, sbk):
    lg = lg_ref[...]
    tl = lg.shape[1]
    g = [lg[j:j + 1] for j in range(N_GROUPS)]
    e = [lg[N_GROUPS + j:N_GROUPS + j + 1] for j in range(N_EXPERTS)]
    gmax = jnp.maximum(jnp.maximum(g[0], g[1]), jnp.maximum(g[2], g[3]))
    den = jnp.exp(g[0] - gmax) + jnp.exp(g[1] - gmax) + jnp.exp(g[2] - gmax) + jnp.exp(g[3] - gmax)
    p_sel = 1.0 / den
    sel = []
    free = jnp.ones((1, tl), F32)
    for j in range(N_GROUPS):
        s = jnp.where(g[j] >= gmax, free, 0.0)
        sel.append(s)
        free = free - s
    es = []
    for j in range(EPG):
        es.append(sel[0] * e[j] + sel[1] * e[EPG + j] + sel[2] * e[2 * EPG + j] + sel[3] * e[3 * EPG + j])
    rank = []
    for j in range(EPG):
        rj = jnp.zeros((1, tl), F32)
        for i in range(EPG):
            if i == j:
                continue
            beats = (es[i] >= es[j]) if i < j else (es[i] > es[j])
            rj = rj + jnp.where(beats, 1.0, 0.0)
        rank.append(rj)
    v1 = jnp.maximum(jnp.maximum(es[0], es[1]), jnp.maximum(es[2], es[3]))
    v2 = sum(jnp.where(rank[j] == 1.0, es[j], 0.0) for j in range(EPG))
    tt = jnp.exp(v2 - v1)
    w1 = p_sel / (1.0 + tt)
    w2 = w1 * tt
    w = [jnp.where(rank[j] == 0.0, w1, jnp.where(rank[j] == 1.0, w2, 0.0)) for j in range(EPG)]
    top2 = [jnp.where(rank[j] < 2.0, 1.0, 0.0) for j in range(EPG)]
    mem = jnp.concatenate([sel[gi] * top2[j] for gi in range(N_GROUPS) for j in range(EPG)], axis=0)
    wts = jnp.concatenate([sel[gi] * w[j] for gi in range(N_GROUPS) for j in range(EPG)], axis=0)
    w_ref[...] = wts
    lane = lax.broadcasted_iota(jnp.int32, (N_EXPERTS, tl), 1) & (sbk - 1)
    c = mem
    k = 1
    while k < sbk:
        c = c + jnp.where(lane >= k, pltpu.roll(c, k, 1), 0.0)
        k *= 2
    pos_ref[...] = jnp.where(mem > 0.0, c - 1.0, -1.0)
    lane128 = lax.broadcasted_iota(jnp.int32, (N_EXPERTS, LANES), 1)
    cnt = jnp.zeros((N_EXPERTS, LANES), F32)
    for kb in range(tl // sbk):
        tot = jnp.sum(mem[:, kb * sbk:(kb + 1) * sbk], axis=1, keepdims=True)
        cnt = cnt + jnp.where(lane128 == kb, tot, 0.0)
    cnt_ref[...] = cnt


def _route(lg, *, tl, sbk):
    t = lg.shape[1]
    row = pl.BlockSpec((N_EXPERTS, tl), lambda i: (0, i))
    return pl.pallas_call(
        functools.partial(_route_kernel, sbk=sbk),
        grid=(t // tl,),
        in_specs=[pl.BlockSpec((32, tl), lambda i: (0, i))],
        out_specs=[row, row, pl.BlockSpec((N_EXPERTS, LANES), lambda i: (i, 0))],
        out_shape=[jax.ShapeDtypeStruct((N_EXPERTS, t), F32),
                   jax.ShapeDtypeStruct((N_EXPERTS, t), F32),
                   jax.ShapeDtypeStruct((t // tl * N_EXPERTS, LANES), F32)],
        compiler_params=_cparams("parallel"),
        name="route",
    )(lg)


def _slots_kernel(tab_ref, pos_ref, w_ref, qrow_ref, qcol_ref, *, off_r, off_o):
    i = pl.program_id(0)
    tl = pos_ref.shape[1]
    q0 = jnp.full((1, tl), -1.0, F32)
    q1 = jnp.full((1, tl), -1.0, F32)
    w0 = jnp.zeros((1, tl), F32)
    w1 = jnp.zeros((1, tl), F32)
    seen = jnp.zeros((1, tl), F32)
    for ex in range(N_EXPERTS):
        rk = pos_ref[ex:ex + 1, :]
        wt = w_ref[ex:ex + 1, :]
        rows = tab_ref[off_r + i * N_EXPERTS + ex].astype(F32)
        first_row = tab_ref[off_o + i * N_EXPERTS + ex].astype(F32)
        m = jnp.where(rk >= 0.0, 1.0, 0.0)
        val = jnp.where(rk < rows, rk + first_row, -1.0)
        first = (m * (1.0 - seen)) > 0.0
        second = (m * seen) > 0.0
        q0 = jnp.where(first, val, q0)
        w0 = jnp.where(first, wt, w0)
        q1 = jnp.where(second, val, q1)
        w1 = jnp.where(second, wt, w1)
        seen = seen + m
    qrow_ref[...] = jnp.concatenate([q0, q1, jnp.zeros((6, tl), F32)], axis=0)
    qcol_ref[...] = jnp.concatenate([q0, q1, w0, w1, jnp.zeros((LANES - 4, tl), F32)], axis=0).T


def _slots(tab, pos, w, *, tl, off_r, off_o):
    t = pos.shape[1]
    row = pl.BlockSpec((N_EXPERTS, tl), lambda i, c: (0, i))
    return pl.pallas_call(
        functools.partial(_slots_kernel, off_r=off_r, off_o=off_o),
        grid_spec=pltpu.PrefetchScalarGridSpec(
            num_scalar_prefetch=1,
            grid=(t // tl,),
            in_specs=[row, row],
            out_specs=[pl.BlockSpec((8, tl), lambda i, c: (0, i)),
                       pl.BlockSpec((tl, LANES), lambda i, c: (i, 0))]),
        out_shape=[jax.ShapeDtypeStruct((8, t), F32),
                   jax.ShapeDtypeStruct((t, LANES), F32)],
        compiler_params=_cparams("parallel"),
        name="slots",
    )(tab, pos, w)


def _mlp(x, wg, wu, wd):
    gt = _dot(x, wg)
    a = ((gt * _sigmoid(gt)) * _dot(x, wu)).astype(BF16)
    return _dot(a, wd).astype(BF16)


def _moe_kernel(tab_ref, h_ref, qrow_ref, pos_ref, w_ref, wg_ref, wu_ref, wd_ref,
                qcol_ref, x1_ref, gg2_ref, o_ref, xs_ref, ovf_ref, *, sbk, slots, csb, off_r, off_o, off_f):
    i = pl.program_id(0)
    step = pl.program_id(1)
    nsb = h_ref.shape[0] // sbk
    e = jnp.minimum(step, N_EXPERTS - 1)
    n_rows = tab_ref[off_r + i * N_EXPERTS + e]
    row_a = tab_ref[off_o + i * N_EXPERTS + e]
    over = tab_ref[off_f + i * N_EXPERTS + e]

    @pl.when(step == 0)
    def _():
        def zero(sb, carry):
            ovf_ref[pl.ds(pl.multiple_of(sb * sbk, sbk), sbk), :] = jnp.zeros((sbk, D_MODEL), F32)
            return carry

        lax.fori_loop(0, nsb, zero, 0)
        pid = lax.broadcasted_iota(jnp.int32, (slots, sbk), 0).astype(F32)

        def select(sb, carry):
            q0 = qrow_ref[0, pl.ds(sb, 1), :]
            q1 = qrow_ref[1, pl.ds(sb, 1), :]
            s = jnp.where(q0 == pid, 1.0, jnp.where(q1 == pid, 1.0, 0.0)).astype(BF16)
            row0 = pl.multiple_of(sb * sbk, sbk)
            xs_ref[sb] = _dot(s, h_ref[pl.ds(row0, sbk), :]).astype(BF16)
            return carry

        lax.fori_loop(0, nsb, select, 0)

    def run_rows(first, n):
        first = pl.multiple_of(first, 16)
        x = jnp.concatenate([xs_ref[sb, pl.ds(first, n), :] for sb in range(nsb)], axis=0)
        y = _mlp(x, wg_ref[0], wu_ref[0], wd_ref[0])
        for sb in range(nsb):
            xs_ref[sb, pl.ds(first, n), :] = y[sb * n:(sb + 1) * n]

    @pl.when(step < N_EXPERTS)
    def _():
        n_pairs = n_rows // 32

        def pair(t, carry):
            run_rows(row_a + t * 32, 32)
            return carry

        lax.fori_loop(0, n_pairs, pair, 0)

        @pl.when(n_rows % 32 != 0)
        def _():
            run_rows(row_a + n_pairs * 32, 16)

    @pl.when(jnp.logical_and(step < N_EXPERTS, over > 0))
    def _():
        rid = lax.broadcasted_iota(jnp.int32, (LANES, sbk), 0).astype(F32)
        wg, wu, wd = wg_ref[0], wu_ref[0], wd_ref[0]

        def sub_block(sb, c1):
            n_over = jnp.maximum(tab_ref[(i * nsb + sb) * N_EXPERTS + e] - n_rows, 0)
            row0 = pl.multiple_of(sb * sbk, sbk)

            def one_pass(s, c2):
                base = (n_rows + s * LANES).astype(F32)
                hit = pos_ref[pl.ds(e, 1), pl.ds(sb, 1), :].reshape(1, sbk) == (rid + base)
                sel = jnp.where(hit, 1.0, 0.0).astype(BF16)
                yo = _mlp(_dot(sel, h_ref[pl.ds(row0, sbk), :]).astype(BF16), wg, wu, wd)
                wct = jnp.where(hit, w_ref[pl.ds(e, 1), pl.ds(sb, 1), :].reshape(1, sbk), 0.0).T
                ovf_ref[pl.ds(row0, sbk), :] += _dot(wct.astype(BF16), yo)
                return c2

            return lax.fori_loop(0, (n_over + LANES - 1) // LANES, one_pass, c1)

        lax.fori_loop(0, nsb, sub_block, 0)

    @pl.when(step >= N_EXPERTS)
    def _():
        lane = lax.broadcasted_iota(jnp.int32, (sbk, slots), 1).astype(F32)
        for k in range(csb):
            sb = (step - N_EXPERTS) * csb + k
            rows = slice(k * sbk, (k + 1) * sbk)
            qc = qcol_ref[rows, :]
            wmat = (jnp.where(lane == qc[:, 0:1], qc[:, 2:3], 0.0)
                    + jnp.where(lane == qc[:, 1:2], qc[:, 3:4], 0.0)).astype(BF16)
            y = _dot(wmat, xs_ref[sb]) + ovf_ref[pl.ds(pl.multiple_of(sb * sbk, sbk), sbk), :]
            o_ref[rows, :] = x1_ref[rows, :] + _rms(y) * gg2_ref[...]


def _moe(tab, h2, qrow3, pos3, w3, wg, wu, wd, qcol, x1, gg2, *, tb, sbk, slots, csb, off_r, off_o, off_f):
    t = h2.shape[0]
    nsb = tb // sbk
    comb_steps = nsb // csb
    wblk = lambda i, s, c: (jnp.minimum(s, N_EXPERTS - 1), 0, 0)
    oblk = lambda i, s, c: (i * comb_steps + jnp.maximum(s - N_EXPERTS, 0), 0)
    r3 = pl.BlockSpec((N_EXPERTS, nsb, sbk), lambda i, s, c: (0, i, 0))
    return pl.pallas_call(
        functools.partial(_moe_kernel, sbk=sbk, slots=slots, csb=csb,
                          off_r=off_r, off_o=off_o, off_f=off_f),
        grid_spec=pltpu.PrefetchScalarGridSpec(
            num_scalar_prefetch=1,
            grid=(t // tb, N_EXPERTS + comb_steps),
            in_specs=[pl.BlockSpec((tb, D_MODEL), lambda i, s, c: (i, 0)),
                      pl.BlockSpec((8, nsb, sbk), lambda i, s, c: (0, i, 0)),
                      r3, r3,
                      pl.BlockSpec((1, D_MODEL, D_EXPERT), wblk),
                      pl.BlockSpec((1, D_MODEL, D_EXPERT), wblk),
                      pl.BlockSpec((1, D_EXPERT, D_MODEL), wblk),
                      pl.BlockSpec((csb * sbk, LANES), oblk),
                      pl.BlockSpec((csb * sbk, D_MODEL), oblk),
                      pl.BlockSpec((1, D_MODEL), lambda i, s, c: (0, 0))],
            out_specs=pl.BlockSpec((csb * sbk, D_MODEL), oblk),
            scratch_shapes=[pltpu.VMEM((nsb, slots, D_MODEL), BF16),
                            pltpu.VMEM((tb, D_MODEL), F32)]),
        out_shape=jax.ShapeDtypeStruct((t, D_MODEL), F32),
        compiler_params=pltpu.CompilerParams(dimension_semantics=("parallel", "arbitrary"),
                                             vmem_limit_bytes=MOE_VMEM_LIMIT),
        name="moe",
    )(tab, h2, qrow3, pos3, w3, wg, wu, wd, qcol, x1, gg2)


def _slot_tables(cnt, *, slots):
    need = ((jnp.max(cnt, axis=1) + 15) // 16) * 16
    start = jnp.cumsum(need, axis=1) - need
    rows = jnp.clip(slots - start, 0, need)
    over = jnp.max(jnp.maximum(cnt - rows[:, None, :], 0), axis=1)
    return rows, start, over


def _pos_tables(rows):
    quarter = D_MODEL // 4
    freq = 1.0 / (POS_BASE ** (np.arange(quarter, dtype=np.float64) / quarter))
    r = np.arange(rows, dtype=np.float64)[:, None] * freq
    cl = np.arange(GRID_W, dtype=np.float64)[:, None] * freq
    er = np.concatenate([np.sin(r), np.cos(r)], axis=-1).astype(np.float32)
    ec = np.concatenate([np.sin(cl), np.cos(cl)], axis=-1).astype(np.float32)
    return jnp.asarray(er[:, None, :]), jnp.asarray(ec[None, :, :])


def _dft_tables(t):
    n = np.arange(CHUNK, dtype=np.int64)
    prod = n[:, None] * n[None, :]
    ang = (prod % CHUNK).astype(np.float64) * (2.0 * np.pi / CHUNK)
    c, s = np.cos(ang), np.sin(ang)
    f1 = np.concatenate([c, s], axis=0)
    a2 = np.concatenate([np.concatenate([c, -s], axis=1), np.concatenate([s, c], axis=1)], axis=0)
    cs = np.concatenate([c, -s], axis=0)
    angw = prod.astype(np.float64) * (2.0 * np.pi / t)
    f32 = lambda a: jnp.asarray(a.astype(np.float32))
    return f32(f1), f32(a2), f32(cs), f32(np.cos(angw)[:, :, None]), f32(np.sin(angw)[:, :, None])


def _blockdiag(w):
    n = w.shape[0]
    size = n * QKV_BLOCK
    spread = np.tile(np.eye(QKV_BLOCK, dtype=np.float32), (1, n))
    rows = jnp.dot(w.reshape(size, QKV_BLOCK), jnp.asarray(spread), precision=lax.Precision.HIGHEST)
    blk = np.arange(size) // QKV_BLOCK
    mask = (blk[:, None] == blk[None, :]).astype(np.float32)
    return rows * jnp.asarray(mask)


def _gate_weights(w_f, b_f, w_b, b_b):
    w = jnp.concatenate([w_f[:, :HEADS], w_b[:, :HEADS], w_f[:, HEADS:], w_b[:, HEADS:]], axis=1).T
    b = jnp.concatenate([b_f[:HEADS], b_b[:HEADS], b_f[HEADS:], b_b[HEADS:]])
    return w.astype(BF16), b[:, None]


def kernel(x, c, ctx, c_ctx, w_ada, b_ada, g_pre_mix, g_post_mix, g_pre_ffn, g_post_ffn,
           w_in, conv_w, conv_b, w_q, w_k, w_v, w_if_fwd, b_if_fwd, w_if_bwd, b_if_bwd,
           mlstm_norm_w, mlstm_skip, w_fourier, w_out, w_router_group, b_router_group,
           w_router_expert, b_router_expert, w_gate, w_up, w_down):
    t = x.shape[1]
    rows = t // GRID_W

    c8 = jnp.concatenate([c, c_ctx[None, :], jnp.zeros((6, D_MODEL), F32)], axis=0)
    mod = _ada(c8, w_ada[0], b_ada[0][None, :])
    shift1, scale1, gate1, shift2, scale2, gate2 = [mod[0:1, k * D_MODEL:(k + 1) * D_MODEL] for k in range(N_MOD)]
    shift1c, scale1c = mod[1:2, 0:D_MODEL], mod[1:2, D_MODEL:2 * D_MODEL]
    g1 = g_pre_mix[0][None, :] * (1.0 + scale1)
    g1c = g_pre_mix[0][None, :] * (1.0 + scale1c)
    gg1 = g_post_mix[0][None, :] * gate1
    g2 = g_pre_ffn[0][None, :] * (1.0 + scale2)
    gg2 = g_post_ffn[0][None, :] * gate2

    er3, ec3 = _pos_tables(rows)
    x3 = x.reshape(rows, GRID_W, D_MODEL)
    ctx3 = ctx.reshape(CTX_LEN // GRID_W, GRID_W, D_MODEL)
    w_in_bf = w_in[0].astype(BF16)

    xm_l, z_l, u_l = _inproj(x3, er3, ec3, g1, shift1, w_in_bf, rows=8, add_pos=True)
    xm_c, _, _ = _inproj(ctx3, er3, ec3, g1c, shift1c, w_in_bf, rows=CTX_LEN // GRID_W, add_pos=False)

    wq = _blockdiag(w_q[0]).astype(BF16)
    wkt = _blockdiag(w_k[0]).T.astype(BF16)
    wv = _blockdiag(w_v[0]).astype(BF16)
    wi, bi = _gate_weights(w_if_fwd[0], b_if_fwd[0], w_if_bwd[0], b_if_bwd[0])
    wiq, wik, wiv = wi[:, :D_MLSTM], wi[:, D_MLSTM:2 * D_MLSTM], wi[:, 2 * D_MLSTM:]
    cb = conv_b[0][None, :]
    q_l, kt_l, v_l, act_l, gp_l = _feat(xm_l, conv_w[0], cb, wq, wkt, wv, wiq, wik, wiv, bi, tm=512)
    q_c, kt_c, v_c, _, gp_c = _feat(xm_c, conv_w[0], cb, wq, wkt, wv, wiq, wik, wiv, bi, tm=CTX_LEN)

    dr_l, gc_l = _gates(gp_l, tl=2048)
    dr_c, gc_c = _gates(gp_c, tl=CTX_LEN)

    c0 = jnp.zeros((2 * HEADS, DH, 2 * DH), F32)
    m0 = jnp.zeros((2 * HEADS, 8, LANES), F32)
    c_ctx_fin, m_ctx_fin = _mlstm(q_c, kt_c, v_c, gc_c, dr_c, c0, m0, cps=CTX_LEN // CHUNK, emit=False)
    hf, hb, _, _ = _mlstm(q_l, kt_l, v_l, gc_l, dr_l, c_ctx_fin, m_ctx_fin, cps=2, emit=True)

    f1, a2, cs, cw3, sw3 = _dft_tables(t)
    n1 = t // CHUNK
    yc, ys = _dft1(u_l.reshape(FGROUPS, n1, CHUNK, FCG), f1.astype(BF16))
    mix = jnp.einsum('kc,gcd->gkd', cs, w_fourier[0], precision=lax.Precision.HIGHEST)
    mix = (mix * float(1.0 / np.sqrt(float(t) * FCG))).astype(BF16)
    yf = _dft2(yc.reshape(FGROUPS, t, FCG), ys.reshape(FGROUPS, t, FCG), cw3, sw3, a2.astype(BF16), mix)
    yf = yf.reshape(FGROUPS, t, FCG)

    wr = jnp.concatenate([w_router_group[0], w_router_expert[0],
                          jnp.zeros((D_MODEL, LANES - N_GROUPS - N_EXPERTS), F32)], axis=1)
    br = jnp.concatenate([b_router_group[0], b_router_expert[0],
                          jnp.zeros((LANES - N_GROUPS - N_EXPERTS,), F32)])[None, :]
    x1, h2, lg = _merge(hf, hb, act_l, z_l, yf, x3, er3, ec3,
                        mlstm_norm_w[0][None, :], mlstm_skip[0][None, :], w_out[0].astype(BF16),
                        gg1, g2, shift2, wr, br, rows=8)
    pos, wts, cnt = _route(lg, tl=MOE_TB, sbk=MOE_SBK)
    nblk, nsb = t // MOE_TB, MOE_TB // MOE_SBK
    cnt = cnt.reshape(nblk, N_EXPERTS, LANES)[:, :, :nsb]
    cnt = jnp.transpose(cnt, (0, 2, 1)).astype(jnp.int32)
    rows, start, over = _slot_tables(cnt, slots=MOE_SLOTS)
    tab = jnp.concatenate([a.reshape(-1) for a in (cnt, rows, start, over)])
    off_r = nblk * nsb * N_EXPERTS
    off_o = off_r + nblk * N_EXPERTS
    off_f = off_o + nblk * N_EXPERTS
    qrow, qcol = _slots(tab, pos, wts, tl=MOE_TB, off_r=off_r, off_o=off_o)
    sub = (t // MOE_SBK, MOE_SBK)
    out = _moe(tab, h2, qrow.reshape(8, *sub), pos.reshape(N_EXPERTS, *sub), wts.reshape(N_EXPERTS, *sub),
               w_gate[0].astype(BF16), w_up[0].astype(BF16), w_down[0].astype(BF16),
               qcol, x1, gg2, tb=MOE_TB, sbk=MOE_SBK, slots=MOE_SLOTS, csb=MOE_CSB,
               off_r=off_r, off_o=off_o, off_f=off_f)
    return out[None]
```

```python
import functools

import numpy as np
import jax
import jax.numpy as jnp
from jax import lax
from jax.experimental import pallas as pl
from jax.experimental.pallas import tpu as pltpu

F32 = jnp.float32
BF16 = jnp.bfloat16

D_MODEL = 1024
SEQ = 16384
GRID_W = 64
CTX_LEN = 256
D_MLSTM = 512
HEADS = 4
DH = 128
QKV_BLOCK = 4
CONV_K = 3
CHUNK = 128
D_FOURIER = 512
FGROUPS = 4
FCG = 128
N_GROUPS = 4
EPG = 4
N_EXPERTS = 16
D_EXPERT = 512
N_MOD = 6
EPS = 1e-6
POS_BASE = 10000.0
LANES = 128
NEG_BIG = -3.0e38

VMEM_LIMIT = 52 * 1024 * 1024
MOE_VMEM_LIMIT = 56 * 1024 * 1024
MOE_TB = 2048
MOE_SBK = 256
MOE_SLOTS = 1024
MOE_CSB = 2


def _cparams(*sem):
    return pltpu.CompilerParams(dimension_semantics=sem, vmem_limit_bytes=VMEM_LIMIT)


def _dot(a, b):
    return jnp.dot(a, b, preferred_element_type=F32)


def _dot_nt(a, b):
    return lax.dot_general(a, b, (((1,), (1,)), ((), ())), preferred_element_type=F32)


def _split_bf16(a):
    hi = a.astype(BF16)
    lo = (a - hi.astype(F32)).astype(BF16)
    return hi, lo


def _dot3(a, b):
    a_hi, a_lo = _split_bf16(a)
    b_hi, b_lo = _split_bf16(b)
    return _dot(a_hi, b_hi) + (_dot(a_hi, b_lo) + _dot(a_lo, b_hi))


def _sigmoid(x):
    return 1.0 / (1.0 + jnp.exp(-x))


def _rms(x):
    return x * lax.rsqrt(jnp.mean(x * x, axis=-1, keepdims=True) + EPS)


def _ada_kernel(c_ref, w_ref, b_ref, o_ref):
    c = c_ref[...]
    s = c * _sigmoid(c)
    o_ref[...] = _dot3(s, w_ref[...]) + b_ref[...]


def _ada(c8, w, b):
    n = w.shape[1]
    tn = 768
    return pl.pallas_call(
        _ada_kernel,
        grid=(n // tn,),
        in_specs=[pl.BlockSpec((8, D_MODEL), lambda j: (0, 0)),
                  pl.BlockSpec((D_MODEL, tn), lambda j: (0, j)),
                  pl.BlockSpec((1, tn), lambda j: (0, j))],
        out_specs=pl.BlockSpec((8, tn), lambda j: (0, j)),
        out_shape=jax.ShapeDtypeStruct((8, n), F32),
        compiler_params=_cparams("parallel"),
        name="ada",
    )(c8, w, b)


def _add_pos(x3, er_ref, ec_ref):
    r = x3.shape[0]
    pr = jnp.broadcast_to(er_ref[...], (r, GRID_W, D_MODEL // 2))
    pc = jnp.broadcast_to(ec_ref[...], (r, GRID_W, D_MODEL // 2))
    return x3 + jnp.concatenate([pr, pc], axis=-1)


def _inproj_kernel(x_ref, er_ref, ec_ref, g_ref, sh_ref, w_ref, xm_ref, z_ref, u_ref, *, add_pos):
    x3 = x_ref[...]
    if add_pos:
        x3 = _add_pos(x3, er_ref, ec_ref)
    x = x3.reshape(x3.shape[0] * GRID_W, D_MODEL)
    h = _rms(x) * g_ref[...] + sh_ref[...]
    proj = _dot(h.astype(BF16), w_ref[...])
    xm_ref[...] = proj[:, :D_MLSTM].astype(BF16)
    z_ref[...] = proj[:, D_MLSTM:2 * D_MLSTM].astype(BF16)
    for g in range(FGROUPS):
        u_ref[g] = proj[:, 2 * D_MLSTM + g * FCG:2 * D_MLSTM + (g + 1) * FCG]


def _inproj(x3, er3, ec3, g_eff, shift, w_in, *, rows, add_pos):
    nr = x3.shape[0]
    t = nr * GRID_W
    tm = rows * GRID_W
    out = jax.ShapeDtypeStruct((t, D_MLSTM), BF16)
    ospec = pl.BlockSpec((tm, D_MLSTM), lambda i: (i, 0))
    vec = pl.BlockSpec((1, D_MODEL), lambda i: (0, 0))
    return pl.pallas_call(
        functools.partial(_inproj_kernel, add_pos=add_pos),
        grid=(nr // rows,),
        in_specs=[pl.BlockSpec((rows, GRID_W, D_MODEL), lambda i: (i, 0, 0)),
                  pl.BlockSpec((rows, 1, D_MODEL // 2), lambda i: (i, 0, 0)),
                  pl.BlockSpec((1, GRID_W, D_MODEL // 2), lambda i: (0, 0, 0)),
                  vec, vec,
                  pl.BlockSpec(w_in.shape, lambda i: (0, 0))],
        out_specs=[ospec, ospec, pl.BlockSpec((FGROUPS, tm, FCG), lambda i: (0, i, 0))],
        out_shape=[out, out, jax.ShapeDtypeStruct((FGROUPS, t, FCG), F32)],
        compiler_params=_cparams("parallel"),
        name="inproj",
    )(x3, er3, ec3, g_eff, shift, w_in)


def _feat_kernel(xm_ref, prev_ref, next_ref, cw_ref, cb_ref, wq_ref, wkt_ref, wv_ref,
                 wiq_ref, wik_ref, wiv_ref, bi_ref,
                 q_ref, kt_ref, v_ref, act_ref, g_ref):
    i = pl.program_id(0)
    n = pl.num_programs(0)
    xm_bf = xm_ref[...]
    xm = xm_bf.astype(F32)
    tm = xm.shape[0]
    prev_row = prev_ref[...].astype(F32)[15:16, :] * jnp.where(i > 0, 1.0, 0.0)
    next_row = next_ref[...].astype(F32)[0:1, :] * jnp.where(i < n - 1, 1.0, 0.0)
    rid = lax.broadcasted_iota(jnp.int32, (tm, 1), 0)
    x_left = jnp.where(rid == 0, prev_row, pltpu.roll(xm, 1, 0))
    x_right = jnp.where(rid == tm - 1, next_row, pltpu.roll(xm, tm - 1, 0))
    cw = cw_ref[...]
    y = cw[0:1] * x_left + cw[1:2] * xm + cw[2:3] * x_right + cb_ref[...]
    act = (y * _sigmoid(y)).astype(BF16)
    act_ref[...] = act
    q = _dot(act, wq_ref[...])
    kt = _dot_nt(wkt_ref[...], act)
    v = _dot(xm_bf, wv_ref[...])
    q_bf = q.astype(BF16)
    kt_bf = kt.astype(BF16)
    v_bf = v.astype(BF16)
    q_ref[...] = (q * (DH ** -0.5)).astype(BF16)
    kt_ref[...] = kt_bf
    v_ref[...] = v_bf
    g = _dot_nt(wiq_ref[...], q_bf) + _dot(wik_ref[...], kt_bf) + _dot_nt(wiv_ref[...], v_bf)
    g_ref[...] = g + bi_ref[...]


def _feat(xm, conv_w, conv_b, wq, wkt, wv, wiq, wik, wiv, bi, *, tm):
    t = xm.shape[0]
    nb16 = t // 16
    k16 = tm // 16
    full = lambda a: pl.BlockSpec(a.shape, lambda i: (0,) * a.ndim)
    tok = pl.BlockSpec((tm, D_MLSTM), lambda i: (i, 0))
    return pl.pallas_call(
        _feat_kernel,
        grid=(t // tm,),
        in_specs=[tok,
                  pl.BlockSpec((16, D_MLSTM), lambda i: (jnp.maximum(i * k16 - 1, 0), 0)),
                  pl.BlockSpec((16, D_MLSTM), lambda i: (jnp.minimum((i + 1) * k16, nb16 - 1), 0)),
                  full(conv_w), full(conv_b), full(wq), full(wkt), full(wv),
                  full(wiq), full(wik), full(wiv), full(bi)],
        out_specs=[tok,
                   pl.BlockSpec((D_MLSTM, tm), lambda i: (0, i)),
                   tok, tok,
                   pl.BlockSpec((16, tm), lambda i: (0, i))],
        out_shape=[jax.ShapeDtypeStruct((t, D_MLSTM), BF16),
                   jax.ShapeDtypeStruct((D_MLSTM, t), BF16),
                   jax.ShapeDtypeStruct((t, D_MLSTM), BF16),
                   jax.ShapeDtypeStruct((t, D_MLSTM), BF16),
                   jax.ShapeDtypeStruct((16, t), F32)],
        compiler_params=_cparams("parallel"),
        name="feat",
    )(xm, xm, xm, conv_w, conv_b, wq, wkt, wv, wiq, wik, wiv, bi)


def _gates_kernel(g_ref, d_ref, gc_ref):
    g = g_ref[...]
    tl = g.shape[1]
    ig = g[0:8]
    fg = g[8:16]
    lf = jnp.minimum(fg, 0.0) - jnp.log(1.0 + jnp.exp(-jnp.abs(fg)))
    pos = lax.broadcasted_iota(jnp.int32, (8, tl), 1) & (CHUNK - 1)
    is_fwd = lax.broadcasted_iota(jnp.int32, (8, tl), 0) < HEADS

    def scan(x, op, ident):
        xf = x
        xb = x
        k = 1
        while k < CHUNK:
            xf = op(xf, jnp.where(pos >= k, pltpu.roll(xf, k, 1), ident))
            xb = op(xb, jnp.where(pos < CHUNK - k, pltpu.roll(xb, tl - k, 1), ident))
            k *= 2
        return jnp.where(is_fwd, xf, xb)

    b = scan(lf, jnp.add, 0.0)
    d = ig - b
    mloc = b + scan(d, jnp.maximum, NEG_BIG)
    d_ref[...] = d
    stack = jnp.concatenate([b, mloc, jnp.zeros((LANES - 16, tl), F32)], axis=0)
    gc_ref[...] = stack.T


def _gates(g, *, tl):
    t = g.shape[1]
    return pl.pallas_call(
        _gates_kernel,
        grid=(t // tl,),
        in_specs=[pl.BlockSpec((16, tl), lambda i: (0, i))],
        out_specs=[pl.BlockSpec((8, tl), lambda i: (0, i)),
                   pl.BlockSpec((tl, LANES), lambda i: (i, 0))],
        out_shape=[jax.ShapeDtypeStruct((8, t), F32),
                   jax.ShapeDtypeStruct((t, LANES), F32)],
        compiler_params=_cparams("parallel"),
        name="gates",
    )(g)


def _mlstm_kernel(*refs, cps, emit):
    (qf_ref, kf_ref, vf_ref, gcf_ref, drf_ref,
     qb_ref, kb_ref, vb_ref, gcb_ref, drb_ref, c0_ref, m0_ref) = refs[:12]
    if emit:
        hf_ref, hb_ref, cfin_ref, mfin_ref, c_scr, m_scr = refs[12:]
    else:
        cfin_ref, mfin_ref, c_scr, m_scr = refs[12:]
        hf_ref = hb_ref = None
    step = pl.program_id(0)

    @pl.when(step == 0)
    def _():
        c_scr[...] = c0_ref[...]
        m_scr[...] = m0_ref[...]

    ti = lax.broadcasted_iota(jnp.int32, (CHUNK, CHUNK), 0)
    si = lax.broadcasted_iota(jnp.int32, (CHUNK, CHUNK), 1)
    ones_col = jnp.where(si == 0, 1.0, 0.0).astype(BF16)

    for j in range(cps):
        for d in range(2):
            if d == 0:
                q_ref, k_ref, v_ref, gc_ref, dr_ref, h_ref = qf_ref, kf_ref, vf_ref, gcf_ref, drf_ref, hf_ref
                r0, mask, last = j * CHUNK, si <= ti, CHUNK - 1
            else:
                q_ref, k_ref, v_ref, gc_ref, dr_ref, h_ref = qb_ref, kb_ref, vb_ref, gcb_ref, drb_ref, hb_ref
                r0, mask, last = (cps - 1 - j) * CHUNK, si >= ti, 0
            rows = slice(r0, r0 + CHUNK)
            gc = gc_ref[rows, :]
            dr = dr_ref[:, rows]
            for hd in range(HEADS):
                r = d * HEADS + hd
                cols = slice(hd * DH, (hd + 1) * DH)
                b = gc[:, r:r + 1]
                mloc = gc[:, 8 + r:9 + r]
                drow = dr[r:r + 1, :]
                m0 = m_scr[r][0:1, 0:1]
                c_aug = c_scr[r]
                kt = k_ref[cols, rows]
                vaug = jnp.concatenate([v_ref[rows, cols], ones_col], axis=1)
                if emit:
                    qh = q_ref[rows, cols]
                    bm = b + m0
                    m_t = jnp.maximum(bm, mloc)
                    dmat = jnp.where(mask, jnp.exp((b - m_t) + drow), 0.0)
                    smat = (_dot(qh, kt) * dmat).astype(BF16)
                    q_in = (qh.astype(F32) * jnp.exp(bm - m_t)).astype(BF16)
                    num = _dot(jnp.concatenate([smat, q_in], axis=1),
                               jnp.concatenate([vaug, c_aug.astype(BF16)], axis=0))
                    den = jnp.maximum(jnp.abs(num[:, DH:DH + 1]), jnp.exp(-m_t))
                    h_ref[rows, cols] = (num[:, :DH] / den).astype(h_ref.dtype)
                btot = gc[last:last + 1, r:r + 1]
                amax = gc[last:last + 1, 8 + r:9 + r]
                m_new = jnp.maximum(btot + m0, amax)
                decay = jnp.exp(btot + m0 - m_new)
                kw = (kt.astype(F32) * jnp.exp(btot + drow - m_new)).astype(BF16)
                c_scr[r] = decay * c_aug + _dot(kw, vaug)
                m_scr[r] = jnp.broadcast_to(m_new, (8, LANES))

    @pl.when(step == pl.num_programs(0) - 1)
    def _():
        cfin_ref[...] = c_scr[...]
        mfin_ref[...] = m_scr[...]


def _mlstm(q, kt, v, gc, dr, c0, m0, *, cps, emit):
    t = q.shape[0]
    cb = cps * CHUNK
    nb = t // cb
    fwd_r = lambda i: (i, 0)
    bwd_r = lambda i: (nb - 1 - i, 0)
    fwd_c = lambda i: (0, i)
    bwd_c = lambda i: (0, nb - 1 - i)
    tok = lambda f: pl.BlockSpec((cb, D_MLSTM), f)
    in_specs = []
    for fr, fc in ((fwd_r, fwd_c), (bwd_r, bwd_c)):
        in_specs += [tok(fr), pl.BlockSpec((D_MLSTM, cb), fc), tok(fr),
                     pl.BlockSpec((cb, LANES), fr), pl.BlockSpec((8, cb), fc)]
    cshape = (2 * HEADS, DH, 2 * DH)
    mshape = (2 * HEADS, 8, LANES)
    cspec = pl.BlockSpec(cshape, lambda i: (0, 0, 0))
    mspec = pl.BlockSpec(mshape, lambda i: (0, 0, 0))
    in_specs += [cspec, mspec]
    out_specs = [cspec, mspec]
    out_shape = [jax.ShapeDtypeStruct(cshape, F32), jax.ShapeDtypeStruct(mshape, F32)]
    if emit:
        out_specs = [tok(fwd_r), tok(bwd_r)] + out_specs
        out_shape = [jax.ShapeDtypeStruct((t, D_MLSTM), BF16)] * 2 + out_shape
    return pl.pallas_call(
        functools.partial(_mlstm_kernel, cps=cps, emit=emit),
        grid=(nb,),
        in_specs=in_specs,
        out_specs=out_specs,
        out_shape=out_shape,
        scratch_shapes=[pltpu.VMEM(cshape, F32), pltpu.VMEM(mshape, F32)],
        compiler_params=_cparams("arbitrary"),
        name="mlstm",
    )(q, kt, v, gc, dr, q, kt, v, gc, dr, c0, m0)


FFT_ROWS = 8


def _dft1_kernel(u_ref, f_ref, yc_ref, ys_ref):
    f = f_ref[...]
    rows = CHUNK * FFT_ROWS
    u2 = u_ref.reshape(FGROUPS * rows, FCG)
    yc2 = yc_ref.reshape(FGROUPS * rows, FCG)
    ys2 = ys_ref.reshape(FGROUPS * rows, FCG)
    for s in range(FFT_ROWS):
        pick = [pl.ds(g * rows + s, CHUNK, stride=FFT_ROWS) for g in range(FGROUPS)]
        x = jnp.concatenate([u2[p, :] for p in pick], axis=1).astype(BF16)
        y = _dot(f, x)
        for g in range(FGROUPS):
            yc2[pick[g], :] = y[:CHUNK, g * FCG:(g + 1) * FCG]
            ys2[pick[g], :] = y[CHUNK:, g * FCG:(g + 1) * FCG]


def _dft1(u4, f1):
    blk = pl.BlockSpec((FGROUPS, CHUNK, FFT_ROWS, FCG), lambda j: (0, 0, j, 0))
    out = jax.ShapeDtypeStruct(u4.shape, F32)
    return pl.pallas_call(
        _dft1_kernel,
        grid=(u4.shape[2] // FFT_ROWS,),
        in_specs=[blk, pl.BlockSpec(f1.shape, lambda j: (0, 0))],
        out_specs=[blk, blk],
        out_shape=[out, out],
        compiler_params=_cparams("parallel"),
        name="dft1",
    )(u4, f1)


def _dft2_kernel(yc_ref, ys_ref, cw_ref, sw_ref, a2_ref, mix_ref, o_ref):
    a2 = a2_ref[...]
    rows = CHUNK * FFT_ROWS
    o2 = o_ref.reshape(FGROUPS * rows, FCG)
    for kk in range(FFT_ROWS):
        blk = slice(kk * CHUNK, (kk + 1) * CHUNK)
        yc = jnp.concatenate([yc_ref[g, blk, :] for g in range(FGROUPS)], axis=1)
        ys = jnp.concatenate([ys_ref[g, blk, :] for g in range(FGROUPS)], axis=1)
        cw = cw_ref[kk]
        sw = sw_ref[kk]
        p = jnp.concatenate([yc * cw - ys * sw, yc * sw + ys * cw], axis=0).astype(BF16)
        x = _dot(a2, p).astype(BF16)
        for g in range(FGROUPS):
            cols = slice(g * FCG, (g + 1) * FCG)
            cat = jnp.concatenate([x[:CHUNK, cols], x[CHUNK:, cols]], axis=1)
            o2[pl.ds(g * rows + kk, CHUNK, stride=FFT_ROWS), :] = _dot(cat, mix_ref[g])


def _dft2(yc, ys, cw3, sw3, a2, mix):
    t = yc.shape[1]
    n1 = t // CHUNK
    tok = pl.BlockSpec((FGROUPS, FFT_ROWS * CHUNK, FCG), lambda i: (0, i, 0))
    tw = pl.BlockSpec((FFT_ROWS, CHUNK, 1), lambda i: (i, 0, 0))
    full = lambda a: pl.BlockSpec(a.shape, lambda i: (0,) * a.ndim)
    return pl.pallas_call(
        _dft2_kernel,
        grid=(n1 // FFT_ROWS,),
        in_specs=[tok, tok, tw, tw, full(a2), full(mix)],
        out_specs=pl.BlockSpec((FGROUPS, CHUNK, FFT_ROWS, FCG), lambda i: (0, 0, i, 0)),
        out_shape=jax.ShapeDtypeStruct((FGROUPS, CHUNK, n1, FCG), F32),
        compiler_params=_cparams("parallel"),
        name="dft2",
    )(yc, ys, cw3, sw3, a2, mix)


def _merge_kernel(hf_ref, hb_ref, act_ref, z_ref, yf_ref, x_ref, er_ref, ec_ref,
                  nw_ref, sk_ref, wout_ref, gg1_ref, g2_ref, sh2_ref, wr_ref, br_ref,
                  x1_ref, h2_ref, lg_ref):
    h = hf_ref[...].astype(F32) + hb_ref[...].astype(F32)
    parts = []
    for hd in range(HEADS):
        hh = h[:, hd * DH:(hd + 1) * DH]
        dl = hh - jnp.mean(hh, axis=-1, keepdims=True)
        var = jnp.mean(dl * dl, axis=-1, keepdims=True)
        parts.append(dl * lax.rsqrt(var + EPS))
    hn = jnp.concatenate(parts, axis=-1)
    z = z_ref[...].astype(F32)
    m = (hn * nw_ref[...] + sk_ref[...] * act_ref[...].astype(F32)) * (z * _sigmoid(z))
    cat = jnp.concatenate([m.astype(BF16)] + [yf_ref[g].astype(BF16) for g in range(FGROUPS)], axis=-1)
    y = _dot(cat, wout_ref[...])
    x3 = _add_pos(x_ref[...], er_ref, ec_ref)
    xp = x3.reshape(x3.shape[0] * GRID_W, D_MODEL)
    x1 = xp + _rms(y) * gg1_ref[...]
    x1_ref[...] = x1
    h2 = _rms(x1) * g2_ref[...] + sh2_ref[...]
    h2_ref[...] = h2.astype(BF16)
    lg = _dot(h2.astype(BF16), wr_ref[...]) + br_ref[...]
    lg_ref[...] = lg.T[:32]


def _merge(hf, hb, act, z, yf, x3, er3, ec3, nw, sk, wout, gg1, g2, sh2, wr, br, *, rows):
    nr = x3.shape[0]
    t = nr * GRID_W
    tm = rows * GRID_W
    tok = pl.BlockSpec((tm, D_MLSTM), lambda i: (i, 0))
    full = lambda a: pl.BlockSpec(a.shape, lambda i: (0,) * a.ndim)
    return pl.pallas_call(
        _merge_kernel,
        grid=(nr // rows,),
        in_specs=[tok, tok, tok, tok,
                  pl.BlockSpec((FGROUPS, tm, FCG), lambda i: (0, i, 0)),
                  pl.BlockSpec((rows, GRID_W, D_MODEL), lambda i: (i, 0, 0)),
                  pl.BlockSpec((rows, 1, D_MODEL // 2), lambda i: (i, 0, 0)),
                  pl.BlockSpec((1, GRID_W, D_MODEL // 2), lambda i: (0, 0, 0)),
                  full(nw), full(sk), full(wout), full(gg1), full(g2), full(sh2), full(wr), full(br)],
        out_specs=[pl.BlockSpec((tm, D_MODEL), lambda i: (i, 0)),
                   pl.BlockSpec((tm, D_MODEL), lambda i: (i, 0)),
                   pl.BlockSpec((32, tm), lambda i: (0, i))],
        out_shape=[jax.ShapeDtypeStruct((t, D_MODEL), F32),
                   jax.ShapeDtypeStruct((t, D_MODEL), BF16),
                   jax.ShapeDtypeStruct((32, t), F32)],
        compiler_params=_cparams("parallel"),
        name="merge",
    )(hf, hb, act, z, yf, x3, er3, ec3, nw, sk, wout, gg1, g2, sh2, wr, br)


def _route_kernel(lg_ref, pos_ref, w_ref, cnt_ref, *, sbk):
    lg = lg_ref[...]
    tl = lg.shape[1]
    g = [lg[j:j + 1] for j in range(N_GROUPS)]
    e = [lg[N_GROUPS + j:N_GROUPS + j + 1] for j in range(N_EXPERTS)]
    gmax = jnp.maximum(jnp.maximum(g[0], g[1]), jnp.maximum(g[2], g[3]))
    den = jnp.exp(g[0] - gmax) + jnp.exp(g[1] - gmax) + jnp.exp(g[2] - gmax) + jnp.exp(g[3] - gmax)
    p_sel = 1.0 / den
    sel = []
    free = jnp.ones((1, tl), F32)
    for j in range(N_GROUPS):
        s = jnp.where(g[j] >= gmax, free, 0.0)
        sel.append(s)
        free = free - s
    es = []
    for j in range(EPG):
        es.append(sel[0] * e[j] + sel[1] * e[EPG + j] + sel[2] * e[2 * EPG + j] + sel[3] * e[3 * EPG + j])
    rank = []
    for j in range(EPG):
        rj = jnp.zeros((1, tl), F32)
        for i in range(EPG):
            if i == j:
                continue
            beats = (es[i] >= es[j]) if i < j else (es[i] > es[j])
            rj = rj + jnp.where(beats, 1.0, 0.0)
        rank.append(rj)
    v1 = jnp.maximum(jnp.maximum(es[0], es[1]), jnp.maximum(es[2], es[3]))
    v2 = sum(jnp.where(rank[j] == 1.0, es[j], 0.0) for j in range(EPG))
    tt = jnp.exp(v2 - v1)
    w1 = p_sel / (1.0 + tt)
    w2 = w1 * tt
    w = [jnp.where(rank[j] == 0.0, w1, jnp.where(rank[j] == 1.0, w2, 0.0)) for j in range(EPG)]
    top2 = [jnp.where(rank[j] < 2.0, 1.0, 0.0) for j in range(EPG)]
    mem = jnp.concatenate([sel[gi] * top2[j] for gi in range(N_GROUPS) for j in range(EPG)], axis=0)
    wts = jnp.concatenate([sel[gi] * w[j] for gi in range(N_GROUPS) for j in range(EPG)], axis=0)
    w_ref[...] = wts
    lane = lax.broadcasted_iota(jnp.int32, (N_EXPERTS, tl), 1) & (sbk - 1)
    c = mem
    k = 1
    while k < sbk:
        c = c + jnp.where(lane >= k, pltpu.roll(c, k, 1), 0.0)
        k *= 2
    pos_ref[...] = jnp.where(mem > 0.0, c - 1.0, -1.0)
    lane128 = lax.broadcasted_iota(jnp.int32, (N_EXPERTS, LANES), 1)
    cnt = jnp.zeros((N_EXPERTS, LANES), F32)
    for kb in range(tl // sbk):
        tot = jnp.sum(mem[:, kb * sbk:(kb + 1) * sbk], axis=1, keepdims=True)
        cnt = cnt + jnp.where(lane128 == kb, tot, 0.0)
    cnt_ref[...] = cnt


def _route(lg, *, tl, sbk):
    t = lg.shape[1]
    row = pl.BlockSpec((N_EXPERTS, tl), lambda i: (0, i))
    return pl.pallas_call(
        functools.partial(_route_kernel, sbk=sbk),
        grid=(t // tl,),
        in_specs=[pl.BlockSpec((32, tl), lambda i: (0, i))],
        out_specs=[row, row, pl.BlockSpec((N_EXPERTS, LANES), lambda i: (i, 0))],
        out_shape=[jax.ShapeDtypeStruct((N_EXPERTS, t), F32),
                   jax.ShapeDtypeStruct((N_EXPERTS, t), F32),
                   jax.ShapeDtypeStruct((t // tl * N_EXPERTS, LANES), F32)],
        compiler_params=_cparams("parallel"),
        name="route",
    )(lg)


def _slots_kernel(tab_ref, pos_ref, w_ref, qrow_ref, qcol_ref, *, off_r, off_o):
    i = pl.program_id(0)
    tl = pos_ref.shape[1]
    q0 = jnp.full((1, tl), -1.0, F32)
    q1 = jnp.full((1, tl), -1.0, F32)
    w0 = jnp.zeros((1, tl), F32)
    w1 = jnp.zeros((1, tl), F32)
    seen = jnp.zeros((1, tl), F32)
    for ex in range(N_EXPERTS):
        rk = pos_ref[ex:ex + 1, :]
        wt = w_ref[ex:ex + 1, :]
        rows = tab_ref[off_r + i * N_EXPERTS + ex].astype(F32)
        first_row = tab_ref[off_o + i * N_EXPERTS + ex].astype(F32)
        m = jnp.where(rk >= 0.0, 1.0, 0.0)
        val = jnp.where(rk < rows, rk + first_row, -1.0)
        first = (m * (1.0 - seen)) > 0.0
        second = (m * seen) > 0.0
        q0 = jnp.where(first, val, q0)
        w0 = jnp.where(first, wt, w0)
        q1 = jnp.where(second, val, q1)
        w1 = jnp.where(second, wt, w1)
        seen = seen + m
    qrow_ref[...] = jnp.concatenate([q0, q1, jnp.zeros((6, tl), F32)], axis=0)
    qcol_ref[...] = jnp.concatenate([q0, q1, w0, w1, jnp.zeros((LANES - 4, tl), F32)], axis=0).T


def _slots(tab, pos, w, *, tl, off_r, off_o):
    t = pos.shape[1]
    row = pl.BlockSpec((N_EXPERTS, tl), lambda i, c: (0, i))
    return pl.pallas_call(
        functools.partial(_slots_kernel, off_r=off_r, off_o=off_o),
        grid_spec=pltpu.PrefetchScalarGridSpec(
            num_scalar_prefetch=1,
            grid=(t // tl,),
            in_specs=[row, row],
            out_specs=[pl.BlockSpec((8, tl), lambda i, c: (0, i)),
                       pl.BlockSpec((tl, LANES), lambda i, c: (i, 0))]),
        out_shape=[jax.ShapeDtypeStruct((8, t), F32),
                   jax.ShapeDtypeStruct((t, LANES), F32)],
        compiler_params=_cparams("parallel"),
        name="slots",
    )(tab, pos, w)


def _mlp(x, wg, wu, wd):
    gt = _dot(x, wg)
    a = ((gt * _sigmoid(gt)) * _dot(x, wu)).astype(BF16)
    return _dot(a, wd).astype(BF16)


def _moe_kernel(tab_ref, h_ref, qrow_ref, pos_ref, w_ref, wg_ref, wu_ref, wd_ref,
                qcol_ref, x1_ref, gg2_ref, o_ref, xs_ref, ovf_ref, *, sbk, slots, csb, off_r, off_o, off_f):
    i = pl.program_id(0)
    step = pl.program_id(1)
    nsb = h_ref.shape[0] // sbk
    e = jnp.minimum(step, N_EXPERTS - 1)
    n_rows = tab_ref[off_r + i * N_EXPERTS + e]
    row_a = tab_ref[off_o + i * N_EXPERTS + e]
    over = tab_ref[off_f + i * N_EXPERTS + e]

    @pl.when(step == 0)
    def _():
        def zero(sb, carry):
            ovf_ref[pl.ds(pl.multiple_of(sb * sbk, sbk), sbk), :] = jnp.zeros((sbk, D_MODEL), F32)
            return carry

        lax.fori_loop(0, nsb, zero, 0)
        pid = lax.broadcasted_iota(jnp.int32, (slots, sbk), 0).astype(F32)

        def select(sb, carry):
            q0 = qrow_ref[0, pl.ds(sb, 1), :]
            q1 = qrow_ref[1, pl.ds(sb, 1), :]
            s = jnp.where(q0 == pid, 1.0, jnp.where(q1 == pid, 1.0, 0.0)).astype(BF16)
            row0 = pl.multiple_of(sb * sbk, sbk)
            xs_ref[sb] = _dot(s, h_ref[pl.ds(row0, sbk), :]).astype(BF16)
            return carry

        lax.fori_loop(0, nsb, select, 0)

    def run_rows(first, n):
        first = pl.multiple_of(first, 16)
        x = jnp.concatenate([xs_ref[sb, pl.ds(first, n), :] for sb in range(nsb)], axis=0)
        y = _mlp(x, wg_ref[0], wu_ref[0], wd_ref[0])
        for sb in range(nsb):
            xs_ref[sb, pl.ds(first, n), :] = y[sb * n:(sb + 1) * n]

    @pl.when(step < N_EXPERTS)
    def _():
        n_pairs = n_rows // 32

        def pair(t, carry):
            run_rows(row_a + t * 32, 32)
            return carry

        lax.fori_loop(0, n_pairs, pair, 0)

        @pl.when(n_rows % 32 != 0)
        def _():
            run_rows(row_a + n_pairs * 32, 16)

    @pl.when(jnp.logical_and(step < N_EXPERTS, over > 0))
    def _():
        rid = lax.broadcasted_iota(jnp.int32, (LANES, sbk), 0).astype(F32)
        wg, wu, wd = wg_ref[0], wu_ref[0], wd_ref[0]

        def sub_block(sb, c1):
            n_over = jnp.maximum(tab_ref[(i * nsb + sb) * N_EXPERTS + e] - n_rows, 0)
            row0 = pl.multiple_of(sb * sbk, sbk)

            def one_pass(s, c2):
                base = (n_rows + s * LANES).astype(F32)
                hit = pos_ref[pl.ds(e, 1), pl.ds(sb, 1), :].reshape(1, sbk) == (rid + base)
                sel = jnp.where(hit, 1.0, 0.0).astype(BF16)
                yo = _mlp(_dot(sel, h_ref[pl.ds(row0, sbk), :]).astype(BF16), wg, wu, wd)
                wct = jnp.where(hit, w_ref[pl.ds(e, 1), pl.ds(sb, 1), :].reshape(1, sbk), 0.0).T
                ovf_ref[pl.ds(row0, sbk), :] += _dot(wct.astype(BF16), yo)
                return c2

            return lax.fori_loop(0, (n_over + LANES - 1) // LANES, one_pass, c1)

        lax.fori_loop(0, nsb, sub_block, 0)

    @pl.when(step >= N_EXPERTS)
    def _():
        lane = lax.broadcasted_iota(jnp.int32, (sbk, slots), 1).astype(F32)
        for k in range(csb):
            sb = (step - N_EXPERTS) * csb + k
            rows = slice(k * sbk, (k + 1) * sbk)
            qc = qcol_ref[rows, :]
            wmat = (jnp.where(lane == qc[:, 0:1], qc[:, 2:3], 0.0)
                    + jnp.where(lane == qc[:, 1:2], qc[:, 3:4], 0.0)).astype(BF16)
            y = _dot(wmat, xs_ref[sb]) + ovf_ref[pl.ds(pl.multiple_of(sb * sbk, sbk), sbk), :]
            o_ref[rows, :] = x1_ref[rows, :] + _rms(y) * gg2_ref[...]


def _moe(tab, h2, qrow3, pos3, w3, wg, wu, wd, qcol, x1, gg2, *, tb, sbk, slots, csb, off_r, off_o, off_f):
    t = h2.shape[0]
    nsb = tb // sbk
    comb_steps = nsb // csb
    wblk = lambda i, s, c: (jnp.minimum(s, N_EXPERTS - 1), 0, 0)
    oblk = lambda i, s, c: (i * comb_steps + jnp.maximum(s - N_EXPERTS, 0), 0)
    r3 = pl.BlockSpec((N_EXPERTS, nsb, sbk), lambda i, s, c: (0, i, 0))
    return pl.pallas_call(
        functools.partial(_moe_kernel, sbk=sbk, slots=slots, csb=csb,
                          off_r=off_r, off_o=off_o, off_f=off_f),
        grid_spec=pltpu.PrefetchScalarGridSpec(
            num_scalar_prefetch=1,
            grid=(t // tb, N_EXPERTS + comb_steps),
            in_specs=[pl.BlockSpec((tb, D_MODEL), lambda i, s, c: (i, 0)),
                      pl.BlockSpec((8, nsb, sbk), lambda i, s, c: (0, i, 0)),
                      r3, r3,
                      pl.BlockSpec((1, D_MODEL, D_EXPERT), wblk),
                      pl.BlockSpec((1, D_MODEL, D_EXPERT), wblk),
                      pl.BlockSpec((1, D_EXPERT, D_MODEL), wblk),
                      pl.BlockSpec((csb * sbk, LANES), oblk),
                      pl.BlockSpec((csb * sbk, D_MODEL), oblk),
                      pl.BlockSpec((1, D_MODEL), lambda i, s, c: (0, 0))],
            out_specs=pl.BlockSpec((csb * sbk, D_MODEL), oblk),
            scratch_shapes=[pltpu.VMEM((nsb, slots, D_MODEL), BF16),
                            pltpu.VMEM((tb, D_MODEL), F32)]),
        out_shape=jax.ShapeDtypeStruct((t, D_MODEL), F32),
        compiler_params=pltpu.CompilerParams(dimension_semantics=("parallel", "arbitrary"),
                                             vmem_limit_bytes=MOE_VMEM_LIMIT),
        name="moe",
    )(tab, h2, qrow3, pos3, w3, wg, wu, wd, qcol, x1, gg2)


def _slot_tables(cnt, *, slots):
    need = ((jnp.max(cnt, axis=1) + 15) // 16) * 16
    start = jnp.cumsum(need, axis=1) - need
    rows = jnp.clip(slots - start, 0, need)
    over = jnp.max(jnp.maximum(cnt - rows[:, None, :], 0), axis=1)
    return rows, start, over


def _pos_tables(rows):
    quarter = D_MODEL // 4
    freq = 1.0 / (POS_BASE ** (np.arange(quarter, dtype=np.float64) / quarter))
    r = np.arange(rows, dtype=np.float64)[:, None] * freq
    cl = np.arange(GRID_W, dtype=np.float64)[:, None] * freq
    er = np.concatenate([np.sin(r), np.cos(r)], axis=-1).astype(np.float32)
    ec = np.concatenate([np.sin(cl), np.cos(cl)], axis=-1).astype(np.float32)
    return jnp.asarray(er[:, None, :]), jnp.asarray(ec[None, :, :])


def _dft_tables(t):
    n = np.arange(CHUNK, dtype=np.int64)
    prod = n[:, None] * n[None, :]
    ang = (prod % CHUNK).astype(np.float64) * (2.0 * np.pi / CHUNK)
    c, s = np.cos(ang), np.sin(ang)
    f1 = np.concatenate([c, s], axis=0)
    a2 = np.concatenate([np.concatenate([c, -s], axis=1), np.concatenate([s, c], axis=1)], axis=0)
    cs = np.concatenate([c, -s], axis=0)
    angw = prod.astype(np.float64) * (2.0 * np.pi / t)
    f32 = lambda a: jnp.asarray(a.astype(np.float32))
    return f32(f1), f32(a2), f32(cs), f32(np.cos(angw)[:, :, None]), f32(np.sin(angw)[:, :, None])


def _blockdiag(w):
    n = w.shape[0]
    size = n * QKV_BLOCK
    spread = np.tile(np.eye(QKV_BLOCK, dtype=np.float32), (1, n))
    rows = jnp.dot(w.reshape(size, QKV_BLOCK), jnp.asarray(spread), precision=lax.Precision.HIGHEST)
    blk = np.arange(size) // QKV_BLOCK
    mask = (blk[:, None] == blk[None, :]).astype(np.float32)
    return rows * jnp.asarray(mask)


def _gate_weights(w_f, b_f, w_b, b_b):
    w = jnp.concatenate([w_f[:, :HEADS], w_b[:, :HEADS], w_f[:, HEADS:], w_b[:, HEADS:]], axis=1).T
    b = jnp.concatenate([b_f[:HEADS], b_b[:HEADS], b_f[HEADS:], b_b[HEADS:]])
    return w.astype(BF16), b[:, None]


def kernel(x, c, ctx, c_ctx, w_ada, b_ada, g_pre_mix, g_post_mix, g_pre_ffn, g_post_ffn,
           w_in, conv_w, conv_b, w_q, w_k, w_v, w_if_fwd, b_if_fwd, w_if_bwd, b_if_bwd,
           mlstm_norm_w, mlstm_skip, w_fourier, w_out, w_router_group, b_router_group,
           w_router_expert, b_router_expert, w_gate, w_up, w_down):
    t = x.shape[1]
    rows = t // GRID_W

    c8 = jnp.concatenate([c, c_ctx[None, :], jnp.zeros((6, D_MODEL), F32)], axis=0)
    mod = _ada(c8, w_ada[0], b_ada[0][None, :])
    shift1, scale1, gate1, shift2, scale2, gate2 = [mod[0:1, k * D_MODEL:(k + 1) * D_MODEL] for k in range(N_MOD)]
    shift1c, scale1c = mod[1:2, 0:D_MODEL], mod[1:2, D_MODEL:2 * D_MODEL]
    g1 = g_pre_mix[0][None, :] * (1.0 + scale1)
    g1c = g_pre_mix[0][None, :] * (1.0 + scale1c)
    gg1 = g_post_mix[0][None, :] * gate1
    g2 = g_pre_ffn[0][None, :] * (1.0 + scale2)
    gg2 = g_post_ffn[0][None, :] * gate2

    er3, ec3 = _pos_tables(rows)
    x3 = x.reshape(rows, GRID_W, D_MODEL)
    ctx3 = ctx.reshape(CTX_LEN // GRID_W, GRID_W, D_MODEL)
    w_in_bf = w_in[0].astype(BF16)

    xm_l, z_l, u_l = _inproj(x3, er3, ec3, g1, shift1, w_in_bf, rows=8, add_pos=True)
    xm_c, _, _ = _inproj(ctx3, er3, ec3, g1c, shift1c, w_in_bf, rows=CTX_LEN // GRID_W, add_pos=False)

    wq = _blockdiag(w_q[0]).astype(BF16)
    wkt = _blockdiag(w_k[0]).T.astype(BF16)
    wv = _blockdiag(w_v[0]).astype(BF16)
    wi, bi = _gate_weights(w_if_fwd[0], b_if_fwd[0], w_if_bwd[0], b_if_bwd[0])
    wiq, wik, wiv = wi[:, :D_MLSTM], wi[:, D_MLSTM:2 * D_MLSTM], wi[:, 2 * D_MLSTM:]
    cb = conv_b[0][None, :]
    q_l, kt_l, v_l, act_l, gp_l = _feat(xm_l, conv_w[0], cb, wq, wkt, wv, wiq, wik, wiv, bi, tm=512)
    q_c, kt_c, v_c, _, gp_c = _feat(xm_c, conv_w[0], cb, wq, wkt, wv, wiq, wik, wiv, bi, tm=CTX_LEN)

    dr_l, gc_l = _gates(gp_l, tl=2048)
    dr_c, gc_c = _gates(gp_c, tl=CTX_LEN)

    c0 = jnp.zeros((2 * HEADS, DH, 2 * DH), F32)
    m0 = jnp.zeros((2 * HEADS, 8, LANES), F32)
    c_ctx_fin, m_ctx_fin = _mlstm(q_c, kt_c, v_c, gc_c, dr_c, c0, m0, cps=CTX_LEN // CHUNK, emit=False)
    hf, hb, _, _ = _mlstm(q_l, kt_l, v_l, gc_l, dr_l, c_ctx_fin, m_ctx_fin, cps=2, emit=True)

    f1, a2, cs, cw3, sw3 = _dft_tables(t)
    n1 = t // CHUNK
    yc, ys = _dft1(u_l.reshape(FGROUPS, n1, CHUNK, FCG), f1.astype(BF16))
    mix = jnp.einsum('kc,gcd->gkd', cs, w_fourier[0], precision=lax.Precision.HIGHEST)
    mix = (mix * float(1.0 / np.sqrt(float(t) * FCG))).astype(BF16)
    yf = _dft2(yc.reshape(FGROUPS, t, FCG), ys.reshape(FGROUPS, t, FCG), cw3, sw3, a2.astype(BF16), mix)
    yf = yf.reshape(FGROUPS, t, FCG)

    wr = jnp.concatenate([w_router_group[0], w_router_expert[0],
                          jnp.zeros((D_MODEL, LANES - N_GROUPS - N_EXPERTS), F32)], axis=1)
    br = jnp.concatenate([b_router_group[0], b_router_expert[0],
                          jnp.zeros((LANES - N_GROUPS - N_EXPERTS,), F32)])[None, :]
    x1, h2, lg = _merge(hf, hb, act_l, z_l, yf, x3, er3, ec3,
                        mlstm_norm_w[0][None, :], mlstm_skip[0][None, :], w_out[0].astype(BF16),
                        gg1, g2, shift2, wr.astype(BF16), br, rows=8)
    pos, wts, cnt = _route(lg, tl=MOE_TB, sbk=MOE_SBK)
    nblk, nsb = t // MOE_TB, MOE_TB // MOE_SBK
    cnt = cnt.reshape(nblk, N_EXPERTS, LANES)[:, :, :nsb]
    cnt = jnp.transpose(cnt, (0, 2, 1)).astype(jnp.int32)
    rows, start, over = _slot_tables(cnt, slots=MOE_SLOTS)
    tab = jnp.concatenate([a.reshape(-1) for a in (cnt, rows, start, over)])
    off_r = nblk * nsb * N_EXPERTS
    off_o = off_r + nblk * N_EXPERTS
    off_f = off_o + nblk * N_EXPERTS
    qrow, qcol = _slots(tab, pos, wts, tl=MOE_TB, off_r=off_r, off_o=off_o)
    sub = (t // MOE_SBK, MOE_SBK)
    out = _moe(tab, h2, qrow.reshape(8, *sub), pos.reshape(N_EXPERTS, *sub), wts.reshape(N_EXPERTS, *sub),
               w_gate[0].astype(BF16), w_up[0].astype(BF16), w_down[0].astype(BF16),
               qcol, x1, gg2, tb=MOE_TB, sbk=MOE_SBK, slots=MOE_SLOTS, csb=MOE_CSB,
               off_r=off_r, off_o=off_o, off_f=off_f)
    return out[None]
```

```python
import functools

import numpy as np
import jax
import jax.numpy as jnp
from jax import lax
from jax.experimental import pallas as pl
from jax.experimental.pallas import tpu as pltpu

F32 = jnp.float32
BF16 = jnp.bfloat16

D_MODEL = 1024
SEQ = 16384
GRID_W = 64
CTX_LEN = 256
D_MLSTM = 512
HEADS = 4
DH = 128
QKV_BLOCK = 4
CONV_K = 3
CHUNK = 128
D_FOURIER = 512
FGROUPS = 4
FCG = 128
N_GROUPS = 4
EPG = 4
N_EXPERTS = 16
D_EXPERT = 512
N_MOD = 6
EPS = 1e-6
POS_BASE = 10000.0
LANES = 128
NEG_BIG = -3.0e38

VMEM_LIMIT = 52 * 1024 * 1024
MOE_VMEM_LIMIT = 56 * 1024 * 1024
MOE_TB = 2048
MOE_SBK = 256
MOE_SLOTS = 1024
MOE_CSB = 2


def _cparams(*sem):
    return pltpu.CompilerParams(dimension_semantics=sem, vmem_limit_bytes=VMEM_LIMIT)


def _dot(a, b):
    return jnp.dot(a, b, preferred_element_type=F32)


def _dot_nt(a, b):
    return lax.dot_general(a, b, (((1,), (1,)), ((), ())), preferred_element_type=F32)


def _split_bf16(a):
    hi = a.astype(BF16)
    lo = (a - hi.astype(F32)).astype(BF16)
    return hi, lo


def _dot3(a, b):
    a_hi, a_lo = _split_bf16(a)
    b_hi, b_lo = _split_bf16(b)
    return _dot(a_hi, b_hi) + (_dot(a_hi, b_lo) + _dot(a_lo, b_hi))


def _sigmoid(x):
    return 1.0 / (1.0 + jnp.exp(-x))


def _rms(x):
    return x * lax.rsqrt(jnp.mean(x * x, axis=-1, keepdims=True) + EPS)


def _ada_kernel(c_ref, w_ref, b_ref, o_ref):
    c = c_ref[...]
    s = c * _sigmoid(c)
    o_ref[...] = _dot3(s, w_ref[...]) + b_ref[...]


def _ada(c8, w, b):
    n = w.shape[1]
    tn = 768
    return pl.pallas_call(
        _ada_kernel,
        grid=(n // tn,),
        in_specs=[pl.BlockSpec((8, D_MODEL), lambda j: (0, 0)),
                  pl.BlockSpec((D_MODEL, tn), lambda j: (0, j)),
                  pl.BlockSpec((1, tn), lambda j: (0, j))],
        out_specs=pl.BlockSpec((8, tn), lambda j: (0, j)),
        out_shape=jax.ShapeDtypeStruct((8, n), F32),
        compiler_params=_cparams("parallel"),
        name="ada",
    )(c8, w, b)


def _add_pos(x3, er_ref, ec_ref):
    r = x3.shape[0]
    pr = jnp.broadcast_to(er_ref[...], (r, GRID_W, D_MODEL // 2))
    pc = jnp.broadcast_to(ec_ref[...], (r, GRID_W, D_MODEL // 2))
    return x3 + jnp.concatenate([pr, pc], axis=-1)


def _inproj_kernel(x_ref, er_ref, ec_ref, g_ref, sh_ref, w_ref, xm_ref, z_ref, u_ref, *, add_pos):
    x3 = x_ref[...]
    if add_pos:
        x3 = _add_pos(x3, er_ref, ec_ref)
    x = x3.reshape(x3.shape[0] * GRID_W, D_MODEL)
    h = _rms(x) * g_ref[...] + sh_ref[...]
    proj = _dot(h.astype(BF16), w_ref[...])
    xm_ref[...] = proj[:, :D_MLSTM].astype(BF16)
    z_ref[...] = proj[:, D_MLSTM:2 * D_MLSTM].astype(BF16)
    for g in range(FGROUPS):
        u_ref[g] = proj[:, 2 * D_MLSTM + g * FCG:2 * D_MLSTM + (g + 1) * FCG]


def _inproj(x3, er3, ec3, g_eff, shift, w_in, *, rows, add_pos):
    nr = x3.shape[0]
    t = nr * GRID_W
    tm = rows * GRID_W
    out = jax.ShapeDtypeStruct((t, D_MLSTM), BF16)
    ospec = pl.BlockSpec((tm, D_MLSTM), lambda i: (i, 0))
    vec = pl.BlockSpec((1, D_MODEL), lambda i: (0, 0))
    return pl.pallas_call(
        functools.partial(_inproj_kernel, add_pos=add_pos),
        grid=(nr // rows,),
        in_specs=[pl.BlockSpec((rows, GRID_W, D_MODEL), lambda i: (i, 0, 0)),
                  pl.BlockSpec((rows, 1, D_MODEL // 2), lambda i: (i, 0, 0)),
                  pl.BlockSpec((1, GRID_W, D_MODEL // 2), lambda i: (0, 0, 0)),
                  vec, vec,
                  pl.BlockSpec(w_in.shape, lambda i: (0, 0))],
        out_specs=[ospec, ospec, pl.BlockSpec((FGROUPS, tm, FCG), lambda i: (0, i, 0))],
        out_shape=[out, out, jax.ShapeDtypeStruct((FGROUPS, t, FCG), F32)],
        compiler_params=_cparams("parallel"),
        name="inproj",
    )(x3, er3, ec3, g_eff, shift, w_in)


def _feat_kernel(xm_ref, prev_ref, next_ref, cw_ref, cb_ref, wq_ref, wkt_ref, wv_ref,
                 wiq_ref, wik_ref, wiv_ref, bi_ref,
                 q_ref, kt_ref, v_ref, act_ref, g_ref):
    i = pl.program_id(0)
    n = pl.num_programs(0)
    xm_bf = xm_ref[...]
    xm = xm_bf.astype(F32)
    tm = xm.shape[0]
    prev_row = prev_ref[...].astype(F32)[15:16, :] * jnp.where(i > 0, 1.0, 0.0)
    next_row = next_ref[...].astype(F32)[0:1, :] * jnp.where(i < n - 1, 1.0, 0.0)
    rid = lax.broadcasted_iota(jnp.int32, (tm, 1), 0)
    x_left = jnp.where(rid == 0, prev_row, pltpu.roll(xm, 1, 0))
    x_right = jnp.where(rid == tm - 1, next_row, pltpu.roll(xm, tm - 1, 0))
    cw = cw_ref[...]
    y = cw[0:1] * x_left + cw[1:2] * xm + cw[2:3] * x_right + cb_ref[...]
    act = (y * _sigmoid(y)).astype(BF16)
    act_ref[...] = act
    q = _dot(act, wq_ref[...])
    kt = _dot_nt(wkt_ref[...], act)
    v = _dot(xm_bf, wv_ref[...])
    q_bf = q.astype(BF16)
    kt_bf = kt.astype(BF16)
    v_bf = v.astype(BF16)
    q_ref[...] = (q * (DH ** -0.5)).astype(BF16)
    kt_ref[...] = kt_bf
    v_ref[...] = v_bf
    g = _dot_nt(wiq_ref[...], q_bf) + _dot(wik_ref[...], kt_bf) + _dot_nt(wiv_ref[...], v_bf)
    g_ref[...] = g + bi_ref[...]


def _feat(xm, conv_w, conv_b, wq, wkt, wv, wiq, wik, wiv, bi, *, tm):
    t = xm.shape[0]
    nb16 = t // 16
    k16 = tm // 16
    full = lambda a: pl.BlockSpec(a.shape, lambda i: (0,) * a.ndim)
    tok = pl.BlockSpec((tm, D_MLSTM), lambda i: (i, 0))
    return pl.pallas_call(
        _feat_kernel,
        grid=(t // tm,),
        in_specs=[tok,
                  pl.BlockSpec((16, D_MLSTM), lambda i: (jnp.maximum(i * k16 - 1, 0), 0)),
                  pl.BlockSpec((16, D_MLSTM), lambda i: (jnp.minimum((i + 1) * k16, nb16 - 1), 0)),
                  full(conv_w), full(conv_b), full(wq), full(wkt), full(wv),
                  full(wiq), full(wik), full(wiv), full(bi)],
        out_specs=[tok,
                   pl.BlockSpec((D_MLSTM, tm), lambda i: (0, i)),
                   tok, tok,
                   pl.BlockSpec((16, tm), lambda i: (0, i))],
        out_shape=[jax.ShapeDtypeStruct((t, D_MLSTM), BF16),
                   jax.ShapeDtypeStruct((D_MLSTM, t), BF16),
                   jax.ShapeDtypeStruct((t, D_MLSTM), BF16),
                   jax.ShapeDtypeStruct((t, D_MLSTM), BF16),
                   jax.ShapeDtypeStruct((16, t), F32)],
        compiler_params=_cparams("parallel"),
        name="feat",
    )(xm, xm, xm, conv_w, conv_b, wq, wkt, wv, wiq, wik, wiv, bi)


def _gates_kernel(g_ref, d_ref, gc_ref):
    g = g_ref[...]
    tl = g.shape[1]
    ig = g[0:8]
    fg = g[8:16]
    lf = jnp.minimum(fg, 0.0) - jnp.log(1.0 + jnp.exp(-jnp.abs(fg)))
    pos = lax.broadcasted_iota(jnp.int32, (8, tl), 1) & (CHUNK - 1)
    is_fwd = lax.broadcasted_iota(jnp.int32, (8, tl), 0) < HEADS

    def scan(x, op, ident):
        xf = x
        xb = x
        k = 1
        while k < CHUNK:
            xf = op(xf, jnp.where(pos >= k, pltpu.roll(xf, k, 1), ident))
            xb = op(xb, jnp.where(pos < CHUNK - k, pltpu.roll(xb, tl - k, 1), ident))
            k *= 2
        return jnp.where(is_fwd, xf, xb)

    b = scan(lf, jnp.add, 0.0)
    d = ig - b
    mloc = b + scan(d, jnp.maximum, NEG_BIG)
    d_ref[...] = d
    stack = jnp.concatenate([b, mloc, jnp.zeros((LANES - 16, tl), F32)], axis=0)
    gc_ref[...] = stack.T


def _gates(g, *, tl):
    t = g.shape[1]
    return pl.pallas_call(
        _gates_kernel,
        grid=(t // tl,),
        in_specs=[pl.BlockSpec((16, tl), lambda i: (0, i))],
        out_specs=[pl.BlockSpec((8, tl), lambda i: (0, i)),
                   pl.BlockSpec((tl, LANES), lambda i: (i, 0))],
        out_shape=[jax.ShapeDtypeStruct((8, t), F32),
                   jax.ShapeDtypeStruct((t, LANES), F32)],
        compiler_params=_cparams("parallel"),
        name="gates",
    )(g)


def _mlstm_kernel(*refs, cps, emit, ncast):
    (qf_ref, kf_ref, vf_ref, gcf_ref, drf_ref,
     qb_ref, kb_ref, vb_ref, gcb_ref, drb_ref, c0_ref, m0_ref) = refs[:12]
    cast_in = refs[12:12 + ncast]
    outs = refs[12 + ncast:]
    if emit:
        hf_ref, hb_ref, cfin_ref, mfin_ref = outs[:4]
        outs = outs[4:]
    else:
        cfin_ref, mfin_ref = outs[:2]
        outs = outs[2:]
        hf_ref = hb_ref = None
    cast_out = outs[:ncast]
    c_scr, m_scr = outs[ncast:]
    step = pl.program_id(0)

    for src, dst in zip(cast_in, cast_out):
        dst[...] = src[...].astype(BF16)

    @pl.when(step == 0)
    def _():
        c_scr[...] = c0_ref[...]
        m_scr[...] = m0_ref[...]

    ti = lax.broadcasted_iota(jnp.int32, (CHUNK, CHUNK), 0)
    si = lax.broadcasted_iota(jnp.int32, (CHUNK, CHUNK), 1)
    ones_col = jnp.where(si == 0, 1.0, 0.0).astype(BF16)

    for j in range(cps):
        for d in range(2):
            if d == 0:
                q_ref, k_ref, v_ref, gc_ref, dr_ref, h_ref = qf_ref, kf_ref, vf_ref, gcf_ref, drf_ref, hf_ref
                r0, mask, last = j * CHUNK, si <= ti, CHUNK - 1
            else:
                q_ref, k_ref, v_ref, gc_ref, dr_ref, h_ref = qb_ref, kb_ref, vb_ref, gcb_ref, drb_ref, hb_ref
                r0, mask, last = (cps - 1 - j) * CHUNK, si >= ti, 0
            rows = slice(r0, r0 + CHUNK)
            gc = gc_ref[rows, :]
            dr = dr_ref[:, rows]
            for hd in range(HEADS):
                r = d * HEADS + hd
                cols = slice(hd * DH, (hd + 1) * DH)
                b = gc[:, r:r + 1]
                mloc = gc[:, 8 + r:9 + r]
                drow = dr[r:r + 1, :]
                m0 = m_scr[r][0:1, 0:1]
                c_aug = c_scr[r]
                kt = k_ref[cols, rows]
                vaug = jnp.concatenate([v_ref[rows, cols], ones_col], axis=1)
                if emit:
                    qh = q_ref[rows, cols]
                    bm = b + m0
                    m_t = jnp.maximum(bm, mloc)
                    dmat = jnp.where(mask, jnp.exp((b - m_t) + drow), 0.0)
                    smat = (_dot(qh, kt) * dmat).astype(BF16)
                    q_in = (qh.astype(F32) * jnp.exp(bm - m_t)).astype(BF16)
                    num = _dot(jnp.concatenate([smat, q_in], axis=1),
                               jnp.concatenate([vaug, c_aug.astype(BF16)], axis=0))
                    den = jnp.maximum(jnp.abs(num[:, DH:DH + 1]), jnp.exp(-m_t))
                    h_ref[rows, cols] = (num[:, :DH] / den).astype(h_ref.dtype)
                btot = gc[last:last + 1, r:r + 1]
                amax = gc[last:last + 1, 8 + r:9 + r]
                m_new = jnp.maximum(btot + m0, amax)
                decay = jnp.exp(btot + m0 - m_new)
                kw = (kt.astype(F32) * jnp.exp(btot + drow - m_new)).astype(BF16)
                c_scr[r] = decay * c_aug + _dot(kw, vaug)
                m_scr[r] = jnp.broadcast_to(m_new, (8, LANES))

    @pl.when(step == pl.num_programs(0) - 1)
    def _():
        cfin_ref[...] = c_scr[...]
        mfin_ref[...] = m_scr[...]


def _mlstm(q, kt, v, gc, dr, c0, m0, *, cps, emit, casts=()):
    t = q.shape[0]
    cb = cps * CHUNK
    nb = t // cb
    fwd_r = lambda i: (i, 0)
    bwd_r = lambda i: (nb - 1 - i, 0)
    fwd_c = lambda i: (0, i)
    bwd_c = lambda i: (0, nb - 1 - i)
    tok = lambda f: pl.BlockSpec((cb, D_MLSTM), f)
    in_specs = []
    for fr, fc in ((fwd_r, fwd_c), (bwd_r, bwd_c)):
        in_specs += [tok(fr), pl.BlockSpec((D_MLSTM, cb), fc), tok(fr),
                     pl.BlockSpec((cb, LANES), fr), pl.BlockSpec((8, cb), fc)]
    cshape = (2 * HEADS, DH, 2 * DH)
    mshape = (2 * HEADS, 8, LANES)
    cspec = pl.BlockSpec(cshape, lambda i: (0, 0, 0))
    mspec = pl.BlockSpec(mshape, lambda i: (0, 0, 0))
    in_specs += [cspec, mspec]
    out_specs = [cspec, mspec]
    out_shape = [jax.ShapeDtypeStruct(cshape, F32), jax.ShapeDtypeStruct(mshape, F32)]
    if emit:
        out_specs = [tok(fwd_r), tok(bwd_r)] + out_specs
        out_shape = [jax.ShapeDtypeStruct((t, D_MLSTM), BF16)] * 2 + out_shape
    for a in casts:
        per = nb // a.shape[0]
        spec = pl.BlockSpec((1, a.shape[1] // per, a.shape[2]), lambda i, per=per: (i // per, i % per, 0))
        in_specs.append(spec)
        out_specs.append(spec)
        out_shape.append(jax.ShapeDtypeStruct(a.shape, BF16))
    return pl.pallas_call(
        functools.partial(_mlstm_kernel, cps=cps, emit=emit, ncast=len(casts)),
        grid=(nb,),
        in_specs=in_specs,
        out_specs=out_specs,
        out_shape=out_shape,
        scratch_shapes=[pltpu.VMEM(cshape, F32), pltpu.VMEM(mshape, F32)],
        compiler_params=_cparams("arbitrary"),
        name="mlstm",
    )(q, kt, v, gc, dr, q, kt, v, gc, dr, c0, m0, *casts)


FFT_ROWS = 8


def _dft1_kernel(u_ref, f_ref, yc_ref, ys_ref):
    f = f_ref[...]
    rows = CHUNK * FFT_ROWS
    u2 = u_ref.reshape(FGROUPS * rows, FCG)
    yc2 = yc_ref.reshape(FGROUPS * rows, FCG)
    ys2 = ys_ref.reshape(FGROUPS * rows, FCG)
    for s in range(FFT_ROWS):
        pick = [pl.ds(g * rows + s, CHUNK, stride=FFT_ROWS) for g in range(FGROUPS)]
        x = jnp.concatenate([u2[p, :] for p in pick], axis=1).astype(BF16)
        y = _dot(f, x)
        for g in range(FGROUPS):
            yc2[pick[g], :] = y[:CHUNK, g * FCG:(g + 1) * FCG]
            ys2[pick[g], :] = y[CHUNK:, g * FCG:(g + 1) * FCG]


def _dft1(u4, f1):
    blk = pl.BlockSpec((FGROUPS, CHUNK, FFT_ROWS, FCG), lambda j: (0, 0, j, 0))
    out = jax.ShapeDtypeStruct(u4.shape, F32)
    return pl.pallas_call(
        _dft1_kernel,
        grid=(u4.shape[2] // FFT_ROWS,),
        in_specs=[blk, pl.BlockSpec(f1.shape, lambda j: (0, 0))],
        out_specs=[blk, blk],
        out_shape=[out, out],
        compiler_params=_cparams("parallel"),
        name="dft1",
    )(u4, f1)


def _dft2_kernel(yc_ref, ys_ref, cw_ref, sw_ref, a2_ref, mix_ref, o_ref):
    a2 = a2_ref[...]
    rows = CHUNK * FFT_ROWS
    o2 = o_ref.reshape(FGROUPS * rows, FCG)
    for kk in range(FFT_ROWS):
        blk = slice(kk * CHUNK, (kk + 1) * CHUNK)
        yc = jnp.concatenate([yc_ref[g, blk, :] for g in range(FGROUPS)], axis=1)
        ys = jnp.concatenate([ys_ref[g, blk, :] for g in range(FGROUPS)], axis=1)
        cw = cw_ref[kk]
        sw = sw_ref[kk]
        p = jnp.concatenate([yc * cw - ys * sw, yc * sw + ys * cw], axis=0).astype(BF16)
        x = _dot(a2, p).astype(BF16)
        for g in range(FGROUPS):
            cols = slice(g * FCG, (g + 1) * FCG)
            cat = jnp.concatenate([x[:CHUNK, cols], x[CHUNK:, cols]], axis=1)
            o2[pl.ds(g * rows + kk, CHUNK, stride=FFT_ROWS), :] = _dot(cat, mix_ref[g])


def _dft2(yc, ys, cw3, sw3, a2, mix):
    t = yc.shape[1]
    n1 = t // CHUNK
    tok = pl.BlockSpec((FGROUPS, FFT_ROWS * CHUNK, FCG), lambda i: (0, i, 0))
    tw = pl.BlockSpec((FFT_ROWS, CHUNK, 1), lambda i: (i, 0, 0))
    full = lambda a: pl.BlockSpec(a.shape, lambda i: (0,) * a.ndim)
    return pl.pallas_call(
        _dft2_kernel,
        grid=(n1 // FFT_ROWS,),
        in_specs=[tok, tok, tw, tw, full(a2), full(mix)],
        out_specs=pl.BlockSpec((FGROUPS, CHUNK, FFT_ROWS, FCG), lambda i: (0, 0, i, 0)),
        out_shape=jax.ShapeDtypeStruct((FGROUPS, CHUNK, n1, FCG), F32),
        compiler_params=_cparams("parallel"),
        name="dft2",
    )(yc, ys, cw3, sw3, a2, mix)


def _merge_kernel(hf_ref, hb_ref, act_ref, z_ref, yf_ref, x_ref, er_ref, ec_ref,
                  nw_ref, sk_ref, wout_ref, gg1_ref, g2_ref, sh2_ref, wr_ref, br_ref,
                  x1_ref, h2_ref, lg_ref):
    h = hf_ref[...].astype(F32) + hb_ref[...].astype(F32)
    parts = []
    for hd in range(HEADS):
        hh = h[:, hd * DH:(hd + 1) * DH]
        dl = hh - jnp.mean(hh, axis=-1, keepdims=True)
        var = jnp.mean(dl * dl, axis=-1, keepdims=True)
        parts.append(dl * lax.rsqrt(var + EPS))
    hn = jnp.concatenate(parts, axis=-1)
    z = z_ref[...].astype(F32)
    m = (hn * nw_ref[...] + sk_ref[...] * act_ref[...].astype(F32)) * (z * _sigmoid(z))
    cat = jnp.concatenate([m.astype(BF16)] + [yf_ref[g].astype(BF16) for g in range(FGROUPS)], axis=-1)
    y = _dot(cat, wout_ref[...])
    x3 = _add_pos(x_ref[...], er_ref, ec_ref)
    xp = x3.reshape(x3.shape[0] * GRID_W, D_MODEL)
    x1 = xp + _rms(y) * gg1_ref[...]
    x1_ref[...] = x1
    h2 = _rms(x1) * g2_ref[...] + sh2_ref[...]
    h2_ref[...] = h2.astype(BF16)
    lg = _dot(h2.astype(BF16), wr_ref[...]) + br_ref[...]
    lg_ref[...] = lg.T[:32]


def _merge(hf, hb, act, z, yf, x3, er3, ec3, nw, sk, wout, gg1, g2, sh2, wr, br, *, rows):
    nr = x3.shape[0]
    t = nr * GRID_W
    tm = rows * GRID_W
    tok = pl.BlockSpec((tm, D_MLSTM), lambda i: (i, 0))
    full = lambda a: pl.BlockSpec(a.shape, lambda i: (0,) * a.ndim)
    return pl.pallas_call(
        _merge_kernel,
        grid=(nr // rows,),
        in_specs=[tok, tok, tok, tok,
                  pl.BlockSpec((FGROUPS, tm, FCG), lambda i: (0, i, 0)),
                  pl.BlockSpec((rows, GRID_W, D_MODEL), lambda i: (i, 0, 0)),
                  pl.BlockSpec((rows, 1, D_MODEL // 2), lambda i: (i, 0, 0)),
                  pl.BlockSpec((1, GRID_W, D_MODEL // 2), lambda i: (0, 0, 0)),
                  full(nw), full(sk), full(wout), full(gg1), full(g2), full(sh2), full(wr), full(br)],
        out_specs=[pl.BlockSpec((tm, D_MODEL), lambda i: (i, 0)),
                   pl.BlockSpec((tm, D_MODEL), lambda i: (i, 0)),
                   pl.BlockSpec((32, tm), lambda i: (0, i))],
        out_shape=[jax.ShapeDtypeStruct((t, D_MODEL), F32),
                   jax.ShapeDtypeStruct((t, D_MODEL), BF16),
                   jax.ShapeDtypeStruct((32, t), F32)],
        compiler_params=_cparams("parallel"),
        name="merge",
    )(hf, hb, act, z, yf, x3, er3, ec3, nw, sk, wout, gg1, g2, sh2, wr, br)


def _route_kernel(lg_ref, pos_ref, w_ref, cnt_ref, *, sbk):
    lg = lg_ref[...]
    tl = lg.shape[1]
    g = [lg[j:j + 1] for j in range(N_GROUPS)]
    e = [lg[N_GROUPS + j:N_GROUPS + j + 1] for j in range(N_EXPERTS)]
    gmax = jnp.maximum(jnp.maximum(g[0], g[1]), jnp.maximum(g[2], g[3]))
    den = jnp.exp(g[0] - gmax) + jnp.exp(g[1] - gmax) + jnp.exp(g[2] - gmax) + jnp.exp(g[3] - gmax)
    p_sel = 1.0 / den
    sel = []
    free = jnp.ones((1, tl), F32)
    for j in range(N_GROUPS):
        s = jnp.where(g[j] >= gmax, free, 0.0)
        sel.append(s)
        free = free - s
    es = []
    for j in range(EPG):
        es.append(sel[0] * e[j] + sel[1] * e[EPG + j] + sel[2] * e[2 * EPG + j] + sel[3] * e[3 * EPG + j])
    rank = []
    for j in range(EPG):
        rj = jnp.zeros((1, tl), F32)
        for i in range(EPG):
            if i == j:
                continue
            beats = (es[i] >= es[j]) if i < j else (es[i] > es[j])
            rj = rj + jnp.where(beats, 1.0, 0.0)
        rank.append(rj)
    v1 = jnp.maximum(jnp.maximum(es[0], es[1]), jnp.maximum(es[2], es[3]))
    v2 = sum(jnp.where(rank[j] == 1.0, es[j], 0.0) for j in range(EPG))
    tt = jnp.exp(v2 - v1)
    w1 = p_sel / (1.0 + tt)
    w2 = w1 * tt
    w = [jnp.where(rank[j] == 0.0, w1, jnp.where(rank[j] == 1.0, w2, 0.0)) for j in range(EPG)]
    top2 = [jnp.where(rank[j] < 2.0, 1.0, 0.0) for j in range(EPG)]
    mem = jnp.concatenate([sel[gi] * top2[j] for gi in range(N_GROUPS) for j in range(EPG)], axis=0)
    wts = jnp.concatenate([sel[gi] * w[j] for gi in range(N_GROUPS) for j in range(EPG)], axis=0)
    w_ref[...] = wts
    lane = lax.broadcasted_iota(jnp.int32, (N_EXPERTS, tl), 1) & (sbk - 1)
    c = mem
    k = 1
    while k < sbk:
        c = c + jnp.where(lane >= k, pltpu.roll(c, k, 1), 0.0)
        k *= 2
    pos_ref[...] = jnp.where(mem > 0.0, c - 1.0, -1.0)
    lane128 = lax.broadcasted_iota(jnp.int32, (N_EXPERTS, LANES), 1)
    cnt = jnp.zeros((N_EXPERTS, LANES), F32)
    for kb in range(tl // sbk):
        tot = jnp.sum(mem[:, kb * sbk:(kb + 1) * sbk], axis=1, keepdims=True)
        cnt = cnt + jnp.where(lane128 == kb, tot, 0.0)
    cnt_ref[...] = cnt


def _route(lg, *, tl, sbk):
    t = lg.shape[1]
    row = pl.BlockSpec((N_EXPERTS, tl), lambda i: (0, i))
    return pl.pallas_call(
        functools.partial(_route_kernel, sbk=sbk),
        grid=(t // tl,),
        in_specs=[pl.BlockSpec((32, tl), lambda i: (0, i))],
        out_specs=[row, row, pl.BlockSpec((N_EXPERTS, LANES), lambda i: (i, 0))],
        out_shape=[jax.ShapeDtypeStruct((N_EXPERTS, t), F32),
                   jax.ShapeDtypeStruct((N_EXPERTS, t), F32),
                   jax.ShapeDtypeStruct((t // tl * N_EXPERTS, LANES), F32)],
        compiler_params=_cparams("parallel"),
        name="route",
    )(lg)


def _slots_kernel(tab_ref, pos_ref, w_ref, qrow_ref, qcol_ref, *, off_r, off_o):
    i = pl.program_id(0)
    tl = pos_ref.shape[1]
    q0 = jnp.full((1, tl), -1.0, F32)
    q1 = jnp.full((1, tl), -1.0, F32)
    w0 = jnp.zeros((1, tl), F32)
    w1 = jnp.zeros((1, tl), F32)
    seen = jnp.zeros((1, tl), F32)
    for ex in range(N_EXPERTS):
        rk = pos_ref[ex:ex + 1, :]
        wt = w_ref[ex:ex + 1, :]
        rows = tab_ref[off_r + i * N_EXPERTS + ex].astype(F32)
        first_row = tab_ref[off_o + i * N_EXPERTS + ex].astype(F32)
        m = jnp.where(rk >= 0.0, 1.0, 0.0)
        val = jnp.where(rk < rows, rk + first_row, -1.0)
        first = (m * (1.0 - seen)) > 0.0
        second = (m * seen) > 0.0
        q0 = jnp.where(first, val, q0)
        w0 = jnp.where(first, wt, w0)
        q1 = jnp.where(second, val, q1)
        w1 = jnp.where(second, wt, w1)
        seen = seen + m
    qrow_ref[...] = jnp.concatenate([q0, q1, jnp.zeros((6, tl), F32)], axis=0)
    qcol_ref[...] = jnp.concatenate([q0, q1, w0, w1, jnp.zeros((LANES - 4, tl), F32)], axis=0).T


def _slots(tab, pos, w, *, tl, off_r, off_o):
    t = pos.shape[1]
    row = pl.BlockSpec((N_EXPERTS, tl), lambda i, c: (0, i))
    return pl.pallas_call(
        functools.partial(_slots_kernel, off_r=off_r, off_o=off_o),
        grid_spec=pltpu.PrefetchScalarGridSpec(
            num_scalar_prefetch=1,
            grid=(t // tl,),
            in_specs=[row, row],
            out_specs=[pl.BlockSpec((8, tl), lambda i, c: (0, i)),
                       pl.BlockSpec((tl, LANES), lambda i, c: (i, 0))]),
        out_shape=[jax.ShapeDtypeStruct((8, t), F32),
                   jax.ShapeDtypeStruct((t, LANES), F32)],
        compiler_params=_cparams("parallel"),
        name="slots",
    )(tab, pos, w)


def _mlp(x, wg, wu, wd):
    gt = _dot(x, wg)
    a = ((gt * _sigmoid(gt)) * _dot(x, wu)).astype(BF16)
    return _dot(a, wd).astype(BF16)


def _moe_kernel(tab_ref, h_ref, qrow_ref, pos_ref, w_ref, wg_ref, wu_ref, wd_ref,
                qcol_ref, x1_ref, gg2_ref, o_ref, xs_ref, ovf_ref, *, sbk, slots, csb, off_r, off_o, off_f):
    i = pl.program_id(0)
    step = pl.program_id(1)
    nsb = h_ref.shape[0] // sbk
    e = jnp.minimum(step, N_EXPERTS - 1)
    n_rows = tab_ref[off_r + i * N_EXPERTS + e]
    row_a = tab_ref[off_o + i * N_EXPERTS + e]
    over = tab_ref[off_f + i * N_EXPERTS + e]

    @pl.when(step == 0)
    def _():
        def zero(sb, carry):
            ovf_ref[pl.ds(pl.multiple_of(sb * sbk, sbk), sbk), :] = jnp.zeros((sbk, D_MODEL), F32)
            return carry

        lax.fori_loop(0, nsb, zero, 0)
        pid = lax.broadcasted_iota(jnp.int32, (slots, sbk), 0).astype(F32)

        def select(sb, carry):
            q0 = qrow_ref[0, pl.ds(sb, 1), :]
            q1 = qrow_ref[1, pl.ds(sb, 1), :]
            s = jnp.where(q0 == pid, 1.0, jnp.where(q1 == pid, 1.0, 0.0)).astype(BF16)
            row0 = pl.multiple_of(sb * sbk, sbk)
            xs_ref[sb] = _dot(s, h_ref[pl.ds(row0, sbk), :]).astype(BF16)
            return carry

        lax.fori_loop(0, nsb, select, 0)

    def run_rows(first, n):
        first = pl.multiple_of(first, 16)
        x = jnp.concatenate([xs_ref[sb, pl.ds(first, n), :] for sb in range(nsb)], axis=0)
        y = _mlp(x, wg_ref[0], wu_ref[0], wd_ref[0])
        for sb in range(nsb):
            xs_ref[sb, pl.ds(first, n), :] = y[sb * n:(sb + 1) * n]

    @pl.when(step < N_EXPERTS)
    def _():
        n_pairs = n_rows // 32

        def pair(t, carry):
            run_rows(row_a + t * 32, 32)
            return carry

        lax.fori_loop(0, n_pairs, pair, 0)

        @pl.when(n_rows % 32 != 0)
        def _():
            run_rows(row_a + n_pairs * 32, 16)

    @pl.when(jnp.logical_and(step < N_EXPERTS, over > 0))
    def _():
        rid = lax.broadcasted_iota(jnp.int32, (LANES, sbk), 0).astype(F32)
        wg, wu, wd = wg_ref[0], wu_ref[0], wd_ref[0]

        def sub_block(sb, c1):
            n_over = jnp.maximum(tab_ref[(i * nsb + sb) * N_EXPERTS + e] - n_rows, 0)
            row0 = pl.multiple_of(sb * sbk, sbk)

            def one_pass(s, c2):
                base = (n_rows + s * LANES).astype(F32)
                hit = pos_ref[pl.ds(e, 1), pl.ds(sb, 1), :].reshape(1, sbk) == (rid + base)
                sel = jnp.where(hit, 1.0, 0.0).astype(BF16)
                yo = _mlp(_dot(sel, h_ref[pl.ds(row0, sbk), :]).astype(BF16), wg, wu, wd)
                wct = jnp.where(hit, w_ref[pl.ds(e, 1), pl.ds(sb, 1), :].reshape(1, sbk), 0.0).T
                ovf_ref[pl.ds(row0, sbk), :] += _dot(wct.astype(BF16), yo)
                return c2

            return lax.fori_loop(0, (n_over + LANES - 1) // LANES, one_pass, c1)

        lax.fori_loop(0, nsb, sub_block, 0)

    @pl.when(step >= N_EXPERTS)
    def _():
        lane = lax.broadcasted_iota(jnp.int32, (sbk, slots), 1).astype(F32)
        for k in range(csb):
            sb = (step - N_EXPERTS) * csb + k
            rows = slice(k * sbk, (k + 1) * sbk)
            qc = qcol_ref[rows, :]
            wmat = (jnp.where(lane == qc[:, 0:1], qc[:, 2:3], 0.0)
                    + jnp.where(lane == qc[:, 1:2], qc[:, 3:4], 0.0)).astype(BF16)
            y = _dot(wmat, xs_ref[sb]) + ovf_ref[pl.ds(pl.multiple_of(sb * sbk, sbk), sbk), :]
            o_ref[rows, :] = x1_ref[rows, :] + _rms(y) * gg2_ref[...]


def _moe(tab, h2, qrow3, pos3, w3, wg, wu, wd, qcol, x1, gg2, *, tb, sbk, slots, csb, off_r, off_o, off_f):
    t = h2.shape[0]
    nsb = tb // sbk
    comb_steps = nsb // csb
    wblk = lambda i, s, c: (jnp.minimum(s, N_EXPERTS - 1), 0, 0)
    oblk = lambda i, s, c: (i * comb_steps + jnp.maximum(s - N_EXPERTS, 0), 0)
    r3 = pl.BlockSpec((N_EXPERTS, nsb, sbk), lambda i, s, c: (0, i, 0))
    return pl.pallas_call(
        functools.partial(_moe_kernel, sbk=sbk, slots=slots, csb=csb,
                          off_r=off_r, off_o=off_o, off_f=off_f),
        grid_spec=pltpu.PrefetchScalarGridSpec(
            num_scalar_prefetch=1,
            grid=(t // tb, N_EXPERTS + comb_steps),
            in_specs=[pl.BlockSpec((tb, D_MODEL), lambda i, s, c: (i, 0)),
                      pl.BlockSpec((8, nsb, sbk), lambda i, s, c: (0, i, 0)),
                      r3, r3,
                      pl.BlockSpec((1, D_MODEL, D_EXPERT), wblk),
                      pl.BlockSpec((1, D_MODEL, D_EXPERT), wblk),
                      pl.BlockSpec((1, D_EXPERT, D_MODEL), wblk),
                      pl.BlockSpec((csb * sbk, LANES), oblk),
                      pl.BlockSpec((csb * sbk, D_MODEL), oblk),
                      pl.BlockSpec((1, D_MODEL), lambda i, s, c: (0, 0))],
            out_specs=pl.BlockSpec((csb * sbk, D_MODEL), oblk),
            scratch_shapes=[pltpu.VMEM((nsb, slots, D_MODEL), BF16),
                            pltpu.VMEM((tb, D_MODEL), F32)]),
        out_shape=jax.ShapeDtypeStruct((t, D_MODEL), F32),
        compiler_params=pltpu.CompilerParams(dimension_semantics=("parallel", "arbitrary"),
                                             vmem_limit_bytes=MOE_VMEM_LIMIT),
        name="moe",
    )(tab, h2, qrow3, pos3, w3, wg, wu, wd, qcol, x1, gg2)


def _slot_tables(cnt, *, slots):
    need = ((jnp.max(cnt, axis=1) + 15) // 16) * 16
    start = jnp.cumsum(need, axis=1) - need
    rows = jnp.clip(slots - start, 0, need)
    over = jnp.max(jnp.maximum(cnt - rows[:, None, :], 0), axis=1)
    return rows, start, over


def _pos_tables(rows):
    quarter = D_MODEL // 4
    freq = 1.0 / (POS_BASE ** (np.arange(quarter, dtype=np.float64) / quarter))
    r = np.arange(rows, dtype=np.float64)[:, None] * freq
    cl = np.arange(GRID_W, dtype=np.float64)[:, None] * freq
    er = np.concatenate([np.sin(r), np.cos(r)], axis=-1).astype(np.float32)
    ec = np.concatenate([np.sin(cl), np.cos(cl)], axis=-1).astype(np.float32)
    return jnp.asarray(er[:, None, :]), jnp.asarray(ec[None, :, :])


def _dft_tables(t):
    n = np.arange(CHUNK, dtype=np.int64)
    prod = n[:, None] * n[None, :]
    ang = (prod % CHUNK).astype(np.float64) * (2.0 * np.pi / CHUNK)
    c, s = np.cos(ang), np.sin(ang)
    f1 = np.concatenate([c, s], axis=0)
    a2 = np.concatenate([np.concatenate([c, -s], axis=1), np.concatenate([s, c], axis=1)], axis=0)
    cs = np.concatenate([c, -s], axis=0)
    angw = prod.astype(np.float64) * (2.0 * np.pi / t)
    f32 = lambda a: jnp.asarray(a.astype(np.float32))
    return f32(f1), f32(a2), f32(cs), f32(np.cos(angw)[:, :, None]), f32(np.sin(angw)[:, :, None])


def _blockdiag(w):
    n = w.shape[0]
    size = n * QKV_BLOCK
    spread = np.tile(np.eye(QKV_BLOCK, dtype=np.float32), (1, n))
    rows = jnp.dot(w.reshape(size, QKV_BLOCK), jnp.asarray(spread), precision=lax.Precision.HIGHEST)
    blk = np.arange(size) // QKV_BLOCK
    mask = (blk[:, None] == blk[None, :]).astype(np.float32)
    return rows * jnp.asarray(mask)


def _gate_weights(w_f, b_f, w_b, b_b):
    w = jnp.concatenate([w_f[:, :HEADS], w_b[:, :HEADS], w_f[:, HEADS:], w_b[:, HEADS:]], axis=1).T
    b = jnp.concatenate([b_f[:HEADS], b_b[:HEADS], b_f[HEADS:], b_b[HEADS:]])
    return w.astype(BF16), b[:, None]


def kernel(x, c, ctx, c_ctx, w_ada, b_ada, g_pre_mix, g_post_mix, g_pre_ffn, g_post_ffn,
           w_in, conv_w, conv_b, w_q, w_k, w_v, w_if_fwd, b_if_fwd, w_if_bwd, b_if_bwd,
           mlstm_norm_w, mlstm_skip, w_fourier, w_out, w_router_group, b_router_group,
           w_router_expert, b_router_expert, w_gate, w_up, w_down):
    t = x.shape[1]
    rows = t // GRID_W

    c8 = jnp.concatenate([c, c_ctx[None, :], jnp.zeros((6, D_MODEL), F32)], axis=0)
    mod = _ada(c8, w_ada[0], b_ada[0][None, :])
    shift1, scale1, gate1, shift2, scale2, gate2 = [mod[0:1, k * D_MODEL:(k + 1) * D_MODEL] for k in range(N_MOD)]
    shift1c, scale1c = mod[1:2, 0:D_MODEL], mod[1:2, D_MODEL:2 * D_MODEL]
    g1 = g_pre_mix[0][None, :] * (1.0 + scale1)
    g1c = g_pre_mix[0][None, :] * (1.0 + scale1c)
    gg1 = g_post_mix[0][None, :] * gate1
    g2 = g_pre_ffn[0][None, :] * (1.0 + scale2)
    gg2 = g_post_ffn[0][None, :] * gate2

    er3, ec3 = _pos_tables(rows)
    x3 = x.reshape(rows, GRID_W, D_MODEL)
    ctx3 = ctx.reshape(CTX_LEN // GRID_W, GRID_W, D_MODEL)
    w_in_bf = w_in[0].astype(BF16)

    xm_l, z_l, u_l = _inproj(x3, er3, ec3, g1, shift1, w_in_bf, rows=8, add_pos=True)
    xm_c, _, _ = _inproj(ctx3, er3, ec3, g1c, shift1c, w_in_bf, rows=CTX_LEN // GRID_W, add_pos=False)

    wq = _blockdiag(w_q[0]).astype(BF16)
    wkt = _blockdiag(w_k[0]).T.astype(BF16)
    wv = _blockdiag(w_v[0]).astype(BF16)
    wi, bi = _gate_weights(w_if_fwd[0], b_if_fwd[0], w_if_bwd[0], b_if_bwd[0])
    wiq, wik, wiv = wi[:, :D_MLSTM], wi[:, D_MLSTM:2 * D_MLSTM], wi[:, 2 * D_MLSTM:]
    cb = conv_b[0][None, :]
    q_l, kt_l, v_l, act_l, gp_l = _feat(xm_l, conv_w[0], cb, wq, wkt, wv, wiq, wik, wiv, bi, tm=512)
    q_c, kt_c, v_c, _, gp_c = _feat(xm_c, conv_w[0], cb, wq, wkt, wv, wiq, wik, wiv, bi, tm=CTX_LEN)

    dr_l, gc_l = _gates(gp_l, tl=2048)
    dr_c, gc_c = _gates(gp_c, tl=CTX_LEN)

    c0 = jnp.zeros((2 * HEADS, DH, 2 * DH), F32)
    m0 = jnp.zeros((2 * HEADS, 8, LANES), F32)
    c_ctx_fin, m_ctx_fin = _mlstm(q_c, kt_c, v_c, gc_c, dr_c, c0, m0, cps=CTX_LEN // CHUNK, emit=False)
    hf, hb, _, _, wg_bf, wu_bf, wd_bf = _mlstm(q_l, kt_l, v_l, gc_l, dr_l, c_ctx_fin, m_ctx_fin, cps=2, emit=True,
                                               casts=(w_gate[0], w_up[0], w_down[0]))

    f1, a2, cs, cw3, sw3 = _dft_tables(t)
    n1 = t // CHUNK
    yc, ys = _dft1(u_l.reshape(FGROUPS, n1, CHUNK, FCG), f1.astype(BF16))
    mix = jnp.einsum('kc,gcd->gkd', cs, w_fourier[0], precision=lax.Precision.HIGHEST)
    mix = (mix * float(1.0 / np.sqrt(float(t) * FCG))).astype(BF16)
    yf = _dft2(yc.reshape(FGROUPS, t, FCG), ys.reshape(FGROUPS, t, FCG), cw3, sw3, a2.astype(BF16), mix)
    yf = yf.reshape(FGROUPS, t, FCG)

    wr = jnp.concatenate([w_router_group[0], w_router_expert[0],
                          jnp.zeros((D_MODEL, LANES - N_GROUPS - N_EXPERTS), F32)], axis=1)
    br = jnp.concatenate([b_router_group[0], b_router_expert[0],
                          jnp.zeros((LANES - N_GROUPS - N_EXPERTS,), F32)])[None, :]
    x1, h2, lg = _merge(hf, hb, act_l, z_l, yf, x3, er3, ec3,
                        mlstm_norm_w[0][None, :], mlstm_skip[0][None, :], w_out[0].astype(BF16),
                        gg1, g2, shift2, wr.astype(BF16), br, rows=8)
    pos, wts, cnt = _route(lg, tl=MOE_TB, sbk=MOE_SBK)
    nblk, nsb = t // MOE_TB, MOE_TB // MOE_SBK
    cnt = cnt.reshape(nblk, N_EXPERTS, LANES)[:, :, :nsb]
    cnt = jnp.transpose(cnt, (0, 2, 1)).astype(jnp.int32)
    rows, start, over = _slot_tables(cnt, slots=MOE_SLOTS)
    tab = jnp.concatenate([a.reshape(-1) for a in (cnt, rows, start, over)])
    off_r = nblk * nsb * N_EXPERTS
    off_o = off_r + nblk * N_EXPERTS
    off_f = off_o + nblk * N_EXPERTS
    qrow, qcol = _slots(tab, pos, wts, tl=MOE_TB, off_r=off_r, off_o=off_o)
    sub = (t // MOE_SBK, MOE_SBK)
    out = _moe(tab, h2, qrow.reshape(8, *sub), pos.reshape(N_EXPERTS, *sub), wts.reshape(N_EXPERTS, *sub),
               wg_bf, wu_bf, wd_bf,
               qcol, x1, gg2, tb=MOE_TB, sbk=MOE_SBK, slots=MOE_SLOTS, csb=MOE_CSB,
               off_r=off_r, off_o=off_o, off_f=off_f)
    return out[None]
```

```python
import functools

import numpy as np
import jax
import jax.numpy as jnp
from jax import lax
from jax.experimental import pallas as pl
from jax.experimental.pallas import tpu as pltpu

F32 = jnp.float32
BF16 = jnp.bfloat16

D_MODEL = 1024
SEQ = 16384
GRID_W = 64
CTX_LEN = 256
D_MLSTM = 512
HEADS = 4
DH = 128
QKV_BLOCK = 4
CONV_K = 3
CHUNK = 128
D_FOURIER = 512
FGROUPS = 4
FCG = 128
N_GROUPS = 4
EPG = 4
N_EXPERTS = 16
D_EXPERT = 512
N_MOD = 6
EPS = 1e-6
POS_BASE = 10000.0
LANES = 128
NEG_BIG = -3.0e38

VMEM_LIMIT = 52 * 1024 * 1024
MOE_VMEM_LIMIT = 56 * 1024 * 1024
MOE_TB = 2048
MOE_SBK = 256
MOE_SLOTS = 1024
MOE_CSB = 2


def _cparams(*sem):
    return pltpu.CompilerParams(dimension_semantics=sem, vmem_limit_bytes=VMEM_LIMIT)


def _dot(a, b):
    return jnp.dot(a, b, preferred_element_type=F32)


def _dot_nt(a, b):
    return lax.dot_general(a, b, (((1,), (1,)), ((), ())), preferred_element_type=F32)


def _split_bf16(a):
    hi = a.astype(BF16)
    lo = (a - hi.astype(F32)).astype(BF16)
    return hi, lo


def _dot3(a, b):
    a_hi, a_lo = _split_bf16(a)
    b_hi, b_lo = _split_bf16(b)
    return _dot(a_hi, b_hi) + (_dot(a_hi, b_lo) + _dot(a_lo, b_hi))


def _sigmoid(x):
    return 1.0 / (1.0 + jnp.exp(-x))


def _rms(x):
    return x * lax.rsqrt(jnp.mean(x * x, axis=-1, keepdims=True) + EPS)


def _ada_kernel(c_ref, w_ref, b_ref, o_ref):
    c = c_ref[...]
    s = c * _sigmoid(c)
    o_ref[...] = _dot3(s, w_ref[...]) + b_ref[...]


def _ada(c8, w, b):
    n = w.shape[1]
    tn = 768
    return pl.pallas_call(
        _ada_kernel,
        grid=(n // tn,),
        in_specs=[pl.BlockSpec((8, D_MODEL), lambda j: (0, 0)),
                  pl.BlockSpec((D_MODEL, tn), lambda j: (0, j)),
                  pl.BlockSpec((1, tn), lambda j: (0, j))],
        out_specs=pl.BlockSpec((8, tn), lambda j: (0, j)),
        out_shape=jax.ShapeDtypeStruct((8, n), F32),
        compiler_params=_cparams("parallel"),
        name="ada",
    )(c8, w, b)


def _add_pos(x3, er_ref, ec_ref):
    r = x3.shape[0]
    pr = jnp.broadcast_to(er_ref[...], (r, GRID_W, D_MODEL // 2))
    pc = jnp.broadcast_to(ec_ref[...], (r, GRID_W, D_MODEL // 2))
    return x3 + jnp.concatenate([pr, pc], axis=-1)


def _inproj_kernel(x_ref, er_ref, ec_ref, g_ref, sh_ref, w_ref, xm_ref, z_ref, u_ref, *, add_pos):
    x3 = x_ref[...]
    if add_pos:
        x3 = _add_pos(x3, er_ref, ec_ref)
    x = x3.reshape(x3.shape[0] * GRID_W, D_MODEL)
    h = _rms(x) * g_ref[...] + sh_ref[...]
    proj = _dot(h.astype(BF16), w_ref[...])
    xm_ref[...] = proj[:, :D_MLSTM].astype(BF16)
    z_ref[...] = proj[:, D_MLSTM:2 * D_MLSTM].astype(BF16)
    for g in range(FGROUPS):
        u_ref[g] = proj[:, 2 * D_MLSTM + g * FCG:2 * D_MLSTM + (g + 1) * FCG]


def _inproj(x3, er3, ec3, g_eff, shift, w_in, *, rows, add_pos):
    nr = x3.shape[0]
    t = nr * GRID_W
    tm = rows * GRID_W
    out = jax.ShapeDtypeStruct((t, D_MLSTM), BF16)
    ospec = pl.BlockSpec((tm, D_MLSTM), lambda i: (i, 0))
    vec = pl.BlockSpec((1, D_MODEL), lambda i: (0, 0))
    return pl.pallas_call(
        functools.partial(_inproj_kernel, add_pos=add_pos),
        grid=(nr // rows,),
        in_specs=[pl.BlockSpec((rows, GRID_W, D_MODEL), lambda i: (i, 0, 0)),
                  pl.BlockSpec((rows, 1, D_MODEL // 2), lambda i: (i, 0, 0)),
                  pl.BlockSpec((1, GRID_W, D_MODEL // 2), lambda i: (0, 0, 0)),
                  vec, vec,
                  pl.BlockSpec(w_in.shape, lambda i: (0, 0))],
        out_specs=[ospec, ospec, pl.BlockSpec((FGROUPS, tm, FCG), lambda i: (0, i, 0))],
        out_shape=[out, out, jax.ShapeDtypeStruct((FGROUPS, t, FCG), F32)],
        compiler_params=_cparams("parallel"),
        name="inproj",
    )(x3, er3, ec3, g_eff, shift, w_in)


def _feat_kernel(xm_ref, prev_ref, next_ref, cw_ref, cb_ref, wq_ref, wkt_ref, wv_ref,
                 wiq_ref, wik_ref, wiv_ref, bi_ref,
                 q_ref, kt_ref, v_ref, act_ref, g_ref):
    i = pl.program_id(0)
    n = pl.num_programs(0)
    xm_bf = xm_ref[...]
    xm = xm_bf.astype(F32)
    tm = xm.shape[0]
    prev_row = prev_ref[...].astype(F32)[15:16, :] * jnp.where(i > 0, 1.0, 0.0)
    next_row = next_ref[...].astype(F32)[0:1, :] * jnp.where(i < n - 1, 1.0, 0.0)
    rid = lax.broadcasted_iota(jnp.int32, (tm, 1), 0)
    x_left = jnp.where(rid == 0, prev_row, pltpu.roll(xm, 1, 0))
    x_right = jnp.where(rid == tm - 1, next_row, pltpu.roll(xm, tm - 1, 0))
    cw = cw_ref[...]
    y = cw[0:1] * x_left + cw[1:2] * xm + cw[2:3] * x_right + cb_ref[...]
    act = (y * _sigmoid(y)).astype(BF16)
    act_ref[...] = act
    q = _dot(act, wq_ref[...])
    kt = _dot_nt(wkt_ref[...], act)
    v = _dot(xm_bf, wv_ref[...])
    q_bf = q.astype(BF16)
    kt_bf = kt.astype(BF16)
    v_bf = v.astype(BF16)
    q_s = (q * (DH ** -0.5)).astype(BF16)
    for hd in range(HEADS):
        q_ref[hd] = q_s[:, hd * DH:(hd + 1) * DH]
        v_ref[hd] = v_bf[:, hd * DH:(hd + 1) * DH]
    kt_ref[...] = kt_bf
    g = _dot_nt(wiq_ref[...], q_bf) + _dot(wik_ref[...], kt_bf) + _dot_nt(wiv_ref[...], v_bf)
    g_ref[...] = g + bi_ref[...]


def _feat(xm, conv_w, conv_b, wq, wkt, wv, wiq, wik, wiv, bi, *, tm):
    t = xm.shape[0]
    nb16 = t // 16
    k16 = tm // 16
    full = lambda a: pl.BlockSpec(a.shape, lambda i: (0,) * a.ndim)
    tok = pl.BlockSpec((tm, D_MLSTM), lambda i: (i, 0))
    heads = pl.BlockSpec((HEADS, tm, DH), lambda i: (0, i, 0))
    return pl.pallas_call(
        _feat_kernel,
        grid=(t // tm,),
        in_specs=[tok,
                  pl.BlockSpec((16, D_MLSTM), lambda i: (jnp.maximum(i * k16 - 1, 0), 0)),
                  pl.BlockSpec((16, D_MLSTM), lambda i: (jnp.minimum((i + 1) * k16, nb16 - 1), 0)),
                  full(conv_w), full(conv_b), full(wq), full(wkt), full(wv),
                  full(wiq), full(wik), full(wiv), full(bi)],
        out_specs=[heads,
                   pl.BlockSpec((D_MLSTM, tm), lambda i: (0, i)),
                   heads, tok,
                   pl.BlockSpec((16, tm), lambda i: (0, i))],
        out_shape=[jax.ShapeDtypeStruct((HEADS, t, DH), BF16),
                   jax.ShapeDtypeStruct((D_MLSTM, t), BF16),
                   jax.ShapeDtypeStruct((HEADS, t, DH), BF16),
                   jax.ShapeDtypeStruct((t, D_MLSTM), BF16),
                   jax.ShapeDtypeStruct((16, t), F32)],
        compiler_params=_cparams("parallel"),
        name="feat",
    )(xm, xm, xm, conv_w, conv_b, wq, wkt, wv, wiq, wik, wiv, bi)


def _gates_kernel(g_ref, d_ref, gc_ref):
    g = g_ref[...]
    tl = g.shape[1]
    ig = g[0:8]
    fg = g[8:16]
    lf = jnp.minimum(fg, 0.0) - jnp.log(1.0 + jnp.exp(-jnp.abs(fg)))
    pos = lax.broadcasted_iota(jnp.int32, (8, tl), 1) & (CHUNK - 1)
    is_fwd = lax.broadcasted_iota(jnp.int32, (8, tl), 0) < HEADS

    def scan(x, op, ident):
        xf = x
        xb = x
        k = 1
        while k < CHUNK:
            xf = op(xf, jnp.where(pos >= k, pltpu.roll(xf, k, 1), ident))
            xb = op(xb, jnp.where(pos < CHUNK - k, pltpu.roll(xb, tl - k, 1), ident))
            k *= 2
        return jnp.where(is_fwd, xf, xb)

    b = scan(lf, jnp.add, 0.0)
    d = ig - b
    mloc = b + scan(d, jnp.maximum, NEG_BIG)
    d_ref[...] = d
    stack = jnp.concatenate([b, mloc, jnp.zeros((LANES - 16, tl), F32)], axis=0)
    gc_ref[...] = stack.T


def _gates(g, *, tl):
    t = g.shape[1]
    return pl.pallas_call(
        _gates_kernel,
        grid=(t // tl,),
        in_specs=[pl.BlockSpec((16, tl), lambda i: (0, i))],
        out_specs=[pl.BlockSpec((8, tl), lambda i: (0, i)),
                   pl.BlockSpec((tl, LANES), lambda i: (i, 0))],
        out_shape=[jax.ShapeDtypeStruct((8, t), F32),
                   jax.ShapeDtypeStruct((t, LANES), F32)],
        compiler_params=_cparams("parallel"),
        name="gates",
    )(g)


def _mlstm_kernel(*refs, cps, emit, ncast):
    (qf_ref, kf_ref, vf_ref, gcf_ref, drf_ref,
     qb_ref, kb_ref, vb_ref, gcb_ref, drb_ref, c0_ref, m0_ref) = refs[:12]
    cast_in = refs[12:12 + ncast]
    outs = refs[12 + ncast:]
    if emit:
        hf_ref, hb_ref, cfin_ref, mfin_ref = outs[:4]
        outs = outs[4:]
    else:
        cfin_ref, mfin_ref = outs[:2]
        outs = outs[2:]
        hf_ref = hb_ref = None
    cast_out = outs[:ncast]
    c_scr, m_scr = outs[ncast:]
    step = pl.program_id(0)

    for src, dst in zip(cast_in, cast_out):
        dst[...] = src[...].astype(BF16)

    @pl.when(step == 0)
    def _():
        c_scr[...] = c0_ref[...]
        m_scr[...] = m0_ref[...]

    ti = lax.broadcasted_iota(jnp.int32, (CHUNK, CHUNK), 0)
    si = lax.broadcasted_iota(jnp.int32, (CHUNK, CHUNK), 1)
    ones_blk = jnp.ones((HEADS, CHUNK, CHUNK), BF16)

    def bdot(a, b):
        return lax.dot_general(a, b, (((2,), (1,)), ((0,), (0,))), preferred_element_type=F32)

    for j in range(cps):
        for d in range(2):
            if d == 0:
                q_ref, k_ref, v_ref, gc_ref, dr_ref, h_ref = qf_ref, kf_ref, vf_ref, gcf_ref, drf_ref, hf_ref
                r0, mask, last = j * CHUNK, si <= ti, CHUNK - 1
            else:
                q_ref, k_ref, v_ref, gc_ref, dr_ref, h_ref = qb_ref, kb_ref, vb_ref, gcb_ref, drb_ref, hb_ref
                r0, mask, last = (cps - 1 - j) * CHUNK, si >= ti, 0
            rows = slice(r0, r0 + CHUNK)
            gc = gc_ref[rows, :]
            dr = dr_ref[:, rows]
            hs = range(d * HEADS, (d + 1) * HEADS)
            b = jnp.stack([gc[:, r:r + 1] for r in hs])
            mloc = jnp.stack([gc[:, 8 + r:9 + r] for r in hs])
            drow = jnp.stack([dr[r:r + 1, :] for r in hs])
            btot = jnp.stack([gc[last:last + 1, r:r + 1] for r in hs])
            amax = jnp.stack([gc[last:last + 1, 8 + r:9 + r] for r in hs])
            m0 = jnp.stack([m_scr[r][0:1, 0:1] for r in hs])
            c_aug = c_scr[d * HEADS:(d + 1) * HEADS]
            kt = k_ref[:, :, rows]
            vaug = jnp.concatenate([v_ref[:, rows, :], ones_blk], axis=2)
            if emit:
                qh = q_ref[:, rows, :]
                bm = b + m0
                m_t = jnp.maximum(bm, mloc)
                dmat = jnp.where(mask[None], jnp.exp((b - m_t) + drow), 0.0)
                smat = (bdot(qh, kt) * dmat).astype(BF16)
                q_in = (qh.astype(F32) * jnp.exp(bm - m_t)).astype(BF16)
                num = bdot(jnp.concatenate([smat, q_in], axis=2),
                           jnp.concatenate([vaug, c_aug.astype(BF16)], axis=1))
                den = jnp.maximum(jnp.abs(num[:, :, DH:]), jnp.exp(-m_t))
                h_ref[:, rows, :] = (num[:, :, :DH] / den).astype(h_ref.dtype)
            m_new = jnp.maximum(btot + m0, amax)
            decay = jnp.exp(btot + m0 - m_new)
            kw = (kt.astype(F32) * jnp.exp(btot + drow - m_new)).astype(BF16)
            c_scr[d * HEADS:(d + 1) * HEADS] = decay * c_aug + bdot(kw, vaug)
            m_scr[d * HEADS:(d + 1) * HEADS] = jnp.broadcast_to(m_new, (HEADS, 8, LANES))

    @pl.when(step == pl.num_programs(0) - 1)
    def _():
        cfin_ref[...] = c_scr[...]
        mfin_ref[...] = m_scr[...]


def _mlstm(q, kt, v, gc, dr, c0, m0, *, cps, emit, casts=()):
    t = q.shape[1]
    cb = cps * CHUNK
    nb = t // cb
    fwd_r = lambda i: (i, 0)
    bwd_r = lambda i: (nb - 1 - i, 0)
    fwd_c = lambda i: (0, i)
    bwd_c = lambda i: (0, nb - 1 - i)
    tok = lambda f: pl.BlockSpec((HEADS, cb, DH), lambda i, f=f: (0, f(i)[0], 0))
    in_specs = []
    for fr, fc in ((fwd_r, fwd_c), (bwd_r, bwd_c)):
        in_specs += [tok(fr), pl.BlockSpec((HEADS, DH, cb), lambda i, fc=fc: (0, 0, fc(i)[1])), tok(fr),
                     pl.BlockSpec((cb, LANES), fr), pl.BlockSpec((8, cb), fc)]
    cshape = (2 * HEADS, DH, 2 * DH)
    mshape = (2 * HEADS, 8, LANES)
    cspec = pl.BlockSpec(cshape, lambda i: (0, 0, 0))
    mspec = pl.BlockSpec(mshape, lambda i: (0, 0, 0))
    in_specs += [cspec, mspec]
    out_specs = [cspec, mspec]
    out_shape = [jax.ShapeDtypeStruct(cshape, F32), jax.ShapeDtypeStruct(mshape, F32)]
    if emit:
        out_specs = [tok(fwd_r), tok(bwd_r)] + out_specs
        out_shape = [jax.ShapeDtypeStruct((HEADS, t, DH), BF16)] * 2 + out_shape
    for a in casts:
        per = nb // a.shape[0]
        spec = pl.BlockSpec((1, a.shape[1] // per, a.shape[2]), lambda i, per=per: (i // per, i % per, 0))
        in_specs.append(spec)
        out_specs.append(spec)
        out_shape.append(jax.ShapeDtypeStruct(a.shape, BF16))
    return pl.pallas_call(
        functools.partial(_mlstm_kernel, cps=cps, emit=emit, ncast=len(casts)),
        grid=(nb,),
        in_specs=in_specs,
        out_specs=out_specs,
        out_shape=out_shape,
        scratch_shapes=[pltpu.VMEM(cshape, F32), pltpu.VMEM(mshape, F32)],
        compiler_params=_cparams("arbitrary"),
        name="mlstm",
    )(q, kt, v, gc, dr, q, kt, v, gc, dr, c0, m0, *casts)


FFT_ROWS = 8


def _dft1_kernel(u_ref, f_ref, yc_ref, ys_ref):
    f = f_ref[...]
    rows = CHUNK * FFT_ROWS
    u2 = u_ref.reshape(FGROUPS * rows, FCG)
    yc2 = yc_ref.reshape(FGROUPS * rows, FCG)
    ys2 = ys_ref.reshape(FGROUPS * rows, FCG)
    for s in range(FFT_ROWS):
        pick = [pl.ds(g * rows + s, CHUNK, stride=FFT_ROWS) for g in range(FGROUPS)]
        x = jnp.concatenate([u2[p, :] for p in pick], axis=1).astype(BF16)
        y = _dot(f, x)
        for g in range(FGROUPS):
            yc2[pick[g], :] = y[:CHUNK, g * FCG:(g + 1) * FCG]
            ys2[pick[g], :] = y[CHUNK:, g * FCG:(g + 1) * FCG]


def _dft1(u4, f1):
    blk = pl.BlockSpec((FGROUPS, CHUNK, FFT_ROWS, FCG), lambda j: (0, 0, j, 0))
    out = jax.ShapeDtypeStruct(u4.shape, F32)
    return pl.pallas_call(
        _dft1_kernel,
        grid=(u4.shape[2] // FFT_ROWS,),
        in_specs=[blk, pl.BlockSpec(f1.shape, lambda j: (0, 0))],
        out_specs=[blk, blk],
        out_shape=[out, out],
        compiler_params=_cparams("parallel"),
        name="dft1",
    )(u4, f1)


def _dft2_kernel(yc_ref, ys_ref, cw_ref, sw_ref, a2_ref, mix_ref, o_ref):
    a2 = a2_ref[...]
    rows = CHUNK * FFT_ROWS
    o2 = o_ref.reshape(FGROUPS * rows, FCG)
    for kk in range(FFT_ROWS):
        blk = slice(kk * CHUNK, (kk + 1) * CHUNK)
        yc = jnp.concatenate([yc_ref[g, blk, :] for g in range(FGROUPS)], axis=1)
        ys = jnp.concatenate([ys_ref[g, blk, :] for g in range(FGROUPS)], axis=1)
        cw = cw_ref[kk]
        sw = sw_ref[kk]
        p = jnp.concatenate([yc * cw - ys * sw, yc * sw + ys * cw], axis=0).astype(BF16)
        x = _dot(a2, p).astype(BF16)
        for g in range(FGROUPS):
            cols = slice(g * FCG, (g + 1) * FCG)
            cat = jnp.concatenate([x[:CHUNK, cols], x[CHUNK:, cols]], axis=1)
            o2[pl.ds(g * rows + kk, CHUNK, stride=FFT_ROWS), :] = _dot(cat, mix_ref[g])


def _dft2(yc, ys, cw3, sw3, a2, mix):
    t = yc.shape[1]
    n1 = t // CHUNK
    tok = pl.BlockSpec((FGROUPS, FFT_ROWS * CHUNK, FCG), lambda i: (0, i, 0))
    tw = pl.BlockSpec((FFT_ROWS, CHUNK, 1), lambda i: (i, 0, 0))
    full = lambda a: pl.BlockSpec(a.shape, lambda i: (0,) * a.ndim)
    return pl.pallas_call(
        _dft2_kernel,
        grid=(n1 // FFT_ROWS,),
        in_specs=[tok, tok, tw, tw, full(a2), full(mix)],
        out_specs=pl.BlockSpec((FGROUPS, CHUNK, FFT_ROWS, FCG), lambda i: (0, 0, i, 0)),
        out_shape=jax.ShapeDtypeStruct((FGROUPS, CHUNK, n1, FCG), F32),
        compiler_params=_cparams("parallel"),
        name="dft2",
    )(yc, ys, cw3, sw3, a2, mix)


def _merge_kernel(hf_ref, hb_ref, act_ref, z_ref, yf_ref, x_ref, er_ref, ec_ref,
                  nw_ref, sk_ref, wout_ref, gg1_ref, g2_ref, sh2_ref, wr_ref, br_ref,
                  x1_ref, h2_ref, lg_ref):
    parts = []
    for hd in range(HEADS):
        hh = hf_ref[hd].astype(F32) + hb_ref[hd].astype(F32)
        dl = hh - jnp.mean(hh, axis=-1, keepdims=True)
        var = jnp.mean(dl * dl, axis=-1, keepdims=True)
        parts.append(dl * lax.rsqrt(var + EPS))
    hn = jnp.concatenate(parts, axis=-1)
    z = z_ref[...].astype(F32)
    m = (hn * nw_ref[...] + sk_ref[...] * act_ref[...].astype(F32)) * (z * _sigmoid(z))
    cat = jnp.concatenate([m.astype(BF16)] + [yf_ref[g].astype(BF16) for g in range(FGROUPS)], axis=-1)
    y = _dot(cat, wout_ref[...])
    x3 = _add_pos(x_ref[...], er_ref, ec_ref)
    xp = x3.reshape(x3.shape[0] * GRID_W, D_MODEL)
    x1 = xp + _rms(y) * gg1_ref[...]
    x1_ref[...] = x1
    h2 = _rms(x1) * g2_ref[...] + sh2_ref[...]
    h2_ref[...] = h2.astype(BF16)
    lg = _dot(h2.astype(BF16), wr_ref[...]) + br_ref[...]
    lg_ref[...] = lg.T[:32]


def _merge(hf, hb, act, z, yf, x3, er3, ec3, nw, sk, wout, gg1, g2, sh2, wr, br, *, rows):
    nr = x3.shape[0]
    t = nr * GRID_W
    tm = rows * GRID_W
    tok = pl.BlockSpec((tm, D_MLSTM), lambda i: (i, 0))
    heads = pl.BlockSpec((HEADS, tm, DH), lambda i: (0, i, 0))
    full = lambda a: pl.BlockSpec(a.shape, lambda i: (0,) * a.ndim)
    return pl.pallas_call(
        _merge_kernel,
        grid=(nr // rows,),
        in_specs=[heads, heads, tok, tok, heads,
                  pl.BlockSpec((rows, GRID_W, D_MODEL), lambda i: (i, 0, 0)),
                  pl.BlockSpec((rows, 1, D_MODEL // 2), lambda i: (i, 0, 0)),
                  pl.BlockSpec((1, GRID_W, D_MODEL // 2), lambda i: (0, 0, 0)),
                  full(nw), full(sk), full(wout), full(gg1), full(g2), full(sh2), full(wr), full(br)],
        out_specs=[pl.BlockSpec((tm, D_MODEL), lambda i: (i, 0)),
                   pl.BlockSpec((tm, D_MODEL), lambda i: (i, 0)),
                   pl.BlockSpec((32, tm), lambda i: (0, i))],
        out_shape=[jax.ShapeDtypeStruct((t, D_MODEL), F32),
                   jax.ShapeDtypeStruct((t, D_MODEL), BF16),
                   jax.ShapeDtypeStruct((32, t), F32)],
        compiler_params=_cparams("parallel"),
        name="merge",
    )(hf, hb, act, z, yf, x3, er3, ec3, nw, sk, wout, gg1, g2, sh2, wr, br)


def _route_kernel(lg_ref, pos_ref, w_ref, cnt_ref, *, sbk):
    lg = lg_ref[...]
    tl = lg.shape[1]
    g = [lg[j:j + 1] for j in range(N_GROUPS)]
    e = [lg[N_GROUPS + j:N_GROUPS + j + 1] for j in range(N_EXPERTS)]
    gmax = jnp.maximum(jnp.maximum(g[0], g[1]), jnp.maximum(g[2], g[3]))
    den = jnp.exp(g[0] - gmax) + jnp.exp(g[1] - gmax) + jnp.exp(g[2] - gmax) + jnp.exp(g[3] - gmax)
    p_sel = 1.0 / den
    sel = []
    free = jnp.ones((1, tl), F32)
    for j in range(N_GROUPS):
        s = jnp.where(g[j] >= gmax, free, 0.0)
        sel.append(s)
        free = free - s
    es = []
    for j in range(EPG):
        es.append(sel[0] * e[j] + sel[1] * e[EPG + j] + sel[2] * e[2 * EPG + j] + sel[3] * e[3 * EPG + j])
    rank = []
    for j in range(EPG):
        rj = jnp.zeros((1, tl), F32)
        for i in range(EPG):
            if i == j:
                continue
            beats = (es[i] >= es[j]) if i < j else (es[i] > es[j])
            rj = rj + jnp.where(beats, 1.0, 0.0)
        rank.append(rj)
    v1 = jnp.maximum(jnp.maximum(es[0], es[1]), jnp.maximum(es[2], es[3]))
    v2 = sum(jnp.where(rank[j] == 1.0, es[j], 0.0) for j in range(EPG))
    tt = jnp.exp(v2 - v1)
    w1 = p_sel / (1.0 + tt)
    w2 = w1 * tt
    w = [jnp.where(rank[j] == 0.0, w1, jnp.where(rank[j] == 1.0, w2, 0.0)) for j in range(EPG)]
    top2 = [jnp.where(rank[j] < 2.0, 1.0, 0.0) for j in range(EPG)]
    mem = jnp.concatenate([sel[gi] * top2[j] for gi in range(N_GROUPS) for j in range(EPG)], axis=0)
    wts = jnp.concatenate([sel[gi] * w[j] for gi in range(N_GROUPS) for j in range(EPG)], axis=0)
    w_ref[...] = wts
    lane = lax.broadcasted_iota(jnp.int32, (N_EXPERTS, tl), 1) & (sbk - 1)
    c = mem
    k = 1
    while k < sbk:
        c = c + jnp.where(lane >= k, pltpu.roll(c, k, 1), 0.0)
        k *= 2
    pos_ref[...] = jnp.where(mem > 0.0, c - 1.0, -1.0)
    lane128 = lax.broadcasted_iota(jnp.int32, (N_EXPERTS, LANES), 1)
    cnt = jnp.zeros((N_EXPERTS, LANES), F32)
    for kb in range(tl // sbk):
        tot = jnp.sum(mem[:, kb * sbk:(kb + 1) * sbk], axis=1, keepdims=True)
        cnt = cnt + jnp.where(lane128 == kb, tot, 0.0)
    cnt_ref[...] = cnt


def _route(lg, *, tl, sbk):
    t = lg.shape[1]
    row = pl.BlockSpec((N_EXPERTS, tl), lambda i: (0, i))
    return pl.pallas_call(
        functools.partial(_route_kernel, sbk=sbk),
        grid=(t // tl,),
        in_specs=[pl.BlockSpec((32, tl), lambda i: (0, i))],
        out_specs=[row, row, pl.BlockSpec((N_EXPERTS, LANES), lambda i: (i, 0))],
        out_shape=[jax.ShapeDtypeStruct((N_EXPERTS, t), F32),
                   jax.ShapeDtypeStruct((N_EXPERTS, t), F32),
                   jax.ShapeDtypeStruct((t // tl * N_EXPERTS, LANES), F32)],
        compiler_params=_cparams("parallel"),
        name="route",
    )(lg)


def _slots_kernel(tab_ref, pos_ref, w_ref, qrow_ref, qcol_ref, *, off_r, off_o):
    i = pl.program_id(0)
    tl = pos_ref.shape[1]
    q0 = jnp.full((1, tl), -1.0, F32)
    q1 = jnp.full((1, tl), -1.0, F32)
    w0 = jnp.zeros((1, tl), F32)
    w1 = jnp.zeros((1, tl), F32)
    seen = jnp.zeros((1, tl), F32)
    for ex in range(N_EXPERTS):
        rk = pos_ref[ex:ex + 1, :]
        wt = w_ref[ex:ex + 1, :]
        rows = tab_ref[off_r + i * N_EXPERTS + ex].astype(F32)
        first_row = tab_ref[off_o + i * N_EXPERTS + ex].astype(F32)
        m = jnp.where(rk >= 0.0, 1.0, 0.0)
        val = jnp.where(rk < rows, rk + first_row, -1.0)
        first = (m * (1.0 - seen)) > 0.0
        second = (m * seen) > 0.0
        q0 = jnp.where(first, val, q0)
        w0 = jnp.where(first, wt, w0)
        q1 = jnp.where(second, val, q1)
        w1 = jnp.where(second, wt, w1)
        seen = seen + m
    qrow_ref[...] = jnp.concatenate([q0, q1, jnp.zeros((6, tl), F32)], axis=0)
    qcol_ref[...] = jnp.concatenate([q0, q1, w0, w1, jnp.zeros((LANES - 4, tl), F32)], axis=0).T


def _slots(tab, pos, w, *, tl, off_r, off_o):
    t = pos.shape[1]
    row = pl.BlockSpec((N_EXPERTS, tl), lambda i, c: (0, i))
    return pl.pallas_call(
        functools.partial(_slots_kernel, off_r=off_r, off_o=off_o),
        grid_spec=pltpu.PrefetchScalarGridSpec(
            num_scalar_prefetch=1,
            grid=(t // tl,),
            in_specs=[row, row],
            out_specs=[pl.BlockSpec((8, tl), lambda i, c: (0, i)),
                       pl.BlockSpec((tl, LANES), lambda i, c: (i, 0))]),
        out_shape=[jax.ShapeDtypeStruct((8, t), F32),
                   jax.ShapeDtypeStruct((t, LANES), F32)],
        compiler_params=_cparams("parallel"),
        name="slots",
    )(tab, pos, w)


def _mlp(x, wg, wu, wd):
    gt = _dot(x, wg)
    a = ((gt * _sigmoid(gt)) * _dot(x, wu)).astype(BF16)
    return _dot(a, wd).astype(BF16)


def _moe_kernel(tab_ref, h_ref, qrow_ref, pos_ref, w_ref, wg_ref, wu_ref, wd_ref,
                qcol_ref, x1_ref, gg2_ref, o_ref, xs_ref, ovf_ref, *, sbk, slots, csb, off_r, off_o, off_f):
    i = pl.program_id(0)
    step = pl.program_id(1)
    nsb = h_ref.shape[0] // sbk
    e = jnp.minimum(step, N_EXPERTS - 1)
    n_rows = tab_ref[off_r + i * N_EXPERTS + e]
    row_a = tab_ref[off_o + i * N_EXPERTS + e]
    over = tab_ref[off_f + i * N_EXPERTS + e]

    @pl.when(step == 0)
    def _():
        def zero(sb, carry):
            ovf_ref[pl.ds(pl.multiple_of(sb * sbk, sbk), sbk), :] = jnp.zeros((sbk, D_MODEL), F32)
            return carry

        lax.fori_loop(0, nsb, zero, 0)
        pid = lax.broadcasted_iota(jnp.int32, (slots, sbk), 0).astype(F32)

        def select(sb, carry):
            q0 = qrow_ref[0, pl.ds(sb, 1), :]
            q1 = qrow_ref[1, pl.ds(sb, 1), :]
            s = jnp.where(q0 == pid, 1.0, jnp.where(q1 == pid, 1.0, 0.0)).astype(BF16)
            row0 = pl.multiple_of(sb * sbk, sbk)
            xs_ref[sb] = _dot(s, h_ref[pl.ds(row0, sbk), :]).astype(BF16)
            return carry

        lax.fori_loop(0, nsb, select, 0)

    def run_rows(first, n):
        first = pl.multiple_of(first, 16)
        x = jnp.concatenate([xs_ref[sb, pl.ds(first, n), :] for sb in range(nsb)], axis=0)
        y = _mlp(x, wg_ref[0], wu_ref[0], wd_ref[0])
        for sb in range(nsb):
            xs_ref[sb, pl.ds(first, n), :] = y[sb * n:(sb + 1) * n]

    @pl.when(step < N_EXPERTS)
    def _():
        n_pairs = n_rows // 32

        def pair(t, carry):
            run_rows(row_a + t * 32, 32)
            return carry

        lax.fori_loop(0, n_pairs, pair, 0)

        @pl.when(n_rows % 32 != 0)
        def _():
            run_rows(row_a + n_pairs * 32, 16)

    @pl.when(jnp.logical_and(step < N_EXPERTS, over > 0))
    def _():
        rid = lax.broadcasted_iota(jnp.int32, (LANES, sbk), 0).astype(F32)
        wg, wu, wd = wg_ref[0], wu_ref[0], wd_ref[0]

        def sub_block(sb, c1):
            n_over = jnp.maximum(tab_ref[(i * nsb + sb) * N_EXPERTS + e] - n_rows, 0)
            row0 = pl.multiple_of(sb * sbk, sbk)

            def one_pass(s, c2):
                base = (n_rows + s * LANES).astype(F32)
                hit = pos_ref[pl.ds(e, 1), pl.ds(sb, 1), :].reshape(1, sbk) == (rid + base)
                sel = jnp.where(hit, 1.0, 0.0).astype(BF16)
                yo = _mlp(_dot(sel, h_ref[pl.ds(row0, sbk), :]).astype(BF16), wg, wu, wd)
                wct = jnp.where(hit, w_ref[pl.ds(e, 1), pl.ds(sb, 1), :].reshape(1, sbk), 0.0).T
                ovf_ref[pl.ds(row0, sbk), :] += _dot(wct.astype(BF16), yo)
                return c2

            return lax.fori_loop(0, (n_over + LANES - 1) // LANES, one_pass, c1)

        lax.fori_loop(0, nsb, sub_block, 0)

    @pl.when(step >= N_EXPERTS)
    def _():
        lane = lax.broadcasted_iota(jnp.int32, (sbk, slots), 1).astype(F32)
        for k in range(csb):
            sb = (step - N_EXPERTS) * csb + k
            rows = slice(k * sbk, (k + 1) * sbk)
            qc = qcol_ref[rows, :]
            wmat = (jnp.where(lane == qc[:, 0:1], qc[:, 2:3], 0.0)
                    + jnp.where(lane == qc[:, 1:2], qc[:, 3:4], 0.0)).astype(BF16)
            y = _dot(wmat, xs_ref[sb]) + ovf_ref[pl.ds(pl.multiple_of(sb * sbk, sbk), sbk), :]
            o_ref[rows, :] = x1_ref[rows, :] + _rms(y) * gg2_ref[...]


def _moe(tab, h2, qrow3, pos3, w3, wg, wu, wd, qcol, x1, gg2, *, tb, sbk, slots, csb, off_r, off_o, off_f):
    t = h2.shape[0]
    nsb = tb // sbk
    comb_steps = nsb // csb
    wblk = lambda i, s, c: (jnp.minimum(s, N_EXPERTS - 1), 0, 0)
    oblk = lambda i, s, c: (i * comb_steps + jnp.maximum(s - N_EXPERTS, 0), 0)
    r3 = pl.BlockSpec((N_EXPERTS, nsb, sbk), lambda i, s, c: (0, i, 0))
    return pl.pallas_call(
        functools.partial(_moe_kernel, sbk=sbk, slots=slots, csb=csb,
                          off_r=off_r, off_o=off_o, off_f=off_f),
        grid_spec=pltpu.PrefetchScalarGridSpec(
            num_scalar_prefetch=1,
            grid=(t // tb, N_EXPERTS + comb_steps),
            in_specs=[pl.BlockSpec((tb, D_MODEL), lambda i, s, c: (i, 0)),
                      pl.BlockSpec((8, nsb, sbk), lambda i, s, c: (0, i, 0)),
                      r3, r3,
                      pl.BlockSpec((1, D_MODEL, D_EXPERT), wblk),
                      pl.BlockSpec((1, D_MODEL, D_EXPERT), wblk),
                      pl.BlockSpec((1, D_EXPERT, D_MODEL), wblk),
                      pl.BlockSpec((csb * sbk, LANES), oblk),
                      pl.BlockSpec((csb * sbk, D_MODEL), oblk),
                      pl.BlockSpec((1, D_MODEL), lambda i, s, c: (0, 0))],
            out_specs=pl.BlockSpec((csb * sbk, D_MODEL), oblk),
            scratch_shapes=[pltpu.VMEM((nsb, slots, D_MODEL), BF16),
                            pltpu.VMEM((tb, D_MODEL), F32)]),
        out_shape=jax.ShapeDtypeStruct((t, D_MODEL), F32),
        compiler_params=pltpu.CompilerParams(dimension_semantics=("parallel", "arbitrary"),
                                             vmem_limit_bytes=MOE_VMEM_LIMIT),
        name="moe",
    )(tab, h2, qrow3, pos3, w3, wg, wu, wd, qcol, x1, gg2)


def _slot_tables(cnt, *, slots):
    need = ((jnp.max(cnt, axis=1) + 15) // 16) * 16
    start = jnp.cumsum(need, axis=1) - need
    rows = jnp.clip(slots - start, 0, need)
    over = jnp.max(jnp.maximum(cnt - rows[:, None, :], 0), axis=1)
    return rows, start, over


def _pos_tables(rows):
    quarter = D_MODEL // 4
    freq = 1.0 / (POS_BASE ** (np.arange(quarter, dtype=np.float64) / quarter))
    r = np.arange(rows, dtype=np.float64)[:, None] * freq
    cl = np.arange(GRID_W, dtype=np.float64)[:, None] * freq
    er = np.concatenate([np.sin(r), np.cos(r)], axis=-1).astype(np.float32)
    ec = np.concatenate([np.sin(cl), np.cos(cl)], axis=-1).astype(np.float32)
    return jnp.asarray(er[:, None, :]), jnp.asarray(ec[None, :, :])


def _dft_tables(t):
    n = np.arange(CHUNK, dtype=np.int64)
    prod = n[:, None] * n[None, :]
    ang = (prod % CHUNK).astype(np.float64) * (2.0 * np.pi / CHUNK)
    c, s = np.cos(ang), np.sin(ang)
    f1 = np.concatenate([c, s], axis=0)
    a2 = np.concatenate([np.concatenate([c, -s], axis=1), np.concatenate([s, c], axis=1)], axis=0)
    cs = np.concatenate([c, -s], axis=0)
    angw = prod.astype(np.float64) * (2.0 * np.pi / t)
    f32 = lambda a: jnp.asarray(a.astype(np.float32))
    return f32(f1), f32(a2), f32(cs), f32(np.cos(angw)[:, :, None]), f32(np.sin(angw)[:, :, None])


def _blockdiag(w):
    n = w.shape[0]
    size = n * QKV_BLOCK
    spread = np.tile(np.eye(QKV_BLOCK, dtype=np.float32), (1, n))
    rows = jnp.dot(w.reshape(size, QKV_BLOCK), jnp.asarray(spread), precision=lax.Precision.HIGHEST)
    blk = np.arange(size) // QKV_BLOCK
    mask = (blk[:, None] == blk[None, :]).astype(np.float32)
    return rows * jnp.asarray(mask)


def _gate_weights(w_f, b_f, w_b, b_b):
    w = jnp.concatenate([w_f[:, :HEADS], w_b[:, :HEADS], w_f[:, HEADS:], w_b[:, HEADS:]], axis=1).T
    b = jnp.concatenate([b_f[:HEADS], b_b[:HEADS], b_f[HEADS:], b_b[HEADS:]])
    return w.astype(BF16), b[:, None]


def kernel(x, c, ctx, c_ctx, w_ada, b_ada, g_pre_mix, g_post_mix, g_pre_ffn, g_post_ffn,
           w_in, conv_w, conv_b, w_q, w_k, w_v, w_if_fwd, b_if_fwd, w_if_bwd, b_if_bwd,
           mlstm_norm_w, mlstm_skip, w_fourier, w_out, w_router_group, b_router_group,
           w_router_expert, b_router_expert, w_gate, w_up, w_down):
    t = x.shape[1]
    rows = t // GRID_W

    c8 = jnp.concatenate([c, c_ctx[None, :], jnp.zeros((6, D_MODEL), F32)], axis=0)
    mod = _ada(c8, w_ada[0], b_ada[0][None, :])
    shift1, scale1, gate1, shift2, scale2, gate2 = [mod[0:1, k * D_MODEL:(k + 1) * D_MODEL] for k in range(N_MOD)]
    shift1c, scale1c = mod[1:2, 0:D_MODEL], mod[1:2, D_MODEL:2 * D_MODEL]
    g1 = g_pre_mix[0][None, :] * (1.0 + scale1)
    g1c = g_pre_mix[0][None, :] * (1.0 + scale1c)
    gg1 = g_post_mix[0][None, :] * gate1
    g2 = g_pre_ffn[0][None, :] * (1.0 + scale2)
    gg2 = g_post_ffn[0][None, :] * gate2

    er3, ec3 = _pos_tables(rows)
    x3 = x.reshape(rows, GRID_W, D_MODEL)
    ctx3 = ctx.reshape(CTX_LEN // GRID_W, GRID_W, D_MODEL)
    w_in_bf = w_in[0].astype(BF16)

    xm_l, z_l, u_l = _inproj(x3, er3, ec3, g1, shift1, w_in_bf, rows=8, add_pos=True)
    xm_c, _, _ = _inproj(ctx3, er3, ec3, g1c, shift1c, w_in_bf, rows=CTX_LEN // GRID_W, add_pos=False)

    wq = _blockdiag(w_q[0]).astype(BF16)
    wkt = _blockdiag(w_k[0]).T.astype(BF16)
    wv = _blockdiag(w_v[0]).astype(BF16)
    wi, bi = _gate_weights(w_if_fwd[0], b_if_fwd[0], w_if_bwd[0], b_if_bwd[0])
    wiq, wik, wiv = wi[:, :D_MLSTM], wi[:, D_MLSTM:2 * D_MLSTM], wi[:, 2 * D_MLSTM:]
    cb = conv_b[0][None, :]
    q_l, kt_l, v_l, act_l, gp_l = _feat(xm_l, conv_w[0], cb, wq, wkt, wv, wiq, wik, wiv, bi, tm=512)
    q_c, kt_c, v_c, _, gp_c = _feat(xm_c, conv_w[0], cb, wq, wkt, wv, wiq, wik, wiv, bi, tm=CTX_LEN)

    dr_l, gc_l = _gates(gp_l, tl=2048)
    dr_c, gc_c = _gates(gp_c, tl=CTX_LEN)

    c0 = jnp.zeros((2 * HEADS, DH, 2 * DH), F32)
    m0 = jnp.zeros((2 * HEADS, 8, LANES), F32)
    kt_c, kt_l = kt_c.reshape(HEADS, DH, CTX_LEN), kt_l.reshape(HEADS, DH, t)
    c_ctx_fin, m_ctx_fin = _mlstm(q_c, kt_c, v_c, gc_c, dr_c, c0, m0, cps=CTX_LEN // CHUNK, emit=False)
    hf, hb, _, _, wg_bf, wu_bf, wd_bf = _mlstm(q_l, kt_l, v_l, gc_l, dr_l, c_ctx_fin, m_ctx_fin, cps=4, emit=True,
                                               casts=(w_gate[0], w_up[0], w_down[0]))

    f1, a2, cs, cw3, sw3 = _dft_tables(t)
    n1 = t // CHUNK
    yc, ys = _dft1(u_l.reshape(FGROUPS, n1, CHUNK, FCG), f1.astype(BF16))
    mix = jnp.einsum('kc,gcd->gkd', cs, w_fourier[0], precision=lax.Precision.HIGHEST)
    mix = (mix * float(1.0 / np.sqrt(float(t) * FCG))).astype(BF16)
    yf = _dft2(yc.reshape(FGROUPS, t, FCG), ys.reshape(FGROUPS, t, FCG), cw3, sw3, a2.astype(BF16), mix)
    yf = yf.reshape(FGROUPS, t, FCG)

    wr = jnp.concatenate([w_router_group[0], w_router_expert[0],
                          jnp.zeros((D_MODEL, LANES - N_GROUPS - N_EXPERTS), F32)], axis=1)
    br = jnp.concatenate([b_router_group[0], b_router_expert[0],
                          jnp.zeros((LANES - N_GROUPS - N_EXPERTS,), F32)])[None, :]
    x1, h2, lg = _merge(hf, hb, act_l, z_l, yf, x3, er3, ec3,
                        mlstm_norm_w[0][None, :], mlstm_skip[0][None, :], w_out[0].astype(BF16),
                        gg1, g2, shift2, wr.astype(BF16), br, rows=8)
    pos, wts, cnt = _route(lg, tl=MOE_TB, sbk=MOE_SBK)
    nblk, nsb = t // MOE_TB, MOE_TB // MOE_SBK
    cnt = cnt.reshape(nblk, N_EXPERTS, LANES)[:, :, :nsb]
    cnt = jnp.transpose(cnt, (0, 2, 1)).astype(jnp.int32)
    rows, start, over = _slot_tables(cnt, slots=MOE_SLOTS)
    tab = jnp.concatenate([a.reshape(-1) for a in (cnt, rows, start, over)])
    off_r = nblk * nsb * N_EXPERTS
    off_o = off_r + nblk * N_EXPERTS
    off_f = off_o + nblk * N_EXPERTS
    qrow, qcol = _slots(tab, pos, wts, tl=MOE_TB, off_r=off_r, off_o=off_o)
    sub = (t // MOE_SBK, MOE_SBK)
    out = _moe(tab, h2, qrow.reshape(8, *sub), pos.reshape(N_EXPERTS, *sub), wts.reshape(N_EXPERTS, *sub),
               wg_bf, wu_bf, wd_bf,
               qcol, x1, gg2, tb=MOE_TB, sbk=MOE_SBK, slots=MOE_SLOTS, csb=MOE_CSB,
               off_r=off_r, off_o=off_o, off_f=off_f)
    return out[None]
```

```python
import functools

import numpy as np
import jax
import jax.numpy as jnp
from jax import lax
from jax.experimental import pallas as pl
from jax.experimental.pallas import tpu as pltpu

F32 = jnp.float32
BF16 = jnp.bfloat16

D_MODEL = 1024
SEQ = 16384
GRID_W = 64
CTX_LEN = 256
D_MLSTM = 512
HEADS = 4
DH = 128
QKV_BLOCK = 4
CONV_K = 3
CHUNK = 128
D_FOURIER = 512
FGROUPS = 4
FCG = 128
N_GROUPS = 4
EPG = 4
N_EXPERTS = 16
D_EXPERT = 512
N_MOD = 6
EPS = 1e-6
POS_BASE = 10000.0
LANES = 128
NEG_BIG = -3.0e38

VMEM_LIMIT = 52 * 1024 * 1024
MOE_VMEM_LIMIT = 58 * 1024 * 1024
MOE_TB = 2048
MOE_SBK = 256
MOE_SLOTS = 1024
MOE_EPS = 2
MOE_CSB = 2


def _cparams(*sem):
    return pltpu.CompilerParams(dimension_semantics=sem, vmem_limit_bytes=VMEM_LIMIT)


def _dot(a, b):
    return jnp.dot(a, b, preferred_element_type=F32)


def _dot_nt(a, b):
    return lax.dot_general(a, b, (((1,), (1,)), ((), ())), preferred_element_type=F32)


def _split_bf16(a):
    hi = a.astype(BF16)
    lo = (a - hi.astype(F32)).astype(BF16)
    return hi, lo


def _dot3(a, b):
    a_hi, a_lo = _split_bf16(a)
    b_hi, b_lo = _split_bf16(b)
    return _dot(a_hi, b_hi) + (_dot(a_hi, b_lo) + _dot(a_lo, b_hi))


def _sigmoid(x):
    return 1.0 / (1.0 + jnp.exp(-x))


def _rms(x):
    return x * lax.rsqrt(jnp.mean(x * x, axis=-1, keepdims=True) + EPS)


def _ada_kernel(c_ref, w_ref, b_ref, o_ref):
    c = c_ref[...]
    s = c * _sigmoid(c)
    o_ref[...] = _dot3(s, w_ref[...]) + b_ref[...]


def _ada(c8, w, b):
    n = w.shape[1]
    tn = 768
    return pl.pallas_call(
        _ada_kernel,
        grid=(n // tn,),
        in_specs=[pl.BlockSpec((8, D_MODEL), lambda j: (0, 0)),
                  pl.BlockSpec((D_MODEL, tn), lambda j: (0, j)),
                  pl.BlockSpec((1, tn), lambda j: (0, j))],
        out_specs=pl.BlockSpec((8, tn), lambda j: (0, j)),
        out_shape=jax.ShapeDtypeStruct((8, n), F32),
        compiler_params=_cparams("parallel"),
        name="ada",
    )(c8, w, b)


def _add_pos(x3, er_ref, ec_ref):
    r = x3.shape[0]
    pr = jnp.broadcast_to(er_ref[...], (r, GRID_W, D_MODEL // 2))
    pc = jnp.broadcast_to(ec_ref[...], (r, GRID_W, D_MODEL // 2))
    return x3 + jnp.concatenate([pr, pc], axis=-1)


def _inproj_kernel(x_ref, er_ref, ec_ref, g_ref, sh_ref, w_ref, xm_ref, z_ref, u_ref, *, add_pos):
    x3 = x_ref[...]
    if add_pos:
        x3 = _add_pos(x3, er_ref, ec_ref)
    x = x3.reshape(x3.shape[0] * GRID_W, D_MODEL)
    h = _rms(x) * g_ref[...] + sh_ref[...]
    proj = _dot(h.astype(BF16), w_ref[...])
    xm_ref[...] = proj[:, :D_MLSTM].astype(BF16)
    z_ref[...] = proj[:, D_MLSTM:2 * D_MLSTM].astype(BF16)
    for g in range(FGROUPS):
        u_ref[g] = proj[:, 2 * D_MLSTM + g * FCG:2 * D_MLSTM + (g + 1) * FCG]


def _inproj(x3, er3, ec3, g_eff, shift, w_in, *, rows, add_pos):
    nr = x3.shape[0]
    t = nr * GRID_W
    tm = rows * GRID_W
    out = jax.ShapeDtypeStruct((t, D_MLSTM), BF16)
    ospec = pl.BlockSpec((tm, D_MLSTM), lambda i: (i, 0))
    vec = pl.BlockSpec((1, D_MODEL), lambda i: (0, 0))
    return pl.pallas_call(
        functools.partial(_inproj_kernel, add_pos=add_pos),
        grid=(nr // rows,),
        in_specs=[pl.BlockSpec((rows, GRID_W, D_MODEL), lambda i: (i, 0, 0)),
                  pl.BlockSpec((rows, 1, D_MODEL // 2), lambda i: (i, 0, 0)),
                  pl.BlockSpec((1, GRID_W, D_MODEL // 2), lambda i: (0, 0, 0)),
                  vec, vec,
                  pl.BlockSpec(w_in.shape, lambda i: (0, 0))],
        out_specs=[ospec, ospec, pl.BlockSpec((FGROUPS, tm, FCG), lambda i: (0, i, 0))],
        out_shape=[out, out, jax.ShapeDtypeStruct((FGROUPS, t, FCG), F32)],
        compiler_params=_cparams("parallel"),
        name="inproj",
    )(x3, er3, ec3, g_eff, shift, w_in)


def _feat_kernel(xm_ref, prev_ref, next_ref, cw_ref, cb_ref, wq_ref, wkt_ref, wv_ref,
                 wiq_ref, wik_ref, wiv_ref, bi_ref,
                 q_ref, kt_ref, v_ref, act_ref, g_ref):
    i = pl.program_id(0)
    n = pl.num_programs(0)
    xm_bf = xm_ref[...]
    xm = xm_bf.astype(F32)
    tm = xm.shape[0]
    prev_row = prev_ref[...].astype(F32)[15:16, :] * jnp.where(i > 0, 1.0, 0.0)
    next_row = next_ref[...].astype(F32)[0:1, :] * jnp.where(i < n - 1, 1.0, 0.0)
    rid = lax.broadcasted_iota(jnp.int32, (tm, 1), 0)
    x_left = jnp.where(rid == 0, prev_row, pltpu.roll(xm, 1, 0))
    x_right = jnp.where(rid == tm - 1, next_row, pltpu.roll(xm, tm - 1, 0))
    cw = cw_ref[...]
    y = cw[0:1] * x_left + cw[1:2] * xm + cw[2:3] * x_right + cb_ref[...]
    act = (y * _sigmoid(y)).astype(BF16)
    act_ref[...] = act
    q = _dot(act, wq_ref[...])
    kt = _dot_nt(wkt_ref[...], act)
    v = _dot(xm_bf, wv_ref[...])
    q_bf = q.astype(BF16)
    kt_bf = kt.astype(BF16)
    v_bf = v.astype(BF16)
    q_s = (q * (DH ** -0.5)).astype(BF16)
    for hd in range(HEADS):
        q_ref[hd] = q_s[:, hd * DH:(hd + 1) * DH]
        v_ref[hd] = v_bf[:, hd * DH:(hd + 1) * DH]
    kt_ref[...] = kt_bf
    g = _dot_nt(wiq_ref[...], q_bf) + _dot(wik_ref[...], kt_bf) + _dot_nt(wiv_ref[...], v_bf)
    g_ref[...] = g + bi_ref[...]


def _feat(xm, conv_w, conv_b, wq, wkt, wv, wiq, wik, wiv, bi, *, tm):
    t = xm.shape[0]
    nb16 = t // 16
    k16 = tm // 16
    full = lambda a: pl.BlockSpec(a.shape, lambda i: (0,) * a.ndim)
    tok = pl.BlockSpec((tm, D_MLSTM), lambda i: (i, 0))
    heads = pl.BlockSpec((HEADS, tm, DH), lambda i: (0, i, 0))
    return pl.pallas_call(
        _feat_kernel,
        grid=(t // tm,),
        in_specs=[tok,
                  pl.BlockSpec((16, D_MLSTM), lambda i: (jnp.maximum(i * k16 - 1, 0), 0)),
                  pl.BlockSpec((16, D_MLSTM), lambda i: (jnp.minimum((i + 1) * k16, nb16 - 1), 0)),
                  full(conv_w), full(conv_b), full(wq), full(wkt), full(wv),
                  full(wiq), full(wik), full(wiv), full(bi)],
        out_specs=[heads,
                   pl.BlockSpec((D_MLSTM, tm), lambda i: (0, i)),
                   heads, tok,
                   pl.BlockSpec((16, tm), lambda i: (0, i))],
        out_shape=[jax.ShapeDtypeStruct((HEADS, t, DH), BF16),
                   jax.ShapeDtypeStruct((D_MLSTM, t), BF16),
                   jax.ShapeDtypeStruct((HEADS, t, DH), BF16),
                   jax.ShapeDtypeStruct((t, D_MLSTM), BF16),
                   jax.ShapeDtypeStruct((16, t), F32)],
        compiler_params=_cparams("parallel"),
        name="feat",
    )(xm, xm, xm, conv_w, conv_b, wq, wkt, wv, wiq, wik, wiv, bi)


def _gates_kernel(g_ref, d_ref, gc_ref):
    g = g_ref[...]
    tl = g.shape[1]
    ig = g[0:8]
    fg = g[8:16]
    lf = jnp.minimum(fg, 0.0) - jnp.log(1.0 + jnp.exp(-jnp.abs(fg)))
    pos = lax.broadcasted_iota(jnp.int32, (8, tl), 1) & (CHUNK - 1)
    is_fwd = lax.broadcasted_iota(jnp.int32, (8, tl), 0) < HEADS

    def scan(x, op, ident):
        xf = x
        xb = x
        k = 1
        while k < CHUNK:
            xf = op(xf, jnp.where(pos >= k, pltpu.roll(xf, k, 1), ident))
            xb = op(xb, jnp.where(pos < CHUNK - k, pltpu.roll(xb, tl - k, 1), ident))
            k *= 2
        return jnp.where(is_fwd, xf, xb)

    b = scan(lf, jnp.add, 0.0)
    d = ig - b
    mloc = b + scan(d, jnp.maximum, NEG_BIG)
    d_ref[...] = d
    stack = jnp.concatenate([b, mloc, jnp.zeros((LANES - 16, tl), F32)], axis=0)
    gc_ref[...] = stack.T


def _gates(g, *, tl):
    t = g.shape[1]
    return pl.pallas_call(
        _gates_kernel,
        grid=(t // tl,),
        in_specs=[pl.BlockSpec((16, tl), lambda i: (0, i))],
        out_specs=[pl.BlockSpec((8, tl), lambda i: (0, i)),
                   pl.BlockSpec((tl, LANES), lambda i: (i, 0))],
        out_shape=[jax.ShapeDtypeStruct((8, t), F32),
                   jax.ShapeDtypeStruct((t, LANES), F32)],
        compiler_params=_cparams("parallel"),
        name="gates",
    )(g)


def _mlstm_kernel(*refs, cps, emit, ncast):
    (qf_ref, kf_ref, vf_ref, gcf_ref, drf_ref,
     qb_ref, kb_ref, vb_ref, gcb_ref, drb_ref, c0_ref, m0_ref) = refs[:12]
    cast_in = refs[12:12 + ncast]
    outs = refs[12 + ncast:]
    if emit:
        hf_ref, hb_ref, cfin_ref, mfin_ref = outs[:4]
        outs = outs[4:]
    else:
        cfin_ref, mfin_ref = outs[:2]
        outs = outs[2:]
        hf_ref = hb_ref = None
    cast_out = outs[:ncast]
    c_scr, m_scr = outs[ncast:]
    step = pl.program_id(0)

    for src, dst in zip(cast_in, cast_out):
        dst[...] = src[...].astype(BF16)

    @pl.when(step == 0)
    def _():
        c_scr[...] = c0_ref[...]
        m_scr[...] = m0_ref[...]

    ti = lax.broadcasted_iota(jnp.int32, (CHUNK, CHUNK), 0)
    si = lax.broadcasted_iota(jnp.int32, (CHUNK, CHUNK), 1)
    ones_blk = jnp.ones((HEADS, CHUNK, CHUNK), BF16)

    def bdot(a, b):
        return lax.dot_general(a, b, (((2,), (1,)), ((0,), (0,))), preferred_element_type=F32)

    for j in range(cps):
        for d in range(2):
            if d == 0:
                q_ref, k_ref, v_ref, gc_ref, dr_ref, h_ref = qf_ref, kf_ref, vf_ref, gcf_ref, drf_ref, hf_ref
                r0, mask, last = j * CHUNK, si <= ti, CHUNK - 1
            else:
                q_ref, k_ref, v_ref, gc_ref, dr_ref, h_ref = qb_ref, kb_ref, vb_ref, gcb_ref, drb_ref, hb_ref
                r0, mask, last = (cps - 1 - j) * CHUNK, si >= ti, 0
            rows = slice(r0, r0 + CHUNK)
            gc = gc_ref[rows, :]
            dr = dr_ref[:, rows]
            hs = range(d * HEADS, (d + 1) * HEADS)
            b = jnp.stack([gc[:, r:r + 1] for r in hs])
            mloc = jnp.stack([gc[:, 8 + r:9 + r] for r in hs])
            drow = jnp.stack([dr[r:r + 1, :] for r in hs])
            btot = jnp.stack([gc[last:last + 1, r:r + 1] for r in hs])
            amax = jnp.stack([gc[last:last + 1, 8 + r:9 + r] for r in hs])
            m0 = jnp.stack([m_scr[r][0:1, 0:1] for r in hs])
            c_aug = c_scr[d * HEADS:(d + 1) * HEADS]
            kt = k_ref[:, :, rows]
            vaug = jnp.concatenate([v_ref[:, rows, :], ones_blk], axis=2)
            if emit:
                qh = q_ref[:, rows, :]
                bm = b + m0
                m_t = jnp.maximum(bm, mloc)
                dmat = jnp.where(mask[None], jnp.exp((b - m_t) + drow), 0.0)
                smat = (bdot(qh, kt) * dmat).astype(BF16)
                q_in = (qh.astype(F32) * jnp.exp(bm - m_t)).astype(BF16)
                num = bdot(jnp.concatenate([smat, q_in], axis=2),
                           jnp.concatenate([vaug, c_aug.astype(BF16)], axis=1))
                den = jnp.maximum(jnp.abs(num[:, :, DH:]), jnp.exp(-m_t))
                h_ref[:, rows, :] = (num[:, :, :DH] / den).astype(h_ref.dtype)
            m_new = jnp.maximum(btot + m0, amax)
            decay = jnp.exp(btot + m0 - m_new)
            kw = (kt.astype(F32) * jnp.exp(btot + drow - m_new)).astype(BF16)
            c_scr[d * HEADS:(d + 1) * HEADS] = decay * c_aug + bdot(kw, vaug)
            m_scr[d * HEADS:(d + 1) * HEADS] = jnp.broadcast_to(m_new, (HEADS, 8, LANES))

    @pl.when(step == pl.num_programs(0) - 1)
    def _():
        cfin_ref[...] = c_scr[...]
        mfin_ref[...] = m_scr[...]


def _mlstm(q, kt, v, gc, dr, c0, m0, *, cps, emit, casts=()):
    t = q.shape[1]
    cb = cps * CHUNK
    nb = t // cb
    fwd_r = lambda i: (i, 0)
    bwd_r = lambda i: (nb - 1 - i, 0)
    fwd_c = lambda i: (0, i)
    bwd_c = lambda i: (0, nb - 1 - i)
    tok = lambda f: pl.BlockSpec((HEADS, cb, DH), lambda i, f=f: (0, f(i)[0], 0))
    in_specs = []
    for fr, fc in ((fwd_r, fwd_c), (bwd_r, bwd_c)):
        in_specs += [tok(fr), pl.BlockSpec((HEADS, DH, cb), lambda i, fc=fc: (0, 0, fc(i)[1])), tok(fr),
                     pl.BlockSpec((cb, LANES), fr), pl.BlockSpec((8, cb), fc)]
    cshape = (2 * HEADS, DH, 2 * DH)
    mshape = (2 * HEADS, 8, LANES)
    cspec = pl.BlockSpec(cshape, lambda i: (0, 0, 0))
    mspec = pl.BlockSpec(mshape, lambda i: (0, 0, 0))
    in_specs += [cspec, mspec]
    out_specs = [cspec, mspec]
    out_shape = [jax.ShapeDtypeStruct(cshape, F32), jax.ShapeDtypeStruct(mshape, F32)]
    if emit:
        out_specs = [tok(fwd_r), tok(bwd_r)] + out_specs
        out_shape = [jax.ShapeDtypeStruct((HEADS, t, DH), BF16)] * 2 + out_shape
    for a in casts:
        per = nb // a.shape[0]
        spec = pl.BlockSpec((1, a.shape[1] // per, a.shape[2]), lambda i, per=per: (i // per, i % per, 0))
        in_specs.append(spec)
        out_specs.append(spec)
        out_shape.append(jax.ShapeDtypeStruct(a.shape, BF16))
    return pl.pallas_call(
        functools.partial(_mlstm_kernel, cps=cps, emit=emit, ncast=len(casts)),
        grid=(nb,),
        in_specs=in_specs,
        out_specs=out_specs,
        out_shape=out_shape,
        scratch_shapes=[pltpu.VMEM(cshape, F32), pltpu.VMEM(mshape, F32)],
        compiler_params=_cparams("arbitrary"),
        name="mlstm",
    )(q, kt, v, gc, dr, q, kt, v, gc, dr, c0, m0, *casts)


FFT_ROWS = 8


def _dft1_kernel(u_ref, f_ref, yc_ref, ys_ref):
    f = f_ref[...]
    rows = CHUNK * FFT_ROWS
    u2 = u_ref.reshape(FGROUPS * rows, FCG)
    yc2 = yc_ref.reshape(FGROUPS * rows, FCG)
    ys2 = ys_ref.reshape(FGROUPS * rows, FCG)
    for s in range(FFT_ROWS):
        pick = [pl.ds(g * rows + s, CHUNK, stride=FFT_ROWS) for g in range(FGROUPS)]
        x = jnp.concatenate([u2[p, :] for p in pick], axis=1).astype(BF16)
        y = _dot(f, x)
        for g in range(FGROUPS):
            yc2[pick[g], :] = y[:CHUNK, g * FCG:(g + 1) * FCG]
            ys2[pick[g], :] = y[CHUNK:, g * FCG:(g + 1) * FCG]


def _dft1(u4, f1):
    blk = pl.BlockSpec((FGROUPS, CHUNK, FFT_ROWS, FCG), lambda j: (0, 0, j, 0))
    out = jax.ShapeDtypeStruct(u4.shape, F32)
    return pl.pallas_call(
        _dft1_kernel,
        grid=(u4.shape[2] // FFT_ROWS,),
        in_specs=[blk, pl.BlockSpec(f1.shape, lambda j: (0, 0))],
        out_specs=[blk, blk],
        out_shape=[out, out],
        compiler_params=_cparams("parallel"),
        name="dft1",
    )(u4, f1)


def _dft2_kernel(yc_ref, ys_ref, cw_ref, sw_ref, a2_ref, mix_ref, o_ref):
    a2 = a2_ref[...]
    rows = CHUNK * FFT_ROWS
    o2 = o_ref.reshape(FGROUPS * rows, FCG)
    for kk in range(FFT_ROWS):
        blk = slice(kk * CHUNK, (kk + 1) * CHUNK)
        yc = jnp.concatenate([yc_ref[g, blk, :] for g in range(FGROUPS)], axis=1)
        ys = jnp.concatenate([ys_ref[g, blk, :] for g in range(FGROUPS)], axis=1)
        cw = cw_ref[kk]
        sw = sw_ref[kk]
        p = jnp.concatenate([yc * cw - ys * sw, yc * sw + ys * cw], axis=0).astype(BF16)
        x = _dot(a2, p).astype(BF16)
        for g in range(FGROUPS):
            cols = slice(g * FCG, (g + 1) * FCG)
            cat = jnp.concatenate([x[:CHUNK, cols], x[CHUNK:, cols]], axis=1)
            o2[pl.ds(g * rows + kk, CHUNK, stride=FFT_ROWS), :] = _dot(cat, mix_ref[g])


def _dft2(yc, ys, cw3, sw3, a2, mix):
    t = yc.shape[1]
    n1 = t // CHUNK
    tok = pl.BlockSpec((FGROUPS, FFT_ROWS * CHUNK, FCG), lambda i: (0, i, 0))
    tw = pl.BlockSpec((FFT_ROWS, CHUNK, 1), lambda i: (i, 0, 0))
    full = lambda a: pl.BlockSpec(a.shape, lambda i: (0,) * a.ndim)
    return pl.pallas_call(
        _dft2_kernel,
        grid=(n1 // FFT_ROWS,),
        in_specs=[tok, tok, tw, tw, full(a2), full(mix)],
        out_specs=pl.BlockSpec((FGROUPS, CHUNK, FFT_ROWS, FCG), lambda i: (0, 0, i, 0)),
        out_shape=jax.ShapeDtypeStruct((FGROUPS, CHUNK, n1, FCG), F32),
        compiler_params=_cparams("parallel"),
        name="dft2",
    )(yc, ys, cw3, sw3, a2, mix)


def _merge_kernel(hf_ref, hb_ref, act_ref, z_ref, yf_ref, x_ref, er_ref, ec_ref,
                  nw_ref, sk_ref, wout_ref, gg1_ref, g2_ref, sh2_ref, wr_ref, br_ref,
                  x1_ref, h2_ref, lg_ref):
    parts = []
    for hd in range(HEADS):
        hh = hf_ref[hd].astype(F32) + hb_ref[hd].astype(F32)
        dl = hh - jnp.mean(hh, axis=-1, keepdims=True)
        var = jnp.mean(dl * dl, axis=-1, keepdims=True)
        parts.append(dl * lax.rsqrt(var + EPS))
    hn = jnp.concatenate(parts, axis=-1)
    z = z_ref[...].astype(F32)
    m = (hn * nw_ref[...] + sk_ref[...] * act_ref[...].astype(F32)) * (z * _sigmoid(z))
    cat = jnp.concatenate([m.astype(BF16)] + [yf_ref[g].astype(BF16) for g in range(FGROUPS)], axis=-1)
    y = _dot(cat, wout_ref[...])
    x3 = _add_pos(x_ref[...], er_ref, ec_ref)
    xp = x3.reshape(x3.shape[0] * GRID_W, D_MODEL)
    x1 = xp + _rms(y) * gg1_ref[...]
    x1_ref[...] = x1
    h2 = _rms(x1) * g2_ref[...] + sh2_ref[...]
    h2_ref[...] = h2.astype(BF16)
    lg = _dot(h2.astype(BF16), wr_ref[...]) + br_ref[...]
    lg_ref[...] = lg.T[:32]


def _merge(hf, hb, act, z, yf, x3, er3, ec3, nw, sk, wout, gg1, g2, sh2, wr, br, *, rows):
    nr = x3.shape[0]
    t = nr * GRID_W
    tm = rows * GRID_W
    tok = pl.BlockSpec((tm, D_MLSTM), lambda i: (i, 0))
    heads = pl.BlockSpec((HEADS, tm, DH), lambda i: (0, i, 0))
    full = lambda a: pl.BlockSpec(a.shape, lambda i: (0,) * a.ndim)
    return pl.pallas_call(
        _merge_kernel,
        grid=(nr // rows,),
        in_specs=[heads, heads, tok, tok, heads,
                  pl.BlockSpec((rows, GRID_W, D_MODEL), lambda i: (i, 0, 0)),
                  pl.BlockSpec((rows, 1, D_MODEL // 2), lambda i: (i, 0, 0)),
                  pl.BlockSpec((1, GRID_W, D_MODEL // 2), lambda i: (0, 0, 0)),
                  full(nw), full(sk), full(wout), full(gg1), full(g2), full(sh2), full(wr), full(br)],
        out_specs=[pl.BlockSpec((tm, D_MODEL), lambda i: (i, 0)),
                   pl.BlockSpec((tm, D_MODEL), lambda i: (i, 0)),
                   pl.BlockSpec((32, tm), lambda i: (0, i))],
        out_shape=[jax.ShapeDtypeStruct((t, D_MODEL), F32),
                   jax.ShapeDtypeStruct((t, D_MODEL), BF16),
                   jax.ShapeDtypeStruct((32, t), F32)],
        compiler_params=_cparams("parallel"),
        name="merge",
    )(hf, hb, act, z, yf, x3, er3, ec3, nw, sk, wout, gg1, g2, sh2, wr, br)


def _route_kernel(lg_ref, pos_ref, w_ref, cnt_ref, *, sbk):
    lg = lg_ref[...]
    tl = lg.shape[1]
    g = [lg[j:j + 1] for j in range(N_GROUPS)]
    e = [lg[N_GROUPS + j:N_GROUPS + j + 1] for j in range(N_EXPERTS)]
    gmax = jnp.maximum(jnp.maximum(g[0], g[1]), jnp.maximum(g[2], g[3]))
    den = jnp.exp(g[0] - gmax) + jnp.exp(g[1] - gmax) + jnp.exp(g[2] - gmax) + jnp.exp(g[3] - gmax)
    p_sel = 1.0 / den
    sel = []
    free = jnp.ones((1, tl), F32)
    for j in range(N_GROUPS):
        s = jnp.where(g[j] >= gmax, free, 0.0)
        sel.append(s)
        free = free - s
    es = []
    for j in range(EPG):
        es.append(sel[0] * e[j] + sel[1] * e[EPG + j] + sel[2] * e[2 * EPG + j] + sel[3] * e[3 * EPG + j])
    rank = []
    for j in range(EPG):
        rj = jnp.zeros((1, tl), F32)
        for i in range(EPG):
            if i == j:
                continue
            beats = (es[i] >= es[j]) if i < j else (es[i] > es[j])
            rj = rj + jnp.where(beats, 1.0, 0.0)
        rank.append(rj)
    v1 = jnp.maximum(jnp.maximum(es[0], es[1]), jnp.maximum(es[2], es[3]))
    v2 = sum(jnp.where(rank[j] == 1.0, es[j], 0.0) for j in range(EPG))
    tt = jnp.exp(v2 - v1)
    w1 = p_sel / (1.0 + tt)
    w2 = w1 * tt
    w = [jnp.where(rank[j] == 0.0, w1, jnp.where(rank[j] == 1.0, w2, 0.0)) for j in range(EPG)]
    top2 = [jnp.where(rank[j] < 2.0, 1.0, 0.0) for j in range(EPG)]
    mem = jnp.concatenate([sel[gi] * top2[j] for gi in range(N_GROUPS) for j in range(EPG)], axis=0)
    wts = jnp.concatenate([sel[gi] * w[j] for gi in range(N_GROUPS) for j in range(EPG)], axis=0)
    w_ref[...] = wts
    lane = lax.broadcasted_iota(jnp.int32, (N_EXPERTS, tl), 1) & (sbk - 1)
    c = mem
    k = 1
    while k < sbk:
        c = c + jnp.where(lane >= k, pltpu.roll(c, k, 1), 0.0)
        k *= 2
    pos_ref[...] = jnp.where(mem > 0.0, c - 1.0, -1.0)
    lane128 = lax.broadcasted_iota(jnp.int32, (N_EXPERTS, LANES), 1)
    cnt = jnp.zeros((N_EXPERTS, LANES), F32)
    for kb in range(tl // sbk):
        tot = jnp.sum(mem[:, kb * sbk:(kb + 1) * sbk], axis=1, keepdims=True)
        cnt = cnt + jnp.where(lane128 == kb, tot, 0.0)
    cnt_ref[...] = cnt


def _route(lg, *, tl, sbk):
    t = lg.shape[1]
    row = pl.BlockSpec((N_EXPERTS, tl), lambda i: (0, i))
    return pl.pallas_call(
        functools.partial(_route_kernel, sbk=sbk),
        grid=(t // tl,),
        in_specs=[pl.BlockSpec((32, tl), lambda i: (0, i))],
        out_specs=[row, row, pl.BlockSpec((N_EXPERTS, LANES), lambda i: (i, 0))],
        out_shape=[jax.ShapeDtypeStruct((N_EXPERTS, t), F32),
                   jax.ShapeDtypeStruct((N_EXPERTS, t), F32),
                   jax.ShapeDtypeStruct((t // tl * N_EXPERTS, LANES), F32)],
        compiler_params=_cparams("parallel"),
        name="route",
    )(lg)


def _slots_kernel(tab_ref, pos_ref, w_ref, qrow_ref, qcol_ref, *, off_r, off_o):
    i = pl.program_id(0)
    tl = pos_ref.shape[1]
    q0 = jnp.full((1, tl), -1.0, F32)
    q1 = jnp.full((1, tl), -1.0, F32)
    w0 = jnp.zeros((1, tl), F32)
    w1 = jnp.zeros((1, tl), F32)
    seen = jnp.zeros((1, tl), F32)
    for ex in range(N_EXPERTS):
        rk = pos_ref[ex:ex + 1, :]
        wt = w_ref[ex:ex + 1, :]
        rows = tab_ref[off_r + i * N_EXPERTS + ex].astype(F32)
        first_row = tab_ref[off_o + i * N_EXPERTS + ex].astype(F32)
        m = jnp.where(rk >= 0.0, 1.0, 0.0)
        val = jnp.where(rk < rows, rk + first_row, -1.0)
        first = (m * (1.0 - seen)) > 0.0
        second = (m * seen) > 0.0
        q0 = jnp.where(first, val, q0)
        w0 = jnp.where(first, wt, w0)
        q1 = jnp.where(second, val, q1)
        w1 = jnp.where(second, wt, w1)
        seen = seen + m
    qrow_ref[...] = jnp.concatenate([q0, q1, jnp.zeros((6, tl), F32)], axis=0)
    qcol_ref[...] = jnp.concatenate([q0, q1, w0, w1, jnp.zeros((LANES - 4, tl), F32)], axis=0).T


def _slots(tab, pos, w, *, tl, off_r, off_o):
    t = pos.shape[1]
    row = pl.BlockSpec((N_EXPERTS, tl), lambda i, c: (0, i))
    return pl.pallas_call(
        functools.partial(_slots_kernel, off_r=off_r, off_o=off_o),
        grid_spec=pltpu.PrefetchScalarGridSpec(
            num_scalar_prefetch=1,
            grid=(t // tl,),
            in_specs=[row, row],
            out_specs=[pl.BlockSpec((8, tl), lambda i, c: (0, i)),
                       pl.BlockSpec((tl, LANES), lambda i, c: (i, 0))]),
        out_shape=[jax.ShapeDtypeStruct((8, t), F32),
                   jax.ShapeDtypeStruct((t, LANES), F32)],
        compiler_params=_cparams("parallel"),
        name="slots",
    )(tab, pos, w)


def _mlp(x, wg, wu, wd):
    gt = _dot(x, wg)
    a = ((gt * _sigmoid(gt)) * _dot(x, wu)).astype(BF16)
    return _dot(a, wd).astype(BF16)


def _moe_kernel(tab_ref, h_ref, qrow_ref, pos_ref, w_ref, wg_ref, wu_ref, wd_ref,
                qcol_ref, x1_ref, gg2_ref, o_ref, xs_ref, ovf_ref, *, sbk, slots, csb, off_r, off_o, off_f):
    i = pl.program_id(0)
    step = pl.program_id(1)
    nsb = h_ref.shape[0] // sbk
    eps = wg_ref.shape[0]
    exp_steps = N_EXPERTS // eps

    @pl.when(step == 0)
    def _():
        def zero(sb, carry):
            ovf_ref[pl.ds(pl.multiple_of(sb * sbk, sbk), sbk), :] = jnp.zeros((sbk, D_MODEL), ovf_ref.dtype)
            return carry

        lax.fori_loop(0, nsb, zero, 0)
        pid = lax.broadcasted_iota(jnp.int32, (slots, sbk), 0).astype(F32)

        def select(sb, carry):
            q0 = qrow_ref[0, pl.ds(sb, 1), :]
            q1 = qrow_ref[1, pl.ds(sb, 1), :]
            s = jnp.where(q0 == pid, 1.0, jnp.where(q1 == pid, 1.0, 0.0)).astype(BF16)
            row0 = pl.multiple_of(sb * sbk, sbk)
            xs_ref[sb] = _dot(s, h_ref[pl.ds(row0, sbk), :]).astype(BF16)
            return carry

        lax.fori_loop(0, nsb, select, 0)

    def run_rows(k, first, n):
        first = pl.multiple_of(first, 16)
        x = jnp.concatenate([xs_ref[sb, pl.ds(first, n), :] for sb in range(nsb)], axis=0)
        y = _mlp(x, wg_ref[k], wu_ref[k], wd_ref[k])
        for sb in range(nsb):
            xs_ref[sb, pl.ds(first, n), :] = y[sb * n:(sb + 1) * n]

    @pl.when(step < exp_steps)
    def _():
        def expert(k, carry):
            e = step * eps + k
            n_rows = tab_ref[off_r + i * N_EXPERTS + e]
            row_a = tab_ref[off_o + i * N_EXPERTS + e]
            over = tab_ref[off_f + i * N_EXPERTS + e]
            n_pairs = n_rows // 32

            def pair(t, c1):
                run_rows(k, row_a + t * 32, 32)
                return c1

            lax.fori_loop(0, n_pairs, pair, 0)

            @pl.when(n_rows % 32 != 0)
            def _():
                run_rows(k, row_a + n_pairs * 32, 16)

            @pl.when(over > 0)
            def _():
                rid = lax.broadcasted_iota(jnp.int32, (LANES, sbk), 0).astype(F32)
                wg, wu, wd = wg_ref[k], wu_ref[k], wd_ref[k]

                def sub_block(sb, c1):
                    n_over = jnp.maximum(tab_ref[(i * nsb + sb) * N_EXPERTS + e] - n_rows, 0)
                    row0 = pl.multiple_of(sb * sbk, sbk)

                    def one_pass(s, c2):
                        base = (n_rows + s * LANES).astype(F32)
                        hit = pos_ref[pl.ds(e, 1), pl.ds(sb, 1), :].reshape(1, sbk) == (rid + base)
                        sel = jnp.where(hit, 1.0, 0.0).astype(BF16)
                        yo = _mlp(_dot(sel, h_ref[pl.ds(row0, sbk), :]).astype(BF16), wg, wu, wd)
                        wct = jnp.where(hit, w_ref[pl.ds(e, 1), pl.ds(sb, 1), :].reshape(1, sbk), 0.0).T
                        acc = ovf_ref[pl.ds(row0, sbk), :].astype(F32) + _dot(wct.astype(BF16), yo)
                        ovf_ref[pl.ds(row0, sbk), :] = acc.astype(ovf_ref.dtype)
                        return c2

                    return lax.fori_loop(0, (n_over + LANES - 1) // LANES, one_pass, c1)

                lax.fori_loop(0, nsb, sub_block, 0)

            return carry

        lax.fori_loop(0, eps, expert, 0)

    @pl.when(step >= exp_steps)
    def _():
        lane = lax.broadcasted_iota(jnp.int32, (sbk, slots), 1).astype(F32)
        for k in range(csb):
            sb = (step - exp_steps) * csb + k
            rows = slice(k * sbk, (k + 1) * sbk)
            qc = qcol_ref[rows, :]
            wmat = (jnp.where(lane == qc[:, 0:1], qc[:, 2:3], 0.0)
                    + jnp.where(lane == qc[:, 1:2], qc[:, 3:4], 0.0)).astype(BF16)
            y = _dot(wmat, xs_ref[sb]) + ovf_ref[pl.ds(pl.multiple_of(sb * sbk, sbk), sbk), :].astype(F32)
            o_ref[rows, :] = x1_ref[rows, :] + _rms(y) * gg2_ref[...]


def _moe(tab, h2, qrow3, pos3, w3, wg, wu, wd, qcol, x1, gg2, *, tb, sbk, slots, eps, csb, off_r, off_o, off_f):
    t = h2.shape[0]
    nsb = tb // sbk
    comb_steps = nsb // csb
    exp_steps = N_EXPERTS // eps
    wblk = lambda i, s, c: (jnp.minimum(s, exp_steps - 1), 0, 0)
    oblk = lambda i, s, c: (i * comb_steps + jnp.maximum(s - exp_steps, 0), 0)
    r3 = pl.BlockSpec((N_EXPERTS, nsb, sbk), lambda i, s, c: (0, i, 0))
    return pl.pallas_call(
        functools.partial(_moe_kernel, sbk=sbk, slots=slots, csb=csb,
                          off_r=off_r, off_o=off_o, off_f=off_f),
        grid_spec=pltpu.PrefetchScalarGridSpec(
            num_scalar_prefetch=1,
            grid=(t // tb, exp_steps + comb_steps),
            in_specs=[pl.BlockSpec((tb, D_MODEL), lambda i, s, c: (i, 0)),
                      pl.BlockSpec((8, nsb, sbk), lambda i, s, c: (0, i, 0)),
                      r3, r3,
                      pl.BlockSpec((eps, D_MODEL, D_EXPERT), wblk),
                      pl.BlockSpec((eps, D_MODEL, D_EXPERT), wblk),
                      pl.BlockSpec((eps, D_EXPERT, D_MODEL), wblk),
                      pl.BlockSpec((csb * sbk, LANES), oblk),
                      pl.BlockSpec((csb * sbk, D_MODEL), oblk),
                      pl.BlockSpec((1, D_MODEL), lambda i, s, c: (0, 0))],
            out_specs=pl.BlockSpec((csb * sbk, D_MODEL), oblk),
            scratch_shapes=[pltpu.VMEM((nsb, slots, D_MODEL), BF16),
                            pltpu.VMEM((tb, D_MODEL), BF16)]),
        out_shape=jax.ShapeDtypeStruct((t, D_MODEL), F32),
        compiler_params=pltpu.CompilerParams(dimension_semantics=("parallel", "arbitrary"),
                                             vmem_limit_bytes=MOE_VMEM_LIMIT),
        name="moe",
    )(tab, h2, qrow3, pos3, w3, wg, wu, wd, qcol, x1, gg2)


def _slot_tables(cnt, *, slots):
    need = ((jnp.max(cnt, axis=1) + 15) // 16) * 16
    start = jnp.cumsum(need, axis=1) - need
    rows = jnp.clip(slots - start, 0, need)
    over = jnp.max(jnp.maximum(cnt - rows[:, None, :], 0), axis=1)
    return rows, start, over


def _pos_tables(rows):
    quarter = D_MODEL // 4
    freq = 1.0 / (POS_BASE ** (np.arange(quarter, dtype=np.float64) / quarter))
    r = np.arange(rows, dtype=np.float64)[:, None] * freq
    cl = np.arange(GRID_W, dtype=np.float64)[:, None] * freq
    er = np.concatenate([np.sin(r), np.cos(r)], axis=-1).astype(np.float32)
    ec = np.concatenate([np.sin(cl), np.cos(cl)], axis=-1).astype(np.float32)
    return jnp.asarray(er[:, None, :]), jnp.asarray(ec[None, :, :])


def _dft_tables(t):
    n = np.arange(CHUNK, dtype=np.int64)
    prod = n[:, None] * n[None, :]
    ang = (prod % CHUNK).astype(np.float64) * (2.0 * np.pi / CHUNK)
    c, s = np.cos(ang), np.sin(ang)
    f1 = np.concatenate([c, s], axis=0)
    a2 = np.concatenate([np.concatenate([c, -s], axis=1), np.concatenate([s, c], axis=1)], axis=0)
    cs = np.concatenate([c, -s], axis=0)
    angw = prod.astype(np.float64) * (2.0 * np.pi / t)
    f32 = lambda a: jnp.asarray(a.astype(np.float32))
    return f32(f1), f32(a2), f32(cs), f32(np.cos(angw)[:, :, None]), f32(np.sin(angw)[:, :, None])


def _blockdiag(w):
    n = w.shape[0]
    size = n * QKV_BLOCK
    spread = np.tile(np.eye(QKV_BLOCK, dtype=np.float32), (1, n))
    rows = jnp.dot(w.reshape(size, QKV_BLOCK), jnp.asarray(spread), precision=lax.Precision.HIGHEST)
    blk = np.arange(size) // QKV_BLOCK
    mask = (blk[:, None] == blk[None, :]).astype(np.float32)
    return rows * jnp.asarray(mask)


def _gate_weights(w_f, b_f, w_b, b_b):
    w = jnp.concatenate([w_f[:, :HEADS], w_b[:, :HEADS], w_f[:, HEADS:], w_b[:, HEADS:]], axis=1).T
    b = jnp.concatenate([b_f[:HEADS], b_b[:HEADS], b_f[HEADS:], b_b[HEADS:]])
    return w.astype(BF16), b[:, None]


def kernel(x, c, ctx, c_ctx, w_ada, b_ada, g_pre_mix, g_post_mix, g_pre_ffn, g_post_ffn,
           w_in, conv_w, conv_b, w_q, w_k, w_v, w_if_fwd, b_if_fwd, w_if_bwd, b_if_bwd,
           mlstm_norm_w, mlstm_skip, w_fourier, w_out, w_router_group, b_router_group,
           w_router_expert, b_router_expert, w_gate, w_up, w_down):
    t = x.shape[1]
    rows = t // GRID_W

    c8 = jnp.concatenate([c, c_ctx[None, :], jnp.zeros((6, D_MODEL), F32)], axis=0)
    mod = _ada(c8, w_ada[0], b_ada[0][None, :])
    shift1, scale1, gate1, shift2, scale2, gate2 = [mod[0:1, k * D_MODEL:(k + 1) * D_MODEL] for k in range(N_MOD)]
    shift1c, scale1c = mod[1:2, 0:D_MODEL], mod[1:2, D_MODEL:2 * D_MODEL]
    g1 = g_pre_mix[0][None, :] * (1.0 + scale1)
    g1c = g_pre_mix[0][None, :] * (1.0 + scale1c)
    gg1 = g_post_mix[0][None, :] * gate1
    g2 = g_pre_ffn[0][None, :] * (1.0 + scale2)
    gg2 = g_post_ffn[0][None, :] * gate2

    er3, ec3 = _pos_tables(rows)
    x3 = x.reshape(rows, GRID_W, D_MODEL)
    ctx3 = ctx.reshape(CTX_LEN // GRID_W, GRID_W, D_MODEL)
    w_in_bf = w_in[0].astype(BF16)

    xm_l, z_l, u_l = _inproj(x3, er3, ec3, g1, shift1, w_in_bf, rows=8, add_pos=True)
    xm_c, _, _ = _inproj(ctx3, er3, ec3, g1c, shift1c, w_in_bf, rows=CTX_LEN // GRID_W, add_pos=False)

    wq = _blockdiag(w_q[0]).astype(BF16)
    wkt = _blockdiag(w_k[0]).T.astype(BF16)
    wv = _blockdiag(w_v[0]).astype(BF16)
    wi, bi = _gate_weights(w_if_fwd[0], b_if_fwd[0], w_if_bwd[0], b_if_bwd[0])
    wiq, wik, wiv = wi[:, :D_MLSTM], wi[:, D_MLSTM:2 * D_MLSTM], wi[:, 2 * D_MLSTM:]
    cb = conv_b[0][None, :]
    q_l, kt_l, v_l, act_l, gp_l = _feat(xm_l, conv_w[0], cb, wq, wkt, wv, wiq, wik, wiv, bi, tm=512)
    q_c, kt_c, v_c, _, gp_c = _feat(xm_c, conv_w[0], cb, wq, wkt, wv, wiq, wik, wiv, bi, tm=CTX_LEN)

    dr_l, gc_l = _gates(gp_l, tl=2048)
    dr_c, gc_c = _gates(gp_c, tl=CTX_LEN)

    c0 = jnp.zeros((2 * HEADS, DH, 2 * DH), F32)
    m0 = jnp.zeros((2 * HEADS, 8, LANES), F32)
    kt_c, kt_l = kt_c.reshape(HEADS, DH, CTX_LEN), kt_l.reshape(HEADS, DH, t)
    c_ctx_fin, m_ctx_fin = _mlstm(q_c, kt_c, v_c, gc_c, dr_c, c0, m0, cps=CTX_LEN // CHUNK, emit=False)
    hf, hb, _, _, wg_bf, wu_bf, wd_bf = _mlstm(q_l, kt_l, v_l, gc_l, dr_l, c_ctx_fin, m_ctx_fin, cps=4, emit=True,
                                               casts=(w_gate[0], w_up[0], w_down[0]))

    f1, a2, cs, cw3, sw3 = _dft_tables(t)
    n1 = t // CHUNK
    yc, ys = _dft1(u_l.reshape(FGROUPS, n1, CHUNK, FCG), f1.astype(BF16))
    mix = jnp.einsum('kc,gcd->gkd', cs, w_fourier[0], precision=lax.Precision.HIGHEST)
    mix = (mix * float(1.0 / np.sqrt(float(t) * FCG))).astype(BF16)
    yf = _dft2(yc.reshape(FGROUPS, t, FCG), ys.reshape(FGROUPS, t, FCG), cw3, sw3, a2.astype(BF16), mix)
    yf = yf.reshape(FGROUPS, t, FCG)

    wr = jnp.concatenate([w_router_group[0], w_router_expert[0],
                          jnp.zeros((D_MODEL, LANES - N_GROUPS - N_EXPERTS), F32)], axis=1)
    br = jnp.concatenate([b_router_group[0], b_router_expert[0],
                          jnp.zeros((LANES - N_GROUPS - N_EXPERTS,), F32)])[None, :]
    x1, h2, lg = _merge(hf, hb, act_l, z_l, yf, x3, er3, ec3,
                        mlstm_norm_w[0][None, :], mlstm_skip[0][None, :], w_out[0].astype(BF16),
                        gg1, g2, shift2, wr.astype(BF16), br, rows=8)
    pos, wts, cnt = _route(lg, tl=MOE_TB, sbk=MOE_SBK)
    nblk, nsb = t // MOE_TB, MOE_TB // MOE_SBK
    cnt = cnt.reshape(nblk, N_EXPERTS, LANES)[:, :, :nsb]
    cnt = jnp.transpose(cnt, (0, 2, 1)).astype(jnp.int32)
    rows, start, over = _slot_tables(cnt, slots=MOE_SLOTS)
    tab = jnp.concatenate([a.reshape(-1) for a in (cnt, rows, start, over)])
    off_r = nblk * nsb * N_EXPERTS
    off_o = off_r + nblk * N_EXPERTS
    off_f = off_o + nblk * N_EXPERTS
    qrow, qcol = _slots(tab, pos, wts, tl=MOE_TB, off_r=off_r, off_o=off_o)
    sub = (t // MOE_SBK, MOE_SBK)
    out = _moe(tab, h2, qrow.reshape(8, *sub), pos.reshape(N_EXPERTS, *sub), wts.reshape(N_EXPERTS, *sub),
               wg_bf, wu_bf, wd_bf,
               qcol, x1, gg2, tb=MOE_TB, sbk=MOE_SBK, slots=MOE_SLOTS, eps=MOE_EPS, csb=MOE_CSB,
               off_r=off_r, off_o=off_o, off_f=off_f)
    return out[None]
```

```python
import functools

import numpy as np
import jax
import jax.numpy as jnp
from jax import lax
from jax.experimental import pallas as pl
from jax.experimental.pallas import tpu as pltpu

F32 = jnp.float32
BF16 = jnp.bfloat16

D_MODEL = 1024
SEQ = 16384
GRID_W = 64
CTX_LEN = 256
D_MLSTM = 512
HEADS = 4
DH = 128
QKV_BLOCK = 4
CONV_K = 3
CHUNK = 128
D_FOURIER = 512
FGROUPS = 4
FCG = 128
N_GROUPS = 4
EPG = 4
N_EXPERTS = 16
D_EXPERT = 512
N_MOD = 6
EPS = 1e-6
POS_BASE = 10000.0
LANES = 128
NEG_BIG = -3.0e38

VMEM_LIMIT = 52 * 1024 * 1024
MOE_VMEM_LIMIT = 58 * 1024 * 1024
MOE_TB = 2048
MOE_SBK = 256
MOE_SLOTS = 1024
MOE_EPS = 2
MOE_CSB = 2


def _cparams(*sem):
    return pltpu.CompilerParams(dimension_semantics=sem, vmem_limit_bytes=VMEM_LIMIT)


def _dot(a, b):
    return jnp.dot(a, b, preferred_element_type=F32)


def _dot_nt(a, b):
    return lax.dot_general(a, b, (((1,), (1,)), ((), ())), preferred_element_type=F32)


def _split_bf16(a):
    hi = a.astype(BF16)
    lo = (a - hi.astype(F32)).astype(BF16)
    return hi, lo


def _dot3(a, b):
    a_hi, a_lo = _split_bf16(a)
    b_hi, b_lo = _split_bf16(b)
    return _dot(a_hi, b_hi) + (_dot(a_hi, b_lo) + _dot(a_lo, b_hi))


def _sigmoid(x):
    return 1.0 / (1.0 + jnp.exp(-x))


def _rms(x):
    return x * lax.rsqrt(jnp.mean(x * x, axis=-1, keepdims=True) + EPS)


def _ada_kernel(c_ref, w_ref, b_ref, o_ref):
    c = c_ref[...]
    s = c * _sigmoid(c)
    o_ref[...] = _dot3(s, w_ref[...]) + b_ref[...]


def _ada(c8, w, b):
    n = w.shape[1]
    tn = 768
    return pl.pallas_call(
        _ada_kernel,
        grid=(n // tn,),
        in_specs=[pl.BlockSpec((8, D_MODEL), lambda j: (0, 0)),
                  pl.BlockSpec((D_MODEL, tn), lambda j: (0, j)),
                  pl.BlockSpec((1, tn), lambda j: (0, j))],
        out_specs=pl.BlockSpec((8, tn), lambda j: (0, j)),
        out_shape=jax.ShapeDtypeStruct((8, n), F32),
        compiler_params=_cparams("parallel"),
        name="ada",
    )(c8, w, b)


def _add_pos(x3, er_ref, ec_ref):
    r = x3.shape[0]
    pr = jnp.broadcast_to(er_ref[...], (r, GRID_W, D_MODEL // 2))
    pc = jnp.broadcast_to(ec_ref[...], (r, GRID_W, D_MODEL // 2))
    return x3 + jnp.concatenate([pr, pc], axis=-1)


def _inproj_kernel(x_ref, er_ref, ec_ref, g_ref, sh_ref, w_ref, xm_ref, z_ref, u_ref, *, add_pos):
    x3 = x_ref[...]
    if add_pos:
        x3 = _add_pos(x3, er_ref, ec_ref)
    x = x3.reshape(x3.shape[0] * GRID_W, D_MODEL)
    h = _rms(x) * g_ref[...] + sh_ref[...]
    proj = _dot(h.astype(BF16), w_ref[...])
    xm_ref[...] = proj[:, :D_MLSTM].astype(BF16)
    z_ref[...] = proj[:, D_MLSTM:2 * D_MLSTM].astype(BF16)
    for g in range(FGROUPS):
        u_ref[g] = proj[:, 2 * D_MLSTM + g * FCG:2 * D_MLSTM + (g + 1) * FCG]


def _inproj(x3, er3, ec3, g_eff, shift, w_in, *, rows, add_pos):
    nr = x3.shape[0]
    t = nr * GRID_W
    tm = rows * GRID_W
    out = jax.ShapeDtypeStruct((t, D_MLSTM), BF16)
    ospec = pl.BlockSpec((tm, D_MLSTM), lambda i: (i, 0))
    vec = pl.BlockSpec((1, D_MODEL), lambda i: (0, 0))
    return pl.pallas_call(
        functools.partial(_inproj_kernel, add_pos=add_pos),
        grid=(nr // rows,),
        in_specs=[pl.BlockSpec((rows, GRID_W, D_MODEL), lambda i: (i, 0, 0)),
                  pl.BlockSpec((rows, 1, D_MODEL // 2), lambda i: (i, 0, 0)),
                  pl.BlockSpec((1, GRID_W, D_MODEL // 2), lambda i: (0, 0, 0)),
                  vec, vec,
                  pl.BlockSpec(w_in.shape, lambda i: (0, 0))],
        out_specs=[ospec, ospec, pl.BlockSpec((FGROUPS, tm, FCG), lambda i: (0, i, 0))],
        out_shape=[out, out, jax.ShapeDtypeStruct((FGROUPS, t, FCG), F32)],
        compiler_params=_cparams("parallel"),
        name="inproj",
    )(x3, er3, ec3, g_eff, shift, w_in)


def _feat_kernel(xm_ref, prev_ref, next_ref, cw_ref, cb_ref, wq_ref, wkt_ref, wv_ref,
                 wiq_ref, wik_ref, wiv_ref, bi_ref,
                 q_ref, kt_ref, v_ref, act_ref, g_ref):
    i = pl.program_id(0)
    n = pl.num_programs(0)
    xm_bf = xm_ref[...]
    xm = xm_bf.astype(F32)
    tm = xm.shape[0]
    prev_row = prev_ref[...].astype(F32)[15:16, :] * jnp.where(i > 0, 1.0, 0.0)
    next_row = next_ref[...].astype(F32)[0:1, :] * jnp.where(i < n - 1, 1.0, 0.0)
    rid = lax.broadcasted_iota(jnp.int32, (tm, 1), 0)
    x_left = jnp.where(rid == 0, prev_row, pltpu.roll(xm, 1, 0))
    x_right = jnp.where(rid == tm - 1, next_row, pltpu.roll(xm, tm - 1, 0))
    cw = cw_ref[...]
    y = cw[0:1] * x_left + cw[1:2] * xm + cw[2:3] * x_right + cb_ref[...]
    act = (y * _sigmoid(y)).astype(BF16)
    act_ref[...] = act
    q = _dot(act, wq_ref[...])
    kt = _dot_nt(wkt_ref[...], act)
    v = _dot(xm_bf, wv_ref[...])
    q_bf = q.astype(BF16)
    kt_bf = kt.astype(BF16)
    v_bf = v.astype(BF16)
    q_s = (q * (DH ** -0.5)).astype(BF16)
    for hd in range(HEADS):
        q_ref[hd] = q_s[:, hd * DH:(hd + 1) * DH]
        v_ref[hd] = v_bf[:, hd * DH:(hd + 1) * DH]
    kt_ref[...] = kt_bf
    g = _dot_nt(wiq_ref[...], q_bf) + _dot(wik_ref[...], kt_bf) + _dot_nt(wiv_ref[...], v_bf)
    g_ref[...] = g + bi_ref[...]


def _feat(xm, conv_w, conv_b, wq, wkt, wv, wiq, wik, wiv, bi, *, tm):
    t = xm.shape[0]
    nb16 = t // 16
    k16 = tm // 16
    full = lambda a: pl.BlockSpec(a.shape, lambda i: (0,) * a.ndim)
    tok = pl.BlockSpec((tm, D_MLSTM), lambda i: (i, 0))
    heads = pl.BlockSpec((HEADS, tm, DH), lambda i: (0, i, 0))
    return pl.pallas_call(
        _feat_kernel,
        grid=(t // tm,),
        in_specs=[tok,
                  pl.BlockSpec((16, D_MLSTM), lambda i: (jnp.maximum(i * k16 - 1, 0), 0)),
                  pl.BlockSpec((16, D_MLSTM), lambda i: (jnp.minimum((i + 1) * k16, nb16 - 1), 0)),
                  full(conv_w), full(conv_b), full(wq), full(wkt), full(wv),
                  full(wiq), full(wik), full(wiv), full(bi)],
        out_specs=[heads,
                   pl.BlockSpec((D_MLSTM, tm), lambda i: (0, i)),
                   heads, tok,
                   pl.BlockSpec((16, tm), lambda i: (0, i))],
        out_shape=[jax.ShapeDtypeStruct((HEADS, t, DH), BF16),
                   jax.ShapeDtypeStruct((D_MLSTM, t), BF16),
                   jax.ShapeDtypeStruct((HEADS, t, DH), BF16),
                   jax.ShapeDtypeStruct((t, D_MLSTM), BF16),
                   jax.ShapeDtypeStruct((16, t), F32)],
        compiler_params=_cparams("parallel"),
        name="feat",
    )(xm, xm, xm, conv_w, conv_b, wq, wkt, wv, wiq, wik, wiv, bi)


def _gates_kernel(g_ref, d_ref, gc_ref):
    g = g_ref[...]
    tl = g.shape[1]
    ig = g[0:8]
    fg = g[8:16]
    lf = jnp.minimum(fg, 0.0) - jnp.log(1.0 + jnp.exp(-jnp.abs(fg)))
    pos = lax.broadcasted_iota(jnp.int32, (8, tl), 1) & (CHUNK - 1)
    is_fwd = lax.broadcasted_iota(jnp.int32, (8, tl), 0) < HEADS

    def scan(x, op, ident):
        xf = x
        xb = x
        k = 1
        while k < CHUNK:
            xf = op(xf, jnp.where(pos >= k, pltpu.roll(xf, k, 1), ident))
            xb = op(xb, jnp.where(pos < CHUNK - k, pltpu.roll(xb, tl - k, 1), ident))
            k *= 2
        return jnp.where(is_fwd, xf, xb)

    b = scan(lf, jnp.add, 0.0)
    d = ig - b
    mloc = b + scan(d, jnp.maximum, NEG_BIG)
    d_ref[...] = d
    stack = jnp.concatenate([b, mloc, jnp.zeros((LANES - 16, tl), F32)], axis=0)
    gc_ref[...] = stack.T


def _gates(g, *, tl):
    t = g.shape[1]
    return pl.pallas_call(
        _gates_kernel,
        grid=(t // tl,),
        in_specs=[pl.BlockSpec((16, tl), lambda i: (0, i))],
        out_specs=[pl.BlockSpec((8, tl), lambda i: (0, i)),
                   pl.BlockSpec((tl, LANES), lambda i: (i, 0))],
        out_shape=[jax.ShapeDtypeStruct((8, t), F32),
                   jax.ShapeDtypeStruct((t, LANES), F32)],
        compiler_params=_cparams("parallel"),
        name="gates",
    )(g)


def _mlstm_kernel(*refs, cps, emit, ncast):
    (qf_ref, kf_ref, vf_ref, gcf_ref, drf_ref,
     qb_ref, kb_ref, vb_ref, gcb_ref, drb_ref, c0_ref, m0_ref) = refs[:12]
    cast_in = refs[12:12 + ncast]
    outs = refs[12 + ncast:]
    if emit:
        hf_ref, hb_ref, cfin_ref, mfin_ref = outs[:4]
        outs = outs[4:]
    else:
        cfin_ref, mfin_ref = outs[:2]
        outs = outs[2:]
        hf_ref = hb_ref = None
    cast_out = outs[:ncast]
    c_scr, m_scr = outs[ncast:]
    step = pl.program_id(0)

    for src, dst in zip(cast_in, cast_out):
        dst[...] = src[...].astype(BF16)

    @pl.when(step == 0)
    def _():
        c_scr[...] = c0_ref[...]
        m_scr[...] = m0_ref[...]

    ti = lax.broadcasted_iota(jnp.int32, (CHUNK, CHUNK), 0)
    si = lax.broadcasted_iota(jnp.int32, (CHUNK, CHUNK), 1)
    ones_blk = jnp.ones((HEADS, CHUNK, CHUNK), BF16)

    def bdot(a, b):
        return lax.dot_general(a, b, (((2,), (1,)), ((0,), (0,))), preferred_element_type=F32)

    for j in range(cps):
        for d in range(2):
            if d == 0:
                q_ref, k_ref, v_ref, gc_ref, dr_ref, h_ref = qf_ref, kf_ref, vf_ref, gcf_ref, drf_ref, hf_ref
                r0, mask, last = j * CHUNK, si <= ti, CHUNK - 1
            else:
                q_ref, k_ref, v_ref, gc_ref, dr_ref, h_ref = qb_ref, kb_ref, vb_ref, gcb_ref, drb_ref, hb_ref
                r0, mask, last = (cps - 1 - j) * CHUNK, si >= ti, 0
            rows = slice(r0, r0 + CHUNK)
            gc = gc_ref[rows, :]
            dr = dr_ref[:, rows]
            hs = range(d * HEADS, (d + 1) * HEADS)
            b = jnp.stack([gc[:, r:r + 1] for r in hs])
            mloc = jnp.stack([gc[:, 8 + r:9 + r] for r in hs])
            drow = jnp.stack([dr[r:r + 1, :] for r in hs])
            btot = jnp.stack([gc[last:last + 1, r:r + 1] for r in hs])
            amax = jnp.stack([gc[last:last + 1, 8 + r:9 + r] for r in hs])
            m0 = jnp.stack([m_scr[r][0:1, 0:1] for r in hs])
            c_aug = c_scr[d * HEADS:(d + 1) * HEADS]
            kt = k_ref[:, :, rows]
            vaug = jnp.concatenate([v_ref[:, rows, :], ones_blk], axis=2)
            if emit:
                qh = q_ref[:, rows, :]
                bm = b + m0
                m_t = jnp.maximum(bm, mloc)
                dmat = jnp.where(mask[None], jnp.exp((b - m_t) + drow), 0.0)
                smat = (bdot(qh, kt) * dmat).astype(BF16)
                q_in = (qh.astype(F32) * jnp.exp(bm - m_t)).astype(BF16)
                num = bdot(jnp.concatenate([smat, q_in], axis=2),
                           jnp.concatenate([vaug, c_aug.astype(BF16)], axis=1))
                den = jnp.maximum(jnp.abs(num[:, :, DH:]), jnp.exp(-m_t))
                h_ref[:, rows, :] = (num[:, :, :DH] / den).astype(h_ref.dtype)
            m_new = jnp.maximum(btot + m0, amax)
            decay = jnp.exp(btot + m0 - m_new)
            kw = (kt.astype(F32) * jnp.exp(btot + drow - m_new)).astype(BF16)
            c_scr[d * HEADS:(d + 1) * HEADS] = decay * c_aug + bdot(kw, vaug)
            m_scr[d * HEADS:(d + 1) * HEADS] = jnp.broadcast_to(m_new, (HEADS, 8, LANES))

    @pl.when(step == pl.num_programs(0) - 1)
    def _():
        cfin_ref[...] = c_scr[...]
        mfin_ref[...] = m_scr[...]


def _mlstm(q, kt, v, gc, dr, c0, m0, *, cps, emit, casts=()):
    t = q.shape[1]
    cb = cps * CHUNK
    nb = t // cb
    fwd_r = lambda i: (i, 0)
    bwd_r = lambda i: (nb - 1 - i, 0)
    fwd_c = lambda i: (0, i)
    bwd_c = lambda i: (0, nb - 1 - i)
    tok = lambda f: pl.BlockSpec((HEADS, cb, DH), lambda i, f=f: (0, f(i)[0], 0))
    in_specs = []
    for fr, fc in ((fwd_r, fwd_c), (bwd_r, bwd_c)):
        in_specs += [tok(fr), pl.BlockSpec((HEADS, DH, cb), lambda i, fc=fc: (0, 0, fc(i)[1])), tok(fr),
                     pl.BlockSpec((cb, LANES), fr), pl.BlockSpec((8, cb), fc)]
    cshape = (2 * HEADS, DH, 2 * DH)
    mshape = (2 * HEADS, 8, LANES)
    cspec = pl.BlockSpec(cshape, lambda i: (0, 0, 0))
    mspec = pl.BlockSpec(mshape, lambda i: (0, 0, 0))
    in_specs += [cspec, mspec]
    out_specs = [cspec, mspec]
    out_shape = [jax.ShapeDtypeStruct(cshape, F32), jax.ShapeDtypeStruct(mshape, F32)]
    if emit:
        out_specs = [tok(fwd_r), tok(bwd_r)] + out_specs
        out_shape = [jax.ShapeDtypeStruct((HEADS, t, DH), BF16)] * 2 + out_shape
    for a in casts:
        per = nb // a.shape[0]
        spec = pl.BlockSpec((1, a.shape[1] // per, a.shape[2]), lambda i, per=per: (i // per, i % per, 0))
        in_specs.append(spec)
        out_specs.append(spec)
        out_shape.append(jax.ShapeDtypeStruct(a.shape, BF16))
    return pl.pallas_call(
        functools.partial(_mlstm_kernel, cps=cps, emit=emit, ncast=len(casts)),
        grid=(nb,),
        in_specs=in_specs,
        out_specs=out_specs,
        out_shape=out_shape,
        scratch_shapes=[pltpu.VMEM(cshape, F32), pltpu.VMEM(mshape, F32)],
        compiler_params=_cparams("arbitrary"),
        name="mlstm",
    )(q, kt, v, gc, dr, q, kt, v, gc, dr, c0, m0, *casts)


FFT_ROWS = 8


def _dft1_kernel(u_ref, f_ref, cw_ref, sw_ref, yc_ref, ys_ref):
    f = f_ref[...]
    rows = CHUNK * FFT_ROWS
    u2 = u_ref.reshape(FGROUPS * rows, FCG)
    yc2 = yc_ref.reshape(FGROUPS * rows, FCG)
    ys2 = ys_ref.reshape(FGROUPS * rows, FCG)
    for s in range(FFT_ROWS):
        pick = [pl.ds(g * rows + s, CHUNK, stride=FFT_ROWS) for g in range(FGROUPS)]
        x = jnp.concatenate([u2[p, :] for p in pick], axis=1).astype(BF16)
        y = _dot(f, x)
        cw = cw_ref[s]
        sw = sw_ref[s]
        pr = y[:CHUNK] * cw - y[CHUNK:] * sw
        pi = y[:CHUNK] * sw + y[CHUNK:] * cw
        for g in range(FGROUPS):
            yc2[pick[g], :] = pr[:, g * FCG:(g + 1) * FCG]
            ys2[pick[g], :] = pi[:, g * FCG:(g + 1) * FCG]


def _dft1(u4, f1, cw3, sw3):
    blk = pl.BlockSpec((FGROUPS, CHUNK, FFT_ROWS, FCG), lambda j: (0, 0, j, 0))
    tw = pl.BlockSpec((FFT_ROWS, CHUNK, 1), lambda j: (j, 0, 0))
    out = jax.ShapeDtypeStruct(u4.shape, F32)
    return pl.pallas_call(
        _dft1_kernel,
        grid=(u4.shape[2] // FFT_ROWS,),
        in_specs=[blk, pl.BlockSpec(f1.shape, lambda j: (0, 0)), tw, tw],
        out_specs=[blk, blk],
        out_shape=[out, out],
        compiler_params=_cparams("parallel"),
        name="dft1",
    )(u4, f1, cw3, sw3)


def _dft2_kernel(yc_ref, ys_ref, a2_ref, mix_ref, o_ref):
    a2 = a2_ref[...]
    rows = CHUNK * FFT_ROWS
    o2 = o_ref.reshape(FGROUPS * rows, FCG)
    for kk in range(FFT_ROWS):
        blk = slice(kk * CHUNK, (kk + 1) * CHUNK)
        yc = jnp.concatenate([yc_ref[g, blk, :] for g in range(FGROUPS)], axis=1)
        ys = jnp.concatenate([ys_ref[g, blk, :] for g in range(FGROUPS)], axis=1)
        p = jnp.concatenate([yc, ys], axis=0).astype(BF16)
        x = _dot(a2, p).astype(BF16)
        for g in range(FGROUPS):
            cols = slice(g * FCG, (g + 1) * FCG)
            cat = jnp.concatenate([x[:CHUNK, cols], x[CHUNK:, cols]], axis=1)
            o2[pl.ds(g * rows + kk, CHUNK, stride=FFT_ROWS), :] = _dot(cat, mix_ref[g])


def _dft2(yc, ys, a2, mix):
    t = yc.shape[1]
    n1 = t // CHUNK
    tok = pl.BlockSpec((FGROUPS, FFT_ROWS * CHUNK, FCG), lambda i: (0, i, 0))
    full = lambda a: pl.BlockSpec(a.shape, lambda i: (0,) * a.ndim)
    return pl.pallas_call(
        _dft2_kernel,
        grid=(n1 // FFT_ROWS,),
        in_specs=[tok, tok, full(a2), full(mix)],
        out_specs=pl.BlockSpec((FGROUPS, CHUNK, FFT_ROWS, FCG), lambda i: (0, 0, i, 0)),
        out_shape=jax.ShapeDtypeStruct((FGROUPS, CHUNK, n1, FCG), F32),
        compiler_params=_cparams("parallel"),
        name="dft2",
    )(yc, ys, a2, mix)


def _merge_kernel(hf_ref, hb_ref, act_ref, z_ref, yf_ref, x_ref, er_ref, ec_ref,
                  nw_ref, sk_ref, wout_ref, gg1_ref, g2_ref, sh2_ref, wr_ref, br_ref,
                  x1_ref, h2_ref, lg_ref):
    parts = []
    for hd in range(HEADS):
        hh = hf_ref[hd].astype(F32) + hb_ref[hd].astype(F32)
        dl = hh - jnp.mean(hh, axis=-1, keepdims=True)
        var = jnp.mean(dl * dl, axis=-1, keepdims=True)
        parts.append(dl * lax.rsqrt(var + EPS))
    hn = jnp.concatenate(parts, axis=-1)
    z = z_ref[...].astype(F32)
    m = (hn * nw_ref[...] + sk_ref[...] * act_ref[...].astype(F32)) * (z * _sigmoid(z))
    cat = jnp.concatenate([m.astype(BF16)] + [yf_ref[g].astype(BF16) for g in range(FGROUPS)], axis=-1)
    y = _dot(cat, wout_ref[...])
    x3 = _add_pos(x_ref[...], er_ref, ec_ref)
    xp = x3.reshape(x3.shape[0] * GRID_W, D_MODEL)
    x1 = xp + _rms(y) * gg1_ref[...]
    x1_ref[...] = x1
    h2 = _rms(x1) * g2_ref[...] + sh2_ref[...]
    h2_ref[...] = h2.astype(BF16)
    lg = _dot(h2.astype(BF16), wr_ref[...]) + br_ref[...]
    lg_ref[...] = lg.T[:32]


def _merge(hf, hb, act, z, yf, x3, er3, ec3, nw, sk, wout, gg1, g2, sh2, wr, br, *, rows):
    nr = x3.shape[0]
    t = nr * GRID_W
    tm = rows * GRID_W
    tok = pl.BlockSpec((tm, D_MLSTM), lambda i: (i, 0))
    heads = pl.BlockSpec((HEADS, tm, DH), lambda i: (0, i, 0))
    full = lambda a: pl.BlockSpec(a.shape, lambda i: (0,) * a.ndim)
    return pl.pallas_call(
        _merge_kernel,
        grid=(nr // rows,),
        in_specs=[heads, heads, tok, tok, heads,
                  pl.BlockSpec((rows, GRID_W, D_MODEL), lambda i: (i, 0, 0)),
                  pl.BlockSpec((rows, 1, D_MODEL // 2), lambda i: (i, 0, 0)),
                  pl.BlockSpec((1, GRID_W, D_MODEL // 2), lambda i: (0, 0, 0)),
                  full(nw), full(sk), full(wout), full(gg1), full(g2), full(sh2), full(wr), full(br)],
        out_specs=[pl.BlockSpec((tm, D_MODEL), lambda i: (i, 0)),
                   pl.BlockSpec((tm, D_MODEL), lambda i: (i, 0)),
                   pl.BlockSpec((32, tm), lambda i: (0, i))],
        out_shape=[jax.ShapeDtypeStruct((t, D_MODEL), F32),
                   jax.ShapeDtypeStruct((t, D_MODEL), BF16),
                   jax.ShapeDtypeStruct((32, t), F32)],
        compiler_params=_cparams("parallel"),
        name="merge",
    )(hf, hb, act, z, yf, x3, er3, ec3, nw, sk, wout, gg1, g2, sh2, wr, br)


def _route_kernel(lg_ref, pos_ref, w_ref, cnt_ref, *, sbk):
    lg = lg_ref[...]
    tl = lg.shape[1]
    g = [lg[j:j + 1] for j in range(N_GROUPS)]
    e = [lg[N_GROUPS + j:N_GROUPS + j + 1] for j in range(N_EXPERTS)]
    gmax = jnp.maximum(jnp.maximum(g[0], g[1]), jnp.maximum(g[2], g[3]))
    den = jnp.exp(g[0] - gmax) + jnp.exp(g[1] - gmax) + jnp.exp(g[2] - gmax) + jnp.exp(g[3] - gmax)
    p_sel = 1.0 / den
    sel = []
    free = jnp.ones((1, tl), F32)
    for j in range(N_GROUPS):
        s = jnp.where(g[j] >= gmax, free, 0.0)
        sel.append(s)
        free = free - s
    es = []
    for j in range(EPG):
        es.append(sel[0] * e[j] + sel[1] * e[EPG + j] + sel[2] * e[2 * EPG + j] + sel[3] * e[3 * EPG + j])
    rank = []
    for j in range(EPG):
        rj = jnp.zeros((1, tl), F32)
        for i in range(EPG):
            if i == j:
                continue
            beats = (es[i] >= es[j]) if i < j else (es[i] > es[j])
            rj = rj + jnp.where(beats, 1.0, 0.0)
        rank.append(rj)
    v1 = jnp.maximum(jnp.maximum(es[0], es[1]), jnp.maximum(es[2], es[3]))
    v2 = sum(jnp.where(rank[j] == 1.0, es[j], 0.0) for j in range(EPG))
    tt = jnp.exp(v2 - v1)
    w1 = p_sel / (1.0 + tt)
    w2 = w1 * tt
    w = [jnp.where(rank[j] == 0.0, w1, jnp.where(rank[j] == 1.0, w2, 0.0)) for j in range(EPG)]
    top2 = [jnp.where(rank[j] < 2.0, 1.0, 0.0) for j in range(EPG)]
    mem = jnp.concatenate([sel[gi] * top2[j] for gi in range(N_GROUPS) for j in range(EPG)], axis=0)
    wts = jnp.concatenate([sel[gi] * w[j] for gi in range(N_GROUPS) for j in range(EPG)], axis=0)
    w_ref[...] = wts
    lane = lax.broadcasted_iota(jnp.int32, (N_EXPERTS, tl), 1) & (sbk - 1)
    c = mem
    k = 1
    while k < sbk:
        c = c + jnp.where(lane >= k, pltpu.roll(c, k, 1), 0.0)
        k *= 2
    pos_ref[...] = jnp.where(mem > 0.0, c - 1.0, -1.0)
    lane128 = lax.broadcasted_iota(jnp.int32, (N_EXPERTS, LANES), 1)
    cnt = jnp.zeros((N_EXPERTS, LANES), F32)
    for kb in range(tl // sbk):
        tot = jnp.sum(mem[:, kb * sbk:(kb + 1) * sbk], axis=1, keepdims=True)
        cnt = cnt + jnp.where(lane128 == kb, tot, 0.0)
    cnt_ref[...] = cnt


def _route(lg, *, tl, sbk):
    t = lg.shape[1]
    row = pl.BlockSpec((N_EXPERTS, tl), lambda i: (0, i))
    return pl.pallas_call(
        functools.partial(_route_kernel, sbk=sbk),
        grid=(t // tl,),
        in_specs=[pl.BlockSpec((32, tl), lambda i: (0, i))],
        out_specs=[row, row, pl.BlockSpec((N_EXPERTS, LANES), lambda i: (i, 0))],
        out_shape=[jax.ShapeDtypeStruct((N_EXPERTS, t), F32),
                   jax.ShapeDtypeStruct((N_EXPERTS, t), F32),
                   jax.ShapeDtypeStruct((t // tl * N_EXPERTS, LANES), F32)],
        compiler_params=_cparams("parallel"),
        name="route",
    )(lg)


def _slots_kernel(tab_ref, pos_ref, w_ref, qrow_ref, qcol_ref, *, off_r, off_o):
    i = pl.program_id(0)
    tl = pos_ref.shape[1]
    q0 = jnp.full((1, tl), -1.0, F32)
    q1 = jnp.full((1, tl), -1.0, F32)
    w0 = jnp.zeros((1, tl), F32)
    w1 = jnp.zeros((1, tl), F32)
    seen = jnp.zeros((1, tl), F32)
    for ex in range(N_EXPERTS):
        rk = pos_ref[ex:ex + 1, :]
        wt = w_ref[ex:ex + 1, :]
        rows = tab_ref[off_r + i * N_EXPERTS + ex].astype(F32)
        first_row = tab_ref[off_o + i * N_EXPERTS + ex].astype(F32)
        m = jnp.where(rk >= 0.0, 1.0, 0.0)
        val = jnp.where(rk < rows, rk + first_row, -1.0)
        first = (m * (1.0 - seen)) > 0.0
        second = (m * seen) > 0.0
        q0 = jnp.where(first, val, q0)
        w0 = jnp.where(first, wt, w0)
        q1 = jnp.where(second, val, q1)
        w1 = jnp.where(second, wt, w1)
        seen = seen + m
    qrow_ref[...] = jnp.concatenate([q0, q1, jnp.zeros((6, tl), F32)], axis=0)
    qcol_ref[...] = jnp.concatenate([q0, q1, w0, w1, jnp.zeros((LANES - 4, tl), F32)], axis=0).T


def _slots(tab, pos, w, *, tl, off_r, off_o):
    t = pos.shape[1]
    row = pl.BlockSpec((N_EXPERTS, tl), lambda i, c: (0, i))
    return pl.pallas_call(
        functools.partial(_slots_kernel, off_r=off_r, off_o=off_o),
        grid_spec=pltpu.PrefetchScalarGridSpec(
            num_scalar_prefetch=1,
            grid=(t // tl,),
            in_specs=[row, row],
            out_specs=[pl.BlockSpec((8, tl), lambda i, c: (0, i)),
                       pl.BlockSpec((tl, LANES), lambda i, c: (i, 0))]),
        out_shape=[jax.ShapeDtypeStruct((8, t), F32),
                   jax.ShapeDtypeStruct((t, LANES), F32)],
        compiler_params=_cparams("parallel"),
        name="slots",
    )(tab, pos, w)


def _mlp(x, wg, wu, wd):
    gt = _dot(x, wg)
    a = ((gt * _sigmoid(gt)) * _dot(x, wu)).astype(BF16)
    return _dot(a, wd).astype(BF16)


def _moe_kernel(tab_ref, h_ref, qrow_ref, pos_ref, w_ref, wg_ref, wu_ref, wd_ref,
                qcol_ref, x1_ref, gg2_ref, o_ref, xs_ref, ovf_ref, *, sbk, slots, csb, off_r, off_o, off_f):
    i = pl.program_id(0)
    step = pl.program_id(1)
    nsb = h_ref.shape[0] // sbk
    eps = wg_ref.shape[0]
    exp_steps = N_EXPERTS // eps

    @pl.when(step == 0)
    def _():
        def zero(sb, carry):
            ovf_ref[pl.ds(pl.multiple_of(sb * sbk, sbk), sbk), :] = jnp.zeros((sbk, D_MODEL), ovf_ref.dtype)
            return carry

        lax.fori_loop(0, nsb, zero, 0)
        pid = lax.broadcasted_iota(jnp.int32, (slots, sbk), 0).astype(F32)

        def select(sb, carry):
            q0 = qrow_ref[0, pl.ds(sb, 1), :]
            q1 = qrow_ref[1, pl.ds(sb, 1), :]
            s = jnp.where(q0 == pid, 1.0, jnp.where(q1 == pid, 1.0, 0.0)).astype(BF16)
            row0 = pl.multiple_of(sb * sbk, sbk)
            xs_ref[sb] = _dot(s, h_ref[pl.ds(row0, sbk), :]).astype(BF16)
            return carry

        lax.fori_loop(0, nsb, select, 0)

    def run_rows(k, first, n):
        first = pl.multiple_of(first, 16)
        x = jnp.concatenate([xs_ref[sb, pl.ds(first, n), :] for sb in range(nsb)], axis=0)
        y = _mlp(x, wg_ref[k], wu_ref[k], wd_ref[k])
        for sb in range(nsb):
            xs_ref[sb, pl.ds(first, n), :] = y[sb * n:(sb + 1) * n]

    @pl.when(step < exp_steps)
    def _():
        def expert(k, carry):
            e = step * eps + k
            n_rows = tab_ref[off_r + i * N_EXPERTS + e]
            row_a = tab_ref[off_o + i * N_EXPERTS + e]
            over = tab_ref[off_f + i * N_EXPERTS + e]
            n_pairs = n_rows // 32

            def pair(t, c1):
                run_rows(k, row_a + t * 32, 32)
                return c1

            lax.fori_loop(0, n_pairs, pair, 0)

            @pl.when(n_rows % 32 != 0)
            def _():
                run_rows(k, row_a + n_pairs * 32, 16)

            @pl.when(over > 0)
            def _():
                rid = lax.broadcasted_iota(jnp.int32, (LANES, sbk), 0).astype(F32)
                wg, wu, wd = wg_ref[k], wu_ref[k], wd_ref[k]

                def sub_block(sb, c1):
                    n_over = jnp.maximum(tab_ref[(i * nsb + sb) * N_EXPERTS + e] - n_rows, 0)
                    row0 = pl.multiple_of(sb * sbk, sbk)

                    def one_pass(s, c2):
                        base = (n_rows + s * LANES).astype(F32)
                        hit = pos_ref[pl.ds(e, 1), pl.ds(sb, 1), :].reshape(1, sbk) == (rid + base)
                        sel = jnp.where(hit, 1.0, 0.0).astype(BF16)
                        yo = _mlp(_dot(sel, h_ref[pl.ds(row0, sbk), :]).astype(BF16), wg, wu, wd)
                        wct = jnp.where(hit, w_ref[pl.ds(e, 1), pl.ds(sb, 1), :].reshape(1, sbk), 0.0).T
                        acc = ovf_ref[pl.ds(row0, sbk), :].astype(F32) + _dot(wct.astype(BF16), yo)
                        ovf_ref[pl.ds(row0, sbk), :] = acc.astype(ovf_ref.dtype)
                        return c2

                    return lax.fori_loop(0, (n_over + LANES - 1) // LANES, one_pass, c1)

                lax.fori_loop(0, nsb, sub_block, 0)

            return carry

        lax.fori_loop(0, eps, expert, 0)

    @pl.when(step >= exp_steps)
    def _():
        lane = lax.broadcasted_iota(jnp.int32, (sbk, slots), 1).astype(F32)
        for k in range(csb):
            sb = (step - exp_steps) * csb + k
            rows = slice(k * sbk, (k + 1) * sbk)
            qc = qcol_ref[rows, :]
            wmat = (jnp.where(lane == qc[:, 0:1], qc[:, 2:3], 0.0)
                    + jnp.where(lane == qc[:, 1:2], qc[:, 3:4], 0.0)).astype(BF16)
            y = _dot(wmat, xs_ref[sb]) + ovf_ref[pl.ds(pl.multiple_of(sb * sbk, sbk), sbk), :].astype(F32)
            o_ref[rows, :] = x1_ref[rows, :] + _rms(y) * gg2_ref[...]


def _moe(tab, h2, qrow3, pos3, w3, wg, wu, wd, qcol, x1, gg2, *, tb, sbk, slots, eps, csb, off_r, off_o, off_f):
    t = h2.shape[0]
    nsb = tb // sbk
    comb_steps = nsb // csb
    exp_steps = N_EXPERTS // eps
    wblk = lambda i, s, c: (jnp.minimum(s, exp_steps - 1), 0, 0)
    oblk = lambda i, s, c: (i * comb_steps + jnp.maximum(s - exp_steps, 0), 0)
    r3 = pl.BlockSpec((N_EXPERTS, nsb, sbk), lambda i, s, c: (0, i, 0))
    return pl.pallas_call(
        functools.partial(_moe_kernel, sbk=sbk, slots=slots, csb=csb,
                          off_r=off_r, off_o=off_o, off_f=off_f),
        grid_spec=pltpu.PrefetchScalarGridSpec(
            num_scalar_prefetch=1,
            grid=(t // tb, exp_steps + comb_steps),
            in_specs=[pl.BlockSpec((tb, D_MODEL), lambda i, s, c: (i, 0)),
                      pl.BlockSpec((8, nsb, sbk), lambda i, s, c: (0, i, 0)),
                      r3, r3,
                      pl.BlockSpec((eps, D_MODEL, D_EXPERT), wblk),
                      pl.BlockSpec((eps, D_MODEL, D_EXPERT), wblk),
                      pl.BlockSpec((eps, D_EXPERT, D_MODEL), wblk),
                      pl.BlockSpec((csb * sbk, LANES), oblk),
                      pl.BlockSpec((csb * sbk, D_MODEL), oblk),
                      pl.BlockSpec((1, D_MODEL), lambda i, s, c: (0, 0))],
            out_specs=pl.BlockSpec((csb * sbk, D_MODEL), oblk),
            scratch_shapes=[pltpu.VMEM((nsb, slots, D_MODEL), BF16),
                            pltpu.VMEM((tb, D_MODEL), BF16)]),
        out_shape=jax.ShapeDtypeStruct((t, D_MODEL), F32),
        compiler_params=pltpu.CompilerParams(dimension_semantics=("parallel", "arbitrary"),
                                             vmem_limit_bytes=MOE_VMEM_LIMIT),
        name="moe",
    )(tab, h2, qrow3, pos3, w3, wg, wu, wd, qcol, x1, gg2)


def _slot_tables(cnt, *, slots):
    need = ((jnp.max(cnt, axis=1) + 15) // 16) * 16
    start = jnp.cumsum(need, axis=1) - need
    rows = jnp.clip(slots - start, 0, need)
    over = jnp.max(jnp.maximum(cnt - rows[:, None, :], 0), axis=1)
    return rows, start, over


def _pos_tables(rows):
    quarter = D_MODEL // 4
    freq = 1.0 / (POS_BASE ** (np.arange(quarter, dtype=np.float64) / quarter))
    r = np.arange(rows, dtype=np.float64)[:, None] * freq
    cl = np.arange(GRID_W, dtype=np.float64)[:, None] * freq
    er = np.concatenate([np.sin(r), np.cos(r)], axis=-1).astype(np.float32)
    ec = np.concatenate([np.sin(cl), np.cos(cl)], axis=-1).astype(np.float32)
    return jnp.asarray(er[:, None, :]), jnp.asarray(ec[None, :, :])


def _dft_tables(t):
    n = np.arange(CHUNK, dtype=np.int64)
    prod = n[:, None] * n[None, :]
    ang = (prod % CHUNK).astype(np.float64) * (2.0 * np.pi / CHUNK)
    c, s = np.cos(ang), np.sin(ang)
    f1 = np.concatenate([c, s], axis=0)
    a2 = np.concatenate([np.concatenate([c, -s], axis=1), np.concatenate([s, c], axis=1)], axis=0)
    cs = np.concatenate([c, -s], axis=0)
    angw = prod.astype(np.float64) * (2.0 * np.pi / t)
    f32 = lambda a: jnp.asarray(a.astype(np.float32))
    return f32(f1), f32(a2), f32(cs), f32(np.cos(angw)[:, :, None]), f32(np.sin(angw)[:, :, None])


def _blockdiag(w):
    n = w.shape[0]
    size = n * QKV_BLOCK
    spread = np.tile(np.eye(QKV_BLOCK, dtype=np.float32), (1, n))
    rows = jnp.dot(w.reshape(size, QKV_BLOCK), jnp.asarray(spread), precision=lax.Precision.HIGHEST)
    blk = np.arange(size) // QKV_BLOCK
    mask = (blk[:, None] == blk[None, :]).astype(np.float32)
    return rows * jnp.asarray(mask)


def _gate_weights(w_f, b_f, w_b, b_b):
    w = jnp.concatenate([w_f[:, :HEADS], w_b[:, :HEADS], w_f[:, HEADS:], w_b[:, HEADS:]], axis=1).T
    b = jnp.concatenate([b_f[:HEADS], b_b[:HEADS], b_f[HEADS:], b_b[HEADS:]])
    return w.astype(BF16), b[:, None]


def kernel(x, c, ctx, c_ctx, w_ada, b_ada, g_pre_mix, g_post_mix, g_pre_ffn, g_post_ffn,
           w_in, conv_w, conv_b, w_q, w_k, w_v, w_if_fwd, b_if_fwd, w_if_bwd, b_if_bwd,
           mlstm_norm_w, mlstm_skip, w_fourier, w_out, w_router_group, b_router_group,
           w_router_expert, b_router_expert, w_gate, w_up, w_down):
    t = x.shape[1]
    rows = t // GRID_W

    c8 = jnp.concatenate([c, c_ctx[None, :], jnp.zeros((6, D_MODEL), F32)], axis=0)
    mod = _ada(c8, w_ada[0], b_ada[0][None, :])
    shift1, scale1, gate1, shift2, scale2, gate2 = [mod[0:1, k * D_MODEL:(k + 1) * D_MODEL] for k in range(N_MOD)]
    shift1c, scale1c = mod[1:2, 0:D_MODEL], mod[1:2, D_MODEL:2 * D_MODEL]
    g1 = g_pre_mix[0][None, :] * (1.0 + scale1)
    g1c = g_pre_mix[0][None, :] * (1.0 + scale1c)
    gg1 = g_post_mix[0][None, :] * gate1
    g2 = g_pre_ffn[0][None, :] * (1.0 + scale2)
    gg2 = g_post_ffn[0][None, :] * gate2

    er3, ec3 = _pos_tables(rows)
    x3 = x.reshape(rows, GRID_W, D_MODEL)
    ctx3 = ctx.reshape(CTX_LEN // GRID_W, GRID_W, D_MODEL)
    w_in_bf = w_in[0].astype(BF16)

    xm_l, z_l, u_l = _inproj(x3, er3, ec3, g1, shift1, w_in_bf, rows=8, add_pos=True)
    xm_c, _, _ = _inproj(ctx3, er3, ec3, g1c, shift1c, w_in_bf, rows=CTX_LEN // GRID_W, add_pos=False)

    wq = _blockdiag(w_q[0]).astype(BF16)
    wkt = _blockdiag(w_k[0]).T.astype(BF16)
    wv = _blockdiag(w_v[0]).astype(BF16)
    wi, bi = _gate_weights(w_if_fwd[0], b_if_fwd[0], w_if_bwd[0], b_if_bwd[0])
    wiq, wik, wiv = wi[:, :D_MLSTM], wi[:, D_MLSTM:2 * D_MLSTM], wi[:, 2 * D_MLSTM:]
    cb = conv_b[0][None, :]
    q_l, kt_l, v_l, act_l, gp_l = _feat(xm_l, conv_w[0], cb, wq, wkt, wv, wiq, wik, wiv, bi, tm=512)
    q_c, kt_c, v_c, _, gp_c = _feat(xm_c, conv_w[0], cb, wq, wkt, wv, wiq, wik, wiv, bi, tm=CTX_LEN)

    dr_l, gc_l = _gates(gp_l, tl=2048)
    dr_c, gc_c = _gates(gp_c, tl=CTX_LEN)

    c0 = jnp.zeros((2 * HEADS, DH, 2 * DH), F32)
    m0 = jnp.zeros((2 * HEADS, 8, LANES), F32)
    kt_c, kt_l = kt_c.reshape(HEADS, DH, CTX_LEN), kt_l.reshape(HEADS, DH, t)
    c_ctx_fin, m_ctx_fin = _mlstm(q_c, kt_c, v_c, gc_c, dr_c, c0, m0, cps=CTX_LEN // CHUNK, emit=False)
    hf, hb, _, _, wg_bf, wu_bf, wd_bf = _mlstm(q_l, kt_l, v_l, gc_l, dr_l, c_ctx_fin, m_ctx_fin, cps=4, emit=True,
                                               casts=(w_gate[0], w_up[0], w_down[0]))

    f1, a2, cs, cw3, sw3 = _dft_tables(t)
    n1 = t // CHUNK
    yc, ys = _dft1(u_l.reshape(FGROUPS, n1, CHUNK, FCG), f1.astype(BF16), cw3, sw3)
    mix = jnp.einsum('kc,gcd->gkd', cs, w_fourier[0], precision=lax.Precision.HIGHEST)
    mix = (mix * float(1.0 / np.sqrt(float(t) * FCG))).astype(BF16)
    yf = _dft2(yc.reshape(FGROUPS, t, FCG), ys.reshape(FGROUPS, t, FCG), a2.astype(BF16), mix)
    yf = yf.reshape(FGROUPS, t, FCG)

    wr = jnp.concatenate([w_router_group[0], w_router_expert[0],
                          jnp.zeros((D_MODEL, LANES - N_GROUPS - N_EXPERTS), F32)], axis=1)
    br = jnp.concatenate([b_router_group[0], b_router_expert[0],
                          jnp.zeros((LANES - N_GROUPS - N_EXPERTS,), F32)])[None, :]
    x1, h2, lg = _merge(hf, hb, act_l, z_l, yf, x3, er3, ec3,
                        mlstm_norm_w[0][None, :], mlstm_skip[0][None, :], w_out[0].astype(BF16),
                        gg1, g2, shift2, wr.astype(BF16), br, rows=16)
    pos, wts, cnt = _route(lg, tl=MOE_TB, sbk=MOE_SBK)
    nblk, nsb = t // MOE_TB, MOE_TB // MOE_SBK
    cnt = cnt.reshape(nblk, N_EXPERTS, LANES)[:, :, :nsb]
    cnt = jnp.transpose(cnt, (0, 2, 1)).astype(jnp.int32)
    rows, start, over = _slot_tables(cnt, slots=MOE_SLOTS)
    tab = jnp.concatenate([a.reshape(-1) for a in (cnt, rows, start, over)])
    off_r = nblk * nsb * N_EXPERTS
    off_o = off_r + nblk * N_EXPERTS
    off_f = off_o + nblk * N_EXPERTS
    qrow, qcol = _slots(tab, pos, wts, tl=MOE_TB, off_r=off_r, off_o=off_o)
    sub = (t // MOE_SBK, MOE_SBK)
    out = _moe(tab, h2, qrow.reshape(8, *sub), pos.reshape(N_EXPERTS, *sub), wts.reshape(N_EXPERTS, *sub),
               wg_bf, wu_bf, wd_bf,
               qcol, x1, gg2, tb=MOE_TB, sbk=MOE_SBK, slots=MOE_SLOTS, eps=MOE_EPS, csb=MOE_CSB,
               off_r=off_r, off_o=off_o, off_f=off_f)
    return out[None]
```

```python
import functools

import numpy as np
import jax
import jax.numpy as jnp
from jax import lax
from jax.experimental import pallas as pl
from jax.experimental.pallas import tpu as pltpu

F32 = jnp.float32
BF16 = jnp.bfloat16

D_MODEL = 1024
SEQ = 16384
GRID_W = 64
CTX_LEN = 256
D_MLSTM = 512
HEADS = 4
DH = 128
QKV_BLOCK = 4
CONV_K = 3
CHUNK = 128
D_FOURIER = 512
FGROUPS = 4
FCG = 128
N_GROUPS = 4
EPG = 4
N_EXPERTS = 16
D_EXPERT = 512
N_MOD = 6
EPS = 1e-6
POS_BASE = 10000.0
LANES = 128
NEG_BIG = -3.0e38

VMEM_LIMIT = 52 * 1024 * 1024
MOE_TB = 2048
MOE_SBK = 256
MOE_SLOTS = 768


def _cparams(*sem):
    return pltpu.CompilerParams(dimension_semantics=sem, vmem_limit_bytes=VMEM_LIMIT)


def _dot(a, b):
    return jnp.dot(a, b, preferred_element_type=F32)


def _dot_nt(a, b):
    return lax.dot_general(a, b, (((1,), (1,)), ((), ())), preferred_element_type=F32)


def _split_bf16(a):
    hi = a.astype(BF16)
    lo = (a - hi.astype(F32)).astype(BF16)
    return hi, lo


def _dot3(a, b):
    a_hi, a_lo = _split_bf16(a)
    b_hi, b_lo = _split_bf16(b)
    return _dot(a_hi, b_hi) + (_dot(a_hi, b_lo) + _dot(a_lo, b_hi))


def _sigmoid(x):
    return 1.0 / (1.0 + jnp.exp(-x))


def _rms(x):
    return x * lax.rsqrt(jnp.mean(x * x, axis=-1, keepdims=True) + EPS)


def _ada_kernel(c_ref, w_ref, b_ref, o_ref):
    c = c_ref[...]
    s = c * _sigmoid(c)
    o_ref[...] = _dot3(s, w_ref[...]) + b_ref[...]


def _ada(c8, w, b):
    n = w.shape[1]
    tn = 768
    return pl.pallas_call(
        _ada_kernel,
        grid=(n // tn,),
        in_specs=[pl.BlockSpec((8, D_MODEL), lambda j: (0, 0)),
                  pl.BlockSpec((D_MODEL, tn), lambda j: (0, j)),
                  pl.BlockSpec((1, tn), lambda j: (0, j))],
        out_specs=pl.BlockSpec((8, tn), lambda j: (0, j)),
        out_shape=jax.ShapeDtypeStruct((8, n), F32),
        compiler_params=_cparams("parallel"),
        name="ada",
    )(c8, w, b)


def _add_pos(x3, er_ref, ec_ref):
    r = x3.shape[0]
    pr = jnp.broadcast_to(er_ref[...], (r, GRID_W, D_MODEL // 2))
    pc = jnp.broadcast_to(ec_ref[...], (r, GRID_W, D_MODEL // 2))
    return x3 + jnp.concatenate([pr, pc], axis=-1)


def _inproj_kernel(x_ref, er_ref, ec_ref, g_ref, sh_ref, w_ref, xm_ref, z_ref, u_ref, *, add_pos):
    x3 = x_ref[...]
    if add_pos:
        x3 = _add_pos(x3, er_ref, ec_ref)
    x = x3.reshape(x3.shape[0] * GRID_W, D_MODEL)
    h = _rms(x) * g_ref[...] + sh_ref[...]
    proj = _dot(h.astype(BF16), w_ref[...])
    xm_ref[...] = proj[:, :D_MLSTM].astype(BF16)
    z_ref[...] = proj[:, D_MLSTM:2 * D_MLSTM].astype(BF16)
    for g in range(FGROUPS):
        u_ref[g] = proj[:, 2 * D_MLSTM + g * FCG:2 * D_MLSTM + (g + 1) * FCG]


def _inproj(x3, er3, ec3, g_eff, shift, w_in, *, rows, add_pos):
    nr = x3.shape[0]
    t = nr * GRID_W
    tm = rows * GRID_W
    out = jax.ShapeDtypeStruct((t, D_MLSTM), BF16)
    ospec = pl.BlockSpec((tm, D_MLSTM), lambda i: (i, 0))
    vec = pl.BlockSpec((1, D_MODEL), lambda i: (0, 0))
    return pl.pallas_call(
        functools.partial(_inproj_kernel, add_pos=add_pos),
        grid=(nr // rows,),
        in_specs=[pl.BlockSpec((rows, GRID_W, D_MODEL), lambda i: (i, 0, 0)),
                  pl.BlockSpec((rows, 1, D_MODEL // 2), lambda i: (i, 0, 0)),
                  pl.BlockSpec((1, GRID_W, D_MODEL // 2), lambda i: (0, 0, 0)),
                  vec, vec,
                  pl.BlockSpec(w_in.shape, lambda i: (0, 0))],
        out_specs=[ospec, ospec, pl.BlockSpec((FGROUPS, tm, FCG), lambda i: (0, i, 0))],
        out_shape=[out, out, jax.ShapeDtypeStruct((FGROUPS, t, FCG), F32)],
        compiler_params=_cparams("parallel"),
        name="inproj",
    )(x3, er3, ec3, g_eff, shift, w_in)


def _feat_kernel(xm_ref, prev_ref, next_ref, cw_ref, cb_ref, wq_ref, wkt_ref, wv_ref,
                 wiq_ref, wik_ref, wiv_ref, bi_ref,
                 q_ref, kt_ref, v_ref, act_ref, g_ref):
    i = pl.program_id(0)
    n = pl.num_programs(0)
    xm_bf = xm_ref[...]
    xm = xm_bf.astype(F32)
    tm = xm.shape[0]
    prev_row = prev_ref[...].astype(F32)[15:16, :] * jnp.where(i > 0, 1.0, 0.0)
    next_row = next_ref[...].astype(F32)[0:1, :] * jnp.where(i < n - 1, 1.0, 0.0)
    rid = lax.broadcasted_iota(jnp.int32, (tm, 1), 0)
    x_left = jnp.where(rid == 0, prev_row, pltpu.roll(xm, 1, 0))
    x_right = jnp.where(rid == tm - 1, next_row, pltpu.roll(xm, tm - 1, 0))
    cw = cw_ref[...]
    y = cw[0:1] * x_left + cw[1:2] * xm + cw[2:3] * x_right + cb_ref[...]
    act = (y * _sigmoid(y)).astype(BF16)
    act_ref[...] = act
    q = _dot(act, wq_ref[...])
    kt = _dot_nt(wkt_ref[...], act)
    v = _dot(xm_bf, wv_ref[...])
    q_bf = q.astype(BF16)
    kt_bf = kt.astype(BF16)
    v_bf = v.astype(BF16)
    q_s = (q * (DH ** -0.5)).astype(BF16)
    for hd in range(HEADS):
        q_ref[hd] = q_s[:, hd * DH:(hd + 1) * DH]
        v_ref[hd] = v_bf[:, hd * DH:(hd + 1) * DH]
    kt_ref[...] = kt_bf
    g = _dot_nt(wiq_ref[...], q_bf) + _dot(wik_ref[...], kt_bf) + _dot_nt(wiv_ref[...], v_bf)
    g_ref[...] = g + bi_ref[...]


def _feat(xm, conv_w, conv_b, wq, wkt, wv, wiq, wik, wiv, bi, *, tm):
    t = xm.shape[0]
    nb16 = t // 16
    k16 = tm // 16
    full = lambda a: pl.BlockSpec(a.shape, lambda i: (0,) * a.ndim)
    tok = pl.BlockSpec((tm, D_MLSTM), lambda i: (i, 0))
    heads = pl.BlockSpec((HEADS, tm, DH), lambda i: (0, i, 0))
    return pl.pallas_call(
        _feat_kernel,
        grid=(t // tm,),
        in_specs=[tok,
                  pl.BlockSpec((16, D_MLSTM), lambda i: (jnp.maximum(i * k16 - 1, 0), 0)),
                  pl.BlockSpec((16, D_MLSTM), lambda i: (jnp.minimum((i + 1) * k16, nb16 - 1), 0)),
                  full(conv_w), full(conv_b), full(wq), full(wkt), full(wv),
                  full(wiq), full(wik), full(wiv), full(bi)],
        out_specs=[heads,
                   pl.BlockSpec((D_MLSTM, tm), lambda i: (0, i)),
                   heads, tok,
                   pl.BlockSpec((16, tm), lambda i: (0, i))],
        out_shape=[jax.ShapeDtypeStruct((HEADS, t, DH), BF16),
                   jax.ShapeDtypeStruct((D_MLSTM, t), BF16),
                   jax.ShapeDtypeStruct((HEADS, t, DH), BF16),
                   jax.ShapeDtypeStruct((t, D_MLSTM), BF16),
                   jax.ShapeDtypeStruct((16, t), F32)],
        compiler_params=_cparams("parallel"),
        name="feat",
    )(xm, xm, xm, conv_w, conv_b, wq, wkt, wv, wiq, wik, wiv, bi)


def _gates_kernel(g_ref, d_ref, gc_ref):
    g = g_ref[...]
    tl = g.shape[1]
    ig = g[0:8]
    fg = g[8:16]
    lf = jnp.minimum(fg, 0.0) - jnp.log(1.0 + jnp.exp(-jnp.abs(fg)))
    pos = lax.broadcasted_iota(jnp.int32, (8, tl), 1) & (CHUNK - 1)
    is_fwd = lax.broadcasted_iota(jnp.int32, (8, tl), 0) < HEADS

    def scan(x, op, ident):
        xf = x
        xb = x
        k = 1
        while k < CHUNK:
            xf = op(xf, jnp.where(pos >= k, pltpu.roll(xf, k, 1), ident))
            xb = op(xb, jnp.where(pos < CHUNK - k, pltpu.roll(xb, tl - k, 1), ident))
            k *= 2
        return jnp.where(is_fwd, xf, xb)

    b = scan(lf, jnp.add, 0.0)
    d = ig - b
    mloc = b + scan(d, jnp.maximum, NEG_BIG)
    d_ref[...] = d
    stack = jnp.concatenate([b, mloc, jnp.zeros((LANES - 16, tl), F32)], axis=0)
    gc_ref[...] = stack.T


def _gates(g, *, tl):
    t = g.shape[1]
    return pl.pallas_call(
        _gates_kernel,
        grid=(t // tl,),
        in_specs=[pl.BlockSpec((16, tl), lambda i: (0, i))],
        out_specs=[pl.BlockSpec((8, tl), lambda i: (0, i)),
                   pl.BlockSpec((tl, LANES), lambda i: (i, 0))],
        out_shape=[jax.ShapeDtypeStruct((8, t), F32),
                   jax.ShapeDtypeStruct((t, LANES), F32)],
        compiler_params=_cparams("parallel"),
        name="gates",
    )(g)


def _mlstm_kernel(*refs, cps, emit, ncast):
    (qf_ref, kf_ref, vf_ref, gcf_ref, drf_ref,
     qb_ref, kb_ref, vb_ref, gcb_ref, drb_ref, c0_ref, m0_ref) = refs[:12]
    cast_in = refs[12:12 + ncast]
    outs = refs[12 + ncast:]
    if emit:
        hf_ref, hb_ref, cfin_ref, mfin_ref = outs[:4]
        outs = outs[4:]
    else:
        cfin_ref, mfin_ref = outs[:2]
        outs = outs[2:]
        hf_ref = hb_ref = None
    cast_out = outs[:ncast]
    c_scr, m_scr = outs[ncast:]
    step = pl.program_id(0)

    for src, dst in zip(cast_in, cast_out):
        dst[...] = src[...].astype(BF16)

    @pl.when(step == 0)
    def _():
        c_scr[...] = c0_ref[...]
        m_scr[...] = m0_ref[...]

    ti = lax.broadcasted_iota(jnp.int32, (CHUNK, CHUNK), 0)
    si = lax.broadcasted_iota(jnp.int32, (CHUNK, CHUNK), 1)
    ones_blk = jnp.ones((HEADS, CHUNK, CHUNK), BF16)

    def bdot(a, b):
        return lax.dot_general(a, b, (((2,), (1,)), ((0,), (0,))), preferred_element_type=F32)

    for j in range(cps):
        for d in range(2):
            if d == 0:
                q_ref, k_ref, v_ref, gc_ref, dr_ref, h_ref = qf_ref, kf_ref, vf_ref, gcf_ref, drf_ref, hf_ref
                r0, mask, last = j * CHUNK, si <= ti, CHUNK - 1
            else:
                q_ref, k_ref, v_ref, gc_ref, dr_ref, h_ref = qb_ref, kb_ref, vb_ref, gcb_ref, drb_ref, hb_ref
                r0, mask, last = (cps - 1 - j) * CHUNK, si >= ti, 0
            rows = slice(r0, r0 + CHUNK)
            gc = gc_ref[rows, :]
            dr = dr_ref[:, rows]
            hs = range(d * HEADS, (d + 1) * HEADS)
            b = jnp.stack([gc[:, r:r + 1] for r in hs])
            mloc = jnp.stack([gc[:, 8 + r:9 + r] for r in hs])
            drow = jnp.stack([dr[r:r + 1, :] for r in hs])
            btot = jnp.stack([gc[last:last + 1, r:r + 1] for r in hs])
            amax = jnp.stack([gc[last:last + 1, 8 + r:9 + r] for r in hs])
            m0 = jnp.stack([m_scr[r][0:1, 0:1] for r in hs])
            c_aug = c_scr[d * HEADS:(d + 1) * HEADS]
            kt = k_ref[:, :, rows]
            vaug = jnp.concatenate([v_ref[:, rows, :], ones_blk], axis=2)
            if emit:
                qh = q_ref[:, rows, :]
                bm = b + m0
                m_t = jnp.maximum(bm, mloc)
                dmat = jnp.where(mask[None], jnp.exp((b - m_t) + drow), 0.0)
                smat = (bdot(qh, kt) * dmat).astype(BF16)
                q_in = (qh.astype(F32) * jnp.exp(bm - m_t)).astype(BF16)
                num = bdot(jnp.concatenate([smat, q_in], axis=2),
                           jnp.concatenate([vaug, c_aug.astype(BF16)], axis=1))
                den = jnp.maximum(jnp.abs(num[:, :, DH:]), jnp.exp(-m_t))
                h_ref[:, rows, :] = (num[:, :, :DH] / den).astype(h_ref.dtype)
            m_new = jnp.maximum(btot + m0, amax)
            decay = jnp.exp(btot + m0 - m_new)
            kw = (kt.astype(F32) * jnp.exp(btot + drow - m_new)).astype(BF16)
            c_scr[d * HEADS:(d + 1) * HEADS] = decay * c_aug + bdot(kw, vaug)
            m_scr[d * HEADS:(d + 1) * HEADS] = jnp.broadcast_to(m_new, (HEADS, 8, LANES))

    @pl.when(step == pl.num_programs(0) - 1)
    def _():
        cfin_ref[...] = c_scr[...]
        mfin_ref[...] = m_scr[...]


def _mlstm(q, kt, v, gc, dr, c0, m0, *, cps, emit, casts=()):
    t = q.shape[1]
    cb = cps * CHUNK
    nb = t // cb
    fwd_r = lambda i: (i, 0)
    bwd_r = lambda i: (nb - 1 - i, 0)
    fwd_c = lambda i: (0, i)
    bwd_c = lambda i: (0, nb - 1 - i)
    tok = lambda f: pl.BlockSpec((HEADS, cb, DH), lambda i, f=f: (0, f(i)[0], 0))
    in_specs = []
    for fr, fc in ((fwd_r, fwd_c), (bwd_r, bwd_c)):
        in_specs += [tok(fr), pl.BlockSpec((HEADS, DH, cb), lambda i, fc=fc: (0, 0, fc(i)[1])), tok(fr),
                     pl.BlockSpec((cb, LANES), fr), pl.BlockSpec((8, cb), fc)]
    cshape = (2 * HEADS, DH, 2 * DH)
    mshape = (2 * HEADS, 8, LANES)
    cspec = pl.BlockSpec(cshape, lambda i: (0, 0, 0))
    mspec = pl.BlockSpec(mshape, lambda i: (0, 0, 0))
    in_specs += [cspec, mspec]
    out_specs = [cspec, mspec]
    out_shape = [jax.ShapeDtypeStruct(cshape, F32), jax.ShapeDtypeStruct(mshape, F32)]
    if emit:
        out_specs = [tok(fwd_r), tok(bwd_r)] + out_specs
        out_shape = [jax.ShapeDtypeStruct((HEADS, t, DH), BF16)] * 2 + out_shape
    for a in casts:
        per = nb // a.shape[0]
        spec = pl.BlockSpec((1, a.shape[1] // per, a.shape[2]), lambda i, per=per: (i // per, i % per, 0))
        in_specs.append(spec)
        out_specs.append(spec)
        out_shape.append(jax.ShapeDtypeStruct(a.shape, BF16))
    return pl.pallas_call(
        functools.partial(_mlstm_kernel, cps=cps, emit=emit, ncast=len(casts)),
        grid=(nb,),
        in_specs=in_specs,
        out_specs=out_specs,
        out_shape=out_shape,
        scratch_shapes=[pltpu.VMEM(cshape, F32), pltpu.VMEM(mshape, F32)],
        compiler_params=_cparams("arbitrary"),
        name="mlstm",
    )(q, kt, v, gc, dr, q, kt, v, gc, dr, c0, m0, *casts)


FFT_ROWS = 8


def _dft1_kernel(u_ref, f_ref, cw_ref, sw_ref, yc_ref, ys_ref):
    f = f_ref[...]
    rows = CHUNK * FFT_ROWS
    u2 = u_ref.reshape(FGROUPS * rows, FCG)
    yc2 = yc_ref.reshape(FGROUPS * rows, FCG)
    ys2 = ys_ref.reshape(FGROUPS * rows, FCG)
    for s in range(FFT_ROWS):
        pick = [pl.ds(g * rows + s, CHUNK, stride=FFT_ROWS) for g in range(FGROUPS)]
        x = jnp.concatenate([u2[p, :] for p in pick], axis=1).astype(BF16)
        y = _dot(f, x)
        cw = cw_ref[s]
        sw = sw_ref[s]
        pr = y[:CHUNK] * cw - y[CHUNK:] * sw
        pi = y[:CHUNK] * sw + y[CHUNK:] * cw
        for g in range(FGROUPS):
            yc2[pick[g], :] = pr[:, g * FCG:(g + 1) * FCG]
            ys2[pick[g], :] = pi[:, g * FCG:(g + 1) * FCG]


def _dft1(u4, f1, cw3, sw3):
    blk = pl.BlockSpec((FGROUPS, CHUNK, FFT_ROWS, FCG), lambda j: (0, 0, j, 0))
    tw = pl.BlockSpec((FFT_ROWS, CHUNK, 1), lambda j: (j, 0, 0))
    out = jax.ShapeDtypeStruct(u4.shape, F32)
    return pl.pallas_call(
        _dft1_kernel,
        grid=(u4.shape[2] // FFT_ROWS,),
        in_specs=[blk, pl.BlockSpec(f1.shape, lambda j: (0, 0)), tw, tw],
        out_specs=[blk, blk],
        out_shape=[out, out],
        compiler_params=_cparams("parallel"),
        name="dft1",
    )(u4, f1, cw3, sw3)


def _dft2_kernel(yc_ref, ys_ref, a2_ref, mix_ref, o_ref):
    a2 = a2_ref[...]
    rows = CHUNK * FFT_ROWS
    o2 = o_ref.reshape(FGROUPS * rows, FCG)
    for kk in range(FFT_ROWS):
        blk = slice(kk * CHUNK, (kk + 1) * CHUNK)
        yc = jnp.concatenate([yc_ref[g, blk, :] for g in range(FGROUPS)], axis=1)
        ys = jnp.concatenate([ys_ref[g, blk, :] for g in range(FGROUPS)], axis=1)
        p = jnp.concatenate([yc, ys], axis=0).astype(BF16)
        x = _dot(a2, p).astype(BF16)
        for g in range(FGROUPS):
            cols = slice(g * FCG, (g + 1) * FCG)
            cat = jnp.concatenate([x[:CHUNK, cols], x[CHUNK:, cols]], axis=1)
            o2[pl.ds(g * rows + kk, CHUNK, stride=FFT_ROWS), :] = _dot(cat, mix_ref[g])


def _dft2(yc, ys, a2, mix):
    t = yc.shape[1]
    n1 = t // CHUNK
    tok = pl.BlockSpec((FGROUPS, FFT_ROWS * CHUNK, FCG), lambda i: (0, i, 0))
    full = lambda a: pl.BlockSpec(a.shape, lambda i: (0,) * a.ndim)
    return pl.pallas_call(
        _dft2_kernel,
        grid=(n1 // FFT_ROWS,),
        in_specs=[tok, tok, full(a2), full(mix)],
        out_specs=pl.BlockSpec((FGROUPS, CHUNK, FFT_ROWS, FCG), lambda i: (0, 0, i, 0)),
        out_shape=jax.ShapeDtypeStruct((FGROUPS, CHUNK, n1, FCG), F32),
        compiler_params=_cparams("parallel"),
        name="dft2",
    )(yc, ys, a2, mix)


def _merge_kernel(hf_ref, hb_ref, act_ref, z_ref, yf_ref, x_ref, er_ref, ec_ref,
                  nw_ref, sk_ref, wout_ref, gg1_ref, g2_ref, sh2_ref, wr_ref, br_ref,
                  x1_ref, h2_ref, lg_ref):
    parts = []
    for hd in range(HEADS):
        hh = hf_ref[hd].astype(F32) + hb_ref[hd].astype(F32)
        dl = hh - jnp.mean(hh, axis=-1, keepdims=True)
        var = jnp.mean(dl * dl, axis=-1, keepdims=True)
        parts.append(dl * lax.rsqrt(var + EPS))
    hn = jnp.concatenate(parts, axis=-1)
    z = z_ref[...].astype(F32)
    m = (hn * nw_ref[...] + sk_ref[...] * act_ref[...].astype(F32)) * (z * _sigmoid(z))
    cat = jnp.concatenate([m.astype(BF16)] + [yf_ref[g].astype(BF16) for g in range(FGROUPS)], axis=-1)
    y = _dot(cat, wout_ref[...])
    x3 = _add_pos(x_ref[...], er_ref, ec_ref)
    xp = x3.reshape(x3.shape[0] * GRID_W, D_MODEL)
    x1 = xp + _rms(y) * gg1_ref[...]
    x1_ref[...] = x1
    h2 = _rms(x1) * g2_ref[...] + sh2_ref[...]
    h2_ref[...] = h2.astype(BF16)
    lg = _dot(h2.astype(BF16), wr_ref[...]) + br_ref[...]
    lg_ref[...] = lg.T[:32]


def _merge(hf, hb, act, z, yf, x3, er3, ec3, nw, sk, wout, gg1, g2, sh2, wr, br, *, rows):
    nr = x3.shape[0]
    t = nr * GRID_W
    tm = rows * GRID_W
    tok = pl.BlockSpec((tm, D_MLSTM), lambda i: (i, 0))
    heads = pl.BlockSpec((HEADS, tm, DH), lambda i: (0, i, 0))
    full = lambda a: pl.BlockSpec(a.shape, lambda i: (0,) * a.ndim)
    return pl.pallas_call(
        _merge_kernel,
        grid=(nr // rows,),
        in_specs=[heads, heads, tok, tok, heads,
                  pl.BlockSpec((rows, GRID_W, D_MODEL), lambda i: (i, 0, 0)),
                  pl.BlockSpec((rows, 1, D_MODEL // 2), lambda i: (i, 0, 0)),
                  pl.BlockSpec((1, GRID_W, D_MODEL // 2), lambda i: (0, 0, 0)),
                  full(nw), full(sk), full(wout), full(gg1), full(g2), full(sh2), full(wr), full(br)],
        out_specs=[pl.BlockSpec((tm, D_MODEL), lambda i: (i, 0)),
                   pl.BlockSpec((tm, D_MODEL), lambda i: (i, 0)),
                   pl.BlockSpec((32, tm), lambda i: (0, i))],
        out_shape=[jax.ShapeDtypeStruct((t, D_MODEL), F32),
                   jax.ShapeDtypeStruct((t, D_MODEL), BF16),
                   jax.ShapeDtypeStruct((32, t), F32)],
        compiler_params=_cparams("parallel"),
        name="merge",
    )(hf, hb, act, z, yf, x3, er3, ec3, nw, sk, wout, gg1, g2, sh2, wr, br)


def _route_kernel(lg_ref, pos_ref, w_ref, cnt_ref, *, sbk):
    lg = lg_ref[...]
    tl = lg.shape[1]
    g = [lg[j:j + 1] for j in range(N_GROUPS)]
    e = [lg[N_GROUPS + j:N_GROUPS + j + 1] for j in range(N_EXPERTS)]
    gmax = jnp.maximum(jnp.maximum(g[0], g[1]), jnp.maximum(g[2], g[3]))
    den = jnp.exp(g[0] - gmax) + jnp.exp(g[1] - gmax) + jnp.exp(g[2] - gmax) + jnp.exp(g[3] - gmax)
    p_sel = 1.0 / den
    sel = []
    free = jnp.ones((1, tl), F32)
    for j in range(N_GROUPS):
        s = jnp.where(g[j] >= gmax, free, 0.0)
        sel.append(s)
        free = free - s
    es = []
    for j in range(EPG):
        es.append(sel[0] * e[j] + sel[1] * e[EPG + j] + sel[2] * e[2 * EPG + j] + sel[3] * e[3 * EPG + j])
    rank = []
    for j in range(EPG):
        rj = jnp.zeros((1, tl), F32)
        for i in range(EPG):
            if i == j:
                continue
            beats = (es[i] >= es[j]) if i < j else (es[i] > es[j])
            rj = rj + jnp.where(beats, 1.0, 0.0)
        rank.append(rj)
    v1 = jnp.maximum(jnp.maximum(es[0], es[1]), jnp.maximum(es[2], es[3]))
    v2 = sum(jnp.where(rank[j] == 1.0, es[j], 0.0) for j in range(EPG))
    tt = jnp.exp(v2 - v1)
    w1 = p_sel / (1.0 + tt)
    w2 = w1 * tt
    w = [jnp.where(rank[j] == 0.0, w1, jnp.where(rank[j] == 1.0, w2, 0.0)) for j in range(EPG)]
    top2 = [jnp.where(rank[j] < 2.0, 1.0, 0.0) for j in range(EPG)]
    mem = jnp.concatenate([sel[gi] * top2[j] for gi in range(N_GROUPS) for j in range(EPG)], axis=0)
    wts = jnp.concatenate([sel[gi] * w[j] for gi in range(N_GROUPS) for j in range(EPG)], axis=0)
    w_ref[...] = wts
    lane = lax.broadcasted_iota(jnp.int32, (N_EXPERTS, tl), 1) & (sbk - 1)
    c = mem
    k = 1
    while k < sbk:
        c = c + jnp.where(lane >= k, pltpu.roll(c, k, 1), 0.0)
        k *= 2
    pos_ref[...] = jnp.where(mem > 0.0, c - 1.0, -1.0)
    lane128 = lax.broadcasted_iota(jnp.int32, (N_EXPERTS, LANES), 1)
    cnt = jnp.zeros((N_EXPERTS, LANES), F32)
    for kb in range(tl // sbk):
        tot = jnp.sum(mem[:, kb * sbk:(kb + 1) * sbk], axis=1, keepdims=True)
        cnt = cnt + jnp.where(lane128 == kb, tot, 0.0)
    cnt_ref[...] = cnt


def _route(lg, *, tl, sbk):
    t = lg.shape[1]
    row = pl.BlockSpec((N_EXPERTS, tl), lambda i: (0, i))
    return pl.pallas_call(
        functools.partial(_route_kernel, sbk=sbk),
        grid=(t // tl,),
        in_specs=[pl.BlockSpec((32, tl), lambda i: (0, i))],
        out_specs=[row, row, pl.BlockSpec((N_EXPERTS, LANES), lambda i: (i, 0))],
        out_shape=[jax.ShapeDtypeStruct((N_EXPERTS, t), F32),
                   jax.ShapeDtypeStruct((N_EXPERTS, t), F32),
                   jax.ShapeDtypeStruct((t // tl * N_EXPERTS, LANES), F32)],
        compiler_params=_cparams("parallel"),
        name="route",
    )(lg)


def _slots_kernel(pos_ref, w_ref, first_ref, qrow_ref, qcol_ref):
    tl = pos_ref.shape[1]
    q0 = jnp.full((1, tl), -1.0, F32)
    q1 = jnp.full((1, tl), -1.0, F32)
    w0 = jnp.zeros((1, tl), F32)
    w1 = jnp.zeros((1, tl), F32)
    seen = jnp.zeros((1, tl), F32)
    for ex in range(N_EXPERTS):
        rk = pos_ref[ex:ex + 1, :]
        wt = w_ref[ex:ex + 1, :]
        m = jnp.where(rk >= 0.0, 1.0, 0.0)
        val = rk + first_ref[ex:ex + 1, :]
        first = (m * (1.0 - seen)) > 0.0
        second = (m * seen) > 0.0
        q0 = jnp.where(first, val, q0)
        w0 = jnp.where(first, wt, w0)
        q1 = jnp.where(second, val, q1)
        w1 = jnp.where(second, wt, w1)
        seen = seen + m
    qrow_ref[...] = jnp.concatenate([q0, q1, jnp.zeros((6, tl), F32)], axis=0)
    qcol_ref[...] = jnp.concatenate([q0, q1, w0, w1, jnp.zeros((LANES - 4, tl), F32)], axis=0).T


def _slots(pos, w, first, *, tl):
    t = pos.shape[1]
    row = pl.BlockSpec((N_EXPERTS, tl), lambda i: (0, i))
    return pl.pallas_call(
        _slots_kernel,
        grid=(t // tl,),
        in_specs=[row, row, row],
        out_specs=[pl.BlockSpec((8, tl), lambda i: (0, i)),
                   pl.BlockSpec((tl, LANES), lambda i: (i, 0))],
        out_shape=[jax.ShapeDtypeStruct((8, t), F32),
                   jax.ShapeDtypeStruct((t, LANES), F32)],
        compiler_params=_cparams("parallel"),
        name="slots",
    )(pos, w, first)


def _mlp(x, wg, wu, wd):
    gt = _dot(x, wg)
    a = ((gt * _sigmoid(gt)) * _dot(x, wu)).astype(BF16)
    return _dot(a, wd).astype(BF16)


def _dispatch_kernel(h_ref, qrow_ref, xs_ref, *, sbk, slots):
    pid = lax.broadcasted_iota(jnp.int32, (slots, sbk), 0).astype(F32)
    for s in range(xs_ref.shape[0]):
        q0 = qrow_ref[s, 0:1, :]
        q1 = qrow_ref[s, 1:2, :]
        sel = jnp.where(q0 == pid, 1.0, jnp.where(q1 == pid, 1.0, 0.0)).astype(BF16)
        xs_ref[s] = _dot(sel, h_ref[s * sbk:(s + 1) * sbk, :]).astype(BF16)


def _dispatch(h2, qrow_sb, *, sbk, slots, per_step):
    t = h2.shape[0]
    return pl.pallas_call(
        functools.partial(_dispatch_kernel, sbk=sbk, slots=slots),
        grid=(t // (per_step * sbk),),
        in_specs=[pl.BlockSpec((per_step * sbk, D_MODEL), lambda i: (i, 0)),
                  pl.BlockSpec((per_step, 8, sbk), lambda i: (i, 0, 0))],
        out_specs=pl.BlockSpec((per_step, slots, D_MODEL), lambda i: (i, 0, 0)),
        out_shape=jax.ShapeDtypeStruct((t // sbk, slots, D_MODEL), BF16),
        compiler_params=_cparams("parallel"),
        name="dispatch",
    )(h2, qrow_sb)


MOE_TILE = 16
MOE_BATCH = 32


def _experts_kernel(desc_ref, bexp_ref, nb_ref, xs_in, wg_ref, wu_ref, wd_ref, xs_out,
                    in_buf, out_buf, sem_in, sem_out):
    g = pl.program_id(0)
    nb = nb_ref[0]
    slot = g % 2

    def gather(b, sl, start):
        def body(k, carry):
            d = desc_ref[b * MOE_BATCH + k]

            @pl.when(d >= 0)
            def _():
                cp = pltpu.make_async_copy(xs_in.at[d], in_buf.at[sl, pl.ds(k * MOE_TILE, MOE_TILE)], sem_in.at[sl])
                if start:
                    cp.start()
                else:
                    cp.wait()
            return carry

        lax.fori_loop(0, MOE_BATCH, body, 0)

    def scatter(b, sl, start):
        def body(k, carry):
            d = desc_ref[b * MOE_BATCH + k]

            @pl.when(d >= 0)
            def _():
                cp = pltpu.make_async_copy(out_buf.at[sl, pl.ds(k * MOE_TILE, MOE_TILE)], xs_out.at[d], sem_out.at[sl])
                if start:
                    cp.start()
                else:
                    cp.wait()
            return carry

        lax.fori_loop(0, MOE_BATCH, body, 0)

    @pl.when(jnp.logical_and(g == 0, nb > 0))
    def _():
        gather(0, 0, True)

    @pl.when(g < nb)
    def _():
        gather(g, slot, False)

        @pl.when(g + 1 < nb)
        def _():
            gather(g + 1, 1 - slot, True)

        y = _mlp(in_buf[slot], wg_ref[0], wu_ref[0], wd_ref[0])

        @pl.when(g >= 2)
        def _():
            scatter(g - 2, slot, False)

        out_buf[slot] = y
        scatter(g, slot, True)

        @pl.when(g == nb - 1)
        def _():
            @pl.when(g >= 1)
            def _():
                scatter(g - 1, 1 - slot, False)

            scatter(g, slot, False)


def _experts(desc, bexp, nb, xs_tiles, wg, wu, wd, *, max_batches):
    rows = MOE_BATCH * MOE_TILE
    wblk = lambda g, d, e, n: (e[jnp.minimum(g, jnp.maximum(n[0] - 1, 0))], 0, 0)
    return pl.pallas_call(
        _experts_kernel,
        grid_spec=pltpu.PrefetchScalarGridSpec(
            num_scalar_prefetch=3,
            grid=(max_batches,),
            in_specs=[pl.BlockSpec(memory_space=pl.ANY),
                      pl.BlockSpec((1, D_MODEL, D_EXPERT), wblk),
                      pl.BlockSpec((1, D_MODEL, D_EXPERT), wblk),
                      pl.BlockSpec((1, D_EXPERT, D_MODEL), wblk)],
            out_specs=pl.BlockSpec(memory_space=pl.ANY),
            scratch_shapes=[pltpu.VMEM((2, rows, D_MODEL), BF16),
                            pltpu.VMEM((2, rows, D_MODEL), BF16),
                            pltpu.SemaphoreType.DMA((2,)),
                            pltpu.SemaphoreType.DMA((2,))]),
        out_shape=jax.ShapeDtypeStruct(xs_tiles.shape, BF16),
        input_output_aliases={3: 0},
        compiler_params=_cparams("arbitrary"),
        name="experts",
    )(desc, bexp, nb, xs_tiles, wg, wu, wd)


def _combine_kernel(ys_ref, qcol_ref, x1_ref, gg2_ref, o_ref, *, sbk, slots):
    lane = lax.broadcasted_iota(jnp.int32, (sbk, slots), 1).astype(F32)
    for s in range(ys_ref.shape[0]):
        rows = slice(s * sbk, (s + 1) * sbk)
        qc = qcol_ref[rows, :]
        wmat = (jnp.where(lane == qc[:, 0:1], qc[:, 2:3], 0.0)
                + jnp.where(lane == qc[:, 1:2], qc[:, 3:4], 0.0)).astype(BF16)
        y = _dot(wmat, ys_ref[s])
        o_ref[rows, :] = x1_ref[rows, :] + _rms(y) * gg2_ref[...]


def _combine(ys, qcol, x1, gg2, *, sbk, slots, per_step):
    t = x1.shape[0]
    tok = lambda n: pl.BlockSpec((per_step * sbk, n), lambda i: (i, 0))
    return pl.pallas_call(
        functools.partial(_combine_kernel, sbk=sbk, slots=slots),
        grid=(t // (per_step * sbk),),
        in_specs=[pl.BlockSpec((per_step, slots, D_MODEL), lambda i: (i, 0, 0)),
                  tok(LANES), tok(D_MODEL),
                  pl.BlockSpec((1, D_MODEL), lambda i: (0, 0))],
        out_specs=tok(D_MODEL),
        out_shape=jax.ShapeDtypeStruct((t, D_MODEL), F32),
        compiler_params=_cparams("parallel"),
        name="combine",
    )(ys, qcol, x1, gg2)


def _expert_tables(cnt, *, slots, max_batches):
    nsb = cnt.shape[0]
    tiles = (cnt + MOE_TILE - 1) // MOE_TILE
    first_tile = jnp.cumsum(tiles, axis=1) - tiles
    n = tiles.T.reshape(-1)
    base = (jnp.arange(nsb, dtype=jnp.int32)[:, None] * (slots // MOE_TILE) + first_tile).T.reshape(-1)
    per_expert = jnp.sum(tiles, axis=0)
    padded = ((per_expert + MOE_BATCH - 1) // MOE_BATCH) * MOE_BATCH
    expert_off = jnp.cumsum(padded) - padded
    chunk_off = (expert_off[:, None] + (jnp.cumsum(tiles.T, axis=1) - tiles.T)).reshape(-1)
    csum = jnp.cumsum(n)
    max_tiles = nsb * (slots // MOE_TILE)
    i = jnp.arange(max_tiles, dtype=jnp.int32)
    k = jnp.minimum(jnp.sum((csum[None, :] <= i[:, None]).astype(jnp.int32), axis=1), n.shape[0] - 1)
    j = i - (csum[k] - n[k])
    n_desc = max_batches * MOE_BATCH
    dest = jnp.where(i < csum[-1], chunk_off[k] + j, n_desc)
    desc = jnp.full((n_desc + 1,), -1, jnp.int32).at[dest].set(base[k] + j, mode='drop')[:n_desc]
    ends = jnp.cumsum(padded) // MOE_BATCH
    batch = jnp.arange(max_batches, dtype=jnp.int32)
    bexp = jnp.minimum(jnp.sum((ends[None, :] <= batch[:, None]).astype(jnp.int32), axis=1), N_EXPERTS - 1)
    return first_tile * MOE_TILE, desc, bexp, ends[-1:].astype(jnp.int32)


def _pos_tables(rows):
    quarter = D_MODEL // 4
    freq = 1.0 / (POS_BASE ** (np.arange(quarter, dtype=np.float64) / quarter))
    r = np.arange(rows, dtype=np.float64)[:, None] * freq
    cl = np.arange(GRID_W, dtype=np.float64)[:, None] * freq
    er = np.concatenate([np.sin(r), np.cos(r)], axis=-1).astype(np.float32)
    ec = np.concatenate([np.sin(cl), np.cos(cl)], axis=-1).astype(np.float32)
    return jnp.asarray(er[:, None, :]), jnp.asarray(ec[None, :, :])


def _dft_tables(t):
    n = np.arange(CHUNK, dtype=np.int64)
    prod = n[:, None] * n[None, :]
    ang = (prod % CHUNK).astype(np.float64) * (2.0 * np.pi / CHUNK)
    c, s = np.cos(ang), np.sin(ang)
    f1 = np.concatenate([c, s], axis=0)
    a2 = np.concatenate([np.concatenate([c, -s], axis=1), np.concatenate([s, c], axis=1)], axis=0)
    cs = np.concatenate([c, -s], axis=0)
    angw = prod.astype(np.float64) * (2.0 * np.pi / t)
    f32 = lambda a: jnp.asarray(a.astype(np.float32))
    return f32(f1), f32(a2), f32(cs), f32(np.cos(angw)[:, :, None]), f32(np.sin(angw)[:, :, None])


def _blockdiag(w):
    n = w.shape[0]
    size = n * QKV_BLOCK
    spread = np.tile(np.eye(QKV_BLOCK, dtype=np.float32), (1, n))
    rows = jnp.dot(w.reshape(size, QKV_BLOCK), jnp.asarray(spread), precision=lax.Precision.HIGHEST)
    blk = np.arange(size) // QKV_BLOCK
    mask = (blk[:, None] == blk[None, :]).astype(np.float32)
    return rows * jnp.asarray(mask)


def _gate_weights(w_f, b_f, w_b, b_b):
    w = jnp.concatenate([w_f[:, :HEADS], w_b[:, :HEADS], w_f[:, HEADS:], w_b[:, HEADS:]], axis=1).T
    b = jnp.concatenate([b_f[:HEADS], b_b[:HEADS], b_f[HEADS:], b_b[HEADS:]])
    return w.astype(BF16), b[:, None]


def kernel(x, c, ctx, c_ctx, w_ada, b_ada, g_pre_mix, g_post_mix, g_pre_ffn, g_post_ffn,
           w_in, conv_w, conv_b, w_q, w_k, w_v, w_if_fwd, b_if_fwd, w_if_bwd, b_if_bwd,
           mlstm_norm_w, mlstm_skip, w_fourier, w_out, w_router_group, b_router_group,
           w_router_expert, b_router_expert, w_gate, w_up, w_down):
    t = x.shape[1]
    rows = t // GRID_W

    c8 = jnp.concatenate([c, c_ctx[None, :], jnp.zeros((6, D_MODEL), F32)], axis=0)
    mod = _ada(c8, w_ada[0], b_ada[0][None, :])
    shift1, scale1, gate1, shift2, scale2, gate2 = [mod[0:1, k * D_MODEL:(k + 1) * D_MODEL] for k in range(N_MOD)]
    shift1c, scale1c = mod[1:2, 0:D_MODEL], mod[1:2, D_MODEL:2 * D_MODEL]
    g1 = g_pre_mix[0][None, :] * (1.0 + scale1)
    g1c = g_pre_mix[0][None, :] * (1.0 + scale1c)
    gg1 = g_post_mix[0][None, :] * gate1
    g2 = g_pre_ffn[0][None, :] * (1.0 + scale2)
    gg2 = g_post_ffn[0][None, :] * gate2

    er3, ec3 = _pos_tables(rows)
    x3 = x.reshape(rows, GRID_W, D_MODEL)
    ctx3 = ctx.reshape(CTX_LEN // GRID_W, GRID_W, D_MODEL)
    w_in_bf = w_in[0].astype(BF16)

    xm_l, z_l, u_l = _inproj(x3, er3, ec3, g1, shift1, w_in_bf, rows=8, add_pos=True)
    xm_c, _, _ = _inproj(ctx3, er3, ec3, g1c, shift1c, w_in_bf, rows=CTX_LEN // GRID_W, add_pos=False)

    wq = _blockdiag(w_q[0]).astype(BF16)
    wkt = _blockdiag(w_k[0]).T.astype(BF16)
    wv = _blockdiag(w_v[0]).astype(BF16)
    wi, bi = _gate_weights(w_if_fwd[0], b_if_fwd[0], w_if_bwd[0], b_if_bwd[0])
    wiq, wik, wiv = wi[:, :D_MLSTM], wi[:, D_MLSTM:2 * D_MLSTM], wi[:, 2 * D_MLSTM:]
    cb = conv_b[0][None, :]
    q_l, kt_l, v_l, act_l, gp_l = _feat(xm_l, conv_w[0], cb, wq, wkt, wv, wiq, wik, wiv, bi, tm=512)
    q_c, kt_c, v_c, _, gp_c = _feat(xm_c, conv_w[0], cb, wq, wkt, wv, wiq, wik, wiv, bi, tm=CTX_LEN)

    dr_l, gc_l = _gates(gp_l, tl=2048)
    dr_c, gc_c = _gates(gp_c, tl=CTX_LEN)

    c0 = jnp.zeros((2 * HEADS, DH, 2 * DH), F32)
    m0 = jnp.zeros((2 * HEADS, 8, LANES), F32)
    kt_c, kt_l = kt_c.reshape(HEADS, DH, CTX_LEN), kt_l.reshape(HEADS, DH, t)
    c_ctx_fin, m_ctx_fin = _mlstm(q_c, kt_c, v_c, gc_c, dr_c, c0, m0, cps=CTX_LEN // CHUNK, emit=False)
    hf, hb, _, _, wg_bf, wu_bf, wd_bf = _mlstm(q_l, kt_l, v_l, gc_l, dr_l, c_ctx_fin, m_ctx_fin, cps=4, emit=True,
                                               casts=(w_gate[0], w_up[0], w_down[0]))

    f1, a2, cs, cw3, sw3 = _dft_tables(t)
    n1 = t // CHUNK
    yc, ys = _dft1(u_l.reshape(FGROUPS, n1, CHUNK, FCG), f1.astype(BF16), cw3, sw3)
    mix = jnp.einsum('kc,gcd->gkd', cs, w_fourier[0], precision=lax.Precision.HIGHEST)
    mix = (mix * float(1.0 / np.sqrt(float(t) * FCG))).astype(BF16)
    yf = _dft2(yc.reshape(FGROUPS, t, FCG), ys.reshape(FGROUPS, t, FCG), a2.astype(BF16), mix)
    yf = yf.reshape(FGROUPS, t, FCG)

    wr = jnp.concatenate([w_router_group[0], w_router_expert[0],
                          jnp.zeros((D_MODEL, LANES - N_GROUPS - N_EXPERTS), F32)], axis=1)
    br = jnp.concatenate([b_router_group[0], b_router_expert[0],
                          jnp.zeros((LANES - N_GROUPS - N_EXPERTS,), F32)])[None, :]
    x1, h2, lg = _merge(hf, hb, act_l, z_l, yf, x3, er3, ec3,
                        mlstm_norm_w[0][None, :], mlstm_skip[0][None, :], w_out[0].astype(BF16),
                        gg1, g2, shift2, wr.astype(BF16), br, rows=16)
    pos, wts, cnt = _route(lg, tl=MOE_TB, sbk=MOE_SBK)
    nsb = t // MOE_SBK
    per_blk = MOE_TB // MOE_SBK
    cnt = cnt.reshape(t // MOE_TB, N_EXPERTS, LANES)[:, :, :per_blk]
    cnt = jnp.transpose(cnt, (0, 2, 1)).reshape(nsb, N_EXPERTS).astype(jnp.int32)
    max_batches = (nsb * (MOE_SLOTS // MOE_TILE)) // MOE_BATCH + N_EXPERTS
    first, desc, bexp, nb = _expert_tables(cnt, slots=MOE_SLOTS, max_batches=max_batches)
    first_rows = jnp.repeat(first.T.astype(F32), MOE_SBK, axis=1)
    qrow, qcol = _slots(pos, wts, first_rows, tl=MOE_TB)
    qrow_sb = jnp.transpose(qrow.reshape(8, nsb, MOE_SBK), (1, 0, 2))
    xs = _dispatch(h2, qrow_sb, sbk=MOE_SBK, slots=MOE_SLOTS, per_step=4)
    ys = _experts(desc, bexp, nb, xs.reshape(nsb * (MOE_SLOTS // MOE_TILE), MOE_TILE, D_MODEL),
                  wg_bf, wu_bf, wd_bf, max_batches=max_batches)
    out = _combine(ys.reshape(nsb, MOE_SLOTS, D_MODEL), qcol, x1, gg2, sbk=MOE_SBK, slots=MOE_SLOTS, per_step=2)
    return out[None]
```

```python
import functools

import numpy as np
import jax
import jax.numpy as jnp
from jax import lax
from jax.experimental import pallas as pl
from jax.experimental.pallas import tpu as pltpu

F32 = jnp.float32
BF16 = jnp.bfloat16

D_MODEL = 1024
SEQ = 16384
GRID_W = 64
CTX_LEN = 256
D_MLSTM = 512
HEADS = 4
DH = 128
QKV_BLOCK = 4
CONV_K = 3
CHUNK = 128
D_FOURIER = 512
FGROUPS = 4
FCG = 128
N_GROUPS = 4
EPG = 4
N_EXPERTS = 16
D_EXPERT = 512
N_MOD = 6
EPS = 1e-6
POS_BASE = 10000.0
LANES = 128
NEG_BIG = -3.0e38

VMEM_LIMIT = 52 * 1024 * 1024
MOE_VMEM_LIMIT = 58 * 1024 * 1024
MOE_TB = 2048
MOE_SBK = 256
MOE_SLOTS = 768
MOE_EPS = 4
MOE_CSB = 2


def _cparams(*sem):
    return pltpu.CompilerParams(dimension_semantics=sem, vmem_limit_bytes=VMEM_LIMIT)


def _dot(a, b):
    return jnp.dot(a, b, preferred_element_type=F32)


def _dot_nt(a, b):
    return lax.dot_general(a, b, (((1,), (1,)), ((), ())), preferred_element_type=F32)


def _split_bf16(a):
    hi = a.astype(BF16)
    lo = (a - hi.astype(F32)).astype(BF16)
    return hi, lo


def _dot3(a, b):
    a_hi, a_lo = _split_bf16(a)
    b_hi, b_lo = _split_bf16(b)
    return _dot(a_hi, b_hi) + (_dot(a_hi, b_lo) + _dot(a_lo, b_hi))


def _sigmoid(x):
    return 1.0 / (1.0 + jnp.exp(-x))


def _rms(x):
    return x * lax.rsqrt(jnp.mean(x * x, axis=-1, keepdims=True) + EPS)


def _ada_kernel(c_ref, w_ref, b_ref, o_ref):
    c = c_ref[...]
    s = c * _sigmoid(c)
    o_ref[...] = _dot3(s, w_ref[...]) + b_ref[...]


def _ada(c8, w, b):
    n = w.shape[1]
    tn = 768
    return pl.pallas_call(
        _ada_kernel,
        grid=(n // tn,),
        in_specs=[pl.BlockSpec((8, D_MODEL), lambda j: (0, 0)),
                  pl.BlockSpec((D_MODEL, tn), lambda j: (0, j)),
                  pl.BlockSpec((1, tn), lambda j: (0, j))],
        out_specs=pl.BlockSpec((8, tn), lambda j: (0, j)),
        out_shape=jax.ShapeDtypeStruct((8, n), F32),
        compiler_params=_cparams("parallel"),
        name="ada",
    )(c8, w, b)


def _add_pos(x3, er_ref, ec_ref):
    r = x3.shape[0]
    pr = jnp.broadcast_to(er_ref[...], (r, GRID_W, D_MODEL // 2))
    pc = jnp.broadcast_to(ec_ref[...], (r, GRID_W, D_MODEL // 2))
    return x3 + jnp.concatenate([pr, pc], axis=-1)


def _inproj_kernel(x_ref, er_ref, ec_ref, g_ref, sh_ref, w_ref, xm_ref, z_ref, u_ref, *, add_pos):
    x3 = x_ref[...]
    if add_pos:
        x3 = _add_pos(x3, er_ref, ec_ref)
    x = x3.reshape(x3.shape[0] * GRID_W, D_MODEL)
    h = _rms(x) * g_ref[...] + sh_ref[...]
    proj = _dot(h.astype(BF16), w_ref[...])
    xm_ref[...] = proj[:, :D_MLSTM].astype(BF16)
    z_ref[...] = proj[:, D_MLSTM:2 * D_MLSTM].astype(BF16)
    for g in range(FGROUPS):
        u_ref[g] = proj[:, 2 * D_MLSTM + g * FCG:2 * D_MLSTM + (g + 1) * FCG]


def _inproj(x3, er3, ec3, g_eff, shift, w_in, *, rows, add_pos):
    nr = x3.shape[0]
    t = nr * GRID_W
    tm = rows * GRID_W
    out = jax.ShapeDtypeStruct((t, D_MLSTM), BF16)
    ospec = pl.BlockSpec((tm, D_MLSTM), lambda i: (i, 0))
    vec = pl.BlockSpec((1, D_MODEL), lambda i: (0, 0))
    return pl.pallas_call(
        functools.partial(_inproj_kernel, add_pos=add_pos),
        grid=(nr // rows,),
        in_specs=[pl.BlockSpec((rows, GRID_W, D_MODEL), lambda i: (i, 0, 0)),
                  pl.BlockSpec((rows, 1, D_MODEL // 2), lambda i: (i, 0, 0)),
                  pl.BlockSpec((1, GRID_W, D_MODEL // 2), lambda i: (0, 0, 0)),
                  vec, vec,
                  pl.BlockSpec(w_in.shape, lambda i: (0, 0))],
        out_specs=[ospec, ospec, pl.BlockSpec((FGROUPS, tm, FCG), lambda i: (0, i, 0))],
        out_shape=[out, out, jax.ShapeDtypeStruct((FGROUPS, t, FCG), F32)],
        compiler_params=_cparams("parallel"),
        name="inproj",
    )(x3, er3, ec3, g_eff, shift, w_in)


def _feat_kernel(xm_ref, prev_ref, next_ref, cw_ref, cb_ref, wq_ref, wkt_ref, wv_ref,
                 wiq_ref, wik_ref, wiv_ref, bi_ref,
                 q_ref, kt_ref, v_ref, act_ref, g_ref):
    i = pl.program_id(0)
    n = pl.num_programs(0)
    xm_bf = xm_ref[...]
    xm = xm_bf.astype(F32)
    tm = xm.shape[0]
    prev_row = prev_ref[...].astype(F32)[15:16, :] * jnp.where(i > 0, 1.0, 0.0)
    next_row = next_ref[...].astype(F32)[0:1, :] * jnp.where(i < n - 1, 1.0, 0.0)
    rid = lax.broadcasted_iota(jnp.int32, (tm, 1), 0)
    x_left = jnp.where(rid == 0, prev_row, pltpu.roll(xm, 1, 0))
    x_right = jnp.where(rid == tm - 1, next_row, pltpu.roll(xm, tm - 1, 0))
    cw = cw_ref[...]
    y = cw[0:1] * x_left + cw[1:2] * xm + cw[2:3] * x_right + cb_ref[...]
    act = (y * _sigmoid(y)).astype(BF16)
    act_ref[...] = act
    q = _dot(act, wq_ref[...])
    kt = _dot_nt(wkt_ref[...], act)
    v = _dot(xm_bf, wv_ref[...])
    q_bf = q.astype(BF16)
    kt_bf = kt.astype(BF16)
    v_bf = v.astype(BF16)
    q_s = (q * (DH ** -0.5)).astype(BF16)
    for hd in range(HEADS):
        q_ref[hd] = q_s[:, hd * DH:(hd + 1) * DH]
        v_ref[hd] = v_bf[:, hd * DH:(hd + 1) * DH]
    kt_ref[...] = kt_bf
    g = _dot_nt(wiq_ref[...], q_bf) + _dot(wik_ref[...], kt_bf) + _dot_nt(wiv_ref[...], v_bf)
    g_ref[...] = g + bi_ref[...]


def _feat(xm, conv_w, conv_b, wq, wkt, wv, wiq, wik, wiv, bi, *, tm):
    t = xm.shape[0]
    nb16 = t // 16
    k16 = tm // 16
    full = lambda a: pl.BlockSpec(a.shape, lambda i: (0,) * a.ndim)
    tok = pl.BlockSpec((tm, D_MLSTM), lambda i: (i, 0))
    heads = pl.BlockSpec((HEADS, tm, DH), lambda i: (0, i, 0))
    return pl.pallas_call(
        _feat_kernel,
        grid=(t // tm,),
        in_specs=[tok,
                  pl.BlockSpec((16, D_MLSTM), lambda i: (jnp.maximum(i * k16 - 1, 0), 0)),
                  pl.BlockSpec((16, D_MLSTM), lambda i: (jnp.minimum((i + 1) * k16, nb16 - 1), 0)),
                  full(conv_w), full(conv_b), full(wq), full(wkt), full(wv),
                  full(wiq), full(wik), full(wiv), full(bi)],
        out_specs=[heads,
                   pl.BlockSpec((D_MLSTM, tm), lambda i: (0, i)),
                   heads, tok,
                   pl.BlockSpec((16, tm), lambda i: (0, i))],
        out_shape=[jax.ShapeDtypeStruct((HEADS, t, DH), BF16),
                   jax.ShapeDtypeStruct((D_MLSTM, t), BF16),
                   jax.ShapeDtypeStruct((HEADS, t, DH), BF16),
                   jax.ShapeDtypeStruct((t, D_MLSTM), BF16),
                   jax.ShapeDtypeStruct((16, t), F32)],
        compiler_params=_cparams("parallel"),
        name="feat",
    )(xm, xm, xm, conv_w, conv_b, wq, wkt, wv, wiq, wik, wiv, bi)


def _gates_kernel(g_ref, d_ref, gc_ref):
    g = g_ref[...]
    tl = g.shape[1]
    ig = g[0:8]
    fg = g[8:16]
    lf = jnp.minimum(fg, 0.0) - jnp.log(1.0 + jnp.exp(-jnp.abs(fg)))
    pos = lax.broadcasted_iota(jnp.int32, (8, tl), 1) & (CHUNK - 1)
    is_fwd = lax.broadcasted_iota(jnp.int32, (8, tl), 0) < HEADS

    def scan(x, op, ident):
        xf = x
        xb = x
        k = 1
        while k < CHUNK:
            xf = op(xf, jnp.where(pos >= k, pltpu.roll(xf, k, 1), ident))
            xb = op(xb, jnp.where(pos < CHUNK - k, pltpu.roll(xb, tl - k, 1), ident))
            k *= 2
        return jnp.where(is_fwd, xf, xb)

    b = scan(lf, jnp.add, 0.0)
    d = ig - b
    mloc = b + scan(d, jnp.maximum, NEG_BIG)
    d_ref[...] = d
    stack = jnp.concatenate([b, mloc, jnp.zeros((LANES - 16, tl), F32)], axis=0)
    gc_ref[...] = stack.T


def _gates(g, *, tl):
    t = g.shape[1]
    return pl.pallas_call(
        _gates_kernel,
        grid=(t // tl,),
        in_specs=[pl.BlockSpec((16, tl), lambda i: (0, i))],
        out_specs=[pl.BlockSpec((8, tl), lambda i: (0, i)),
                   pl.BlockSpec((tl, LANES), lambda i: (i, 0))],
        out_shape=[jax.ShapeDtypeStruct((8, t), F32),
                   jax.ShapeDtypeStruct((t, LANES), F32)],
        compiler_params=_cparams("parallel"),
        name="gates",
    )(g)


def _mlstm_kernel(*refs, cps, emit, ncast):
    (qf_ref, kf_ref, vf_ref, gcf_ref, drf_ref,
     qb_ref, kb_ref, vb_ref, gcb_ref, drb_ref, c0_ref, m0_ref) = refs[:12]
    cast_in = refs[12:12 + ncast]
    outs = refs[12 + ncast:]
    if emit:
        hf_ref, hb_ref, cfin_ref, mfin_ref = outs[:4]
        outs = outs[4:]
    else:
        cfin_ref, mfin_ref = outs[:2]
        outs = outs[2:]
        hf_ref = hb_ref = None
    cast_out = outs[:ncast]
    c_scr, m_scr = outs[ncast:]
    step = pl.program_id(0)

    for src, dst in zip(cast_in, cast_out):
        dst[...] = src[...].astype(BF16)

    @pl.when(step == 0)
    def _():
        c_scr[...] = c0_ref[...]
        m_scr[...] = m0_ref[...]

    ti = lax.broadcasted_iota(jnp.int32, (CHUNK, CHUNK), 0)
    si = lax.broadcasted_iota(jnp.int32, (CHUNK, CHUNK), 1)
    nch = 2 * HEADS
    mask = jnp.concatenate([jnp.broadcast_to((si <= ti)[None], (HEADS, CHUNK, CHUNK)),
                            jnp.broadcast_to((si >= ti)[None], (HEADS, CHUNK, CHUNK))], axis=0)
    ones_blk = jnp.ones((nch, CHUNK, CHUNK), BF16)

    def bdot(a, b):
        return lax.dot_general(a, b, (((2,), (1,)), ((0,), (0,))), preferred_element_type=F32)

    def both(f, g):
        return [f(r) for r in range(HEADS)] + [g(r) for r in range(HEADS, nch)]

    for j in range(cps):
        rf = slice(j * CHUNK, (j + 1) * CHUNK)
        rb = slice((cps - 1 - j) * CHUNK, (cps - j) * CHUNK)
        gcf, gcb = gcf_ref[rf, :], gcb_ref[rb, :]
        drf, drb = drf_ref[:, rf], drb_ref[:, rb]
        lf, lb = CHUNK - 1, 0
        b = jnp.stack(both(lambda r: gcf[:, r:r + 1], lambda r: gcb[:, r:r + 1]))
        mloc = jnp.stack(both(lambda r: gcf[:, 8 + r:9 + r], lambda r: gcb[:, 8 + r:9 + r]))
        drow = jnp.stack(both(lambda r: drf[r:r + 1, :], lambda r: drb[r:r + 1, :]))
        btot = jnp.stack(both(lambda r: gcf[lf:lf + 1, r:r + 1], lambda r: gcb[lb:lb + 1, r:r + 1]))
        amax = jnp.stack(both(lambda r: gcf[lf:lf + 1, 8 + r:9 + r], lambda r: gcb[lb:lb + 1, 8 + r:9 + r]))
        m0 = jnp.stack([m_scr[r][0:1, 0:1] for r in range(nch)])
        c_aug = c_scr[...]
        kt = jnp.concatenate([kf_ref[:, :, rf], kb_ref[:, :, rb]], axis=0)
        vaug = jnp.concatenate([jnp.concatenate([vf_ref[:, rf, :], vb_ref[:, rb, :]], axis=0), ones_blk], axis=2)
        if emit:
            qh = jnp.concatenate([qf_ref[:, rf, :], qb_ref[:, rb, :]], axis=0)
            bm = b + m0
            m_t = jnp.maximum(bm, mloc)
            dmat = jnp.where(mask, jnp.exp((b - m_t) + drow), 0.0)
            smat = (bdot(qh, kt) * dmat).astype(BF16)
            q_in = (qh.astype(F32) * jnp.exp(bm - m_t)).astype(BF16)
            num = bdot(jnp.concatenate([smat, q_in], axis=2),
                       jnp.concatenate([vaug, c_aug.astype(BF16)], axis=1))
            den = jnp.maximum(jnp.abs(num[:, :, DH:]), jnp.exp(-m_t))
            h = (num[:, :, :DH] / den).astype(hf_ref.dtype)
            hf_ref[:, rf, :] = h[:HEADS]
            hb_ref[:, rb, :] = h[HEADS:]
        m_new = jnp.maximum(btot + m0, amax)
        decay = jnp.exp(btot + m0 - m_new)
        kw = (kt.astype(F32) * jnp.exp(btot + drow - m_new)).astype(BF16)
        c_scr[...] = decay * c_aug + bdot(kw, vaug)
        m_scr[...] = jnp.broadcast_to(m_new, (nch, 8, LANES))

    @pl.when(step == pl.num_programs(0) - 1)
    def _():
        cfin_ref[...] = c_scr[...]
        mfin_ref[...] = m_scr[...]


def _mlstm(q, kt, v, gc, dr, c0, m0, *, cps, emit, casts=()):
    t = q.shape[1]
    cb = cps * CHUNK
    nb = t // cb
    fwd_r = lambda i: (i, 0)
    bwd_r = lambda i: (nb - 1 - i, 0)
    fwd_c = lambda i: (0, i)
    bwd_c = lambda i: (0, nb - 1 - i)
    tok = lambda f: pl.BlockSpec((HEADS, cb, DH), lambda i, f=f: (0, f(i)[0], 0))
    in_specs = []
    for fr, fc in ((fwd_r, fwd_c), (bwd_r, bwd_c)):
        in_specs += [tok(fr), pl.BlockSpec((HEADS, DH, cb), lambda i, fc=fc: (0, 0, fc(i)[1])), tok(fr),
                     pl.BlockSpec((cb, LANES), fr), pl.BlockSpec((8, cb), fc)]
    cshape = (2 * HEADS, DH, 2 * DH)
    mshape = (2 * HEADS, 8, LANES)
    cspec = pl.BlockSpec(cshape, lambda i: (0, 0, 0))
    mspec = pl.BlockSpec(mshape, lambda i: (0, 0, 0))
    in_specs += [cspec, mspec]
    out_specs = [cspec, mspec]
    out_shape = [jax.ShapeDtypeStruct(cshape, F32), jax.ShapeDtypeStruct(mshape, F32)]
    if emit:
        out_specs = [tok(fwd_r), tok(bwd_r)] + out_specs
        out_shape = [jax.ShapeDtypeStruct((HEADS, t, DH), BF16)] * 2 + out_shape
    for a in casts:
        per = nb // a.shape[0]
        spec = pl.BlockSpec((1, a.shape[1] // per, a.shape[2]), lambda i, per=per: (i // per, i % per, 0))
        in_specs.append(spec)
        out_specs.append(spec)
        out_shape.append(jax.ShapeDtypeStruct(a.shape, BF16))
    return pl.pallas_call(
        functools.partial(_mlstm_kernel, cps=cps, emit=emit, ncast=len(casts)),
        grid=(nb,),
        in_specs=in_specs,
        out_specs=out_specs,
        out_shape=out_shape,
        scratch_shapes=[pltpu.VMEM(cshape, F32), pltpu.VMEM(mshape, F32)],
        compiler_params=_cparams("arbitrary"),
        name="mlstm",
    )(q, kt, v, gc, dr, q, kt, v, gc, dr, c0, m0, *casts)


FFT_ROWS = 8


def _dft1_kernel(u_ref, f_ref, cw_ref, sw_ref, yc_ref, ys_ref):
    f = f_ref[...]
    rows = CHUNK * FFT_ROWS
    u2 = u_ref.reshape(FGROUPS * rows, FCG)
    yc2 = yc_ref.reshape(FGROUPS * rows, FCG)
    ys2 = ys_ref.reshape(FGROUPS * rows, FCG)
    for s in range(FFT_ROWS):
        pick = [pl.ds(g * rows + s, CHUNK, stride=FFT_ROWS) for g in range(FGROUPS)]
        x = jnp.concatenate([u2[p, :] for p in pick], axis=1).astype(BF16)
        y = _dot(f, x)
        cw = cw_ref[s]
        sw = sw_ref[s]
        pr = y[:CHUNK] * cw - y[CHUNK:] * sw
        pi = y[:CHUNK] * sw + y[CHUNK:] * cw
        for g in range(FGROUPS):
            yc2[pick[g], :] = pr[:, g * FCG:(g + 1) * FCG]
            ys2[pick[g], :] = pi[:, g * FCG:(g + 1) * FCG]


def _dft1(u4, f1, cw3, sw3):
    blk = pl.BlockSpec((FGROUPS, CHUNK, FFT_ROWS, FCG), lambda j: (0, 0, j, 0))
    tw = pl.BlockSpec((FFT_ROWS, CHUNK, 1), lambda j: (j, 0, 0))
    out = jax.ShapeDtypeStruct(u4.shape, F32)
    return pl.pallas_call(
        _dft1_kernel,
        grid=(u4.shape[2] // FFT_ROWS,),
        in_specs=[blk, pl.BlockSpec(f1.shape, lambda j: (0, 0)), tw, tw],
        out_specs=[blk, blk],
        out_shape=[out, out],
        compiler_params=_cparams("parallel"),
        name="dft1",
    )(u4, f1, cw3, sw3)


def _dft2_kernel(yc_ref, ys_ref, a2_ref, mix_ref, o_ref):
    a2 = a2_ref[...]
    rows = CHUNK * FFT_ROWS
    o2 = o_ref.reshape(FGROUPS * rows, FCG)
    for kk in range(FFT_ROWS):
        blk = slice(kk * CHUNK, (kk + 1) * CHUNK)
        yc = jnp.concatenate([yc_ref[g, blk, :] for g in range(FGROUPS)], axis=1)
        ys = jnp.concatenate([ys_ref[g, blk, :] for g in range(FGROUPS)], axis=1)
        p = jnp.concatenate([yc, ys], axis=0).astype(BF16)
        x = _dot(a2, p).astype(BF16)
        for g in range(FGROUPS):
            cols = slice(g * FCG, (g + 1) * FCG)
            cat = jnp.concatenate([x[:CHUNK, cols], x[CHUNK:, cols]], axis=1)
            o2[pl.ds(g * rows + kk, CHUNK, stride=FFT_ROWS), :] = _dot(cat, mix_ref[g])


def _dft2(yc, ys, a2, mix):
    t = yc.shape[1]
    n1 = t // CHUNK
    tok = pl.BlockSpec((FGROUPS, FFT_ROWS * CHUNK, FCG), lambda i: (0, i, 0))
    full = lambda a: pl.BlockSpec(a.shape, lambda i: (0,) * a.ndim)
    return pl.pallas_call(
        _dft2_kernel,
        grid=(n1 // FFT_ROWS,),
        in_specs=[tok, tok, full(a2), full(mix)],
        out_specs=pl.BlockSpec((FGROUPS, CHUNK, FFT_ROWS, FCG), lambda i: (0, 0, i, 0)),
        out_shape=jax.ShapeDtypeStruct((FGROUPS, CHUNK, n1, FCG), F32),
        compiler_params=_cparams("parallel"),
        name="dft2",
    )(yc, ys, a2, mix)


def _merge_kernel(hf_ref, hb_ref, act_ref, z_ref, yf_ref, x_ref, er_ref, ec_ref,
                  nw_ref, sk_ref, wout_ref, gg1_ref, g2_ref, sh2_ref, wr_ref, br_ref,
                  x1_ref, h2_ref, lg_ref):
    parts = []
    for hd in range(HEADS):
        hh = hf_ref[hd].astype(F32) + hb_ref[hd].astype(F32)
        dl = hh - jnp.mean(hh, axis=-1, keepdims=True)
        var = jnp.mean(dl * dl, axis=-1, keepdims=True)
        parts.append(dl * lax.rsqrt(var + EPS))
    hn = jnp.concatenate(parts, axis=-1)
    z = z_ref[...].astype(F32)
    m = (hn * nw_ref[...] + sk_ref[...] * act_ref[...].astype(F32)) * (z * _sigmoid(z))
    cat = jnp.concatenate([m.astype(BF16)] + [yf_ref[g].astype(BF16) for g in range(FGROUPS)], axis=-1)
    y = _dot(cat, wout_ref[...])
    x3 = _add_pos(x_ref[...], er_ref, ec_ref)
    xp = x3.reshape(x3.shape[0] * GRID_W, D_MODEL)
    x1 = xp + _rms(y) * gg1_ref[...]
    x1_ref[...] = x1
    h2 = _rms(x1) * g2_ref[...] + sh2_ref[...]
    h2_ref[...] = h2.astype(BF16)
    lg = _dot(h2.astype(BF16), wr_ref[...]) + br_ref[...]
    lg_ref[...] = lg.T[:32]


def _merge(hf, hb, act, z, yf, x3, er3, ec3, nw, sk, wout, gg1, g2, sh2, wr, br, *, rows):
    nr = x3.shape[0]
    t = nr * GRID_W
    tm = rows * GRID_W
    tok = pl.BlockSpec((tm, D_MLSTM), lambda i: (i, 0))
    heads = pl.BlockSpec((HEADS, tm, DH), lambda i: (0, i, 0))
    full = lambda a: pl.BlockSpec(a.shape, lambda i: (0,) * a.ndim)
    return pl.pallas_call(
        _merge_kernel,
        grid=(nr // rows,),
        in_specs=[heads, heads, tok, tok, heads,
                  pl.BlockSpec((rows, GRID_W, D_MODEL), lambda i: (i, 0, 0)),
                  pl.BlockSpec((rows, 1, D_MODEL // 2), lambda i: (i, 0, 0)),
                  pl.BlockSpec((1, GRID_W, D_MODEL // 2), lambda i: (0, 0, 0)),
                  full(nw), full(sk), full(wout), full(gg1), full(g2), full(sh2), full(wr), full(br)],
        out_specs=[pl.BlockSpec((tm, D_MODEL), lambda i: (i, 0)),
                   pl.BlockSpec((tm, D_MODEL), lambda i: (i, 0)),
                   pl.BlockSpec((32, tm), lambda i: (0, i))],
        out_shape=[jax.ShapeDtypeStruct((t, D_MODEL), F32),
                   jax.ShapeDtypeStruct((t, D_MODEL), BF16),
                   jax.ShapeDtypeStruct((32, t), F32)],
        compiler_params=_cparams("parallel"),
        name="merge",
    )(hf, hb, act, z, yf, x3, er3, ec3, nw, sk, wout, gg1, g2, sh2, wr, br)


def _route_kernel(lg_ref, pos_ref, w_ref, cnt_ref, *, sbk):
    lg = lg_ref[...]
    tl = lg.shape[1]
    g = [lg[j:j + 1] for j in range(N_GROUPS)]
    e = [lg[N_GROUPS + j:N_GROUPS + j + 1] for j in range(N_EXPERTS)]
    gmax = jnp.maximum(jnp.maximum(g[0], g[1]), jnp.maximum(g[2], g[3]))
    den = jnp.exp(g[0] - gmax) + jnp.exp(g[1] - gmax) + jnp.exp(g[2] - gmax) + jnp.exp(g[3] - gmax)
    p_sel = 1.0 / den
    sel = []
    free = jnp.ones((1, tl), F32)
    for j in range(N_GROUPS):
        s = jnp.where(g[j] >= gmax, free, 0.0)
        sel.append(s)
        free = free - s
    es = []
    for j in range(EPG):
        es.append(sel[0] * e[j] + sel[1] * e[EPG + j] + sel[2] * e[2 * EPG + j] + sel[3] * e[3 * EPG + j])
    rank = []
    for j in range(EPG):
        rj = jnp.zeros((1, tl), F32)
        for i in range(EPG):
            if i == j:
                continue
            beats = (es[i] >= es[j]) if i < j else (es[i] > es[j])
            rj = rj + jnp.where(beats, 1.0, 0.0)
        rank.append(rj)
    v1 = jnp.maximum(jnp.maximum(es[0], es[1]), jnp.maximum(es[2], es[3]))
    v2 = sum(jnp.where(rank[j] == 1.0, es[j], 0.0) for j in range(EPG))
    tt = jnp.exp(v2 - v1)
    w1 = p_sel / (1.0 + tt)
    w2 = w1 * tt
    w = [jnp.where(rank[j] == 0.0, w1, jnp.where(rank[j] == 1.0, w2, 0.0)) for j in range(EPG)]
    top2 = [jnp.where(rank[j] < 2.0, 1.0, 0.0) for j in range(EPG)]
    mem = jnp.concatenate([sel[gi] * top2[j] for gi in range(N_GROUPS) for j in range(EPG)], axis=0)
    wts = jnp.concatenate([sel[gi] * w[j] for gi in range(N_GROUPS) for j in range(EPG)], axis=0)
    w_ref[...] = wts
    lane = lax.broadcasted_iota(jnp.int32, (N_EXPERTS, tl), 1) & (sbk - 1)
    c = mem
    k = 1
    while k < sbk:
        c = c + jnp.where(lane >= k, pltpu.roll(c, k, 1), 0.0)
        k *= 2
    pos_ref[...] = jnp.where(mem > 0.0, c - 1.0, -1.0)
    lane128 = lax.broadcasted_iota(jnp.int32, (N_EXPERTS, LANES), 1)
    cnt = jnp.zeros((N_EXPERTS, LANES), F32)
    for kb in range(tl // sbk):
        tot = jnp.sum(mem[:, kb * sbk:(kb + 1) * sbk], axis=1, keepdims=True)
        cnt = cnt + jnp.where(lane128 == kb, tot, 0.0)
    cnt_ref[...] = cnt


def _route(lg, *, tl, sbk):
    t = lg.shape[1]
    row = pl.BlockSpec((N_EXPERTS, tl), lambda i: (0, i))
    return pl.pallas_call(
        functools.partial(_route_kernel, sbk=sbk),
        grid=(t // tl,),
        in_specs=[pl.BlockSpec((32, tl), lambda i: (0, i))],
        out_specs=[row, row, pl.BlockSpec((N_EXPERTS, LANES), lambda i: (i, 0))],
        out_shape=[jax.ShapeDtypeStruct((N_EXPERTS, t), F32),
                   jax.ShapeDtypeStruct((N_EXPERTS, t), F32),
                   jax.ShapeDtypeStruct((t // tl * N_EXPERTS, LANES), F32)],
        compiler_params=_cparams("parallel"),
        name="route",
    )(lg)


def _slots_kernel(pos_ref, w_ref, first_ref, qrow_ref, qcol_ref):
    tl = pos_ref.shape[1]
    q0 = jnp.full((1, tl), -1.0, F32)
    q1 = jnp.full((1, tl), -1.0, F32)
    w0 = jnp.zeros((1, tl), F32)
    w1 = jnp.zeros((1, tl), F32)
    seen = jnp.zeros((1, tl), F32)
    for ex in range(N_EXPERTS):
        rk = pos_ref[ex:ex + 1, :]
        wt = w_ref[ex:ex + 1, :]
        m = jnp.where(rk >= 0.0, 1.0, 0.0)
        val = rk + first_ref[ex:ex + 1, :]
        first = (m * (1.0 - seen)) > 0.0
        second = (m * seen) > 0.0
        q0 = jnp.where(first, val, q0)
        w0 = jnp.where(first, wt, w0)
        q1 = jnp.where(second, val, q1)
        w1 = jnp.where(second, wt, w1)
        seen = seen + m
    qrow_ref[...] = jnp.concatenate([q0, q1, jnp.zeros((6, tl), F32)], axis=0)
    qcol_ref[...] = jnp.concatenate([q0, q1, w0, w1, jnp.zeros((LANES - 4, tl), F32)], axis=0).T


def _slots(pos, w, first, *, tl):
    t = pos.shape[1]
    row = pl.BlockSpec((N_EXPERTS, tl), lambda i: (0, i))
    return pl.pallas_call(
        _slots_kernel,
        grid=(t // tl,),
        in_specs=[row, row, row],
        out_specs=[pl.BlockSpec((8, tl), lambda i: (0, i)),
                   pl.BlockSpec((tl, LANES), lambda i: (i, 0))],
        out_shape=[jax.ShapeDtypeStruct((8, t), F32),
                   jax.ShapeDtypeStruct((t, LANES), F32)],
        compiler_params=_cparams("parallel"),
        name="slots",
    )(pos, w, first)


def _mlp(x, wg, wu, wd):
    gt = _dot(x, wg)
    a = ((gt * _sigmoid(gt)) * _dot(x, wu)).astype(BF16)
    return _dot(a, wd).astype(BF16)


def _moe_kernel(tab_ref, h_ref, qrow_ref, wg_ref, wu_ref, wd_ref,
                qcol_ref, x1_ref, gg2_ref, o_ref, xs_ref, *, sbk, slots, csb, off_f, off_m):
    i = pl.program_id(0)
    step = pl.program_id(1)
    nsb = h_ref.shape[0] // sbk
    eps = wg_ref.shape[0]
    exp_steps = N_EXPERTS // eps
    dump = slots

    @pl.when(step == 0)
    def _():
        pid = lax.broadcasted_iota(jnp.int32, (slots, sbk), 0).astype(F32)

        def select(sb, carry):
            q0 = qrow_ref[0, pl.ds(sb, 1), :]
            q1 = qrow_ref[1, pl.ds(sb, 1), :]
            s = jnp.where(q0 == pid, 1.0, jnp.where(q1 == pid, 1.0, 0.0)).astype(BF16)
            row0 = pl.multiple_of(sb * sbk, sbk)
            xs_ref[sb, 0:slots, :] = _dot(s, h_ref[pl.ds(row0, sbk), :]).astype(BF16)
            xs_ref[sb, slots:slots + 32, :] = jnp.zeros((32, D_MODEL), BF16)
            return carry

        lax.fori_loop(0, nsb, select, 0)

    def run_pieces(k, e, r0, halves):
        offs = []
        for sb in range(nsb):
            idx = (i * nsb + sb) * N_EXPERTS + e
            n_rows = tab_ref[idx]
            first = tab_ref[off_f + idx]
            for p in range(halves):
                r = r0 + 16 * p
                offs.append(pl.multiple_of(jnp.where(r < n_rows, first + r, dump + 16 * p), 16))
        x = jnp.concatenate([xs_ref[j // halves, pl.ds(off, 16), :] for j, off in enumerate(offs)], axis=0)
        y = _mlp(x, wg_ref[k], wu_ref[k], wd_ref[k])
        for j, off in enumerate(offs):
            xs_ref[j // halves, pl.ds(off, 16), :] = y[j * 16:(j + 1) * 16]

    @pl.when(step < exp_steps)
    def _():
        def expert(k, carry):
            e = step * eps + k
            most = tab_ref[off_m + i * N_EXPERTS + e]

            def pair(t, c1):
                run_pieces(k, e, t * 32, 2)
                return c1

            lax.fori_loop(0, most // 32, pair, 0)

            @pl.when(most % 32 != 0)
            def _():
                run_pieces(k, e, (most // 32) * 32, 1)

            return carry

        lax.fori_loop(0, eps, expert, 0)

    @pl.when(step >= exp_steps)
    def _():
        lane = lax.broadcasted_iota(jnp.int32, (sbk, slots), 1).astype(F32)
        for k in range(csb):
            sb = (step - exp_steps) * csb + k
            rows = slice(k * sbk, (k + 1) * sbk)
            qc = qcol_ref[rows, :]
            wmat = (jnp.where(lane == qc[:, 0:1], qc[:, 2:3], 0.0)
                    + jnp.where(lane == qc[:, 1:2], qc[:, 3:4], 0.0)).astype(BF16)
            y = _dot(wmat, xs_ref[sb, 0:slots, :])
            o_ref[rows, :] = x1_ref[rows, :] + _rms(y) * gg2_ref[...]


def _moe(tab, h2, qrow3, wg, wu, wd, qcol, x1, gg2, *, tb, sbk, slots, eps, csb, off_f, off_m):
    t = h2.shape[0]
    nsb = tb // sbk
    comb_steps = nsb // csb
    exp_steps = N_EXPERTS // eps
    wblk = lambda i, s, c: (jnp.minimum(s, exp_steps - 1), 0, 0)
    oblk = lambda i, s, c: (i * comb_steps + jnp.maximum(s - exp_steps, 0), 0)
    return pl.pallas_call(
        functools.partial(_moe_kernel, sbk=sbk, slots=slots, csb=csb, off_f=off_f, off_m=off_m),
        grid_spec=pltpu.PrefetchScalarGridSpec(
            num_scalar_prefetch=1,
            grid=(t // tb, exp_steps + comb_steps),
            in_specs=[pl.BlockSpec((tb, D_MODEL), lambda i, s, c: (i, 0)),
                      pl.BlockSpec((8, nsb, sbk), lambda i, s, c: (0, i, 0)),
                      pl.BlockSpec((eps, D_MODEL, D_EXPERT), wblk),
                      pl.BlockSpec((eps, D_MODEL, D_EXPERT), wblk),
                      pl.BlockSpec((eps, D_EXPERT, D_MODEL), wblk),
                      pl.BlockSpec((csb * sbk, LANES), oblk),
                      pl.BlockSpec((csb * sbk, D_MODEL), oblk),
                      pl.BlockSpec((1, D_MODEL), lambda i, s, c: (0, 0))],
            out_specs=pl.BlockSpec((csb * sbk, D_MODEL), oblk),
            scratch_shapes=[pltpu.VMEM((nsb, slots + 32, D_MODEL), BF16)]),
        out_shape=jax.ShapeDtypeStruct((t, D_MODEL), F32),
        compiler_params=pltpu.CompilerParams(dimension_semantics=("parallel", "arbitrary"),
                                             vmem_limit_bytes=MOE_VMEM_LIMIT),
        name="moe",
    )(tab, h2, qrow3, wg, wu, wd, qcol, x1, gg2)


def _slot_tables(cnt, *, nsb):
    rows = ((cnt + 15) // 16) * 16
    first = jnp.cumsum(rows, axis=1) - rows
    most = jnp.max(rows.reshape(-1, nsb, N_EXPERTS), axis=1)
    return rows, first, most


def _pos_tables(rows):
    quarter = D_MODEL // 4
    freq = 1.0 / (POS_BASE ** (np.arange(quarter, dtype=np.float64) / quarter))
    r = np.arange(rows, dtype=np.float64)[:, None] * freq
    cl = np.arange(GRID_W, dtype=np.float64)[:, None] * freq
    er = np.concatenate([np.sin(r), np.cos(r)], axis=-1).astype(np.float32)
    ec = np.concatenate([np.sin(cl), np.cos(cl)], axis=-1).astype(np.float32)
    return jnp.asarray(er[:, None, :]), jnp.asarray(ec[None, :, :])


def _dft_tables(t):
    n = np.arange(CHUNK, dtype=np.int64)
    prod = n[:, None] * n[None, :]
    ang = (prod % CHUNK).astype(np.float64) * (2.0 * np.pi / CHUNK)
    c, s = np.cos(ang), np.sin(ang)
    f1 = np.concatenate([c, s], axis=0)
    a2 = np.concatenate([np.concatenate([c, -s], axis=1), np.concatenate([s, c], axis=1)], axis=0)
    cs = np.concatenate([c, -s], axis=0)
    angw = prod.astype(np.float64) * (2.0 * np.pi / t)
    f32 = lambda a: jnp.asarray(a.astype(np.float32))
    return f32(f1), f32(a2), f32(cs), f32(np.cos(angw)[:, :, None]), f32(np.sin(angw)[:, :, None])


def _blockdiag(w):
    n = w.shape[0]
    size = n * QKV_BLOCK
    spread = np.tile(np.eye(QKV_BLOCK, dtype=np.float32), (1, n))
    rows = jnp.dot(w.reshape(size, QKV_BLOCK), jnp.asarray(spread), precision=lax.Precision.HIGHEST)
    blk = np.arange(size) // QKV_BLOCK
    mask = (blk[:, None] == blk[None, :]).astype(np.float32)
    return rows * jnp.asarray(mask)


def _gate_weights(w_f, b_f, w_b, b_b):
    w = jnp.concatenate([w_f[:, :HEADS], w_b[:, :HEADS], w_f[:, HEADS:], w_b[:, HEADS:]], axis=1).T
    b = jnp.concatenate([b_f[:HEADS], b_b[:HEADS], b_f[HEADS:], b_b[HEADS:]])
    return w.astype(BF16), b[:, None]


def kernel(x, c, ctx, c_ctx, w_ada, b_ada, g_pre_mix, g_post_mix, g_pre_ffn, g_post_ffn,
           w_in, conv_w, conv_b, w_q, w_k, w_v, w_if_fwd, b_if_fwd, w_if_bwd, b_if_bwd,
           mlstm_norm_w, mlstm_skip, w_fourier, w_out, w_router_group, b_router_group,
           w_router_expert, b_router_expert, w_gate, w_up, w_down):
    t = x.shape[1]
    rows = t // GRID_W

    c8 = jnp.concatenate([c, c_ctx[None, :], jnp.zeros((6, D_MODEL), F32)], axis=0)
    mod = _ada(c8, w_ada[0], b_ada[0][None, :])
    shift1, scale1, gate1, shift2, scale2, gate2 = [mod[0:1, k * D_MODEL:(k + 1) * D_MODEL] for k in range(N_MOD)]
    shift1c, scale1c = mod[1:2, 0:D_MODEL], mod[1:2, D_MODEL:2 * D_MODEL]
    g1 = g_pre_mix[0][None, :] * (1.0 + scale1)
    g1c = g_pre_mix[0][None, :] * (1.0 + scale1c)
    gg1 = g_post_mix[0][None, :] * gate1
    g2 = g_pre_ffn[0][None, :] * (1.0 + scale2)
    gg2 = g_post_ffn[0][None, :] * gate2

    er3, ec3 = _pos_tables(rows)
    x3 = x.reshape(rows, GRID_W, D_MODEL)
    ctx3 = ctx.reshape(CTX_LEN // GRID_W, GRID_W, D_MODEL)
    w_in_bf = w_in[0].astype(BF16)

    xm_l, z_l, u_l = _inproj(x3, er3, ec3, g1, shift1, w_in_bf, rows=8, add_pos=True)
    xm_c, _, _ = _inproj(ctx3, er3, ec3, g1c, shift1c, w_in_bf, rows=CTX_LEN // GRID_W, add_pos=False)

    wq = _blockdiag(w_q[0]).astype(BF16)
    wkt = _blockdiag(w_k[0]).T.astype(BF16)
    wv = _blockdiag(w_v[0]).astype(BF16)
    wi, bi = _gate_weights(w_if_fwd[0], b_if_fwd[0], w_if_bwd[0], b_if_bwd[0])
    wiq, wik, wiv = wi[:, :D_MLSTM], wi[:, D_MLSTM:2 * D_MLSTM], wi[:, 2 * D_MLSTM:]
    cb = conv_b[0][None, :]
    q_l, kt_l, v_l, act_l, gp_l = _feat(xm_l, conv_w[0], cb, wq, wkt, wv, wiq, wik, wiv, bi, tm=512)
    q_c, kt_c, v_c, _, gp_c = _feat(xm_c, conv_w[0], cb, wq, wkt, wv, wiq, wik, wiv, bi, tm=CTX_LEN)

    dr_l, gc_l = _gates(gp_l, tl=2048)
    dr_c, gc_c = _gates(gp_c, tl=CTX_LEN)

    c0 = jnp.zeros((2 * HEADS, DH, 2 * DH), F32)
    m0 = jnp.zeros((2 * HEADS, 8, LANES), F32)
    kt_c, kt_l = kt_c.reshape(HEADS, DH, CTX_LEN), kt_l.reshape(HEADS, DH, t)
    c_ctx_fin, m_ctx_fin = _mlstm(q_c, kt_c, v_c, gc_c, dr_c, c0, m0, cps=CTX_LEN // CHUNK, emit=False)
    hf, hb, _, _, wg_bf, wu_bf, wd_bf = _mlstm(q_l, kt_l, v_l, gc_l, dr_l, c_ctx_fin, m_ctx_fin, cps=4, emit=True,
                                               casts=(w_gate[0], w_up[0], w_down[0]))

    f1, a2, cs, cw3, sw3 = _dft_tables(t)
    n1 = t // CHUNK
    yc, ys = _dft1(u_l.reshape(FGROUPS, n1, CHUNK, FCG), f1.astype(BF16), cw3, sw3)
    mix = jnp.einsum('kc,gcd->gkd', cs, w_fourier[0], precision=lax.Precision.HIGHEST)
    mix = (mix * float(1.0 / np.sqrt(float(t) * FCG))).astype(BF16)
    yf = _dft2(yc.reshape(FGROUPS, t, FCG), ys.reshape(FGROUPS, t, FCG), a2.astype(BF16), mix)
    yf = yf.reshape(FGROUPS, t, FCG)

    wr = jnp.concatenate([w_router_group[0], w_router_expert[0],
                          jnp.zeros((D_MODEL, LANES - N_GROUPS - N_EXPERTS), F32)], axis=1)
    br = jnp.concatenate([b_router_group[0], b_router_expert[0],
                          jnp.zeros((LANES - N_GROUPS - N_EXPERTS,), F32)])[None, :]
    x1, h2, lg = _merge(hf, hb, act_l, z_l, yf, x3, er3, ec3,
                        mlstm_norm_w[0][None, :], mlstm_skip[0][None, :], w_out[0].astype(BF16),
                        gg1, g2, shift2, wr.astype(BF16), br, rows=16)
    pos, wts, cnt = _route(lg, tl=MOE_TB, sbk=MOE_SBK)
    nblk, nsb = t // MOE_TB, MOE_TB // MOE_SBK
    cnt = cnt.reshape(nblk, N_EXPERTS, LANES)[:, :, :nsb]
    cnt = jnp.transpose(cnt, (0, 2, 1)).reshape(nblk * nsb, N_EXPERTS).astype(jnp.int32)
    rows, first, most = _slot_tables(cnt, nsb=nsb)
    tab = jnp.concatenate([a.reshape(-1) for a in (rows, first, most)])
    off_f = nblk * nsb * N_EXPERTS
    off_m = 2 * off_f
    first_rows = jnp.repeat(first.T.astype(F32), MOE_SBK, axis=1)
    qrow, qcol = _slots(pos, wts, first_rows, tl=MOE_TB)
    out = _moe(tab, h2, qrow.reshape(8, t // MOE_SBK, MOE_SBK), wg_bf, wu_bf, wd_bf,
               qcol, x1, gg2, tb=MOE_TB, sbk=MOE_SBK, slots=MOE_SLOTS, eps=MOE_EPS, csb=MOE_CSB,
               off_f=off_f, off_m=off_m)
    return out[None]
```

```python
import functools

import numpy as np
import jax
import jax.numpy as jnp
from jax import lax
from jax.experimental import pallas as pl
from jax.experimental.pallas import tpu as pltpu

F32 = jnp.float32
BF16 = jnp.bfloat16

D_MODEL = 1024
SEQ = 16384
GRID_W = 64
CTX_LEN = 256
D_MLSTM = 512
HEADS = 4
DH = 128
QKV_BLOCK = 4
CONV_K = 3
CHUNK = 128
D_FOURIER = 512
FGROUPS = 4
FCG = 128
N_GROUPS = 4
EPG = 4
N_EXPERTS = 16
D_EXPERT = 512
N_MOD = 6
EPS = 1e-6
POS_BASE = 10000.0
LANES = 128
NEG_BIG = -3.0e38

VMEM_LIMIT = 52 * 1024 * 1024
MOE_VMEM_LIMIT = 58 * 1024 * 1024
MOE_TB = 2048
MOE_SBK = 256
MOE_SLOTS = 768
MOE_EPS = 4
MOE_CSB = 2


def _cparams(*sem):
    return pltpu.CompilerParams(dimension_semantics=sem, vmem_limit_bytes=VMEM_LIMIT)


def _dot(a, b):
    return jnp.dot(a, b, preferred_element_type=F32)


def _dot_nt(a, b):
    return lax.dot_general(a, b, (((1,), (1,)), ((), ())), preferred_element_type=F32)


def _split_bf16(a):
    hi = a.astype(BF16)
    lo = (a - hi.astype(F32)).astype(BF16)
    return hi, lo


def _dot3(a, b):
    a_hi, a_lo = _split_bf16(a)
    b_hi, b_lo = _split_bf16(b)
    return _dot(a_hi, b_hi) + (_dot(a_hi, b_lo) + _dot(a_lo, b_hi))


def _sigmoid(x):
    return 1.0 / (1.0 + jnp.exp(-x))


def _rms(x):
    return x * lax.rsqrt(jnp.mean(x * x, axis=-1, keepdims=True) + EPS)


def _ada_kernel(c_ref, w_ref, b_ref, o_ref):
    c = c_ref[...]
    s = c * _sigmoid(c)
    o_ref[...] = _dot3(s, w_ref[...]) + b_ref[...]


def _ada(c8, w, b):
    n = w.shape[1]
    tn = 768
    return pl.pallas_call(
        _ada_kernel,
        grid=(n // tn,),
        in_specs=[pl.BlockSpec((8, D_MODEL), lambda j: (0, 0)),
                  pl.BlockSpec((D_MODEL, tn), lambda j: (0, j)),
                  pl.BlockSpec((1, tn), lambda j: (0, j))],
        out_specs=pl.BlockSpec((8, tn), lambda j: (0, j)),
        out_shape=jax.ShapeDtypeStruct((8, n), F32),
        compiler_params=_cparams("parallel"),
        name="ada",
    )(c8, w, b)


def _add_pos(x3, er_ref, ec_ref):
    r = x3.shape[0]
    pr = jnp.broadcast_to(er_ref[...], (r, GRID_W, D_MODEL // 2))
    pc = jnp.broadcast_to(ec_ref[...], (r, GRID_W, D_MODEL // 2))
    return x3 + jnp.concatenate([pr, pc], axis=-1)


def _inproj_kernel(x_ref, er_ref, ec_ref, g_ref, sh_ref, w_ref, xm_ref, z_ref, u_ref, *, add_pos):
    x3 = x_ref[...]
    if add_pos:
        x3 = _add_pos(x3, er_ref, ec_ref)
    x = x3.reshape(x3.shape[0] * GRID_W, D_MODEL)
    h = _rms(x) * g_ref[...] + sh_ref[...]
    proj = _dot(h.astype(BF16), w_ref[...])
    xm_ref[...] = proj[:, :D_MLSTM].astype(BF16)
    z_ref[...] = proj[:, D_MLSTM:2 * D_MLSTM].astype(BF16)
    for g in range(FGROUPS):
        u_ref[g] = proj[:, 2 * D_MLSTM + g * FCG:2 * D_MLSTM + (g + 1) * FCG]


def _inproj(x3, er3, ec3, g_eff, shift, w_in, *, rows, add_pos):
    nr = x3.shape[0]
    t = nr * GRID_W
    tm = rows * GRID_W
    out = jax.ShapeDtypeStruct((t, D_MLSTM), BF16)
    ospec = pl.BlockSpec((tm, D_MLSTM), lambda i: (i, 0))
    vec = pl.BlockSpec((1, D_MODEL), lambda i: (0, 0))
    return pl.pallas_call(
        functools.partial(_inproj_kernel, add_pos=add_pos),
        grid=(nr // rows,),
        in_specs=[pl.BlockSpec((rows, GRID_W, D_MODEL), lambda i: (i, 0, 0)),
                  pl.BlockSpec((rows, 1, D_MODEL // 2), lambda i: (i, 0, 0)),
                  pl.BlockSpec((1, GRID_W, D_MODEL // 2), lambda i: (0, 0, 0)),
                  vec, vec,
                  pl.BlockSpec(w_in.shape, lambda i: (0, 0))],
        out_specs=[ospec, ospec, pl.BlockSpec((FGROUPS, tm, FCG), lambda i: (0, i, 0))],
        out_shape=[out, out, jax.ShapeDtypeStruct((FGROUPS, t, FCG), F32)],
        compiler_params=_cparams("parallel"),
        name="inproj",
    )(x3, er3, ec3, g_eff, shift, w_in)


def _feat_kernel(xm_ref, prev_ref, next_ref, cw_ref, cb_ref, wq_ref, wkt_ref, wv_ref,
                 wiq_ref, wik_ref, wiv_ref, bi_ref,
                 q_ref, kt_ref, v_ref, act_ref, g_ref):
    i = pl.program_id(0)
    n = pl.num_programs(0)
    xm_bf = xm_ref[...]
    xm = xm_bf.astype(F32)
    tm = xm.shape[0]
    prev_row = prev_ref[...].astype(F32)[15:16, :] * jnp.where(i > 0, 1.0, 0.0)
    next_row = next_ref[...].astype(F32)[0:1, :] * jnp.where(i < n - 1, 1.0, 0.0)
    rid = lax.broadcasted_iota(jnp.int32, (tm, 1), 0)
    x_left = jnp.where(rid == 0, prev_row, pltpu.roll(xm, 1, 0))
    x_right = jnp.where(rid == tm - 1, next_row, pltpu.roll(xm, tm - 1, 0))
    cw = cw_ref[...]
    y = cw[0:1] * x_left + cw[1:2] * xm + cw[2:3] * x_right + cb_ref[...]
    act = (y * _sigmoid(y)).astype(BF16)
    act_ref[...] = act
    q = _dot(act, wq_ref[...])
    kt = _dot_nt(wkt_ref[...], act)
    v = _dot(xm_bf, wv_ref[...])
    q_bf = q.astype(BF16)
    kt_bf = kt.astype(BF16)
    v_bf = v.astype(BF16)
    q_s = (q * (DH ** -0.5)).astype(BF16)
    for hd in range(HEADS):
        q_ref[hd] = q_s[:, hd * DH:(hd + 1) * DH]
        v_ref[hd] = v_bf[:, hd * DH:(hd + 1) * DH]
    kt_ref[...] = kt_bf
    g = _dot_nt(wiq_ref[...], q_bf) + _dot(wik_ref[...], kt_bf) + _dot_nt(wiv_ref[...], v_bf)
    g_ref[...] = g + bi_ref[...]


def _feat(xm, conv_w, conv_b, wq, wkt, wv, wiq, wik, wiv, bi, *, tm):
    t = xm.shape[0]
    nb16 = t // 16
    k16 = tm // 16
    full = lambda a: pl.BlockSpec(a.shape, lambda i: (0,) * a.ndim)
    tok = pl.BlockSpec((tm, D_MLSTM), lambda i: (i, 0))
    heads = pl.BlockSpec((HEADS, tm, DH), lambda i: (0, i, 0))
    return pl.pallas_call(
        _feat_kernel,
        grid=(t // tm,),
        in_specs=[tok,
                  pl.BlockSpec((16, D_MLSTM), lambda i: (jnp.maximum(i * k16 - 1, 0), 0)),
                  pl.BlockSpec((16, D_MLSTM), lambda i: (jnp.minimum((i + 1) * k16, nb16 - 1), 0)),
                  full(conv_w), full(conv_b), full(wq), full(wkt), full(wv),
                  full(wiq), full(wik), full(wiv), full(bi)],
        out_specs=[heads,
                   pl.BlockSpec((D_MLSTM, tm), lambda i: (0, i)),
                   heads, tok,
                   pl.BlockSpec((16, tm), lambda i: (0, i))],
        out_shape=[jax.ShapeDtypeStruct((HEADS, t, DH), BF16),
                   jax.ShapeDtypeStruct((D_MLSTM, t), BF16),
                   jax.ShapeDtypeStruct((HEADS, t, DH), BF16),
                   jax.ShapeDtypeStruct((t, D_MLSTM), BF16),
                   jax.ShapeDtypeStruct((16, t), F32)],
        compiler_params=_cparams("parallel"),
        name="feat",
    )(xm, xm, xm, conv_w, conv_b, wq, wkt, wv, wiq, wik, wiv, bi)


def _gates_kernel(g_ref, d_ref, gc_ref):
    g = g_ref[...]
    tl = g.shape[1]
    ig = g[0:8]
    fg = g[8:16]
    lf = jnp.minimum(fg, 0.0) - jnp.log(1.0 + jnp.exp(-jnp.abs(fg)))
    pos = lax.broadcasted_iota(jnp.int32, (8, tl), 1) & (CHUNK - 1)
    is_fwd = lax.broadcasted_iota(jnp.int32, (8, tl), 0) < HEADS

    def scan(x, op, ident):
        xf = x
        xb = x
        k = 1
        while k < CHUNK:
            xf = op(xf, jnp.where(pos >= k, pltpu.roll(xf, k, 1), ident))
            xb = op(xb, jnp.where(pos < CHUNK - k, pltpu.roll(xb, tl - k, 1), ident))
            k *= 2
        return jnp.where(is_fwd, xf, xb)

    b = scan(lf, jnp.add, 0.0)
    d = ig - b
    mloc = b + scan(d, jnp.maximum, NEG_BIG)
    d_ref[...] = d
    stack = jnp.concatenate([b, mloc, jnp.zeros((LANES - 16, tl), F32)], axis=0)
    gc_ref[...] = stack.T


def _gates(g, *, tl):
    t = g.shape[1]
    return pl.pallas_call(
        _gates_kernel,
        grid=(t // tl,),
        in_specs=[pl.BlockSpec((16, tl), lambda i: (0, i))],
        out_specs=[pl.BlockSpec((8, tl), lambda i: (0, i)),
                   pl.BlockSpec((tl, LANES), lambda i: (i, 0))],
        out_shape=[jax.ShapeDtypeStruct((8, t), F32),
                   jax.ShapeDtypeStruct((t, LANES), F32)],
        compiler_params=_cparams("parallel"),
        name="gates",
    )(g)


def _mlstm_kernel(*refs, cps, emit, ncast):
    (qf_ref, kf_ref, vf_ref, gcf_ref, drf_ref,
     qb_ref, kb_ref, vb_ref, gcb_ref, drb_ref, c0_ref, m0_ref) = refs[:12]
    cast_in = refs[12:12 + ncast]
    outs = refs[12 + ncast:]
    if emit:
        hf_ref, hb_ref, cfin_ref, mfin_ref = outs[:4]
        outs = outs[4:]
    else:
        cfin_ref, mfin_ref = outs[:2]
        outs = outs[2:]
        hf_ref = hb_ref = None
    cast_out = outs[:ncast]
    c_scr, m_scr = outs[ncast:]
    step = pl.program_id(0)

    for src, dst in zip(cast_in, cast_out):
        dst[...] = src[...].astype(BF16)

    @pl.when(step == 0)
    def _():
        c_scr[...] = c0_ref[...]
        m_scr[...] = m0_ref[...]

    ti = lax.broadcasted_iota(jnp.int32, (CHUNK, CHUNK), 0)
    si = lax.broadcasted_iota(jnp.int32, (CHUNK, CHUNK), 1)
    nch = 2 * HEADS
    mask = jnp.concatenate([jnp.broadcast_to((si <= ti)[None], (HEADS, CHUNK, CHUNK)),
                            jnp.broadcast_to((si >= ti)[None], (HEADS, CHUNK, CHUNK))], axis=0)
    ones_blk = jnp.ones((nch, CHUNK, CHUNK), BF16)

    def bdot(a, b):
        return lax.dot_general(a, b, (((2,), (1,)), ((0,), (0,))), preferred_element_type=F32)

    def both(f, g):
        return [f(r) for r in range(HEADS)] + [g(r) for r in range(HEADS, nch)]

    for j in range(cps):
        rf = slice(j * CHUNK, (j + 1) * CHUNK)
        rb = slice((cps - 1 - j) * CHUNK, (cps - j) * CHUNK)
        gcf, gcb = gcf_ref[rf, :], gcb_ref[rb, :]
        drf, drb = drf_ref[:, rf], drb_ref[:, rb]
        lf, lb = CHUNK - 1, 0
        b = jnp.stack(both(lambda r: gcf[:, r:r + 1], lambda r: gcb[:, r:r + 1]))
        mloc = jnp.stack(both(lambda r: gcf[:, 8 + r:9 + r], lambda r: gcb[:, 8 + r:9 + r]))
        drow = jnp.stack(both(lambda r: drf[r:r + 1, :], lambda r: drb[r:r + 1, :]))
        btot = jnp.stack(both(lambda r: gcf[lf:lf + 1, r:r + 1], lambda r: gcb[lb:lb + 1, r:r + 1]))
        amax = jnp.stack(both(lambda r: gcf[lf:lf + 1, 8 + r:9 + r], lambda r: gcb[lb:lb + 1, 8 + r:9 + r]))
        m0 = jnp.stack([m_scr[r][0:1, 0:1] for r in range(nch)])
        c_aug = c_scr[...]
        kt = jnp.concatenate([kf_ref[:, :, rf], kb_ref[:, :, rb]], axis=0)
        vaug = jnp.concatenate([jnp.concatenate([vf_ref[:, rf, :], vb_ref[:, rb, :]], axis=0), ones_blk], axis=2)
        if emit:
            qh = jnp.concatenate([qf_ref[:, rf, :], qb_ref[:, rb, :]], axis=0)
            bm = b + m0
            m_t = jnp.maximum(bm, mloc)
            dmat = jnp.where(mask, jnp.exp((b - m_t) + drow), 0.0)
            smat = (bdot(qh, kt) * dmat).astype(BF16)
            q_in = (qh.astype(F32) * jnp.exp(bm - m_t)).astype(BF16)
            num = bdot(jnp.concatenate([smat, q_in], axis=2),
                       jnp.concatenate([vaug, c_aug.astype(BF16)], axis=1))
            den = jnp.maximum(jnp.abs(num[:, :, DH:]), jnp.exp(-m_t))
            h = (num[:, :, :DH] / den).astype(hf_ref.dtype)
            hf_ref[:, rf, :] = h[:HEADS]
            hb_ref[:, rb, :] = h[HEADS:]
        m_new = jnp.maximum(btot + m0, amax)
        decay = jnp.exp(btot + m0 - m_new)
        kw = (kt.astype(F32) * jnp.exp(btot + drow - m_new)).astype(BF16)
        c_scr[...] = decay * c_aug + bdot(kw, vaug)
        m_scr[...] = jnp.broadcast_to(m_new, (nch, 8, LANES))

    @pl.when(step == pl.num_programs(0) - 1)
    def _():
        cfin_ref[...] = c_scr[...]
        mfin_ref[...] = m_scr[...]


def _expert_slot(e):
    return (e % EPG) * N_GROUPS + e // EPG


def _mlstm(q, kt, v, gc, dr, c0, m0, *, cps, emit, casts=()):
    t = q.shape[1]
    cb = cps * CHUNK
    nb = t // cb
    fwd_r = lambda i: (i, 0)
    bwd_r = lambda i: (nb - 1 - i, 0)
    fwd_c = lambda i: (0, i)
    bwd_c = lambda i: (0, nb - 1 - i)
    tok = lambda f: pl.BlockSpec((HEADS, cb, DH), lambda i, f=f: (0, f(i)[0], 0))
    in_specs = []
    for fr, fc in ((fwd_r, fwd_c), (bwd_r, bwd_c)):
        in_specs += [tok(fr), pl.BlockSpec((HEADS, DH, cb), lambda i, fc=fc: (0, 0, fc(i)[1])), tok(fr),
                     pl.BlockSpec((cb, LANES), fr), pl.BlockSpec((8, cb), fc)]
    cshape = (2 * HEADS, DH, 2 * DH)
    mshape = (2 * HEADS, 8, LANES)
    cspec = pl.BlockSpec(cshape, lambda i: (0, 0, 0))
    mspec = pl.BlockSpec(mshape, lambda i: (0, 0, 0))
    in_specs += [cspec, mspec]
    out_specs = [cspec, mspec]
    out_shape = [jax.ShapeDtypeStruct(cshape, F32), jax.ShapeDtypeStruct(mshape, F32)]
    if emit:
        out_specs = [tok(fwd_r), tok(bwd_r)] + out_specs
        out_shape = [jax.ShapeDtypeStruct((HEADS, t, DH), BF16)] * 2 + out_shape
    for a in casts:
        per = nb // a.shape[0]
        blk = (1, a.shape[1] // per, a.shape[2])
        in_specs.append(pl.BlockSpec(blk, lambda i, per=per: (i // per, i % per, 0)))
        out_specs.append(pl.BlockSpec(blk, lambda i, per=per: (_expert_slot(i // per), i % per, 0)))
        out_shape.append(jax.ShapeDtypeStruct(a.shape, BF16))
    return pl.pallas_call(
        functools.partial(_mlstm_kernel, cps=cps, emit=emit, ncast=len(casts)),
        grid=(nb,),
        in_specs=in_specs,
        out_specs=out_specs,
        out_shape=out_shape,
        scratch_shapes=[pltpu.VMEM(cshape, F32), pltpu.VMEM(mshape, F32)],
        compiler_params=_cparams("arbitrary"),
        name="mlstm",
    )(q, kt, v, gc, dr, q, kt, v, gc, dr, c0, m0, *casts)


FFT_ROWS = 8


def _dft1_kernel(u_ref, f_ref, cw_ref, sw_ref, yc_ref, ys_ref):
    f = f_ref[...]
    rows = CHUNK * FFT_ROWS
    u2 = u_ref.reshape(FGROUPS * rows, FCG)
    yc2 = yc_ref.reshape(FGROUPS * rows, FCG)
    ys2 = ys_ref.reshape(FGROUPS * rows, FCG)
    for s in range(FFT_ROWS):
        pick = [pl.ds(g * rows + s, CHUNK, stride=FFT_ROWS) for g in range(FGROUPS)]
        x = jnp.concatenate([u2[p, :] for p in pick], axis=1).astype(BF16)
        y = _dot(f, x)
        cw = cw_ref[s]
        sw = sw_ref[s]
        pr = y[:CHUNK] * cw - y[CHUNK:] * sw
        pi = y[:CHUNK] * sw + y[CHUNK:] * cw
        for g in range(FGROUPS):
            yc2[pick[g], :] = pr[:, g * FCG:(g + 1) * FCG]
            ys2[pick[g], :] = pi[:, g * FCG:(g + 1) * FCG]


def _dft1(u4, f1, cw3, sw3):
    blk = pl.BlockSpec((FGROUPS, CHUNK, FFT_ROWS, FCG), lambda j: (0, 0, j, 0))
    tw = pl.BlockSpec((FFT_ROWS, CHUNK, 1), lambda j: (j, 0, 0))
    out = jax.ShapeDtypeStruct(u4.shape, F32)
    return pl.pallas_call(
        _dft1_kernel,
        grid=(u4.shape[2] // FFT_ROWS,),
        in_specs=[blk, pl.BlockSpec(f1.shape, lambda j: (0, 0)), tw, tw],
        out_specs=[blk, blk],
        out_shape=[out, out],
        compiler_params=_cparams("parallel"),
        name="dft1",
    )(u4, f1, cw3, sw3)


def _dft2_kernel(yc_ref, ys_ref, a2_ref, mix_ref, o_ref):
    a2 = a2_ref[...]
    rows = CHUNK * FFT_ROWS
    o2 = o_ref.reshape(FGROUPS * rows, FCG)
    for kk in range(FFT_ROWS):
        blk = slice(kk * CHUNK, (kk + 1) * CHUNK)
        yc = jnp.concatenate([yc_ref[g, blk, :] for g in range(FGROUPS)], axis=1)
        ys = jnp.concatenate([ys_ref[g, blk, :] for g in range(FGROUPS)], axis=1)
        p = jnp.concatenate([yc, ys], axis=0).astype(BF16)
        x = _dot(a2, p).astype(BF16)
        for g in range(FGROUPS):
            cols = slice(g * FCG, (g + 1) * FCG)
            cat = jnp.concatenate([x[:CHUNK, cols], x[CHUNK:, cols]], axis=1)
            o2[pl.ds(g * rows + kk, CHUNK, stride=FFT_ROWS), :] = _dot(cat, mix_ref[g])


def _dft2(yc, ys, a2, mix):
    t = yc.shape[1]
    n1 = t // CHUNK
    tok = pl.BlockSpec((FGROUPS, FFT_ROWS * CHUNK, FCG), lambda i: (0, i, 0))
    full = lambda a: pl.BlockSpec(a.shape, lambda i: (0,) * a.ndim)
    return pl.pallas_call(
        _dft2_kernel,
        grid=(n1 // FFT_ROWS,),
        in_specs=[tok, tok, full(a2), full(mix)],
        out_specs=pl.BlockSpec((FGROUPS, CHUNK, FFT_ROWS, FCG), lambda i: (0, 0, i, 0)),
        out_shape=jax.ShapeDtypeStruct((FGROUPS, CHUNK, n1, FCG), F32),
        compiler_params=_cparams("parallel"),
        name="dft2",
    )(yc, ys, a2, mix)


def _merge_kernel(hf_ref, hb_ref, act_ref, z_ref, yf_ref, x_ref, er_ref, ec_ref,
                  nw_ref, sk_ref, wout_ref, gg1_ref, g2_ref, sh2_ref, wr_ref, br_ref,
                  x1_ref, h2_ref, lg_ref):
    parts = []
    for hd in range(HEADS):
        hh = hf_ref[hd].astype(F32) + hb_ref[hd].astype(F32)
        dl = hh - jnp.mean(hh, axis=-1, keepdims=True)
        var = jnp.mean(dl * dl, axis=-1, keepdims=True)
        parts.append(dl * lax.rsqrt(var + EPS))
    hn = jnp.concatenate(parts, axis=-1)
    z = z_ref[...].astype(F32)
    m = (hn * nw_ref[...] + sk_ref[...] * act_ref[...].astype(F32)) * (z * _sigmoid(z))
    cat = jnp.concatenate([m.astype(BF16)] + [yf_ref[g].astype(BF16) for g in range(FGROUPS)], axis=-1)
    y = _dot(cat, wout_ref[...])
    x3 = _add_pos(x_ref[...], er_ref, ec_ref)
    xp = x3.reshape(x3.shape[0] * GRID_W, D_MODEL)
    x1 = xp + _rms(y) * gg1_ref[...]
    x1_ref[...] = x1
    h2 = _rms(x1) * g2_ref[...] + sh2_ref[...]
    h2_ref[...] = h2.astype(BF16)
    lg = _dot(h2.astype(BF16), wr_ref[...]) + br_ref[...]
    lg_ref[...] = lg.T[:32]


def _merge(hf, hb, act, z, yf, x3, er3, ec3, nw, sk, wout, gg1, g2, sh2, wr, br, *, rows):
    nr = x3.shape[0]
    t = nr * GRID_W
    tm = rows * GRID_W
    tok = pl.BlockSpec((tm, D_MLSTM), lambda i: (i, 0))
    heads = pl.BlockSpec((HEADS, tm, DH), lambda i: (0, i, 0))
    full = lambda a: pl.BlockSpec(a.shape, lambda i: (0,) * a.ndim)
    return pl.pallas_call(
        _merge_kernel,
        grid=(nr // rows,),
        in_specs=[heads, heads, tok, tok, heads,
                  pl.BlockSpec((rows, GRID_W, D_MODEL), lambda i: (i, 0, 0)),
                  pl.BlockSpec((rows, 1, D_MODEL // 2), lambda i: (i, 0, 0)),
                  pl.BlockSpec((1, GRID_W, D_MODEL // 2), lambda i: (0, 0, 0)),
                  full(nw), full(sk), full(wout), full(gg1), full(g2), full(sh2), full(wr), full(br)],
        out_specs=[pl.BlockSpec((tm, D_MODEL), lambda i: (i, 0)),
                   pl.BlockSpec((tm, D_MODEL), lambda i: (i, 0)),
                   pl.BlockSpec((32, tm), lambda i: (0, i))],
        out_shape=[jax.ShapeDtypeStruct((t, D_MODEL), F32),
                   jax.ShapeDtypeStruct((t, D_MODEL), BF16),
                   jax.ShapeDtypeStruct((32, t), F32)],
        compiler_params=_cparams("parallel"),
        name="merge",
    )(hf, hb, act, z, yf, x3, er3, ec3, nw, sk, wout, gg1, g2, sh2, wr, br)


def _route_kernel(lg_ref, pos_ref, w_ref, cnt_ref, *, sbk):
    lg = lg_ref[...]
    tl = lg.shape[1]
    g = [lg[j:j + 1] for j in range(N_GROUPS)]
    e = [lg[N_GROUPS + j:N_GROUPS + j + 1] for j in range(N_EXPERTS)]
    gmax = jnp.maximum(jnp.maximum(g[0], g[1]), jnp.maximum(g[2], g[3]))
    den = jnp.exp(g[0] - gmax) + jnp.exp(g[1] - gmax) + jnp.exp(g[2] - gmax) + jnp.exp(g[3] - gmax)
    p_sel = 1.0 / den
    sel = []
    free = jnp.ones((1, tl), F32)
    for j in range(N_GROUPS):
        s = jnp.where(g[j] >= gmax, free, 0.0)
        sel.append(s)
        free = free - s
    es = []
    for j in range(EPG):
        es.append(sel[0] * e[j] + sel[1] * e[EPG + j] + sel[2] * e[2 * EPG + j] + sel[3] * e[3 * EPG + j])
    rank = []
    for j in range(EPG):
        rj = jnp.zeros((1, tl), F32)
        for i in range(EPG):
            if i == j:
                continue
            beats = (es[i] >= es[j]) if i < j else (es[i] > es[j])
            rj = rj + jnp.where(beats, 1.0, 0.0)
        rank.append(rj)
    v1 = jnp.maximum(jnp.maximum(es[0], es[1]), jnp.maximum(es[2], es[3]))
    v2 = sum(jnp.where(rank[j] == 1.0, es[j], 0.0) for j in range(EPG))
    tt = jnp.exp(v2 - v1)
    w1 = p_sel / (1.0 + tt)
    w2 = w1 * tt
    w = [jnp.where(rank[j] == 0.0, w1, jnp.where(rank[j] == 1.0, w2, 0.0)) for j in range(EPG)]
    top2 = [jnp.where(rank[j] < 2.0, 1.0, 0.0) for j in range(EPG)]
    mem = jnp.concatenate([sel[gi] * top2[j] for gi in range(N_GROUPS) for j in range(EPG)], axis=0)
    wts = jnp.concatenate([sel[gi] * w[j] for gi in range(N_GROUPS) for j in range(EPG)], axis=0)
    w_ref[...] = wts
    lane = lax.broadcasted_iota(jnp.int32, (N_EXPERTS, tl), 1) & (sbk - 1)
    c = mem
    k = 1
    while k < sbk:
        c = c + jnp.where(lane >= k, pltpu.roll(c, k, 1), 0.0)
        k *= 2
    pos_ref[...] = jnp.where(mem > 0.0, c - 1.0, -1.0)
    lane128 = lax.broadcasted_iota(jnp.int32, (N_EXPERTS, LANES), 1)
    cnt = jnp.zeros((N_EXPERTS, LANES), F32)
    for kb in range(tl // sbk):
        tot = jnp.sum(mem[:, kb * sbk:(kb + 1) * sbk], axis=1, keepdims=True)
        cnt = cnt + jnp.where(lane128 == kb, tot, 0.0)
    cnt_ref[...] = cnt


def _route(lg, *, tl, sbk):
    t = lg.shape[1]
    row = pl.BlockSpec((N_EXPERTS, tl), lambda i: (0, i))
    return pl.pallas_call(
        functools.partial(_route_kernel, sbk=sbk),
        grid=(t // tl,),
        in_specs=[pl.BlockSpec((32, tl), lambda i: (0, i))],
        out_specs=[row, row, pl.BlockSpec((N_EXPERTS, LANES), lambda i: (i, 0))],
        out_shape=[jax.ShapeDtypeStruct((N_EXPERTS, t), F32),
                   jax.ShapeDtypeStruct((N_EXPERTS, t), F32),
                   jax.ShapeDtypeStruct((t // tl * N_EXPERTS, LANES), F32)],
        compiler_params=_cparams("parallel"),
        name="route",
    )(lg)


def _slots_kernel(pos_ref, w_ref, first_ref, qrow_ref, qcol_ref):
    tl = pos_ref.shape[1]
    q0 = jnp.full((1, tl), -1.0, F32)
    q1 = jnp.full((1, tl), -1.0, F32)
    w0 = jnp.zeros((1, tl), F32)
    w1 = jnp.zeros((1, tl), F32)
    seen = jnp.zeros((1, tl), F32)
    for ex in range(N_EXPERTS):
        rk = pos_ref[ex:ex + 1, :]
        wt = w_ref[ex:ex + 1, :]
        m = jnp.where(rk >= 0.0, 1.0, 0.0)
        val = rk + first_ref[ex:ex + 1, :]
        first = (m * (1.0 - seen)) > 0.0
        second = (m * seen) > 0.0
        q0 = jnp.where(first, val, q0)
        w0 = jnp.where(first, wt, w0)
        q1 = jnp.where(second, val, q1)
        w1 = jnp.where(second, wt, w1)
        seen = seen + m
    qrow_ref[...] = jnp.concatenate([q0, q1, jnp.zeros((6, tl), F32)], axis=0)
    qcol_ref[...] = jnp.concatenate([q0, q1, w0, w1, jnp.zeros((LANES - 4, tl), F32)], axis=0).T


def _slots(pos, w, first, *, tl):
    t = pos.shape[1]
    row = pl.BlockSpec((N_EXPERTS, tl), lambda i: (0, i))
    return pl.pallas_call(
        _slots_kernel,
        grid=(t // tl,),
        in_specs=[row, row, row],
        out_specs=[pl.BlockSpec((8, tl), lambda i: (0, i)),
                   pl.BlockSpec((tl, LANES), lambda i: (i, 0))],
        out_shape=[jax.ShapeDtypeStruct((8, t), F32),
                   jax.ShapeDtypeStruct((t, LANES), F32)],
        compiler_params=_cparams("parallel"),
        name="slots",
    )(pos, w, first)


def _mlp(x, wg, wu, wd):
    gt = _dot(x, wg)
    a = ((gt * _sigmoid(gt)) * _dot(x, wu)).astype(BF16)
    return _dot(a, wd).astype(BF16)


def _moe_kernel(tab_ref, h_ref, qrow_ref, wg_ref, wu_ref, wd_ref,
                qcol_ref, x1_ref, gg2_ref, o_ref, xs_ref, *, sbk, slots, csb, off_f, off_m):
    i = pl.program_id(0)
    step = pl.program_id(1)
    nsb = h_ref.shape[0] // sbk
    eps = wg_ref.shape[0]
    exp_steps = N_EXPERTS // eps
    dump = slots

    @pl.when(step == 0)
    def _():
        pid = lax.broadcasted_iota(jnp.int32, (slots, sbk), 0).astype(F32)

        def select(sb, carry):
            q0 = qrow_ref[0, pl.ds(sb, 1), :]
            q1 = qrow_ref[1, pl.ds(sb, 1), :]
            s = jnp.where(q0 == pid, 1.0, jnp.where(q1 == pid, 1.0, 0.0)).astype(BF16)
            row0 = pl.multiple_of(sb * sbk, sbk)
            xs_ref[sb, 0:slots, :] = _dot(s, h_ref[pl.ds(row0, sbk), :]).astype(BF16)
            xs_ref[sb, slots:slots + 32, :] = jnp.zeros((32, D_MODEL), BF16)
            return carry

        lax.fori_loop(0, nsb, select, 0)

    def run_pieces(k, e, r0, halves):
        offs = []
        for sb in range(nsb):
            idx = (i * nsb + sb) * N_EXPERTS + e
            n_rows = tab_ref[idx]
            first = tab_ref[off_f + idx]
            for p in range(halves):
                r = r0 + 16 * p
                offs.append(pl.multiple_of(jnp.where(r < n_rows, first + r, dump + 16 * p), 16))
        x = jnp.concatenate([xs_ref[j // halves, pl.ds(off, 16), :] for j, off in enumerate(offs)], axis=0)
        y = _mlp(x, wg_ref[k], wu_ref[k], wd_ref[k])
        for j, off in enumerate(offs):
            xs_ref[j // halves, pl.ds(off, 16), :] = y[j * 16:(j + 1) * 16]

    @pl.when(jnp.logical_and(step >= 1, step <= exp_steps))
    def _():
        def expert(k, carry):
            e = _expert_slot((step - 1) * eps + k)
            most = tab_ref[off_m + i * N_EXPERTS + e]

            def pair(t, c1):
                run_pieces(k, e, t * 32, 2)
                return c1

            lax.fori_loop(0, most // 32, pair, 0)

            @pl.when(most % 32 != 0)
            def _():
                run_pieces(k, e, (most // 32) * 32, 1)

            return carry

        lax.fori_loop(0, eps, expert, 0)

    @pl.when(step > exp_steps)
    def _():
        lane = lax.broadcasted_iota(jnp.int32, (sbk, slots), 1).astype(F32)
        for k in range(csb):
            sb = (step - 1 - exp_steps) * csb + k
            rows = slice(k * sbk, (k + 1) * sbk)
            qc = qcol_ref[rows, :]
            wmat = (jnp.where(lane == qc[:, 0:1], qc[:, 2:3], 0.0)
                    + jnp.where(lane == qc[:, 1:2], qc[:, 3:4], 0.0)).astype(BF16)
            y = _dot(wmat, xs_ref[sb, 0:slots, :])
            o_ref[rows, :] = x1_ref[rows, :] + _rms(y) * gg2_ref[...]


def _moe(tab, h2, qrow3, wg, wu, wd, qcol, x1, gg2, *, tb, sbk, slots, eps, csb, off_f, off_m):
    t = h2.shape[0]
    nsb = tb // sbk
    comb_steps = nsb // csb
    exp_steps = N_EXPERTS // eps
    wblk = lambda i, s, c: (jnp.where(s == 0, exp_steps - 1, jnp.minimum(s - 1, exp_steps - 1)), 0, 0)
    oblk = lambda i, s, c: (i * comb_steps + jnp.maximum(s - 1 - exp_steps, 0), 0)
    return pl.pallas_call(
        functools.partial(_moe_kernel, sbk=sbk, slots=slots, csb=csb, off_f=off_f, off_m=off_m),
        grid_spec=pltpu.PrefetchScalarGridSpec(
            num_scalar_prefetch=1,
            grid=(t // tb, 1 + exp_steps + comb_steps),
            in_specs=[pl.BlockSpec((tb, D_MODEL), lambda i, s, c: (i, 0)),
                      pl.BlockSpec((8, nsb, sbk), lambda i, s, c: (0, i, 0)),
                      pl.BlockSpec((eps, D_MODEL, D_EXPERT), wblk),
                      pl.BlockSpec((eps, D_MODEL, D_EXPERT), wblk),
                      pl.BlockSpec((eps, D_EXPERT, D_MODEL), wblk),
                      pl.BlockSpec((csb * sbk, LANES), oblk),
                      pl.BlockSpec((csb * sbk, D_MODEL), oblk),
                      pl.BlockSpec((1, D_MODEL), lambda i, s, c: (0, 0))],
            out_specs=pl.BlockSpec((csb * sbk, D_MODEL), oblk),
            scratch_shapes=[pltpu.VMEM((nsb, slots + 32, D_MODEL), BF16)]),
        out_shape=jax.ShapeDtypeStruct((t, D_MODEL), F32),
        compiler_params=pltpu.CompilerParams(dimension_semantics=("parallel", "arbitrary"),
                                             vmem_limit_bytes=MOE_VMEM_LIMIT),
        name="moe",
    )(tab, h2, qrow3, wg, wu, wd, qcol, x1, gg2)


def _slot_tables(cnt, *, nsb):
    rows = ((cnt + 15) // 16) * 16
    first = jnp.cumsum(rows, axis=1) - rows
    most = jnp.max(rows.reshape(-1, nsb, N_EXPERTS), axis=1)
    return rows, first, most


def _pos_tables(rows):
    quarter = D_MODEL // 4
    freq = 1.0 / (POS_BASE ** (np.arange(quarter, dtype=np.float64) / quarter))
    r = np.arange(rows, dtype=np.float64)[:, None] * freq
    cl = np.arange(GRID_W, dtype=np.float64)[:, None] * freq
    er = np.concatenate([np.sin(r), np.cos(r)], axis=-1).astype(np.float32)
    ec = np.concatenate([np.sin(cl), np.cos(cl)], axis=-1).astype(np.float32)
    return jnp.asarray(er[:, None, :]), jnp.asarray(ec[None, :, :])


def _dft_tables(t):
    n = np.arange(CHUNK, dtype=np.int64)
    prod = n[:, None] * n[None, :]
    ang = (prod % CHUNK).astype(np.float64) * (2.0 * np.pi / CHUNK)
    c, s = np.cos(ang), np.sin(ang)
    f1 = np.concatenate([c, s], axis=0)
    a2 = np.concatenate([np.concatenate([c, -s], axis=1), np.concatenate([s, c], axis=1)], axis=0)
    cs = np.concatenate([c, -s], axis=0)
    angw = prod.astype(np.float64) * (2.0 * np.pi / t)
    f32 = lambda a: jnp.asarray(a.astype(np.float32))
    return f32(f1), f32(a2), f32(cs), f32(np.cos(angw)[:, :, None]), f32(np.sin(angw)[:, :, None])


def _blockdiag(w):
    n = w.shape[0]
    size = n * QKV_BLOCK
    spread = np.tile(np.eye(QKV_BLOCK, dtype=np.float32), (1, n))
    rows = jnp.dot(w.reshape(size, QKV_BLOCK), jnp.asarray(spread), precision=lax.Precision.HIGHEST)
    blk = np.arange(size) // QKV_BLOCK
    mask = (blk[:, None] == blk[None, :]).astype(np.float32)
    return rows * jnp.asarray(mask)


def _gate_weights(w_f, b_f, w_b, b_b):
    w = jnp.concatenate([w_f[:, :HEADS], w_b[:, :HEADS], w_f[:, HEADS:], w_b[:, HEADS:]], axis=1).T
    b = jnp.concatenate([b_f[:HEADS], b_b[:HEADS], b_f[HEADS:], b_b[HEADS:]])
    return w.astype(BF16), b[:, None]


def kernel(x, c, ctx, c_ctx, w_ada, b_ada, g_pre_mix, g_post_mix, g_pre_ffn, g_post_ffn,
           w_in, conv_w, conv_b, w_q, w_k, w_v, w_if_fwd, b_if_fwd, w_if_bwd, b_if_bwd,
           mlstm_norm_w, mlstm_skip, w_fourier, w_out, w_router_group, b_router_group,
           w_router_expert, b_router_expert, w_gate, w_up, w_down):
    t = x.shape[1]
    rows = t // GRID_W

    c8 = jnp.concatenate([c, c_ctx[None, :], jnp.zeros((6, D_MODEL), F32)], axis=0)
    mod = _ada(c8, w_ada[0], b_ada[0][None, :])
    shift1, scale1, gate1, shift2, scale2, gate2 = [mod[0:1, k * D_MODEL:(k + 1) * D_MODEL] for k in range(N_MOD)]
    shift1c, scale1c = mod[1:2, 0:D_MODEL], mod[1:2, D_MODEL:2 * D_MODEL]
    g1 = g_pre_mix[0][None, :] * (1.0 + scale1)
    g1c = g_pre_mix[0][None, :] * (1.0 + scale1c)
    gg1 = g_post_mix[0][None, :] * gate1
    g2 = g_pre_ffn[0][None, :] * (1.0 + scale2)
    gg2 = g_post_ffn[0][None, :] * gate2

    er3, ec3 = _pos_tables(rows)
    x3 = x.reshape(rows, GRID_W, D_MODEL)
    ctx3 = ctx.reshape(CTX_LEN // GRID_W, GRID_W, D_MODEL)
    w_in_bf = w_in[0].astype(BF16)

    xm_l, z_l, u_l = _inproj(x3, er3, ec3, g1, shift1, w_in_bf, rows=8, add_pos=True)
    xm_c, _, _ = _inproj(ctx3, er3, ec3, g1c, shift1c, w_in_bf, rows=CTX_LEN // GRID_W, add_pos=False)

    wq = _blockdiag(w_q[0]).astype(BF16)
    wkt = _blockdiag(w_k[0]).T.astype(BF16)
    wv = _blockdiag(w_v[0]).astype(BF16)
    wi, bi = _gate_weights(w_if_fwd[0], b_if_fwd[0], w_if_bwd[0], b_if_bwd[0])
    wiq, wik, wiv = wi[:, :D_MLSTM], wi[:, D_MLSTM:2 * D_MLSTM], wi[:, 2 * D_MLSTM:]
    cb = conv_b[0][None, :]
    q_l, kt_l, v_l, act_l, gp_l = _feat(xm_l, conv_w[0], cb, wq, wkt, wv, wiq, wik, wiv, bi, tm=512)
    q_c, kt_c, v_c, _, gp_c = _feat(xm_c, conv_w[0], cb, wq, wkt, wv, wiq, wik, wiv, bi, tm=CTX_LEN)

    dr_l, gc_l = _gates(gp_l, tl=2048)
    dr_c, gc_c = _gates(gp_c, tl=CTX_LEN)

    c0 = jnp.zeros((2 * HEADS, DH, 2 * DH), F32)
    m0 = jnp.zeros((2 * HEADS, 8, LANES), F32)
    kt_c, kt_l = kt_c.reshape(HEADS, DH, CTX_LEN), kt_l.reshape(HEADS, DH, t)
    c_ctx_fin, m_ctx_fin = _mlstm(q_c, kt_c, v_c, gc_c, dr_c, c0, m0, cps=CTX_LEN // CHUNK, emit=False)
    hf, hb, _, _, wg_bf, wu_bf, wd_bf = _mlstm(q_l, kt_l, v_l, gc_l, dr_l, c_ctx_fin, m_ctx_fin, cps=4, emit=True,
                                               casts=(w_gate[0], w_up[0], w_down[0]))

    f1, a2, cs, cw3, sw3 = _dft_tables(t)
    n1 = t // CHUNK
    yc, ys = _dft1(u_l.reshape(FGROUPS, n1, CHUNK, FCG), f1.astype(BF16), cw3, sw3)
    mix = jnp.einsum('kc,gcd->gkd', cs, w_fourier[0], precision=lax.Precision.HIGHEST)
    mix = (mix * float(1.0 / np.sqrt(float(t) * FCG))).astype(BF16)
    yf = _dft2(yc.reshape(FGROUPS, t, FCG), ys.reshape(FGROUPS, t, FCG), a2.astype(BF16), mix)
    yf = yf.reshape(FGROUPS, t, FCG)

    wr = jnp.concatenate([w_router_group[0], w_router_expert[0],
                          jnp.zeros((D_MODEL, LANES - N_GROUPS - N_EXPERTS), F32)], axis=1)
    br = jnp.concatenate([b_router_group[0], b_router_expert[0],
                          jnp.zeros((LANES - N_GROUPS - N_EXPERTS,), F32)])[None, :]
    x1, h2, lg = _merge(hf, hb, act_l, z_l, yf, x3, er3, ec3,
                        mlstm_norm_w[0][None, :], mlstm_skip[0][None, :], w_out[0].astype(BF16),
                        gg1, g2, shift2, wr.astype(BF16), br, rows=16)
    pos, wts, cnt = _route(lg, tl=MOE_TB, sbk=MOE_SBK)
    nblk, nsb = t // MOE_TB, MOE_TB // MOE_SBK
    cnt = cnt.reshape(nblk, N_EXPERTS, LANES)[:, :, :nsb]
    cnt = jnp.transpose(cnt, (0, 2, 1)).reshape(nblk * nsb, N_EXPERTS).astype(jnp.int32)
    rows, first, most = _slot_tables(cnt, nsb=nsb)
    tab = jnp.concatenate([a.reshape(-1) for a in (rows, first, most)])
    off_f = nblk * nsb * N_EXPERTS
    off_m = 2 * off_f
    first_rows = jnp.repeat(first.T.astype(F32), MOE_SBK, axis=1)
    qrow, qcol = _slots(pos, wts, first_rows, tl=MOE_TB)
    out = _moe(tab, h2, qrow.reshape(8, t // MOE_SBK, MOE_SBK), wg_bf, wu_bf, wd_bf,
               qcol, x1, gg2, tb=MOE_TB, sbk=MOE_SBK, slots=MOE_SLOTS, eps=MOE_EPS, csb=MOE_CSB,
               off_f=off_f, off_m=off_m)
    return out[None]
```

```python
import functools

import numpy as np
import jax
import jax.numpy as jnp
from jax import lax
from jax.experimental import pallas as pl
from jax.experimental.pallas import tpu as pltpu

F32 = jnp.float32
BF16 = jnp.bfloat16

D_MODEL = 1024
SEQ = 16384
GRID_W = 64
CTX_LEN = 256
D_MLSTM = 512
HEADS = 4
DH = 128
QKV_BLOCK = 4
CONV_K = 3
CHUNK = 128
D_FOURIER = 512
FGROUPS = 4
FCG = 128
N_GROUPS = 4
EPG = 4
N_EXPERTS = 16
D_EXPERT = 512
N_MOD = 6
EPS = 1e-6
POS_BASE = 10000.0
LANES = 128
NEG_BIG = -3.0e38

VMEM_LIMIT = 58 * 1024 * 1024
MOE_VMEM_LIMIT = 58 * 1024 * 1024
MOE_TB = 2048
MOE_SBK = 256
MOE_SLOTS = 768
MOE_EPS = 4
MOE_CSB = 2


def _cparams(*sem):
    return pltpu.CompilerParams(dimension_semantics=sem, vmem_limit_bytes=VMEM_LIMIT)


def _dot(a, b):
    return jnp.dot(a, b, preferred_element_type=F32)


def _dot_nt(a, b):
    return lax.dot_general(a, b, (((1,), (1,)), ((), ())), preferred_element_type=F32)


def _split_bf16(a):
    hi = a.astype(BF16)
    lo = (a - hi.astype(F32)).astype(BF16)
    return hi, lo


def _dot3(a, b):
    a_hi, a_lo = _split_bf16(a)
    b_hi, b_lo = _split_bf16(b)
    return _dot(a_hi, b_hi) + (_dot(a_hi, b_lo) + _dot(a_lo, b_hi))


def _sigmoid(x):
    return 1.0 / (1.0 + jnp.exp(-x))


def _rms(x):
    return x * lax.rsqrt(jnp.mean(x * x, axis=-1, keepdims=True) + EPS)


def _ada_kernel(c_ref, w_ref, b_ref, o_ref):
    c = c_ref[...]
    s = c * _sigmoid(c)
    o_ref[...] = _dot3(s, w_ref[...]) + b_ref[...]


def _ada(c8, w, b):
    n = w.shape[1]
    tn = 768
    return pl.pallas_call(
        _ada_kernel,
        grid=(n // tn,),
        in_specs=[pl.BlockSpec((8, D_MODEL), lambda j: (0, 0)),
                  pl.BlockSpec((D_MODEL, tn), lambda j: (0, j)),
                  pl.BlockSpec((1, tn), lambda j: (0, j))],
        out_specs=pl.BlockSpec((8, tn), lambda j: (0, j)),
        out_shape=jax.ShapeDtypeStruct((8, n), F32),
        compiler_params=_cparams("parallel"),
        name="ada",
    )(c8, w, b)


def _add_pos(x3, er_ref, ec_ref):
    r = x3.shape[0]
    pr = jnp.broadcast_to(er_ref[...], (r, GRID_W, D_MODEL // 2))
    pc = jnp.broadcast_to(ec_ref[...], (r, GRID_W, D_MODEL // 2))
    return x3 + jnp.concatenate([pr, pc], axis=-1)


def _inproj_kernel(x_ref, er_ref, ec_ref, g_ref, sh_ref, w_ref, xm_ref, z_ref, u_ref, *, add_pos):
    x3 = x_ref[...]
    if add_pos:
        x3 = _add_pos(x3, er_ref, ec_ref)
    x = x3.reshape(x3.shape[0] * GRID_W, D_MODEL)
    h = _rms(x) * g_ref[...] + sh_ref[...]
    proj = _dot(h.astype(BF16), w_ref[...])
    xm_ref[...] = proj[:, :D_MLSTM].astype(BF16)
    z_ref[...] = proj[:, D_MLSTM:2 * D_MLSTM].astype(BF16)
    for g in range(FGROUPS):
        u_ref[g] = proj[:, 2 * D_MLSTM + g * FCG:2 * D_MLSTM + (g + 1) * FCG]


def _inproj(x3, er3, ec3, g_eff, shift, w_in, *, rows, add_pos):
    nr = x3.shape[0]
    t = nr * GRID_W
    tm = rows * GRID_W
    out = jax.ShapeDtypeStruct((t, D_MLSTM), BF16)
    ospec = pl.BlockSpec((tm, D_MLSTM), lambda i: (i, 0))
    vec = pl.BlockSpec((1, D_MODEL), lambda i: (0, 0))
    return pl.pallas_call(
        functools.partial(_inproj_kernel, add_pos=add_pos),
        grid=(nr // rows,),
        in_specs=[pl.BlockSpec((rows, GRID_W, D_MODEL), lambda i: (i, 0, 0)),
                  pl.BlockSpec((rows, 1, D_MODEL // 2), lambda i: (i, 0, 0)),
                  pl.BlockSpec((1, GRID_W, D_MODEL // 2), lambda i: (0, 0, 0)),
                  vec, vec,
                  pl.BlockSpec(w_in.shape, lambda i: (0, 0))],
        out_specs=[ospec, ospec, pl.BlockSpec((FGROUPS, tm, FCG), lambda i: (0, i, 0))],
        out_shape=[out, out, jax.ShapeDtypeStruct((FGROUPS, t, FCG), F32)],
        compiler_params=_cparams("parallel"),
        name="inproj",
    )(x3, er3, ec3, g_eff, shift, w_in)


def _feat_kernel(xm_ref, prev_ref, next_ref, cw_ref, cb_ref, wq_ref, wkt_ref, wv_ref,
                 wiq_ref, wik_ref, wiv_ref, bi_ref, *rest):
    if len(rest) > 5:
        _dft1_kernel(*rest[:4], *rest[9:])
        rest = rest[4:9]
    q_ref, kt_ref, v_ref, act_ref, g_ref = rest
    i = pl.program_id(0)
    n = pl.num_programs(0)
    xm_bf = xm_ref[...]
    xm = xm_bf.astype(F32)
    tm = xm.shape[0]
    prev_row = prev_ref[...].astype(F32)[15:16, :] * jnp.where(i > 0, 1.0, 0.0)
    next_row = next_ref[...].astype(F32)[0:1, :] * jnp.where(i < n - 1, 1.0, 0.0)
    rid = lax.broadcasted_iota(jnp.int32, (tm, 1), 0)
    x_left = jnp.where(rid == 0, prev_row, pltpu.roll(xm, 1, 0))
    x_right = jnp.where(rid == tm - 1, next_row, pltpu.roll(xm, tm - 1, 0))
    cw = cw_ref[...]
    y = cw[0:1] * x_left + cw[1:2] * xm + cw[2:3] * x_right + cb_ref[...]
    act = (y * _sigmoid(y)).astype(BF16)
    act_ref[...] = act
    q = _dot(act, wq_ref[...])
    kt = _dot_nt(wkt_ref[...], act)
    v = _dot(xm_bf, wv_ref[...])
    q_bf = q.astype(BF16)
    kt_bf = kt.astype(BF16)
    v_bf = v.astype(BF16)
    q_s = (q * (DH ** -0.5)).astype(BF16)
    for hd in range(HEADS):
        q_ref[hd] = q_s[:, hd * DH:(hd + 1) * DH]
        v_ref[hd] = v_bf[:, hd * DH:(hd + 1) * DH]
    kt_ref[...] = kt_bf
    g = _dot_nt(wiq_ref[...], q_bf) + _dot(wik_ref[...], kt_bf) + _dot_nt(wiv_ref[...], v_bf)
    g_ref[...] = g + bi_ref[...]


def _feat(xm, conv_w, conv_b, wq, wkt, wv, wiq, wik, wiv, bi, *, tm, fft=()):
    t = xm.shape[0]
    nb16 = t // 16
    k16 = tm // 16
    full = lambda a: pl.BlockSpec(a.shape, lambda i: (0,) * a.ndim)
    tok = pl.BlockSpec((tm, D_MLSTM), lambda i: (i, 0))
    heads = pl.BlockSpec((HEADS, tm, DH), lambda i: (0, i, 0))
    in_specs = [tok,
                pl.BlockSpec((16, D_MLSTM), lambda i: (jnp.maximum(i * k16 - 1, 0), 0)),
                pl.BlockSpec((16, D_MLSTM), lambda i: (jnp.minimum((i + 1) * k16, nb16 - 1), 0)),
                full(conv_w), full(conv_b), full(wq), full(wkt), full(wv),
                full(wiq), full(wik), full(wiv), full(bi)]
    out_specs = [heads,
                 pl.BlockSpec((D_MLSTM, tm), lambda i: (0, i)),
                 heads, tok,
                 pl.BlockSpec((16, tm), lambda i: (0, i))]
    out_shape = [jax.ShapeDtypeStruct((HEADS, t, DH), BF16),
                 jax.ShapeDtypeStruct((D_MLSTM, t), BF16),
                 jax.ShapeDtypeStruct((HEADS, t, DH), BF16),
                 jax.ShapeDtypeStruct((t, D_MLSTM), BF16),
                 jax.ShapeDtypeStruct((16, t), F32)]
    if fft:
        u4, f1, cw3, sw3 = fft
        blk = pl.BlockSpec((FGROUPS, CHUNK, FFT_ROWS, FCG), lambda j: (0, 0, j, 0))
        tw = pl.BlockSpec((FFT_ROWS, CHUNK, 1), lambda j: (j, 0, 0))
        in_specs += [blk, full(f1), tw, tw]
        out_specs += [blk, blk]
        out_shape += [jax.ShapeDtypeStruct(u4.shape, F32)] * 2
    return pl.pallas_call(
        _feat_kernel,
        grid=(t // tm,),
        in_specs=in_specs,
        out_specs=out_specs,
        out_shape=out_shape,
        compiler_params=_cparams("parallel"),
        name="feat",
    )(xm, xm, xm, conv_w, conv_b, wq, wkt, wv, wiq, wik, wiv, bi, *fft)


def _gates_kernel(g_ref, d_ref, gc_ref):
    g = g_ref[...]
    tl = g.shape[1]
    ig = g[0:8]
    fg = g[8:16]
    lf = jnp.minimum(fg, 0.0) - jnp.log(1.0 + jnp.exp(-jnp.abs(fg)))
    pos = lax.broadcasted_iota(jnp.int32, (8, tl), 1) & (CHUNK - 1)
    is_fwd = lax.broadcasted_iota(jnp.int32, (8, tl), 0) < HEADS

    def scan(x, op, ident):
        xf = x
        xb = x
        k = 1
        while k < CHUNK:
            xf = op(xf, jnp.where(pos >= k, pltpu.roll(xf, k, 1), ident))
            xb = op(xb, jnp.where(pos < CHUNK - k, pltpu.roll(xb, tl - k, 1), ident))
            k *= 2
        return jnp.where(is_fwd, xf, xb)

    b = scan(lf, jnp.add, 0.0)
    d = ig - b
    mloc = b + scan(d, jnp.maximum, NEG_BIG)
    d_ref[...] = d
    stack = jnp.concatenate([b, mloc, jnp.zeros((LANES - 16, tl), F32)], axis=0)
    gc_ref[...] = stack.T


def _gates(g, *, tl):
    t = g.shape[1]
    return pl.pallas_call(
        _gates_kernel,
        grid=(t // tl,),
        in_specs=[pl.BlockSpec((16, tl), lambda i: (0, i))],
        out_specs=[pl.BlockSpec((8, tl), lambda i: (0, i)),
                   pl.BlockSpec((tl, LANES), lambda i: (i, 0))],
        out_shape=[jax.ShapeDtypeStruct((8, t), F32),
                   jax.ShapeDtypeStruct((t, LANES), F32)],
        compiler_params=_cparams("parallel"),
        name="gates",
    )(g)


def _mlstm_kernel(*refs, cps, emit, ncast, nfft=0):
    (qf_ref, kf_ref, vf_ref, gcf_ref, drf_ref,
     qb_ref, kb_ref, vb_ref, gcb_ref, drb_ref, c0_ref, m0_ref) = refs[:12]
    cast_in = refs[12:12 + ncast]
    fft_in = refs[12 + ncast:12 + ncast + nfft]
    outs = refs[12 + ncast + nfft:]
    if emit:
        hf_ref, hb_ref, cfin_ref, mfin_ref = outs[:4]
        outs = outs[4:]
    else:
        cfin_ref, mfin_ref = outs[:2]
        outs = outs[2:]
        hf_ref = hb_ref = None
    cast_out = outs[:ncast]
    fft_out = outs[ncast:ncast + (1 if nfft else 0)]
    c_scr, m_scr = outs[ncast + (1 if nfft else 0):]
    step = pl.program_id(0)

    for src, dst in zip(cast_in, cast_out):
        dst[...] = src[...].astype(BF16)
    if nfft:
        _dft2_kernel(*fft_in, *fft_out)

    @pl.when(step == 0)
    def _():
        c_scr[...] = c0_ref[...]
        m_scr[...] = m0_ref[...]

    ti = lax.broadcasted_iota(jnp.int32, (CHUNK, CHUNK), 0)
    si = lax.broadcasted_iota(jnp.int32, (CHUNK, CHUNK), 1)
    nch = 2 * HEADS
    mask = jnp.concatenate([jnp.broadcast_to((si <= ti)[None], (HEADS, CHUNK, CHUNK)),
                            jnp.broadcast_to((si >= ti)[None], (HEADS, CHUNK, CHUNK))], axis=0)
    ones_blk = jnp.ones((nch, CHUNK, CHUNK), BF16)

    def bdot(a, b):
        return lax.dot_general(a, b, (((2,), (1,)), ((0,), (0,))), preferred_element_type=F32)

    def both(f, g):
        return [f(r) for r in range(HEADS)] + [g(r) for r in range(HEADS, nch)]

    for j in range(cps):
        rf = slice(j * CHUNK, (j + 1) * CHUNK)
        rb = slice((cps - 1 - j) * CHUNK, (cps - j) * CHUNK)
        gcf, gcb = gcf_ref[rf, :], gcb_ref[rb, :]
        drf, drb = drf_ref[:, rf], drb_ref[:, rb]
        lf, lb = CHUNK - 1, 0
        b = jnp.stack(both(lambda r: gcf[:, r:r + 1], lambda r: gcb[:, r:r + 1]))
        mloc = jnp.stack(both(lambda r: gcf[:, 8 + r:9 + r], lambda r: gcb[:, 8 + r:9 + r]))
        drow = jnp.stack(both(lambda r: drf[r:r + 1, :], lambda r: drb[r:r + 1, :]))
        btot = jnp.stack(both(lambda r: gcf[lf:lf + 1, r:r + 1], lambda r: gcb[lb:lb + 1, r:r + 1]))
        amax = jnp.stack(both(lambda r: gcf[lf:lf + 1, 8 + r:9 + r], lambda r: gcb[lb:lb + 1, 8 + r:9 + r]))
        m0 = jnp.stack([m_scr[r][0:1, 0:1] for r in range(nch)])
        c_aug = c_scr[...]
        kt = jnp.concatenate([kf_ref[:, :, rf], kb_ref[:, :, rb]], axis=0)
        vaug = jnp.concatenate([jnp.concatenate([vf_ref[:, rf, :], vb_ref[:, rb, :]], axis=0), ones_blk], axis=2)
        if emit:
            qh = jnp.concatenate([qf_ref[:, rf, :], qb_ref[:, rb, :]], axis=0)
            bm = b + m0
            m_t = jnp.maximum(bm, mloc)
            dmat = jnp.where(mask, jnp.exp((b - m_t) + drow), 0.0)
            smat = (bdot(qh, kt) * dmat).astype(BF16)
            q_in = (qh.astype(F32) * jnp.exp(bm - m_t)).astype(BF16)
            num = bdot(jnp.concatenate([smat, q_in], axis=2),
                       jnp.concatenate([vaug, c_aug.astype(BF16)], axis=1))
            den = jnp.maximum(jnp.abs(num[:, :, DH:]), jnp.exp(-m_t))
            h = (num[:, :, :DH] / den).astype(hf_ref.dtype)
            hf_ref[:, rf, :] = h[:HEADS]
            hb_ref[:, rb, :] = h[HEADS:]
        m_new = jnp.maximum(btot + m0, amax)
        decay = jnp.exp(btot + m0 - m_new)
        kw = (kt.astype(F32) * jnp.exp(btot + drow - m_new)).astype(BF16)
        c_scr[...] = decay * c_aug + bdot(kw, vaug)
        m_scr[...] = jnp.broadcast_to(m_new, (nch, 8, LANES))

    @pl.when(step == pl.num_programs(0) - 1)
    def _():
        cfin_ref[...] = c_scr[...]
        mfin_ref[...] = m_scr[...]


def _expert_slot(e):
    return (e % EPG) * N_GROUPS + e // EPG


def _mlstm(q, kt, v, gc, dr, c0, m0, *, cps, emit, casts=(), fft=()):
    t = q.shape[1]
    cb = cps * CHUNK
    nb = t // cb
    fwd_r = lambda i: (i, 0)
    bwd_r = lambda i: (nb - 1 - i, 0)
    fwd_c = lambda i: (0, i)
    bwd_c = lambda i: (0, nb - 1 - i)
    tok = lambda f: pl.BlockSpec((HEADS, cb, DH), lambda i, f=f: (0, f(i)[0], 0))
    in_specs = []
    for fr, fc in ((fwd_r, fwd_c), (bwd_r, bwd_c)):
        in_specs += [tok(fr), pl.BlockSpec((HEADS, DH, cb), lambda i, fc=fc: (0, 0, fc(i)[1])), tok(fr),
                     pl.BlockSpec((cb, LANES), fr), pl.BlockSpec((8, cb), fc)]
    cshape = (2 * HEADS, DH, 2 * DH)
    mshape = (2 * HEADS, 8, LANES)
    cspec = pl.BlockSpec(cshape, lambda i: (0, 0, 0))
    mspec = pl.BlockSpec(mshape, lambda i: (0, 0, 0))
    in_specs += [cspec, mspec]
    out_specs = [cspec, mspec]
    out_shape = [jax.ShapeDtypeStruct(cshape, F32), jax.ShapeDtypeStruct(mshape, F32)]
    if emit:
        out_specs = [tok(fwd_r), tok(bwd_r)] + out_specs
        out_shape = [jax.ShapeDtypeStruct((HEADS, t, DH), BF16)] * 2 + out_shape
    for a in casts:
        per = nb // a.shape[0]
        blk = (1, a.shape[1] // per, a.shape[2])
        in_specs.append(pl.BlockSpec(blk, lambda i, per=per: (i // per, i % per, 0)))
        out_specs.append(pl.BlockSpec(blk, lambda i, per=per: (_expert_slot(i // per), i % per, 0)))
        out_shape.append(jax.ShapeDtypeStruct(a.shape, BF16))
    if fft:
        yc, ys, a2, mix = fft
        tokf = pl.BlockSpec((FGROUPS, FFT_ROWS * CHUNK, FCG), lambda i: (0, i, 0))
        fullf = lambda a: pl.BlockSpec(a.shape, lambda i: (0,) * a.ndim)
        in_specs += [tokf, tokf, fullf(a2), fullf(mix)]
        out_specs.append(pl.BlockSpec((FGROUPS, CHUNK, FFT_ROWS, FCG), lambda i: (0, 0, i, 0)))
        out_shape.append(jax.ShapeDtypeStruct((FGROUPS, CHUNK, t // CHUNK, FCG), F32))
    return pl.pallas_call(
        functools.partial(_mlstm_kernel, cps=cps, emit=emit, ncast=len(casts), nfft=len(fft)),
        grid=(nb,),
        in_specs=in_specs,
        out_specs=out_specs,
        out_shape=out_shape,
        scratch_shapes=[pltpu.VMEM(cshape, F32), pltpu.VMEM(mshape, F32)],
        compiler_params=_cparams("arbitrary"),
        name="mlstm",
    )(q, kt, v, gc, dr, q, kt, v, gc, dr, c0, m0, *casts, *fft)


FFT_ROWS = 8


def _dft1_kernel(u_ref, f_ref, cw_ref, sw_ref, yc_ref, ys_ref):
    f = f_ref[...]
    rows = CHUNK * FFT_ROWS
    u2 = u_ref.reshape(FGROUPS * rows, FCG)
    yc2 = yc_ref.reshape(FGROUPS * rows, FCG)
    ys2 = ys_ref.reshape(FGROUPS * rows, FCG)
    for s in range(FFT_ROWS):
        pick = [pl.ds(g * rows + s, CHUNK, stride=FFT_ROWS) for g in range(FGROUPS)]
        x = jnp.concatenate([u2[p, :] for p in pick], axis=1).astype(BF16)
        y = _dot(f, x)
        cw = cw_ref[s]
        sw = sw_ref[s]
        pr = y[:CHUNK] * cw - y[CHUNK:] * sw
        pi = y[:CHUNK] * sw + y[CHUNK:] * cw
        for g in range(FGROUPS):
            yc2[pick[g], :] = pr[:, g * FCG:(g + 1) * FCG]
            ys2[pick[g], :] = pi[:, g * FCG:(g + 1) * FCG]


def _dft1(u4, f1, cw3, sw3):
    blk = pl.BlockSpec((FGROUPS, CHUNK, FFT_ROWS, FCG), lambda j: (0, 0, j, 0))
    tw = pl.BlockSpec((FFT_ROWS, CHUNK, 1), lambda j: (j, 0, 0))
    out = jax.ShapeDtypeStruct(u4.shape, F32)
    return pl.pallas_call(
        _dft1_kernel,
        grid=(u4.shape[2] // FFT_ROWS,),
        in_specs=[blk, pl.BlockSpec(f1.shape, lambda j: (0, 0)), tw, tw],
        out_specs=[blk, blk],
        out_shape=[out, out],
        compiler_params=_cparams("parallel"),
        name="dft1",
    )(u4, f1, cw3, sw3)


def _dft2_kernel(yc_ref, ys_ref, a2_ref, mix_ref, o_ref):
    a2 = a2_ref[...]
    rows = CHUNK * FFT_ROWS
    o2 = o_ref.reshape(FGROUPS * rows, FCG)
    for kk in range(FFT_ROWS):
        blk = slice(kk * CHUNK, (kk + 1) * CHUNK)
        yc = jnp.concatenate([yc_ref[g, blk, :] for g in range(FGROUPS)], axis=1)
        ys = jnp.concatenate([ys_ref[g, blk, :] for g in range(FGROUPS)], axis=1)
        p = jnp.concatenate([yc, ys], axis=0).astype(BF16)
        x = _dot(a2, p).astype(BF16)
        for g in range(FGROUPS):
            cols = slice(g * FCG, (g + 1) * FCG)
            cat = jnp.concatenate([x[:CHUNK, cols], x[CHUNK:, cols]], axis=1)
            o2[pl.ds(g * rows + kk, CHUNK, stride=FFT_ROWS), :] = _dot(cat, mix_ref[g])


def _dft2(yc, ys, a2, mix):
    t = yc.shape[1]
    n1 = t // CHUNK
    tok = pl.BlockSpec((FGROUPS, FFT_ROWS * CHUNK, FCG), lambda i: (0, i, 0))
    full = lambda a: pl.BlockSpec(a.shape, lambda i: (0,) * a.ndim)
    return pl.pallas_call(
        _dft2_kernel,
        grid=(n1 // FFT_ROWS,),
        in_specs=[tok, tok, full(a2), full(mix)],
        out_specs=pl.BlockSpec((FGROUPS, CHUNK, FFT_ROWS, FCG), lambda i: (0, 0, i, 0)),
        out_shape=jax.ShapeDtypeStruct((FGROUPS, CHUNK, n1, FCG), F32),
        compiler_params=_cparams("parallel"),
        name="dft2",
    )(yc, ys, a2, mix)


def _merge_kernel(hf_ref, hb_ref, act_ref, z_ref, yf_ref, x_ref, er_ref, ec_ref,
                  nw_ref, sk_ref, wout_ref, gg1_ref, g2_ref, sh2_ref, wr_ref, br_ref,
                  x1_ref, h2_ref, lg_ref):
    parts = []
    for hd in range(HEADS):
        hh = hf_ref[hd].astype(F32) + hb_ref[hd].astype(F32)
        dl = hh - jnp.mean(hh, axis=-1, keepdims=True)
        var = jnp.mean(dl * dl, axis=-1, keepdims=True)
        parts.append(dl * lax.rsqrt(var + EPS))
    hn = jnp.concatenate(parts, axis=-1)
    z = z_ref[...].astype(F32)
    m = (hn * nw_ref[...] + sk_ref[...] * act_ref[...].astype(F32)) * (z * _sigmoid(z))
    cat = jnp.concatenate([m.astype(BF16)] + [yf_ref[g].astype(BF16) for g in range(FGROUPS)], axis=-1)
    y = _dot(cat, wout_ref[...])
    x3 = _add_pos(x_ref[...], er_ref, ec_ref)
    xp = x3.reshape(x3.shape[0] * GRID_W, D_MODEL)
    x1 = xp + _rms(y) * gg1_ref[...]
    x1_ref[...] = x1
    h2 = _rms(x1) * g2_ref[...] + sh2_ref[...]
    h2_ref[...] = h2.astype(BF16)
    lg = _dot(h2.astype(BF16), wr_ref[...]) + br_ref[...]
    lg_ref[...] = lg.T[:32]


def _merge(hf, hb, act, z, yf, x3, er3, ec3, nw, sk, wout, gg1, g2, sh2, wr, br, *, rows):
    nr = x3.shape[0]
    t = nr * GRID_W
    tm = rows * GRID_W
    tok = pl.BlockSpec((tm, D_MLSTM), lambda i: (i, 0))
    heads = pl.BlockSpec((HEADS, tm, DH), lambda i: (0, i, 0))
    full = lambda a: pl.BlockSpec(a.shape, lambda i: (0,) * a.ndim)
    return pl.pallas_call(
        _merge_kernel,
        grid=(nr // rows,),
        in_specs=[heads, heads, tok, tok, heads,
                  pl.BlockSpec((rows, GRID_W, D_MODEL), lambda i: (i, 0, 0)),
                  pl.BlockSpec((rows, 1, D_MODEL // 2), lambda i: (i, 0, 0)),
                  pl.BlockSpec((1, GRID_W, D_MODEL // 2), lambda i: (0, 0, 0)),
                  full(nw), full(sk), full(wout), full(gg1), full(g2), full(sh2), full(wr), full(br)],
        out_specs=[pl.BlockSpec((tm, D_MODEL), lambda i: (i, 0)),
                   pl.BlockSpec((tm, D_MODEL), lambda i: (i, 0)),
                   pl.BlockSpec((32, tm), lambda i: (0, i))],
        out_shape=[jax.ShapeDtypeStruct((t, D_MODEL), F32),
                   jax.ShapeDtypeStruct((t, D_MODEL), BF16),
                   jax.ShapeDtypeStruct((32, t), F32)],
        compiler_params=_cparams("parallel"),
        name="merge",
    )(hf, hb, act, z, yf, x3, er3, ec3, nw, sk, wout, gg1, g2, sh2, wr, br)


def _route_kernel(lg_ref, pos_ref, w_ref, cnt_ref, *, sbk):
    lg = lg_ref[...]
    tl = lg.shape[1]
    g = [lg[j:j + 1] for j in range(N_GROUPS)]
    e = [lg[N_GROUPS + j:N_GROUPS + j + 1] for j in range(N_EXPERTS)]
    gmax = jnp.maximum(jnp.maximum(g[0], g[1]), jnp.maximum(g[2], g[3]))
    den = jnp.exp(g[0] - gmax) + jnp.exp(g[1] - gmax) + jnp.exp(g[2] - gmax) + jnp.exp(g[3] - gmax)
    p_sel = 1.0 / den
    sel = []
    free = jnp.ones((1, tl), F32)
    for j in range(N_GROUPS):
        s = jnp.where(g[j] >= gmax, free, 0.0)
        sel.append(s)
        free = free - s
    es = []
    for j in range(EPG):
        es.append(sel[0] * e[j] + sel[1] * e[EPG + j] + sel[2] * e[2 * EPG + j] + sel[3] * e[3 * EPG + j])
    rank = []
    for j in range(EPG):
        rj = jnp.zeros((1, tl), F32)
        for i in range(EPG):
            if i == j:
                continue
            beats = (es[i] >= es[j]) if i < j else (es[i] > es[j])
            rj = rj + jnp.where(beats, 1.0, 0.0)
        rank.append(rj)
    v1 = jnp.maximum(jnp.maximum(es[0], es[1]), jnp.maximum(es[2], es[3]))
    v2 = sum(jnp.where(rank[j] == 1.0, es[j], 0.0) for j in range(EPG))
    tt = jnp.exp(v2 - v1)
    w1 = p_sel / (1.0 + tt)
    w2 = w1 * tt
    w = [jnp.where(rank[j] == 0.0, w1, jnp.where(rank[j] == 1.0, w2, 0.0)) for j in range(EPG)]
    top2 = [jnp.where(rank[j] < 2.0, 1.0, 0.0) for j in range(EPG)]
    mem = jnp.concatenate([sel[gi] * top2[j] for gi in range(N_GROUPS) for j in range(EPG)], axis=0)
    wts = jnp.concatenate([sel[gi] * w[j] for gi in range(N_GROUPS) for j in range(EPG)], axis=0)
    w_ref[...] = wts
    lane = lax.broadcasted_iota(jnp.int32, (N_EXPERTS, tl), 1) & (sbk - 1)
    c = mem
    k = 1
    while k < sbk:
        c = c + jnp.where(lane >= k, pltpu.roll(c, k, 1), 0.0)
        k *= 2
    pos_ref[...] = jnp.where(mem > 0.0, c - 1.0, -1.0)
    lane128 = lax.broadcasted_iota(jnp.int32, (N_EXPERTS, LANES), 1)
    cnt = jnp.zeros((N_EXPERTS, LANES), F32)
    for kb in range(tl // sbk):
        tot = jnp.sum(mem[:, kb * sbk:(kb + 1) * sbk], axis=1, keepdims=True)
        cnt = cnt + jnp.where(lane128 == kb, tot, 0.0)
    cnt_ref[...] = cnt


def _route(lg, *, tl, sbk):
    t = lg.shape[1]
    row = pl.BlockSpec((N_EXPERTS, tl), lambda i: (0, i))
    return pl.pallas_call(
        functools.partial(_route_kernel, sbk=sbk),
        grid=(t // tl,),
        in_specs=[pl.BlockSpec((32, tl), lambda i: (0, i))],
        out_specs=[row, row, pl.BlockSpec((N_EXPERTS, LANES), lambda i: (i, 0))],
        out_shape=[jax.ShapeDtypeStruct((N_EXPERTS, t), F32),
                   jax.ShapeDtypeStruct((N_EXPERTS, t), F32),
                   jax.ShapeDtypeStruct((t // tl * N_EXPERTS, LANES), F32)],
        compiler_params=_cparams("parallel"),
        name="route",
    )(lg)


def _slots_kernel(pos_ref, w_ref, first_ref, qrow_ref, qcol_ref):
    tl = pos_ref.shape[1]
    q0 = jnp.full((1, tl), -1.0, F32)
    q1 = jnp.full((1, tl), -1.0, F32)
    w0 = jnp.zeros((1, tl), F32)
    w1 = jnp.zeros((1, tl), F32)
    seen = jnp.zeros((1, tl), F32)
    for ex in range(N_EXPERTS):
        rk = pos_ref[ex:ex + 1, :]
        wt = w_ref[ex:ex + 1, :]
        m = jnp.where(rk >= 0.0, 1.0, 0.0)
        val = rk + first_ref[ex:ex + 1, :]
        first = (m * (1.0 - seen)) > 0.0
        second = (m * seen) > 0.0
        q0 = jnp.where(first, val, q0)
        w0 = jnp.where(first, wt, w0)
        q1 = jnp.where(second, val, q1)
        w1 = jnp.where(second, wt, w1)
        seen = seen + m
    qrow_ref[...] = jnp.concatenate([q0, q1, jnp.zeros((6, tl), F32)], axis=0)
    qcol_ref[...] = jnp.concatenate([q0, q1, w0, w1, jnp.zeros((LANES - 4, tl), F32)], axis=0).T


def _slots(pos, w, first, *, tl):
    t = pos.shape[1]
    row = pl.BlockSpec((N_EXPERTS, tl), lambda i: (0, i))
    return pl.pallas_call(
        _slots_kernel,
        grid=(t // tl,),
        in_specs=[row, row, row],
        out_specs=[pl.BlockSpec((8, tl), lambda i: (0, i)),
                   pl.BlockSpec((tl, LANES), lambda i: (i, 0))],
        out_shape=[jax.ShapeDtypeStruct((8, t), F32),
                   jax.ShapeDtypeStruct((t, LANES), F32)],
        compiler_params=_cparams("parallel"),
        name="slots",
    )(pos, w, first)


def _mlp(x, wg, wu, wd):
    gt = _dot(x, wg)
    a = ((gt * _sigmoid(gt)) * _dot(x, wu)).astype(BF16)
    return _dot(a, wd).astype(BF16)


def _moe_kernel(tab_ref, h_ref, qrow_ref, wg_ref, wu_ref, wd_ref,
                qcol_ref, x1_ref, gg2_ref, o_ref, xs_ref, *, sbk, slots, csb, off_f, off_m):
    i = pl.program_id(0)
    step = pl.program_id(1)
    nsb = h_ref.shape[0] // sbk
    eps = wg_ref.shape[0]
    exp_steps = N_EXPERTS // eps
    dump = slots

    @pl.when(step == 0)
    def _():
        pid = lax.broadcasted_iota(jnp.int32, (slots, sbk), 0).astype(F32)

        def select(sb, carry):
            q0 = qrow_ref[0, pl.ds(sb, 1), :]
            q1 = qrow_ref[1, pl.ds(sb, 1), :]
            s = jnp.where(q0 == pid, 1.0, jnp.where(q1 == pid, 1.0, 0.0)).astype(BF16)
            row0 = pl.multiple_of(sb * sbk, sbk)
            xs_ref[sb, 0:slots, :] = _dot(s, h_ref[pl.ds(row0, sbk), :]).astype(BF16)
            xs_ref[sb, slots:slots + 32, :] = jnp.zeros((32, D_MODEL), BF16)
            return carry

        lax.fori_loop(0, nsb, select, 0)

    def run_pieces(k, e, r0, halves):
        offs = []
        for sb in range(nsb):
            idx = (i * nsb + sb) * N_EXPERTS + e
            n_rows = tab_ref[idx]
            first = tab_ref[off_f + idx]
            for p in range(halves):
                r = r0 + 16 * p
                offs.append(pl.multiple_of(jnp.where(r < n_rows, first + r, dump + 16 * p), 16))
        x = jnp.concatenate([xs_ref[j // halves, pl.ds(off, 16), :] for j, off in enumerate(offs)], axis=0)
        y = _mlp(x, wg_ref[k], wu_ref[k], wd_ref[k])
        for j, off in enumerate(offs):
            xs_ref[j // halves, pl.ds(off, 16), :] = y[j * 16:(j + 1) * 16]

    @pl.when(jnp.logical_and(step >= 1, step <= exp_steps))
    def _():
        def expert(k, carry):
            e = _expert_slot((step - 1) * eps + k)
            most = tab_ref[off_m + i * N_EXPERTS + e]

            def pair(t, c1):
                run_pieces(k, e, t * 32, 2)
                return c1

            lax.fori_loop(0, most // 32, pair, 0)

            @pl.when(most % 32 != 0)
            def _():
                run_pieces(k, e, (most // 32) * 32, 1)

            return carry

        lax.fori_loop(0, eps, expert, 0)

    @pl.when(step > exp_steps)
    def _():
        lane = lax.broadcasted_iota(jnp.int32, (sbk, slots), 1).astype(F32)
        for k in range(csb):
            sb = (step - 1 - exp_steps) * csb + k
            rows = slice(k * sbk, (k + 1) * sbk)
            qc = qcol_ref[rows, :]
            wmat = (jnp.where(lane == qc[:, 0:1], qc[:, 2:3], 0.0)
                    + jnp.where(lane == qc[:, 1:2], qc[:, 3:4], 0.0)).astype(BF16)
            y = _dot(wmat, xs_ref[sb, 0:slots, :])
            o_ref[rows, :] = x1_ref[rows, :] + _rms(y) * gg2_ref[...]


def _moe(tab, h2, qrow3, wg, wu, wd, qcol, x1, gg2, *, tb, sbk, slots, eps, csb, off_f, off_m):
    t = h2.shape[0]
    nsb = tb // sbk
    comb_steps = nsb // csb
    exp_steps = N_EXPERTS // eps
    wblk = lambda i, s, c: (jnp.where(s == 0, exp_steps - 1, jnp.minimum(s - 1, exp_steps - 1)), 0, 0)
    oblk = lambda i, s, c: (i * comb_steps + jnp.maximum(s - 1 - exp_steps, 0), 0)
    return pl.pallas_call(
        functools.partial(_moe_kernel, sbk=sbk, slots=slots, csb=csb, off_f=off_f, off_m=off_m),
        grid_spec=pltpu.PrefetchScalarGridSpec(
            num_scalar_prefetch=1,
            grid=(t // tb, 1 + exp_steps + comb_steps),
            in_specs=[pl.BlockSpec((tb, D_MODEL), lambda i, s, c: (i, 0)),
                      pl.BlockSpec((8, nsb, sbk), lambda i, s, c: (0, i, 0)),
                      pl.BlockSpec((eps, D_MODEL, D_EXPERT), wblk),
                      pl.BlockSpec((eps, D_MODEL, D_EXPERT), wblk),
                      pl.BlockSpec((eps, D_EXPERT, D_MODEL), wblk),
                      pl.BlockSpec((csb * sbk, LANES), oblk),
                      pl.BlockSpec((csb * sbk, D_MODEL), oblk),
                      pl.BlockSpec((1, D_MODEL), lambda i, s, c: (0, 0))],
            out_specs=pl.BlockSpec((csb * sbk, D_MODEL), oblk),
            scratch_shapes=[pltpu.VMEM((nsb, slots + 32, D_MODEL), BF16)]),
        out_shape=jax.ShapeDtypeStruct((t, D_MODEL), F32),
        compiler_params=pltpu.CompilerParams(dimension_semantics=("parallel", "arbitrary"),
                                             vmem_limit_bytes=MOE_VMEM_LIMIT),
        name="moe",
    )(tab, h2, qrow3, wg, wu, wd, qcol, x1, gg2)


def _slot_tables(cnt, *, nsb):
    rows = ((cnt + 15) // 16) * 16
    first = jnp.cumsum(rows, axis=1) - rows
    most = jnp.max(rows.reshape(-1, nsb, N_EXPERTS), axis=1)
    return rows, first, most


def _pos_tables(rows):
    quarter = D_MODEL // 4
    freq = 1.0 / (POS_BASE ** (np.arange(quarter, dtype=np.float64) / quarter))
    r = np.arange(rows, dtype=np.float64)[:, None] * freq
    cl = np.arange(GRID_W, dtype=np.float64)[:, None] * freq
    er = np.concatenate([np.sin(r), np.cos(r)], axis=-1).astype(np.float32)
    ec = np.concatenate([np.sin(cl), np.cos(cl)], axis=-1).astype(np.float32)
    return jnp.asarray(er[:, None, :]), jnp.asarray(ec[None, :, :])


def _dft_tables(t):
    n = np.arange(CHUNK, dtype=np.int64)
    prod = n[:, None] * n[None, :]
    ang = (prod % CHUNK).astype(np.float64) * (2.0 * np.pi / CHUNK)
    c, s = np.cos(ang), np.sin(ang)
    f1 = np.concatenate([c, s], axis=0)
    a2 = np.concatenate([np.concatenate([c, -s], axis=1), np.concatenate([s, c], axis=1)], axis=0)
    cs = np.concatenate([c, -s], axis=0)
    angw = prod.astype(np.float64) * (2.0 * np.pi / t)
    f32 = lambda a: jnp.asarray(a.astype(np.float32))
    return f32(f1), f32(a2), f32(cs), f32(np.cos(angw)[:, :, None]), f32(np.sin(angw)[:, :, None])


def _blockdiag(w):
    n = w.shape[0]
    size = n * QKV_BLOCK
    spread = np.tile(np.eye(QKV_BLOCK, dtype=np.float32), (1, n))
    rows = jnp.dot(w.reshape(size, QKV_BLOCK), jnp.asarray(spread), precision=lax.Precision.HIGHEST)
    blk = np.arange(size) // QKV_BLOCK
    mask = (blk[:, None] == blk[None, :]).astype(np.float32)
    return rows * jnp.asarray(mask)


def _gate_weights(w_f, b_f, w_b, b_b):
    w = jnp.concatenate([w_f[:, :HEADS], w_b[:, :HEADS], w_f[:, HEADS:], w_b[:, HEADS:]], axis=1).T
    b = jnp.concatenate([b_f[:HEADS], b_b[:HEADS], b_f[HEADS:], b_b[HEADS:]])
    return w.astype(BF16), b[:, None]


def kernel(x, c, ctx, c_ctx, w_ada, b_ada, g_pre_mix, g_post_mix, g_pre_ffn, g_post_ffn,
           w_in, conv_w, conv_b, w_q, w_k, w_v, w_if_fwd, b_if_fwd, w_if_bwd, b_if_bwd,
           mlstm_norm_w, mlstm_skip, w_fourier, w_out, w_router_group, b_router_group,
           w_router_expert, b_router_expert, w_gate, w_up, w_down):
    t = x.shape[1]
    rows = t // GRID_W

    c8 = jnp.concatenate([c, c_ctx[None, :], jnp.zeros((6, D_MODEL), F32)], axis=0)
    mod = _ada(c8, w_ada[0], b_ada[0][None, :])
    shift1, scale1, gate1, shift2, scale2, gate2 = [mod[0:1, k * D_MODEL:(k + 1) * D_MODEL] for k in range(N_MOD)]
    shift1c, scale1c = mod[1:2, 0:D_MODEL], mod[1:2, D_MODEL:2 * D_MODEL]
    g1 = g_pre_mix[0][None, :] * (1.0 + scale1)
    g1c = g_pre_mix[0][None, :] * (1.0 + scale1c)
    gg1 = g_post_mix[0][None, :] * gate1
    g2 = g_pre_ffn[0][None, :] * (1.0 + scale2)
    gg2 = g_post_ffn[0][None, :] * gate2

    er3, ec3 = _pos_tables(rows)
    x3 = x.reshape(rows, GRID_W, D_MODEL)
    ctx3 = ctx.reshape(CTX_LEN // GRID_W, GRID_W, D_MODEL)
    w_in_bf = w_in[0].astype(BF16)

    xm_l, z_l, u_l = _inproj(x3, er3, ec3, g1, shift1, w_in_bf, rows=8, add_pos=True)
    xm_c, _, _ = _inproj(ctx3, er3, ec3, g1c, shift1c, w_in_bf, rows=CTX_LEN // GRID_W, add_pos=False)

    wq = _blockdiag(w_q[0]).astype(BF16)
    wkt = _blockdiag(w_k[0]).T.astype(BF16)
    wv = _blockdiag(w_v[0]).astype(BF16)
    wi, bi = _gate_weights(w_if_fwd[0], b_if_fwd[0], w_if_bwd[0], b_if_bwd[0])
    wiq, wik, wiv = wi[:, :D_MLSTM], wi[:, D_MLSTM:2 * D_MLSTM], wi[:, 2 * D_MLSTM:]
    cb = conv_b[0][None, :]
    f1, a2, cs, cw3, sw3 = _dft_tables(t)
    n1 = t // CHUNK
    q_l, kt_l, v_l, act_l, gp_l, yc, ys = _feat(
        xm_l, conv_w[0], cb, wq, wkt, wv, wiq, wik, wiv, bi, tm=t // (n1 // FFT_ROWS),
        fft=(u_l.reshape(FGROUPS, n1, CHUNK, FCG), f1.astype(BF16), cw3, sw3))
    q_c, kt_c, v_c, _, gp_c = _feat(xm_c, conv_w[0], cb, wq, wkt, wv, wiq, wik, wiv, bi, tm=CTX_LEN)

    dr_l, gc_l = _gates(gp_l, tl=2048)
    dr_c, gc_c = _gates(gp_c, tl=CTX_LEN)

    mix = jnp.einsum('kc,gcd->gkd', cs, w_fourier[0], precision=lax.Precision.HIGHEST)
    mix = (mix * float(1.0 / np.sqrt(float(t) * FCG))).astype(BF16)

    c0 = jnp.zeros((2 * HEADS, DH, 2 * DH), F32)
    m0 = jnp.zeros((2 * HEADS, 8, LANES), F32)
    kt_c, kt_l = kt_c.reshape(HEADS, DH, CTX_LEN), kt_l.reshape(HEADS, DH, t)
    c_ctx_fin, m_ctx_fin = _mlstm(q_c, kt_c, v_c, gc_c, dr_c, c0, m0, cps=CTX_LEN // CHUNK, emit=False)
    hf, hb, _, _, wg_bf, wu_bf, wd_bf, yf = _mlstm(
        q_l, kt_l, v_l, gc_l, dr_l, c_ctx_fin, m_ctx_fin, cps=8, emit=True,
        casts=(w_gate[0], w_up[0], w_down[0]),
        fft=(yc.reshape(FGROUPS, t, FCG), ys.reshape(FGROUPS, t, FCG), a2.astype(BF16), mix))
    yf = yf.reshape(FGROUPS, t, FCG)

    wr = jnp.concatenate([w_router_group[0], w_router_expert[0],
                          jnp.zeros((D_MODEL, LANES - N_GROUPS - N_EXPERTS), F32)], axis=1)
    br = jnp.concatenate([b_router_group[0], b_router_expert[0],
                          jnp.zeros((LANES - N_GROUPS - N_EXPERTS,), F32)])[None, :]
    x1, h2, lg = _merge(hf, hb, act_l, z_l, yf, x3, er3, ec3,
                        mlstm_norm_w[0][None, :], mlstm_skip[0][None, :], w_out[0].astype(BF16),
                        gg1, g2, shift2, wr.astype(BF16), br, rows=16)
    pos, wts, cnt = _route(lg, tl=MOE_TB, sbk=MOE_SBK)
    nblk, nsb = t // MOE_TB, MOE_TB // MOE_SBK
    cnt = cnt.reshape(nblk, N_EXPERTS, LANES)[:, :, :nsb]
    cnt = jnp.transpose(cnt, (0, 2, 1)).reshape(nblk * nsb, N_EXPERTS).astype(jnp.int32)
    rows, first, most = _slot_tables(cnt, nsb=nsb)
    tab = jnp.concatenate([a.reshape(-1) for a in (rows, first, most)])
    off_f = nblk * nsb * N_EXPERTS
    off_m = 2 * off_f
    first_rows = jnp.repeat(first.T.astype(F32), MOE_SBK, axis=1)
    qrow, qcol = _slots(pos, wts, first_rows, tl=MOE_TB)
    out = _moe(tab, h2, qrow.reshape(8, t // MOE_SBK, MOE_SBK), wg_bf, wu_bf, wd_bf,
               qcol, x1, gg2, tb=MOE_TB, sbk=MOE_SBK, slots=MOE_SLOTS, eps=MOE_EPS, csb=MOE_CSB,
               off_f=off_f, off_m=off_m)
    return out[None]
```

```python
import functools

import numpy as np
import jax
import jax.numpy as jnp
from jax import lax
from jax.experimental import pallas as pl
from jax.experimental.pallas import tpu as pltpu

F32 = jnp.float32
BF16 = jnp.bfloat16

D_MODEL = 1024
SEQ = 16384
GRID_W = 64
CTX_LEN = 256
D_MLSTM = 512
HEADS = 4
DH = 128
QKV_BLOCK = 4
CONV_K = 3
CHUNK = 128
D_FOURIER = 512
FGROUPS = 4
FCG = 128
N_GROUPS = 4
EPG = 4
N_EXPERTS = 16
D_EXPERT = 512
N_MOD = 6
EPS = 1e-6
POS_BASE = 10000.0
LANES = 128
NEG_BIG = -3.0e38

VMEM_LIMIT = 58 * 1024 * 1024
MOE_VMEM_LIMIT = 58 * 1024 * 1024
MOE_TB = 2048
MOE_SBK = 256
MOE_SLOTS = 768
MOE_EPS = 4
MOE_CSB = 2


def _cparams(*sem):
    return pltpu.CompilerParams(dimension_semantics=sem, vmem_limit_bytes=VMEM_LIMIT)


def _dot(a, b):
    return jnp.dot(a, b, preferred_element_type=F32)


def _dot_nt(a, b):
    return lax.dot_general(a, b, (((1,), (1,)), ((), ())), preferred_element_type=F32)


def _split_bf16(a):
    hi = a.astype(BF16)
    lo = (a - hi.astype(F32)).astype(BF16)
    return hi, lo


def _dot3(a, b):
    a_hi, a_lo = _split_bf16(a)
    b_hi, b_lo = _split_bf16(b)
    return _dot(a_hi, b_hi) + (_dot(a_hi, b_lo) + _dot(a_lo, b_hi))


def _sigmoid(x):
    return 1.0 / (1.0 + jnp.exp(-x))


def _rms(x):
    return x * lax.rsqrt(jnp.mean(x * x, axis=-1, keepdims=True) + EPS)


def _ada_kernel(c_ref, w_ref, b_ref, o_ref):
    c = c_ref[...]
    s = c * _sigmoid(c)
    o_ref[...] = _dot3(s, w_ref[...]) + b_ref[...]


def _ada(c8, w, b):
    n = w.shape[1]
    tn = 768
    return pl.pallas_call(
        _ada_kernel,
        grid=(n // tn,),
        in_specs=[pl.BlockSpec((8, D_MODEL), lambda j: (0, 0)),
                  pl.BlockSpec((D_MODEL, tn), lambda j: (0, j)),
                  pl.BlockSpec((1, tn), lambda j: (0, j))],
        out_specs=pl.BlockSpec((8, tn), lambda j: (0, j)),
        out_shape=jax.ShapeDtypeStruct((8, n), F32),
        compiler_params=_cparams("parallel"),
        name="ada",
    )(c8, w, b)


def _add_pos(x3, er_ref, ec_ref):
    r = x3.shape[0]
    pr = jnp.broadcast_to(er_ref[...], (r, GRID_W, D_MODEL // 2))
    pc = jnp.broadcast_to(ec_ref[...], (r, GRID_W, D_MODEL // 2))
    return x3 + jnp.concatenate([pr, pc], axis=-1)


def _inproj_kernel(x_ref, er_ref, ec_ref, g_ref, sh_ref, w_ref, xm_ref, z_ref, u_ref, *, add_pos):
    x3 = x_ref[...]
    if add_pos:
        x3 = _add_pos(x3, er_ref, ec_ref)
    x = x3.reshape(x3.shape[0] * GRID_W, D_MODEL)
    h = _rms(x) * g_ref[...] + sh_ref[...]
    proj = _dot(h.astype(BF16), w_ref[...])
    xm_ref[...] = proj[:, :D_MLSTM].astype(BF16)
    z_ref[...] = proj[:, D_MLSTM:2 * D_MLSTM].astype(BF16)
    for g in range(FGROUPS):
        u_ref[g] = proj[:, 2 * D_MLSTM + g * FCG:2 * D_MLSTM + (g + 1) * FCG]


def _inproj(x3, er3, ec3, g_eff, shift, w_in, *, rows, add_pos):
    nr = x3.shape[0]
    t = nr * GRID_W
    tm = rows * GRID_W
    out = jax.ShapeDtypeStruct((t, D_MLSTM), BF16)
    ospec = pl.BlockSpec((tm, D_MLSTM), lambda i: (i, 0))
    vec = pl.BlockSpec((1, D_MODEL), lambda i: (0, 0))
    return pl.pallas_call(
        functools.partial(_inproj_kernel, add_pos=add_pos),
        grid=(nr // rows,),
        in_specs=[pl.BlockSpec((rows, GRID_W, D_MODEL), lambda i: (i, 0, 0)),
                  pl.BlockSpec((rows, 1, D_MODEL // 2), lambda i: (i, 0, 0)),
                  pl.BlockSpec((1, GRID_W, D_MODEL // 2), lambda i: (0, 0, 0)),
                  vec, vec,
                  pl.BlockSpec(w_in.shape, lambda i: (0, 0))],
        out_specs=[ospec, ospec, pl.BlockSpec((FGROUPS, tm, FCG), lambda i: (0, i, 0))],
        out_shape=[out, out, jax.ShapeDtypeStruct((FGROUPS, t, FCG), F32)],
        compiler_params=_cparams("parallel"),
        name="inproj",
    )(x3, er3, ec3, g_eff, shift, w_in)


def _feat_kernel(xm_ref, prev_ref, next_ref, cw_ref, cb_ref, wq_ref, wkt_ref, wv_ref,
                 wiq_ref, wik_ref, wiv_ref, bi_ref, *rest):
    if len(rest) > 5:
        _dft1_kernel(*rest[:4], *rest[9:])
        rest = rest[4:9]
    q_ref, kt_ref, v_ref, act_ref, g_ref = rest
    i = pl.program_id(0)
    n = pl.num_programs(0)
    xm_bf = xm_ref[...]
    xm = xm_bf.astype(F32)
    tm = xm.shape[0]
    prev_row = prev_ref[...].astype(F32)[15:16, :] * jnp.where(i > 0, 1.0, 0.0)
    next_row = next_ref[...].astype(F32)[0:1, :] * jnp.where(i < n - 1, 1.0, 0.0)
    rid = lax.broadcasted_iota(jnp.int32, (tm, 1), 0)
    x_left = jnp.where(rid == 0, prev_row, pltpu.roll(xm, 1, 0))
    x_right = jnp.where(rid == tm - 1, next_row, pltpu.roll(xm, tm - 1, 0))
    cw = cw_ref[...]
    y = cw[0:1] * x_left + cw[1:2] * xm + cw[2:3] * x_right + cb_ref[...]
    act = (y * _sigmoid(y)).astype(BF16)
    act_ref[...] = act
    q = _dot(act, wq_ref[...])
    kt = _dot_nt(wkt_ref[...], act)
    v = _dot(xm_bf, wv_ref[...])
    q_bf = q.astype(BF16)
    kt_bf = kt.astype(BF16)
    v_bf = v.astype(BF16)
    q_s = (q * (DH ** -0.5)).astype(BF16)
    for hd in range(HEADS):
        q_ref[hd] = q_s[:, hd * DH:(hd + 1) * DH]
        v_ref[hd] = v_bf[:, hd * DH:(hd + 1) * DH]
    kt_ref[...] = kt_bf
    g = _dot_nt(wiq_ref[...], q_bf) + _dot(wik_ref[...], kt_bf) + _dot_nt(wiv_ref[...], v_bf)
    g_ref[...] = g + bi_ref[...]


def _feat(xm, conv_w, conv_b, wq, wkt, wv, wiq, wik, wiv, bi, *, tm, fft=()):
    t = xm.shape[0]
    nb16 = t // 16
    k16 = tm // 16
    full = lambda a: pl.BlockSpec(a.shape, lambda i: (0,) * a.ndim)
    tok = pl.BlockSpec((tm, D_MLSTM), lambda i: (i, 0))
    heads = pl.BlockSpec((HEADS, tm, DH), lambda i: (0, i, 0))
    in_specs = [tok,
                pl.BlockSpec((16, D_MLSTM), lambda i: (jnp.maximum(i * k16 - 1, 0), 0)),
                pl.BlockSpec((16, D_MLSTM), lambda i: (jnp.minimum((i + 1) * k16, nb16 - 1), 0)),
                full(conv_w), full(conv_b), full(wq), full(wkt), full(wv),
                full(wiq), full(wik), full(wiv), full(bi)]
    out_specs = [heads,
                 pl.BlockSpec((D_MLSTM, tm), lambda i: (0, i)),
                 heads, tok,
                 pl.BlockSpec((16, tm), lambda i: (0, i))]
    out_shape = [jax.ShapeDtypeStruct((HEADS, t, DH), BF16),
                 jax.ShapeDtypeStruct((D_MLSTM, t), BF16),
                 jax.ShapeDtypeStruct((HEADS, t, DH), BF16),
                 jax.ShapeDtypeStruct((t, D_MLSTM), BF16),
                 jax.ShapeDtypeStruct((16, t), F32)]
    if fft:
        u4, f1, cw3, sw3 = fft
        blk = pl.BlockSpec((FGROUPS, CHUNK, FFT_ROWS, FCG), lambda j: (0, 0, j, 0))
        tw = pl.BlockSpec((FFT_ROWS, CHUNK, 1), lambda j: (j, 0, 0))
        in_specs += [blk, full(f1), tw, tw]
        out_specs += [blk, blk]
        out_shape += [jax.ShapeDtypeStruct(u4.shape, F32)] * 2
    return pl.pallas_call(
        _feat_kernel,
        grid=(t // tm,),
        in_specs=in_specs,
        out_specs=out_specs,
        out_shape=out_shape,
        compiler_params=_cparams("parallel"),
        name="feat",
    )(xm, xm, xm, conv_w, conv_b, wq, wkt, wv, wiq, wik, wiv, bi, *fft)


def _gates_kernel(g_ref, d_ref, gc_ref):
    g = g_ref[...]
    tl = g.shape[1]
    ig = g[0:8]
    fg = g[8:16]
    lf = jnp.minimum(fg, 0.0) - jnp.log(1.0 + jnp.exp(-jnp.abs(fg)))
    pos = lax.broadcasted_iota(jnp.int32, (8, tl), 1) & (CHUNK - 1)
    is_fwd = lax.broadcasted_iota(jnp.int32, (8, tl), 0) < HEADS

    def scan(x, op, ident):
        xf = x
        xb = x
        k = 1
        while k < CHUNK:
            xf = op(xf, jnp.where(pos >= k, pltpu.roll(xf, k, 1), ident))
            xb = op(xb, jnp.where(pos < CHUNK - k, pltpu.roll(xb, tl - k, 1), ident))
            k *= 2
        return jnp.where(is_fwd, xf, xb)

    b = scan(lf, jnp.add, 0.0)
    d = ig - b
    mloc = b + scan(d, jnp.maximum, NEG_BIG)
    d_ref[...] = d
    stack = jnp.concatenate([b, mloc, jnp.zeros((LANES - 16, tl), F32)], axis=0)
    gc_ref[...] = stack.T


def _gates(g, *, tl):
    t = g.shape[1]
    return pl.pallas_call(
        _gates_kernel,
        grid=(t // tl,),
        in_specs=[pl.BlockSpec((16, tl), lambda i: (0, i))],
        out_specs=[pl.BlockSpec((8, tl), lambda i: (0, i)),
                   pl.BlockSpec((tl, LANES), lambda i: (i, 0))],
        out_shape=[jax.ShapeDtypeStruct((8, t), F32),
                   jax.ShapeDtypeStruct((t, LANES), F32)],
        compiler_params=_cparams("parallel"),
        name="gates",
    )(g)


def _mlstm_kernel(*refs, cps, emit, ncast, nfft=0):
    (qf_ref, kf_ref, vf_ref, gcf_ref, drf_ref,
     qb_ref, kb_ref, vb_ref, gcb_ref, drb_ref, c0_ref, m0_ref) = refs[:12]
    cast_in = refs[12:12 + ncast]
    fft_in = refs[12 + ncast:12 + ncast + nfft]
    outs = refs[12 + ncast + nfft:]
    if emit:
        hf_ref, hb_ref, cfin_ref, mfin_ref = outs[:4]
        outs = outs[4:]
    else:
        cfin_ref, mfin_ref = outs[:2]
        outs = outs[2:]
        hf_ref = hb_ref = None
    cast_out = outs[:ncast]
    fft_out = outs[ncast:ncast + (1 if nfft else 0)]
    c_scr, m_scr = outs[ncast + (1 if nfft else 0):]
    step = pl.program_id(0)

    for src, dst in zip(cast_in, cast_out):
        dst[...] = src[...].astype(BF16)
    if nfft:
        _dft2_kernel(*fft_in, *fft_out)

    @pl.when(step == 0)
    def _():
        c_scr[...] = c0_ref[...]
        m_scr[...] = m0_ref[...]

    ti = lax.broadcasted_iota(jnp.int32, (CHUNK, CHUNK), 0)
    si = lax.broadcasted_iota(jnp.int32, (CHUNK, CHUNK), 1)
    nch = 2 * HEADS
    mask = jnp.concatenate([jnp.broadcast_to((si <= ti)[None], (HEADS, CHUNK, CHUNK)),
                            jnp.broadcast_to((si >= ti)[None], (HEADS, CHUNK, CHUNK))], axis=0)
    ones_blk = jnp.ones((nch, CHUNK, CHUNK), BF16)

    def bdot(a, b):
        return lax.dot_general(a, b, (((2,), (1,)), ((0,), (0,))), preferred_element_type=F32)

    def both(f, g):
        return [f(r) for r in range(HEADS)] + [g(r) for r in range(HEADS, nch)]

    for j in range(cps):
        rf = slice(j * CHUNK, (j + 1) * CHUNK)
        rb = slice((cps - 1 - j) * CHUNK, (cps - j) * CHUNK)
        gcf, gcb = gcf_ref[rf, :], gcb_ref[rb, :]
        drf, drb = drf_ref[:, rf], drb_ref[:, rb]
        lf, lb = CHUNK - 1, 0
        b = jnp.stack(both(lambda r: gcf[:, r:r + 1], lambda r: gcb[:, r:r + 1]))
        mloc = jnp.stack(both(lambda r: gcf[:, 8 + r:9 + r], lambda r: gcb[:, 8 + r:9 + r]))
        drow = jnp.stack(both(lambda r: drf[r:r + 1, :], lambda r: drb[r:r + 1, :]))
        btot = jnp.stack(both(lambda r: gcf[lf:lf + 1, r:r + 1], lambda r: gcb[lb:lb + 1, r:r + 1]))
        amax = jnp.stack(both(lambda r: gcf[lf:lf + 1, 8 + r:9 + r], lambda r: gcb[lb:lb + 1, 8 + r:9 + r]))
        m0 = jnp.stack([m_scr[r][0:1, 0:1] for r in range(nch)])
        c_aug = c_scr[...]
        kt = jnp.concatenate([kf_ref[:, :, rf], kb_ref[:, :, rb]], axis=0)
        vaug = jnp.concatenate([jnp.concatenate([vf_ref[:, rf, :], vb_ref[:, rb, :]], axis=0), ones_blk], axis=2)
        if emit:
            qh = jnp.concatenate([qf_ref[:, rf, :], qb_ref[:, rb, :]], axis=0)
            bm = b + m0
            m_t = jnp.maximum(bm, mloc)
            dmat = jnp.where(mask, jnp.exp((b - m_t) + drow), 0.0)
            smat = (bdot(qh, kt) * dmat).astype(BF16)
            q_in = (qh.astype(F32) * jnp.exp(bm - m_t)).astype(BF16)
            num = bdot(jnp.concatenate([smat, q_in], axis=2),
                       jnp.concatenate([vaug, c_aug.astype(BF16)], axis=1))
            den = jnp.maximum(jnp.abs(num[:, :, DH:]), jnp.exp(-m_t))
            h = (num[:, :, :DH] / den).astype(hf_ref.dtype)
            hf_ref[:, rf, :] = h[:HEADS]
            hb_ref[:, rb, :] = h[HEADS:]
        m_new = jnp.maximum(btot + m0, amax)
        decay = jnp.exp(btot + m0 - m_new)
        kw = (kt.astype(F32) * jnp.exp(btot + drow - m_new)).astype(BF16)
        c_scr[...] = decay * c_aug + bdot(kw, vaug)
        m_scr[...] = jnp.broadcast_to(m_new, (nch, 8, LANES))

    @pl.when(step == pl.num_programs(0) - 1)
    def _():
        cfin_ref[...] = c_scr[...]
        mfin_ref[...] = m_scr[...]


def _expert_slot(e):
    return (e % EPG) * N_GROUPS + e // EPG


def _mlstm(q, kt, v, gc, dr, c0, m0, *, cps, emit, casts=(), fft=()):
    t = q.shape[1]
    cb = cps * CHUNK
    nb = t // cb
    fwd_r = lambda i: (i, 0)
    bwd_r = lambda i: (nb - 1 - i, 0)
    fwd_c = lambda i: (0, i)
    bwd_c = lambda i: (0, nb - 1 - i)
    tok = lambda f: pl.BlockSpec((HEADS, cb, DH), lambda i, f=f: (0, f(i)[0], 0))
    in_specs = []
    for fr, fc in ((fwd_r, fwd_c), (bwd_r, bwd_c)):
        in_specs += [tok(fr), pl.BlockSpec((HEADS, DH, cb), lambda i, fc=fc: (0, 0, fc(i)[1])), tok(fr),
                     pl.BlockSpec((cb, LANES), fr), pl.BlockSpec((8, cb), fc)]
    cshape = (2 * HEADS, DH, 2 * DH)
    mshape = (2 * HEADS, 8, LANES)
    cspec = pl.BlockSpec(cshape, lambda i: (0, 0, 0))
    mspec = pl.BlockSpec(mshape, lambda i: (0, 0, 0))
    in_specs += [cspec, mspec]
    out_specs = [cspec, mspec]
    out_shape = [jax.ShapeDtypeStruct(cshape, F32), jax.ShapeDtypeStruct(mshape, F32)]
    if emit:
        out_specs = [tok(fwd_r), tok(bwd_r)] + out_specs
        out_shape = [jax.ShapeDtypeStruct((HEADS, t, DH), BF16)] * 2 + out_shape
    for a in casts:
        per = nb // a.shape[0]
        blk = (1, a.shape[1] // per, a.shape[2])
        in_specs.append(pl.BlockSpec(blk, lambda i, per=per: (i // per, i % per, 0)))
        out_specs.append(pl.BlockSpec(blk, lambda i, per=per: (_expert_slot(i // per), i % per, 0)))
        out_shape.append(jax.ShapeDtypeStruct(a.shape, BF16))
    if fft:
        yc, ys, a2, mix = fft
        tokf = pl.BlockSpec((FGROUPS, FFT_ROWS * CHUNK, FCG), lambda i: (0, i, 0))
        fullf = lambda a: pl.BlockSpec(a.shape, lambda i: (0,) * a.ndim)
        in_specs += [tokf, tokf, fullf(a2), fullf(mix)]
        out_specs.append(pl.BlockSpec((FGROUPS, CHUNK, FFT_ROWS, FCG), lambda i: (0, 0, i, 0)))
        out_shape.append(jax.ShapeDtypeStruct((FGROUPS, CHUNK, t // CHUNK, FCG), F32))
    return pl.pallas_call(
        functools.partial(_mlstm_kernel, cps=cps, emit=emit, ncast=len(casts), nfft=len(fft)),
        grid=(nb,),
        in_specs=in_specs,
        out_specs=out_specs,
        out_shape=out_shape,
        scratch_shapes=[pltpu.VMEM(cshape, F32), pltpu.VMEM(mshape, F32)],
        compiler_params=_cparams("arbitrary"),
        name="mlstm",
    )(q, kt, v, gc, dr, q, kt, v, gc, dr, c0, m0, *casts, *fft)


FFT_ROWS = 8


def _dft1_kernel(u_ref, f_ref, cw_ref, sw_ref, yc_ref, ys_ref):
    f = f_ref[...]
    rows = CHUNK * FFT_ROWS
    u2 = u_ref.reshape(FGROUPS * rows, FCG)
    yc2 = yc_ref.reshape(FGROUPS * rows, FCG)
    ys2 = ys_ref.reshape(FGROUPS * rows, FCG)
    for s in range(FFT_ROWS):
        pick = [pl.ds(g * rows + s, CHUNK, stride=FFT_ROWS) for g in range(FGROUPS)]
        x = jnp.concatenate([u2[p, :] for p in pick], axis=1).astype(BF16)
        y = _dot(f, x)
        cw = cw_ref[s]
        sw = sw_ref[s]
        pr = y[:CHUNK] * cw - y[CHUNK:] * sw
        pi = y[:CHUNK] * sw + y[CHUNK:] * cw
        for g in range(FGROUPS):
            yc2[pick[g], :] = pr[:, g * FCG:(g + 1) * FCG]
            ys2[pick[g], :] = pi[:, g * FCG:(g + 1) * FCG]


def _dft1(u4, f1, cw3, sw3):
    blk = pl.BlockSpec((FGROUPS, CHUNK, FFT_ROWS, FCG), lambda j: (0, 0, j, 0))
    tw = pl.BlockSpec((FFT_ROWS, CHUNK, 1), lambda j: (j, 0, 0))
    out = jax.ShapeDtypeStruct(u4.shape, F32)
    return pl.pallas_call(
        _dft1_kernel,
        grid=(u4.shape[2] // FFT_ROWS,),
        in_specs=[blk, pl.BlockSpec(f1.shape, lambda j: (0, 0)), tw, tw],
        out_specs=[blk, blk],
        out_shape=[out, out],
        compiler_params=_cparams("parallel"),
        name="dft1",
    )(u4, f1, cw3, sw3)


def _dft2_kernel(yc_ref, ys_ref, a2_ref, mix_ref, o_ref):
    a2 = a2_ref[...]
    rows = CHUNK * FFT_ROWS
    o2 = o_ref.reshape(FGROUPS * rows, FCG)
    for kk in range(FFT_ROWS):
        blk = slice(kk * CHUNK, (kk + 1) * CHUNK)
        yc = jnp.concatenate([yc_ref[g, blk, :] for g in range(FGROUPS)], axis=1)
        ys = jnp.concatenate([ys_ref[g, blk, :] for g in range(FGROUPS)], axis=1)
        p = jnp.concatenate([yc, ys], axis=0).astype(BF16)
        x = _dot(a2, p).astype(BF16)
        for g in range(FGROUPS):
            cols = slice(g * FCG, (g + 1) * FCG)
            cat = jnp.concatenate([x[:CHUNK, cols], x[CHUNK:, cols]], axis=1)
            o2[pl.ds(g * rows + kk, CHUNK, stride=FFT_ROWS), :] = _dot(cat, mix_ref[g])


def _dft2(yc, ys, a2, mix):
    t = yc.shape[1]
    n1 = t // CHUNK
    tok = pl.BlockSpec((FGROUPS, FFT_ROWS * CHUNK, FCG), lambda i: (0, i, 0))
    full = lambda a: pl.BlockSpec(a.shape, lambda i: (0,) * a.ndim)
    return pl.pallas_call(
        _dft2_kernel,
        grid=(n1 // FFT_ROWS,),
        in_specs=[tok, tok, full(a2), full(mix)],
        out_specs=pl.BlockSpec((FGROUPS, CHUNK, FFT_ROWS, FCG), lambda i: (0, 0, i, 0)),
        out_shape=jax.ShapeDtypeStruct((FGROUPS, CHUNK, n1, FCG), F32),
        compiler_params=_cparams("parallel"),
        name="dft2",
    )(yc, ys, a2, mix)


def _merge_kernel(hf_ref, hb_ref, act_ref, z_ref, yf_ref, x_ref, er_ref, ec_ref,
                  nw_ref, sk_ref, wout_ref, gg1_ref, g2_ref, sh2_ref, wr_ref, br_ref,
                  x1_ref, h2_ref, lg_ref):
    parts = []
    for hd in range(HEADS):
        hh = hf_ref[hd].astype(F32) + hb_ref[hd].astype(F32)
        dl = hh - jnp.mean(hh, axis=-1, keepdims=True)
        var = jnp.mean(dl * dl, axis=-1, keepdims=True)
        parts.append(dl * lax.rsqrt(var + EPS))
    hn = jnp.concatenate(parts, axis=-1)
    z = z_ref[...].astype(F32)
    m = (hn * nw_ref[...] + sk_ref[...] * act_ref[...].astype(F32)) * (z * _sigmoid(z))
    cat = jnp.concatenate([m.astype(BF16)] + [yf_ref[g].astype(BF16) for g in range(FGROUPS)], axis=-1)
    y = _dot(cat, wout_ref[...])
    x3 = _add_pos(x_ref[...], er_ref, ec_ref)
    xp = x3.reshape(x3.shape[0] * GRID_W, D_MODEL)
    x1 = xp + _rms(y) * gg1_ref[...]
    x1_ref[...] = x1
    h2 = _rms(x1) * g2_ref[...] + sh2_ref[...]
    h2_ref[...] = h2.astype(BF16)
    lg = _dot(h2.astype(BF16), wr_ref[...]) + br_ref[...]
    lg_ref[...] = lg.T[:32]


def _merge(hf, hb, act, z, yf, x3, er3, ec3, nw, sk, wout, gg1, g2, sh2, wr, br, *, rows):
    nr = x3.shape[0]
    t = nr * GRID_W
    tm = rows * GRID_W
    tok = pl.BlockSpec((tm, D_MLSTM), lambda i: (i, 0))
    heads = pl.BlockSpec((HEADS, tm, DH), lambda i: (0, i, 0))
    full = lambda a: pl.BlockSpec(a.shape, lambda i: (0,) * a.ndim)
    return pl.pallas_call(
        _merge_kernel,
        grid=(nr // rows,),
        in_specs=[heads, heads, tok, tok, heads,
                  pl.BlockSpec((rows, GRID_W, D_MODEL), lambda i: (i, 0, 0)),
                  pl.BlockSpec((rows, 1, D_MODEL // 2), lambda i: (i, 0, 0)),
                  pl.BlockSpec((1, GRID_W, D_MODEL // 2), lambda i: (0, 0, 0)),
                  full(nw), full(sk), full(wout), full(gg1), full(g2), full(sh2), full(wr), full(br)],
        out_specs=[pl.BlockSpec((tm, D_MODEL), lambda i: (i, 0)),
                   pl.BlockSpec((tm, D_MODEL), lambda i: (i, 0)),
                   pl.BlockSpec((32, tm), lambda i: (0, i))],
        out_shape=[jax.ShapeDtypeStruct((t, D_MODEL), F32),
                   jax.ShapeDtypeStruct((t, D_MODEL), BF16),
                   jax.ShapeDtypeStruct((32, t), F32)],
        compiler_params=_cparams("parallel"),
        name="merge",
    )(hf, hb, act, z, yf, x3, er3, ec3, nw, sk, wout, gg1, g2, sh2, wr, br)


def _route_kernel(lg_ref, pos_ref, w_ref, cnt_ref, *, sbk):
    lg = lg_ref[...]
    tl = lg.shape[1]
    g = [lg[j:j + 1] for j in range(N_GROUPS)]
    e = [lg[N_GROUPS + j:N_GROUPS + j + 1] for j in range(N_EXPERTS)]
    gmax = jnp.maximum(jnp.maximum(g[0], g[1]), jnp.maximum(g[2], g[3]))
    den = jnp.exp(g[0] - gmax) + jnp.exp(g[1] - gmax) + jnp.exp(g[2] - gmax) + jnp.exp(g[3] - gmax)
    p_sel = 1.0 / den
    sel = []
    free = jnp.ones((1, tl), F32)
    for j in range(N_GROUPS):
        s = jnp.where(g[j] >= gmax, free, 0.0)
        sel.append(s)
        free = free - s
    es = []
    for j in range(EPG):
        es.append(sel[0] * e[j] + sel[1] * e[EPG + j] + sel[2] * e[2 * EPG + j] + sel[3] * e[3 * EPG + j])
    rank = []
    for j in range(EPG):
        rj = jnp.zeros((1, tl), F32)
        for i in range(EPG):
            if i == j:
                continue
            beats = (es[i] >= es[j]) if i < j else (es[i] > es[j])
            rj = rj + jnp.where(beats, 1.0, 0.0)
        rank.append(rj)
    v1 = jnp.maximum(jnp.maximum(es[0], es[1]), jnp.maximum(es[2], es[3]))
    v2 = sum(jnp.where(rank[j] == 1.0, es[j], 0.0) for j in range(EPG))
    tt = jnp.exp(v2 - v1)
    w1 = p_sel / (1.0 + tt)
    w2 = w1 * tt
    w = [jnp.where(rank[j] == 0.0, w1, jnp.where(rank[j] == 1.0, w2, 0.0)) for j in range(EPG)]
    top2 = [jnp.where(rank[j] < 2.0, 1.0, 0.0) for j in range(EPG)]
    mem = jnp.concatenate([sel[gi] * top2[j] for gi in range(N_GROUPS) for j in range(EPG)], axis=0)
    wts = jnp.concatenate([sel[gi] * w[j] for gi in range(N_GROUPS) for j in range(EPG)], axis=0)
    w_ref[...] = wts
    lane = lax.broadcasted_iota(jnp.int32, (N_EXPERTS, tl), 1) & (sbk - 1)
    c = mem
    k = 1
    while k < sbk:
        c = c + jnp.where(lane >= k, pltpu.roll(c, k, 1), 0.0)
        k *= 2
    pos_ref[...] = jnp.where(mem > 0.0, c - 1.0, -1.0)
    lane128 = lax.broadcasted_iota(jnp.int32, (N_EXPERTS, LANES), 1)
    cnt = jnp.zeros((N_EXPERTS, LANES), F32)
    for kb in range(tl // sbk):
        tot = jnp.sum(mem[:, kb * sbk:(kb + 1) * sbk], axis=1, keepdims=True)
        cnt = cnt + jnp.where(lane128 == kb, tot, 0.0)
    cnt_ref[...] = cnt


def _route(lg, *, tl, sbk):
    t = lg.shape[1]
    row = pl.BlockSpec((N_EXPERTS, tl), lambda i: (0, i))
    return pl.pallas_call(
        functools.partial(_route_kernel, sbk=sbk),
        grid=(t // tl,),
        in_specs=[pl.BlockSpec((32, tl), lambda i: (0, i))],
        out_specs=[row, row, pl.BlockSpec((N_EXPERTS, LANES), lambda i: (i, 0))],
        out_shape=[jax.ShapeDtypeStruct((N_EXPERTS, t), F32),
                   jax.ShapeDtypeStruct((N_EXPERTS, t), F32),
                   jax.ShapeDtypeStruct((t // tl * N_EXPERTS, LANES), F32)],
        compiler_params=_cparams("parallel"),
        name="route",
    )(lg)


def _slots_kernel(pos_ref, w_ref, first_ref, qrow_ref, qcol_ref):
    tl = pos_ref.shape[1]
    q0 = jnp.full((1, tl), -1.0, F32)
    q1 = jnp.full((1, tl), -1.0, F32)
    w0 = jnp.zeros((1, tl), F32)
    w1 = jnp.zeros((1, tl), F32)
    seen = jnp.zeros((1, tl), F32)
    for ex in range(N_EXPERTS):
        rk = pos_ref[ex:ex + 1, :]
        wt = w_ref[ex:ex + 1, :]
        m = jnp.where(rk >= 0.0, 1.0, 0.0)
        val = rk + first_ref[ex:ex + 1, :]
        first = (m * (1.0 - seen)) > 0.0
        second = (m * seen) > 0.0
        q0 = jnp.where(first, val, q0)
        w0 = jnp.where(first, wt, w0)
        q1 = jnp.where(second, val, q1)
        w1 = jnp.where(second, wt, w1)
        seen = seen + m
    qrow_ref[...] = jnp.concatenate([q0, q1, jnp.zeros((6, tl), F32)], axis=0)
    qcol_ref[...] = jnp.concatenate([q0, q1, w0, w1, jnp.zeros((LANES - 4, tl), F32)], axis=0).T


def _slots(pos, w, first, *, tl):
    t = pos.shape[1]
    row = pl.BlockSpec((N_EXPERTS, tl), lambda i: (0, i))
    return pl.pallas_call(
        _slots_kernel,
        grid=(t // tl,),
        in_specs=[row, row, row],
        out_specs=[pl.BlockSpec((8, tl), lambda i: (0, i)),
                   pl.BlockSpec((tl, LANES), lambda i: (i, 0))],
        out_shape=[jax.ShapeDtypeStruct((8, t), F32),
                   jax.ShapeDtypeStruct((t, LANES), F32)],
        compiler_params=_cparams("parallel"),
        name="slots",
    )(pos, w, first)


def _mlp(x, wg, wu, wd):
    gt = _dot(x, wg)
    a = ((gt * _sigmoid(gt)) * _dot(x, wu)).astype(BF16)
    return _dot(a, wd).astype(BF16)


def _moe_kernel(tab_ref, loc_ref, h_ref, qrow_ref, wg_ref, wu_ref, wd_ref,
                qcol_ref, x1_ref, gg2_ref, o_ref, xs_ref, *, sbk, slots, csb):
    i = pl.program_id(0)
    step = pl.program_id(1)
    nsb = h_ref.shape[0] // sbk
    eps = wg_ref.shape[0]
    exp_steps = N_EXPERTS // eps

    @pl.when(step == 0)
    def _():
        pid = lax.broadcasted_iota(jnp.int32, (slots, sbk), 0).astype(F32)

        def select(sb, carry):
            q0 = qrow_ref[0, pl.ds(sb, 1), :]
            q1 = qrow_ref[1, pl.ds(sb, 1), :]
            s = jnp.where(q0 == pid, 1.0, jnp.where(q1 == pid, 1.0, 0.0)).astype(BF16)
            row0 = pl.multiple_of(sb * sbk, sbk)
            xs_ref[sb, 0:slots, :] = _dot(s, h_ref[pl.ds(row0, sbk), :]).astype(BF16)
            xs_ref[sb, slots:slots + 32, :] = jnp.zeros((32, D_MODEL), BF16)
            return carry

        lax.fori_loop(0, nsb, select, 0)

    def run_pieces(k, base, n_pieces):
        locs = []
        for m in range(n_pieces):
            d = loc_ref[0, 0, base + m]
            locs.append((d // MOE_LOC, pl.multiple_of(d % MOE_LOC, 16)))
        x = jnp.concatenate([xs_ref[sb, pl.ds(off, 16), :] for sb, off in locs], axis=0)
        y = _mlp(x, wg_ref[k], wu_ref[k], wd_ref[k])
        for m, (sb, off) in enumerate(locs):
            xs_ref[sb, pl.ds(off, 16), :] = y[m * 16:(m + 1) * 16]

    @pl.when(jnp.logical_and(step >= 1, step <= exp_steps))
    def _():
        def expert(k, carry):
            e = _expert_slot((step - 1) * eps + k)
            total = tab_ref[i * N_EXPERTS + e]
            base = e * MOE_PIECES

            def full(t, c1):
                run_pieces(k, base + t * 16, 16)
                return c1

            lax.fori_loop(0, total // 16, full, 0)
            rest = total % 16

            @pl.when(rest > 8)
            def _():
                run_pieces(k, base + (total // 16) * 16, 16)

            @pl.when(jnp.logical_and(rest > 0, rest <= 8))
            def _():
                run_pieces(k, base + (total // 16) * 16, 8)

            return carry

        lax.fori_loop(0, eps, expert, 0)

    @pl.when(step > exp_steps)
    def _():
        lane = lax.broadcasted_iota(jnp.int32, (sbk, slots), 1).astype(F32)
        for k in range(csb):
            sb = (step - 1 - exp_steps) * csb + k
            rows = slice(k * sbk, (k + 1) * sbk)
            qc = qcol_ref[rows, :]
            wmat = (jnp.where(lane == qc[:, 0:1], qc[:, 2:3], 0.0)
                    + jnp.where(lane == qc[:, 1:2], qc[:, 3:4], 0.0)).astype(BF16)
            y = _dot(wmat, xs_ref[sb, 0:slots, :])
            o_ref[rows, :] = x1_ref[rows, :] + _rms(y) * gg2_ref[...]


def _moe(tab, loc, h2, qrow3, wg, wu, wd, qcol, x1, gg2, *, tb, sbk, slots, eps, csb):
    t = h2.shape[0]
    nsb = tb // sbk
    comb_steps = nsb // csb
    exp_steps = N_EXPERTS // eps
    wblk = lambda i, s, c: (jnp.where(s == 0, exp_steps - 1, jnp.minimum(s - 1, exp_steps - 1)), 0, 0)
    oblk = lambda i, s, c: (i * comb_steps + jnp.maximum(s - 1 - exp_steps, 0), 0)
    return pl.pallas_call(
        functools.partial(_moe_kernel, sbk=sbk, slots=slots, csb=csb),
        grid_spec=pltpu.PrefetchScalarGridSpec(
            num_scalar_prefetch=1,
            grid=(t // tb, 1 + exp_steps + comb_steps),
            in_specs=[pl.BlockSpec((1, 1, N_EXPERTS * MOE_PIECES), lambda i, s, c: (i, 0, 0), memory_space=pltpu.SMEM),
                      pl.BlockSpec((tb, D_MODEL), lambda i, s, c: (i, 0)),
                      pl.BlockSpec((8, nsb, sbk), lambda i, s, c: (0, i, 0)),
                      pl.BlockSpec((eps, D_MODEL, D_EXPERT), wblk),
                      pl.BlockSpec((eps, D_MODEL, D_EXPERT), wblk),
                      pl.BlockSpec((eps, D_EXPERT, D_MODEL), wblk),
                      pl.BlockSpec((csb * sbk, LANES), oblk),
                      pl.BlockSpec((csb * sbk, D_MODEL), oblk),
                      pl.BlockSpec((1, D_MODEL), lambda i, s, c: (0, 0))],
            out_specs=pl.BlockSpec((csb * sbk, D_MODEL), oblk),
            scratch_shapes=[pltpu.VMEM((nsb, slots + 32, D_MODEL), BF16)]),
        out_shape=jax.ShapeDtypeStruct((t, D_MODEL), F32),
        compiler_params=pltpu.CompilerParams(dimension_semantics=("parallel", "arbitrary"),
                                             vmem_limit_bytes=MOE_VMEM_LIMIT),
        name="moe",
    )(tab, loc, h2, qrow3, wg, wu, wd, qcol, x1, gg2)


MOE_LOC = 1024
MOE_PIECES = 128


def _slot_tables(cnt, *, nsb, slots):
    tiles = (cnt + 15) // 16
    first = (jnp.cumsum(tiles, axis=1) - tiles) * 16
    tl = tiles.reshape(-1, nsb, N_EXPERTS)
    end = jnp.cumsum(tl, axis=1)
    beg = end - tl
    q = jnp.arange(MOE_PIECES, dtype=jnp.int32)[None, None, None, :]
    inside = jnp.logical_and(q >= beg[..., None], q < end[..., None])
    sb = jnp.arange(nsb, dtype=jnp.int32)[None, :, None, None]
    code = sb * MOE_LOC + first.reshape(-1, nsb, N_EXPERTS)[..., None] + (q - beg[..., None]) * 16
    loc = jnp.sum(jnp.where(inside, code, 0), axis=1)
    total = end[:, -1, :]
    dump = slots + 16 * (q[0, 0] % 2)
    loc = jnp.where(q[0] < total[..., None], loc, dump)
    return first, total.reshape(-1), loc.reshape(loc.shape[0], 1, -1)


def _pos_tables(rows):
    quarter = D_MODEL // 4
    freq = 1.0 / (POS_BASE ** (np.arange(quarter, dtype=np.float64) / quarter))
    r = np.arange(rows, dtype=np.float64)[:, None] * freq
    cl = np.arange(GRID_W, dtype=np.float64)[:, None] * freq
    er = np.concatenate([np.sin(r), np.cos(r)], axis=-1).astype(np.float32)
    ec = np.concatenate([np.sin(cl), np.cos(cl)], axis=-1).astype(np.float32)
    return jnp.asarray(er[:, None, :]), jnp.asarray(ec[None, :, :])


def _dft_tables(t):
    n = np.arange(CHUNK, dtype=np.int64)
    prod = n[:, None] * n[None, :]
    ang = (prod % CHUNK).astype(np.float64) * (2.0 * np.pi / CHUNK)
    c, s = np.cos(ang), np.sin(ang)
    f1 = np.concatenate([c, s], axis=0)
    a2 = np.concatenate([np.concatenate([c, -s], axis=1), np.concatenate([s, c], axis=1)], axis=0)
    cs = np.concatenate([c, -s], axis=0)
    angw = prod.astype(np.float64) * (2.0 * np.pi / t)
    f32 = lambda a: jnp.asarray(a.astype(np.float32))
    return f32(f1), f32(a2), f32(cs), f32(np.cos(angw)[:, :, None]), f32(np.sin(angw)[:, :, None])


def _blockdiag(w):
    n = w.shape[0]
    size = n * QKV_BLOCK
    spread = np.tile(np.eye(QKV_BLOCK, dtype=np.float32), (1, n))
    rows = jnp.dot(w.reshape(size, QKV_BLOCK), jnp.asarray(spread), precision=lax.Precision.HIGHEST)
    blk = np.arange(size) // QKV_BLOCK
    mask = (blk[:, None] == blk[None, :]).astype(np.float32)
    return rows * jnp.asarray(mask)


def _gate_weights(w_f, b_f, w_b, b_b):
    w = jnp.concatenate([w_f[:, :HEADS], w_b[:, :HEADS], w_f[:, HEADS:], w_b[:, HEADS:]], axis=1).T
    b = jnp.concatenate([b_f[:HEADS], b_b[:HEADS], b_f[HEADS:], b_b[HEADS:]])
    return w.astype(BF16), b[:, None]


def kernel(x, c, ctx, c_ctx, w_ada, b_ada, g_pre_mix, g_post_mix, g_pre_ffn, g_post_ffn,
           w_in, conv_w, conv_b, w_q, w_k, w_v, w_if_fwd, b_if_fwd, w_if_bwd, b_if_bwd,
           mlstm_norm_w, mlstm_skip, w_fourier, w_out, w_router_group, b_router_group,
           w_router_expert, b_router_expert, w_gate, w_up, w_down):
    t = x.shape[1]
    rows = t // GRID_W

    c8 = jnp.concatenate([c, c_ctx[None, :], jnp.zeros((6, D_MODEL), F32)], axis=0)
    mod = _ada(c8, w_ada[0], b_ada[0][None, :])
    shift1, scale1, gate1, shift2, scale2, gate2 = [mod[0:1, k * D_MODEL:(k + 1) * D_MODEL] for k in range(N_MOD)]
    shift1c, scale1c = mod[1:2, 0:D_MODEL], mod[1:2, D_MODEL:2 * D_MODEL]
    g1 = g_pre_mix[0][None, :] * (1.0 + scale1)
    g1c = g_pre_mix[0][None, :] * (1.0 + scale1c)
    gg1 = g_post_mix[0][None, :] * gate1
    g2 = g_pre_ffn[0][None, :] * (1.0 + scale2)
    gg2 = g_post_ffn[0][None, :] * gate2

    er3, ec3 = _pos_tables(rows)
    x3 = x.reshape(rows, GRID_W, D_MODEL)
    ctx3 = ctx.reshape(CTX_LEN // GRID_W, GRID_W, D_MODEL)
    w_in_bf = w_in[0].astype(BF16)

    xm_l, z_l, u_l = _inproj(x3, er3, ec3, g1, shift1, w_in_bf, rows=8, add_pos=True)
    xm_c, _, _ = _inproj(ctx3, er3, ec3, g1c, shift1c, w_in_bf, rows=CTX_LEN // GRID_W, add_pos=False)

    wq = _blockdiag(w_q[0]).astype(BF16)
    wkt = _blockdiag(w_k[0]).T.astype(BF16)
    wv = _blockdiag(w_v[0]).astype(BF16)
    wi, bi = _gate_weights(w_if_fwd[0], b_if_fwd[0], w_if_bwd[0], b_if_bwd[0])
    wiq, wik, wiv = wi[:, :D_MLSTM], wi[:, D_MLSTM:2 * D_MLSTM], wi[:, 2 * D_MLSTM:]
    cb = conv_b[0][None, :]
    f1, a2, cs, cw3, sw3 = _dft_tables(t)
    n1 = t // CHUNK
    q_l, kt_l, v_l, act_l, gp_l, yc, ys = _feat(
        xm_l, conv_w[0], cb, wq, wkt, wv, wiq, wik, wiv, bi, tm=t // (n1 // FFT_ROWS),
        fft=(u_l.reshape(FGROUPS, n1, CHUNK, FCG), f1.astype(BF16), cw3, sw3))
    q_c, kt_c, v_c, _, gp_c = _feat(xm_c, conv_w[0], cb, wq, wkt, wv, wiq, wik, wiv, bi, tm=CTX_LEN)

    dr_l, gc_l = _gates(gp_l, tl=2048)
    dr_c, gc_c = _gates(gp_c, tl=CTX_LEN)

    mix = jnp.einsum('kc,gcd->gkd', cs, w_fourier[0], precision=lax.Precision.HIGHEST)
    mix = (mix * float(1.0 / np.sqrt(float(t) * FCG))).astype(BF16)

    c0 = jnp.zeros((2 * HEADS, DH, 2 * DH), F32)
    m0 = jnp.zeros((2 * HEADS, 8, LANES), F32)
    kt_c, kt_l = kt_c.reshape(HEADS, DH, CTX_LEN), kt_l.reshape(HEADS, DH, t)
    c_ctx_fin, m_ctx_fin = _mlstm(q_c, kt_c, v_c, gc_c, dr_c, c0, m0, cps=CTX_LEN // CHUNK, emit=False)
    hf, hb, _, _, wg_bf, wu_bf, wd_bf, yf = _mlstm(
        q_l, kt_l, v_l, gc_l, dr_l, c_ctx_fin, m_ctx_fin, cps=8, emit=True,
        casts=(w_gate[0], w_up[0], w_down[0]),
        fft=(yc.reshape(FGROUPS, t, FCG), ys.reshape(FGROUPS, t, FCG), a2.astype(BF16), mix))
    yf = yf.reshape(FGROUPS, t, FCG)

    wr = jnp.concatenate([w_router_group[0], w_router_expert[0],
                          jnp.zeros((D_MODEL, LANES - N_GROUPS - N_EXPERTS), F32)], axis=1)
    br = jnp.concatenate([b_router_group[0], b_router_expert[0],
                          jnp.zeros((LANES - N_GROUPS - N_EXPERTS,), F32)])[None, :]
    x1, h2, lg = _merge(hf, hb, act_l, z_l, yf, x3, er3, ec3,
                        mlstm_norm_w[0][None, :], mlstm_skip[0][None, :], w_out[0].astype(BF16),
                        gg1, g2, shift2, wr.astype(BF16), br, rows=16)
    pos, wts, cnt = _route(lg, tl=MOE_TB, sbk=MOE_SBK)
    nblk, nsb = t // MOE_TB, MOE_TB // MOE_SBK
    cnt = cnt.reshape(nblk, N_EXPERTS, LANES)[:, :, :nsb]
    cnt = jnp.transpose(cnt, (0, 2, 1)).reshape(nblk * nsb, N_EXPERTS).astype(jnp.int32)
    first, total, loc = _slot_tables(cnt, nsb=nsb, slots=MOE_SLOTS)
    first_rows = jnp.repeat(first.T.astype(F32), MOE_SBK, axis=1)
    qrow, qcol = _slots(pos, wts, first_rows, tl=MOE_TB)
    out = _moe(total, loc, h2, qrow.reshape(8, t // MOE_SBK, MOE_SBK), wg_bf, wu_bf, wd_bf,
               qcol, x1, gg2, tb=MOE_TB, sbk=MOE_SBK, slots=MOE_SLOTS, eps=MOE_EPS, csb=MOE_CSB)
    return out[None]
```

```python
import functools

import numpy as np
import jax
import jax.numpy as jnp
from jax import lax
from jax.experimental import pallas as pl
from jax.experimental.pallas import tpu as pltpu

F32 = jnp.float32
BF16 = jnp.bfloat16

D_MODEL = 1024
SEQ = 16384
GRID_W = 64
CTX_LEN = 256
D_MLSTM = 512
HEADS = 4
DH = 128
QKV_BLOCK = 4
CONV_K = 3
CHUNK = 128
D_FOURIER = 512
FGROUPS = 4
FCG = 128
N_GROUPS = 4
EPG = 4
N_EXPERTS = 16
D_EXPERT = 512
N_MOD = 6
EPS = 1e-6
POS_BASE = 10000.0
LANES = 128
NEG_BIG = -3.0e38

VMEM_LIMIT = 58 * 1024 * 1024
MOE_VMEM_LIMIT = 58 * 1024 * 1024
MOE_TB = 2048
MOE_SBK = 256
MOE_SLOTS = 768
MOE_EPS = 4
MOE_CSB = 2
MOE_LOC = 1024
MOE_PIECES = MOE_TB // 16


def _cparams(*sem):
    return pltpu.CompilerParams(dimension_semantics=sem, vmem_limit_bytes=VMEM_LIMIT)


def _dot(a, b):
    return jnp.dot(a, b, preferred_element_type=F32)


def _dot_nt(a, b):
    return lax.dot_general(a, b, (((1,), (1,)), ((), ())), preferred_element_type=F32)


def _split_bf16(a):
    hi = a.astype(BF16)
    lo = (a - hi.astype(F32)).astype(BF16)
    return hi, lo


def _dot3(a, b):
    a_hi, a_lo = _split_bf16(a)
    b_hi, b_lo = _split_bf16(b)
    return _dot(a_hi, b_hi) + (_dot(a_hi, b_lo) + _dot(a_lo, b_hi))


def _sigmoid(x):
    return 1.0 / (1.0 + jnp.exp(-x))


def _rms(x):
    return x * lax.rsqrt(jnp.mean(x * x, axis=-1, keepdims=True) + EPS)


def _ada_kernel(c_ref, w_ref, b_ref, o_ref):
    c = c_ref[...]
    s = c * _sigmoid(c)
    o_ref[...] = _dot3(s, w_ref[...]) + b_ref[...]


def _ada(c8, w, b):
    n = w.shape[1]
    tn = 768
    return pl.pallas_call(
        _ada_kernel,
        grid=(n // tn,),
        in_specs=[pl.BlockSpec((8, D_MODEL), lambda j: (0, 0)),
                  pl.BlockSpec((D_MODEL, tn), lambda j: (0, j)),
                  pl.BlockSpec((1, tn), lambda j: (0, j))],
        out_specs=pl.BlockSpec((8, tn), lambda j: (0, j)),
        out_shape=jax.ShapeDtypeStruct((8, n), F32),
        compiler_params=_cparams("parallel"),
        name="ada",
    )(c8, w, b)


def _add_pos(x3, er_ref, ec_ref):
    r = x3.shape[0]
    pr = jnp.broadcast_to(er_ref[...], (r, GRID_W, D_MODEL // 2))
    pc = jnp.broadcast_to(ec_ref[...], (r, GRID_W, D_MODEL // 2))
    return x3 + jnp.concatenate([pr, pc], axis=-1)


def _inproj_kernel(x_ref, er_ref, ec_ref, g_ref, sh_ref, w_ref, xm_ref, z_ref, u_ref, *, add_pos):
    x3 = x_ref[...]
    if add_pos:
        x3 = _add_pos(x3, er_ref, ec_ref)
    x = x3.reshape(x3.shape[0] * GRID_W, D_MODEL)
    h = _rms(x) * g_ref[...] + sh_ref[...]
    proj = _dot(h.astype(BF16), w_ref[...])
    xm_ref[...] = proj[:, :D_MLSTM].astype(BF16)
    z_ref[...] = proj[:, D_MLSTM:2 * D_MLSTM].astype(BF16)
    for g in range(FGROUPS):
        u_ref[g] = proj[:, 2 * D_MLSTM + g * FCG:2 * D_MLSTM + (g + 1) * FCG]


def _inproj(x3, er3, ec3, g_eff, shift, w_in, *, rows, add_pos):
    nr = x3.shape[0]
    t = nr * GRID_W
    tm = rows * GRID_W
    out = jax.ShapeDtypeStruct((t, D_MLSTM), BF16)
    ospec = pl.BlockSpec((tm, D_MLSTM), lambda i: (i, 0))
    vec = pl.BlockSpec((1, D_MODEL), lambda i: (0, 0))
    return pl.pallas_call(
        functools.partial(_inproj_kernel, add_pos=add_pos),
        grid=(nr // rows,),
        in_specs=[pl.BlockSpec((rows, GRID_W, D_MODEL), lambda i: (i, 0, 0)),
                  pl.BlockSpec((rows, 1, D_MODEL // 2), lambda i: (i, 0, 0)),
                  pl.BlockSpec((1, GRID_W, D_MODEL // 2), lambda i: (0, 0, 0)),
                  vec, vec,
                  pl.BlockSpec(w_in.shape, lambda i: (0, 0))],
        out_specs=[ospec, ospec, pl.BlockSpec((FGROUPS, tm, FCG), lambda i: (0, i, 0))],
        out_shape=[out, out, jax.ShapeDtypeStruct((FGROUPS, t, FCG), F32)],
        compiler_params=_cparams("parallel"),
        name="inproj",
    )(x3, er3, ec3, g_eff, shift, w_in)


def _feat_kernel(xm_ref, prev_ref, next_ref, cw_ref, cb_ref, wq_ref, wkt_ref, wv_ref,
                 wiq_ref, wik_ref, wiv_ref, bi_ref, *rest):
    if len(rest) > 5:
        _dft1_kernel(*rest[:4], *rest[9:])
        rest = rest[4:9]
    q_ref, kt_ref, v_ref, act_ref, g_ref = rest
    i = pl.program_id(0)
    n = pl.num_programs(0)
    xm_bf = xm_ref[...]
    xm = xm_bf.astype(F32)
    tm = xm.shape[0]
    prev_row = prev_ref[...].astype(F32)[15:16, :] * jnp.where(i > 0, 1.0, 0.0)
    next_row = next_ref[...].astype(F32)[0:1, :] * jnp.where(i < n - 1, 1.0, 0.0)
    rid = lax.broadcasted_iota(jnp.int32, (tm, 1), 0)
    x_left = jnp.where(rid == 0, prev_row, pltpu.roll(xm, 1, 0))
    x_right = jnp.where(rid == tm - 1, next_row, pltpu.roll(xm, tm - 1, 0))
    cw = cw_ref[...]
    y = cw[0:1] * x_left + cw[1:2] * xm + cw[2:3] * x_right + cb_ref[...]
    act = (y * _sigmoid(y)).astype(BF16)
    act_ref[...] = act
    q = _dot(act, wq_ref[...])
    kt = _dot_nt(wkt_ref[...], act)
    v = _dot(xm_bf, wv_ref[...])
    q_bf = q.astype(BF16)
    kt_bf = kt.astype(BF16)
    v_bf = v.astype(BF16)
    q_s = (q * (DH ** -0.5)).astype(BF16)
    for hd in range(HEADS):
        q_ref[hd] = q_s[:, hd * DH:(hd + 1) * DH]
        v_ref[hd] = v_bf[:, hd * DH:(hd + 1) * DH]
    kt_ref[...] = kt_bf
    g = _dot_nt(wiq_ref[...], q_bf) + _dot(wik_ref[...], kt_bf) + _dot_nt(wiv_ref[...], v_bf)
    g_ref[...] = g + bi_ref[...]


def _feat(xm, conv_w, conv_b, wq, wkt, wv, wiq, wik, wiv, bi, *, tm, fft=()):
    t = xm.shape[0]
    nb16 = t // 16
    k16 = tm // 16
    full = lambda a: pl.BlockSpec(a.shape, lambda i: (0,) * a.ndim)
    tok = pl.BlockSpec((tm, D_MLSTM), lambda i: (i, 0))
    heads = pl.BlockSpec((HEADS, tm, DH), lambda i: (0, i, 0))
    in_specs = [tok,
                pl.BlockSpec((16, D_MLSTM), lambda i: (jnp.maximum(i * k16 - 1, 0), 0)),
                pl.BlockSpec((16, D_MLSTM), lambda i: (jnp.minimum((i + 1) * k16, nb16 - 1), 0)),
                full(conv_w), full(conv_b), full(wq), full(wkt), full(wv),
                full(wiq), full(wik), full(wiv), full(bi)]
    out_specs = [heads,
                 pl.BlockSpec((D_MLSTM, tm), lambda i: (0, i)),
                 heads, tok,
                 pl.BlockSpec((16, tm), lambda i: (0, i))]
    out_shape = [jax.ShapeDtypeStruct((HEADS, t, DH), BF16),
                 jax.ShapeDtypeStruct((D_MLSTM, t), BF16),
                 jax.ShapeDtypeStruct((HEADS, t, DH), BF16),
                 jax.ShapeDtypeStruct((t, D_MLSTM), BF16),
                 jax.ShapeDtypeStruct((16, t), F32)]
    if fft:
        u4, f1, cw3, sw3 = fft
        blk = pl.BlockSpec((FGROUPS, CHUNK, FFT_ROWS, FCG), lambda j: (0, 0, j, 0))
        tw = pl.BlockSpec((FFT_ROWS, CHUNK, 1), lambda j: (j, 0, 0))
        in_specs += [blk, full(f1), tw, tw]
        out_specs += [blk, blk]
        out_shape += [jax.ShapeDtypeStruct(u4.shape, F32)] * 2
    return pl.pallas_call(
        _feat_kernel,
        grid=(t // tm,),
        in_specs=in_specs,
        out_specs=out_specs,
        out_shape=out_shape,
        compiler_params=_cparams("parallel"),
        name="feat",
    )(xm, xm, xm, conv_w, conv_b, wq, wkt, wv, wiq, wik, wiv, bi, *fft)


def _gates_kernel(g_ref, d_ref, gc_ref):
    g = g_ref[...]
    tl = g.shape[1]
    ig = g[0:8]
    fg = g[8:16]
    lf = jnp.minimum(fg, 0.0) - jnp.log(1.0 + jnp.exp(-jnp.abs(fg)))
    pos = lax.broadcasted_iota(jnp.int32, (8, tl), 1) & (CHUNK - 1)
    is_fwd = lax.broadcasted_iota(jnp.int32, (8, tl), 0) < HEADS

    def scan(x, op, ident):
        xf = x
        xb = x
        k = 1
        while k < CHUNK:
            xf = op(xf, jnp.where(pos >= k, pltpu.roll(xf, k, 1), ident))
            xb = op(xb, jnp.where(pos < CHUNK - k, pltpu.roll(xb, tl - k, 1), ident))
            k *= 2
        return jnp.where(is_fwd, xf, xb)

    b = scan(lf, jnp.add, 0.0)
    d = ig - b
    mloc = b + scan(d, jnp.maximum, NEG_BIG)
    d_ref[...] = d
    stack = jnp.concatenate([b, mloc, jnp.zeros((LANES - 16, tl), F32)], axis=0)
    gc_ref[...] = stack.T


def _gates(g, *, tl):
    t = g.shape[1]
    return pl.pallas_call(
        _gates_kernel,
        grid=(t // tl,),
        in_specs=[pl.BlockSpec((16, tl), lambda i: (0, i))],
        out_specs=[pl.BlockSpec((8, tl), lambda i: (0, i)),
                   pl.BlockSpec((tl, LANES), lambda i: (i, 0))],
        out_shape=[jax.ShapeDtypeStruct((8, t), F32),
                   jax.ShapeDtypeStruct((t, LANES), F32)],
        compiler_params=_cparams("parallel"),
        name="gates",
    )(g)


def _mlstm_kernel(*refs, cps, emit, ncast, nfft=0):
    (qf_ref, kf_ref, vf_ref, gcf_ref, drf_ref,
     qb_ref, kb_ref, vb_ref, gcb_ref, drb_ref, c0_ref, m0_ref) = refs[:12]
    cast_in = refs[12:12 + ncast]
    fft_in = refs[12 + ncast:12 + ncast + nfft]
    outs = refs[12 + ncast + nfft:]
    if emit:
        hf_ref, hb_ref, cfin_ref, mfin_ref = outs[:4]
        outs = outs[4:]
    else:
        cfin_ref, mfin_ref = outs[:2]
        outs = outs[2:]
        hf_ref = hb_ref = None
    cast_out = outs[:ncast]
    fft_out = outs[ncast:ncast + (1 if nfft else 0)]
    c_scr, m_scr = outs[ncast + (1 if nfft else 0):]
    step = pl.program_id(0)

    for src, dst in zip(cast_in, cast_out):
        dst[...] = src[...].astype(BF16)
    if nfft:
        _dft2_kernel(*fft_in, *fft_out)

    @pl.when(step == 0)
    def _():
        c_scr[...] = c0_ref[...]
        m_scr[...] = m0_ref[...]

    ti = lax.broadcasted_iota(jnp.int32, (CHUNK, CHUNK), 0)
    si = lax.broadcasted_iota(jnp.int32, (CHUNK, CHUNK), 1)
    nch = 2 * HEADS
    mask = jnp.concatenate([jnp.broadcast_to((si <= ti)[None], (HEADS, CHUNK, CHUNK)),
                            jnp.broadcast_to((si >= ti)[None], (HEADS, CHUNK, CHUNK))], axis=0)
    ones_blk = jnp.ones((nch, CHUNK, CHUNK), BF16)

    def bdot(a, b):
        return lax.dot_general(a, b, (((2,), (1,)), ((0,), (0,))), preferred_element_type=F32)

    def both(f, g):
        return [f(r) for r in range(HEADS)] + [g(r) for r in range(HEADS, nch)]

    for j in range(cps):
        rf = slice(j * CHUNK, (j + 1) * CHUNK)
        rb = slice((cps - 1 - j) * CHUNK, (cps - j) * CHUNK)
        gcf, gcb = gcf_ref[rf, :], gcb_ref[rb, :]
        drf, drb = drf_ref[:, rf], drb_ref[:, rb]
        lf, lb = CHUNK - 1, 0
        b = jnp.stack(both(lambda r: gcf[:, r:r + 1], lambda r: gcb[:, r:r + 1]))
        mloc = jnp.stack(both(lambda r: gcf[:, 8 + r:9 + r], lambda r: gcb[:, 8 + r:9 + r]))
        drow = jnp.stack(both(lambda r: drf[r:r + 1, :], lambda r: drb[r:r + 1, :]))
        btot = jnp.stack(both(lambda r: gcf[lf:lf + 1, r:r + 1], lambda r: gcb[lb:lb + 1, r:r + 1]))
        amax = jnp.stack(both(lambda r: gcf[lf:lf + 1, 8 + r:9 + r], lambda r: gcb[lb:lb + 1, 8 + r:9 + r]))
        m0 = jnp.stack([m_scr[r][0:1, 0:1] for r in range(nch)])
        c_aug = c_scr[...]
        kt = jnp.concatenate([kf_ref[:, :, rf], kb_ref[:, :, rb]], axis=0)
        vaug = jnp.concatenate([jnp.concatenate([vf_ref[:, rf, :], vb_ref[:, rb, :]], axis=0), ones_blk], axis=2)
        if emit:
            qh = jnp.concatenate([qf_ref[:, rf, :], qb_ref[:, rb, :]], axis=0)
            bm = b + m0
            m_t = jnp.maximum(bm, mloc)
            dmat = jnp.where(mask, jnp.exp((b - m_t) + drow), 0.0)
            smat = (bdot(qh, kt) * dmat).astype(BF16)
            q_in = (qh.astype(F32) * jnp.exp(bm - m_t)).astype(BF16)
            num = bdot(jnp.concatenate([smat, q_in], axis=2),
                       jnp.concatenate([vaug, c_aug.astype(BF16)], axis=1))
            den = jnp.maximum(jnp.abs(num[:, :, DH:]), jnp.exp(-m_t))
            h = (num[:, :, :DH] / den).astype(hf_ref.dtype)
            hf_ref[:, rf, :] = h[:HEADS]
            hb_ref[:, rb, :] = h[HEADS:]
        m_new = jnp.maximum(btot + m0, amax)
        decay = jnp.exp(btot + m0 - m_new)
        kw = (kt.astype(F32) * jnp.exp(btot + drow - m_new)).astype(BF16)
        c_scr[...] = decay * c_aug + bdot(kw, vaug)
        m_scr[...] = jnp.broadcast_to(m_new, (nch, 8, LANES))

    @pl.when(step == pl.num_programs(0) - 1)
    def _():
        cfin_ref[...] = c_scr[...]
        mfin_ref[...] = m_scr[...]


def _expert_slot(e):
    return (e % EPG) * N_GROUPS + e // EPG


def _mlstm(q, kt, v, gc, dr, c0, m0, *, cps, emit, casts=(), fft=()):
    t = q.shape[1]
    cb = cps * CHUNK
    nb = t // cb
    fwd_r = lambda i: (i, 0)
    bwd_r = lambda i: (nb - 1 - i, 0)
    fwd_c = lambda i: (0, i)
    bwd_c = lambda i: (0, nb - 1 - i)
    tok = lambda f: pl.BlockSpec((HEADS, cb, DH), lambda i, f=f: (0, f(i)[0], 0))
    in_specs = []
    for fr, fc in ((fwd_r, fwd_c), (bwd_r, bwd_c)):
        in_specs += [tok(fr), pl.BlockSpec((HEADS, DH, cb), lambda i, fc=fc: (0, 0, fc(i)[1])), tok(fr),
                     pl.BlockSpec((cb, LANES), fr), pl.BlockSpec((8, cb), fc)]
    cshape = (2 * HEADS, DH, 2 * DH)
    mshape = (2 * HEADS, 8, LANES)
    cspec = pl.BlockSpec(cshape, lambda i: (0, 0, 0))
    mspec = pl.BlockSpec(mshape, lambda i: (0, 0, 0))
    in_specs += [cspec, mspec]
    out_specs = [cspec, mspec]
    out_shape = [jax.ShapeDtypeStruct(cshape, F32), jax.ShapeDtypeStruct(mshape, F32)]
    if emit:
        out_specs = [tok(fwd_r), tok(bwd_r)] + out_specs
        out_shape = [jax.ShapeDtypeStruct((HEADS, t, DH), BF16)] * 2 + out_shape
    for a in casts:
        per = nb // a.shape[0]
        blk = (1, a.shape[1] // per, a.shape[2])
        in_specs.append(pl.BlockSpec(blk, lambda i, per=per: (i // per, i % per, 0)))
        out_specs.append(pl.BlockSpec(blk, lambda i, per=per: (_expert_slot(i // per), i % per, 0)))
        out_shape.append(jax.ShapeDtypeStruct(a.shape, BF16))
    if fft:
        yc, ys, a2, mix = fft
        tokf = pl.BlockSpec((FGROUPS, FFT_ROWS * CHUNK, FCG), lambda i: (0, i, 0))
        fullf = lambda a: pl.BlockSpec(a.shape, lambda i: (0,) * a.ndim)
        in_specs += [tokf, tokf, fullf(a2), fullf(mix)]
        out_specs.append(pl.BlockSpec((FGROUPS, CHUNK, FFT_ROWS, FCG), lambda i: (0, 0, i, 0)))
        out_shape.append(jax.ShapeDtypeStruct((FGROUPS, CHUNK, t // CHUNK, FCG), F32))
    return pl.pallas_call(
        functools.partial(_mlstm_kernel, cps=cps, emit=emit, ncast=len(casts), nfft=len(fft)),
        grid=(nb,),
        in_specs=in_specs,
        out_specs=out_specs,
        out_shape=out_shape,
        scratch_shapes=[pltpu.VMEM(cshape, F32), pltpu.VMEM(mshape, F32)],
        compiler_params=_cparams("arbitrary"),
        name="mlstm",
    )(q, kt, v, gc, dr, q, kt, v, gc, dr, c0, m0, *casts, *fft)


FFT_ROWS = 8


def _dft1_kernel(u_ref, f_ref, cw_ref, sw_ref, yc_ref, ys_ref):
    f = f_ref[...]
    rows = CHUNK * FFT_ROWS
    u2 = u_ref.reshape(FGROUPS * rows, FCG)
    yc2 = yc_ref.reshape(FGROUPS * rows, FCG)
    ys2 = ys_ref.reshape(FGROUPS * rows, FCG)
    for s in range(FFT_ROWS):
        pick = [pl.ds(g * rows + s, CHUNK, stride=FFT_ROWS) for g in range(FGROUPS)]
        x = jnp.concatenate([u2[p, :] for p in pick], axis=1).astype(BF16)
        y = _dot(f, x)
        cw = cw_ref[s]
        sw = sw_ref[s]
        pr = y[:CHUNK] * cw - y[CHUNK:] * sw
        pi = y[:CHUNK] * sw + y[CHUNK:] * cw
        for g in range(FGROUPS):
            yc2[pick[g], :] = pr[:, g * FCG:(g + 1) * FCG]
            ys2[pick[g], :] = pi[:, g * FCG:(g + 1) * FCG]


def _dft2_kernel(yc_ref, ys_ref, a2_ref, mix_ref, o_ref):
    a2 = a2_ref[...]
    rows = CHUNK * FFT_ROWS
    o2 = o_ref.reshape(FGROUPS * rows, FCG)
    for kk in range(FFT_ROWS):
        blk = slice(kk * CHUNK, (kk + 1) * CHUNK)
        yc = jnp.concatenate([yc_ref[g, blk, :] for g in range(FGROUPS)], axis=1)
        ys = jnp.concatenate([ys_ref[g, blk, :] for g in range(FGROUPS)], axis=1)
        p = jnp.concatenate([yc, ys], axis=0).astype(BF16)
        x = _dot(a2, p).astype(BF16)
        for g in range(FGROUPS):
            cols = slice(g * FCG, (g + 1) * FCG)
            cat = jnp.concatenate([x[:CHUNK, cols], x[CHUNK:, cols]], axis=1)
            o2[pl.ds(g * rows + kk, CHUNK, stride=FFT_ROWS), :] = _dot(cat, mix_ref[g])


def _merge_kernel(hf_ref, hb_ref, act_ref, z_ref, yf_ref, x_ref, er_ref, ec_ref,
                  nw_ref, sk_ref, wout_ref, gg1_ref, g2_ref, sh2_ref, wr_ref, br_ref,
                  x1_ref, h2_ref, lg_ref):
    parts = []
    for hd in range(HEADS):
        hh = hf_ref[hd].astype(F32) + hb_ref[hd].astype(F32)
        dl = hh - jnp.mean(hh, axis=-1, keepdims=True)
        var = jnp.mean(dl * dl, axis=-1, keepdims=True)
        parts.append(dl * lax.rsqrt(var + EPS))
    hn = jnp.concatenate(parts, axis=-1)
    z = z_ref[...].astype(F32)
    m = (hn * nw_ref[...] + sk_ref[...] * act_ref[...].astype(F32)) * (z * _sigmoid(z))
    cat = jnp.concatenate([m.astype(BF16)] + [yf_ref[g].astype(BF16) for g in range(FGROUPS)], axis=-1)
    y = _dot(cat, wout_ref[...])
    x3 = _add_pos(x_ref[...], er_ref, ec_ref)
    xp = x3.reshape(x3.shape[0] * GRID_W, D_MODEL)
    x1 = xp + _rms(y) * gg1_ref[...]
    x1_ref[...] = x1
    h2 = _rms(x1) * g2_ref[...] + sh2_ref[...]
    h2_ref[...] = h2.astype(BF16)
    lg = _dot(h2.astype(BF16), wr_ref[...]) + br_ref[...]
    lg_ref[...] = lg.T[:32]


def _merge(hf, hb, act, z, yf, x3, er3, ec3, nw, sk, wout, gg1, g2, sh2, wr, br, *, rows):
    nr = x3.shape[0]
    t = nr * GRID_W
    tm = rows * GRID_W
    tok = pl.BlockSpec((tm, D_MLSTM), lambda i: (i, 0))
    heads = pl.BlockSpec((HEADS, tm, DH), lambda i: (0, i, 0))
    full = lambda a: pl.BlockSpec(a.shape, lambda i: (0,) * a.ndim)
    return pl.pallas_call(
        _merge_kernel,
        grid=(nr // rows,),
        in_specs=[heads, heads, tok, tok, heads,
                  pl.BlockSpec((rows, GRID_W, D_MODEL), lambda i: (i, 0, 0)),
                  pl.BlockSpec((rows, 1, D_MODEL // 2), lambda i: (i, 0, 0)),
                  pl.BlockSpec((1, GRID_W, D_MODEL // 2), lambda i: (0, 0, 0)),
                  full(nw), full(sk), full(wout), full(gg1), full(g2), full(sh2), full(wr), full(br)],
        out_specs=[pl.BlockSpec((tm, D_MODEL), lambda i: (i, 0)),
                   pl.BlockSpec((tm, D_MODEL), lambda i: (i, 0)),
                   pl.BlockSpec((32, tm), lambda i: (0, i))],
        out_shape=[jax.ShapeDtypeStruct((t, D_MODEL), F32),
                   jax.ShapeDtypeStruct((t, D_MODEL), BF16),
                   jax.ShapeDtypeStruct((32, t), F32)],
        compiler_params=_cparams("parallel"),
        name="merge",
    )(hf, hb, act, z, yf, x3, er3, ec3, nw, sk, wout, gg1, g2, sh2, wr, br)


def _route_kernel(lg_ref, pos_ref, w_ref, cnt_ref, *, sbk):
    lg = lg_ref[...]
    tl = lg.shape[1]
    g = [lg[j:j + 1] for j in range(N_GROUPS)]
    e = [lg[N_GROUPS + j:N_GROUPS + j + 1] for j in range(N_EXPERTS)]
    gmax = jnp.maximum(jnp.maximum(g[0], g[1]), jnp.maximum(g[2], g[3]))
    den = jnp.exp(g[0] - gmax) + jnp.exp(g[1] - gmax) + jnp.exp(g[2] - gmax) + jnp.exp(g[3] - gmax)
    p_sel = 1.0 / den
    sel = []
    free = jnp.ones((1, tl), F32)
    for j in range(N_GROUPS):
        s = jnp.where(g[j] >= gmax, free, 0.0)
        sel.append(s)
        free = free - s
    es = []
    for j in range(EPG):
        es.append(sel[0] * e[j] + sel[1] * e[EPG + j] + sel[2] * e[2 * EPG + j] + sel[3] * e[3 * EPG + j])
    rank = []
    for j in range(EPG):
        rj = jnp.zeros((1, tl), F32)
        for i in range(EPG):
            if i == j:
                continue
            beats = (es[i] >= es[j]) if i < j else (es[i] > es[j])
            rj = rj + jnp.where(beats, 1.0, 0.0)
        rank.append(rj)
    v1 = jnp.maximum(jnp.maximum(es[0], es[1]), jnp.maximum(es[2], es[3]))
    v2 = sum(jnp.where(rank[j] == 1.0, es[j], 0.0) for j in range(EPG))
    tt = jnp.exp(v2 - v1)
    w1 = p_sel / (1.0 + tt)
    w2 = w1 * tt
    w = [jnp.where(rank[j] == 0.0, w1, jnp.where(rank[j] == 1.0, w2, 0.0)) for j in range(EPG)]
    top2 = [jnp.where(rank[j] < 2.0, 1.0, 0.0) for j in range(EPG)]
    mem = jnp.concatenate([sel[gi] * top2[j] for gi in range(N_GROUPS) for j in range(EPG)], axis=0)
    wts = jnp.concatenate([sel[gi] * w[j] for gi in range(N_GROUPS) for j in range(EPG)], axis=0)
    w_ref[...] = wts
    lane = lax.broadcasted_iota(jnp.int32, (N_EXPERTS, tl), 1) & (sbk - 1)
    c = mem
    k = 1
    while k < sbk:
        c = c + jnp.where(lane >= k, pltpu.roll(c, k, 1), 0.0)
        k *= 2
    pos_ref[...] = jnp.where(mem > 0.0, c - 1.0, -1.0)
    lane128 = lax.broadcasted_iota(jnp.int32, (N_EXPERTS, LANES), 1)
    cnt = jnp.zeros((N_EXPERTS, LANES), F32)
    for kb in range(tl // sbk):
        tot = jnp.sum(mem[:, kb * sbk:(kb + 1) * sbk], axis=1, keepdims=True)
        cnt = cnt + jnp.where(lane128 == kb, tot, 0.0)
    cnt_ref[...] = cnt


def _route(lg, *, tl, sbk):
    t = lg.shape[1]
    row = pl.BlockSpec((N_EXPERTS, tl), lambda i: (0, i))
    return pl.pallas_call(
        functools.partial(_route_kernel, sbk=sbk),
        grid=(t // tl,),
        in_specs=[pl.BlockSpec((32, tl), lambda i: (0, i))],
        out_specs=[row, row, pl.BlockSpec((N_EXPERTS, LANES), lambda i: (i, 0))],
        out_shape=[jax.ShapeDtypeStruct((N_EXPERTS, t), F32),
                   jax.ShapeDtypeStruct((N_EXPERTS, t), F32),
                   jax.ShapeDtypeStruct((t // tl * N_EXPERTS, LANES), F32)],
        compiler_params=_cparams("parallel"),
        name="route",
    )(lg)


def _slots_kernel(pos_ref, w_ref, first_ref, qrow_ref, qcol_ref):
    tl = pos_ref.shape[1]
    q0 = jnp.full((1, tl), -1.0, F32)
    q1 = jnp.full((1, tl), -1.0, F32)
    w0 = jnp.zeros((1, tl), F32)
    w1 = jnp.zeros((1, tl), F32)
    seen = jnp.zeros((1, tl), F32)
    for ex in range(N_EXPERTS):
        rk = pos_ref[ex:ex + 1, :]
        wt = w_ref[ex:ex + 1, :]
        m = jnp.where(rk >= 0.0, 1.0, 0.0)
        val = rk + first_ref[ex:ex + 1, :]
        first = (m * (1.0 - seen)) > 0.0
        second = (m * seen) > 0.0
        q0 = jnp.where(first, val, q0)
        w0 = jnp.where(first, wt, w0)
        q1 = jnp.where(second, val, q1)
        w1 = jnp.where(second, wt, w1)
        seen = seen + m
    qrow_ref[...] = jnp.concatenate([q0, q1, jnp.zeros((6, tl), F32)], axis=0)
    qcol_ref[...] = jnp.concatenate([q0, q1, w0, w1, jnp.zeros((LANES - 4, tl), F32)], axis=0).T


def _slots(pos, w, first, *, tl):
    t = pos.shape[1]
    row = pl.BlockSpec((N_EXPERTS, tl), lambda i: (0, i))
    return pl.pallas_call(
        _slots_kernel,
        grid=(t // tl,),
        in_specs=[row, row, row],
        out_specs=[pl.BlockSpec((8, tl), lambda i: (0, i)),
                   pl.BlockSpec((tl, LANES), lambda i: (i, 0))],
        out_shape=[jax.ShapeDtypeStruct((8, t), F32),
                   jax.ShapeDtypeStruct((t, LANES), F32)],
        compiler_params=_cparams("parallel"),
        name="slots",
    )(pos, w, first)


def _mlp(x, wg, wu, wd):
    gt = _dot(x, wg)
    a = ((gt * _sigmoid(gt)) * _dot(x, wu)).astype(BF16)
    return _dot(a, wd).astype(BF16)


def _moe_kernel(tab_ref, loc_ref, h_ref, qrow_ref, wg_ref, wu_ref, wd_ref,
                qcol_ref, x1_ref, gg2_ref, o_ref, xs_ref, *, sbk, slots, csb):
    i = pl.program_id(0)
    step = pl.program_id(1)
    nsb = h_ref.shape[0] // sbk
    eps = wg_ref.shape[0]
    exp_steps = N_EXPERTS // eps

    @pl.when(step == 0)
    def _():
        pid = lax.broadcasted_iota(jnp.int32, (slots, sbk), 0).astype(F32)

        def select(sb, carry):
            q0 = qrow_ref[0, pl.ds(sb, 1), :]
            q1 = qrow_ref[1, pl.ds(sb, 1), :]
            s = jnp.where(q0 == pid, 1.0, jnp.where(q1 == pid, 1.0, 0.0)).astype(BF16)
            row0 = pl.multiple_of(sb * sbk, sbk)
            xs_ref[sb, 0:slots, :] = _dot(s, h_ref[pl.ds(row0, sbk), :]).astype(BF16)
            xs_ref[sb, slots:slots + 32, :] = jnp.zeros((32, D_MODEL), BF16)
            return carry

        lax.fori_loop(0, nsb, select, 0)

    def run_pieces(k, base, n_pieces):
        locs = []
        for m in range(n_pieces):
            d = loc_ref[0, 0, base + m]
            locs.append((d // MOE_LOC, pl.multiple_of(d % MOE_LOC, 16)))
        x = jnp.concatenate([xs_ref[sb, pl.ds(off, 16), :] for sb, off in locs], axis=0)
        y = _mlp(x, wg_ref[k], wu_ref[k], wd_ref[k])
        for m, (sb, off) in enumerate(locs):
            xs_ref[sb, pl.ds(off, 16), :] = y[m * 16:(m + 1) * 16]

    @pl.when(jnp.logical_and(step >= 1, step <= exp_steps))
    def _():
        def expert(k, carry):
            e = _expert_slot((step - 1) * eps + k)
            total = tab_ref[i * N_EXPERTS + e]
            base = e * MOE_PIECES

            def full(t, c1):
                run_pieces(k, base + t * 16, 16)
                return c1

            lax.fori_loop(0, total // 16, full, 0)
            rest = total % 16

            @pl.when(rest > 8)
            def _():
                run_pieces(k, base + (total // 16) * 16, 16)

            @pl.when(jnp.logical_and(rest > 0, rest <= 8))
            def _():
                run_pieces(k, base + (total // 16) * 16, 8)

            return carry

        lax.fori_loop(0, eps, expert, 0)

    @pl.when(step > exp_steps)
    def _():
        lane = lax.broadcasted_iota(jnp.int32, (sbk, slots), 1).astype(F32)
        for k in range(csb):
            sb = (step - 1 - exp_steps) * csb + k
            rows = slice(k * sbk, (k + 1) * sbk)
            qc = qcol_ref[rows, :]
            wmat = (jnp.where(lane == qc[:, 0:1], qc[:, 2:3], 0.0)
                    + jnp.where(lane == qc[:, 1:2], qc[:, 3:4], 0.0)).astype(BF16)
            y = _dot(wmat, xs_ref[sb, 0:slots, :])
            o_ref[rows, :] = x1_ref[rows, :] + _rms(y) * gg2_ref[...]


def _moe(tab, loc, h2, qrow3, wg, wu, wd, qcol, x1, gg2, *, tb, sbk, slots, eps, csb):
    t = h2.shape[0]
    nsb = tb // sbk
    comb_steps = nsb // csb
    exp_steps = N_EXPERTS // eps
    wblk = lambda i, s, c: (jnp.where(s == 0, exp_steps - 1, jnp.minimum(s - 1, exp_steps - 1)), 0, 0)
    oblk = lambda i, s, c: (i * comb_steps + jnp.maximum(s - 1 - exp_steps, 0), 0)
    return pl.pallas_call(
        functools.partial(_moe_kernel, sbk=sbk, slots=slots, csb=csb),
        grid_spec=pltpu.PrefetchScalarGridSpec(
            num_scalar_prefetch=1,
            grid=(t // tb, 1 + exp_steps + comb_steps),
            in_specs=[pl.BlockSpec((1, 1, N_EXPERTS * MOE_PIECES), lambda i, s, c: (i, 0, 0), memory_space=pltpu.SMEM),
                      pl.BlockSpec((tb, D_MODEL), lambda i, s, c: (i, 0)),
                      pl.BlockSpec((8, nsb, sbk), lambda i, s, c: (0, i, 0)),
                      pl.BlockSpec((eps, D_MODEL, D_EXPERT), wblk),
                      pl.BlockSpec((eps, D_MODEL, D_EXPERT), wblk),
                      pl.BlockSpec((eps, D_EXPERT, D_MODEL), wblk),
                      pl.BlockSpec((csb * sbk, LANES), oblk),
                      pl.BlockSpec((csb * sbk, D_MODEL), oblk),
                      pl.BlockSpec((1, D_MODEL), lambda i, s, c: (0, 0))],
            out_specs=pl.BlockSpec((csb * sbk, D_MODEL), oblk),
            scratch_shapes=[pltpu.VMEM((nsb, slots + 32, D_MODEL), BF16)]),
        out_shape=jax.ShapeDtypeStruct((t, D_MODEL), F32),
        compiler_params=pltpu.CompilerParams(dimension_semantics=("parallel", "arbitrary"),
                                             vmem_limit_bytes=MOE_VMEM_LIMIT),
        name="moe",
    )(tab, loc, h2, qrow3, wg, wu, wd, qcol, x1, gg2)


def _slot_tables(cnt, *, nsb, slots):
    tiles = (cnt + 15) // 16
    first = (jnp.cumsum(tiles, axis=1) - tiles) * 16
    tl = tiles.reshape(-1, nsb, N_EXPERTS)
    end = jnp.cumsum(tl, axis=1)
    beg = end - tl
    q = jnp.arange(MOE_PIECES, dtype=jnp.int32)[None, None, None, :]
    inside = jnp.logical_and(q >= beg[..., None], q < end[..., None])
    sb = jnp.arange(nsb, dtype=jnp.int32)[None, :, None, None]
    code = sb * MOE_LOC + first.reshape(-1, nsb, N_EXPERTS)[..., None] + (q - beg[..., None]) * 16
    loc = jnp.sum(jnp.where(inside, code, 0), axis=1)
    total = end[:, -1, :]
    dump = slots + 16 * (q[0, 0] % 2)
    loc = jnp.where(q[0] < total[..., None], loc, dump)
    return first, total.reshape(-1), loc.reshape(loc.shape[0], 1, -1)


def _pos_tables(rows):
    quarter = D_MODEL // 4
    freq = 1.0 / (POS_BASE ** (np.arange(quarter, dtype=np.float64) / quarter))
    r = np.arange(rows, dtype=np.float64)[:, None] * freq
    cl = np.arange(GRID_W, dtype=np.float64)[:, None] * freq
    er = np.concatenate([np.sin(r), np.cos(r)], axis=-1).astype(np.float32)
    ec = np.concatenate([np.sin(cl), np.cos(cl)], axis=-1).astype(np.float32)
    return jnp.asarray(er[:, None, :]), jnp.asarray(ec[None, :, :])


def _dft_tables(t):
    n = np.arange(CHUNK, dtype=np.int64)
    prod = n[:, None] * n[None, :]
    ang = (prod % CHUNK).astype(np.float64) * (2.0 * np.pi / CHUNK)
    c, s = np.cos(ang), np.sin(ang)
    f1 = np.concatenate([c, s], axis=0)
    a2 = np.concatenate([np.concatenate([c, -s], axis=1), np.concatenate([s, c], axis=1)], axis=0)
    cs = np.concatenate([c, -s], axis=0)
    angw = prod.astype(np.float64) * (2.0 * np.pi / t)
    f32 = lambda a: jnp.asarray(a.astype(np.float32))
    return f32(f1), f32(a2), f32(cs), f32(np.cos(angw)[:, :, None]), f32(np.sin(angw)[:, :, None])


def _blockdiag(w):
    n = w.shape[0]
    size = n * QKV_BLOCK
    spread = np.tile(np.eye(QKV_BLOCK, dtype=np.float32), (1, n))
    rows = jnp.dot(w.reshape(size, QKV_BLOCK), jnp.asarray(spread), precision=lax.Precision.HIGHEST)
    blk = np.arange(size) // QKV_BLOCK
    mask = (blk[:, None] == blk[None, :]).astype(np.float32)
    return rows * jnp.asarray(mask)


def _gate_weights(w_f, b_f, w_b, b_b):
    w = jnp.concatenate([w_f[:, :HEADS], w_b[:, :HEADS], w_f[:, HEADS:], w_b[:, HEADS:]], axis=1).T
    b = jnp.concatenate([b_f[:HEADS], b_b[:HEADS], b_f[HEADS:], b_b[HEADS:]])
    return w.astype(BF16), b[:, None]


def kernel(x, c, ctx, c_ctx, w_ada, b_ada, g_pre_mix, g_post_mix, g_pre_ffn, g_post_ffn,
           w_in, conv_w, conv_b, w_q, w_k, w_v, w_if_fwd, b_if_fwd, w_if_bwd, b_if_bwd,
           mlstm_norm_w, mlstm_skip, w_fourier, w_out, w_router_group, b_router_group,
           w_router_expert, b_router_expert, w_gate, w_up, w_down):
    t = x.shape[1]
    rows = t // GRID_W

    c8 = jnp.concatenate([c, c_ctx[None, :], jnp.zeros((6, D_MODEL), F32)], axis=0)
    mod = _ada(c8, w_ada[0], b_ada[0][None, :])
    shift1, scale1, gate1, shift2, scale2, gate2 = [mod[0:1, k * D_MODEL:(k + 1) * D_MODEL] for k in range(N_MOD)]
    shift1c, scale1c = mod[1:2, 0:D_MODEL], mod[1:2, D_MODEL:2 * D_MODEL]
    g1 = g_pre_mix[0][None, :] * (1.0 + scale1)
    g1c = g_pre_mix[0][None, :] * (1.0 + scale1c)
    gg1 = g_post_mix[0][None, :] * gate1
    g2 = g_pre_ffn[0][None, :] * (1.0 + scale2)
    gg2 = g_post_ffn[0][None, :] * gate2

    er3, ec3 = _pos_tables(rows)
    x3 = x.reshape(rows, GRID_W, D_MODEL)
    ctx3 = ctx.reshape(CTX_LEN // GRID_W, GRID_W, D_MODEL)
    w_in_bf = w_in[0].astype(BF16)

    xm_l, z_l, u_l = _inproj(x3, er3, ec3, g1, shift1, w_in_bf, rows=8, add_pos=True)
    xm_c, _, _ = _inproj(ctx3, er3, ec3, g1c, shift1c, w_in_bf, rows=CTX_LEN // GRID_W, add_pos=False)

    wq = _blockdiag(w_q[0]).astype(BF16)
    wkt = _blockdiag(w_k[0]).T.astype(BF16)
    wv = _blockdiag(w_v[0]).astype(BF16)
    wi, bi = _gate_weights(w_if_fwd[0], b_if_fwd[0], w_if_bwd[0], b_if_bwd[0])
    wiq, wik, wiv = wi[:, :D_MLSTM], wi[:, D_MLSTM:2 * D_MLSTM], wi[:, 2 * D_MLSTM:]
    cb = conv_b[0][None, :]
    f1, a2, cs, cw3, sw3 = _dft_tables(t)
    n1 = t // CHUNK
    q_l, kt_l, v_l, act_l, gp_l, yc, ys = _feat(
        xm_l, conv_w[0], cb, wq, wkt, wv, wiq, wik, wiv, bi, tm=t // (n1 // FFT_ROWS),
        fft=(u_l.reshape(FGROUPS, n1, CHUNK, FCG), f1.astype(BF16), cw3, sw3))
    q_c, kt_c, v_c, _, gp_c = _feat(xm_c, conv_w[0], cb, wq, wkt, wv, wiq, wik, wiv, bi, tm=CTX_LEN)

    dr_l, gc_l = _gates(gp_l, tl=2048)
    dr_c, gc_c = _gates(gp_c, tl=CTX_LEN)

    mix = jnp.einsum('kc,gcd->gkd', cs, w_fourier[0], precision=lax.Precision.HIGHEST)
    mix = (mix * float(1.0 / np.sqrt(float(t) * FCG))).astype(BF16)

    c0 = jnp.zeros((2 * HEADS, DH, 2 * DH), F32)
    m0 = jnp.zeros((2 * HEADS, 8, LANES), F32)
    kt_c, kt_l = kt_c.reshape(HEADS, DH, CTX_LEN), kt_l.reshape(HEADS, DH, t)
    c_ctx_fin, m_ctx_fin = _mlstm(q_c, kt_c, v_c, gc_c, dr_c, c0, m0, cps=CTX_LEN // CHUNK, emit=False)
    hf, hb, _, _, wg_bf, wu_bf, wd_bf, yf = _mlstm(
        q_l, kt_l, v_l, gc_l, dr_l, c_ctx_fin, m_ctx_fin, cps=8, emit=True,
        casts=(w_gate[0], w_up[0], w_down[0]),
        fft=(yc.reshape(FGROUPS, t, FCG), ys.reshape(FGROUPS, t, FCG), a2.astype(BF16), mix))
    yf = yf.reshape(FGROUPS, t, FCG)

    wr = jnp.concatenate([w_router_group[0], w_router_expert[0],
                          jnp.zeros((D_MODEL, LANES - N_GROUPS - N_EXPERTS), F32)], axis=1)
    br = jnp.concatenate([b_router_group[0], b_router_expert[0],
                          jnp.zeros((LANES - N_GROUPS - N_EXPERTS,), F32)])[None, :]
    x1, h2, lg = _merge(hf, hb, act_l, z_l, yf, x3, er3, ec3,
                        mlstm_norm_w[0][None, :], mlstm_skip[0][None, :], w_out[0].astype(BF16),
                        gg1, g2, shift2, wr.astype(BF16), br, rows=16)
    pos, wts, cnt = _route(lg, tl=MOE_TB, sbk=MOE_SBK)
    nblk, nsb = t // MOE_TB, MOE_TB // MOE_SBK
    cnt = cnt.reshape(nblk, N_EXPERTS, LANES)[:, :, :nsb]
    cnt = jnp.transpose(cnt, (0, 2, 1)).reshape(nblk * nsb, N_EXPERTS).astype(jnp.int32)
    first, total, loc = _slot_tables(cnt, nsb=nsb, slots=MOE_SLOTS)
    first_rows = jnp.repeat(first.T.astype(F32), MOE_SBK, axis=1)
    qrow, qcol = _slots(pos, wts, first_rows, tl=MOE_TB)
    out = _moe(total, loc, h2, qrow.reshape(8, t // MOE_SBK, MOE_SBK), wg_bf, wu_bf, wd_bf,
               qcol, x1, gg2, tb=MOE_TB, sbk=MOE_SBK, slots=MOE_SLOTS, eps=MOE_EPS, csb=MOE_CSB)
    return out[None]
```

```python
import functools

import numpy as np
import jax
import jax.numpy as jnp
from jax import lax
from jax.experimental import pallas as pl
from jax.experimental.pallas import tpu as pltpu

F32 = jnp.float32
BF16 = jnp.bfloat16

D_MODEL = 1024
SEQ = 16384
GRID_W = 64
CTX_LEN = 256
D_MLSTM = 512
HEADS = 4
DH = 128
QKV_BLOCK = 4
CONV_K = 3
CHUNK = 128
D_FOURIER = 512
FGROUPS = 4
FCG = 128
N_GROUPS = 4
EPG = 4
N_EXPERTS = 16
D_EXPERT = 512
N_MOD = 6
EPS = 1e-6
POS_BASE = 10000.0
LANES = 128
NEG_BIG = -3.0e38

VMEM_LIMIT = 58 * 1024 * 1024
MOE_VMEM_LIMIT = 58 * 1024 * 1024
MOE_TB = 2048
MOE_SBK = 256
MOE_SLOTS = 768
MOE_EPS = 4
MOE_CSB = 2
MOE_LOC = 1024
MOE_PIECES = MOE_TB // 16


def _cparams(*sem):
    return pltpu.CompilerParams(dimension_semantics=sem, vmem_limit_bytes=VMEM_LIMIT)


def _dot(a, b):
    return jnp.dot(a, b, preferred_element_type=F32)


def _dot_nt(a, b):
    return lax.dot_general(a, b, (((1,), (1,)), ((), ())), preferred_element_type=F32)


def _split_bf16(a):
    hi = a.astype(BF16)
    lo = (a - hi.astype(F32)).astype(BF16)
    return hi, lo


def _dot3(a, b):
    a_hi, a_lo = _split_bf16(a)
    b_hi, b_lo = _split_bf16(b)
    return _dot(a_hi, b_hi) + (_dot(a_hi, b_lo) + _dot(a_lo, b_hi))


def _sigmoid(x):
    return 1.0 / (1.0 + jnp.exp(-x))


def _rms(x):
    return x * lax.rsqrt(jnp.mean(x * x, axis=-1, keepdims=True) + EPS)


def _ada_kernel(c_ref, w_ref, b_ref, o_ref):
    c = c_ref[...]
    s = c * _sigmoid(c)
    o_ref[...] = _dot3(s, w_ref[...]) + b_ref[...]


def _ada(c8, w, b):
    n = w.shape[1]
    tn = 768
    return pl.pallas_call(
        _ada_kernel,
        grid=(n // tn,),
        in_specs=[pl.BlockSpec((8, D_MODEL), lambda j: (0, 0)),
                  pl.BlockSpec((D_MODEL, tn), lambda j: (0, j)),
                  pl.BlockSpec((1, tn), lambda j: (0, j))],
        out_specs=pl.BlockSpec((8, tn), lambda j: (0, j)),
        out_shape=jax.ShapeDtypeStruct((8, n), F32),
        compiler_params=_cparams("parallel"),
        name="ada",
    )(c8, w, b)


def _add_pos(x3, er_ref, ec_ref):
    r = x3.shape[0]
    pr = jnp.broadcast_to(er_ref[...], (r, GRID_W, D_MODEL // 2))
    pc = jnp.broadcast_to(ec_ref[...], (r, GRID_W, D_MODEL // 2))
    return x3 + jnp.concatenate([pr, pc], axis=-1)


def _cast_specs(a, steps):
    per = steps // a.shape[0]
    blk = (1, a.shape[1] // per, a.shape[2])
    return (pl.BlockSpec(blk, lambda i: (i // per, i % per, 0)),
            pl.BlockSpec(blk, lambda i: (_expert_slot(i // per), i % per, 0)),
            jax.ShapeDtypeStruct(a.shape, BF16))


def _inproj_kernel(x_ref, er_ref, ec_ref, g_ref, sh_ref, w_ref, *rest, add_pos):
    ncast = (len(rest) - 3) // 2
    for src, dst in zip(rest[:ncast], rest[ncast + 3:]):
        dst[...] = src[...].astype(BF16)
    xm_ref, z_ref, u_ref = rest[ncast:ncast + 3]
    x3 = x_ref[...]
    if add_pos:
        x3 = _add_pos(x3, er_ref, ec_ref)
    x = x3.reshape(x3.shape[0] * GRID_W, D_MODEL)
    h = _rms(x) * g_ref[...] + sh_ref[...]
    proj = _dot(h.astype(BF16), w_ref[...])
    xm_ref[...] = proj[:, :D_MLSTM].astype(BF16)
    z_ref[...] = proj[:, D_MLSTM:2 * D_MLSTM].astype(BF16)
    for g in range(FGROUPS):
        u_ref[g] = proj[:, 2 * D_MLSTM + g * FCG:2 * D_MLSTM + (g + 1) * FCG]


def _inproj(x3, er3, ec3, g_eff, shift, w_in, *, rows, add_pos, casts=()):
    nr = x3.shape[0]
    t = nr * GRID_W
    tm = rows * GRID_W
    steps = nr // rows
    out = jax.ShapeDtypeStruct((t, D_MLSTM), BF16)
    ospec = pl.BlockSpec((tm, D_MLSTM), lambda i: (i, 0))
    vec = pl.BlockSpec((1, D_MODEL), lambda i: (0, 0))
    cast_specs = [_cast_specs(a, steps) for a in casts]
    return pl.pallas_call(
        functools.partial(_inproj_kernel, add_pos=add_pos),
        grid=(steps,),
        in_specs=[pl.BlockSpec((rows, GRID_W, D_MODEL), lambda i: (i, 0, 0)),
                  pl.BlockSpec((rows, 1, D_MODEL // 2), lambda i: (i, 0, 0)),
                  pl.BlockSpec((1, GRID_W, D_MODEL // 2), lambda i: (0, 0, 0)),
                  vec, vec,
                  pl.BlockSpec(w_in.shape, lambda i: (0, 0))] + [c[0] for c in cast_specs],
        out_specs=[ospec, ospec, pl.BlockSpec((FGROUPS, tm, FCG), lambda i: (0, i, 0))] + [c[1] for c in cast_specs],
        out_shape=[out, out, jax.ShapeDtypeStruct((FGROUPS, t, FCG), F32)] + [c[2] for c in cast_specs],
        compiler_params=_cparams("parallel"),
        name="inproj",
    )(x3, er3, ec3, g_eff, shift, w_in, *casts)


def _feat_kernel(xm_ref, prev_ref, next_ref, cw_ref, cb_ref, wq_ref, wkt_ref, wv_ref,
                 wiq_ref, wik_ref, wiv_ref, bi_ref, *rest):
    if len(rest) > 5:
        _dft1_kernel(*rest[:4], *rest[9:])
        rest = rest[4:9]
    q_ref, kt_ref, v_ref, act_ref, g_ref = rest
    i = pl.program_id(0)
    n = pl.num_programs(0)
    xm_bf = xm_ref[...]
    xm = xm_bf.astype(F32)
    tm = xm.shape[0]
    prev_row = prev_ref[...].astype(F32)[15:16, :] * jnp.where(i > 0, 1.0, 0.0)
    next_row = next_ref[...].astype(F32)[0:1, :] * jnp.where(i < n - 1, 1.0, 0.0)
    rid = lax.broadcasted_iota(jnp.int32, (tm, 1), 0)
    x_left = jnp.where(rid == 0, prev_row, pltpu.roll(xm, 1, 0))
    x_right = jnp.where(rid == tm - 1, next_row, pltpu.roll(xm, tm - 1, 0))
    cw = cw_ref[...]
    y = cw[0:1] * x_left + cw[1:2] * xm + cw[2:3] * x_right + cb_ref[...]
    act = (y * _sigmoid(y)).astype(BF16)
    act_ref[...] = act
    q = _dot(act, wq_ref[...])
    kt = _dot_nt(wkt_ref[...], act)
    v = _dot(xm_bf, wv_ref[...])
    q_bf = q.astype(BF16)
    kt_bf = kt.astype(BF16)
    v_bf = v.astype(BF16)
    q_s = (q * (DH ** -0.5)).astype(BF16)
    for hd in range(HEADS):
        q_ref[hd] = q_s[:, hd * DH:(hd + 1) * DH]
        v_ref[hd] = v_bf[:, hd * DH:(hd + 1) * DH]
    kt_ref[...] = kt_bf
    g = _dot_nt(wiq_ref[...], q_bf) + _dot(wik_ref[...], kt_bf) + _dot_nt(wiv_ref[...], v_bf)
    g_ref[...] = g + bi_ref[...]


def _feat(xm, conv_w, conv_b, wq, wkt, wv, wiq, wik, wiv, bi, *, tm, fft=()):
    t = xm.shape[0]
    nb16 = t // 16
    k16 = tm // 16
    full = lambda a: pl.BlockSpec(a.shape, lambda i: (0,) * a.ndim)
    tok = pl.BlockSpec((tm, D_MLSTM), lambda i: (i, 0))
    heads = pl.BlockSpec((HEADS, tm, DH), lambda i: (0, i, 0))
    in_specs = [tok,
                pl.BlockSpec((16, D_MLSTM), lambda i: (jnp.maximum(i * k16 - 1, 0), 0)),
                pl.BlockSpec((16, D_MLSTM), lambda i: (jnp.minimum((i + 1) * k16, nb16 - 1), 0)),
                full(conv_w), full(conv_b), full(wq), full(wkt), full(wv),
                full(wiq), full(wik), full(wiv), full(bi)]
    out_specs = [heads,
                 pl.BlockSpec((D_MLSTM, tm), lambda i: (0, i)),
                 heads, tok,
                 pl.BlockSpec((16, tm), lambda i: (0, i))]
    out_shape = [jax.ShapeDtypeStruct((HEADS, t, DH), BF16),
                 jax.ShapeDtypeStruct((D_MLSTM, t), BF16),
                 jax.ShapeDtypeStruct((HEADS, t, DH), BF16),
                 jax.ShapeDtypeStruct((t, D_MLSTM), BF16),
                 jax.ShapeDtypeStruct((16, t), F32)]
    if fft:
        u4, f1, cw3, sw3 = fft
        blk = pl.BlockSpec((FGROUPS, CHUNK, FFT_ROWS, FCG), lambda j: (0, 0, j, 0))
        tw = pl.BlockSpec((FFT_ROWS, CHUNK, 1), lambda j: (j, 0, 0))
        in_specs += [blk, full(f1), tw, tw]
        out_specs += [blk, blk]
        out_shape += [jax.ShapeDtypeStruct(u4.shape, F32)] * 2
    return pl.pallas_call(
        _feat_kernel,
        grid=(t // tm,),
        in_specs=in_specs,
        out_specs=out_specs,
        out_shape=out_shape,
        compiler_params=_cparams("parallel"),
        name="feat",
    )(xm, xm, xm, conv_w, conv_b, wq, wkt, wv, wiq, wik, wiv, bi, *fft)


def _gates_kernel(g_ref, d_ref, gc_ref):
    g = g_ref[...]
    tl = g.shape[1]
    ig = g[0:8]
    fg = g[8:16]
    lf = jnp.minimum(fg, 0.0) - jnp.log(1.0 + jnp.exp(-jnp.abs(fg)))
    pos = lax.broadcasted_iota(jnp.int32, (8, tl), 1) & (CHUNK - 1)
    is_fwd = lax.broadcasted_iota(jnp.int32, (8, tl), 0) < HEADS

    def scan(x, op, ident):
        xf = x
        xb = x
        k = 1
        while k < CHUNK:
            xf = op(xf, jnp.where(pos >= k, pltpu.roll(xf, k, 1), ident))
            xb = op(xb, jnp.where(pos < CHUNK - k, pltpu.roll(xb, tl - k, 1), ident))
            k *= 2
        return jnp.where(is_fwd, xf, xb)

    b = scan(lf, jnp.add, 0.0)
    d = ig - b
    mloc = b + scan(d, jnp.maximum, NEG_BIG)
    d_ref[...] = d
    stack = jnp.concatenate([b, mloc, jnp.zeros((LANES - 16, tl), F32)], axis=0)
    gc_ref[...] = stack.T


def _gates(g, *, tl):
    t = g.shape[1]
    return pl.pallas_call(
        _gates_kernel,
        grid=(t // tl,),
        in_specs=[pl.BlockSpec((16, tl), lambda i: (0, i))],
        out_specs=[pl.BlockSpec((8, tl), lambda i: (0, i)),
                   pl.BlockSpec((tl, LANES), lambda i: (i, 0))],
        out_shape=[jax.ShapeDtypeStruct((8, t), F32),
                   jax.ShapeDtypeStruct((t, LANES), F32)],
        compiler_params=_cparams("parallel"),
        name="gates",
    )(g)


def _mlstm_kernel(*refs, cps, emit, ncast, nfft=0):
    (qf_ref, kf_ref, vf_ref, gcf_ref, drf_ref,
     qb_ref, kb_ref, vb_ref, gcb_ref, drb_ref, c0_ref, m0_ref) = refs[:12]
    cast_in = refs[12:12 + ncast]
    fft_in = refs[12 + ncast:12 + ncast + nfft]
    outs = refs[12 + ncast + nfft:]
    if emit:
        hf_ref, hb_ref, cfin_ref, mfin_ref = outs[:4]
        outs = outs[4:]
    else:
        cfin_ref, mfin_ref = outs[:2]
        outs = outs[2:]
        hf_ref = hb_ref = None
    cast_out = outs[:ncast]
    fft_out = outs[ncast:ncast + (1 if nfft else 0)]
    c_scr, m_scr = outs[ncast + (1 if nfft else 0):]
    step = pl.program_id(0)

    for src, dst in zip(cast_in, cast_out):
        dst[...] = src[...].astype(BF16)
    if nfft:
        _dft2_kernel(*fft_in, *fft_out)

    @pl.when(step == 0)
    def _():
        c_scr[...] = c0_ref[...]
        m_scr[...] = m0_ref[...]

    ti = lax.broadcasted_iota(jnp.int32, (CHUNK, CHUNK), 0)
    si = lax.broadcasted_iota(jnp.int32, (CHUNK, CHUNK), 1)
    nch = 2 * HEADS
    mask = jnp.concatenate([jnp.broadcast_to((si <= ti)[None], (HEADS, CHUNK, CHUNK)),
                            jnp.broadcast_to((si >= ti)[None], (HEADS, CHUNK, CHUNK))], axis=0)
    ones_blk = jnp.ones((nch, CHUNK, CHUNK), BF16)

    def bdot(a, b):
        return lax.dot_general(a, b, (((2,), (1,)), ((0,), (0,))), preferred_element_type=F32)

    def both(f, g):
        return [f(r) for r in range(HEADS)] + [g(r) for r in range(HEADS, nch)]

    for j in range(cps):
        rf = slice(j * CHUNK, (j + 1) * CHUNK)
        rb = slice((cps - 1 - j) * CHUNK, (cps - j) * CHUNK)
        gcf, gcb = gcf_ref[rf, :], gcb_ref[rb, :]
        drf, drb = drf_ref[:, rf], drb_ref[:, rb]
        lf, lb = CHUNK - 1, 0
        b = jnp.stack(both(lambda r: gcf[:, r:r + 1], lambda r: gcb[:, r:r + 1]))
        mloc = jnp.stack(both(lambda r: gcf[:, 8 + r:9 + r], lambda r: gcb[:, 8 + r:9 + r]))
        drow = jnp.stack(both(lambda r: drf[r:r + 1, :], lambda r: drb[r:r + 1, :]))
        btot = jnp.stack(both(lambda r: gcf[lf:lf + 1, r:r + 1], lambda r: gcb[lb:lb + 1, r:r + 1]))
        amax = jnp.stack(both(lambda r: gcf[lf:lf + 1, 8 + r:9 + r], lambda r: gcb[lb:lb + 1, 8 + r:9 + r]))
        m0 = jnp.stack([m_scr[r][0:1, 0:1] for r in range(nch)])
        c_aug = c_scr[...]
        kt = jnp.concatenate([kf_ref[:, :, rf], kb_ref[:, :, rb]], axis=0)
        vaug = jnp.concatenate([jnp.concatenate([vf_ref[:, rf, :], vb_ref[:, rb, :]], axis=0), ones_blk], axis=2)
        if emit:
            qh = jnp.concatenate([qf_ref[:, rf, :], qb_ref[:, rb, :]], axis=0)
            bm = b + m0
            m_t = jnp.maximum(bm, mloc)
            dmat = jnp.where(mask, jnp.exp((b - m_t) + drow), 0.0)
            smat = (bdot(qh, kt) * dmat).astype(BF16)
            q_in = (qh.astype(F32) * jnp.exp(bm - m_t)).astype(BF16)
            num = bdot(jnp.concatenate([smat, q_in], axis=2),
                       jnp.concatenate([vaug, c_aug.astype(BF16)], axis=1))
            den = jnp.maximum(jnp.abs(num[:, :, DH:]), jnp.exp(-m_t))
            h = (num[:, :, :DH] / den).astype(hf_ref.dtype)
            hf_ref[:, rf, :] = h[:HEADS]
            hb_ref[:, rb, :] = h[HEADS:]
        m_new = jnp.maximum(btot + m0, amax)
        decay = jnp.exp(btot + m0 - m_new)
        kw = (kt.astype(F32) * jnp.exp(btot + drow - m_new)).astype(BF16)
        c_scr[...] = decay * c_aug + bdot(kw, vaug)
        m_scr[...] = jnp.broadcast_to(m_new, (nch, 8, LANES))

    @pl.when(step == pl.num_programs(0) - 1)
    def _():
        cfin_ref[...] = c_scr[...]
        mfin_ref[...] = m_scr[...]


def _expert_slot(e):
    return (e % EPG) * N_GROUPS + e // EPG


def _mlstm(q, kt, v, gc, dr, c0, m0, *, cps, emit, casts=(), fft=()):
    t = q.shape[1]
    cb = cps * CHUNK
    nb = t // cb
    fwd_r = lambda i: (i, 0)
    bwd_r = lambda i: (nb - 1 - i, 0)
    fwd_c = lambda i: (0, i)
    bwd_c = lambda i: (0, nb - 1 - i)
    tok = lambda f: pl.BlockSpec((HEADS, cb, DH), lambda i, f=f: (0, f(i)[0], 0))
    in_specs = []
    for fr, fc in ((fwd_r, fwd_c), (bwd_r, bwd_c)):
        in_specs += [tok(fr), pl.BlockSpec((HEADS, DH, cb), lambda i, fc=fc: (0, 0, fc(i)[1])), tok(fr),
                     pl.BlockSpec((cb, LANES), fr), pl.BlockSpec((8, cb), fc)]
    cshape = (2 * HEADS, DH, 2 * DH)
    mshape = (2 * HEADS, 8, LANES)
    cspec = pl.BlockSpec(cshape, lambda i: (0, 0, 0))
    mspec = pl.BlockSpec(mshape, lambda i: (0, 0, 0))
    in_specs += [cspec, mspec]
    out_specs = [cspec, mspec]
    out_shape = [jax.ShapeDtypeStruct(cshape, F32), jax.ShapeDtypeStruct(mshape, F32)]
    if emit:
        out_specs = [tok(fwd_r), tok(bwd_r)] + out_specs
        out_shape = [jax.ShapeDtypeStruct((HEADS, t, DH), BF16)] * 2 + out_shape
    for a in casts:
        src, dst, shape = _cast_specs(a, nb)
        in_specs.append(src)
        out_specs.append(dst)
        out_shape.append(shape)
    if fft:
        yc, ys, a2, mix = fft
        tokf = pl.BlockSpec((FGROUPS, FFT_ROWS * CHUNK, FCG), lambda i: (0, i, 0))
        fullf = lambda a: pl.BlockSpec(a.shape, lambda i: (0,) * a.ndim)
        in_specs += [tokf, tokf, fullf(a2), fullf(mix)]
        out_specs.append(pl.BlockSpec((FGROUPS, CHUNK, FFT_ROWS, FCG), lambda i: (0, 0, i, 0)))
        out_shape.append(jax.ShapeDtypeStruct((FGROUPS, CHUNK, t // CHUNK, FCG), F32))
    return pl.pallas_call(
        functools.partial(_mlstm_kernel, cps=cps, emit=emit, ncast=len(casts), nfft=len(fft)),
        grid=(nb,),
        in_specs=in_specs,
        out_specs=out_specs,
        out_shape=out_shape,
        scratch_shapes=[pltpu.VMEM(cshape, F32), pltpu.VMEM(mshape, F32)],
        compiler_params=_cparams("arbitrary"),
        name="mlstm",
    )(q, kt, v, gc, dr, q, kt, v, gc, dr, c0, m0, *casts, *fft)


FFT_ROWS = 8


def _dft1_kernel(u_ref, f_ref, cw_ref, sw_ref, yc_ref, ys_ref):
    f = f_ref[...]
    rows = CHUNK * FFT_ROWS
    u2 = u_ref.reshape(FGROUPS * rows, FCG)
    yc2 = yc_ref.reshape(FGROUPS * rows, FCG)
    ys2 = ys_ref.reshape(FGROUPS * rows, FCG)
    for s in range(FFT_ROWS):
        pick = [pl.ds(g * rows + s, CHUNK, stride=FFT_ROWS) for g in range(FGROUPS)]
        x = jnp.concatenate([u2[p, :] for p in pick], axis=1).astype(BF16)
        y = _dot(f, x)
        cw = cw_ref[s]
        sw = sw_ref[s]
        pr = y[:CHUNK] * cw - y[CHUNK:] * sw
        pi = y[:CHUNK] * sw + y[CHUNK:] * cw
        for g in range(FGROUPS):
            yc2[pick[g], :] = pr[:, g * FCG:(g + 1) * FCG]
            ys2[pick[g], :] = pi[:, g * FCG:(g + 1) * FCG]


def _dft2_kernel(yc_ref, ys_ref, a2_ref, mix_ref, o_ref):
    a2 = a2_ref[...]
    rows = CHUNK * FFT_ROWS
    o2 = o_ref.reshape(FGROUPS * rows, FCG)
    for kk in range(FFT_ROWS):
        blk = slice(kk * CHUNK, (kk + 1) * CHUNK)
        yc = jnp.concatenate([yc_ref[g, blk, :] for g in range(FGROUPS)], axis=1)
        ys = jnp.concatenate([ys_ref[g, blk, :] for g in range(FGROUPS)], axis=1)
        p = jnp.concatenate([yc, ys], axis=0).astype(BF16)
        x = _dot(a2, p).astype(BF16)
        for g in range(FGROUPS):
            cols = slice(g * FCG, (g + 1) * FCG)
            cat = jnp.concatenate([x[:CHUNK, cols], x[CHUNK:, cols]], axis=1)
            o2[pl.ds(g * rows + kk, CHUNK, stride=FFT_ROWS), :] = _dot(cat, mix_ref[g])


def _merge_kernel(hf_ref, hb_ref, act_ref, z_ref, yf_ref, x_ref, er_ref, ec_ref,
                  nw_ref, sk_ref, wout_ref, gg1_ref, g2_ref, sh2_ref, wr_ref, br_ref,
                  x1_ref, h2_ref, lg_ref):
    parts = []
    for hd in range(HEADS):
        hh = hf_ref[hd].astype(F32) + hb_ref[hd].astype(F32)
        dl = hh - jnp.mean(hh, axis=-1, keepdims=True)
        var = jnp.mean(dl * dl, axis=-1, keepdims=True)
        parts.append(dl * lax.rsqrt(var + EPS))
    hn = jnp.concatenate(parts, axis=-1)
    z = z_ref[...].astype(F32)
    m = (hn * nw_ref[...] + sk_ref[...] * act_ref[...].astype(F32)) * (z * _sigmoid(z))
    cat = jnp.concatenate([m.astype(BF16)] + [yf_ref[g].astype(BF16) for g in range(FGROUPS)], axis=-1)
    y = _dot(cat, wout_ref[...])
    x3 = _add_pos(x_ref[...], er_ref, ec_ref)
    xp = x3.reshape(x3.shape[0] * GRID_W, D_MODEL)
    x1 = xp + _rms(y) * gg1_ref[...]
    x1_ref[...] = x1
    h2 = _rms(x1) * g2_ref[...] + sh2_ref[...]
    h2_ref[...] = h2.astype(BF16)
    lg = _dot(h2.astype(BF16), wr_ref[...]) + br_ref[...]
    lg_ref[...] = lg.T[:32]


def _merge(hf, hb, act, z, yf, x3, er3, ec3, nw, sk, wout, gg1, g2, sh2, wr, br, *, rows):
    nr = x3.shape[0]
    t = nr * GRID_W
    tm = rows * GRID_W
    tok = pl.BlockSpec((tm, D_MLSTM), lambda i: (i, 0))
    heads = pl.BlockSpec((HEADS, tm, DH), lambda i: (0, i, 0))
    full = lambda a: pl.BlockSpec(a.shape, lambda i: (0,) * a.ndim)
    return pl.pallas_call(
        _merge_kernel,
        grid=(nr // rows,),
        in_specs=[heads, heads, tok, tok, heads,
                  pl.BlockSpec((rows, GRID_W, D_MODEL), lambda i: (i, 0, 0)),
                  pl.BlockSpec((rows, 1, D_MODEL // 2), lambda i: (i, 0, 0)),
                  pl.BlockSpec((1, GRID_W, D_MODEL // 2), lambda i: (0, 0, 0)),
                  full(nw), full(sk), full(wout), full(gg1), full(g2), full(sh2), full(wr), full(br)],
        out_specs=[pl.BlockSpec((tm, D_MODEL), lambda i: (i, 0)),
                   pl.BlockSpec((tm, D_MODEL), lambda i: (i, 0)),
                   pl.BlockSpec((32, tm), lambda i: (0, i))],
        out_shape=[jax.ShapeDtypeStruct((t, D_MODEL), F32),
                   jax.ShapeDtypeStruct((t, D_MODEL), BF16),
                   jax.ShapeDtypeStruct((32, t), F32)],
        compiler_params=_cparams("parallel"),
        name="merge",
    )(hf, hb, act, z, yf, x3, er3, ec3, nw, sk, wout, gg1, g2, sh2, wr, br)


def _route_kernel(lg_ref, pos_ref, w_ref, cnt_ref, *, sbk):
    lg = lg_ref[...]
    tl = lg.shape[1]
    g = [lg[j:j + 1] for j in range(N_GROUPS)]
    e = [lg[N_GROUPS + j:N_GROUPS + j + 1] for j in range(N_EXPERTS)]
    gmax = jnp.maximum(jnp.maximum(g[0], g[1]), jnp.maximum(g[2], g[3]))
    den = jnp.exp(g[0] - gmax) + jnp.exp(g[1] - gmax) + jnp.exp(g[2] - gmax) + jnp.exp(g[3] - gmax)
    p_sel = 1.0 / den
    sel = []
    free = jnp.ones((1, tl), F32)
    for j in range(N_GROUPS):
        s = jnp.where(g[j] >= gmax, free, 0.0)
        sel.append(s)
        free = free - s
    es = []
    for j in range(EPG):
        es.append(sel[0] * e[j] + sel[1] * e[EPG + j] + sel[2] * e[2 * EPG + j] + sel[3] * e[3 * EPG + j])
    rank = []
    for j in range(EPG):
        rj = jnp.zeros((1, tl), F32)
        for i in range(EPG):
            if i == j:
                continue
            beats = (es[i] >= es[j]) if i < j else (es[i] > es[j])
            rj = rj + jnp.where(beats, 1.0, 0.0)
        rank.append(rj)
    v1 = jnp.maximum(jnp.maximum(es[0], es[1]), jnp.maximum(es[2], es[3]))
    v2 = sum(jnp.where(rank[j] == 1.0, es[j], 0.0) for j in range(EPG))
    tt = jnp.exp(v2 - v1)
    w1 = p_sel / (1.0 + tt)
    w2 = w1 * tt
    w = [jnp.where(rank[j] == 0.0, w1, jnp.where(rank[j] == 1.0, w2, 0.0)) for j in range(EPG)]
    top2 = [jnp.where(rank[j] < 2.0, 1.0, 0.0) for j in range(EPG)]
    mem = jnp.concatenate([sel[gi] * top2[j] for gi in range(N_GROUPS) for j in range(EPG)], axis=0)
    wts = jnp.concatenate([sel[gi] * w[j] for gi in range(N_GROUPS) for j in range(EPG)], axis=0)
    w_ref[...] = wts
    lane = lax.broadcasted_iota(jnp.int32, (N_EXPERTS, tl), 1) & (sbk - 1)
    c = mem
    k = 1
    while k < sbk:
        c = c + jnp.where(lane >= k, pltpu.roll(c, k, 1), 0.0)
        k *= 2
    pos_ref[...] = jnp.where(mem > 0.0, c - 1.0, -1.0)
    lane128 = lax.broadcasted_iota(jnp.int32, (N_EXPERTS, LANES), 1)
    cnt = jnp.zeros((N_EXPERTS, LANES), F32)
    for kb in range(tl // sbk):
        tot = jnp.sum(mem[:, kb * sbk:(kb + 1) * sbk], axis=1, keepdims=True)
        cnt = cnt + jnp.where(lane128 == kb, tot, 0.0)
    cnt_ref[...] = cnt


def _route(lg, *, tl, sbk):
    t = lg.shape[1]
    row = pl.BlockSpec((N_EXPERTS, tl), lambda i: (0, i))
    return pl.pallas_call(
        functools.partial(_route_kernel, sbk=sbk),
        grid=(t // tl,),
        in_specs=[pl.BlockSpec((32, tl), lambda i: (0, i))],
        out_specs=[row, row, pl.BlockSpec((N_EXPERTS, LANES), lambda i: (i, 0))],
        out_shape=[jax.ShapeDtypeStruct((N_EXPERTS, t), F32),
                   jax.ShapeDtypeStruct((N_EXPERTS, t), F32),
                   jax.ShapeDtypeStruct((t // tl * N_EXPERTS, LANES), F32)],
        compiler_params=_cparams("parallel"),
        name="route",
    )(lg)


def _slots_kernel(pos_ref, w_ref, first_ref, qrow_ref, qcol_ref):
    tl = pos_ref.shape[1]
    q0 = jnp.full((1, tl), -1.0, F32)
    q1 = jnp.full((1, tl), -1.0, F32)
    w0 = jnp.zeros((1, tl), F32)
    w1 = jnp.zeros((1, tl), F32)
    seen = jnp.zeros((1, tl), F32)
    for ex in range(N_EXPERTS):
        rk = pos_ref[ex:ex + 1, :]
        wt = w_ref[ex:ex + 1, :]
        m = jnp.where(rk >= 0.0, 1.0, 0.0)
        val = rk + first_ref[ex:ex + 1, :]
        first = (m * (1.0 - seen)) > 0.0
        second = (m * seen) > 0.0
        q0 = jnp.where(first, val, q0)
        w0 = jnp.where(first, wt, w0)
        q1 = jnp.where(second, val, q1)
        w1 = jnp.where(second, wt, w1)
        seen = seen + m
    qrow_ref[...] = jnp.concatenate([q0, q1, jnp.zeros((6, tl), F32)], axis=0)
    qcol_ref[...] = jnp.concatenate([q0, q1, w0, w1, jnp.zeros((LANES - 4, tl), F32)], axis=0).T


def _slots(pos, w, first, *, tl):
    t = pos.shape[1]
    row = pl.BlockSpec((N_EXPERTS, tl), lambda i: (0, i))
    return pl.pallas_call(
        _slots_kernel,
        grid=(t // tl,),
        in_specs=[row, row, row],
        out_specs=[pl.BlockSpec((8, tl), lambda i: (0, i)),
                   pl.BlockSpec((tl, LANES), lambda i: (i, 0))],
        out_shape=[jax.ShapeDtypeStruct((8, t), F32),
                   jax.ShapeDtypeStruct((t, LANES), F32)],
        compiler_params=_cparams("parallel"),
        name="slots",
    )(pos, w, first)


def _mlp(x, wg, wu, wd):
    gt = _dot(x, wg)
    a = ((gt * _sigmoid(gt)) * _dot(x, wu)).astype(BF16)
    return _dot(a, wd).astype(BF16)


def _moe_kernel(tab_ref, loc_ref, h_ref, qrow_ref, wg_ref, wu_ref, wd_ref,
                qcol_ref, x1_ref, gg2_ref, o_ref, xs_ref, *, sbk, slots, csb):
    i = pl.program_id(0)
    step = pl.program_id(1)
    nsb = h_ref.shape[0] // sbk
    eps = wg_ref.shape[0]
    exp_steps = N_EXPERTS // eps

    @pl.when(step == 0)
    def _():
        pid = lax.broadcasted_iota(jnp.int32, (slots, sbk), 0).astype(F32)

        def select(sb, carry):
            q0 = qrow_ref[0, pl.ds(sb, 1), :]
            q1 = qrow_ref[1, pl.ds(sb, 1), :]
            s = jnp.where(q0 == pid, 1.0, jnp.where(q1 == pid, 1.0, 0.0)).astype(BF16)
            row0 = pl.multiple_of(sb * sbk, sbk)
            xs_ref[sb, 0:slots, :] = _dot(s, h_ref[pl.ds(row0, sbk), :]).astype(BF16)
            xs_ref[sb, slots:slots + 32, :] = jnp.zeros((32, D_MODEL), BF16)
            return carry

        lax.fori_loop(0, nsb, select, 0)

    def run_pieces(k, base, n_pieces):
        locs = []
        for m in range(n_pieces):
            d = loc_ref[0, 0, base + m]
            locs.append((d // MOE_LOC, pl.multiple_of(d % MOE_LOC, 16)))
        x = jnp.concatenate([xs_ref[sb, pl.ds(off, 16), :] for sb, off in locs], axis=0)
        y = _mlp(x, wg_ref[k], wu_ref[k], wd_ref[k])
        for m, (sb, off) in enumerate(locs):
            xs_ref[sb, pl.ds(off, 16), :] = y[m * 16:(m + 1) * 16]

    @pl.when(jnp.logical_and(step >= 1, step <= exp_steps))
    def _():
        def expert(k, carry):
            e = _expert_slot((step - 1) * eps + k)
            total = tab_ref[i * N_EXPERTS + e]
            base = e * MOE_PIECES

            def full(t, c1):
                run_pieces(k, base + t * 16, 16)
                return c1

            lax.fori_loop(0, total // 16, full, 0)
            rest = total % 16

            @pl.when(rest > 8)
            def _():
                run_pieces(k, base + (total // 16) * 16, 16)

            @pl.when(jnp.logical_and(rest > 0, rest <= 8))
            def _():
                run_pieces(k, base + (total // 16) * 16, 8)

            return carry

        lax.fori_loop(0, eps, expert, 0)

    @pl.when(step > exp_steps)
    def _():
        lane = lax.broadcasted_iota(jnp.int32, (sbk, slots), 1).astype(F32)
        for k in range(csb):
            sb = (step - 1 - exp_steps) * csb + k
            rows = slice(k * sbk, (k + 1) * sbk)
            qc = qcol_ref[rows, :]
            wmat = (jnp.where(lane == qc[:, 0:1], qc[:, 2:3], 0.0)
                    + jnp.where(lane == qc[:, 1:2], qc[:, 3:4], 0.0)).astype(BF16)
            y = _dot(wmat, xs_ref[sb, 0:slots, :])
            o_ref[rows, :] = x1_ref[rows, :] + _rms(y) * gg2_ref[...]


def _moe(tab, loc, h2, qrow3, wg, wu, wd, qcol, x1, gg2, *, tb, sbk, slots, eps, csb):
    t = h2.shape[0]
    nsb = tb // sbk
    comb_steps = nsb // csb
    exp_steps = N_EXPERTS // eps
    wblk = lambda i, s, c: (jnp.where(s == 0, exp_steps - 1, jnp.minimum(s - 1, exp_steps - 1)), 0, 0)
    oblk = lambda i, s, c: (i * comb_steps + jnp.maximum(s - 1 - exp_steps, 0), 0)
    return pl.pallas_call(
        functools.partial(_moe_kernel, sbk=sbk, slots=slots, csb=csb),
        grid_spec=pltpu.PrefetchScalarGridSpec(
            num_scalar_prefetch=1,
            grid=(t // tb, 1 + exp_steps + comb_steps),
            in_specs=[pl.BlockSpec((1, 1, N_EXPERTS * MOE_PIECES), lambda i, s, c: (i, 0, 0), memory_space=pltpu.SMEM),
                      pl.BlockSpec((tb, D_MODEL), lambda i, s, c: (i, 0)),
                      pl.BlockSpec((8, nsb, sbk), lambda i, s, c: (0, i, 0)),
                      pl.BlockSpec((eps, D_MODEL, D_EXPERT), wblk),
                      pl.BlockSpec((eps, D_MODEL, D_EXPERT), wblk),
                      pl.BlockSpec((eps, D_EXPERT, D_MODEL), wblk),
                      pl.BlockSpec((csb * sbk, LANES), oblk),
                      pl.BlockSpec((csb * sbk, D_MODEL), oblk),
                      pl.BlockSpec((1, D_MODEL), lambda i, s, c: (0, 0))],
            out_specs=pl.BlockSpec((csb * sbk, D_MODEL), oblk),
            scratch_shapes=[pltpu.VMEM((nsb, slots + 32, D_MODEL), BF16)]),
        out_shape=jax.ShapeDtypeStruct((t, D_MODEL), F32),
        compiler_params=pltpu.CompilerParams(dimension_semantics=("parallel", "arbitrary"),
                                             vmem_limit_bytes=MOE_VMEM_LIMIT),
        name="moe",
    )(tab, loc, h2, qrow3, wg, wu, wd, qcol, x1, gg2)


def _slot_tables(cnt, *, nsb, slots):
    tiles = (cnt + 15) // 16
    first = (jnp.cumsum(tiles, axis=1) - tiles) * 16
    tl = tiles.reshape(-1, nsb, N_EXPERTS)
    end = jnp.cumsum(tl, axis=1)
    beg = end - tl
    q = jnp.arange(MOE_PIECES, dtype=jnp.int32)[None, None, None, :]
    inside = jnp.logical_and(q >= beg[..., None], q < end[..., None])
    sb = jnp.arange(nsb, dtype=jnp.int32)[None, :, None, None]
    code = sb * MOE_LOC + first.reshape(-1, nsb, N_EXPERTS)[..., None] + (q - beg[..., None]) * 16
    loc = jnp.sum(jnp.where(inside, code, 0), axis=1)
    total = end[:, -1, :]
    dump = slots + 16 * (q[0, 0] % 2)
    loc = jnp.where(q[0] < total[..., None], loc, dump)
    return first, total.reshape(-1), loc.reshape(loc.shape[0], 1, -1)


def _pos_tables(rows):
    quarter = D_MODEL // 4
    freq = 1.0 / (POS_BASE ** (np.arange(quarter, dtype=np.float64) / quarter))
    r = np.arange(rows, dtype=np.float64)[:, None] * freq
    cl = np.arange(GRID_W, dtype=np.float64)[:, None] * freq
    er = np.concatenate([np.sin(r), np.cos(r)], axis=-1).astype(np.float32)
    ec = np.concatenate([np.sin(cl), np.cos(cl)], axis=-1).astype(np.float32)
    return jnp.asarray(er[:, None, :]), jnp.asarray(ec[None, :, :])


def _dft_tables(t):
    n = np.arange(CHUNK, dtype=np.int64)
    prod = n[:, None] * n[None, :]
    ang = (prod % CHUNK).astype(np.float64) * (2.0 * np.pi / CHUNK)
    c, s = np.cos(ang), np.sin(ang)
    f1 = np.concatenate([c, s], axis=0)
    a2 = np.concatenate([np.concatenate([c, -s], axis=1), np.concatenate([s, c], axis=1)], axis=0)
    cs = np.concatenate([c, -s], axis=0)
    angw = prod.astype(np.float64) * (2.0 * np.pi / t)
    f32 = lambda a: jnp.asarray(a.astype(np.float32))
    return f32(f1), f32(a2), f32(cs), f32(np.cos(angw)[:, :, None]), f32(np.sin(angw)[:, :, None])


def _blockdiag(w):
    n = w.shape[0]
    size = n * QKV_BLOCK
    spread = np.tile(np.eye(QKV_BLOCK, dtype=np.float32), (1, n))
    rows = jnp.dot(w.reshape(size, QKV_BLOCK), jnp.asarray(spread), precision=lax.Precision.HIGHEST)
    blk = np.arange(size) // QKV_BLOCK
    mask = (blk[:, None] == blk[None, :]).astype(np.float32)
    return rows * jnp.asarray(mask)


def _gate_weights(w_f, b_f, w_b, b_b):
    w = jnp.concatenate([w_f[:, :HEADS], w_b[:, :HEADS], w_f[:, HEADS:], w_b[:, HEADS:]], axis=1).T
    b = jnp.concatenate([b_f[:HEADS], b_b[:HEADS], b_f[HEADS:], b_b[HEADS:]])
    return w.astype(BF16), b[:, None]


def kernel(x, c, ctx, c_ctx, w_ada, b_ada, g_pre_mix, g_post_mix, g_pre_ffn, g_post_ffn,
           w_in, conv_w, conv_b, w_q, w_k, w_v, w_if_fwd, b_if_fwd, w_if_bwd, b_if_bwd,
           mlstm_norm_w, mlstm_skip, w_fourier, w_out, w_router_group, b_router_group,
           w_router_expert, b_router_expert, w_gate, w_up, w_down):
    t = x.shape[1]
    rows = t // GRID_W

    c8 = jnp.concatenate([c, c_ctx[None, :], jnp.zeros((6, D_MODEL), F32)], axis=0)
    mod = _ada(c8, w_ada[0], b_ada[0][None, :])
    shift1, scale1, gate1, shift2, scale2, gate2 = [mod[0:1, k * D_MODEL:(k + 1) * D_MODEL] for k in range(N_MOD)]
    shift1c, scale1c = mod[1:2, 0:D_MODEL], mod[1:2, D_MODEL:2 * D_MODEL]
    g1 = g_pre_mix[0][None, :] * (1.0 + scale1)
    g1c = g_pre_mix[0][None, :] * (1.0 + scale1c)
    gg1 = g_post_mix[0][None, :] * gate1
    g2 = g_pre_ffn[0][None, :] * (1.0 + scale2)
    gg2 = g_post_ffn[0][None, :] * gate2

    er3, ec3 = _pos_tables(rows)
    x3 = x.reshape(rows, GRID_W, D_MODEL)
    ctx3 = ctx.reshape(CTX_LEN // GRID_W, GRID_W, D_MODEL)
    w_in_bf = w_in[0].astype(BF16)

    xm_l, z_l, u_l, wg_bf = _inproj(x3, er3, ec3, g1, shift1, w_in_bf, rows=8, add_pos=True, casts=(w_gate[0],))
    xm_c, _, _ = _inproj(ctx3, er3, ec3, g1c, shift1c, w_in_bf, rows=CTX_LEN // GRID_W, add_pos=False)

    wq = _blockdiag(w_q[0]).astype(BF16)
    wkt = _blockdiag(w_k[0]).T.astype(BF16)
    wv = _blockdiag(w_v[0]).astype(BF16)
    wi, bi = _gate_weights(w_if_fwd[0], b_if_fwd[0], w_if_bwd[0], b_if_bwd[0])
    wiq, wik, wiv = wi[:, :D_MLSTM], wi[:, D_MLSTM:2 * D_MLSTM], wi[:, 2 * D_MLSTM:]
    cb = conv_b[0][None, :]
    f1, a2, cs, cw3, sw3 = _dft_tables(t)
    n1 = t // CHUNK
    q_l, kt_l, v_l, act_l, gp_l, yc, ys = _feat(
        xm_l, conv_w[0], cb, wq, wkt, wv, wiq, wik, wiv, bi, tm=t // (n1 // FFT_ROWS),
        fft=(u_l.reshape(FGROUPS, n1, CHUNK, FCG), f1.astype(BF16), cw3, sw3))
    q_c, kt_c, v_c, _, gp_c = _feat(xm_c, conv_w[0], cb, wq, wkt, wv, wiq, wik, wiv, bi, tm=CTX_LEN)

    dr_l, gc_l = _gates(gp_l, tl=2048)
    dr_c, gc_c = _gates(gp_c, tl=CTX_LEN)

    mix = jnp.einsum('kc,gcd->gkd', cs, w_fourier[0], precision=lax.Precision.HIGHEST)
    mix = (mix * float(1.0 / np.sqrt(float(t) * FCG))).astype(BF16)

    c0 = jnp.zeros((2 * HEADS, DH, 2 * DH), F32)
    m0 = jnp.zeros((2 * HEADS, 8, LANES), F32)
    kt_c, kt_l = kt_c.reshape(HEADS, DH, CTX_LEN), kt_l.reshape(HEADS, DH, t)
    c_ctx_fin, m_ctx_fin = _mlstm(q_c, kt_c, v_c, gc_c, dr_c, c0, m0, cps=CTX_LEN // CHUNK, emit=False)
    hf, hb, _, _, wu_bf, wd_bf, yf = _mlstm(
        q_l, kt_l, v_l, gc_l, dr_l, c_ctx_fin, m_ctx_fin, cps=8, emit=True,
        casts=(w_up[0], w_down[0]),
        fft=(yc.reshape(FGROUPS, t, FCG), ys.reshape(FGROUPS, t, FCG), a2.astype(BF16), mix))
    yf = yf.reshape(FGROUPS, t, FCG)

    wr = jnp.concatenate([w_router_group[0], w_router_expert[0],
                          jnp.zeros((D_MODEL, LANES - N_GROUPS - N_EXPERTS), F32)], axis=1)
    br = jnp.concatenate([b_router_group[0], b_router_expert[0],
                          jnp.zeros((LANES - N_GROUPS - N_EXPERTS,), F32)])[None, :]
    x1, h2, lg = _merge(hf, hb, act_l, z_l, yf, x3, er3, ec3,
                        mlstm_norm_w[0][None, :], mlstm_skip[0][None, :], w_out[0].astype(BF16),
                        gg1, g2, shift2, wr.astype(BF16), br, rows=16)
    pos, wts, cnt = _route(lg, tl=MOE_TB, sbk=MOE_SBK)
    nblk, nsb = t // MOE_TB, MOE_TB // MOE_SBK
    cnt = cnt.reshape(nblk, N_EXPERTS, LANES)[:, :, :nsb]
    cnt = jnp.transpose(cnt, (0, 2, 1)).reshape(nblk * nsb, N_EXPERTS).astype(jnp.int32)
    first, total, loc = _slot_tables(cnt, nsb=nsb, slots=MOE_SLOTS)
    first_rows = jnp.repeat(first.T.astype(F32), MOE_SBK, axis=1)
    qrow, qcol = _slots(pos, wts, first_rows, tl=MOE_TB)
    out = _moe(total, loc, h2, qrow.reshape(8, t // MOE_SBK, MOE_SBK), wg_bf, wu_bf, wd_bf,
               qcol, x1, gg2, tb=MOE_TB, sbk=MOE_SBK, slots=MOE_SLOTS, eps=MOE_EPS, csb=MOE_CSB)
    return out[None]
```

```python
import functools

import numpy as np
import jax
import jax.numpy as jnp
from jax import lax
from jax.experimental import pallas as pl
from jax.experimental.pallas import tpu as pltpu

F32 = jnp.float32
BF16 = jnp.bfloat16

D_MODEL = 1024
SEQ = 16384
GRID_W = 64
CTX_LEN = 256
D_MLSTM = 512
HEADS = 4
DH = 128
QKV_BLOCK = 4
CONV_K = 3
CHUNK = 128
D_FOURIER = 512
FGROUPS = 4
FCG = 128
N_GROUPS = 4
EPG = 4
N_EXPERTS = 16
D_EXPERT = 512
N_MOD = 6
EPS = 1e-6
POS_BASE = 10000.0
LANES = 128
NEG_BIG = -3.0e38

VMEM_LIMIT = 58 * 1024 * 1024
MOE_VMEM_LIMIT = 58 * 1024 * 1024
MOE_TB = 2048
MOE_SBK = 256
MOE_SLOTS = 768
MOE_EPS = 4
MOE_CSB = 2
MOE_LOC = 1024
MOE_PIECES = MOE_TB // 16


def _cparams(*sem):
    return pltpu.CompilerParams(dimension_semantics=sem, vmem_limit_bytes=VMEM_LIMIT)


def _dot(a, b):
    return jnp.dot(a, b, preferred_element_type=F32)


def _dot_nt(a, b):
    return lax.dot_general(a, b, (((1,), (1,)), ((), ())), preferred_element_type=F32)


def _split_bf16(a):
    hi = a.astype(BF16)
    lo = (a - hi.astype(F32)).astype(BF16)
    return hi, lo


def _dot3(a, b):
    a_hi, a_lo = _split_bf16(a)
    b_hi, b_lo = _split_bf16(b)
    return _dot(a_hi, b_hi) + (_dot(a_hi, b_lo) + _dot(a_lo, b_hi))


def _sigmoid(x):
    return 1.0 / (1.0 + jnp.exp(-x))


def _rms(x):
    return x * lax.rsqrt(jnp.mean(x * x, axis=-1, keepdims=True) + EPS)


def _ada_kernel(c_ref, w_ref, b_ref, o_ref):
    c = c_ref[...]
    s = c * _sigmoid(c)
    o_ref[...] = _dot3(s, w_ref[...]) + b_ref[...]


def _ada(c8, w, b):
    n = w.shape[1]
    tn = 768
    return pl.pallas_call(
        _ada_kernel,
        grid=(n // tn,),
        in_specs=[pl.BlockSpec((8, D_MODEL), lambda j: (0, 0)),
                  pl.BlockSpec((D_MODEL, tn), lambda j: (0, j)),
                  pl.BlockSpec((1, tn), lambda j: (0, j))],
        out_specs=pl.BlockSpec((8, tn), lambda j: (0, j)),
        out_shape=jax.ShapeDtypeStruct((8, n), F32),
        compiler_params=_cparams("parallel"),
        name="ada",
    )(c8, w, b)


def _add_pos(x3, er_ref, ec_ref):
    r = x3.shape[0]
    pr = jnp.broadcast_to(er_ref[...], (r, GRID_W, D_MODEL // 2))
    pc = jnp.broadcast_to(ec_ref[...], (r, GRID_W, D_MODEL // 2))
    return x3 + jnp.concatenate([pr, pc], axis=-1)


def _inproj_kernel(x_ref, er_ref, ec_ref, g_ref, sh_ref, w_ref, xm_ref, z_ref, u_ref, *, add_pos):
    x3 = x_ref[...]
    if add_pos:
        x3 = _add_pos(x3, er_ref, ec_ref)
    x = x3.reshape(x3.shape[0] * GRID_W, D_MODEL)
    h = _rms(x) * g_ref[...] + sh_ref[...]
    proj = _dot(h.astype(BF16), w_ref[...])
    xm_ref[...] = proj[:, :D_MLSTM].astype(BF16)
    z_ref[...] = proj[:, D_MLSTM:2 * D_MLSTM].astype(BF16)
    for g in range(FGROUPS):
        u_ref[g] = proj[:, 2 * D_MLSTM + g * FCG:2 * D_MLSTM + (g + 1) * FCG]


def _inproj(x3, er3, ec3, g_eff, shift, w_in, *, rows, add_pos):
    nr = x3.shape[0]
    t = nr * GRID_W
    tm = rows * GRID_W
    out = jax.ShapeDtypeStruct((t, D_MLSTM), BF16)
    ospec = pl.BlockSpec((tm, D_MLSTM), lambda i: (i, 0))
    vec = pl.BlockSpec((1, D_MODEL), lambda i: (0, 0))
    return pl.pallas_call(
        functools.partial(_inproj_kernel, add_pos=add_pos),
        grid=(nr // rows,),
        in_specs=[pl.BlockSpec((rows, GRID_W, D_MODEL), lambda i: (i, 0, 0)),
                  pl.BlockSpec((rows, 1, D_MODEL // 2), lambda i: (i, 0, 0)),
                  pl.BlockSpec((1, GRID_W, D_MODEL // 2), lambda i: (0, 0, 0)),
                  vec, vec,
                  pl.BlockSpec(w_in.shape, lambda i: (0, 0))],
        out_specs=[ospec, ospec, pl.BlockSpec((FGROUPS, tm, FCG), lambda i: (0, i, 0))],
        out_shape=[out, out, jax.ShapeDtypeStruct((FGROUPS, t, FCG), F32)],
        compiler_params=_cparams("parallel"),
        name="inproj",
    )(x3, er3, ec3, g_eff, shift, w_in)


def _feat_kernel(xm_ref, prev_ref, next_ref, cw_ref, cb_ref, wq_ref, wkt_ref, wv_ref,
                 wiq_ref, wik_ref, wiv_ref, bi_ref, *rest):
    if len(rest) > 5:
        _dft1_kernel(*rest[:4], *rest[9:])
        rest = rest[4:9]
    q_ref, kt_ref, v_ref, act_ref, g_ref = rest
    i = pl.program_id(0)
    n = pl.num_programs(0)
    xm_bf = xm_ref[...]
    xm = xm_bf.astype(F32)
    tm = xm.shape[0]
    prev_row = prev_ref[...].astype(F32)[15:16, :] * jnp.where(i > 0, 1.0, 0.0)
    next_row = next_ref[...].astype(F32)[0:1, :] * jnp.where(i < n - 1, 1.0, 0.0)
    rid = lax.broadcasted_iota(jnp.int32, (tm, 1), 0)
    x_left = jnp.where(rid == 0, prev_row, pltpu.roll(xm, 1, 0))
    x_right = jnp.where(rid == tm - 1, next_row, pltpu.roll(xm, tm - 1, 0))
    cw = cw_ref[...]
    y = cw[0:1] * x_left + cw[1:2] * xm + cw[2:3] * x_right + cb_ref[...]
    act = (y * _sigmoid(y)).astype(BF16)
    act_ref[...] = act
    q = _dot(act, wq_ref[...])
    kt = _dot_nt(wkt_ref[...], act)
    v = _dot(xm_bf, wv_ref[...])
    q_bf = q.astype(BF16)
    kt_bf = kt.astype(BF16)
    v_bf = v.astype(BF16)
    q_s = (q * (DH ** -0.5)).astype(BF16)
    for hd in range(HEADS):
        q_ref[hd] = q_s[:, hd * DH:(hd + 1) * DH]
        v_ref[hd] = v_bf[:, hd * DH:(hd + 1) * DH]
    kt_ref[...] = kt_bf
    g = _dot_nt(wiq_ref[...], q_bf) + _dot(wik_ref[...], kt_bf) + _dot_nt(wiv_ref[...], v_bf)
    g_ref[...] = g + bi_ref[...]


def _feat(xm, conv_w, conv_b, wq, wkt, wv, wiq, wik, wiv, bi, *, tm, fft=()):
    t = xm.shape[0]
    nb16 = t // 16
    k16 = tm // 16
    full = lambda a: pl.BlockSpec(a.shape, lambda i: (0,) * a.ndim)
    tok = pl.BlockSpec((tm, D_MLSTM), lambda i: (i, 0))
    heads = pl.BlockSpec((HEADS, tm, DH), lambda i: (0, i, 0))
    in_specs = [tok,
                pl.BlockSpec((16, D_MLSTM), lambda i: (jnp.maximum(i * k16 - 1, 0), 0)),
                pl.BlockSpec((16, D_MLSTM), lambda i: (jnp.minimum((i + 1) * k16, nb16 - 1), 0)),
                full(conv_w), full(conv_b), full(wq), full(wkt), full(wv),
                full(wiq), full(wik), full(wiv), full(bi)]
    out_specs = [heads,
                 pl.BlockSpec((D_MLSTM, tm), lambda i: (0, i)),
                 heads, tok,
                 pl.BlockSpec((16, tm), lambda i: (0, i))]
    out_shape = [jax.ShapeDtypeStruct((HEADS, t, DH), BF16),
                 jax.ShapeDtypeStruct((D_MLSTM, t), BF16),
                 jax.ShapeDtypeStruct((HEADS, t, DH), BF16),
                 jax.ShapeDtypeStruct((t, D_MLSTM), BF16),
                 jax.ShapeDtypeStruct((16, t), F32)]
    if fft:
        u4, f1, cw3, sw3 = fft
        blk = pl.BlockSpec((FGROUPS, CHUNK, FFT_ROWS, FCG), lambda j: (0, 0, j, 0))
        tw = pl.BlockSpec((FFT_ROWS, CHUNK, 1), lambda j: (j, 0, 0))
        in_specs += [blk, full(f1), tw, tw]
        out_specs += [blk, blk]
        out_shape += [jax.ShapeDtypeStruct(u4.shape, F32)] * 2
    return pl.pallas_call(
        _feat_kernel,
        grid=(t // tm,),
        in_specs=in_specs,
        out_specs=out_specs,
        out_shape=out_shape,
        compiler_params=_cparams("parallel"),
        name="feat",
    )(xm, xm, xm, conv_w, conv_b, wq, wkt, wv, wiq, wik, wiv, bi, *fft)


def _gates_kernel(g_ref, d_ref, gc_ref):
    g = g_ref[...]
    tl = g.shape[1]
    ig = g[0:8]
    fg = g[8:16]
    lf = jnp.minimum(fg, 0.0) - jnp.log(1.0 + jnp.exp(-jnp.abs(fg)))
    pos = lax.broadcasted_iota(jnp.int32, (8, tl), 1) & (CHUNK - 1)
    is_fwd = lax.broadcasted_iota(jnp.int32, (8, tl), 0) < HEADS

    def scan(x, op, ident):
        xf = x
        xb = x
        k = 1
        while k < CHUNK:
            xf = op(xf, jnp.where(pos >= k, pltpu.roll(xf, k, 1), ident))
            xb = op(xb, jnp.where(pos < CHUNK - k, pltpu.roll(xb, tl - k, 1), ident))
            k *= 2
        return jnp.where(is_fwd, xf, xb)

    b = scan(lf, jnp.add, 0.0)
    d = ig - b
    mloc = b + scan(d, jnp.maximum, NEG_BIG)
    d_ref[...] = d
    stack = jnp.concatenate([b, mloc, jnp.zeros((LANES - 16, tl), F32)], axis=0)
    gc_ref[...] = stack.T


def _gates(g, *, tl):
    t = g.shape[1]
    return pl.pallas_call(
        _gates_kernel,
        grid=(t // tl,),
        in_specs=[pl.BlockSpec((16, tl), lambda i: (0, i))],
        out_specs=[pl.BlockSpec((8, tl), lambda i: (0, i)),
                   pl.BlockSpec((tl, LANES), lambda i: (i, 0))],
        out_shape=[jax.ShapeDtypeStruct((8, t), F32),
                   jax.ShapeDtypeStruct((t, LANES), F32)],
        compiler_params=_cparams("parallel"),
        name="gates",
    )(g)


def _mlstm_kernel(*refs, cps, emit, ncast, nfft=0):
    (qf_ref, kf_ref, vf_ref, gcf_ref, drf_ref,
     qb_ref, kb_ref, vb_ref, gcb_ref, drb_ref, c0_ref, m0_ref) = refs[:12]
    cast_in = refs[12:12 + ncast]
    fft_in = refs[12 + ncast:12 + ncast + nfft]
    outs = refs[12 + ncast + nfft:]
    if emit:
        hf_ref, hb_ref, cfin_ref, mfin_ref = outs[:4]
        outs = outs[4:]
    else:
        cfin_ref, mfin_ref = outs[:2]
        outs = outs[2:]
        hf_ref = hb_ref = None
    cast_out = outs[:ncast]
    fft_out = outs[ncast:ncast + (1 if nfft else 0)]
    c_scr, m_scr = outs[ncast + (1 if nfft else 0):]
    step = pl.program_id(0)

    for src, dst in zip(cast_in, cast_out):
        dst[...] = src[...].astype(BF16)
    if nfft:
        _dft2_kernel(*fft_in, *fft_out)

    @pl.when(step == 0)
    def _():
        c_scr[...] = c0_ref[...]
        m_scr[...] = m0_ref[...]

    ti = lax.broadcasted_iota(jnp.int32, (CHUNK, CHUNK), 0)
    si = lax.broadcasted_iota(jnp.int32, (CHUNK, CHUNK), 1)
    nch = 2 * HEADS
    mask = jnp.concatenate([jnp.broadcast_to((si <= ti)[None], (HEADS, CHUNK, CHUNK)),
                            jnp.broadcast_to((si >= ti)[None], (HEADS, CHUNK, CHUNK))], axis=0)
    ones_blk = jnp.ones((nch, CHUNK, CHUNK), BF16)

    def bdot(a, b):
        return lax.dot_general(a, b, (((2,), (1,)), ((0,), (0,))), preferred_element_type=F32)

    def both(f, g):
        return [f(r) for r in range(HEADS)] + [g(r) for r in range(HEADS, nch)]

    for j in range(cps):
        rf = slice(j * CHUNK, (j + 1) * CHUNK)
        rb = slice((cps - 1 - j) * CHUNK, (cps - j) * CHUNK)
        gcf, gcb = gcf_ref[rf, :], gcb_ref[rb, :]
        drf, drb = drf_ref[:, rf], drb_ref[:, rb]
        lf, lb = CHUNK - 1, 0
        b = jnp.stack(both(lambda r: gcf[:, r:r + 1], lambda r: gcb[:, r:r + 1]))
        mloc = jnp.stack(both(lambda r: gcf[:, 8 + r:9 + r], lambda r: gcb[:, 8 + r:9 + r]))
        drow = jnp.stack(both(lambda r: drf[r:r + 1, :], lambda r: drb[r:r + 1, :]))
        btot = jnp.stack(both(lambda r: gcf[lf:lf + 1, r:r + 1], lambda r: gcb[lb:lb + 1, r:r + 1]))
        amax = jnp.stack(both(lambda r: gcf[lf:lf + 1, 8 + r:9 + r], lambda r: gcb[lb:lb + 1, 8 + r:9 + r]))
        m0 = jnp.stack([m_scr[r][0:1, 0:1] for r in range(nch)])
        c_aug = c_scr[...]
        kt = jnp.concatenate([kf_ref[:, :, rf], kb_ref[:, :, rb]], axis=0)
        vaug = jnp.concatenate([jnp.concatenate([vf_ref[:, rf, :], vb_ref[:, rb, :]], axis=0), ones_blk], axis=2)
        if emit:
            qh = jnp.concatenate([qf_ref[:, rf, :], qb_ref[:, rb, :]], axis=0)
            bm = b + m0
            m_t = jnp.maximum(bm, mloc)
            dmat = jnp.where(mask, jnp.exp((b - m_t) + drow), 0.0)
            smat = (bdot(qh, kt) * dmat).astype(BF16)
            q_in = (qh.astype(F32) * jnp.exp(bm - m_t)).astype(BF16)
            num = bdot(jnp.concatenate([smat, q_in], axis=2),
                       jnp.concatenate([vaug, c_aug.astype(BF16)], axis=1))
            den = jnp.maximum(jnp.abs(num[:, :, DH:]), jnp.exp(-m_t))
            h = (num[:, :, :DH] / den).astype(hf_ref.dtype)
            hf_ref[:, rf, :] = h[:HEADS]
            hb_ref[:, rb, :] = h[HEADS:]
        m_new = jnp.maximum(btot + m0, amax)
        decay = jnp.exp(btot + m0 - m_new)
        kw = (kt.astype(F32) * jnp.exp(btot + drow - m_new)).astype(BF16)
        c_scr[...] = decay * c_aug + bdot(kw, vaug)
        m_scr[...] = jnp.broadcast_to(m_new, (nch, 8, LANES))

    @pl.when(step == pl.num_programs(0) - 1)
    def _():
        cfin_ref[...] = c_scr[...]
        mfin_ref[...] = m_scr[...]


def _expert_slot(e):
    return (e % EPG) * N_GROUPS + e // EPG


def _mlstm(q, kt, v, gc, dr, c0, m0, *, cps, emit, casts=(), fft=()):
    t = q.shape[1]
    cb = cps * CHUNK
    nb = t // cb
    fwd_r = lambda i: (i, 0)
    bwd_r = lambda i: (nb - 1 - i, 0)
    fwd_c = lambda i: (0, i)
    bwd_c = lambda i: (0, nb - 1 - i)
    tok = lambda f: pl.BlockSpec((HEADS, cb, DH), lambda i, f=f: (0, f(i)[0], 0))
    in_specs = []
    for fr, fc in ((fwd_r, fwd_c), (bwd_r, bwd_c)):
        in_specs += [tok(fr), pl.BlockSpec((HEADS, DH, cb), lambda i, fc=fc: (0, 0, fc(i)[1])), tok(fr),
                     pl.BlockSpec((cb, LANES), fr), pl.BlockSpec((8, cb), fc)]
    cshape = (2 * HEADS, DH, 2 * DH)
    mshape = (2 * HEADS, 8, LANES)
    cspec = pl.BlockSpec(cshape, lambda i: (0, 0, 0))
    mspec = pl.BlockSpec(mshape, lambda i: (0, 0, 0))
    in_specs += [cspec, mspec]
    out_specs = [cspec, mspec]
    out_shape = [jax.ShapeDtypeStruct(cshape, F32), jax.ShapeDtypeStruct(mshape, F32)]
    if emit:
        out_specs = [tok(fwd_r), tok(bwd_r)] + out_specs
        out_shape = [jax.ShapeDtypeStruct((HEADS, t, DH), BF16)] * 2 + out_shape
    for a in casts:
        per = nb // a.shape[0]
        blk = (1, a.shape[1] // per, a.shape[2])
        in_specs.append(pl.BlockSpec(blk, lambda i, per=per: (i // per, i % per, 0)))
        out_specs.append(pl.BlockSpec(blk, lambda i, per=per: (_expert_slot(i // per), i % per, 0)))
        out_shape.append(jax.ShapeDtypeStruct(a.shape, BF16))
    if fft:
        yc, ys, a2, mix = fft
        tokf = pl.BlockSpec((FGROUPS, FFT_ROWS * CHUNK, FCG), lambda i: (0, i, 0))
        fullf = lambda a: pl.BlockSpec(a.shape, lambda i: (0,) * a.ndim)
        in_specs += [tokf, tokf, fullf(a2), fullf(mix)]
        out_specs.append(pl.BlockSpec((FGROUPS, CHUNK, FFT_ROWS, FCG), lambda i: (0, 0, i, 0)))
        out_shape.append(jax.ShapeDtypeStruct((FGROUPS, CHUNK, t // CHUNK, FCG), F32))
    return pl.pallas_call(
        functools.partial(_mlstm_kernel, cps=cps, emit=emit, ncast=len(casts), nfft=len(fft)),
        grid=(nb,),
        in_specs=in_specs,
        out_specs=out_specs,
        out_shape=out_shape,
        scratch_shapes=[pltpu.VMEM(cshape, F32), pltpu.VMEM(mshape, F32)],
        compiler_params=_cparams("arbitrary"),
        name="mlstm",
    )(q, kt, v, gc, dr, q, kt, v, gc, dr, c0, m0, *casts, *fft)


FFT_ROWS = 8


def _dft1_kernel(u_ref, f_ref, cw_ref, sw_ref, yc_ref, ys_ref):
    f = f_ref[...]
    rows = CHUNK * FFT_ROWS
    u2 = u_ref.reshape(FGROUPS * rows, FCG)
    yc2 = yc_ref.reshape(FGROUPS * rows, FCG)
    ys2 = ys_ref.reshape(FGROUPS * rows, FCG)
    for s in range(FFT_ROWS):
        pick = [pl.ds(g * rows + s, CHUNK, stride=FFT_ROWS) for g in range(FGROUPS)]
        x = jnp.concatenate([u2[p, :] for p in pick], axis=1).astype(BF16)
        y = _dot(f, x)
        cw = cw_ref[s]
        sw = sw_ref[s]
        pr = y[:CHUNK] * cw - y[CHUNK:] * sw
        pi = y[:CHUNK] * sw + y[CHUNK:] * cw
        for g in range(FGROUPS):
            yc2[pick[g], :] = pr[:, g * FCG:(g + 1) * FCG]
            ys2[pick[g], :] = pi[:, g * FCG:(g + 1) * FCG]


def _dft2_kernel(yc_ref, ys_ref, a2_ref, mix_ref, o_ref):
    a2 = a2_ref[...]
    rows = CHUNK * FFT_ROWS
    o2 = o_ref.reshape(FGROUPS * rows, FCG)
    for kk in range(FFT_ROWS):
        blk = slice(kk * CHUNK, (kk + 1) * CHUNK)
        yc = jnp.concatenate([yc_ref[g, blk, :] for g in range(FGROUPS)], axis=1)
        ys = jnp.concatenate([ys_ref[g, blk, :] for g in range(FGROUPS)], axis=1)
        p = jnp.concatenate([yc, ys], axis=0).astype(BF16)
        x = _dot(a2, p).astype(BF16)
        for g in range(FGROUPS):
            cols = slice(g * FCG, (g + 1) * FCG)
            cat = jnp.concatenate([x[:CHUNK, cols], x[CHUNK:, cols]], axis=1)
            o2[pl.ds(g * rows + kk, CHUNK, stride=FFT_ROWS), :] = _dot(cat, mix_ref[g])


def _merge_kernel(hf_ref, hb_ref, act_ref, z_ref, yf_ref, x_ref, er_ref, ec_ref,
                  nw_ref, sk_ref, wout_ref, gg1_ref, g2_ref, sh2_ref, wr_ref, br_ref,
                  x1_ref, h2_ref, lg_ref):
    parts = []
    for hd in range(HEADS):
        hh = hf_ref[hd].astype(F32) + hb_ref[hd].astype(F32)
        dl = hh - jnp.mean(hh, axis=-1, keepdims=True)
        var = jnp.mean(dl * dl, axis=-1, keepdims=True)
        parts.append(dl * lax.rsqrt(var + EPS))
    hn = jnp.concatenate(parts, axis=-1)
    z = z_ref[...].astype(F32)
    m = (hn * nw_ref[...] + sk_ref[...] * act_ref[...].astype(F32)) * (z * _sigmoid(z))
    cat = jnp.concatenate([m.astype(BF16)] + [yf_ref[g].astype(BF16) for g in range(FGROUPS)], axis=-1)
    y = _dot(cat, wout_ref[...])
    x3 = _add_pos(x_ref[...], er_ref, ec_ref)
    xp = x3.reshape(x3.shape[0] * GRID_W, D_MODEL)
    x1 = xp + _rms(y) * gg1_ref[...]
    x1_ref[...] = x1
    h2 = _rms(x1) * g2_ref[...] + sh2_ref[...]
    h2_ref[...] = h2.astype(BF16)
    lg = _dot(h2.astype(BF16), wr_ref[...]) + br_ref[...]
    lg_ref[...] = lg.T[:32]


def _merge(hf, hb, act, z, yf, x3, er3, ec3, nw, sk, wout, gg1, g2, sh2, wr, br, *, rows):
    nr = x3.shape[0]
    t = nr * GRID_W
    tm = rows * GRID_W
    tok = pl.BlockSpec((tm, D_MLSTM), lambda i: (i, 0))
    heads = pl.BlockSpec((HEADS, tm, DH), lambda i: (0, i, 0))
    full = lambda a: pl.BlockSpec(a.shape, lambda i: (0,) * a.ndim)
    return pl.pallas_call(
        _merge_kernel,
        grid=(nr // rows,),
        in_specs=[heads, heads, tok, tok, heads,
                  pl.BlockSpec((rows, GRID_W, D_MODEL), lambda i: (i, 0, 0)),
                  pl.BlockSpec((rows, 1, D_MODEL // 2), lambda i: (i, 0, 0)),
                  pl.BlockSpec((1, GRID_W, D_MODEL // 2), lambda i: (0, 0, 0)),
                  full(nw), full(sk), full(wout), full(gg1), full(g2), full(sh2), full(wr), full(br)],
        out_specs=[pl.BlockSpec((tm, D_MODEL), lambda i: (i, 0)),
                   pl.BlockSpec((tm, D_MODEL), lambda i: (i, 0)),
                   pl.BlockSpec((32, tm), lambda i: (0, i))],
        out_shape=[jax.ShapeDtypeStruct((t, D_MODEL), F32),
                   jax.ShapeDtypeStruct((t, D_MODEL), BF16),
                   jax.ShapeDtypeStruct((32, t), F32)],
        compiler_params=_cparams("parallel"),
        name="merge",
    )(hf, hb, act, z, yf, x3, er3, ec3, nw, sk, wout, gg1, g2, sh2, wr, br)


def _route_kernel(lg_ref, pos_ref, w_ref, cnt_ref, *, sbk):
    lg = lg_ref[...]
    tl = lg.shape[1]
    g = [lg[j:j + 1] for j in range(N_GROUPS)]
    e = [lg[N_GROUPS + j:N_GROUPS + j + 1] for j in range(N_EXPERTS)]
    gmax = jnp.maximum(jnp.maximum(g[0], g[1]), jnp.maximum(g[2], g[3]))
    den = jnp.exp(g[0] - gmax) + jnp.exp(g[1] - gmax) + jnp.exp(g[2] - gmax) + jnp.exp(g[3] - gmax)
    p_sel = 1.0 / den
    sel = []
    free = jnp.ones((1, tl), F32)
    for j in range(N_GROUPS):
        s = jnp.where(g[j] >= gmax, free, 0.0)
        sel.append(s)
        free = free - s
    es = []
    for j in range(EPG):
        es.append(sel[0] * e[j] + sel[1] * e[EPG + j] + sel[2] * e[2 * EPG + j] + sel[3] * e[3 * EPG + j])
    rank = []
    for j in range(EPG):
        rj = jnp.zeros((1, tl), F32)
        for i in range(EPG):
            if i == j:
                continue
            beats = (es[i] >= es[j]) if i < j else (es[i] > es[j])
            rj = rj + jnp.where(beats, 1.0, 0.0)
        rank.append(rj)
    v1 = jnp.maximum(jnp.maximum(es[0], es[1]), jnp.maximum(es[2], es[3]))
    v2 = sum(jnp.where(rank[j] == 1.0, es[j], 0.0) for j in range(EPG))
    tt = jnp.exp(v2 - v1)
    w1 = p_sel / (1.0 + tt)
    w2 = w1 * tt
    w = [jnp.where(rank[j] == 0.0, w1, jnp.where(rank[j] == 1.0, w2, 0.0)) for j in range(EPG)]
    top2 = [jnp.where(rank[j] < 2.0, 1.0, 0.0) for j in range(EPG)]
    mem = jnp.concatenate([sel[gi] * top2[j] for gi in range(N_GROUPS) for j in range(EPG)], axis=0)
    wts = jnp.concatenate([sel[gi] * w[j] for gi in range(N_GROUPS) for j in range(EPG)], axis=0)
    w_ref[...] = wts
    lane = lax.broadcasted_iota(jnp.int32, (N_EXPERTS, tl), 1) & (sbk - 1)
    c = mem
    k = 1
    while k < sbk:
        c = c + jnp.where(lane >= k, pltpu.roll(c, k, 1), 0.0)
        k *= 2
    pos_ref[...] = jnp.where(mem > 0.0, c - 1.0, -1.0)
    lane128 = lax.broadcasted_iota(jnp.int32, (N_EXPERTS, LANES), 1)
    cnt = jnp.zeros((N_EXPERTS, LANES), F32)
    for kb in range(tl // sbk):
        tot = jnp.sum(mem[:, kb * sbk:(kb + 1) * sbk], axis=1, keepdims=True)
        cnt = cnt + jnp.where(lane128 == kb, tot, 0.0)
    cnt_ref[...] = cnt


def _route(lg, *, tl, sbk):
    t = lg.shape[1]
    row = pl.BlockSpec((N_EXPERTS, tl), lambda i: (0, i))
    return pl.pallas_call(
        functools.partial(_route_kernel, sbk=sbk),
        grid=(t // tl,),
        in_specs=[pl.BlockSpec((32, tl), lambda i: (0, i))],
        out_specs=[row, row, pl.BlockSpec((N_EXPERTS, LANES), lambda i: (i, 0))],
        out_shape=[jax.ShapeDtypeStruct((N_EXPERTS, t), F32),
                   jax.ShapeDtypeStruct((N_EXPERTS, t), F32),
                   jax.ShapeDtypeStruct((t // tl * N_EXPERTS, LANES), F32)],
        compiler_params=_cparams("parallel"),
        name="route",
    )(lg)


def _slots_kernel(pos_ref, w_ref, first_ref, qrow_ref, qcol_ref):
    tl = pos_ref.shape[1]
    q0 = jnp.full((1, tl), -1.0, F32)
    q1 = jnp.full((1, tl), -1.0, F32)
    w0 = jnp.zeros((1, tl), F32)
    w1 = jnp.zeros((1, tl), F32)
    seen = jnp.zeros((1, tl), F32)
    for ex in range(N_EXPERTS):
        rk = pos_ref[ex:ex + 1, :]
        wt = w_ref[ex:ex + 1, :]
        m = jnp.where(rk >= 0.0, 1.0, 0.0)
        val = rk + first_ref[ex:ex + 1, :]
        first = (m * (1.0 - seen)) > 0.0
        second = (m * seen) > 0.0
        q0 = jnp.where(first, val, q0)
        w0 = jnp.where(first, wt, w0)
        q1 = jnp.where(second, val, q1)
        w1 = jnp.where(second, wt, w1)
        seen = seen + m
    qrow_ref[...] = jnp.concatenate([q0, q1, jnp.zeros((6, tl), F32)], axis=0)
    qcol_ref[...] = jnp.concatenate([q0, q1, w0, w1, jnp.zeros((LANES - 4, tl), F32)], axis=0).T


def _slots(pos, w, first, *, tl):
    t = pos.shape[1]
    row = pl.BlockSpec((N_EXPERTS, tl), lambda i: (0, i))
    return pl.pallas_call(
        _slots_kernel,
        grid=(t // tl,),
        in_specs=[row, row, row],
        out_specs=[pl.BlockSpec((8, tl), lambda i: (0, i)),
                   pl.BlockSpec((tl, LANES), lambda i: (i, 0))],
        out_shape=[jax.ShapeDtypeStruct((8, t), F32),
                   jax.ShapeDtypeStruct((t, LANES), F32)],
        compiler_params=_cparams("parallel"),
        name="slots",
    )(pos, w, first)


def _mlp(x, wg, wu, wd):
    gt = _dot(x, wg)
    a = ((gt * _sigmoid(gt)) * _dot(x, wu)).astype(BF16)
    return _dot(a, wd).astype(BF16)


def _moe_kernel(tab_ref, loc_ref, h_ref, qrow_ref, wg_ref, wu_ref, wd_ref,
                qcol_ref, x1_ref, gg2_ref, o_ref, xs_ref, *, sbk, slots, csb):
    i = pl.program_id(0)
    step = pl.program_id(1)
    nsb = h_ref.shape[0] // sbk
    eps = wg_ref.shape[0]
    exp_steps = N_EXPERTS // eps

    @pl.when(step == 0)
    def _():
        pid = lax.broadcasted_iota(jnp.int32, (slots, sbk), 0).astype(F32)

        def select(sb, carry):
            q0 = qrow_ref[0, pl.ds(sb, 1), :]
            q1 = qrow_ref[1, pl.ds(sb, 1), :]
            s = jnp.where(q0 == pid, 1.0, jnp.where(q1 == pid, 1.0, 0.0)).astype(BF16)
            row0 = pl.multiple_of(sb * sbk, sbk)
            xs_ref[sb, 0:slots, :] = _dot(s, h_ref[pl.ds(row0, sbk), :]).astype(BF16)
            xs_ref[sb, slots:slots + 32, :] = jnp.zeros((32, D_MODEL), BF16)
            return carry

        lax.fori_loop(0, nsb, select, 0)

    def run_pieces(k, base, n_pieces):
        locs = []
        for m in range(n_pieces):
            d = loc_ref[0, 0, base + m]
            locs.append((d // MOE_LOC, pl.multiple_of(d % MOE_LOC, 16)))
        x = jnp.concatenate([xs_ref[sb, pl.ds(off, 16), :] for sb, off in locs], axis=0)
        y = _mlp(x, wg_ref[k], wu_ref[k], wd_ref[k])
        for m, (sb, off) in enumerate(locs):
            xs_ref[sb, pl.ds(off, 16), :] = y[m * 16:(m + 1) * 16]

    @pl.when(jnp.logical_and(step >= 1, step <= exp_steps))
    def _():
        def expert(k, carry):
            e = _expert_slot((step - 1) * eps + k)
            total = tab_ref[i * N_EXPERTS + e]
            base = e * MOE_PIECES

            def full(t, c1):
                run_pieces(k, base + t * 32, 32)
                return c1

            lax.fori_loop(0, total // 32, full, 0)
            done = (total // 32) * 32

            @pl.when(total - done > 16)
            def _():
                run_pieces(k, base + done, 16)

            done = done + jnp.where(total - done > 16, 16, 0)
            rest = total - done

            @pl.when(rest > 8)
            def _():
                run_pieces(k, base + done, 16)

            @pl.when(jnp.logical_and(rest > 0, rest <= 8))
            def _():
                run_pieces(k, base + done, 8)

            return carry

        lax.fori_loop(0, eps, expert, 0)

    @pl.when(step > exp_steps)
    def _():
        lane = lax.broadcasted_iota(jnp.int32, (sbk, slots), 1).astype(F32)
        for k in range(csb):
            sb = (step - 1 - exp_steps) * csb + k
            rows = slice(k * sbk, (k + 1) * sbk)
            qc = qcol_ref[rows, :]
            wmat = (jnp.where(lane == qc[:, 0:1], qc[:, 2:3], 0.0)
                    + jnp.where(lane == qc[:, 1:2], qc[:, 3:4], 0.0)).astype(BF16)
            y = _dot(wmat, xs_ref[sb, 0:slots, :])
            o_ref[rows, :] = x1_ref[rows, :] + _rms(y) * gg2_ref[...]


def _moe(tab, loc, h2, qrow3, wg, wu, wd, qcol, x1, gg2, *, tb, sbk, slots, eps, csb):
    t = h2.shape[0]
    nsb = tb // sbk
    comb_steps = nsb // csb
    exp_steps = N_EXPERTS // eps
    wblk = lambda i, s, c: (jnp.where(s == 0, exp_steps - 1, jnp.minimum(s - 1, exp_steps - 1)), 0, 0)
    oblk = lambda i, s, c: (i * comb_steps + jnp.maximum(s - 1 - exp_steps, 0), 0)
    return pl.pallas_call(
        functools.partial(_moe_kernel, sbk=sbk, slots=slots, csb=csb),
        grid_spec=pltpu.PrefetchScalarGridSpec(
            num_scalar_prefetch=1,
            grid=(t // tb, 1 + exp_steps + comb_steps),
            in_specs=[pl.BlockSpec((1, 1, N_EXPERTS * MOE_PIECES), lambda i, s, c: (i, 0, 0), memory_space=pltpu.SMEM),
                      pl.BlockSpec((tb, D_MODEL), lambda i, s, c: (i, 0)),
                      pl.BlockSpec((8, nsb, sbk), lambda i, s, c: (0, i, 0)),
                      pl.BlockSpec((eps, D_MODEL, D_EXPERT), wblk),
                      pl.BlockSpec((eps, D_MODEL, D_EXPERT), wblk),
                      pl.BlockSpec((eps, D_EXPERT, D_MODEL), wblk),
                      pl.BlockSpec((csb * sbk, LANES), oblk),
                      pl.BlockSpec((csb * sbk, D_MODEL), oblk),
                      pl.BlockSpec((1, D_MODEL), lambda i, s, c: (0, 0))],
            out_specs=pl.BlockSpec((csb * sbk, D_MODEL), oblk),
            scratch_shapes=[pltpu.VMEM((nsb, slots + 32, D_MODEL), BF16)]),
        out_shape=jax.ShapeDtypeStruct((t, D_MODEL), F32),
        compiler_params=pltpu.CompilerParams(dimension_semantics=("parallel", "arbitrary"),
                                             vmem_limit_bytes=MOE_VMEM_LIMIT),
        name="moe",
    )(tab, loc, h2, qrow3, wg, wu, wd, qcol, x1, gg2)


def _slot_tables(cnt, *, nsb, slots):
    tiles = (cnt + 15) // 16
    first = (jnp.cumsum(tiles, axis=1) - tiles) * 16
    tl = tiles.reshape(-1, nsb, N_EXPERTS)
    end = jnp.cumsum(tl, axis=1)
    beg = end - tl
    q = jnp.arange(MOE_PIECES, dtype=jnp.int32)[None, None, None, :]
    inside = jnp.logical_and(q >= beg[..., None], q < end[..., None])
    sb = jnp.arange(nsb, dtype=jnp.int32)[None, :, None, None]
    code = sb * MOE_LOC + first.reshape(-1, nsb, N_EXPERTS)[..., None] + (q - beg[..., None]) * 16
    loc = jnp.sum(jnp.where(inside, code, 0), axis=1)
    total = end[:, -1, :]
    dump = slots + 16 * (q[0, 0] % 2)
    loc = jnp.where(q[0] < total[..., None], loc, dump)
    return first, total.reshape(-1), loc.reshape(loc.shape[0], 1, -1)


def _pos_tables(rows):
    quarter = D_MODEL // 4
    freq = 1.0 / (POS_BASE ** (np.arange(quarter, dtype=np.float64) / quarter))
    r = np.arange(rows, dtype=np.float64)[:, None] * freq
    cl = np.arange(GRID_W, dtype=np.float64)[:, None] * freq
    er = np.concatenate([np.sin(r), np.cos(r)], axis=-1).astype(np.float32)
    ec = np.concatenate([np.sin(cl), np.cos(cl)], axis=-1).astype(np.float32)
    return jnp.asarray(er[:, None, :]), jnp.asarray(ec[None, :, :])


def _dft_tables(t):
    n = np.arange(CHUNK, dtype=np.int64)
    prod = n[:, None] * n[None, :]
    ang = (prod % CHUNK).astype(np.float64) * (2.0 * np.pi / CHUNK)
    c, s = np.cos(ang), np.sin(ang)
    f1 = np.concatenate([c, s], axis=0)
    a2 = np.concatenate([np.concatenate([c, -s], axis=1), np.concatenate([s, c], axis=1)], axis=0)
    cs = np.concatenate([c, -s], axis=0)
    angw = prod.astype(np.float64) * (2.0 * np.pi / t)
    f32 = lambda a: jnp.asarray(a.astype(np.float32))
    return f32(f1), f32(a2), f32(cs), f32(np.cos(angw)[:, :, None]), f32(np.sin(angw)[:, :, None])


def _blockdiag(w):
    n = w.shape[0]
    size = n * QKV_BLOCK
    spread = np.tile(np.eye(QKV_BLOCK, dtype=np.float32), (1, n))
    rows = jnp.dot(w.reshape(size, QKV_BLOCK), jnp.asarray(spread), precision=lax.Precision.HIGHEST)
    blk = np.arange(size) // QKV_BLOCK
    mask = (blk[:, None] == blk[None, :]).astype(np.float32)
    return rows * jnp.asarray(mask)


def _gate_weights(w_f, b_f, w_b, b_b):
    w = jnp.concatenate([w_f[:, :HEADS], w_b[:, :HEADS], w_f[:, HEADS:], w_b[:, HEADS:]], axis=1).T
    b = jnp.concatenate([b_f[:HEADS], b_b[:HEADS], b_f[HEADS:], b_b[HEADS:]])
    return w.astype(BF16), b[:, None]


def kernel(x, c, ctx, c_ctx, w_ada, b_ada, g_pre_mix, g_post_mix, g_pre_ffn, g_post_ffn,
           w_in, conv_w, conv_b, w_q, w_k, w_v, w_if_fwd, b_if_fwd, w_if_bwd, b_if_bwd,
           mlstm_norm_w, mlstm_skip, w_fourier, w_out, w_router_group, b_router_group,
           w_router_expert, b_router_expert, w_gate, w_up, w_down):
    t = x.shape[1]
    rows = t // GRID_W

    c8 = jnp.concatenate([c, c_ctx[None, :], jnp.zeros((6, D_MODEL), F32)], axis=0)
    mod = _ada(c8, w_ada[0], b_ada[0][None, :])
    shift1, scale1, gate1, shift2, scale2, gate2 = [mod[0:1, k * D_MODEL:(k + 1) * D_MODEL] for k in range(N_MOD)]
    shift1c, scale1c = mod[1:2, 0:D_MODEL], mod[1:2, D_MODEL:2 * D_MODEL]
    g1 = g_pre_mix[0][None, :] * (1.0 + scale1)
    g1c = g_pre_mix[0][None, :] * (1.0 + scale1c)
    gg1 = g_post_mix[0][None, :] * gate1
    g2 = g_pre_ffn[0][None, :] * (1.0 + scale2)
    gg2 = g_post_ffn[0][None, :] * gate2

    er3, ec3 = _pos_tables(rows)
    x3 = x.reshape(rows, GRID_W, D_MODEL)
    ctx3 = ctx.reshape(CTX_LEN // GRID_W, GRID_W, D_MODEL)
    w_in_bf = w_in[0].astype(BF16)

    xm_l, z_l, u_l = _inproj(x3, er3, ec3, g1, shift1, w_in_bf, rows=8, add_pos=True)
    xm_c, _, _ = _inproj(ctx3, er3, ec3, g1c, shift1c, w_in_bf, rows=CTX_LEN // GRID_W, add_pos=False)

    wq = _blockdiag(w_q[0]).astype(BF16)
    wkt = _blockdiag(w_k[0]).T.astype(BF16)
    wv = _blockdiag(w_v[0]).astype(BF16)
    wi, bi = _gate_weights(w_if_fwd[0], b_if_fwd[0], w_if_bwd[0], b_if_bwd[0])
    wiq, wik, wiv = wi[:, :D_MLSTM], wi[:, D_MLSTM:2 * D_MLSTM], wi[:, 2 * D_MLSTM:]
    cb = conv_b[0][None, :]
    f1, a2, cs, cw3, sw3 = _dft_tables(t)
    n1 = t // CHUNK
    q_l, kt_l, v_l, act_l, gp_l, yc, ys = _feat(
        xm_l, conv_w[0], cb, wq, wkt, wv, wiq, wik, wiv, bi, tm=t // (n1 // FFT_ROWS),
        fft=(u_l.reshape(FGROUPS, n1, CHUNK, FCG), f1.astype(BF16), cw3, sw3))
    q_c, kt_c, v_c, _, gp_c = _feat(xm_c, conv_w[0], cb, wq, wkt, wv, wiq, wik, wiv, bi, tm=CTX_LEN)

    dr_l, gc_l = _gates(gp_l, tl=2048)
    dr_c, gc_c = _gates(gp_c, tl=CTX_LEN)

    mix = jnp.einsum('kc,gcd->gkd', cs, w_fourier[0], precision=lax.Precision.HIGHEST)
    mix = (mix * float(1.0 / np.sqrt(float(t) * FCG))).astype(BF16)

    c0 = jnp.zeros((2 * HEADS, DH, 2 * DH), F32)
    m0 = jnp.zeros((2 * HEADS, 8, LANES), F32)
    kt_c, kt_l = kt_c.reshape(HEADS, DH, CTX_LEN), kt_l.reshape(HEADS, DH, t)
    c_ctx_fin, m_ctx_fin = _mlstm(q_c, kt_c, v_c, gc_c, dr_c, c0, m0, cps=CTX_LEN // CHUNK, emit=False)
    hf, hb, _, _, wg_bf, wu_bf, wd_bf, yf = _mlstm(
        q_l, kt_l, v_l, gc_l, dr_l, c_ctx_fin, m_ctx_fin, cps=8, emit=True,
        casts=(w_gate[0], w_up[0], w_down[0]),
        fft=(yc.reshape(FGROUPS, t, FCG), ys.reshape(FGROUPS, t, FCG), a2.astype(BF16), mix))
    yf = yf.reshape(FGROUPS, t, FCG)

    wr = jnp.concatenate([w_router_group[0], w_router_expert[0],
                          jnp.zeros((D_MODEL, LANES - N_GROUPS - N_EXPERTS), F32)], axis=1)
    br = jnp.concatenate([b_router_group[0], b_router_expert[0],
                          jnp.zeros((LANES - N_GROUPS - N_EXPERTS,), F32)])[None, :]
    x1, h2, lg = _merge(hf, hb, act_l, z_l, yf, x3, er3, ec3,
                        mlstm_norm_w[0][None, :], mlstm_skip[0][None, :], w_out[0].astype(BF16),
                        gg1, g2, shift2, wr.astype(BF16), br, rows=16)
    pos, wts, cnt = _route(lg, tl=MOE_TB, sbk=MOE_SBK)
    nblk, nsb = t // MOE_TB, MOE_TB // MOE_SBK
    cnt = cnt.reshape(nblk, N_EXPERTS, LANES)[:, :, :nsb]
    cnt = jnp.transpose(cnt, (0, 2, 1)).reshape(nblk * nsb, N_EXPERTS).astype(jnp.int32)
    first, total, loc = _slot_tables(cnt, nsb=nsb, slots=MOE_SLOTS)
    first_rows = jnp.repeat(first.T.astype(F32), MOE_SBK, axis=1)
    qrow, qcol = _slots(pos, wts, first_rows, tl=MOE_TB)
    out = _moe(total, loc, h2, qrow.reshape(8, t // MOE_SBK, MOE_SBK), wg_bf, wu_bf, wd_bf,
               qcol, x1, gg2, tb=MOE_TB, sbk=MOE_SBK, slots=MOE_SLOTS, eps=MOE_EPS, csb=MOE_CSB)
    return out[None]
```

```python
import functools

import numpy as np
import jax
import jax.numpy as jnp
from jax import lax
from jax.experimental import pallas as pl
from jax.experimental.pallas import tpu as pltpu

F32 = jnp.float32
BF16 = jnp.bfloat16

D_MODEL = 1024
SEQ = 16384
GRID_W = 64
CTX_LEN = 256
D_MLSTM = 512
HEADS = 4
DH = 128
QKV_BLOCK = 4
CONV_K = 3
CHUNK = 128
D_FOURIER = 512
FGROUPS = 4
FCG = 128
N_GROUPS = 4
EPG = 4
N_EXPERTS = 16
D_EXPERT = 512
N_MOD = 6
EPS = 1e-6
POS_BASE = 10000.0
LANES = 128
NEG_BIG = -3.0e38

VMEM_LIMIT = 58 * 1024 * 1024
MOE_VMEM_LIMIT = 58 * 1024 * 1024
MOE_TB = 2048
MOE_SBK = 256
MOE_SLOTS = 768
MOE_EPS = 4
MOE_CSB = 2
MOE_LOC = 1024
MOE_PIECES = MOE_TB // 16


def _cparams(*sem):
    return pltpu.CompilerParams(dimension_semantics=sem, vmem_limit_bytes=VMEM_LIMIT)


def _dot(a, b):
    return jnp.dot(a, b, preferred_element_type=F32)


def _dot_nt(a, b):
    return lax.dot_general(a, b, (((1,), (1,)), ((), ())), preferred_element_type=F32)


def _split_bf16(a):
    hi = a.astype(BF16)
    lo = (a - hi.astype(F32)).astype(BF16)
    return hi, lo


def _dot3(a, b):
    a_hi, a_lo = _split_bf16(a)
    b_hi, b_lo = _split_bf16(b)
    return _dot(a_hi, b_hi) + (_dot(a_hi, b_lo) + _dot(a_lo, b_hi))


def _sigmoid(x):
    return 1.0 / (1.0 + jnp.exp(-x))


def _rms(x):
    return x * lax.rsqrt(jnp.mean(x * x, axis=-1, keepdims=True) + EPS)


def _ada_kernel(c_ref, w_ref, b_ref, o_ref):
    c = c_ref[...]
    s = c * _sigmoid(c)
    o_ref[...] = _dot3(s, w_ref[...]) + b_ref[...]


def _ada(c8, w, b):
    n = w.shape[1]
    tn = 768
    return pl.pallas_call(
        _ada_kernel,
        grid=(n // tn,),
        in_specs=[pl.BlockSpec((8, D_MODEL), lambda j: (0, 0)),
                  pl.BlockSpec((D_MODEL, tn), lambda j: (0, j)),
                  pl.BlockSpec((1, tn), lambda j: (0, j))],
        out_specs=pl.BlockSpec((8, tn), lambda j: (0, j)),
        out_shape=jax.ShapeDtypeStruct((8, n), F32),
        compiler_params=_cparams("parallel"),
        name="ada",
    )(c8, w, b)


def _add_pos(x3, er_ref, ec_ref):
    r = x3.shape[0]
    pr = jnp.broadcast_to(er_ref[...], (r, GRID_W, D_MODEL // 2))
    pc = jnp.broadcast_to(ec_ref[...], (r, GRID_W, D_MODEL // 2))
    return x3 + jnp.concatenate([pr, pc], axis=-1)


def _inproj_kernel(x_ref, er_ref, ec_ref, g_ref, sh_ref, w_ref, xm_ref, z_ref, u_ref, *, add_pos):
    x3 = x_ref[...]
    if add_pos:
        x3 = _add_pos(x3, er_ref, ec_ref)
    x = x3.reshape(x3.shape[0] * GRID_W, D_MODEL)
    h = _rms(x) * g_ref[...] + sh_ref[...]
    proj = _dot(h.astype(BF16), w_ref[...])
    xm_ref[...] = proj[:, :D_MLSTM].astype(BF16)
    z_ref[...] = proj[:, D_MLSTM:2 * D_MLSTM].astype(BF16)
    for g in range(FGROUPS):
        u_ref[g] = proj[:, 2 * D_MLSTM + g * FCG:2 * D_MLSTM + (g + 1) * FCG]


def _inproj(x3, er3, ec3, g_eff, shift, w_in, *, rows, add_pos):
    nr = x3.shape[0]
    t = nr * GRID_W
    tm = rows * GRID_W
    out = jax.ShapeDtypeStruct((t, D_MLSTM), BF16)
    ospec = pl.BlockSpec((tm, D_MLSTM), lambda i: (i, 0))
    vec = pl.BlockSpec((1, D_MODEL), lambda i: (0, 0))
    return pl.pallas_call(
        functools.partial(_inproj_kernel, add_pos=add_pos),
        grid=(nr // rows,),
        in_specs=[pl.BlockSpec((rows, GRID_W, D_MODEL), lambda i: (i, 0, 0)),
                  pl.BlockSpec((rows, 1, D_MODEL // 2), lambda i: (i, 0, 0)),
                  pl.BlockSpec((1, GRID_W, D_MODEL // 2), lambda i: (0, 0, 0)),
                  vec, vec,
                  pl.BlockSpec(w_in.shape, lambda i: (0, 0))],
        out_specs=[ospec, ospec, pl.BlockSpec((FGROUPS, tm, FCG), lambda i: (0, i, 0))],
        out_shape=[out, out, jax.ShapeDtypeStruct((FGROUPS, t, FCG), F32)],
        compiler_params=_cparams("parallel"),
        name="inproj",
    )(x3, er3, ec3, g_eff, shift, w_in)


def _feat_kernel(xm_ref, prev_ref, next_ref, cw_ref, cb_ref, wq_ref, wkt_ref, wv_ref,
                 wiq_ref, wik_ref, wiv_ref, bi_ref, *rest):
    if len(rest) > 5:
        _dft1_kernel(*rest[:4], *rest[9:])
        rest = rest[4:9]
    q_ref, kt_ref, v_ref, act_ref, g_ref = rest
    i = pl.program_id(0)
    n = pl.num_programs(0)
    xm_bf = xm_ref[...]
    xm = xm_bf.astype(F32)
    tm = xm.shape[0]
    prev_row = prev_ref[...].astype(F32)[15:16, :] * jnp.where(i > 0, 1.0, 0.0)
    next_row = next_ref[...].astype(F32)[0:1, :] * jnp.where(i < n - 1, 1.0, 0.0)
    rid = lax.broadcasted_iota(jnp.int32, (tm, 1), 0)
    x_left = jnp.where(rid == 0, prev_row, pltpu.roll(xm, 1, 0))
    x_right = jnp.where(rid == tm - 1, next_row, pltpu.roll(xm, tm - 1, 0))
    cw = cw_ref[...]
    y = cw[0:1] * x_left + cw[1:2] * xm + cw[2:3] * x_right + cb_ref[...]
    act = (y * _sigmoid(y)).astype(BF16)
    act_ref[...] = act
    q = _dot(act, wq_ref[...])
    kt = _dot_nt(wkt_ref[...], act)
    v = _dot(xm_bf, wv_ref[...])
    q_bf = q.astype(BF16)
    kt_bf = kt.astype(BF16)
    v_bf = v.astype(BF16)
    q_s = (q * (DH ** -0.5)).astype(BF16)
    for hd in range(HEADS):
        q_ref[hd] = q_s[:, hd * DH:(hd + 1) * DH]
        v_ref[hd] = v_bf[:, hd * DH:(hd + 1) * DH]
    kt_ref[...] = kt_bf
    g = _dot_nt(wiq_ref[...], q_bf) + _dot(wik_ref[...], kt_bf) + _dot_nt(wiv_ref[...], v_bf)
    g_ref[...] = g + bi_ref[...]


def _feat(xm, conv_w, conv_b, wq, wkt, wv, wiq, wik, wiv, bi, *, tm, fft=()):
    t = xm.shape[0]
    nb16 = t // 16
    k16 = tm // 16
    full = lambda a: pl.BlockSpec(a.shape, lambda i: (0,) * a.ndim)
    tok = pl.BlockSpec((tm, D_MLSTM), lambda i: (i, 0))
    heads = pl.BlockSpec((HEADS, tm, DH), lambda i: (0, i, 0))
    in_specs = [tok,
                pl.BlockSpec((16, D_MLSTM), lambda i: (jnp.maximum(i * k16 - 1, 0), 0)),
                pl.BlockSpec((16, D_MLSTM), lambda i: (jnp.minimum((i + 1) * k16, nb16 - 1), 0)),
                full(conv_w), full(conv_b), full(wq), full(wkt), full(wv),
                full(wiq), full(wik), full(wiv), full(bi)]
    out_specs = [heads,
                 pl.BlockSpec((D_MLSTM, tm), lambda i: (0, i)),
                 heads, tok,
                 pl.BlockSpec((16, tm), lambda i: (0, i))]
    out_shape = [jax.ShapeDtypeStruct((HEADS, t, DH), BF16),
                 jax.ShapeDtypeStruct((D_MLSTM, t), BF16),
                 jax.ShapeDtypeStruct((HEADS, t, DH), BF16),
                 jax.ShapeDtypeStruct((t, D_MLSTM), BF16),
                 jax.ShapeDtypeStruct((16, t), F32)]
    if fft:
        u4, f1, cw3, sw3 = fft
        blk = pl.BlockSpec((FGROUPS, CHUNK, FFT_ROWS, FCG), lambda j: (0, 0, j, 0))
        tw = pl.BlockSpec((FFT_ROWS, CHUNK, 1), lambda j: (j, 0, 0))
        in_specs += [blk, full(f1), tw, tw]
        out_specs += [blk, blk]
        out_shape += [jax.ShapeDtypeStruct(u4.shape, F32)] * 2
    return pl.pallas_call(
        _feat_kernel,
        grid=(t // tm,),
        in_specs=in_specs,
        out_specs=out_specs,
        out_shape=out_shape,
        compiler_params=_cparams("parallel"),
        name="feat",
    )(xm, xm, xm, conv_w, conv_b, wq, wkt, wv, wiq, wik, wiv, bi, *fft)


def _gates_kernel(g_ref, d_ref, gc_ref):
    g = g_ref[...]
    tl = g.shape[1]
    ig = g[0:8]
    fg = g[8:16]
    lf = jnp.minimum(fg, 0.0) - jnp.log(1.0 + jnp.exp(-jnp.abs(fg)))
    pos = lax.broadcasted_iota(jnp.int32, (8, tl), 1) & (CHUNK - 1)
    is_fwd = lax.broadcasted_iota(jnp.int32, (8, tl), 0) < HEADS

    def scan(x, op, ident):
        xf = x
        xb = x
        k = 1
        while k < CHUNK:
            xf = op(xf, jnp.where(pos >= k, pltpu.roll(xf, k, 1), ident))
            xb = op(xb, jnp.where(pos < CHUNK - k, pltpu.roll(xb, tl - k, 1), ident))
            k *= 2
        return jnp.where(is_fwd, xf, xb)

    b = scan(lf, jnp.add, 0.0)
    d = ig - b
    mloc = b + scan(d, jnp.maximum, NEG_BIG)
    d_ref[...] = d
    stack = jnp.concatenate([b, mloc, jnp.zeros((LANES - 16, tl), F32)], axis=0)
    gc_ref[...] = stack.T


def _gates(g, *, tl):
    t = g.shape[1]
    return pl.pallas_call(
        _gates_kernel,
        grid=(t // tl,),
        in_specs=[pl.BlockSpec((16, tl), lambda i: (0, i))],
        out_specs=[pl.BlockSpec((8, tl), lambda i: (0, i)),
                   pl.BlockSpec((tl, LANES), lambda i: (i, 0))],
        out_shape=[jax.ShapeDtypeStruct((8, t), F32),
                   jax.ShapeDtypeStruct((t, LANES), F32)],
        compiler_params=_cparams("parallel"),
        name="gates",
    )(g)


def _mlstm_kernel(*refs, cps, emit, ncast, nfft=0):
    (qf_ref, kf_ref, vf_ref, gcf_ref, drf_ref,
     qb_ref, kb_ref, vb_ref, gcb_ref, drb_ref, c0_ref, m0_ref) = refs[:12]
    cast_in = refs[12:12 + ncast]
    fft_in = refs[12 + ncast:12 + ncast + nfft]
    outs = refs[12 + ncast + nfft:]
    if emit:
        hf_ref, hb_ref, cfin_ref, mfin_ref = outs[:4]
        outs = outs[4:]
    else:
        cfin_ref, mfin_ref = outs[:2]
        outs = outs[2:]
        hf_ref = hb_ref = None
    cast_out = outs[:ncast]
    fft_out = outs[ncast:ncast + (1 if nfft else 0)]
    c_scr, m_scr = outs[ncast + (1 if nfft else 0):]
    step = pl.program_id(0)

    for src, dst in zip(cast_in, cast_out):
        dst[...] = src[...].astype(BF16)
    if nfft:
        _dft2_kernel(*fft_in, *fft_out)

    @pl.when(step == 0)
    def _():
        c_scr[...] = c0_ref[...]
        m_scr[...] = m0_ref[...]

    ti = lax.broadcasted_iota(jnp.int32, (CHUNK, CHUNK), 0)
    si = lax.broadcasted_iota(jnp.int32, (CHUNK, CHUNK), 1)
    nch = 2 * HEADS
    mask = jnp.concatenate([jnp.broadcast_to((si <= ti)[None], (HEADS, CHUNK, CHUNK)),
                            jnp.broadcast_to((si >= ti)[None], (HEADS, CHUNK, CHUNK))], axis=0)
    ones_blk = jnp.ones((nch, CHUNK, CHUNK), BF16)

    def bdot(a, b):
        return lax.dot_general(a, b, (((2,), (1,)), ((0,), (0,))), preferred_element_type=F32)

    def both(f, g):
        return [f(r) for r in range(HEADS)] + [g(r) for r in range(HEADS, nch)]

    for j in range(cps):
        rf = slice(j * CHUNK, (j + 1) * CHUNK)
        rb = slice((cps - 1 - j) * CHUNK, (cps - j) * CHUNK)
        gcf, gcb = gcf_ref[rf, :], gcb_ref[rb, :]
        drf, drb = drf_ref[:, rf], drb_ref[:, rb]
        lf, lb = CHUNK - 1, 0
        b = jnp.stack(both(lambda r: gcf[:, r:r + 1], lambda r: gcb[:, r:r + 1]))
        mloc = jnp.stack(both(lambda r: gcf[:, 8 + r:9 + r], lambda r: gcb[:, 8 + r:9 + r]))
        drow = jnp.stack(both(lambda r: drf[r:r + 1, :], lambda r: drb[r:r + 1, :]))
        btot = jnp.stack(both(lambda r: gcf[lf:lf + 1, r:r + 1], lambda r: gcb[lb:lb + 1, r:r + 1]))
        amax = jnp.stack(both(lambda r: gcf[lf:lf + 1, 8 + r:9 + r], lambda r: gcb[lb:lb + 1, 8 + r:9 + r]))
        m0 = jnp.stack([m_scr[r][0:1, 0:1] for r in range(nch)])
        c_aug = c_scr[...]
        kt = jnp.concatenate([kf_ref[:, :, rf], kb_ref[:, :, rb]], axis=0)
        vaug = jnp.concatenate([jnp.concatenate([vf_ref[:, rf, :], vb_ref[:, rb, :]], axis=0), ones_blk], axis=2)
        if emit:
            qh = jnp.concatenate([qf_ref[:, rf, :], qb_ref[:, rb, :]], axis=0)
            bm = b + m0
            m_t = jnp.maximum(bm, mloc)
            rel = jnp.broadcast_to(b - m_t, (nch, CHUNK, CHUNK))
            dmat = jnp.where(mask, jnp.exp(rel + drow), 0.0)
            smat = (bdot(qh, kt) * dmat).astype(BF16)
            q_in = (qh.astype(F32) * jnp.exp(rel + m0)).astype(BF16)
            num = bdot(jnp.concatenate([smat, q_in], axis=2),
                       jnp.concatenate([vaug, c_aug.astype(BF16)], axis=1))
            den = jnp.maximum(jnp.abs(num[:, :, DH:]), jnp.exp(-m_t))
            h = (num[:, :, :DH] / den).astype(hf_ref.dtype)
            hf_ref[:, rf, :] = h[:HEADS]
            hb_ref[:, rb, :] = h[HEADS:]
        m_new = jnp.maximum(btot + m0, amax)
        decay = jnp.exp(btot + m0 - m_new)
        kw = (kt.astype(F32) * jnp.exp(btot + drow - m_new)).astype(BF16)
        c_scr[...] = decay * c_aug + bdot(kw, vaug)
        m_scr[...] = jnp.broadcast_to(m_new, (nch, 8, LANES))

    @pl.when(step == pl.num_programs(0) - 1)
    def _():
        cfin_ref[...] = c_scr[...]
        mfin_ref[...] = m_scr[...]


def _expert_slot(e):
    return (e % EPG) * N_GROUPS + e // EPG


def _mlstm(q, kt, v, gc, dr, c0, m0, *, cps, emit, casts=(), fft=()):
    t = q.shape[1]
    cb = cps * CHUNK
    nb = t // cb
    fwd_r = lambda i: (i, 0)
    bwd_r = lambda i: (nb - 1 - i, 0)
    fwd_c = lambda i: (0, i)
    bwd_c = lambda i: (0, nb - 1 - i)
    tok = lambda f: pl.BlockSpec((HEADS, cb, DH), lambda i, f=f: (0, f(i)[0], 0))
    in_specs = []
    for fr, fc in ((fwd_r, fwd_c), (bwd_r, bwd_c)):
        in_specs += [tok(fr), pl.BlockSpec((HEADS, DH, cb), lambda i, fc=fc: (0, 0, fc(i)[1])), tok(fr),
                     pl.BlockSpec((cb, LANES), fr), pl.BlockSpec((8, cb), fc)]
    cshape = (2 * HEADS, DH, 2 * DH)
    mshape = (2 * HEADS, 8, LANES)
    cspec = pl.BlockSpec(cshape, lambda i: (0, 0, 0))
    mspec = pl.BlockSpec(mshape, lambda i: (0, 0, 0))
    in_specs += [cspec, mspec]
    out_specs = [cspec, mspec]
    out_shape = [jax.ShapeDtypeStruct(cshape, F32), jax.ShapeDtypeStruct(mshape, F32)]
    if emit:
        out_specs = [tok(fwd_r), tok(bwd_r)] + out_specs
        out_shape = [jax.ShapeDtypeStruct((HEADS, t, DH), BF16)] * 2 + out_shape
    for a in casts:
        per = nb // a.shape[0]
        blk = (1, a.shape[1] // per, a.shape[2])
        in_specs.append(pl.BlockSpec(blk, lambda i, per=per: (i // per, i % per, 0)))
        out_specs.append(pl.BlockSpec(blk, lambda i, per=per: (_expert_slot(i // per), i % per, 0)))
        out_shape.append(jax.ShapeDtypeStruct(a.shape, BF16))
    if fft:
        yc, ys, a2, mix = fft
        tokf = pl.BlockSpec((FGROUPS, FFT_ROWS * CHUNK, FCG), lambda i: (0, i, 0))
        fullf = lambda a: pl.BlockSpec(a.shape, lambda i: (0,) * a.ndim)
        in_specs += [tokf, tokf, fullf(a2), fullf(mix)]
        out_specs.append(pl.BlockSpec((FGROUPS, CHUNK, FFT_ROWS, FCG), lambda i: (0, 0, i, 0)))
        out_shape.append(jax.ShapeDtypeStruct((FGROUPS, CHUNK, t // CHUNK, FCG), F32))
    return pl.pallas_call(
        functools.partial(_mlstm_kernel, cps=cps, emit=emit, ncast=len(casts), nfft=len(fft)),
        grid=(nb,),
        in_specs=in_specs,
        out_specs=out_specs,
        out_shape=out_shape,
        scratch_shapes=[pltpu.VMEM(cshape, F32), pltpu.VMEM(mshape, F32)],
        compiler_params=_cparams("arbitrary"),
        name="mlstm",
    )(q, kt, v, gc, dr, q, kt, v, gc, dr, c0, m0, *casts, *fft)


FFT_ROWS = 8


def _dft1_kernel(u_ref, f_ref, cw_ref, sw_ref, yc_ref, ys_ref):
    f = f_ref[...]
    rows = CHUNK * FFT_ROWS
    u2 = u_ref.reshape(FGROUPS * rows, FCG)
    yc2 = yc_ref.reshape(FGROUPS * rows, FCG)
    ys2 = ys_ref.reshape(FGROUPS * rows, FCG)
    for s in range(FFT_ROWS):
        pick = [pl.ds(g * rows + s, CHUNK, stride=FFT_ROWS) for g in range(FGROUPS)]
        x = jnp.concatenate([u2[p, :] for p in pick], axis=1).astype(BF16)
        y = _dot(f, x)
        cw = cw_ref[s]
        sw = sw_ref[s]
        pr = y[:CHUNK] * cw - y[CHUNK:] * sw
        pi = y[:CHUNK] * sw + y[CHUNK:] * cw
        for g in range(FGROUPS):
            yc2[pick[g], :] = pr[:, g * FCG:(g + 1) * FCG]
            ys2[pick[g], :] = pi[:, g * FCG:(g + 1) * FCG]


def _dft2_kernel(yc_ref, ys_ref, a2_ref, mix_ref, o_ref):
    a2 = a2_ref[...]
    rows = CHUNK * FFT_ROWS
    o2 = o_ref.reshape(FGROUPS * rows, FCG)
    for kk in range(FFT_ROWS):
        blk = slice(kk * CHUNK, (kk + 1) * CHUNK)
        yc = jnp.concatenate([yc_ref[g, blk, :] for g in range(FGROUPS)], axis=1)
        ys = jnp.concatenate([ys_ref[g, blk, :] for g in range(FGROUPS)], axis=1)
        p = jnp.concatenate([yc, ys], axis=0).astype(BF16)
        x = _dot(a2, p).astype(BF16)
        for g in range(FGROUPS):
            cols = slice(g * FCG, (g + 1) * FCG)
            cat = jnp.concatenate([x[:CHUNK, cols], x[CHUNK:, cols]], axis=1)
            o2[pl.ds(g * rows + kk, CHUNK, stride=FFT_ROWS), :] = _dot(cat, mix_ref[g])


def _merge_kernel(hf_ref, hb_ref, act_ref, z_ref, yf_ref, x_ref, er_ref, ec_ref,
                  nw_ref, sk_ref, wout_ref, gg1_ref, g2_ref, sh2_ref, wr_ref, br_ref,
                  x1_ref, h2_ref, lg_ref):
    parts = []
    for hd in range(HEADS):
        hh = hf_ref[hd].astype(F32) + hb_ref[hd].astype(F32)
        dl = hh - jnp.mean(hh, axis=-1, keepdims=True)
        var = jnp.mean(dl * dl, axis=-1, keepdims=True)
        parts.append(dl * lax.rsqrt(var + EPS))
    hn = jnp.concatenate(parts, axis=-1)
    z = z_ref[...].astype(F32)
    m = (hn * nw_ref[...] + sk_ref[...] * act_ref[...].astype(F32)) * (z * _sigmoid(z))
    cat = jnp.concatenate([m.astype(BF16)] + [yf_ref[g].astype(BF16) for g in range(FGROUPS)], axis=-1)
    y = _dot(cat, wout_ref[...])
    x3 = _add_pos(x_ref[...], er_ref, ec_ref)
    xp = x3.reshape(x3.shape[0] * GRID_W, D_MODEL)
    x1 = xp + _rms(y) * gg1_ref[...]
    x1_ref[...] = x1
    h2 = _rms(x1) * g2_ref[...] + sh2_ref[...]
    h2_ref[...] = h2.astype(BF16)
    lg = _dot(h2.astype(BF16), wr_ref[...]) + br_ref[...]
    lg_ref[...] = lg.T[:32]


def _merge(hf, hb, act, z, yf, x3, er3, ec3, nw, sk, wout, gg1, g2, sh2, wr, br, *, rows):
    nr = x3.shape[0]
    t = nr * GRID_W
    tm = rows * GRID_W
    tok = pl.BlockSpec((tm, D_MLSTM), lambda i: (i, 0))
    heads = pl.BlockSpec((HEADS, tm, DH), lambda i: (0, i, 0))
    full = lambda a: pl.BlockSpec(a.shape, lambda i: (0,) * a.ndim)
    return pl.pallas_call(
        _merge_kernel,
        grid=(nr // rows,),
        in_specs=[heads, heads, tok, tok, heads,
                  pl.BlockSpec((rows, GRID_W, D_MODEL), lambda i: (i, 0, 0)),
                  pl.BlockSpec((rows, 1, D_MODEL // 2), lambda i: (i, 0, 0)),
                  pl.BlockSpec((1, GRID_W, D_MODEL // 2), lambda i: (0, 0, 0)),
                  full(nw), full(sk), full(wout), full(gg1), full(g2), full(sh2), full(wr), full(br)],
        out_specs=[pl.BlockSpec((tm, D_MODEL), lambda i: (i, 0)),
                   pl.BlockSpec((tm, D_MODEL), lambda i: (i, 0)),
                   pl.BlockSpec((32, tm), lambda i: (0, i))],
        out_shape=[jax.ShapeDtypeStruct((t, D_MODEL), F32),
                   jax.ShapeDtypeStruct((t, D_MODEL), BF16),
                   jax.ShapeDtypeStruct((32, t), F32)],
        compiler_params=_cparams("parallel"),
        name="merge",
    )(hf, hb, act, z, yf, x3, er3, ec3, nw, sk, wout, gg1, g2, sh2, wr, br)


def _route_kernel(lg_ref, pos_ref, w_ref, cnt_ref, *, sbk):
    lg = lg_ref[...]
    tl = lg.shape[1]
    g = [lg[j:j + 1] for j in range(N_GROUPS)]
    e = [lg[N_GROUPS + j:N_GROUPS + j + 1] for j in range(N_EXPERTS)]
    gmax = jnp.maximum(jnp.maximum(g[0], g[1]), jnp.maximum(g[2], g[3]))
    den = jnp.exp(g[0] - gmax) + jnp.exp(g[1] - gmax) + jnp.exp(g[2] - gmax) + jnp.exp(g[3] - gmax)
    p_sel = 1.0 / den
    sel = []
    free = jnp.ones((1, tl), F32)
    for j in range(N_GROUPS):
        s = jnp.where(g[j] >= gmax, free, 0.0)
        sel.append(s)
        free = free - s
    es = []
    for j in range(EPG):
        es.append(sel[0] * e[j] + sel[1] * e[EPG + j] + sel[2] * e[2 * EPG + j] + sel[3] * e[3 * EPG + j])
    rank = []
    for j in range(EPG):
        rj = jnp.zeros((1, tl), F32)
        for i in range(EPG):
            if i == j:
                continue
            beats = (es[i] >= es[j]) if i < j else (es[i] > es[j])
            rj = rj + jnp.where(beats, 1.0, 0.0)
        rank.append(rj)
    v1 = jnp.maximum(jnp.maximum(es[0], es[1]), jnp.maximum(es[2], es[3]))
    v2 = sum(jnp.where(rank[j] == 1.0, es[j], 0.0) for j in range(EPG))
    tt = jnp.exp(v2 - v1)
    w1 = p_sel / (1.0 + tt)
    w2 = w1 * tt
    w = [jnp.where(rank[j] == 0.0, w1, jnp.where(rank[j] == 1.0, w2, 0.0)) for j in range(EPG)]
    top2 = [jnp.where(rank[j] < 2.0, 1.0, 0.0) for j in range(EPG)]
    mem = jnp.concatenate([sel[gi] * top2[j] for gi in range(N_GROUPS) for j in range(EPG)], axis=0)
    wts = jnp.concatenate([sel[gi] * w[j] for gi in range(N_GROUPS) for j in range(EPG)], axis=0)
    w_ref[...] = wts
    lane = lax.broadcasted_iota(jnp.int32, (N_EXPERTS, tl), 1) & (sbk - 1)
    c = mem
    k = 1
    while k < sbk:
        c = c + jnp.where(lane >= k, pltpu.roll(c, k, 1), 0.0)
        k *= 2
    pos_ref[...] = jnp.where(mem > 0.0, c - 1.0, -1.0)
    lane128 = lax.broadcasted_iota(jnp.int32, (N_EXPERTS, LANES), 1)
    cnt = jnp.zeros((N_EXPERTS, LANES), F32)
    for kb in range(tl // sbk):
        tot = jnp.sum(mem[:, kb * sbk:(kb + 1) * sbk], axis=1, keepdims=True)
        cnt = cnt + jnp.where(lane128 == kb, tot, 0.0)
    cnt_ref[...] = cnt


def _route(lg, *, tl, sbk):
    t = lg.shape[1]
    row = pl.BlockSpec((N_EXPERTS, tl), lambda i: (0, i))
    return pl.pallas_call(
        functools.partial(_route_kernel, sbk=sbk),
        grid=(t // tl,),
        in_specs=[pl.BlockSpec((32, tl), lambda i: (0, i))],
        out_specs=[row, row, pl.BlockSpec((N_EXPERTS, LANES), lambda i: (i, 0))],
        out_shape=[jax.ShapeDtypeStruct((N_EXPERTS, t), F32),
                   jax.ShapeDtypeStruct((N_EXPERTS, t), F32),
                   jax.ShapeDtypeStruct((t // tl * N_EXPERTS, LANES), F32)],
        compiler_params=_cparams("parallel"),
        name="route",
    )(lg)


def _slots_kernel(pos_ref, w_ref, first_ref, qrow_ref, qcol_ref):
    tl = pos_ref.shape[1]
    q0 = jnp.full((1, tl), -1.0, F32)
    q1 = jnp.full((1, tl), -1.0, F32)
    w0 = jnp.zeros((1, tl), F32)
    w1 = jnp.zeros((1, tl), F32)
    seen = jnp.zeros((1, tl), F32)
    for ex in range(N_EXPERTS):
        rk = pos_ref[ex:ex + 1, :]
        wt = w_ref[ex:ex + 1, :]
        m = jnp.where(rk >= 0.0, 1.0, 0.0)
        val = rk + first_ref[ex:ex + 1, :]
        first = (m * (1.0 - seen)) > 0.0
        second = (m * seen) > 0.0
        q0 = jnp.where(first, val, q0)
        w0 = jnp.where(first, wt, w0)
        q1 = jnp.where(second, val, q1)
        w1 = jnp.where(second, wt, w1)
        seen = seen + m
    qrow_ref[...] = jnp.concatenate([q0, q1, jnp.zeros((6, tl), F32)], axis=0)
    qcol_ref[...] = jnp.concatenate([q0, q1, w0, w1, jnp.zeros((LANES - 4, tl), F32)], axis=0).T


def _slots(pos, w, first, *, tl):
    t = pos.shape[1]
    row = pl.BlockSpec((N_EXPERTS, tl), lambda i: (0, i))
    return pl.pallas_call(
        _slots_kernel,
        grid=(t // tl,),
        in_specs=[row, row, row],
        out_specs=[pl.BlockSpec((8, tl), lambda i: (0, i)),
                   pl.BlockSpec((tl, LANES), lambda i: (i, 0))],
        out_shape=[jax.ShapeDtypeStruct((8, t), F32),
                   jax.ShapeDtypeStruct((t, LANES), F32)],
        compiler_params=_cparams("parallel"),
        name="slots",
    )(pos, w, first)


def _mlp(x, wg, wu, wd):
    gt = _dot(x, wg)
    a = ((gt * _sigmoid(gt)) * _dot(x, wu)).astype(BF16)
    return _dot(a, wd).astype(BF16)


def _moe_kernel(tab_ref, loc_ref, h_ref, qrow_ref, wg_ref, wu_ref, wd_ref,
                qcol_ref, x1_ref, gg2_ref, o_ref, xs_ref, *, sbk, slots, csb):
    i = pl.program_id(0)
    step = pl.program_id(1)
    nsb = h_ref.shape[0] // sbk
    eps = wg_ref.shape[0]
    exp_steps = N_EXPERTS // eps

    @pl.when(step == 0)
    def _():
        pid = lax.broadcasted_iota(jnp.int32, (slots, sbk), 0).astype(F32)

        def select(sb, carry):
            q0 = qrow_ref[0, pl.ds(sb, 1), :]
            q1 = qrow_ref[1, pl.ds(sb, 1), :]
            s = jnp.where(q0 == pid, 1.0, jnp.where(q1 == pid, 1.0, 0.0)).astype(BF16)
            row0 = pl.multiple_of(sb * sbk, sbk)
            xs_ref[sb, 0:slots, :] = _dot(s, h_ref[pl.ds(row0, sbk), :]).astype(BF16)
            xs_ref[sb, slots:slots + 32, :] = jnp.zeros((32, D_MODEL), BF16)
            return carry

        lax.fori_loop(0, nsb, select, 0)

    def run_pieces(k, base, n_pieces):
        locs = []
        for m in range(n_pieces):
            d = loc_ref[0, 0, base + m]
            locs.append((d // MOE_LOC, pl.multiple_of(d % MOE_LOC, 16)))
        x = jnp.concatenate([xs_ref[sb, pl.ds(off, 16), :] for sb, off in locs], axis=0)
        y = _mlp(x, wg_ref[k], wu_ref[k], wd_ref[k])
        for m, (sb, off) in enumerate(locs):
            xs_ref[sb, pl.ds(off, 16), :] = y[m * 16:(m + 1) * 16]

    @pl.when(jnp.logical_and(step >= 1, step <= exp_steps))
    def _():
        def expert(k, carry):
            e = _expert_slot((step - 1) * eps + k)
            total = tab_ref[i * N_EXPERTS + e]
            base = e * MOE_PIECES

            def full(t, c1):
                run_pieces(k, base + t * 32, 32)
                return c1

            n_full = (total + 7) // 32
            lax.fori_loop(0, n_full, full, 0)
            done = n_full * 32
            rest = total - done

            for lo, n in ((0, 8), (8, 12), (12, 16), (16, 24)):
                @pl.when(jnp.logical_and(rest > lo, rest <= n))
                def _():
                    run_pieces(k, base + done, n)

            return carry

        lax.fori_loop(0, eps, expert, 0)

    @pl.when(step > exp_steps)
    def _():
        lane = lax.broadcasted_iota(jnp.int32, (sbk, slots), 1).astype(F32)
        for k in range(csb):
            sb = (step - 1 - exp_steps) * csb + k
            rows = slice(k * sbk, (k + 1) * sbk)
            qc = qcol_ref[rows, :]
            wmat = (jnp.where(lane == qc[:, 0:1], qc[:, 2:3], 0.0)
                    + jnp.where(lane == qc[:, 1:2], qc[:, 3:4], 0.0)).astype(BF16)
            y = _dot(wmat, xs_ref[sb, 0:slots, :])
            o_ref[rows, :] = x1_ref[rows, :] + _rms(y) * gg2_ref[...]


def _moe(tab, loc, h2, qrow3, wg, wu, wd, qcol, x1, gg2, *, tb, sbk, slots, eps, csb):
    t = h2.shape[0]
    nsb = tb // sbk
    comb_steps = nsb // csb
    exp_steps = N_EXPERTS // eps
    wblk = lambda i, s, c: (jnp.where(s == 0, exp_steps - 1, jnp.minimum(s - 1, exp_steps - 1)), 0, 0)
    oblk = lambda i, s, c: (i * comb_steps + jnp.maximum(s - 1 - exp_steps, 0), 0)
    return pl.pallas_call(
        functools.partial(_moe_kernel, sbk=sbk, slots=slots, csb=csb),
        grid_spec=pltpu.PrefetchScalarGridSpec(
            num_scalar_prefetch=1,
            grid=(t // tb, 1 + exp_steps + comb_steps),
            in_specs=[pl.BlockSpec((1, 1, N_EXPERTS * MOE_PIECES), lambda i, s, c: (i, 0, 0), memory_space=pltpu.SMEM),
                      pl.BlockSpec((tb, D_MODEL), lambda i, s, c: (i, 0)),
                      pl.BlockSpec((8, nsb, sbk), lambda i, s, c: (0, i, 0)),
                      pl.BlockSpec((eps, D_MODEL, D_EXPERT), wblk),
                      pl.BlockSpec((eps, D_MODEL, D_EXPERT), wblk),
                      pl.BlockSpec((eps, D_EXPERT, D_MODEL), wblk),
                      pl.BlockSpec((csb * sbk, LANES), oblk),
                      pl.BlockSpec((csb * sbk, D_MODEL), oblk),
                      pl.BlockSpec((1, D_MODEL), lambda i, s, c: (0, 0))],
            out_specs=pl.BlockSpec((csb * sbk, D_MODEL), oblk),
            scratch_shapes=[pltpu.VMEM((nsb, slots + 32, D_MODEL), BF16)]),
        out_shape=jax.ShapeDtypeStruct((t, D_MODEL), F32),
        compiler_params=pltpu.CompilerParams(dimension_semantics=("parallel", "arbitrary"),
                                             vmem_limit_bytes=MOE_VMEM_LIMIT),
        name="moe",
    )(tab, loc, h2, qrow3, wg, wu, wd, qcol, x1, gg2)


def _slot_tables(cnt, *, nsb, slots):
    tiles = (cnt + 15) // 16
    first = (jnp.cumsum(tiles, axis=1) - tiles) * 16
    tl = tiles.reshape(-1, nsb, N_EXPERTS)
    end = jnp.cumsum(tl, axis=1)
    beg = end - tl
    q = jnp.arange(MOE_PIECES, dtype=jnp.int32)[None, None, None, :]
    inside = jnp.logical_and(q >= beg[..., None], q < end[..., None])
    sb = jnp.arange(nsb, dtype=jnp.int32)[None, :, None, None]
    code = sb * MOE_LOC + first.reshape(-1, nsb, N_EXPERTS)[..., None] + (q - beg[..., None]) * 16
    loc = jnp.sum(jnp.where(inside, code, 0), axis=1)
    total = end[:, -1, :]
    dump = slots + 16 * (q[0, 0] % 2)
    loc = jnp.where(q[0] < total[..., None], loc, dump)
    return first, total.reshape(-1), loc.reshape(loc.shape[0], 1, -1)


def _pos_tables(rows):
    quarter = D_MODEL // 4
    freq = 1.0 / (POS_BASE ** (np.arange(quarter, dtype=np.float64) / quarter))
    r = np.arange(rows, dtype=np.float64)[:, None] * freq
    cl = np.arange(GRID_W, dtype=np.float64)[:, None] * freq
    er = np.concatenate([np.sin(r), np.cos(r)], axis=-1).astype(np.float32)
    ec = np.concatenate([np.sin(cl), np.cos(cl)], axis=-1).astype(np.float32)
    return jnp.asarray(er[:, None, :]), jnp.asarray(ec[None, :, :])


def _dft_tables(t):
    n = np.arange(CHUNK, dtype=np.int64)
    prod = n[:, None] * n[None, :]
    ang = (prod % CHUNK).astype(np.float64) * (2.0 * np.pi / CHUNK)
    c, s = np.cos(ang), np.sin(ang)
    f1 = np.concatenate([c, s], axis=0)
    a2 = np.concatenate([np.concatenate([c, -s], axis=1), np.concatenate([s, c], axis=1)], axis=0)
    cs = np.concatenate([c, -s], axis=0)
    angw = prod.astype(np.float64) * (2.0 * np.pi / t)
    f32 = lambda a: jnp.asarray(a.astype(np.float32))
    return f32(f1), f32(a2), f32(cs), f32(np.cos(angw)[:, :, None]), f32(np.sin(angw)[:, :, None])


def _blockdiag(w):
    n = w.shape[0]
    size = n * QKV_BLOCK
    spread = np.tile(np.eye(QKV_BLOCK, dtype=np.float32), (1, n))
    rows = jnp.dot(w.reshape(size, QKV_BLOCK), jnp.asarray(spread), precision=lax.Precision.HIGHEST)
    blk = np.arange(size) // QKV_BLOCK
    mask = (blk[:, None] == blk[None, :]).astype(np.float32)
    return rows * jnp.asarray(mask)


def _gate_weights(w_f, b_f, w_b, b_b):
    w = jnp.concatenate([w_f[:, :HEADS], w_b[:, :HEADS], w_f[:, HEADS:], w_b[:, HEADS:]], axis=1).T
    b = jnp.concatenate([b_f[:HEADS], b_b[:HEADS], b_f[HEADS:], b_b[HEADS:]])
    return w.astype(BF16), b[:, None]


def kernel(x, c, ctx, c_ctx, w_ada, b_ada, g_pre_mix, g_post_mix, g_pre_ffn, g_post_ffn,
           w_in, conv_w, conv_b, w_q, w_k, w_v, w_if_fwd, b_if_fwd, w_if_bwd, b_if_bwd,
           mlstm_norm_w, mlstm_skip, w_fourier, w_out, w_router_group, b_router_group,
           w_router_expert, b_router_expert, w_gate, w_up, w_down):
    t = x.shape[1]
    rows = t // GRID_W

    c8 = jnp.concatenate([c, c_ctx[None, :], jnp.zeros((6, D_MODEL), F32)], axis=0)
    mod = _ada(c8, w_ada[0], b_ada[0][None, :])
    shift1, scale1, gate1, shift2, scale2, gate2 = [mod[0:1, k * D_MODEL:(k + 1) * D_MODEL] for k in range(N_MOD)]
    shift1c, scale1c = mod[1:2, 0:D_MODEL], mod[1:2, D_MODEL:2 * D_MODEL]
    g1 = g_pre_mix[0][None, :] * (1.0 + scale1)
    g1c = g_pre_mix[0][None, :] * (1.0 + scale1c)
    gg1 = g_post_mix[0][None, :] * gate1
    g2 = g_pre_ffn[0][None, :] * (1.0 + scale2)
    gg2 = g_post_ffn[0][None, :] * gate2

    er3, ec3 = _pos_tables(rows)
    x3 = x.reshape(rows, GRID_W, D_MODEL)
    ctx3 = ctx.reshape(CTX_LEN // GRID_W, GRID_W, D_MODEL)
    w_in_bf = w_in[0].astype(BF16)

    xm_l, z_l, u_l = _inproj(x3, er3, ec3, g1, shift1, w_in_bf, rows=8, add_pos=True)
    xm_c, _, _ = _inproj(ctx3, er3, ec3, g1c, shift1c, w_in_bf, rows=CTX_LEN // GRID_W, add_pos=False)

    wq = _blockdiag(w_q[0]).astype(BF16)
    wkt = _blockdiag(w_k[0]).T.astype(BF16)
    wv = _blockdiag(w_v[0]).astype(BF16)
    wi, bi = _gate_weights(w_if_fwd[0], b_if_fwd[0], w_if_bwd[0], b_if_bwd[0])
    wiq, wik, wiv = wi[:, :D_MLSTM], wi[:, D_MLSTM:2 * D_MLSTM], wi[:, 2 * D_MLSTM:]
    cb = conv_b[0][None, :]
    f1, a2, cs, cw3, sw3 = _dft_tables(t)
    n1 = t // CHUNK
    q_l, kt_l, v_l, act_l, gp_l, yc, ys = _feat(
        xm_l, conv_w[0], cb, wq, wkt, wv, wiq, wik, wiv, bi, tm=t // (n1 // FFT_ROWS),
        fft=(u_l.reshape(FGROUPS, n1, CHUNK, FCG), f1.astype(BF16), cw3, sw3))
    q_c, kt_c, v_c, _, gp_c = _feat(xm_c, conv_w[0], cb, wq, wkt, wv, wiq, wik, wiv, bi, tm=CTX_LEN)

    dr_l, gc_l = _gates(gp_l, tl=2048)
    dr_c, gc_c = _gates(gp_c, tl=CTX_LEN)

    mix = jnp.einsum('kc,gcd->gkd', cs, w_fourier[0], precision=lax.Precision.HIGHEST)
    mix = (mix * float(1.0 / np.sqrt(float(t) * FCG))).astype(BF16)

    c0 = jnp.zeros((2 * HEADS, DH, 2 * DH), F32)
    m0 = jnp.zeros((2 * HEADS, 8, LANES), F32)
    kt_c, kt_l = kt_c.reshape(HEADS, DH, CTX_LEN), kt_l.reshape(HEADS, DH, t)
    c_ctx_fin, m_ctx_fin = _mlstm(q_c, kt_c, v_c, gc_c, dr_c, c0, m0, cps=CTX_LEN // CHUNK, emit=False)
    hf, hb, _, _, wg_bf, wu_bf, wd_bf, yf = _mlstm(
        q_l, kt_l, v_l, gc_l, dr_l, c_ctx_fin, m_ctx_fin, cps=8, emit=True,
        casts=(w_gate[0], w_up[0], w_down[0]),
        fft=(yc.reshape(FGROUPS, t, FCG), ys.reshape(FGROUPS, t, FCG), a2.astype(BF16), mix))
    yf = yf.reshape(FGROUPS, t, FCG)

    wr = jnp.concatenate([w_router_group[0], w_router_expert[0],
                          jnp.zeros((D_MODEL, LANES - N_GROUPS - N_EXPERTS), F32)], axis=1)
    br = jnp.concatenate([b_router_group[0], b_router_expert[0],
                          jnp.zeros((LANES - N_GROUPS - N_EXPERTS,), F32)])[None, :]
    x1, h2, lg = _merge(hf, hb, act_l, z_l, yf, x3, er3, ec3,
                        mlstm_norm_w[0][None, :], mlstm_skip[0][None, :], w_out[0].astype(BF16),
                        gg1, g2, shift2, wr.astype(BF16), br, rows=16)
    pos, wts, cnt = _route(lg, tl=MOE_TB, sbk=MOE_SBK)
    nblk, nsb = t // MOE_TB, MOE_TB // MOE_SBK
    cnt = cnt.reshape(nblk, N_EXPERTS, LANES)[:, :, :nsb]
    cnt = jnp.transpose(cnt, (0, 2, 1)).reshape(nblk * nsb, N_EXPERTS).astype(jnp.int32)
    first, total, loc = _slot_tables(cnt, nsb=nsb, slots=MOE_SLOTS)
    first_rows = jnp.repeat(first.T.astype(F32), MOE_SBK, axis=1)
    qrow, qcol = _slots(pos, wts, first_rows, tl=MOE_TB)
    out = _moe(total, loc, h2, qrow.reshape(8, t // MOE_SBK, MOE_SBK), wg_bf, wu_bf, wd_bf,
               qcol, x1, gg2, tb=MOE_TB, sbk=MOE_SBK, slots=MOE_SLOTS, eps=MOE_EPS, csb=MOE_CSB)
    return out[None]
```

```python
import functools

import numpy as np
import jax
import jax.numpy as jnp
from jax import lax
from jax.experimental import pallas as pl
from jax.experimental.pallas import tpu as pltpu

F32 = jnp.float32
BF16 = jnp.bfloat16

D_MODEL = 1024
SEQ = 16384
GRID_W = 64
CTX_LEN = 256
D_MLSTM = 512
HEADS = 4
DH = 128
QKV_BLOCK = 4
CONV_K = 3
CHUNK = 128
D_FOURIER = 512
FGROUPS = 4
FCG = 128
N_GROUPS = 4
EPG = 4
N_EXPERTS = 16
D_EXPERT = 512
N_MOD = 6
EPS = 1e-6
POS_BASE = 10000.0
LANES = 128
NEG_BIG = -3.0e38

VMEM_LIMIT = 58 * 1024 * 1024
MOE_VMEM_LIMIT = 58 * 1024 * 1024
MOE_TB = 2048
MOE_SBK = 256
MOE_SLOTS = 768
MOE_EPS = 4
MOE_CSB = 2
MOE_LOC = 1024
MOE_PIECES = MOE_TB // 16


def _cparams(*sem):
    return pltpu.CompilerParams(dimension_semantics=sem, vmem_limit_bytes=VMEM_LIMIT)


def _dot(a, b):
    return jnp.dot(a, b, preferred_element_type=F32)


def _dot_nt(a, b):
    return lax.dot_general(a, b, (((1,), (1,)), ((), ())), preferred_element_type=F32)


def _split_bf16(a):
    hi = a.astype(BF16)
    lo = (a - hi.astype(F32)).astype(BF16)
    return hi, lo


def _dot3(a, b):
    a_hi, a_lo = _split_bf16(a)
    b_hi, b_lo = _split_bf16(b)
    return _dot(a_hi, b_hi) + (_dot(a_hi, b_lo) + _dot(a_lo, b_hi))


def _sigmoid(x):
    return 1.0 / (1.0 + jnp.exp(-x))


def _rms(x):
    return x * lax.rsqrt(jnp.mean(x * x, axis=-1, keepdims=True) + EPS)


def _ada_kernel(c_ref, w_ref, b_ref, o_ref):
    c = c_ref[...]
    s = c * _sigmoid(c)
    o_ref[...] = _dot3(s, w_ref[...]) + b_ref[...]


def _ada(c8, w, b):
    n = w.shape[1]
    tn = 768
    return pl.pallas_call(
        _ada_kernel,
        grid=(n // tn,),
        in_specs=[pl.BlockSpec((8, D_MODEL), lambda j: (0, 0)),
                  pl.BlockSpec((D_MODEL, tn), lambda j: (0, j)),
                  pl.BlockSpec((1, tn), lambda j: (0, j))],
        out_specs=pl.BlockSpec((8, tn), lambda j: (0, j)),
        out_shape=jax.ShapeDtypeStruct((8, n), F32),
        compiler_params=_cparams("parallel"),
        name="ada",
    )(c8, w, b)


def _add_pos(x3, er_ref, ec_ref):
    r = x3.shape[0]
    pr = jnp.broadcast_to(er_ref[...], (r, GRID_W, D_MODEL // 2))
    pc = jnp.broadcast_to(ec_ref[...], (r, GRID_W, D_MODEL // 2))
    return x3 + jnp.concatenate([pr, pc], axis=-1)


def _inproj_kernel(x_ref, er_ref, ec_ref, g_ref, sh_ref, w_ref, xm_ref, z_ref, u_ref, *, add_pos):
    x3 = x_ref[...]
    if add_pos:
        x3 = _add_pos(x3, er_ref, ec_ref)
    x = x3.reshape(x3.shape[0] * GRID_W, D_MODEL)
    h = _rms(x) * g_ref[...] + sh_ref[...]
    proj = _dot(h.astype(BF16), w_ref[...])
    xm_ref[...] = proj[:, :D_MLSTM].astype(BF16)
    z_ref[...] = proj[:, D_MLSTM:2 * D_MLSTM].astype(BF16)
    for g in range(FGROUPS):
        u_ref[g] = proj[:, 2 * D_MLSTM + g * FCG:2 * D_MLSTM + (g + 1) * FCG]


def _inproj(x3, er3, ec3, g_eff, shift, w_in, *, rows, add_pos):
    nr = x3.shape[0]
    t = nr * GRID_W
    tm = rows * GRID_W
    out = jax.ShapeDtypeStruct((t, D_MLSTM), BF16)
    ospec = pl.BlockSpec((tm, D_MLSTM), lambda i: (i, 0))
    vec = pl.BlockSpec((1, D_MODEL), lambda i: (0, 0))
    return pl.pallas_call(
        functools.partial(_inproj_kernel, add_pos=add_pos),
        grid=(nr // rows,),
        in_specs=[pl.BlockSpec((rows, GRID_W, D_MODEL), lambda i: (i, 0, 0)),
                  pl.BlockSpec((rows, 1, D_MODEL // 2), lambda i: (i, 0, 0)),
                  pl.BlockSpec((1, GRID_W, D_MODEL // 2), lambda i: (0, 0, 0)),
                  vec, vec,
                  pl.BlockSpec(w_in.shape, lambda i: (0, 0))],
        out_specs=[ospec, ospec, pl.BlockSpec((FGROUPS, tm, FCG), lambda i: (0, i, 0))],
        out_shape=[out, out, jax.ShapeDtypeStruct((FGROUPS, t, FCG), F32)],
        compiler_params=_cparams("parallel"),
        name="inproj",
    )(x3, er3, ec3, g_eff, shift, w_in)


def _feat_kernel(xm_ref, prev_ref, next_ref, cw_ref, cb_ref, wq_ref, wkt_ref, wv_ref,
                 wiq_ref, wik_ref, wiv_ref, bi_ref, *rest):
    if len(rest) > 5:
        _dft1_kernel(*rest[:4], *rest[9:])
        rest = rest[4:9]
    q_ref, kt_ref, v_ref, act_ref, g_ref = rest
    i = pl.program_id(0)
    n = pl.num_programs(0)
    xm_bf = xm_ref[...]
    xm = xm_bf.astype(F32)
    tm = xm.shape[0]
    prev_row = prev_ref[...].astype(F32)[15:16, :] * jnp.where(i > 0, 1.0, 0.0)
    next_row = next_ref[...].astype(F32)[0:1, :] * jnp.where(i < n - 1, 1.0, 0.0)
    rid = lax.broadcasted_iota(jnp.int32, (tm, 1), 0)
    x_left = jnp.where(rid == 0, prev_row, pltpu.roll(xm, 1, 0))
    x_right = jnp.where(rid == tm - 1, next_row, pltpu.roll(xm, tm - 1, 0))
    cw = cw_ref[...]
    y = cw[0:1] * x_left + cw[1:2] * xm + cw[2:3] * x_right + cb_ref[...]
    act = (y * _sigmoid(y)).astype(BF16)
    act_ref[...] = act
    q = _dot(act, wq_ref[...])
    kt = _dot_nt(wkt_ref[...], act)
    v = _dot(xm_bf, wv_ref[...])
    q_bf = q.astype(BF16)
    kt_bf = kt.astype(BF16)
    v_bf = v.astype(BF16)
    q_s = (q * (DH ** -0.5)).astype(BF16)
    for hd in range(HEADS):
        q_ref[hd] = q_s[:, hd * DH:(hd + 1) * DH]
        v_ref[hd] = v_bf[:, hd * DH:(hd + 1) * DH]
    kt_ref[...] = kt_bf
    g = _dot_nt(wiq_ref[...], q_bf) + _dot(wik_ref[...], kt_bf) + _dot_nt(wiv_ref[...], v_bf)
    g_ref[...] = g + bi_ref[...]


def _feat(xm, conv_w, conv_b, wq, wkt, wv, wiq, wik, wiv, bi, *, tm, fft=()):
    t = xm.shape[0]
    nb16 = t // 16
    k16 = tm // 16
    full = lambda a: pl.BlockSpec(a.shape, lambda i: (0,) * a.ndim)
    tok = pl.BlockSpec((tm, D_MLSTM), lambda i: (i, 0))
    heads = pl.BlockSpec((HEADS, tm, DH), lambda i: (0, i, 0))
    in_specs = [tok,
                pl.BlockSpec((16, D_MLSTM), lambda i: (jnp.maximum(i * k16 - 1, 0), 0)),
                pl.BlockSpec((16, D_MLSTM), lambda i: (jnp.minimum((i + 1) * k16, nb16 - 1), 0)),
                full(conv_w), full(conv_b), full(wq), full(wkt), full(wv),
                full(wiq), full(wik), full(wiv), full(bi)]
    out_specs = [heads,
                 pl.BlockSpec((D_MLSTM, tm), lambda i: (0, i)),
                 heads, tok,
                 pl.BlockSpec((16, tm), lambda i: (0, i))]
    out_shape = [jax.ShapeDtypeStruct((HEADS, t, DH), BF16),
                 jax.ShapeDtypeStruct((D_MLSTM, t), BF16),
                 jax.ShapeDtypeStruct((HEADS, t, DH), BF16),
                 jax.ShapeDtypeStruct((t, D_MLSTM), BF16),
                 jax.ShapeDtypeStruct((16, t), F32)]
    if fft:
        u4, f1, cw3, sw3 = fft
        blk = pl.BlockSpec((FGROUPS, CHUNK, FFT_ROWS, FCG), lambda j: (0, 0, j, 0))
        tw = pl.BlockSpec((FFT_ROWS, CHUNK, 1), lambda j: (j, 0, 0))
        in_specs += [blk, full(f1), tw, tw]
        out_specs += [blk, blk]
        out_shape += [jax.ShapeDtypeStruct(u4.shape, F32)] * 2
    return pl.pallas_call(
        _feat_kernel,
        grid=(t // tm,),
        in_specs=in_specs,
        out_specs=out_specs,
        out_shape=out_shape,
        compiler_params=_cparams("parallel"),
        name="feat",
    )(xm, xm, xm, conv_w, conv_b, wq, wkt, wv, wiq, wik, wiv, bi, *fft)


def _gates_kernel(g_ref, d_ref, gc_ref):
    g = g_ref[...]
    tl = g.shape[1]
    ig = g[0:8]
    fg = g[8:16]
    lf = jnp.minimum(fg, 0.0) - jnp.log(1.0 + jnp.exp(-jnp.abs(fg)))
    pos = lax.broadcasted_iota(jnp.int32, (8, tl), 1) & (CHUNK - 1)
    is_fwd = lax.broadcasted_iota(jnp.int32, (8, tl), 0) < HEADS

    def scan(x, op, ident):
        xf = x
        xb = x
        k = 1
        while k < CHUNK:
            xf = op(xf, jnp.where(pos >= k, pltpu.roll(xf, k, 1), ident))
            xb = op(xb, jnp.where(pos < CHUNK - k, pltpu.roll(xb, tl - k, 1), ident))
            k *= 2
        return jnp.where(is_fwd, xf, xb)

    b = scan(lf, jnp.add, 0.0)
    d = ig - b
    mloc = b + scan(d, jnp.maximum, NEG_BIG)
    d_ref[...] = d
    stack = jnp.concatenate([b, mloc, jnp.zeros((LANES - 16, tl), F32)], axis=0)
    gc_ref[...] = stack.T


def _gates(g, *, tl):
    t = g.shape[1]
    return pl.pallas_call(
        _gates_kernel,
        grid=(t // tl,),
        in_specs=[pl.BlockSpec((16, tl), lambda i: (0, i))],
        out_specs=[pl.BlockSpec((8, tl), lambda i: (0, i)),
                   pl.BlockSpec((tl, LANES), lambda i: (i, 0))],
        out_shape=[jax.ShapeDtypeStruct((8, t), F32),
                   jax.ShapeDtypeStruct((t, LANES), F32)],
        compiler_params=_cparams("parallel"),
        name="gates",
    )(g)


def _mlstm_kernel(*refs, cps, emit, ncast, nfft=0):
    (qf_ref, kf_ref, vf_ref, gcf_ref, drf_ref,
     qb_ref, kb_ref, vb_ref, gcb_ref, drb_ref, c0_ref, m0_ref) = refs[:12]
    cast_in = refs[12:12 + ncast]
    fft_in = refs[12 + ncast:12 + ncast + nfft]
    outs = refs[12 + ncast + nfft:]
    if emit:
        hf_ref, hb_ref, cfin_ref, mfin_ref = outs[:4]
        outs = outs[4:]
    else:
        cfin_ref, mfin_ref = outs[:2]
        outs = outs[2:]
        hf_ref = hb_ref = None
    cast_out = outs[:ncast]
    fft_out = outs[ncast:ncast + (1 if nfft else 0)]
    c_scr, m_scr = outs[ncast + (1 if nfft else 0):]
    step = pl.program_id(0)

    for src, dst in zip(cast_in, cast_out):
        dst[...] = src[...].astype(BF16)
    if nfft:
        _dft2_kernel(*fft_in, *fft_out)

    @pl.when(step == 0)
    def _():
        c_scr[...] = c0_ref[...]
        m_scr[...] = m0_ref[...]

    ti = lax.broadcasted_iota(jnp.int32, (CHUNK, CHUNK), 0)
    si = lax.broadcasted_iota(jnp.int32, (CHUNK, CHUNK), 1)
    nch = 2 * HEADS
    mask = jnp.concatenate([jnp.broadcast_to((si <= ti)[None], (HEADS, CHUNK, CHUNK)),
                            jnp.broadcast_to((si >= ti)[None], (HEADS, CHUNK, CHUNK))], axis=0)
    ones_blk = jnp.ones((nch, CHUNK, CHUNK), BF16)

    def bdot(a, b):
        return lax.dot_general(a, b, (((2,), (1,)), ((0,), (0,))), preferred_element_type=F32)

    def both(f, g):
        return [f(r) for r in range(HEADS)] + [g(r) for r in range(HEADS, nch)]

    for j in range(cps):
        rf = slice(j * CHUNK, (j + 1) * CHUNK)
        rb = slice((cps - 1 - j) * CHUNK, (cps - j) * CHUNK)
        gcf, gcb = gcf_ref[rf, :], gcb_ref[rb, :]
        drf, drb = drf_ref[:, rf], drb_ref[:, rb]
        lf, lb = CHUNK - 1, 0
        b = jnp.stack(both(lambda r: gcf[:, r:r + 1], lambda r: gcb[:, r:r + 1]))
        mloc = jnp.stack(both(lambda r: gcf[:, 8 + r:9 + r], lambda r: gcb[:, 8 + r:9 + r]))
        drow = jnp.stack(both(lambda r: drf[r:r + 1, :], lambda r: drb[r:r + 1, :]))
        btot = jnp.stack(both(lambda r: gcf[lf:lf + 1, r:r + 1], lambda r: gcb[lb:lb + 1, r:r + 1]))
        amax = jnp.stack(both(lambda r: gcf[lf:lf + 1, 8 + r:9 + r], lambda r: gcb[lb:lb + 1, 8 + r:9 + r]))
        m0 = jnp.stack([m_scr[r][0:1, 0:1] for r in range(nch)])
        c_aug = c_scr[...]
        kt = jnp.concatenate([kf_ref[:, :, rf], kb_ref[:, :, rb]], axis=0)
        vaug = jnp.concatenate([jnp.concatenate([vf_ref[:, rf, :], vb_ref[:, rb, :]], axis=0), ones_blk], axis=2)
        if emit:
            qh = jnp.concatenate([qf_ref[:, rf, :], qb_ref[:, rb, :]], axis=0)
            bm = b + m0
            m_t = jnp.maximum(bm, mloc)
            rel = jnp.broadcast_to(b - m_t, (nch, CHUNK, CHUNK))
            dmat = jnp.where(mask, jnp.exp(rel + drow), 0.0)
            smat = (bdot(qh, kt) * dmat).astype(BF16)
            q_in = (qh.astype(F32) * jnp.exp(rel + m0)).astype(BF16)
            num = bdot(jnp.concatenate([smat, q_in], axis=2),
                       jnp.concatenate([vaug, c_aug.astype(BF16)], axis=1))
            den = jnp.maximum(jnp.abs(num[:, :, DH:]), jnp.exp(-m_t))
            h = (num[:, :, :DH] / den).astype(hf_ref.dtype)
            hf_ref[:, rf, :] = h[:HEADS]
            hb_ref[:, rb, :] = h[HEADS:]
        m_new = jnp.maximum(btot + m0, amax)
        decay = jnp.exp(btot + m0 - m_new)
        kw = (kt.astype(F32) * jnp.exp(btot + drow - m_new)).astype(BF16)
        c_scr[...] = decay * c_aug + bdot(kw, vaug)
        m_scr[...] = jnp.broadcast_to(m_new, (nch, 8, LANES))

    @pl.when(step == pl.num_programs(0) - 1)
    def _():
        cfin_ref[...] = c_scr[...]
        mfin_ref[...] = m_scr[...]


def _expert_slot(e):
    return (e % EPG) * N_GROUPS + e // EPG


def _mlstm(q, kt, v, gc, dr, c0, m0, *, cps, emit, casts=(), fft=()):
    t = q.shape[1]
    cb = cps * CHUNK
    nb = t // cb
    fwd_r = lambda i: (i, 0)
    bwd_r = lambda i: (nb - 1 - i, 0)
    fwd_c = lambda i: (0, i)
    bwd_c = lambda i: (0, nb - 1 - i)
    tok = lambda f: pl.BlockSpec((HEADS, cb, DH), lambda i, f=f: (0, f(i)[0], 0))
    in_specs = []
    for fr, fc in ((fwd_r, fwd_c), (bwd_r, bwd_c)):
        in_specs += [tok(fr), pl.BlockSpec((HEADS, DH, cb), lambda i, fc=fc: (0, 0, fc(i)[1])), tok(fr),
                     pl.BlockSpec((cb, LANES), fr), pl.BlockSpec((8, cb), fc)]
    cshape = (2 * HEADS, DH, 2 * DH)
    mshape = (2 * HEADS, 8, LANES)
    cspec = pl.BlockSpec(cshape, lambda i: (0, 0, 0))
    mspec = pl.BlockSpec(mshape, lambda i: (0, 0, 0))
    in_specs += [cspec, mspec]
    out_specs = [cspec, mspec]
    out_shape = [jax.ShapeDtypeStruct(cshape, F32), jax.ShapeDtypeStruct(mshape, F32)]
    if emit:
        out_specs = [tok(fwd_r), tok(bwd_r)] + out_specs
        out_shape = [jax.ShapeDtypeStruct((HEADS, t, DH), BF16)] * 2 + out_shape
    for a in casts:
        per = nb // a.shape[0]
        blk = (1, a.shape[1] // per, a.shape[2])
        in_specs.append(pl.BlockSpec(blk, lambda i, per=per: (i // per, i % per, 0)))
        out_specs.append(pl.BlockSpec(blk, lambda i, per=per: (_expert_slot(i // per), i % per, 0)))
        out_shape.append(jax.ShapeDtypeStruct(a.shape, BF16))
    if fft:
        yc, ys, a2, mix = fft
        tokf = pl.BlockSpec((FGROUPS, FFT_ROWS * CHUNK, FCG), lambda i: (0, i, 0))
        fullf = lambda a: pl.BlockSpec(a.shape, lambda i: (0,) * a.ndim)
        in_specs += [tokf, tokf, fullf(a2), fullf(mix)]
        out_specs.append(pl.BlockSpec((FGROUPS, CHUNK, FFT_ROWS, FCG), lambda i: (0, 0, i, 0)))
        out_shape.append(jax.ShapeDtypeStruct((FGROUPS, CHUNK, t // CHUNK, FCG), F32))
    return pl.pallas_call(
        functools.partial(_mlstm_kernel, cps=cps, emit=emit, ncast=len(casts), nfft=len(fft)),
        grid=(nb,),
        in_specs=in_specs,
        out_specs=out_specs,
        out_shape=out_shape,
        scratch_shapes=[pltpu.VMEM(cshape, F32), pltpu.VMEM(mshape, F32)],
        compiler_params=_cparams("arbitrary"),
        name="mlstm",
    )(q, kt, v, gc, dr, q, kt, v, gc, dr, c0, m0, *casts, *fft)


FFT_ROWS = 8


def _dft1_kernel(u_ref, f_ref, cw_ref, sw_ref, yc_ref, ys_ref):
    f = f_ref[...]
    rows = CHUNK * FFT_ROWS
    u2 = u_ref.reshape(FGROUPS * rows, FCG)
    yc2 = yc_ref.reshape(FGROUPS * rows, FCG)
    ys2 = ys_ref.reshape(FGROUPS * rows, FCG)
    for s in range(FFT_ROWS):
        pick = [pl.ds(g * rows + s, CHUNK, stride=FFT_ROWS) for g in range(FGROUPS)]
        x = jnp.concatenate([u2[p, :] for p in pick], axis=1).astype(BF16)
        y = _dot(f, x)
        cw = cw_ref[s]
        sw = sw_ref[s]
        pr = y[:CHUNK] * cw - y[CHUNK:] * sw
        pi = y[:CHUNK] * sw + y[CHUNK:] * cw
        for g in range(FGROUPS):
            yc2[pick[g], :] = pr[:, g * FCG:(g + 1) * FCG]
            ys2[pick[g], :] = pi[:, g * FCG:(g + 1) * FCG]


def _dft2_kernel(yc_ref, ys_ref, a2_ref, mix_ref, o_ref):
    a2 = a2_ref[...]
    rows = CHUNK * FFT_ROWS
    o2 = o_ref.reshape(FGROUPS * rows, FCG)
    for kk in range(FFT_ROWS):
        blk = slice(kk * CHUNK, (kk + 1) * CHUNK)
        yc = jnp.concatenate([yc_ref[g, blk, :] for g in range(FGROUPS)], axis=1)
        ys = jnp.concatenate([ys_ref[g, blk, :] for g in range(FGROUPS)], axis=1)
        p = jnp.concatenate([yc, ys], axis=0).astype(BF16)
        x = _dot(a2, p).astype(BF16)
        for g in range(FGROUPS):
            cols = slice(g * FCG, (g + 1) * FCG)
            cat = jnp.concatenate([x[:CHUNK, cols], x[CHUNK:, cols]], axis=1)
            o2[pl.ds(g * rows + kk, CHUNK, stride=FFT_ROWS), :] = _dot(cat, mix_ref[g])


def _merge_kernel(hf_ref, hb_ref, act_ref, z_ref, yf_ref, x_ref, er_ref, ec_ref,
                  nw_ref, sk_ref, wout_ref, gg1_ref, g2_ref, sh2_ref, wr_ref, br_ref,
                  x1_ref, h2_ref, lg_ref):
    parts = []
    for hd in range(HEADS):
        hh = hf_ref[hd].astype(F32) + hb_ref[hd].astype(F32)
        dl = hh - jnp.mean(hh, axis=-1, keepdims=True)
        var = jnp.mean(dl * dl, axis=-1, keepdims=True)
        parts.append(dl * lax.rsqrt(var + EPS))
    hn = jnp.concatenate(parts, axis=-1)
    z = z_ref[...].astype(F32)
    m = (hn * nw_ref[...] + sk_ref[...] * act_ref[...].astype(F32)) * (z * _sigmoid(z))
    cat = jnp.concatenate([m.astype(BF16)] + [yf_ref[g].astype(BF16) for g in range(FGROUPS)], axis=-1)
    y = _dot(cat, wout_ref[...])
    x3 = _add_pos(x_ref[...], er_ref, ec_ref)
    xp = x3.reshape(x3.shape[0] * GRID_W, D_MODEL)
    x1 = xp + _rms(y) * gg1_ref[...]
    x1_ref[...] = x1
    h2 = _rms(x1) * g2_ref[...] + sh2_ref[...]
    h2_ref[...] = h2.astype(BF16)
    lg = _dot(h2.astype(BF16), wr_ref[...]) + br_ref[...]
    lg_ref[...] = lg.T[:32]


def _merge(hf, hb, act, z, yf, x3, er3, ec3, nw, sk, wout, gg1, g2, sh2, wr, br, *, rows):
    nr = x3.shape[0]
    t = nr * GRID_W
    tm = rows * GRID_W
    tok = pl.BlockSpec((tm, D_MLSTM), lambda i: (i, 0))
    heads = pl.BlockSpec((HEADS, tm, DH), lambda i: (0, i, 0))
    full = lambda a: pl.BlockSpec(a.shape, lambda i: (0,) * a.ndim)
    return pl.pallas_call(
        _merge_kernel,
        grid=(nr // rows,),
        in_specs=[heads, heads, tok, tok, heads,
                  pl.BlockSpec((rows, GRID_W, D_MODEL), lambda i: (i, 0, 0)),
                  pl.BlockSpec((rows, 1, D_MODEL // 2), lambda i: (i, 0, 0)),
                  pl.BlockSpec((1, GRID_W, D_MODEL // 2), lambda i: (0, 0, 0)),
                  full(nw), full(sk), full(wout), full(gg1), full(g2), full(sh2), full(wr), full(br)],
        out_specs=[pl.BlockSpec((tm, D_MODEL), lambda i: (i, 0)),
                   pl.BlockSpec((tm, D_MODEL), lambda i: (i, 0)),
                   pl.BlockSpec((32, tm), lambda i: (0, i))],
        out_shape=[jax.ShapeDtypeStruct((t, D_MODEL), F32),
                   jax.ShapeDtypeStruct((t, D_MODEL), BF16),
                   jax.ShapeDtypeStruct((32, t), F32)],
        compiler_params=_cparams("parallel"),
        name="merge",
    )(hf, hb, act, z, yf, x3, er3, ec3, nw, sk, wout, gg1, g2, sh2, wr, br)


def _route_kernel(lg_ref, pos_ref, w_ref, cnt_ref, *, sbk):
    lg = lg_ref[...]
    tl = lg.shape[1]
    g = [lg[j:j + 1] for j in range(N_GROUPS)]
    e = [lg[N_GROUPS + j:N_GROUPS + j + 1] for j in range(N_EXPERTS)]
    gmax = jnp.maximum(jnp.maximum(g[0], g[1]), jnp.maximum(g[2], g[3]))
    den = jnp.exp(g[0] - gmax) + jnp.exp(g[1] - gmax) + jnp.exp(g[2] - gmax) + jnp.exp(g[3] - gmax)
    p_sel = 1.0 / den
    sel = []
    free = jnp.ones((1, tl), F32)
    for j in range(N_GROUPS):
        s = jnp.where(g[j] >= gmax, free, 0.0)
        sel.append(s)
        free = free - s
    es = []
    for j in range(EPG):
        es.append(sel[0] * e[j] + sel[1] * e[EPG + j] + sel[2] * e[2 * EPG + j] + sel[3] * e[3 * EPG + j])
    rank = []
    for j in range(EPG):
        rj = jnp.zeros((1, tl), F32)
        for i in range(EPG):
            if i == j:
                continue
            beats = (es[i] >= es[j]) if i < j else (es[i] > es[j])
            rj = rj + jnp.where(beats, 1.0, 0.0)
        rank.append(rj)
    v1 = jnp.maximum(jnp.maximum(es[0], es[1]), jnp.maximum(es[2], es[3]))
    v2 = sum(jnp.where(rank[j] == 1.0, es[j], 0.0) for j in range(EPG))
    tt = jnp.exp(v2 - v1)
    w1 = p_sel / (1.0 + tt)
    w2 = w1 * tt
    w = [jnp.where(rank[j] == 0.0, w1, jnp.where(rank[j] == 1.0, w2, 0.0)) for j in range(EPG)]
    top2 = [jnp.where(rank[j] < 2.0, 1.0, 0.0) for j in range(EPG)]
    mem = jnp.concatenate([sel[gi] * top2[j] for gi in range(N_GROUPS) for j in range(EPG)], axis=0)
    wts = jnp.concatenate([sel[gi] * w[j] for gi in range(N_GROUPS) for j in range(EPG)], axis=0)
    w_ref[...] = wts
    lane = lax.broadcasted_iota(jnp.int32, (N_EXPERTS, tl), 1) & (sbk - 1)
    c = mem
    k = 1
    while k < sbk:
        c = c + jnp.where(lane >= k, pltpu.roll(c, k, 1), 0.0)
        k *= 2
    pos_ref[...] = jnp.where(mem > 0.0, c - 1.0, -1.0)
    lane128 = lax.broadcasted_iota(jnp.int32, (N_EXPERTS, LANES), 1)
    cnt = jnp.zeros((N_EXPERTS, LANES), F32)
    for kb in range(tl // sbk):
        tot = jnp.sum(mem[:, kb * sbk:(kb + 1) * sbk], axis=1, keepdims=True)
        cnt = cnt + jnp.where(lane128 == kb, tot, 0.0)
    cnt_ref[...] = cnt


def _route(lg, *, tl, sbk):
    t = lg.shape[1]
    row = pl.BlockSpec((N_EXPERTS, tl), lambda i: (0, i))
    return pl.pallas_call(
        functools.partial(_route_kernel, sbk=sbk),
        grid=(t // tl,),
        in_specs=[pl.BlockSpec((32, tl), lambda i: (0, i))],
        out_specs=[row, row, pl.BlockSpec((N_EXPERTS, LANES), lambda i: (i, 0))],
        out_shape=[jax.ShapeDtypeStruct((N_EXPERTS, t), F32),
                   jax.ShapeDtypeStruct((N_EXPERTS, t), F32),
                   jax.ShapeDtypeStruct((t // tl * N_EXPERTS, LANES), F32)],
        compiler_params=_cparams("parallel"),
        name="route",
    )(lg)


def _slots_kernel(pos_ref, w_ref, first_ref, qrow_ref, qcol_ref):
    tl = pos_ref.shape[1]
    q0 = jnp.full((1, tl), -1.0, F32)
    q1 = jnp.full((1, tl), -1.0, F32)
    w0 = jnp.zeros((1, tl), F32)
    w1 = jnp.zeros((1, tl), F32)
    seen = jnp.zeros((1, tl), F32)
    for ex in range(N_EXPERTS):
        rk = pos_ref[ex:ex + 1, :]
        wt = w_ref[ex:ex + 1, :]
        m = jnp.where(rk >= 0.0, 1.0, 0.0)
        val = rk + first_ref[ex:ex + 1, :]
        first = (m * (1.0 - seen)) > 0.0
        second = (m * seen) > 0.0
        q0 = jnp.where(first, val, q0)
        w0 = jnp.where(first, wt, w0)
        q1 = jnp.where(second, val, q1)
        w1 = jnp.where(second, wt, w1)
        seen = seen + m
    qrow_ref[...] = jnp.concatenate([q0, q1, jnp.zeros((6, tl), F32)], axis=0)
    qcol_ref[...] = jnp.concatenate([q0, q1, w0, w1, jnp.zeros((LANES - 4, tl), F32)], axis=0).T


def _slots(pos, w, first, *, tl):
    t = pos.shape[1]
    row = pl.BlockSpec((N_EXPERTS, tl), lambda i: (0, i))
    return pl.pallas_call(
        _slots_kernel,
        grid=(t // tl,),
        in_specs=[row, row, row],
        out_specs=[pl.BlockSpec((8, tl), lambda i: (0, i)),
                   pl.BlockSpec((tl, LANES), lambda i: (i, 0))],
        out_shape=[jax.ShapeDtypeStruct((8, t), F32),
                   jax.ShapeDtypeStruct((t, LANES), F32)],
        compiler_params=_cparams("parallel"),
        name="slots",
    )(pos, w, first)


def _mlp(x, wg, wu, wd):
    gt = _dot(x, wg)
    a = ((gt * _sigmoid(gt)) * _dot(x, wu)).astype(BF16)
    return _dot(a, wd).astype(BF16)


def _moe_kernel(tab_ref, loc_ref, h_ref, qrow_ref, wg_ref, wu_ref, wd_ref,
                qcol_ref, x1_ref, gg2_ref, o_ref, xs_ref, *, sbk, slots, csb):
    i = pl.program_id(0)
    step = pl.program_id(1)
    nsb = h_ref.shape[0] // sbk
    eps = wg_ref.shape[0]
    exp_steps = N_EXPERTS // eps

    @pl.when(step == 0)
    def _():
        pid = lax.broadcasted_iota(jnp.int32, (slots, sbk), 0).astype(F32)

        def select(sb, carry):
            q0 = qrow_ref[0, pl.ds(sb, 1), :]
            q1 = qrow_ref[1, pl.ds(sb, 1), :]
            s = jnp.where(q0 == pid, 1.0, jnp.where(q1 == pid, 1.0, 0.0)).astype(BF16)
            row0 = pl.multiple_of(sb * sbk, sbk)
            xs_ref[sb, 0:slots, :] = _dot(s, h_ref[pl.ds(row0, sbk), :]).astype(BF16)
            xs_ref[sb, slots:slots + 32, :] = jnp.zeros((32, D_MODEL), BF16)
            return carry

        lax.fori_loop(0, nsb, select, 0)

    def run_pieces(k, base, n_pieces):
        locs = []
        for m in range(n_pieces):
            d = loc_ref[0, 0, base + m]
            locs.append((d // MOE_LOC, pl.multiple_of(d % MOE_LOC, 16)))
        x = jnp.concatenate([xs_ref[sb, pl.ds(off, 16), :] for sb, off in locs], axis=0)
        y = _mlp(x, wg_ref[k], wu_ref[k], wd_ref[k])
        for m, (sb, off) in enumerate(locs):
            xs_ref[sb, pl.ds(off, 16), :] = y[m * 16:(m + 1) * 16]

    @pl.when(jnp.logical_and(step >= 1, step <= exp_steps))
    def _():
        def expert(k, carry):
            e = _expert_slot((step - 1) * eps + k)
            total = tab_ref[i * N_EXPERTS + e]
            base = e * MOE_PIECES

            def full(t, c1):
                run_pieces(k, base + t * 32, 32)
                return c1

            n_full = (total + 3) // 32
            lax.fori_loop(0, n_full, full, 0)
            done = n_full * 32
            rest = total - done

            for lo, n in ((0, 8), (8, 12), (12, 16), (16, 20), (20, 24), (24, 28)):
                @pl.when(jnp.logical_and(rest > lo, rest <= n))
                def _():
                    run_pieces(k, base + done, n)

            return carry

        lax.fori_loop(0, eps, expert, 0)

    @pl.when(step > exp_steps)
    def _():
        lane = lax.broadcasted_iota(jnp.int32, (sbk, slots), 1).astype(F32)
        for k in range(csb):
            sb = (step - 1 - exp_steps) * csb + k
            rows = slice(k * sbk, (k + 1) * sbk)
            qc = qcol_ref[rows, :]
            wmat = (jnp.where(lane == qc[:, 0:1], qc[:, 2:3], 0.0)
                    + jnp.where(lane == qc[:, 1:2], qc[:, 3:4], 0.0)).astype(BF16)
            y = _dot(wmat, xs_ref[sb, 0:slots, :])
            o_ref[rows, :] = x1_ref[rows, :] + _rms(y) * gg2_ref[...]


def _moe(tab, loc, h2, qrow3, wg, wu, wd, qcol, x1, gg2, *, tb, sbk, slots, eps, csb):
    t = h2.shape[0]
    nsb = tb // sbk
    comb_steps = nsb // csb
    exp_steps = N_EXPERTS // eps
    wblk = lambda i, s, c: (jnp.where(s == 0, exp_steps - 1, jnp.minimum(s - 1, exp_steps - 1)), 0, 0)
    oblk = lambda i, s, c: (i * comb_steps + jnp.maximum(s - 1 - exp_steps, 0), 0)
    return pl.pallas_call(
        functools.partial(_moe_kernel, sbk=sbk, slots=slots, csb=csb),
        grid_spec=pltpu.PrefetchScalarGridSpec(
            num_scalar_prefetch=1,
            grid=(t // tb, 1 + exp_steps + comb_steps),
            in_specs=[pl.BlockSpec((1, 1, N_EXPERTS * MOE_PIECES), lambda i, s, c: (i, 0, 0), memory_space=pltpu.SMEM),
                      pl.BlockSpec((tb, D_MODEL), lambda i, s, c: (i, 0)),
                      pl.BlockSpec((8, nsb, sbk), lambda i, s, c: (0, i, 0)),
                      pl.BlockSpec((eps, D_MODEL, D_EXPERT), wblk),
                      pl.BlockSpec((eps, D_MODEL, D_EXPERT), wblk),
                      pl.BlockSpec((eps, D_EXPERT, D_MODEL), wblk),
                      pl.BlockSpec((csb * sbk, LANES), oblk),
                      pl.BlockSpec((csb * sbk, D_MODEL), oblk),
                      pl.BlockSpec((1, D_MODEL), lambda i, s, c: (0, 0))],
            out_specs=pl.BlockSpec((csb * sbk, D_MODEL), oblk),
            scratch_shapes=[pltpu.VMEM((nsb, slots + 32, D_MODEL), BF16)]),
        out_shape=jax.ShapeDtypeStruct((t, D_MODEL), F32),
        compiler_params=pltpu.CompilerParams(dimension_semantics=("parallel", "arbitrary"),
                                             vmem_limit_bytes=MOE_VMEM_LIMIT),
        name="moe",
    )(tab, loc, h2, qrow3, wg, wu, wd, qcol, x1, gg2)


def _slot_tables(cnt, *, nsb, slots):
    tiles = (cnt + 15) // 16
    first = (jnp.cumsum(tiles, axis=1) - tiles) * 16
    tl = tiles.reshape(-1, nsb, N_EXPERTS)
    end = jnp.cumsum(tl, axis=1)
    beg = end - tl
    q = jnp.arange(MOE_PIECES, dtype=jnp.int32)[None, None, None, :]
    inside = jnp.logical_and(q >= beg[..., None], q < end[..., None])
    sb = jnp.arange(nsb, dtype=jnp.int32)[None, :, None, None]
    code = sb * MOE_LOC + first.reshape(-1, nsb, N_EXPERTS)[..., None] + (q - beg[..., None]) * 16
    loc = jnp.sum(jnp.where(inside, code, 0), axis=1)
    total = end[:, -1, :]
    dump = slots + 16 * (q[0, 0] % 2)
    loc = jnp.where(q[0] < total[..., None], loc, dump)
    return first, total.reshape(-1), loc.reshape(loc.shape[0], 1, -1)


def _pos_tables(rows):
    quarter = D_MODEL // 4
    freq = 1.0 / (POS_BASE ** (np.arange(quarter, dtype=np.float64) / quarter))
    r = np.arange(rows, dtype=np.float64)[:, None] * freq
    cl = np.arange(GRID_W, dtype=np.float64)[:, None] * freq
    er = np.concatenate([np.sin(r), np.cos(r)], axis=-1).astype(np.float32)
    ec = np.concatenate([np.sin(cl), np.cos(cl)], axis=-1).astype(np.float32)
    return jnp.asarray(er[:, None, :]), jnp.asarray(ec[None, :, :])


def _dft_tables(t):
    n = np.arange(CHUNK, dtype=np.int64)
    prod = n[:, None] * n[None, :]
    ang = (prod % CHUNK).astype(np.float64) * (2.0 * np.pi / CHUNK)
    c, s = np.cos(ang), np.sin(ang)
    f1 = np.concatenate([c, s], axis=0)
    a2 = np.concatenate([np.concatenate([c, -s], axis=1), np.concatenate([s, c], axis=1)], axis=0)
    cs = np.concatenate([c, -s], axis=0)
    angw = prod.astype(np.float64) * (2.0 * np.pi / t)
    f32 = lambda a: jnp.asarray(a.astype(np.float32))
    return f32(f1), f32(a2), f32(cs), f32(np.cos(angw)[:, :, None]), f32(np.sin(angw)[:, :, None])


def _blockdiag(w):
    n = w.shape[0]
    size = n * QKV_BLOCK
    spread = np.tile(np.eye(QKV_BLOCK, dtype=np.float32), (1, n))
    rows = jnp.dot(w.reshape(size, QKV_BLOCK), jnp.asarray(spread), precision=lax.Precision.HIGHEST)
    blk = np.arange(size) // QKV_BLOCK
    mask = (blk[:, None] == blk[None, :]).astype(np.float32)
    return rows * jnp.asarray(mask)


def _gate_weights(w_f, b_f, w_b, b_b):
    w = jnp.concatenate([w_f[:, :HEADS], w_b[:, :HEADS], w_f[:, HEADS:], w_b[:, HEADS:]], axis=1).T
    b = jnp.concatenate([b_f[:HEADS], b_b[:HEADS], b_f[HEADS:], b_b[HEADS:]])
    return w.astype(BF16), b[:, None]


def kernel(x, c, ctx, c_ctx, w_ada, b_ada, g_pre_mix, g_post_mix, g_pre_ffn, g_post_ffn,
           w_in, conv_w, conv_b, w_q, w_k, w_v, w_if_fwd, b_if_fwd, w_if_bwd, b_if_bwd,
           mlstm_norm_w, mlstm_skip, w_fourier, w_out, w_router_group, b_router_group,
           w_router_expert, b_router_expert, w_gate, w_up, w_down):
    t = x.shape[1]
    rows = t // GRID_W

    c8 = jnp.concatenate([c, c_ctx[None, :], jnp.zeros((6, D_MODEL), F32)], axis=0)
    mod = _ada(c8, w_ada[0], b_ada[0][None, :])
    shift1, scale1, gate1, shift2, scale2, gate2 = [mod[0:1, k * D_MODEL:(k + 1) * D_MODEL] for k in range(N_MOD)]
    shift1c, scale1c = mod[1:2, 0:D_MODEL], mod[1:2, D_MODEL:2 * D_MODEL]
    g1 = g_pre_mix[0][None, :] * (1.0 + scale1)
    g1c = g_pre_mix[0][None, :] * (1.0 + scale1c)
    gg1 = g_post_mix[0][None, :] * gate1
    g2 = g_pre_ffn[0][None, :] * (1.0 + scale2)
    gg2 = g_post_ffn[0][None, :] * gate2

    er3, ec3 = _pos_tables(rows)
    x3 = x.reshape(rows, GRID_W, D_MODEL)
    ctx3 = ctx.reshape(CTX_LEN // GRID_W, GRID_W, D_MODEL)
    w_in_bf = w_in[0].astype(BF16)

    xm_l, z_l, u_l = _inproj(x3, er3, ec3, g1, shift1, w_in_bf, rows=8, add_pos=True)
    xm_c, _, _ = _inproj(ctx3, er3, ec3, g1c, shift1c, w_in_bf, rows=CTX_LEN // GRID_W, add_pos=False)

    wq = _blockdiag(w_q[0]).astype(BF16)
    wkt = _blockdiag(w_k[0]).T.astype(BF16)
    wv = _blockdiag(w_v[0]).astype(BF16)
    wi, bi = _gate_weights(w_if_fwd[0], b_if_fwd[0], w_if_bwd[0], b_if_bwd[0])
    wiq, wik, wiv = wi[:, :D_MLSTM], wi[:, D_MLSTM:2 * D_MLSTM], wi[:, 2 * D_MLSTM:]
    cb = conv_b[0][None, :]
    f1, a2, cs, cw3, sw3 = _dft_tables(t)
    n1 = t // CHUNK
    q_l, kt_l, v_l, act_l, gp_l, yc, ys = _feat(
        xm_l, conv_w[0], cb, wq, wkt, wv, wiq, wik, wiv, bi, tm=t // (n1 // FFT_ROWS),
        fft=(u_l.reshape(FGROUPS, n1, CHUNK, FCG), f1.astype(BF16), cw3, sw3))
    q_c, kt_c, v_c, _, gp_c = _feat(xm_c, conv_w[0], cb, wq, wkt, wv, wiq, wik, wiv, bi, tm=CTX_LEN)

    dr_l, gc_l = _gates(gp_l, tl=2048)
    dr_c, gc_c = _gates(gp_c, tl=CTX_LEN)

    mix = jnp.einsum('kc,gcd->gkd', cs, w_fourier[0], precision=lax.Precision.HIGHEST)
    mix = (mix * float(1.0 / np.sqrt(float(t) * FCG))).astype(BF16)

    c0 = jnp.zeros((2 * HEADS, DH, 2 * DH), F32)
    m0 = jnp.zeros((2 * HEADS, 8, LANES), F32)
    kt_c, kt_l = kt_c.reshape(HEADS, DH, CTX_LEN), kt_l.reshape(HEADS, DH, t)
    c_ctx_fin, m_ctx_fin = _mlstm(q_c, kt_c, v_c, gc_c, dr_c, c0, m0, cps=CTX_LEN // CHUNK, emit=False)
    hf, hb, _, _, wg_bf, wu_bf, wd_bf, yf = _mlstm(
        q_l, kt_l, v_l, gc_l, dr_l, c_ctx_fin, m_ctx_fin, cps=8, emit=True,
        casts=(w_gate[0], w_up[0], w_down[0]),
        fft=(yc.reshape(FGROUPS, t, FCG), ys.reshape(FGROUPS, t, FCG), a2.astype(BF16), mix))
    yf = yf.reshape(FGROUPS, t, FCG)

    wr = jnp.concatenate([w_router_group[0], w_router_expert[0],
                          jnp.zeros((D_MODEL, LANES - N_GROUPS - N_EXPERTS), F32)], axis=1)
    br = jnp.concatenate([b_router_group[0], b_router_expert[0],
                          jnp.zeros((LANES - N_GROUPS - N_EXPERTS,), F32)])[None, :]
    x1, h2, lg = _merge(hf, hb, act_l, z_l, yf, x3, er3, ec3,
                        mlstm_norm_w[0][None, :], mlstm_skip[0][None, :], w_out[0].astype(BF16),
                        gg1, g2, shift2, wr.astype(BF16), br, rows=16)
    pos, wts, cnt = _route(lg, tl=MOE_TB, sbk=MOE_SBK)
    nblk, nsb = t // MOE_TB, MOE_TB // MOE_SBK
    cnt = cnt.reshape(nblk, N_EXPERTS, LANES)[:, :, :nsb]
    cnt = jnp.transpose(cnt, (0, 2, 1)).reshape(nblk * nsb, N_EXPERTS).astype(jnp.int32)
    first, total, loc = _slot_tables(cnt, nsb=nsb, slots=MOE_SLOTS)
    first_rows = jnp.repeat(first.T.astype(F32), MOE_SBK, axis=1)
    qrow, qcol = _slots(pos, wts, first_rows, tl=MOE_TB)
    out = _moe(total, loc, h2, qrow.reshape(8, t // MOE_SBK, MOE_SBK), wg_bf, wu_bf, wd_bf,
               qcol, x1, gg2, tb=MOE_TB, sbk=MOE_SBK, slots=MOE_SLOTS, eps=MOE_EPS, csb=MOE_CSB)
    return out[None]
```

```python
import functools

import numpy as np
import jax
import jax.numpy as jnp
from jax import lax
from jax.experimental import pallas as pl
from jax.experimental.pallas import tpu as pltpu

F32 = jnp.float32
BF16 = jnp.bfloat16

D_MODEL = 1024
SEQ = 16384
GRID_W = 64
CTX_LEN = 256
D_MLSTM = 512
HEADS = 4
DH = 128
QKV_BLOCK = 4
CONV_K = 3
CHUNK = 128
D_FOURIER = 512
FGROUPS = 4
FCG = 128
N_GROUPS = 4
EPG = 4
N_EXPERTS = 16
D_EXPERT = 512
N_MOD = 6
EPS = 1e-6
POS_BASE = 10000.0
LANES = 128
NEG_BIG = -3.0e38

VMEM_LIMIT = 58 * 1024 * 1024
MOE_VMEM_LIMIT = 58 * 1024 * 1024
MOE_TB = 2048
MOE_SBK = 256
MOE_SLOTS = 768
MOE_EPS = 4
MOE_CSB = 2
MOE_LOC = 1024
MOE_PIECES = MOE_TB // 16


def _cparams(*sem):
    return pltpu.CompilerParams(dimension_semantics=sem, vmem_limit_bytes=VMEM_LIMIT)


def _dot(a, b):
    return jnp.dot(a, b, preferred_element_type=F32)


def _dot_nt(a, b):
    return lax.dot_general(a, b, (((1,), (1,)), ((), ())), preferred_element_type=F32)


def _split_bf16(a):
    hi = a.astype(BF16)
    lo = (a - hi.astype(F32)).astype(BF16)
    return hi, lo


def _dot3(a, b):
    a_hi, a_lo = _split_bf16(a)
    b_hi, b_lo = _split_bf16(b)
    return _dot(a_hi, b_hi) + (_dot(a_hi, b_lo) + _dot(a_lo, b_hi))


def _sigmoid(x):
    return 1.0 / (1.0 + jnp.exp(-x))


def _rms(x):
    return x * lax.rsqrt(jnp.mean(x * x, axis=-1, keepdims=True) + EPS)


def _ada_kernel(c_ref, w_ref, b_ref, o_ref):
    c = c_ref[...]
    s = c * _sigmoid(c)
    o_ref[...] = _dot3(s, w_ref[...]) + b_ref[...]


def _ada(c8, w, b):
    n = w.shape[1]
    tn = 768
    return pl.pallas_call(
        _ada_kernel,
        grid=(n // tn,),
        in_specs=[pl.BlockSpec((8, D_MODEL), lambda j: (0, 0)),
                  pl.BlockSpec((D_MODEL, tn), lambda j: (0, j)),
                  pl.BlockSpec((1, tn), lambda j: (0, j))],
        out_specs=pl.BlockSpec((8, tn), lambda j: (0, j)),
        out_shape=jax.ShapeDtypeStruct((8, n), F32),
        compiler_params=_cparams("parallel"),
        name="ada",
    )(c8, w, b)


def _add_pos(x3, er_ref, ec_ref):
    r = x3.shape[0]
    pr = jnp.broadcast_to(er_ref[...], (r, GRID_W, D_MODEL // 2))
    pc = jnp.broadcast_to(ec_ref[...], (r, GRID_W, D_MODEL // 2))
    return x3 + jnp.concatenate([pr, pc], axis=-1)


def _inproj_kernel(x_ref, er_ref, ec_ref, g_ref, sh_ref, w_ref, xm_ref, z_ref, u_ref, *, add_pos):
    x3 = x_ref[...]
    if add_pos:
        x3 = _add_pos(x3, er_ref, ec_ref)
    x = x3.reshape(x3.shape[0] * GRID_W, D_MODEL)
    h = _rms(x) * g_ref[...] + sh_ref[...]
    proj = _dot(h.astype(BF16), w_ref[...])
    xm_ref[...] = proj[:, :D_MLSTM].astype(BF16)
    z_ref[...] = proj[:, D_MLSTM:2 * D_MLSTM].astype(BF16)
    for g in range(FGROUPS):
        u_ref[g] = proj[:, 2 * D_MLSTM + g * FCG:2 * D_MLSTM + (g + 1) * FCG]


def _inproj(x3, er3, ec3, g_eff, shift, w_in, *, rows, add_pos):
    nr = x3.shape[0]
    t = nr * GRID_W
    tm = rows * GRID_W
    out = jax.ShapeDtypeStruct((t, D_MLSTM), BF16)
    ospec = pl.BlockSpec((tm, D_MLSTM), lambda i: (i, 0))
    vec = pl.BlockSpec((1, D_MODEL), lambda i: (0, 0))
    return pl.pallas_call(
        functools.partial(_inproj_kernel, add_pos=add_pos),
        grid=(nr // rows,),
        in_specs=[pl.BlockSpec((rows, GRID_W, D_MODEL), lambda i: (i, 0, 0)),
                  pl.BlockSpec((rows, 1, D_MODEL // 2), lambda i: (i, 0, 0)),
                  pl.BlockSpec((1, GRID_W, D_MODEL // 2), lambda i: (0, 0, 0)),
                  vec, vec,
                  pl.BlockSpec(w_in.shape, lambda i: (0, 0))],
        out_specs=[ospec, ospec, pl.BlockSpec((FGROUPS, tm, FCG), lambda i: (0, i, 0))],
        out_shape=[out, out, jax.ShapeDtypeStruct((FGROUPS, t, FCG), F32)],
        compiler_params=_cparams("parallel"),
        name="inproj",
    )(x3, er3, ec3, g_eff, shift, w_in)


def _feat_kernel(xm_ref, prev_ref, next_ref, cw_ref, cb_ref, wq_ref, wkt_ref, wv_ref,
                 wiq_ref, wik_ref, wiv_ref, bi_ref, *rest):
    if len(rest) > 5:
        _dft1_kernel(*rest[:4], *rest[9:])
        rest = rest[4:9]
    q_ref, kt_ref, v_ref, act_ref, g_ref = rest
    i = pl.program_id(0)
    n = pl.num_programs(0)
    xm_bf = xm_ref[...]
    xm = xm_bf.astype(F32)
    tm = xm.shape[0]
    prev_row = prev_ref[...].astype(F32)[15:16, :] * jnp.where(i > 0, 1.0, 0.0)
    next_row = next_ref[...].astype(F32)[0:1, :] * jnp.where(i < n - 1, 1.0, 0.0)
    rid = lax.broadcasted_iota(jnp.int32, (tm, 1), 0)
    x_left = jnp.where(rid == 0, prev_row, pltpu.roll(xm, 1, 0))
    x_right = jnp.where(rid == tm - 1, next_row, pltpu.roll(xm, tm - 1, 0))
    cw = cw_ref[...]
    y = cw[0:1] * x_left + cw[1:2] * xm + cw[2:3] * x_right + cb_ref[...]
    act = (y * _sigmoid(y)).astype(BF16)
    act_ref[...] = act
    q = _dot(act, wq_ref[...])
    kt = _dot_nt(wkt_ref[...], act)
    v = _dot(xm_bf, wv_ref[...])
    q_bf = q.astype(BF16)
    kt_bf = kt.astype(BF16)
    v_bf = v.astype(BF16)
    q_s = (q * (DH ** -0.5)).astype(BF16)
    for hd in range(HEADS):
        q_ref[hd] = q_s[:, hd * DH:(hd + 1) * DH]
        v_ref[hd] = v_bf[:, hd * DH:(hd + 1) * DH]
    kt_ref[...] = kt_bf
    g = _dot_nt(wiq_ref[...], q_bf) + _dot(wik_ref[...], kt_bf) + _dot_nt(wiv_ref[...], v_bf)
    g_ref[...] = g + bi_ref[...]


def _feat(xm, conv_w, conv_b, wq, wkt, wv, wiq, wik, wiv, bi, *, tm, fft=()):
    t = xm.shape[0]
    nb16 = t // 16
    k16 = tm // 16
    full = lambda a: pl.BlockSpec(a.shape, lambda i: (0,) * a.ndim)
    tok = pl.BlockSpec((tm, D_MLSTM), lambda i: (i, 0))
    heads = pl.BlockSpec((HEADS, tm, DH), lambda i: (0, i, 0))
    in_specs = [tok,
                pl.BlockSpec((16, D_MLSTM), lambda i: (jnp.maximum(i * k16 - 1, 0), 0)),
                pl.BlockSpec((16, D_MLSTM), lambda i: (jnp.minimum((i + 1) * k16, nb16 - 1), 0)),
                full(conv_w), full(conv_b), full(wq), full(wkt), full(wv),
                full(wiq), full(wik), full(wiv), full(bi)]
    out_specs = [heads,
                 pl.BlockSpec((D_MLSTM, tm), lambda i: (0, i)),
                 heads, tok,
                 pl.BlockSpec((16, tm), lambda i: (0, i))]
    out_shape = [jax.ShapeDtypeStruct((HEADS, t, DH), BF16),
                 jax.ShapeDtypeStruct((D_MLSTM, t), BF16),
                 jax.ShapeDtypeStruct((HEADS, t, DH), BF16),
                 jax.ShapeDtypeStruct((t, D_MLSTM), BF16),
                 jax.ShapeDtypeStruct((16, t), F32)]
    if fft:
        u4, f1, cw3, sw3 = fft
        blk = pl.BlockSpec((FGROUPS, CHUNK, FFT_ROWS, FCG), lambda j: (0, 0, j, 0))
        tw = pl.BlockSpec((FFT_ROWS, CHUNK, 1), lambda j: (j, 0, 0))
        in_specs += [blk, full(f1), tw, tw]
        out_specs += [blk, blk]
        out_shape += [jax.ShapeDtypeStruct(u4.shape, F32)] * 2
    return pl.pallas_call(
        _feat_kernel,
        grid=(t // tm,),
        in_specs=in_specs,
        out_specs=out_specs,
        out_shape=out_shape,
        compiler_params=_cparams("parallel"),
        name="feat",
    )(xm, xm, xm, conv_w, conv_b, wq, wkt, wv, wiq, wik, wiv, bi, *fft)


def _gates_kernel(g_ref, d_ref, gc_ref):
    g = g_ref[...]
    tl = g.shape[1]
    ig = g[0:8]
    fg = g[8:16]
    lf = jnp.minimum(fg, 0.0) - jnp.log(1.0 + jnp.exp(-jnp.abs(fg)))
    pos = lax.broadcasted_iota(jnp.int32, (8, tl), 1) & (CHUNK - 1)
    is_fwd = lax.broadcasted_iota(jnp.int32, (8, tl), 0) < HEADS

    def scan(x, op, ident):
        xf = x
        xb = x
        k = 1
        while k < CHUNK:
            xf = op(xf, jnp.where(pos >= k, pltpu.roll(xf, k, 1), ident))
            xb = op(xb, jnp.where(pos < CHUNK - k, pltpu.roll(xb, tl - k, 1), ident))
            k *= 2
        return jnp.where(is_fwd, xf, xb)

    b = scan(lf, jnp.add, 0.0)
    d = ig - b
    mloc = b + scan(d, jnp.maximum, NEG_BIG)
    d_ref[...] = d
    stack = jnp.concatenate([b, mloc, jnp.zeros((LANES - 16, tl), F32)], axis=0)
    gc_ref[...] = stack.T


def _gates(g, *, tl):
    t = g.shape[1]
    return pl.pallas_call(
        _gates_kernel,
        grid=(t // tl,),
        in_specs=[pl.BlockSpec((16, tl), lambda i: (0, i))],
        out_specs=[pl.BlockSpec((8, tl), lambda i: (0, i)),
                   pl.BlockSpec((tl, LANES), lambda i: (i, 0))],
        out_shape=[jax.ShapeDtypeStruct((8, t), F32),
                   jax.ShapeDtypeStruct((t, LANES), F32)],
        compiler_params=_cparams("parallel"),
        name="gates",
    )(g)


def _mlstm_kernel(*refs, cps, emit, ncast, nfft=0):
    (qf_ref, kf_ref, vf_ref, gcf_ref, drf_ref,
     qb_ref, kb_ref, vb_ref, gcb_ref, drb_ref, c0_ref, m0_ref) = refs[:12]
    cast_in = refs[12:12 + ncast]
    fft_in = refs[12 + ncast:12 + ncast + nfft]
    outs = refs[12 + ncast + nfft:]
    if emit:
        hf_ref, hb_ref, cfin_ref, mfin_ref = outs[:4]
        outs = outs[4:]
    else:
        cfin_ref, mfin_ref = outs[:2]
        outs = outs[2:]
        hf_ref = hb_ref = None
    cast_out = outs[:ncast]
    fft_out = outs[ncast:ncast + (1 if nfft else 0)]
    c_scr, m_scr = outs[ncast + (1 if nfft else 0):]
    step = pl.program_id(0)

    for src, dst in zip(cast_in, cast_out):
        dst[...] = src[...].astype(BF16)
    if nfft:
        _dft2_kernel(*fft_in, *fft_out)

    @pl.when(step == 0)
    def _():
        c_scr[...] = c0_ref[...]
        m_scr[...] = m0_ref[...]

    ti = lax.broadcasted_iota(jnp.int32, (CHUNK, CHUNK), 0)
    si = lax.broadcasted_iota(jnp.int32, (CHUNK, CHUNK), 1)
    nch = 2 * HEADS
    mask = jnp.concatenate([jnp.broadcast_to((si <= ti)[None], (HEADS, CHUNK, CHUNK)),
                            jnp.broadcast_to((si >= ti)[None], (HEADS, CHUNK, CHUNK))], axis=0)
    ones_blk = jnp.ones((nch, CHUNK, CHUNK), BF16)
    lane8 = lax.broadcasted_iota(jnp.int32, (1, LANES), 1)

    def bdot(a, b):
        return lax.dot_general(a, b, (((2,), (1,)), ((0,), (0,))), preferred_element_type=F32)

    def both(f, g):
        return [f(r) for r in range(HEADS)] + [g(r) for r in range(HEADS, nch)]

    for j in range(cps):
        rf = slice(j * CHUNK, (j + 1) * CHUNK)
        rb = slice((cps - 1 - j) * CHUNK, (cps - j) * CHUNK)
        gcf, gcb = gcf_ref[rf, :], gcb_ref[rb, :]
        drf, drb = drf_ref[:, rf], drb_ref[:, rb]
        lf, lb = CHUNK - 1, 0
        drow = jnp.stack(both(lambda r: drf[r:r + 1, :], lambda r: drb[r:r + 1, :]))
        btot = jnp.stack(both(lambda r: gcf[lf:lf + 1, r:r + 1], lambda r: gcb[lb:lb + 1, r:r + 1]))
        amax = jnp.stack(both(lambda r: gcf[lf:lf + 1, 8 + r:9 + r], lambda r: gcb[lb:lb + 1, 8 + r:9 + r]))
        m0 = jnp.stack([m_scr[r][0:1, 0:1] for r in range(nch)])
        c_aug = c_scr[...]
        kt = jnp.concatenate([kf_ref[:, :, rf], kb_ref[:, :, rb]], axis=0)
        vaug = jnp.concatenate([jnp.concatenate([vf_ref[:, rf, :], vb_ref[:, rb, :]], axis=0), ones_blk], axis=2)
        if emit:
            qh = jnp.concatenate([qf_ref[:, rf, :], qb_ref[:, rb, :]], axis=0)
            m0_lanes = sum(jnp.where(lane8 == r, m_scr[r][0:1, :], 0.0) for r in range(nch))
            rel_l, eneg_l = [], []
            for gc in (gcf, gcb):
                m_t = jnp.maximum(gc + m0_lanes, pltpu.roll(gc, LANES - 8, 1))
                rel_l.append(gc - m_t)
                eneg_l.append(jnp.exp(-m_t))
            pick = lambda x, r: jnp.take_along_axis(x, jnp.full((CHUNK, LANES), r, jnp.int32), axis=1)
            rel = jnp.stack(both(lambda r: pick(rel_l[0], r), lambda r: pick(rel_l[1], r)))
            eneg = jnp.stack(both(lambda r: pick(eneg_l[0], r), lambda r: pick(eneg_l[1], r)))
            dmat = jnp.where(mask, jnp.exp(rel + drow), 0.0)
            smat = (bdot(qh, kt) * dmat).astype(BF16)
            q_in = (qh.astype(F32) * jnp.exp(rel + m0)).astype(BF16)
            num = bdot(jnp.concatenate([smat, q_in], axis=2),
                       jnp.concatenate([vaug, c_aug.astype(BF16)], axis=1))
            den = jnp.maximum(jnp.abs(num[:, :, DH:]), eneg)
            h = (num[:, :, :DH] / den).astype(hf_ref.dtype)
            hf_ref[:, rf, :] = h[:HEADS]
            hb_ref[:, rb, :] = h[HEADS:]
        m_new = jnp.maximum(btot + m0, amax)
        decay = jnp.exp(btot + m0 - m_new)
        kw = (kt.astype(F32) * jnp.exp(btot + drow - m_new)).astype(BF16)
        c_scr[...] = decay * c_aug + bdot(kw, vaug)
        m_scr[...] = jnp.broadcast_to(m_new, (nch, 8, LANES))

    @pl.when(step == pl.num_programs(0) - 1)
    def _():
        cfin_ref[...] = c_scr[...]
        mfin_ref[...] = m_scr[...]


def _expert_slot(e):
    return (e % EPG) * N_GROUPS + e // EPG


def _mlstm(q, kt, v, gc, dr, c0, m0, *, cps, emit, casts=(), fft=()):
    t = q.shape[1]
    cb = cps * CHUNK
    nb = t // cb
    fwd_r = lambda i: (i, 0)
    bwd_r = lambda i: (nb - 1 - i, 0)
    fwd_c = lambda i: (0, i)
    bwd_c = lambda i: (0, nb - 1 - i)
    tok = lambda f: pl.BlockSpec((HEADS, cb, DH), lambda i, f=f: (0, f(i)[0], 0))
    in_specs = []
    for fr, fc in ((fwd_r, fwd_c), (bwd_r, bwd_c)):
        in_specs += [tok(fr), pl.BlockSpec((HEADS, DH, cb), lambda i, fc=fc: (0, 0, fc(i)[1])), tok(fr),
                     pl.BlockSpec((cb, LANES), fr), pl.BlockSpec((8, cb), fc)]
    cshape = (2 * HEADS, DH, 2 * DH)
    mshape = (2 * HEADS, 8, LANES)
    cspec = pl.BlockSpec(cshape, lambda i: (0, 0, 0))
    mspec = pl.BlockSpec(mshape, lambda i: (0, 0, 0))
    in_specs += [cspec, mspec]
    out_specs = [cspec, mspec]
    out_shape = [jax.ShapeDtypeStruct(cshape, F32), jax.ShapeDtypeStruct(mshape, F32)]
    if emit:
        out_specs = [tok(fwd_r), tok(bwd_r)] + out_specs
        out_shape = [jax.ShapeDtypeStruct((HEADS, t, DH), BF16)] * 2 + out_shape
    for a in casts:
        per = nb // a.shape[0]
        blk = (1, a.shape[1] // per, a.shape[2])
        in_specs.append(pl.BlockSpec(blk, lambda i, per=per: (i // per, i % per, 0)))
        out_specs.append(pl.BlockSpec(blk, lambda i, per=per: (_expert_slot(i // per), i % per, 0)))
        out_shape.append(jax.ShapeDtypeStruct(a.shape, BF16))
    if fft:
        yc, ys, a2, mix = fft
        tokf = pl.BlockSpec((FGROUPS, FFT_ROWS * CHUNK, FCG), lambda i: (0, i, 0))
        fullf = lambda a: pl.BlockSpec(a.shape, lambda i: (0,) * a.ndim)
        in_specs += [tokf, tokf, fullf(a2), fullf(mix)]
        out_specs.append(pl.BlockSpec((FGROUPS, CHUNK, FFT_ROWS, FCG), lambda i: (0, 0, i, 0)))
        out_shape.append(jax.ShapeDtypeStruct((FGROUPS, CHUNK, t // CHUNK, FCG), F32))
    return pl.pallas_call(
        functools.partial(_mlstm_kernel, cps=cps, emit=emit, ncast=len(casts), nfft=len(fft)),
        grid=(nb,),
        in_specs=in_specs,
        out_specs=out_specs,
        out_shape=out_shape,
        scratch_shapes=[pltpu.VMEM(cshape, F32), pltpu.VMEM(mshape, F32)],
        compiler_params=_cparams("arbitrary"),
        name="mlstm",
    )(q, kt, v, gc, dr, q, kt, v, gc, dr, c0, m0, *casts, *fft)


FFT_ROWS = 8


def _dft1_kernel(u_ref, f_ref, cw_ref, sw_ref, yc_ref, ys_ref):
    f = f_ref[...]
    rows = CHUNK * FFT_ROWS
    u2 = u_ref.reshape(FGROUPS * rows, FCG)
    yc2 = yc_ref.reshape(FGROUPS * rows, FCG)
    ys2 = ys_ref.reshape(FGROUPS * rows, FCG)
    for s in range(FFT_ROWS):
        pick = [pl.ds(g * rows + s, CHUNK, stride=FFT_ROWS) for g in range(FGROUPS)]
        x = jnp.concatenate([u2[p, :] for p in pick], axis=1).astype(BF16)
        y = _dot(f, x)
        cw = cw_ref[s]
        sw = sw_ref[s]
        pr = y[:CHUNK] * cw - y[CHUNK:] * sw
        pi = y[:CHUNK] * sw + y[CHUNK:] * cw
        for g in range(FGROUPS):
            yc2[pick[g], :] = pr[:, g * FCG:(g + 1) * FCG]
            ys2[pick[g], :] = pi[:, g * FCG:(g + 1) * FCG]


def _dft2_kernel(yc_ref, ys_ref, a2_ref, mix_ref, o_ref):
    a2 = a2_ref[...]
    rows = CHUNK * FFT_ROWS
    o2 = o_ref.reshape(FGROUPS * rows, FCG)
    for kk in range(FFT_ROWS):
        blk = slice(kk * CHUNK, (kk + 1) * CHUNK)
        yc = jnp.concatenate([yc_ref[g, blk, :] for g in range(FGROUPS)], axis=1)
        ys = jnp.concatenate([ys_ref[g, blk, :] for g in range(FGROUPS)], axis=1)
        p = jnp.concatenate([yc, ys], axis=0).astype(BF16)
        x = _dot(a2, p).astype(BF16)
        for g in range(FGROUPS):
            cols = slice(g * FCG, (g + 1) * FCG)
            cat = jnp.concatenate([x[:CHUNK, cols], x[CHUNK:, cols]], axis=1)
            o2[pl.ds(g * rows + kk, CHUNK, stride=FFT_ROWS), :] = _dot(cat, mix_ref[g])


def _merge_kernel(hf_ref, hb_ref, act_ref, z_ref, yf_ref, x_ref, er_ref, ec_ref,
                  nw_ref, sk_ref, wout_ref, gg1_ref, g2_ref, sh2_ref, wr_ref, br_ref,
                  x1_ref, h2_ref, lg_ref):
    parts = []
    for hd in range(HEADS):
        hh = hf_ref[hd].astype(F32) + hb_ref[hd].astype(F32)
        dl = hh - jnp.mean(hh, axis=-1, keepdims=True)
        var = jnp.mean(dl * dl, axis=-1, keepdims=True)
        parts.append(dl * lax.rsqrt(var + EPS))
    hn = jnp.concatenate(parts, axis=-1)
    z = z_ref[...].astype(F32)
    m = (hn * nw_ref[...] + sk_ref[...] * act_ref[...].astype(F32)) * (z * _sigmoid(z))
    cat = jnp.concatenate([m.astype(BF16)] + [yf_ref[g].astype(BF16) for g in range(FGROUPS)], axis=-1)
    y = _dot(cat, wout_ref[...])
    x3 = _add_pos(x_ref[...], er_ref, ec_ref)
    xp = x3.reshape(x3.shape[0] * GRID_W, D_MODEL)
    x1 = xp + _rms(y) * gg1_ref[...]
    x1_ref[...] = x1
    h2 = _rms(x1) * g2_ref[...] + sh2_ref[...]
    h2_ref[...] = h2.astype(BF16)
    lg = _dot(h2.astype(BF16), wr_ref[...]) + br_ref[...]
    lg_ref[...] = lg.T[:32]


def _merge(hf, hb, act, z, yf, x3, er3, ec3, nw, sk, wout, gg1, g2, sh2, wr, br, *, rows):
    nr = x3.shape[0]
    t = nr * GRID_W
    tm = rows * GRID_W
    tok = pl.BlockSpec((tm, D_MLSTM), lambda i: (i, 0))
    heads = pl.BlockSpec((HEADS, tm, DH), lambda i: (0, i, 0))
    full = lambda a: pl.BlockSpec(a.shape, lambda i: (0,) * a.ndim)
    return pl.pallas_call(
        _merge_kernel,
        grid=(nr // rows,),
        in_specs=[heads, heads, tok, tok, heads,
                  pl.BlockSpec((rows, GRID_W, D_MODEL), lambda i: (i, 0, 0)),
                  pl.BlockSpec((rows, 1, D_MODEL // 2), lambda i: (i, 0, 0)),
                  pl.BlockSpec((1, GRID_W, D_MODEL // 2), lambda i: (0, 0, 0)),
                  full(nw), full(sk), full(wout), full(gg1), full(g2), full(sh2), full(wr), full(br)],
        out_specs=[pl.BlockSpec((tm, D_MODEL), lambda i: (i, 0)),
                   pl.BlockSpec((tm, D_MODEL), lambda i: (i, 0)),
                   pl.BlockSpec((32, tm), lambda i: (0, i))],
        out_shape=[jax.ShapeDtypeStruct((t, D_MODEL), F32),
                   jax.ShapeDtypeStruct((t, D_MODEL), BF16),
                   jax.ShapeDtypeStruct((32, t), F32)],
        compiler_params=_cparams("parallel"),
        name="merge",
    )(hf, hb, act, z, yf, x3, er3, ec3, nw, sk, wout, gg1, g2, sh2, wr, br)


def _route_kernel(lg_ref, pos_ref, w_ref, cnt_ref, *, sbk):
    lg = lg_ref[...]
    tl = lg.shape[1]
    g = [lg[j:j + 1] for j in range(N_GROUPS)]
    e = [lg[N_GROUPS + j:N_GROUPS + j + 1] for j in range(N_EXPERTS)]
    gmax = jnp.maximum(jnp.maximum(g[0], g[1]), jnp.maximum(g[2], g[3]))
    den = jnp.exp(g[0] - gmax) + jnp.exp(g[1] - gmax) + jnp.exp(g[2] - gmax) + jnp.exp(g[3] - gmax)
    p_sel = 1.0 / den
    sel = []
    free = jnp.ones((1, tl), F32)
    for j in range(N_GROUPS):
        s = jnp.where(g[j] >= gmax, free, 0.0)
        sel.append(s)
        free = free - s
    es = []
    for j in range(EPG):
        es.append(sel[0] * e[j] + sel[1] * e[EPG + j] + sel[2] * e[2 * EPG + j] + sel[3] * e[3 * EPG + j])
    rank = []
    for j in range(EPG):
        rj = jnp.zeros((1, tl), F32)
        for i in range(EPG):
            if i == j:
                continue
            beats = (es[i] >= es[j]) if i < j else (es[i] > es[j])
            rj = rj + jnp.where(beats, 1.0, 0.0)
        rank.append(rj)
    v1 = jnp.maximum(jnp.maximum(es[0], es[1]), jnp.maximum(es[2], es[3]))
    v2 = sum(jnp.where(rank[j] == 1.0, es[j], 0.0) for j in range(EPG))
    tt = jnp.exp(v2 - v1)
    w1 = p_sel / (1.0 + tt)
    w2 = w1 * tt
    w = [jnp.where(rank[j] == 0.0, w1, jnp.where(rank[j] == 1.0, w2, 0.0)) for j in range(EPG)]
    top2 = [jnp.where(rank[j] < 2.0, 1.0, 0.0) for j in range(EPG)]
    mem = jnp.concatenate([sel[gi] * top2[j] for gi in range(N_GROUPS) for j in range(EPG)], axis=0)
    wts = jnp.concatenate([sel[gi] * w[j] for gi in range(N_GROUPS) for j in range(EPG)], axis=0)
    w_ref[...] = wts
    lane = lax.broadcasted_iota(jnp.int32, (N_EXPERTS, tl), 1) & (sbk - 1)
    c = mem
    k = 1
    while k < sbk:
        c = c + jnp.where(lane >= k, pltpu.roll(c, k, 1), 0.0)
        k *= 2
    pos_ref[...] = jnp.where(mem > 0.0, c - 1.0, -1.0)
    lane128 = lax.broadcasted_iota(jnp.int32, (N_EXPERTS, LANES), 1)
    cnt = jnp.zeros((N_EXPERTS, LANES), F32)
    for kb in range(tl // sbk):
        tot = jnp.sum(mem[:, kb * sbk:(kb + 1) * sbk], axis=1, keepdims=True)
        cnt = cnt + jnp.where(lane128 == kb, tot, 0.0)
    cnt_ref[...] = cnt


def _route(lg, *, tl, sbk):
    t = lg.shape[1]
    row = pl.BlockSpec((N_EXPERTS, tl), lambda i: (0, i))
    return pl.pallas_call(
        functools.partial(_route_kernel, sbk=sbk),
        grid=(t // tl,),
        in_specs=[pl.BlockSpec((32, tl), lambda i: (0, i))],
        out_specs=[row, row, pl.BlockSpec((N_EXPERTS, LANES), lambda i: (i, 0))],
        out_shape=[jax.ShapeDtypeStruct((N_EXPERTS, t), F32),
                   jax.ShapeDtypeStruct((N_EXPERTS, t), F32),
                   jax.ShapeDtypeStruct((t // tl * N_EXPERTS, LANES), F32)],
        compiler_params=_cparams("parallel"),
        name="route",
    )(lg)


def _slots_kernel(pos_ref, w_ref, first_ref, qrow_ref, qcol_ref):
    tl = pos_ref.shape[1]
    q0 = jnp.full((1, tl), -1.0, F32)
    q1 = jnp.full((1, tl), -1.0, F32)
    w0 = jnp.zeros((1, tl), F32)
    w1 = jnp.zeros((1, tl), F32)
    seen = jnp.zeros((1, tl), F32)
    for ex in range(N_EXPERTS):
        rk = pos_ref[ex:ex + 1, :]
        wt = w_ref[ex:ex + 1, :]
        m = jnp.where(rk >= 0.0, 1.0, 0.0)
        val = rk + first_ref[ex:ex + 1, :]
        first = (m * (1.0 - seen)) > 0.0
        second = (m * seen) > 0.0
        q0 = jnp.where(first, val, q0)
        w0 = jnp.where(first, wt, w0)
        q1 = jnp.where(second, val, q1)
        w1 = jnp.where(second, wt, w1)
        seen = seen + m
    qrow_ref[...] = jnp.concatenate([q0, q1, jnp.zeros((6, tl), F32)], axis=0)
    qcol_ref[...] = jnp.concatenate([q0, q1, w0, w1, jnp.zeros((LANES - 4, tl), F32)], axis=0).T


def _slots(pos, w, first, *, tl):
    t = pos.shape[1]
    row = pl.BlockSpec((N_EXPERTS, tl), lambda i: (0, i))
    return pl.pallas_call(
        _slots_kernel,
        grid=(t // tl,),
        in_specs=[row, row, row],
        out_specs=[pl.BlockSpec((8, tl), lambda i: (0, i)),
                   pl.BlockSpec((tl, LANES), lambda i: (i, 0))],
        out_shape=[jax.ShapeDtypeStruct((8, t), F32),
                   jax.ShapeDtypeStruct((t, LANES), F32)],
        compiler_params=_cparams("parallel"),
        name="slots",
    )(pos, w, first)


def _mlp(x, wg, wu, wd):
    gt = _dot(x, wg)
    a = ((gt * _sigmoid(gt)) * _dot(x, wu)).astype(BF16)
    return _dot(a, wd).astype(BF16)


def _moe_kernel(tab_ref, loc_ref, h_ref, qrow_ref, wg_ref, wu_ref, wd_ref,
                qcol_ref, x1_ref, gg2_ref, o_ref, xs_ref, *, sbk, slots, csb):
    i = pl.program_id(0)
    step = pl.program_id(1)
    nsb = h_ref.shape[0] // sbk
    eps = wg_ref.shape[0]
    exp_steps = N_EXPERTS // eps

    @pl.when(step == 0)
    def _():
        pid = lax.broadcasted_iota(jnp.int32, (slots, sbk), 0).astype(F32)

        def select(sb, carry):
            q0 = qrow_ref[0, pl.ds(sb, 1), :]
            q1 = qrow_ref[1, pl.ds(sb, 1), :]
            s = jnp.where(q0 == pid, 1.0, jnp.where(q1 == pid, 1.0, 0.0)).astype(BF16)
            row0 = pl.multiple_of(sb * sbk, sbk)
            xs_ref[sb, 0:slots, :] = _dot(s, h_ref[pl.ds(row0, sbk), :]).astype(BF16)
            xs_ref[sb, slots:slots + 32, :] = jnp.zeros((32, D_MODEL), BF16)
            return carry

        lax.fori_loop(0, nsb, select, 0)

    def run_pieces(k, base, n_pieces):
        locs = []
        for m in range(n_pieces):
            d = loc_ref[0, 0, base + m]
            locs.append((d // MOE_LOC, pl.multiple_of(d % MOE_LOC, 16)))
        x = jnp.concatenate([xs_ref[sb, pl.ds(off, 16), :] for sb, off in locs], axis=0)
        y = _mlp(x, wg_ref[k], wu_ref[k], wd_ref[k])
        for m, (sb, off) in enumerate(locs):
            xs_ref[sb, pl.ds(off, 16), :] = y[m * 16:(m + 1) * 16]

    @pl.when(jnp.logical_and(step >= 1, step <= exp_steps))
    def _():
        def expert(k, carry):
            e = _expert_slot((step - 1) * eps + k)
            total = tab_ref[i * N_EXPERTS + e]
            base = e * MOE_PIECES

            def full(t, c1):
                run_pieces(k, base + t * 32, 32)
                return c1

            n_full = (total + 3) // 32
            lax.fori_loop(0, n_full, full, 0)
            done = n_full * 32
            rest = total - done

            for lo, n in ((0, 8), (8, 12), (12, 16), (16, 20), (20, 24), (24, 28)):
                @pl.when(jnp.logical_and(rest > lo, rest <= n))
                def _():
                    run_pieces(k, base + done, n)

            return carry

        lax.fori_loop(0, eps, expert, 0)

    @pl.when(step > exp_steps)
    def _():
        lane = lax.broadcasted_iota(jnp.int32, (sbk, slots), 1).astype(F32)
        for k in range(csb):
            sb = (step - 1 - exp_steps) * csb + k
            rows = slice(k * sbk, (k + 1) * sbk)
            qc = qcol_ref[rows, :]
            wmat = (jnp.where(lane == qc[:, 0:1], qc[:, 2:3], 0.0)
                    + jnp.where(lane == qc[:, 1:2], qc[:, 3:4], 0.0)).astype(BF16)
            y = _dot(wmat, xs_ref[sb, 0:slots, :])
            o_ref[rows, :] = x1_ref[rows, :] + _rms(y) * gg2_ref[...]


def _moe(tab, loc, h2, qrow3, wg, wu, wd, qcol, x1, gg2, *, tb, sbk, slots, eps, csb):
    t = h2.shape[0]
    nsb = tb // sbk
    comb_steps = nsb // csb
    exp_steps = N_EXPERTS // eps
    wblk = lambda i, s, c: (jnp.where(s == 0, exp_steps - 1, jnp.minimum(s - 1, exp_steps - 1)), 0, 0)
    oblk = lambda i, s, c: (i * comb_steps + jnp.maximum(s - 1 - exp_steps, 0), 0)
    return pl.pallas_call(
        functools.partial(_moe_kernel, sbk=sbk, slots=slots, csb=csb),
        grid_spec=pltpu.PrefetchScalarGridSpec(
            num_scalar_prefetch=1,
            grid=(t // tb, 1 + exp_steps + comb_steps),
            in_specs=[pl.BlockSpec((1, 1, N_EXPERTS * MOE_PIECES), lambda i, s, c: (i, 0, 0), memory_space=pltpu.SMEM),
                      pl.BlockSpec((tb, D_MODEL), lambda i, s, c: (i, 0)),
                      pl.BlockSpec((8, nsb, sbk), lambda i, s, c: (0, i, 0)),
                      pl.BlockSpec((eps, D_MODEL, D_EXPERT), wblk),
                      pl.BlockSpec((eps, D_MODEL, D_EXPERT), wblk),
                      pl.BlockSpec((eps, D_EXPERT, D_MODEL), wblk),
                      pl.BlockSpec((csb * sbk, LANES), oblk),
                      pl.BlockSpec((csb * sbk, D_MODEL), oblk),
                      pl.BlockSpec((1, D_MODEL), lambda i, s, c: (0, 0))],
            out_specs=pl.BlockSpec((csb * sbk, D_MODEL), oblk),
            scratch_shapes=[pltpu.VMEM((nsb, slots + 32, D_MODEL), BF16)]),
        out_shape=jax.ShapeDtypeStruct((t, D_MODEL), F32),
        compiler_params=pltpu.CompilerParams(dimension_semantics=("parallel", "arbitrary"),
                                             vmem_limit_bytes=MOE_VMEM_LIMIT),
        name="moe",
    )(tab, loc, h2, qrow3, wg, wu, wd, qcol, x1, gg2)


def _slot_tables(cnt, *, nsb, slots):
    tiles = (cnt + 15) // 16
    first = (jnp.cumsum(tiles, axis=1) - tiles) * 16
    tl = tiles.reshape(-1, nsb, N_EXPERTS)
    end = jnp.cumsum(tl, axis=1)
    beg = end - tl
    q = jnp.arange(MOE_PIECES, dtype=jnp.int32)[None, None, None, :]
    inside = jnp.logical_and(q >= beg[..., None], q < end[..., None])
    sb = jnp.arange(nsb, dtype=jnp.int32)[None, :, None, None]
    code = sb * MOE_LOC + first.reshape(-1, nsb, N_EXPERTS)[..., None] + (q - beg[..., None]) * 16
    loc = jnp.sum(jnp.where(inside, code, 0), axis=1)
    total = end[:, -1, :]
    dump = slots + 16 * (q[0, 0] % 2)
    loc = jnp.where(q[0] < total[..., None], loc, dump)
    return first, total.reshape(-1), loc.reshape(loc.shape[0], 1, -1)


def _pos_tables(rows):
    quarter = D_MODEL // 4
    freq = 1.0 / (POS_BASE ** (np.arange(quarter, dtype=np.float64) / quarter))
    r = np.arange(rows, dtype=np.float64)[:, None] * freq
    cl = np.arange(GRID_W, dtype=np.float64)[:, None] * freq
    er = np.concatenate([np.sin(r), np.cos(r)], axis=-1).astype(np.float32)
    ec = np.concatenate([np.sin(cl), np.cos(cl)], axis=-1).astype(np.float32)
    return jnp.asarray(er[:, None, :]), jnp.asarray(ec[None, :, :])


def _dft_tables(t):
    n = np.arange(CHUNK, dtype=np.int64)
    prod = n[:, None] * n[None, :]
    ang = (prod % CHUNK).astype(np.float64) * (2.0 * np.pi / CHUNK)
    c, s = np.cos(ang), np.sin(ang)
    f1 = np.concatenate([c, s], axis=0)
    a2 = np.concatenate([np.concatenate([c, -s], axis=1), np.concatenate([s, c], axis=1)], axis=0)
    cs = np.concatenate([c, -s], axis=0)
    angw = prod.astype(np.float64) * (2.0 * np.pi / t)
    f32 = lambda a: jnp.asarray(a.astype(np.float32))
    return f32(f1), f32(a2), f32(cs), f32(np.cos(angw)[:, :, None]), f32(np.sin(angw)[:, :, None])


def _blockdiag(w):
    n = w.shape[0]
    size = n * QKV_BLOCK
    spread = np.tile(np.eye(QKV_BLOCK, dtype=np.float32), (1, n))
    rows = jnp.dot(w.reshape(size, QKV_BLOCK), jnp.asarray(spread), precision=lax.Precision.HIGHEST)
    blk = np.arange(size) // QKV_BLOCK
    mask = (blk[:, None] == blk[None, :]).astype(np.float32)
    return rows * jnp.asarray(mask)


def _gate_weights(w_f, b_f, w_b, b_b):
    w = jnp.concatenate([w_f[:, :HEADS], w_b[:, :HEADS], w_f[:, HEADS:], w_b[:, HEADS:]], axis=1).T
    b = jnp.concatenate([b_f[:HEADS], b_b[:HEADS], b_f[HEADS:], b_b[HEADS:]])
    return w.astype(BF16), b[:, None]


def kernel(x, c, ctx, c_ctx, w_ada, b_ada, g_pre_mix, g_post_mix, g_pre_ffn, g_post_ffn,
           w_in, conv_w, conv_b, w_q, w_k, w_v, w_if_fwd, b_if_fwd, w_if_bwd, b_if_bwd,
           mlstm_norm_w, mlstm_skip, w_fourier, w_out, w_router_group, b_router_group,
           w_router_expert, b_router_expert, w_gate, w_up, w_down):
    t = x.shape[1]
    rows = t // GRID_W

    c8 = jnp.concatenate([c, c_ctx[None, :], jnp.zeros((6, D_MODEL), F32)], axis=0)
    mod = _ada(c8, w_ada[0], b_ada[0][None, :])
    shift1, scale1, gate1, shift2, scale2, gate2 = [mod[0:1, k * D_MODEL:(k + 1) * D_MODEL] for k in range(N_MOD)]
    shift1c, scale1c = mod[1:2, 0:D_MODEL], mod[1:2, D_MODEL:2 * D_MODEL]
    g1 = g_pre_mix[0][None, :] * (1.0 + scale1)
    g1c = g_pre_mix[0][None, :] * (1.0 + scale1c)
    gg1 = g_post_mix[0][None, :] * gate1
    g2 = g_pre_ffn[0][None, :] * (1.0 + scale2)
    gg2 = g_post_ffn[0][None, :] * gate2

    er3, ec3 = _pos_tables(rows)
    x3 = x.reshape(rows, GRID_W, D_MODEL)
    ctx3 = ctx.reshape(CTX_LEN // GRID_W, GRID_W, D_MODEL)
    w_in_bf = w_in[0].astype(BF16)

    xm_l, z_l, u_l = _inproj(x3, er3, ec3, g1, shift1, w_in_bf, rows=8, add_pos=True)
    xm_c, _, _ = _inproj(ctx3, er3, ec3, g1c, shift1c, w_in_bf, rows=CTX_LEN // GRID_W, add_pos=False)

    wq = _blockdiag(w_q[0]).astype(BF16)
    wkt = _blockdiag(w_k[0]).T.astype(BF16)
    wv = _blockdiag(w_v[0]).astype(BF16)
    wi, bi = _gate_weights(w_if_fwd[0], b_if_fwd[0], w_if_bwd[0], b_if_bwd[0])
    wiq, wik, wiv = wi[:, :D_MLSTM], wi[:, D_MLSTM:2 * D_MLSTM], wi[:, 2 * D_MLSTM:]
    cb = conv_b[0][None, :]
    f1, a2, cs, cw3, sw3 = _dft_tables(t)
    n1 = t // CHUNK
    q_l, kt_l, v_l, act_l, gp_l, yc, ys = _feat(
        xm_l, conv_w[0], cb, wq, wkt, wv, wiq, wik, wiv, bi, tm=t // (n1 // FFT_ROWS),
        fft=(u_l.reshape(FGROUPS, n1, CHUNK, FCG), f1.astype(BF16), cw3, sw3))
    q_c, kt_c, v_c, _, gp_c = _feat(xm_c, conv_w[0], cb, wq, wkt, wv, wiq, wik, wiv, bi, tm=CTX_LEN)

    dr_l, gc_l = _gates(gp_l, tl=2048)
    dr_c, gc_c = _gates(gp_c, tl=CTX_LEN)

    mix = jnp.einsum('kc,gcd->gkd', cs, w_fourier[0], precision=lax.Precision.HIGHEST)
    mix = (mix * float(1.0 / np.sqrt(float(t) * FCG))).astype(BF16)

    c0 = jnp.zeros((2 * HEADS, DH, 2 * DH), F32)
    m0 = jnp.zeros((2 * HEADS, 8, LANES), F32)
    kt_c, kt_l = kt_c.reshape(HEADS, DH, CTX_LEN), kt_l.reshape(HEADS, DH, t)
    c_ctx_fin, m_ctx_fin = _mlstm(q_c, kt_c, v_c, gc_c, dr_c, c0, m0, cps=CTX_LEN // CHUNK, emit=False)
    hf, hb, _, _, wg_bf, wu_bf, wd_bf, yf = _mlstm(
        q_l, kt_l, v_l, gc_l, dr_l, c_ctx_fin, m_ctx_fin, cps=8, emit=True,
        casts=(w_gate[0], w_up[0], w_down[0]),
        fft=(yc.reshape(FGROUPS, t, FCG), ys.reshape(FGROUPS, t, FCG), a2.astype(BF16), mix))
    yf = yf.reshape(FGROUPS, t, FCG)

    wr = jnp.concatenate([w_router_group[0], w_router_expert[0],
                          jnp.zeros((D_MODEL, LANES - N_GROUPS - N_EXPERTS), F32)], axis=1)
    br = jnp.concatenate([b_router_group[0], b_router_expert[0],
                          jnp.zeros((LANES - N_GROUPS - N_EXPERTS,), F32)])[None, :]
    x1, h2, lg = _merge(hf, hb, act_l, z_l, yf, x3, er3, ec3,
                        mlstm_norm_w[0][None, :], mlstm_skip[0][None, :], w_out[0].astype(BF16),
                        gg1, g2, shift2, wr.astype(BF16), br, rows=16)
    pos, wts, cnt = _route(lg, tl=MOE_TB, sbk=MOE_SBK)
    nblk, nsb = t // MOE_TB, MOE_TB // MOE_SBK
    cnt = cnt.reshape(nblk, N_EXPERTS, LANES)[:, :, :nsb]
    cnt = jnp.transpose(cnt, (0, 2, 1)).reshape(nblk * nsb, N_EXPERTS).astype(jnp.int32)
    first, total, loc = _slot_tables(cnt, nsb=nsb, slots=MOE_SLOTS)
    first_rows = jnp.repeat(first.T.astype(F32), MOE_SBK, axis=1)
    qrow, qcol = _slots(pos, wts, first_rows, tl=MOE_TB)
    out = _moe(total, loc, h2, qrow.reshape(8, t // MOE_SBK, MOE_SBK), wg_bf, wu_bf, wd_bf,
               qcol, x1, gg2, tb=MOE_TB, sbk=MOE_SBK, slots=MOE_SLOTS, eps=MOE_EPS, csb=MOE_CSB)
    return out[None]
```

```python
import functools

import numpy as np
import jax
import jax.numpy as jnp
from jax import lax
from jax.experimental import pallas as pl
from jax.experimental.pallas import tpu as pltpu

F32 = jnp.float32
BF16 = jnp.bfloat16

D_MODEL = 1024
SEQ = 16384
GRID_W = 64
CTX_LEN = 256
D_MLSTM = 512
HEADS = 4
DH = 128
QKV_BLOCK = 4
CONV_K = 3
CHUNK = 128
D_FOURIER = 512
FGROUPS = 4
FCG = 128
N_GROUPS = 4
EPG = 4
N_EXPERTS = 16
D_EXPERT = 512
N_MOD = 6
EPS = 1e-6
POS_BASE = 10000.0
LANES = 128
NEG_BIG = -3.0e38

VMEM_LIMIT = 58 * 1024 * 1024
MOE_VMEM_LIMIT = 58 * 1024 * 1024
MOE_TB = 2048
MOE_SBK = 256
MOE_SLOTS = 768
MOE_EPS = 4
MOE_CSB = 2
MOE_LOC = 1024
MOE_PIECES = MOE_TB // 16


def _cparams(*sem):
    return pltpu.CompilerParams(dimension_semantics=sem, vmem_limit_bytes=VMEM_LIMIT)


def _dot(a, b):
    return jnp.dot(a, b, preferred_element_type=F32)


def _dot_nt(a, b):
    return lax.dot_general(a, b, (((1,), (1,)), ((), ())), preferred_element_type=F32)


def _split_bf16(a):
    hi = a.astype(BF16)
    lo = (a - hi.astype(F32)).astype(BF16)
    return hi, lo


def _dot3(a, b):
    a_hi, a_lo = _split_bf16(a)
    b_hi, b_lo = _split_bf16(b)
    return _dot(a_hi, b_hi) + (_dot(a_hi, b_lo) + _dot(a_lo, b_hi))


def _sigmoid(x):
    return 1.0 / (1.0 + jnp.exp(-x))


def _rms(x):
    return x * lax.rsqrt(jnp.mean(x * x, axis=-1, keepdims=True) + EPS)


def _ada_kernel(c_ref, w_ref, b_ref, o_ref):
    c = c_ref[...]
    s = c * _sigmoid(c)
    o_ref[...] = _dot3(s, w_ref[...]) + b_ref[...]


def _ada(c8, w, b):
    n = w.shape[1]
    tn = 768
    return pl.pallas_call(
        _ada_kernel,
        grid=(n // tn,),
        in_specs=[pl.BlockSpec((8, D_MODEL), lambda j: (0, 0)),
                  pl.BlockSpec((D_MODEL, tn), lambda j: (0, j)),
                  pl.BlockSpec((1, tn), lambda j: (0, j))],
        out_specs=pl.BlockSpec((8, tn), lambda j: (0, j)),
        out_shape=jax.ShapeDtypeStruct((8, n), F32),
        compiler_params=_cparams("parallel"),
        name="ada",
    )(c8, w, b)


def _add_pos(x3, er_ref, ec_ref):
    r = x3.shape[0]
    pr = jnp.broadcast_to(er_ref[...], (r, GRID_W, D_MODEL // 2))
    pc = jnp.broadcast_to(ec_ref[...], (r, GRID_W, D_MODEL // 2))
    return x3 + jnp.concatenate([pr, pc], axis=-1)


def _inproj_kernel(x_ref, er_ref, ec_ref, g_ref, sh_ref, w_ref, xm_ref, z_ref, u_ref, *, add_pos):
    x3 = x_ref[...]
    if add_pos:
        x3 = _add_pos(x3, er_ref, ec_ref)
    x = x3.reshape(x3.shape[0] * GRID_W, D_MODEL)
    h = _rms(x) * g_ref[...] + sh_ref[...]
    proj = _dot(h.astype(BF16), w_ref[...])
    xm_ref[...] = proj[:, :D_MLSTM].astype(BF16)
    z_ref[...] = proj[:, D_MLSTM:2 * D_MLSTM].astype(BF16)
    for g in range(FGROUPS):
        u_ref[g] = proj[:, 2 * D_MLSTM + g * FCG:2 * D_MLSTM + (g + 1) * FCG]


def _inproj(x3, er3, ec3, g_eff, shift, w_in, *, rows, add_pos):
    nr = x3.shape[0]
    t = nr * GRID_W
    tm = rows * GRID_W
    out = jax.ShapeDtypeStruct((t, D_MLSTM), BF16)
    ospec = pl.BlockSpec((tm, D_MLSTM), lambda i: (i, 0))
    vec = pl.BlockSpec((1, D_MODEL), lambda i: (0, 0))
    return pl.pallas_call(
        functools.partial(_inproj_kernel, add_pos=add_pos),
        grid=(nr // rows,),
        in_specs=[pl.BlockSpec((rows, GRID_W, D_MODEL), lambda i: (i, 0, 0)),
                  pl.BlockSpec((rows, 1, D_MODEL // 2), lambda i: (i, 0, 0)),
                  pl.BlockSpec((1, GRID_W, D_MODEL // 2), lambda i: (0, 0, 0)),
                  vec, vec,
                  pl.BlockSpec(w_in.shape, lambda i: (0, 0))],
        out_specs=[ospec, ospec, pl.BlockSpec((FGROUPS, tm, FCG), lambda i: (0, i, 0))],
        out_shape=[out, out, jax.ShapeDtypeStruct((FGROUPS, t, FCG), F32)],
        compiler_params=_cparams("parallel"),
        name="inproj",
    )(x3, er3, ec3, g_eff, shift, w_in)


def _feat_kernel(xm_ref, prev_ref, next_ref, cw_ref, cb_ref, wq_ref, wkt_ref, wv_ref,
                 wiq_ref, wik_ref, wiv_ref, bi_ref, *rest):
    if len(rest) > 5:
        _dft1_kernel(*rest[:4], *rest[9:])
        rest = rest[4:9]
    q_ref, kt_ref, v_ref, act_ref, g_ref = rest
    i = pl.program_id(0)
    n = pl.num_programs(0)
    xm_bf = xm_ref[...]
    xm = xm_bf.astype(F32)
    tm = xm.shape[0]
    prev_row = prev_ref[...].astype(F32)[15:16, :] * jnp.where(i > 0, 1.0, 0.0)
    next_row = next_ref[...].astype(F32)[0:1, :] * jnp.where(i < n - 1, 1.0, 0.0)
    rid = lax.broadcasted_iota(jnp.int32, (tm, 1), 0)
    x_left = jnp.where(rid == 0, prev_row, pltpu.roll(xm, 1, 0))
    x_right = jnp.where(rid == tm - 1, next_row, pltpu.roll(xm, tm - 1, 0))
    cw = cw_ref[...]
    y = cw[0:1] * x_left + cw[1:2] * xm + cw[2:3] * x_right + cb_ref[...]
    act = (y * _sigmoid(y)).astype(BF16)
    act_ref[...] = act
    q = _dot(act, wq_ref[...])
    kt = _dot_nt(wkt_ref[...], act)
    v = _dot(xm_bf, wv_ref[...])
    q_bf = q.astype(BF16)
    kt_bf = kt.astype(BF16)
    v_bf = v.astype(BF16)
    q_s = (q * (DH ** -0.5)).astype(BF16)
    for hd in range(HEADS):
        q_ref[hd] = q_s[:, hd * DH:(hd + 1) * DH]
        v_ref[hd] = v_bf[:, hd * DH:(hd + 1) * DH]
    kt_ref[...] = kt_bf
    g = _dot_nt(wiq_ref[...], q_bf) + _dot(wik_ref[...], kt_bf) + _dot_nt(wiv_ref[...], v_bf)
    g_ref[...] = g + bi_ref[...]


def _feat(xm, conv_w, conv_b, wq, wkt, wv, wiq, wik, wiv, bi, *, tm, fft=()):
    t = xm.shape[0]
    nb16 = t // 16
    k16 = tm // 16
    full = lambda a: pl.BlockSpec(a.shape, lambda i: (0,) * a.ndim)
    tok = pl.BlockSpec((tm, D_MLSTM), lambda i: (i, 0))
    heads = pl.BlockSpec((HEADS, tm, DH), lambda i: (0, i, 0))
    in_specs = [tok,
                pl.BlockSpec((16, D_MLSTM), lambda i: (jnp.maximum(i * k16 - 1, 0), 0)),
                pl.BlockSpec((16, D_MLSTM), lambda i: (jnp.minimum((i + 1) * k16, nb16 - 1), 0)),
                full(conv_w), full(conv_b), full(wq), full(wkt), full(wv),
                full(wiq), full(wik), full(wiv), full(bi)]
    out_specs = [heads,
                 pl.BlockSpec((D_MLSTM, tm), lambda i: (0, i)),
                 heads, tok,
                 pl.BlockSpec((16, tm), lambda i: (0, i))]
    out_shape = [jax.ShapeDtypeStruct((HEADS, t, DH), BF16),
                 jax.ShapeDtypeStruct((D_MLSTM, t), BF16),
                 jax.ShapeDtypeStruct((HEADS, t, DH), BF16),
                 jax.ShapeDtypeStruct((t, D_MLSTM), BF16),
                 jax.ShapeDtypeStruct((16, t), F32)]
    if fft:
        u4, f1, cw3, sw3 = fft
        blk = pl.BlockSpec((FGROUPS, CHUNK, FFT_ROWS, FCG), lambda j: (0, 0, j, 0))
        tw = pl.BlockSpec((FFT_ROWS, CHUNK, 1), lambda j: (j, 0, 0))
        in_specs += [blk, full(f1), tw, tw]
        out_specs += [blk, blk]
        out_shape += [jax.ShapeDtypeStruct(u4.shape, F32)] * 2
    return pl.pallas_call(
        _feat_kernel,
        grid=(t // tm,),
        in_specs=in_specs,
        out_specs=out_specs,
        out_shape=out_shape,
        compiler_params=_cparams("parallel"),
        name="feat",
    )(xm, xm, xm, conv_w, conv_b, wq, wkt, wv, wiq, wik, wiv, bi, *fft)


def _gates_kernel(g_ref, d_ref, gc_ref):
    g = g_ref[...]
    tl = g.shape[1]
    ig = g[0:8]
    fg = g[8:16]
    lf = jnp.minimum(fg, 0.0) - jnp.log(1.0 + jnp.exp(-jnp.abs(fg)))
    pos = lax.broadcasted_iota(jnp.int32, (8, tl), 1) & (CHUNK - 1)
    is_fwd = lax.broadcasted_iota(jnp.int32, (8, tl), 0) < HEADS

    def scan(x, op, ident):
        xf = x
        xb = x
        k = 1
        while k < CHUNK:
            xf = op(xf, jnp.where(pos >= k, pltpu.roll(xf, k, 1), ident))
            xb = op(xb, jnp.where(pos < CHUNK - k, pltpu.roll(xb, tl - k, 1), ident))
            k *= 2
        return jnp.where(is_fwd, xf, xb)

    b = scan(lf, jnp.add, 0.0)
    d = ig - b
    mloc = b + scan(d, jnp.maximum, NEG_BIG)
    d_ref[...] = d
    stack = jnp.concatenate([b, mloc, jnp.zeros((LANES - 16, tl), F32)], axis=0)
    gc_ref[...] = stack.T


def _gates(g, *, tl):
    t = g.shape[1]
    return pl.pallas_call(
        _gates_kernel,
        grid=(t // tl,),
        in_specs=[pl.BlockSpec((16, tl), lambda i: (0, i))],
        out_specs=[pl.BlockSpec((8, tl), lambda i: (0, i)),
                   pl.BlockSpec((tl, LANES), lambda i: (i, 0))],
        out_shape=[jax.ShapeDtypeStruct((8, t), F32),
                   jax.ShapeDtypeStruct((t, LANES), F32)],
        compiler_params=_cparams("parallel"),
        name="gates",
    )(g)


def _mlstm_kernel(*refs, cps, emit, ncast, nfft=0):
    (qf_ref, kf_ref, vf_ref, gcf_ref, drf_ref,
     qb_ref, kb_ref, vb_ref, gcb_ref, drb_ref, c0_ref, m0_ref) = refs[:12]
    cast_in = refs[12:12 + ncast]
    fft_in = refs[12 + ncast:12 + ncast + nfft]
    outs = refs[12 + ncast + nfft:]
    if emit:
        hf_ref, hb_ref, cfin_ref, mfin_ref = outs[:4]
        outs = outs[4:]
    else:
        cfin_ref, mfin_ref = outs[:2]
        outs = outs[2:]
        hf_ref = hb_ref = None
    cast_out = outs[:ncast]
    fft_out = outs[ncast:ncast + (1 if nfft else 0)]
    c_scr, m_scr = outs[ncast + (1 if nfft else 0):]
    step = pl.program_id(0)

    @pl.when(step == 0)
    def _():
        c_scr[...] = c0_ref[...]
        m_scr[...] = m0_ref[...]

    ti = lax.broadcasted_iota(jnp.int32, (CHUNK, CHUNK), 0)
    si = lax.broadcasted_iota(jnp.int32, (CHUNK, CHUNK), 1)
    nch = 2 * HEADS
    mask = jnp.concatenate([jnp.broadcast_to((si <= ti)[None], (HEADS, CHUNK, CHUNK)),
                            jnp.broadcast_to((si >= ti)[None], (HEADS, CHUNK, CHUNK))], axis=0)
    ones_blk = jnp.ones((nch, CHUNK, CHUNK), BF16)
    lane8 = lax.broadcasted_iota(jnp.int32, (1, LANES), 1)

    def bdot(a, b):
        return lax.dot_general(a, b, (((2,), (1,)), ((0,), (0,))), preferred_element_type=F32)

    def both(f, g):
        return [f(r) for r in range(HEADS)] + [g(r) for r in range(HEADS, nch)]

    def chunk(j, carry):
        for src, dst in zip(cast_in, cast_out):
            share = src.shape[1] // cps
            rs = pl.ds(pl.multiple_of(j * share, share), share)
            dst[:, rs, :] = src[:, rs, :].astype(BF16)
        if nfft:
            _dft2_rows(j, *fft_in, *fft_out)

        rf = pl.ds(pl.multiple_of(j * CHUNK, CHUNK), CHUNK)
        rb = pl.ds(pl.multiple_of((cps - 1 - j) * CHUNK, CHUNK), CHUNK)
        gcf, gcb = gcf_ref[rf, :], gcb_ref[rb, :]
        drf, drb = drf_ref[:, rf], drb_ref[:, rb]
        lf, lb = CHUNK - 1, 0
        drow = jnp.stack(both(lambda r: drf[r:r + 1, :], lambda r: drb[r:r + 1, :]))
        btot = jnp.stack(both(lambda r: gcf[lf:lf + 1, r:r + 1], lambda r: gcb[lb:lb + 1, r:r + 1]))
        amax = jnp.stack(both(lambda r: gcf[lf:lf + 1, 8 + r:9 + r], lambda r: gcb[lb:lb + 1, 8 + r:9 + r]))
        m0 = jnp.stack([m_scr[r][0:1, 0:1] for r in range(nch)])
        c_aug = c_scr[...]
        kt = jnp.concatenate([kf_ref[:, :, rf], kb_ref[:, :, rb]], axis=0)
        vaug = jnp.concatenate([jnp.concatenate([vf_ref[:, rf, :], vb_ref[:, rb, :]], axis=0), ones_blk], axis=2)
        if emit:
            qh = jnp.concatenate([qf_ref[:, rf, :], qb_ref[:, rb, :]], axis=0)
            m0_lanes = sum(jnp.where(lane8 == r, m_scr[r][0:1, :], 0.0) for r in range(nch))
            rel_l, eneg_l = [], []
            for gc in (gcf, gcb):
                m_t = jnp.maximum(gc + m0_lanes, pltpu.roll(gc, LANES - 8, 1))
                rel_l.append(gc - m_t)
                eneg_l.append(jnp.exp(-m_t))
            pick = lambda x, r: jnp.take_along_axis(x, jnp.full((CHUNK, LANES), r, jnp.int32), axis=1)
            rel = jnp.stack(both(lambda r: pick(rel_l[0], r), lambda r: pick(rel_l[1], r)))
            eneg = jnp.stack(both(lambda r: pick(eneg_l[0], r), lambda r: pick(eneg_l[1], r)))
            dmat = jnp.where(mask, jnp.exp(rel + drow), 0.0)
            smat = (bdot(qh, kt) * dmat).astype(BF16)
            q_in = (qh.astype(F32) * jnp.exp(rel + m0)).astype(BF16)
            num = bdot(jnp.concatenate([smat, q_in], axis=2),
                       jnp.concatenate([vaug, c_aug.astype(BF16)], axis=1))
            den = jnp.maximum(jnp.abs(num[:, :, DH:]), eneg)
            h = (num[:, :, :DH] / den).astype(hf_ref.dtype)
            hf_ref[:, rf, :] = h[:HEADS]
            hb_ref[:, rb, :] = h[HEADS:]
        m_new = jnp.maximum(btot + m0, amax)
        decay = jnp.exp(btot + m0 - m_new)
        kw = (kt.astype(F32) * jnp.exp(btot + drow - m_new)).astype(BF16)
        c_scr[...] = decay * c_aug + bdot(kw, vaug)
        m_scr[...] = jnp.broadcast_to(m_new, (nch, 8, LANES))
        return carry

    lax.fori_loop(0, cps, chunk, 0)

    @pl.when(step == pl.num_programs(0) - 1)
    def _():
        cfin_ref[...] = c_scr[...]
        mfin_ref[...] = m_scr[...]


def _expert_slot(e):
    return (e % EPG) * N_GROUPS + e // EPG


def _mlstm(q, kt, v, gc, dr, c0, m0, *, cps, emit, casts=(), fft=()):
    t = q.shape[1]
    cb = cps * CHUNK
    nb = t // cb
    fwd_r = lambda i: (i, 0)
    bwd_r = lambda i: (nb - 1 - i, 0)
    fwd_c = lambda i: (0, i)
    bwd_c = lambda i: (0, nb - 1 - i)
    tok = lambda f: pl.BlockSpec((HEADS, cb, DH), lambda i, f=f: (0, f(i)[0], 0))
    in_specs = []
    for fr, fc in ((fwd_r, fwd_c), (bwd_r, bwd_c)):
        in_specs += [tok(fr), pl.BlockSpec((HEADS, DH, cb), lambda i, fc=fc: (0, 0, fc(i)[1])), tok(fr),
                     pl.BlockSpec((cb, LANES), fr), pl.BlockSpec((8, cb), fc)]
    cshape = (2 * HEADS, DH, 2 * DH)
    mshape = (2 * HEADS, 8, LANES)
    cspec = pl.BlockSpec(cshape, lambda i: (0, 0, 0))
    mspec = pl.BlockSpec(mshape, lambda i: (0, 0, 0))
    in_specs += [cspec, mspec]
    out_specs = [cspec, mspec]
    out_shape = [jax.ShapeDtypeStruct(cshape, F32), jax.ShapeDtypeStruct(mshape, F32)]
    if emit:
        out_specs = [tok(fwd_r), tok(bwd_r)] + out_specs
        out_shape = [jax.ShapeDtypeStruct((HEADS, t, DH), BF16)] * 2 + out_shape
    for a in casts:
        per = nb // a.shape[0]
        blk = (1, a.shape[1] // per, a.shape[2])
        in_specs.append(pl.BlockSpec(blk, lambda i, per=per: (i // per, i % per, 0)))
        out_specs.append(pl.BlockSpec(blk, lambda i, per=per: (_expert_slot(i // per), i % per, 0)))
        out_shape.append(jax.ShapeDtypeStruct(a.shape, BF16))
    if fft:
        yc, ys, a2, mix = fft
        tokf = pl.BlockSpec((FGROUPS, FFT_ROWS * CHUNK, FCG), lambda i: (0, i, 0))
        fullf = lambda a: pl.BlockSpec(a.shape, lambda i: (0,) * a.ndim)
        in_specs += [tokf, tokf, fullf(a2), fullf(mix)]
        out_specs.append(pl.BlockSpec((FGROUPS, CHUNK, FFT_ROWS, FCG), lambda i: (0, 0, i, 0)))
        out_shape.append(jax.ShapeDtypeStruct((FGROUPS, CHUNK, t // CHUNK, FCG), F32))
    return pl.pallas_call(
        functools.partial(_mlstm_kernel, cps=cps, emit=emit, ncast=len(casts), nfft=len(fft)),
        grid=(nb,),
        in_specs=in_specs,
        out_specs=out_specs,
        out_shape=out_shape,
        scratch_shapes=[pltpu.VMEM(cshape, F32), pltpu.VMEM(mshape, F32)],
        compiler_params=_cparams("arbitrary"),
        name="mlstm",
    )(q, kt, v, gc, dr, q, kt, v, gc, dr, c0, m0, *casts, *fft)


FFT_ROWS = 8


def _dft1_kernel(u_ref, f_ref, cw_ref, sw_ref, yc_ref, ys_ref):
    f = f_ref[...]
    rows = CHUNK * FFT_ROWS
    u2 = u_ref.reshape(FGROUPS * rows, FCG)
    yc2 = yc_ref.reshape(FGROUPS * rows, FCG)
    ys2 = ys_ref.reshape(FGROUPS * rows, FCG)
    for s in range(FFT_ROWS):
        pick = [pl.ds(g * rows + s, CHUNK, stride=FFT_ROWS) for g in range(FGROUPS)]
        x = jnp.concatenate([u2[p, :] for p in pick], axis=1).astype(BF16)
        y = _dot(f, x)
        cw = cw_ref[s]
        sw = sw_ref[s]
        pr = y[:CHUNK] * cw - y[CHUNK:] * sw
        pi = y[:CHUNK] * sw + y[CHUNK:] * cw
        for g in range(FGROUPS):
            yc2[pick[g], :] = pr[:, g * FCG:(g + 1) * FCG]
            ys2[pick[g], :] = pi[:, g * FCG:(g + 1) * FCG]


def _dft2_rows(kk, yc_ref, ys_ref, a2_ref, mix_ref, o_ref):
    rows = CHUNK * FFT_ROWS
    o2 = o_ref.reshape(FGROUPS * rows, FCG)
    blk = pl.ds(pl.multiple_of(kk * CHUNK, CHUNK), CHUNK)
    yc = jnp.concatenate([yc_ref[g, blk, :] for g in range(FGROUPS)], axis=1)
    ys = jnp.concatenate([ys_ref[g, blk, :] for g in range(FGROUPS)], axis=1)
    p = jnp.concatenate([yc, ys], axis=0).astype(BF16)
    x = _dot(a2_ref[...], p).astype(BF16)
    for g in range(FGROUPS):
        cols = slice(g * FCG, (g + 1) * FCG)
        cat = jnp.concatenate([x[:CHUNK, cols], x[CHUNK:, cols]], axis=1)
        o2[pl.ds(g * rows + kk, CHUNK, stride=FFT_ROWS), :] = _dot(cat, mix_ref[g])


def _merge_kernel(hf_ref, hb_ref, act_ref, z_ref, yf_ref, x_ref, er_ref, ec_ref,
                  nw_ref, sk_ref, wout_ref, gg1_ref, g2_ref, sh2_ref, wr_ref, br_ref,
                  x1_ref, h2_ref, lg_ref):
    parts = []
    for hd in range(HEADS):
        hh = hf_ref[hd].astype(F32) + hb_ref[hd].astype(F32)
        dl = hh - jnp.mean(hh, axis=-1, keepdims=True)
        var = jnp.mean(dl * dl, axis=-1, keepdims=True)
        parts.append(dl * lax.rsqrt(var + EPS))
    hn = jnp.concatenate(parts, axis=-1)
    z = z_ref[...].astype(F32)
    m = (hn * nw_ref[...] + sk_ref[...] * act_ref[...].astype(F32)) * (z * _sigmoid(z))
    cat = jnp.concatenate([m.astype(BF16)] + [yf_ref[g].astype(BF16) for g in range(FGROUPS)], axis=-1)
    y = _dot(cat, wout_ref[...])
    x3 = _add_pos(x_ref[...], er_ref, ec_ref)
    xp = x3.reshape(x3.shape[0] * GRID_W, D_MODEL)
    x1 = xp + _rms(y) * gg1_ref[...]
    x1_ref[...] = x1
    h2 = _rms(x1) * g2_ref[...] + sh2_ref[...]
    h2_ref[...] = h2.astype(BF16)
    lg = _dot(h2.astype(BF16), wr_ref[...]) + br_ref[...]
    lg_ref[...] = lg.T[:32]


def _merge(hf, hb, act, z, yf, x3, er3, ec3, nw, sk, wout, gg1, g2, sh2, wr, br, *, rows):
    nr = x3.shape[0]
    t = nr * GRID_W
    tm = rows * GRID_W
    tok = pl.BlockSpec((tm, D_MLSTM), lambda i: (i, 0))
    heads = pl.BlockSpec((HEADS, tm, DH), lambda i: (0, i, 0))
    full = lambda a: pl.BlockSpec(a.shape, lambda i: (0,) * a.ndim)
    return pl.pallas_call(
        _merge_kernel,
        grid=(nr // rows,),
        in_specs=[heads, heads, tok, tok, heads,
                  pl.BlockSpec((rows, GRID_W, D_MODEL), lambda i: (i, 0, 0)),
                  pl.BlockSpec((rows, 1, D_MODEL // 2), lambda i: (i, 0, 0)),
                  pl.BlockSpec((1, GRID_W, D_MODEL // 2), lambda i: (0, 0, 0)),
                  full(nw), full(sk), full(wout), full(gg1), full(g2), full(sh2), full(wr), full(br)],
        out_specs=[pl.BlockSpec((tm, D_MODEL), lambda i: (i, 0)),
                   pl.BlockSpec((tm, D_MODEL), lambda i: (i, 0)),
                   pl.BlockSpec((32, tm), lambda i: (0, i))],
        out_shape=[jax.ShapeDtypeStruct((t, D_MODEL), F32),
                   jax.ShapeDtypeStruct((t, D_MODEL), BF16),
                   jax.ShapeDtypeStruct((32, t), F32)],
        compiler_params=_cparams("parallel"),
        name="merge",
    )(hf, hb, act, z, yf, x3, er3, ec3, nw, sk, wout, gg1, g2, sh2, wr, br)


def _route_kernel(lg_ref, pos_ref, w_ref, cnt_ref, *, sbk):
    lg = lg_ref[...]
    tl = lg.shape[1]
    g = [lg[j:j + 1] for j in range(N_GROUPS)]
    e = [lg[N_GROUPS + j:N_GROUPS + j + 1] for j in range(N_EXPERTS)]
    gmax = jnp.maximum(jnp.maximum(g[0], g[1]), jnp.maximum(g[2], g[3]))
    den = jnp.exp(g[0] - gmax) + jnp.exp(g[1] - gmax) + jnp.exp(g[2] - gmax) + jnp.exp(g[3] - gmax)
    p_sel = 1.0 / den
    sel = []
    free = jnp.ones((1, tl), F32)
    for j in range(N_GROUPS):
        s = jnp.where(g[j] >= gmax, free, 0.0)
        sel.append(s)
        free = free - s
    es = []
    for j in range(EPG):
        es.append(sel[0] * e[j] + sel[1] * e[EPG + j] + sel[2] * e[2 * EPG + j] + sel[3] * e[3 * EPG + j])
    rank = []
    for j in range(EPG):
        rj = jnp.zeros((1, tl), F32)
        for i in range(EPG):
            if i == j:
                continue
            beats = (es[i] >= es[j]) if i < j else (es[i] > es[j])
            rj = rj + jnp.where(beats, 1.0, 0.0)
        rank.append(rj)
    v1 = jnp.maximum(jnp.maximum(es[0], es[1]), jnp.maximum(es[2], es[3]))
    v2 = sum(jnp.where(rank[j] == 1.0, es[j], 0.0) for j in range(EPG))
    tt = jnp.exp(v2 - v1)
    w1 = p_sel / (1.0 + tt)
    w2 = w1 * tt
    w = [jnp.where(rank[j] == 0.0, w1, jnp.where(rank[j] == 1.0, w2, 0.0)) for j in range(EPG)]
    top2 = [jnp.where(rank[j] < 2.0, 1.0, 0.0) for j in range(EPG)]
    mem = jnp.concatenate([sel[gi] * top2[j] for gi in range(N_GROUPS) for j in range(EPG)], axis=0)
    wts = jnp.concatenate([sel[gi] * w[j] for gi in range(N_GROUPS) for j in range(EPG)], axis=0)
    w_ref[...] = wts
    lane = lax.broadcasted_iota(jnp.int32, (N_EXPERTS, tl), 1) & (sbk - 1)
    c = mem
    k = 1
    while k < sbk:
        c = c + jnp.where(lane >= k, pltpu.roll(c, k, 1), 0.0)
        k *= 2
    pos_ref[...] = jnp.where(mem > 0.0, c - 1.0, -1.0)
    lane128 = lax.broadcasted_iota(jnp.int32, (N_EXPERTS, LANES), 1)
    cnt = jnp.zeros((N_EXPERTS, LANES), F32)
    for kb in range(tl // sbk):
        tot = jnp.sum(mem[:, kb * sbk:(kb + 1) * sbk], axis=1, keepdims=True)
        cnt = cnt + jnp.where(lane128 == kb, tot, 0.0)
    cnt_ref[...] = cnt


def _route(lg, *, tl, sbk):
    t = lg.shape[1]
    row = pl.BlockSpec((N_EXPERTS, tl), lambda i: (0, i))
    return pl.pallas_call(
        functools.partial(_route_kernel, sbk=sbk),
        grid=(t // tl,),
        in_specs=[pl.BlockSpec((32, tl), lambda i: (0, i))],
        out_specs=[row, row, pl.BlockSpec((N_EXPERTS, LANES), lambda i: (i, 0))],
        out_shape=[jax.ShapeDtypeStruct((N_EXPERTS, t), F32),
                   jax.ShapeDtypeStruct((N_EXPERTS, t), F32),
                   jax.ShapeDtypeStruct((t // tl * N_EXPERTS, LANES), F32)],
        compiler_params=_cparams("parallel"),
        name="route",
    )(lg)


def _slots_kernel(pos_ref, w_ref, first_ref, qrow_ref, qcol_ref):
    tl = pos_ref.shape[1]
    q0 = jnp.full((1, tl), -1.0, F32)
    q1 = jnp.full((1, tl), -1.0, F32)
    w0 = jnp.zeros((1, tl), F32)
    w1 = jnp.zeros((1, tl), F32)
    seen = jnp.zeros((1, tl), F32)
    for ex in range(N_EXPERTS):
        rk = pos_ref[ex:ex + 1, :]
        wt = w_ref[ex:ex + 1, :]
        m = jnp.where(rk >= 0.0, 1.0, 0.0)
        val = rk + first_ref[ex:ex + 1, :]
        first = (m * (1.0 - seen)) > 0.0
        second = (m * seen) > 0.0
        q0 = jnp.where(first, val, q0)
        w0 = jnp.where(first, wt, w0)
        q1 = jnp.where(second, val, q1)
        w1 = jnp.where(second, wt, w1)
        seen = seen + m
    qrow_ref[...] = jnp.concatenate([q0, q1, jnp.zeros((6, tl), F32)], axis=0)
    qcol_ref[...] = jnp.concatenate([q0, q1, w0, w1, jnp.zeros((LANES - 4, tl), F32)], axis=0).T


def _slots(pos, w, first, *, tl):
    t = pos.shape[1]
    row = pl.BlockSpec((N_EXPERTS, tl), lambda i: (0, i))
    return pl.pallas_call(
        _slots_kernel,
        grid=(t // tl,),
        in_specs=[row, row, row],
        out_specs=[pl.BlockSpec((8, tl), lambda i: (0, i)),
                   pl.BlockSpec((tl, LANES), lambda i: (i, 0))],
        out_shape=[jax.ShapeDtypeStruct((8, t), F32),
                   jax.ShapeDtypeStruct((t, LANES), F32)],
        compiler_params=_cparams("parallel"),
        name="slots",
    )(pos, w, first)


def _mlp(x, wg, wu, wd):
    gt = _dot(x, wg)
    a = ((gt * _sigmoid(gt)) * _dot(x, wu)).astype(BF16)
    return _dot(a, wd).astype(BF16)


def _moe_kernel(tab_ref, loc_ref, h_ref, qrow_ref, wg_ref, wu_ref, wd_ref,
                qcol_ref, x1_ref, gg2_ref, o_ref, xs_ref, *, sbk, slots, csb):
    i = pl.program_id(0)
    step = pl.program_id(1)
    nsb = h_ref.shape[0] // sbk
    eps = wg_ref.shape[0]
    exp_steps = N_EXPERTS // eps

    @pl.when(step == 0)
    def _():
        pid = lax.broadcasted_iota(jnp.int32, (slots, sbk), 0).astype(F32)

        def select(sb, carry):
            q0 = qrow_ref[0, pl.ds(sb, 1), :]
            q1 = qrow_ref[1, pl.ds(sb, 1), :]
            s = jnp.where(q0 == pid, 1.0, jnp.where(q1 == pid, 1.0, 0.0)).astype(BF16)
            row0 = pl.multiple_of(sb * sbk, sbk)
            xs_ref[sb, 0:slots, :] = _dot(s, h_ref[pl.ds(row0, sbk), :]).astype(BF16)
            xs_ref[sb, slots:slots + 32, :] = jnp.zeros((32, D_MODEL), BF16)
            return carry

        lax.fori_loop(0, nsb, select, 0)

    def run_pieces(k, base, n_pieces):
        locs = []
        for m in range(n_pieces):
            d = loc_ref[0, 0, base + m]
            locs.append((d // MOE_LOC, pl.multiple_of(d % MOE_LOC, 16)))
        x = jnp.concatenate([xs_ref[sb, pl.ds(off, 16), :] for sb, off in locs], axis=0)
        y = _mlp(x, wg_ref[k], wu_ref[k], wd_ref[k])
        for m, (sb, off) in enumerate(locs):
            xs_ref[sb, pl.ds(off, 16), :] = y[m * 16:(m + 1) * 16]

    @pl.when(jnp.logical_and(step >= 1, step <= exp_steps))
    def _():
        def expert(k, carry):
            e = _expert_slot((step - 1) * eps + k)
            total = tab_ref[i * N_EXPERTS + e]
            base = e * MOE_PIECES

            def full(t, c1):
                run_pieces(k, base + t * 32, 32)
                return c1

            n_full = (total + 3) // 32
            lax.fori_loop(0, n_full, full, 0)
            done = n_full * 32
            rest = total - done

            for lo, n in ((0, 8), (8, 12), (12, 16), (16, 20), (20, 24), (24, 28)):
                @pl.when(jnp.logical_and(rest > lo, rest <= n))
                def _():
                    run_pieces(k, base + done, n)

            return carry

        lax.fori_loop(0, eps, expert, 0)

    @pl.when(step > exp_steps)
    def _():
        lane = lax.broadcasted_iota(jnp.int32, (sbk, slots), 1).astype(F32)
        for k in range(csb):
            sb = (step - 1 - exp_steps) * csb + k
            rows = slice(k * sbk, (k + 1) * sbk)
            qc = qcol_ref[rows, :]
            wmat = (jnp.where(lane == qc[:, 0:1], qc[:, 2:3], 0.0)
                    + jnp.where(lane == qc[:, 1:2], qc[:, 3:4], 0.0)).astype(BF16)
            y = _dot(wmat, xs_ref[sb, 0:slots, :])
            o_ref[rows, :] = x1_ref[rows, :] + _rms(y) * gg2_ref[...]


def _moe(tab, loc, h2, qrow3, wg, wu, wd, qcol, x1, gg2, *, tb, sbk, slots, eps, csb):
    t = h2.shape[0]
    nsb = tb // sbk
    comb_steps = nsb // csb
    exp_steps = N_EXPERTS // eps
    wblk = lambda i, s, c: (jnp.where(s == 0, exp_steps - 1, jnp.minimum(s - 1, exp_steps - 1)), 0, 0)
    oblk = lambda i, s, c: (i * comb_steps + jnp.maximum(s - 1 - exp_steps, 0), 0)
    return pl.pallas_call(
        functools.partial(_moe_kernel, sbk=sbk, slots=slots, csb=csb),
        grid_spec=pltpu.PrefetchScalarGridSpec(
            num_scalar_prefetch=1,
            grid=(t // tb, 1 + exp_steps + comb_steps),
            in_specs=[pl.BlockSpec((1, 1, N_EXPERTS * MOE_PIECES), lambda i, s, c: (i, 0, 0), memory_space=pltpu.SMEM),
                      pl.BlockSpec((tb, D_MODEL), lambda i, s, c: (i, 0)),
                      pl.BlockSpec((8, nsb, sbk), lambda i, s, c: (0, i, 0)),
                      pl.BlockSpec((eps, D_MODEL, D_EXPERT), wblk),
                      pl.BlockSpec((eps, D_MODEL, D_EXPERT), wblk),
                      pl.BlockSpec((eps, D_EXPERT, D_MODEL), wblk),
                      pl.BlockSpec((csb * sbk, LANES), oblk),
                      pl.BlockSpec((csb * sbk, D_MODEL), oblk),
                      pl.BlockSpec((1, D_MODEL), lambda i, s, c: (0, 0))],
            out_specs=pl.BlockSpec((csb * sbk, D_MODEL), oblk),
            scratch_shapes=[pltpu.VMEM((nsb, slots + 32, D_MODEL), BF16)]),
        out_shape=jax.ShapeDtypeStruct((t, D_MODEL), F32),
        compiler_params=pltpu.CompilerParams(dimension_semantics=("parallel", "arbitrary"),
                                             vmem_limit_bytes=MOE_VMEM_LIMIT),
        name="moe",
    )(tab, loc, h2, qrow3, wg, wu, wd, qcol, x1, gg2)


def _slot_tables(cnt, *, nsb, slots):
    tiles = (cnt + 15) // 16
    first = (jnp.cumsum(tiles, axis=1) - tiles) * 16
    tl = tiles.reshape(-1, nsb, N_EXPERTS)
    end = jnp.cumsum(tl, axis=1)
    beg = end - tl
    q = jnp.arange(MOE_PIECES, dtype=jnp.int32)[None, None, None, :]
    inside = jnp.logical_and(q >= beg[..., None], q < end[..., None])
    sb = jnp.arange(nsb, dtype=jnp.int32)[None, :, None, None]
    code = sb * MOE_LOC + first.reshape(-1, nsb, N_EXPERTS)[..., None] + (q - beg[..., None]) * 16
    loc = jnp.sum(jnp.where(inside, code, 0), axis=1)
    total = end[:, -1, :]
    dump = slots + 16 * (q[0, 0] % 2)
    loc = jnp.where(q[0] < total[..., None], loc, dump)
    return first, total.reshape(-1), loc.reshape(loc.shape[0], 1, -1)


def _pos_tables(rows):
    quarter = D_MODEL // 4
    freq = 1.0 / (POS_BASE ** (np.arange(quarter, dtype=np.float64) / quarter))
    r = np.arange(rows, dtype=np.float64)[:, None] * freq
    cl = np.arange(GRID_W, dtype=np.float64)[:, None] * freq
    er = np.concatenate([np.sin(r), np.cos(r)], axis=-1).astype(np.float32)
    ec = np.concatenate([np.sin(cl), np.cos(cl)], axis=-1).astype(np.float32)
    return jnp.asarray(er[:, None, :]), jnp.asarray(ec[None, :, :])


def _dft_tables(t):
    n = np.arange(CHUNK, dtype=np.int64)
    prod = n[:, None] * n[None, :]
    ang = (prod % CHUNK).astype(np.float64) * (2.0 * np.pi / CHUNK)
    c, s = np.cos(ang), np.sin(ang)
    f1 = np.concatenate([c, s], axis=0)
    a2 = np.concatenate([np.concatenate([c, -s], axis=1), np.concatenate([s, c], axis=1)], axis=0)
    cs = np.concatenate([c, -s], axis=0)
    angw = prod.astype(np.float64) * (2.0 * np.pi / t)
    f32 = lambda a: jnp.asarray(a.astype(np.float32))
    return f32(f1), f32(a2), f32(cs), f32(np.cos(angw)[:, :, None]), f32(np.sin(angw)[:, :, None])


def _blockdiag(w):
    n = w.shape[0]
    size = n * QKV_BLOCK
    spread = np.tile(np.eye(QKV_BLOCK, dtype=np.float32), (1, n))
    rows = jnp.dot(w.reshape(size, QKV_BLOCK), jnp.asarray(spread), precision=lax.Precision.HIGHEST)
    blk = np.arange(size) // QKV_BLOCK
    mask = (blk[:, None] == blk[None, :]).astype(np.float32)
    return rows * jnp.asarray(mask)


def _gate_weights(w_f, b_f, w_b, b_b):
    w = jnp.concatenate([w_f[:, :HEADS], w_b[:, :HEADS], w_f[:, HEADS:], w_b[:, HEADS:]], axis=1).T
    b = jnp.concatenate([b_f[:HEADS], b_b[:HEADS], b_f[HEADS:], b_b[HEADS:]])
    return w.astype(BF16), b[:, None]


def kernel(x, c, ctx, c_ctx, w_ada, b_ada, g_pre_mix, g_post_mix, g_pre_ffn, g_post_ffn,
           w_in, conv_w, conv_b, w_q, w_k, w_v, w_if_fwd, b_if_fwd, w_if_bwd, b_if_bwd,
           mlstm_norm_w, mlstm_skip, w_fourier, w_out, w_router_group, b_router_group,
           w_router_expert, b_router_expert, w_gate, w_up, w_down):
    t = x.shape[1]
    rows = t // GRID_W

    c8 = jnp.concatenate([c, c_ctx[None, :], jnp.zeros((6, D_MODEL), F32)], axis=0)
    mod = _ada(c8, w_ada[0], b_ada[0][None, :])
    shift1, scale1, gate1, shift2, scale2, gate2 = [mod[0:1, k * D_MODEL:(k + 1) * D_MODEL] for k in range(N_MOD)]
    shift1c, scale1c = mod[1:2, 0:D_MODEL], mod[1:2, D_MODEL:2 * D_MODEL]
    g1 = g_pre_mix[0][None, :] * (1.0 + scale1)
    g1c = g_pre_mix[0][None, :] * (1.0 + scale1c)
    gg1 = g_post_mix[0][None, :] * gate1
    g2 = g_pre_ffn[0][None, :] * (1.0 + scale2)
    gg2 = g_post_ffn[0][None, :] * gate2

    er3, ec3 = _pos_tables(rows)
    x3 = x.reshape(rows, GRID_W, D_MODEL)
    ctx3 = ctx.reshape(CTX_LEN // GRID_W, GRID_W, D_MODEL)
    w_in_bf = w_in[0].astype(BF16)

    xm_l, z_l, u_l = _inproj(x3, er3, ec3, g1, shift1, w_in_bf, rows=8, add_pos=True)
    xm_c, _, _ = _inproj(ctx3, er3, ec3, g1c, shift1c, w_in_bf, rows=CTX_LEN // GRID_W, add_pos=False)

    wq = _blockdiag(w_q[0]).astype(BF16)
    wkt = _blockdiag(w_k[0]).T.astype(BF16)
    wv = _blockdiag(w_v[0]).astype(BF16)
    wi, bi = _gate_weights(w_if_fwd[0], b_if_fwd[0], w_if_bwd[0], b_if_bwd[0])
    wiq, wik, wiv = wi[:, :D_MLSTM], wi[:, D_MLSTM:2 * D_MLSTM], wi[:, 2 * D_MLSTM:]
    cb = conv_b[0][None, :]
    f1, a2, cs, cw3, sw3 = _dft_tables(t)
    n1 = t // CHUNK
    q_l, kt_l, v_l, act_l, gp_l, yc, ys = _feat(
        xm_l, conv_w[0], cb, wq, wkt, wv, wiq, wik, wiv, bi, tm=t // (n1 // FFT_ROWS),
        fft=(u_l.reshape(FGROUPS, n1, CHUNK, FCG), f1.astype(BF16), cw3, sw3))
    q_c, kt_c, v_c, _, gp_c = _feat(xm_c, conv_w[0], cb, wq, wkt, wv, wiq, wik, wiv, bi, tm=CTX_LEN)

    dr_l, gc_l = _gates(gp_l, tl=2048)
    dr_c, gc_c = _gates(gp_c, tl=CTX_LEN)

    mix = jnp.einsum('kc,gcd->gkd', cs, w_fourier[0], precision=lax.Precision.HIGHEST)
    mix = (mix * float(1.0 / np.sqrt(float(t) * FCG))).astype(BF16)

    c0 = jnp.zeros((2 * HEADS, DH, 2 * DH), F32)
    m0 = jnp.zeros((2 * HEADS, 8, LANES), F32)
    kt_c, kt_l = kt_c.reshape(HEADS, DH, CTX_LEN), kt_l.reshape(HEADS, DH, t)
    c_ctx_fin, m_ctx_fin = _mlstm(q_c, kt_c, v_c, gc_c, dr_c, c0, m0, cps=CTX_LEN // CHUNK, emit=False)
    hf, hb, _, _, wg_bf, wu_bf, wd_bf, yf = _mlstm(
        q_l, kt_l, v_l, gc_l, dr_l, c_ctx_fin, m_ctx_fin, cps=8, emit=True,
        casts=(w_gate[0], w_up[0], w_down[0]),
        fft=(yc.reshape(FGROUPS, t, FCG), ys.reshape(FGROUPS, t, FCG), a2.astype(BF16), mix))
    yf = yf.reshape(FGROUPS, t, FCG)

    wr = jnp.concatenate([w_router_group[0], w_router_expert[0],
                          jnp.zeros((D_MODEL, LANES - N_GROUPS - N_EXPERTS), F32)], axis=1)
    br = jnp.concatenate([b_router_group[0], b_router_expert[0],
                          jnp.zeros((LANES - N_GROUPS - N_EXPERTS,), F32)])[None, :]
    x1, h2, lg = _merge(hf, hb, act_l, z_l, yf, x3, er3, ec3,
                        mlstm_norm_w[0][None, :], mlstm_skip[0][None, :], w_out[0].astype(BF16),
                        gg1, g2, shift2, wr.astype(BF16), br, rows=16)
    pos, wts, cnt = _route(lg, tl=MOE_TB, sbk=MOE_SBK)
    nblk, nsb = t // MOE_TB, MOE_TB // MOE_SBK
    cnt = cnt.reshape(nblk, N_EXPERTS, LANES)[:, :, :nsb]
    cnt = jnp.transpose(cnt, (0, 2, 1)).reshape(nblk * nsb, N_EXPERTS).astype(jnp.int32)
    first, total, loc = _slot_tables(cnt, nsb=nsb, slots=MOE_SLOTS)
    first_rows = jnp.repeat(first.T.astype(F32), MOE_SBK, axis=1)
    qrow, qcol = _slots(pos, wts, first_rows, tl=MOE_TB)
    out = _moe(total, loc, h2, qrow.reshape(8, t // MOE_SBK, MOE_SBK), wg_bf, wu_bf, wd_bf,
               qcol, x1, gg2, tb=MOE_TB, sbk=MOE_SBK, slots=MOE_SLOTS, eps=MOE_EPS, csb=MOE_CSB)
    return out[None]
```

```python
import functools

import numpy as np
import jax
import jax.numpy as jnp
from jax import lax
from jax.experimental import pallas as pl
from jax.experimental.pallas import tpu as pltpu

F32 = jnp.float32
BF16 = jnp.bfloat16

D_MODEL = 1024
SEQ = 16384
GRID_W = 64
CTX_LEN = 256
D_MLSTM = 512
HEADS = 4
DH = 128
QKV_BLOCK = 4
CONV_K = 3
CHUNK = 128
D_FOURIER = 512
FGROUPS = 4
FCG = 128
N_GROUPS = 4
EPG = 4
N_EXPERTS = 16
D_EXPERT = 512
N_MOD = 6
EPS = 1e-6
POS_BASE = 10000.0
LANES = 128
NEG_BIG = -3.0e38

VMEM_LIMIT = 58 * 1024 * 1024
MOE_VMEM_LIMIT = 58 * 1024 * 1024
MOE_TB = 2048
MOE_SBK = 256
MOE_SLOTS = 768
MOE_EPS = 4
MOE_CSB = 2
MOE_LOC = 1024
MOE_PIECES = MOE_TB // 16


def _cparams(*sem):
    return pltpu.CompilerParams(dimension_semantics=sem, vmem_limit_bytes=VMEM_LIMIT)


def _dot(a, b):
    return jnp.dot(a, b, preferred_element_type=F32)


def _dot_nt(a, b):
    return lax.dot_general(a, b, (((1,), (1,)), ((), ())), preferred_element_type=F32)


def _split_bf16(a):
    hi = a.astype(BF16)
    lo = (a - hi.astype(F32)).astype(BF16)
    return hi, lo


def _dot3(a, b):
    a_hi, a_lo = _split_bf16(a)
    b_hi, b_lo = _split_bf16(b)
    return _dot(a_hi, b_hi) + (_dot(a_hi, b_lo) + _dot(a_lo, b_hi))


def _sigmoid(x):
    return 1.0 / (1.0 + jnp.exp(-x))


def _rms(x):
    return x * lax.rsqrt(jnp.mean(x * x, axis=-1, keepdims=True) + EPS)


def _ada_kernel(c_ref, w_ref, b_ref, o_ref):
    c = c_ref[...]
    s = c * _sigmoid(c)
    o_ref[...] = _dot3(s, w_ref[...]) + b_ref[...]


def _ada(c8, w, b):
    n = w.shape[1]
    tn = 768
    return pl.pallas_call(
        _ada_kernel,
        grid=(n // tn,),
        in_specs=[pl.BlockSpec((8, D_MODEL), lambda j: (0, 0)),
                  pl.BlockSpec((D_MODEL, tn), lambda j: (0, j)),
                  pl.BlockSpec((1, tn), lambda j: (0, j))],
        out_specs=pl.BlockSpec((8, tn), lambda j: (0, j)),
        out_shape=jax.ShapeDtypeStruct((8, n), F32),
        compiler_params=_cparams("parallel"),
        name="ada",
    )(c8, w, b)


def _add_pos(x3, er_ref, ec_ref):
    r = x3.shape[0]
    pr = jnp.broadcast_to(er_ref[...], (r, GRID_W, D_MODEL // 2))
    pc = jnp.broadcast_to(ec_ref[...], (r, GRID_W, D_MODEL // 2))
    return x3 + jnp.concatenate([pr, pc], axis=-1)


def _inproj_kernel(x_ref, er_ref, ec_ref, g_ref, sh_ref, w_ref, xm_ref, z_ref, u_ref, *, add_pos):
    x3 = x_ref[...]
    if add_pos:
        x3 = _add_pos(x3, er_ref, ec_ref)
    x = x3.reshape(x3.shape[0] * GRID_W, D_MODEL)
    h = _rms(x) * g_ref[...] + sh_ref[...]
    proj = _dot(h.astype(BF16), w_ref[...])
    xm_ref[...] = proj[:, :D_MLSTM].astype(BF16)
    z_ref[...] = proj[:, D_MLSTM:2 * D_MLSTM].astype(BF16)
    for g in range(FGROUPS):
        u_ref[g] = proj[:, 2 * D_MLSTM + g * FCG:2 * D_MLSTM + (g + 1) * FCG]


def _inproj(x3, er3, ec3, g_eff, shift, w_in, *, rows, add_pos):
    nr = x3.shape[0]
    t = nr * GRID_W
    tm = rows * GRID_W
    out = jax.ShapeDtypeStruct((t, D_MLSTM), BF16)
    ospec = pl.BlockSpec((tm, D_MLSTM), lambda i: (i, 0))
    vec = pl.BlockSpec((1, D_MODEL), lambda i: (0, 0))
    return pl.pallas_call(
        functools.partial(_inproj_kernel, add_pos=add_pos),
        grid=(nr // rows,),
        in_specs=[pl.BlockSpec((rows, GRID_W, D_MODEL), lambda i: (i, 0, 0)),
                  pl.BlockSpec((rows, 1, D_MODEL // 2), lambda i: (i, 0, 0)),
                  pl.BlockSpec((1, GRID_W, D_MODEL // 2), lambda i: (0, 0, 0)),
                  vec, vec,
                  pl.BlockSpec(w_in.shape, lambda i: (0, 0))],
        out_specs=[ospec, ospec, pl.BlockSpec((FGROUPS, tm, FCG), lambda i: (0, i, 0))],
        out_shape=[out, out, jax.ShapeDtypeStruct((FGROUPS, t, FCG), F32)],
        compiler_params=_cparams("parallel"),
        name="inproj",
    )(x3, er3, ec3, g_eff, shift, w_in)


def _feat_kernel(xm_ref, prev_ref, next_ref, cw_ref, cb_ref, wq_ref, wkt_ref, wv_ref,
                 wiq_ref, wik_ref, wiv_ref, bi_ref, *rest):
    if len(rest) > 5:
        _dft1_kernel(*rest[:4], *rest[9:])
        rest = rest[4:9]
    q_ref, kt_ref, v_ref, act_ref, g_ref = rest
    i = pl.program_id(0)
    n = pl.num_programs(0)
    xm_bf = xm_ref[...]
    xm = xm_bf.astype(F32)
    tm = xm.shape[0]
    prev_row = prev_ref[...].astype(F32)[15:16, :] * jnp.where(i > 0, 1.0, 0.0)
    next_row = next_ref[...].astype(F32)[0:1, :] * jnp.where(i < n - 1, 1.0, 0.0)
    rid = lax.broadcasted_iota(jnp.int32, (tm, 1), 0)
    x_left = jnp.where(rid == 0, prev_row, pltpu.roll(xm, 1, 0))
    x_right = jnp.where(rid == tm - 1, next_row, pltpu.roll(xm, tm - 1, 0))
    cw = cw_ref[...]
    y = cw[0:1] * x_left + cw[1:2] * xm + cw[2:3] * x_right + cb_ref[...]
    act = (y * _sigmoid(y)).astype(BF16)
    act_ref[...] = act
    q = _dot(act, wq_ref[...])
    kt = _dot_nt(wkt_ref[...], act)
    v = _dot(xm_bf, wv_ref[...])
    q_bf = q.astype(BF16)
    kt_bf = kt.astype(BF16)
    v_bf = v.astype(BF16)
    q_s = (q * (DH ** -0.5)).astype(BF16)
    for hd in range(HEADS):
        q_ref[hd] = q_s[:, hd * DH:(hd + 1) * DH]
        v_ref[hd] = v_bf[:, hd * DH:(hd + 1) * DH]
    kt_ref[...] = kt_bf
    g = _dot_nt(wiq_ref[...], q_bf) + _dot(wik_ref[...], kt_bf) + _dot_nt(wiv_ref[...], v_bf)
    g_ref[...] = g + bi_ref[...]


def _feat(xm, conv_w, conv_b, wq, wkt, wv, wiq, wik, wiv, bi, *, tm, fft=()):
    t = xm.shape[0]
    nb16 = t // 16
    k16 = tm // 16
    full = lambda a: pl.BlockSpec(a.shape, lambda i: (0,) * a.ndim)
    tok = pl.BlockSpec((tm, D_MLSTM), lambda i: (i, 0))
    heads = pl.BlockSpec((HEADS, tm, DH), lambda i: (0, i, 0))
    in_specs = [tok,
                pl.BlockSpec((16, D_MLSTM), lambda i: (jnp.maximum(i * k16 - 1, 0), 0)),
                pl.BlockSpec((16, D_MLSTM), lambda i: (jnp.minimum((i + 1) * k16, nb16 - 1), 0)),
                full(conv_w), full(conv_b), full(wq), full(wkt), full(wv),
                full(wiq), full(wik), full(wiv), full(bi)]
    out_specs = [heads,
                 pl.BlockSpec((D_MLSTM, tm), lambda i: (0, i)),
                 heads, tok,
                 pl.BlockSpec((16, tm), lambda i: (0, i))]
    out_shape = [jax.ShapeDtypeStruct((HEADS, t, DH), BF16),
                 jax.ShapeDtypeStruct((D_MLSTM, t), BF16),
                 jax.ShapeDtypeStruct((HEADS, t, DH), BF16),
                 jax.ShapeDtypeStruct((t, D_MLSTM), BF16),
                 jax.ShapeDtypeStruct((16, t), F32)]
    if fft:
        u4, f1, cw3, sw3 = fft
        blk = pl.BlockSpec((FGROUPS, CHUNK, FFT_ROWS, FCG), lambda j: (0, 0, j, 0))
        tw = pl.BlockSpec((FFT_ROWS, CHUNK, 1), lambda j: (j, 0, 0))
        in_specs += [blk, full(f1), tw, tw]
        out_specs += [blk, blk]
        out_shape += [jax.ShapeDtypeStruct(u4.shape, F32)] * 2
    return pl.pallas_call(
        _feat_kernel,
        grid=(t // tm,),
        in_specs=in_specs,
        out_specs=out_specs,
        out_shape=out_shape,
        compiler_params=_cparams("parallel"),
        name="feat",
    )(xm, xm, xm, conv_w, conv_b, wq, wkt, wv, wiq, wik, wiv, bi, *fft)


def _gates_kernel(g_ref, d_ref, gc_ref):
    g = g_ref[...]
    tl = g.shape[1]
    ig = g[0:8]
    fg = g[8:16]
    lf = jnp.minimum(fg, 0.0) - jnp.log(1.0 + jnp.exp(-jnp.abs(fg)))
    pos = lax.broadcasted_iota(jnp.int32, (8, tl), 1) & (CHUNK - 1)
    is_fwd = lax.broadcasted_iota(jnp.int32, (8, tl), 0) < HEADS

    def scan(x, op, ident):
        xf = x
        xb = x
        k = 1
        while k < CHUNK:
            xf = op(xf, jnp.where(pos >= k, pltpu.roll(xf, k, 1), ident))
            xb = op(xb, jnp.where(pos < CHUNK - k, pltpu.roll(xb, tl - k, 1), ident))
            k *= 2
        return jnp.where(is_fwd, xf, xb)

    b = scan(lf, jnp.add, 0.0)
    d = ig - b
    mloc = b + scan(d, jnp.maximum, NEG_BIG)
    d_ref[...] = d
    stack = jnp.concatenate([b, mloc, jnp.zeros((LANES - 16, tl), F32)], axis=0)
    gc_ref[...] = stack.T


def _gates(g, *, tl):
    t = g.shape[1]
    return pl.pallas_call(
        _gates_kernel,
        grid=(t // tl,),
        in_specs=[pl.BlockSpec((16, tl), lambda i: (0, i))],
        out_specs=[pl.BlockSpec((8, tl), lambda i: (0, i)),
                   pl.BlockSpec((tl, LANES), lambda i: (i, 0))],
        out_shape=[jax.ShapeDtypeStruct((8, t), F32),
                   jax.ShapeDtypeStruct((t, LANES), F32)],
        compiler_params=_cparams("parallel"),
        name="gates",
    )(g)


def _mlstm_kernel(*refs, cps, emit, ncast, nfft=0):
    (qf_ref, kf_ref, vf_ref, gcf_ref, drf_ref,
     qb_ref, kb_ref, vb_ref, gcb_ref, drb_ref, c0_ref, m0_ref) = refs[:12]
    cast_in = refs[12:12 + ncast]
    fft_in = refs[12 + ncast:12 + ncast + nfft]
    outs = refs[12 + ncast + nfft:]
    if emit:
        hf_ref, hb_ref, cfin_ref, mfin_ref = outs[:4]
        outs = outs[4:]
    else:
        cfin_ref, mfin_ref = outs[:2]
        outs = outs[2:]
        hf_ref = hb_ref = None
    cast_out = outs[:ncast]
    fft_out = outs[ncast:ncast + (1 if nfft else 0)]
    c_scr, m_scr = outs[ncast + (1 if nfft else 0):]
    step = pl.program_id(0)

    for src, dst in zip(cast_in, cast_out):
        dst[...] = src[...].astype(BF16)
    if nfft:
        _dft2_kernel(*fft_in, *fft_out)

    @pl.when(step == 0)
    def _():
        c_scr[...] = c0_ref[...]
        m_scr[...] = m0_ref[...]

    ti = lax.broadcasted_iota(jnp.int32, (CHUNK, CHUNK), 0)
    si = lax.broadcasted_iota(jnp.int32, (CHUNK, CHUNK), 1)
    nch = 2 * HEADS
    mask = jnp.concatenate([jnp.broadcast_to((si <= ti)[None], (HEADS, CHUNK, CHUNK)),
                            jnp.broadcast_to((si >= ti)[None], (HEADS, CHUNK, CHUNK))], axis=0)
    ones_blk = jnp.ones((nch, CHUNK, CHUNK), BF16)
    lane8 = lax.broadcasted_iota(jnp.int32, (1, LANES), 1)

    def bdot(a, b):
        return lax.dot_general(a, b, (((2,), (1,)), ((0,), (0,))), preferred_element_type=F32)

    def both(f, g):
        return [f(r) for r in range(HEADS)] + [g(r) for r in range(HEADS, nch)]

    for j in range(cps):
        rf = slice(j * CHUNK, (j + 1) * CHUNK)
        rb = slice((cps - 1 - j) * CHUNK, (cps - j) * CHUNK)
        gcf, gcb = gcf_ref[rf, :], gcb_ref[rb, :]
        drf, drb = drf_ref[:, rf], drb_ref[:, rb]
        lf, lb = CHUNK - 1, 0
        drow = jnp.stack(both(lambda r: drf[r:r + 1, :], lambda r: drb[r:r + 1, :]))
        btot = jnp.stack(both(lambda r: gcf[lf:lf + 1, r:r + 1], lambda r: gcb[lb:lb + 1, r:r + 1]))
        amax = jnp.stack(both(lambda r: gcf[lf:lf + 1, 8 + r:9 + r], lambda r: gcb[lb:lb + 1, 8 + r:9 + r]))
        m0 = jnp.stack([m_scr[r][0:1, 0:1] for r in range(nch)])
        c_aug = c_scr[...]
        kt = jnp.concatenate([kf_ref[:, :, rf], kb_ref[:, :, rb]], axis=0)
        vaug = jnp.concatenate([jnp.concatenate([vf_ref[:, rf, :], vb_ref[:, rb, :]], axis=0), ones_blk], axis=2)
        if emit:
            qh = jnp.concatenate([qf_ref[:, rf, :], qb_ref[:, rb, :]], axis=0)
            m0_lanes = sum(jnp.where(lane8 == r, m_scr[r][0:1, :], 0.0) for r in range(nch))
            rel_l, eneg_l = [], []
            for gc in (gcf, gcb):
                m_t = jnp.maximum(gc + m0_lanes, pltpu.roll(gc, LANES - 8, 1))
                rel_l.append(gc - m_t)
                eneg_l.append(jnp.exp(-m_t))
            pick = lambda x, r: jnp.take_along_axis(x, jnp.full((CHUNK, LANES), r, jnp.int32), axis=1)
            rel = jnp.stack(both(lambda r: pick(rel_l[0], r), lambda r: pick(rel_l[1], r)))
            eneg = jnp.stack(both(lambda r: pick(eneg_l[0], r), lambda r: pick(eneg_l[1], r)))
            dmat = jnp.where(mask, jnp.exp(rel + drow), 0.0)
            smat = (bdot(qh, kt) * dmat).astype(BF16)
            q_in = (qh.astype(F32) * jnp.exp(rel + m0)).astype(BF16)
            num = bdot(jnp.concatenate([smat, q_in], axis=2),
                       jnp.concatenate([vaug, c_aug.astype(BF16)], axis=1))
            den = jnp.maximum(jnp.abs(num[:, :, DH:]), eneg)
            h = (num[:, :, :DH] / den).astype(hf_ref.dtype)
            hf_ref[:, rf, :] = h[:HEADS]
            hb_ref[:, rb, :] = h[HEADS:]
        m_new = jnp.maximum(btot + m0, amax)
        decay = jnp.exp(btot + m0 - m_new)
        kw = (kt.astype(F32) * jnp.exp(btot + drow - m_new)).astype(BF16)
        c_scr[...] = decay * c_aug + bdot(kw, vaug)
        m_scr[...] = jnp.broadcast_to(m_new, (nch, 8, LANES))

    @pl.when(step == pl.num_programs(0) - 1)
    def _():
        cfin_ref[...] = c_scr[...]
        mfin_ref[...] = m_scr[...]


def _expert_slot(e):
    return (e % EPG) * N_GROUPS + e // EPG


def _mlstm(q, kt, v, gc, dr, c0, m0, *, cps, emit, casts=(), fft=()):
    t = q.shape[1]
    cb = cps * CHUNK
    nb = t // cb
    fwd_r = lambda i: (i, 0)
    bwd_r = lambda i: (nb - 1 - i, 0)
    fwd_c = lambda i: (0, i)
    bwd_c = lambda i: (0, nb - 1 - i)
    tok = lambda f: pl.BlockSpec((HEADS, cb, DH), lambda i, f=f: (0, f(i)[0], 0))
    in_specs = []
    for fr, fc in ((fwd_r, fwd_c), (bwd_r, bwd_c)):
        in_specs += [tok(fr), pl.BlockSpec((HEADS, DH, cb), lambda i, fc=fc: (0, 0, fc(i)[1])), tok(fr),
                     pl.BlockSpec((cb, LANES), fr), pl.BlockSpec((8, cb), fc)]
    cshape = (2 * HEADS, DH, 2 * DH)
    mshape = (2 * HEADS, 8, LANES)
    cspec = pl.BlockSpec(cshape, lambda i: (0, 0, 0))
    mspec = pl.BlockSpec(mshape, lambda i: (0, 0, 0))
    in_specs += [cspec, mspec]
    out_specs = [cspec, mspec]
    out_shape = [jax.ShapeDtypeStruct(cshape, F32), jax.ShapeDtypeStruct(mshape, F32)]
    if emit:
        out_specs = [tok(fwd_r), tok(bwd_r)] + out_specs
        out_shape = [jax.ShapeDtypeStruct((HEADS, t, DH), BF16)] * 2 + out_shape
    for a in casts:
        per = nb // a.shape[0]
        blk = (1, a.shape[1] // per, a.shape[2])
        in_specs.append(pl.BlockSpec(blk, lambda i, per=per: (i // per, i % per, 0)))
        out_specs.append(pl.BlockSpec(blk, lambda i, per=per: (_expert_slot(i // per), i % per, 0)))
        out_shape.append(jax.ShapeDtypeStruct(a.shape, BF16))
    if fft:
        yc, ys, a2, mix = fft
        tokf = pl.BlockSpec((FGROUPS, FFT_ROWS * CHUNK, FCG), lambda i: (0, i, 0))
        fullf = lambda a: pl.BlockSpec(a.shape, lambda i: (0,) * a.ndim)
        in_specs += [tokf, tokf, fullf(a2), fullf(mix)]
        out_specs.append(pl.BlockSpec((FGROUPS, CHUNK, FFT_ROWS, FCG), lambda i: (0, 0, i, 0)))
        out_shape.append(jax.ShapeDtypeStruct((FGROUPS, CHUNK, t // CHUNK, FCG), F32))
    return pl.pallas_call(
        functools.partial(_mlstm_kernel, cps=cps, emit=emit, ncast=len(casts), nfft=len(fft)),
        grid=(nb,),
        in_specs=in_specs,
        out_specs=out_specs,
        out_shape=out_shape,
        scratch_shapes=[pltpu.VMEM(cshape, F32), pltpu.VMEM(mshape, F32)],
        compiler_params=_cparams("arbitrary"),
        name="mlstm",
    )(q, kt, v, gc, dr, q, kt, v, gc, dr, c0, m0, *casts, *fft)


FFT_ROWS = 8


def _dft1_kernel(u_ref, f_ref, cw_ref, sw_ref, yc_ref, ys_ref):
    f = f_ref[...]
    rows = CHUNK * FFT_ROWS
    u2 = u_ref.reshape(FGROUPS * rows, FCG)
    yc2 = yc_ref.reshape(FGROUPS * rows, FCG)
    ys2 = ys_ref.reshape(FGROUPS * rows, FCG)
    for s in range(FFT_ROWS):
        pick = [pl.ds(g * rows + s, CHUNK, stride=FFT_ROWS) for g in range(FGROUPS)]
        x = jnp.concatenate([u2[p, :] for p in pick], axis=1).astype(BF16)
        y = _dot(f, x)
        cw = cw_ref[s]
        sw = sw_ref[s]
        pr = y[:CHUNK] * cw - y[CHUNK:] * sw
        pi = y[:CHUNK] * sw + y[CHUNK:] * cw
        for g in range(FGROUPS):
            yc2[pick[g], :] = pr[:, g * FCG:(g + 1) * FCG]
            ys2[pick[g], :] = pi[:, g * FCG:(g + 1) * FCG]


def _dft2_kernel(yc_ref, ys_ref, a2_ref, mix_ref, o_ref):
    a2 = a2_ref[...]
    rows = CHUNK * FFT_ROWS
    o2 = o_ref.reshape(FGROUPS * rows, FCG)
    for kk in range(FFT_ROWS):
        blk = slice(kk * CHUNK, (kk + 1) * CHUNK)
        yc = jnp.concatenate([yc_ref[g, blk, :] for g in range(FGROUPS)], axis=1)
        ys = jnp.concatenate([ys_ref[g, blk, :] for g in range(FGROUPS)], axis=1)
        p = jnp.concatenate([yc, ys], axis=0).astype(BF16)
        x = _dot(a2, p).astype(BF16)
        for g in range(FGROUPS):
            cols = slice(g * FCG, (g + 1) * FCG)
            cat = jnp.concatenate([x[:CHUNK, cols], x[CHUNK:, cols]], axis=1)
            o2[pl.ds(g * rows + kk, CHUNK, stride=FFT_ROWS), :] = _dot(cat, mix_ref[g])


def _merge_kernel(hf_ref, hb_ref, act_ref, z_ref, yf_ref, x_ref, er_ref, ec_ref,
                  nw_ref, sk_ref, wout_ref, gg1_ref, g2_ref, sh2_ref, wr_ref, br_ref,
                  x1_ref, h2_ref, lg_ref):
    parts = []
    for hd in range(HEADS):
        hh = hf_ref[hd].astype(F32) + hb_ref[hd].astype(F32)
        dl = hh - jnp.mean(hh, axis=-1, keepdims=True)
        var = jnp.mean(dl * dl, axis=-1, keepdims=True)
        parts.append(dl * lax.rsqrt(var + EPS))
    hn = jnp.concatenate(parts, axis=-1)
    z = z_ref[...].astype(F32)
    m = (hn * nw_ref[...] + sk_ref[...] * act_ref[...].astype(F32)) * (z * _sigmoid(z))
    cat = jnp.concatenate([m.astype(BF16)] + [yf_ref[g].astype(BF16) for g in range(FGROUPS)], axis=-1)
    y = _dot(cat, wout_ref[...])
    x3 = _add_pos(x_ref[...], er_ref, ec_ref)
    xp = x3.reshape(x3.shape[0] * GRID_W, D_MODEL)
    x1 = xp + _rms(y) * gg1_ref[...]
    x1_ref[...] = x1
    h2 = _rms(x1) * g2_ref[...] + sh2_ref[...]
    h2_ref[...] = h2.astype(BF16)
    lg = _dot(h2.astype(BF16), wr_ref[...]) + br_ref[...]
    lg_ref[...] = lg.T[:32]


def _merge(hf, hb, act, z, yf, x3, er3, ec3, nw, sk, wout, gg1, g2, sh2, wr, br, *, rows):
    nr = x3.shape[0]
    t = nr * GRID_W
    tm = rows * GRID_W
    tok = pl.BlockSpec((tm, D_MLSTM), lambda i: (i, 0))
    heads = pl.BlockSpec((HEADS, tm, DH), lambda i: (0, i, 0))
    full = lambda a: pl.BlockSpec(a.shape, lambda i: (0,) * a.ndim)
    return pl.pallas_call(
        _merge_kernel,
        grid=(nr // rows,),
        in_specs=[heads, heads, tok, tok, heads,
                  pl.BlockSpec((rows, GRID_W, D_MODEL), lambda i: (i, 0, 0)),
                  pl.BlockSpec((rows, 1, D_MODEL // 2), lambda i: (i, 0, 0)),
                  pl.BlockSpec((1, GRID_W, D_MODEL // 2), lambda i: (0, 0, 0)),
                  full(nw), full(sk), full(wout), full(gg1), full(g2), full(sh2), full(wr), full(br)],
        out_specs=[pl.BlockSpec((tm, D_MODEL), lambda i: (i, 0)),
                   pl.BlockSpec((tm, D_MODEL), lambda i: (i, 0)),
                   pl.BlockSpec((32, tm), lambda i: (0, i))],
        out_shape=[jax.ShapeDtypeStruct((t, D_MODEL), F32),
                   jax.ShapeDtypeStruct((t, D_MODEL), BF16),
                   jax.ShapeDtypeStruct((32, t), F32)],
        compiler_params=_cparams("parallel"),
        name="merge",
    )(hf, hb, act, z, yf, x3, er3, ec3, nw, sk, wout, gg1, g2, sh2, wr, br)


def _route_kernel(lg_ref, pos_ref, w_ref, cnt_ref, *, sbk):
    lg = lg_ref[...]
    tl = lg.shape[1]
    g = [lg[j:j + 1] for j in range(N_GROUPS)]
    e = [lg[N_GROUPS + j:N_GROUPS + j + 1] for j in range(N_EXPERTS)]
    gmax = jnp.maximum(jnp.maximum(g[0], g[1]), jnp.maximum(g[2], g[3]))
    den = jnp.exp(g[0] - gmax) + jnp.exp(g[1] - gmax) + jnp.exp(g[2] - gmax) + jnp.exp(g[3] - gmax)
    p_sel = 1.0 / den
    sel = []
    free = jnp.ones((1, tl), F32)
    for j in range(N_GROUPS):
        s = jnp.where(g[j] >= gmax, free, 0.0)
        sel.append(s)
        free = free - s
    es = []
    for j in range(EPG):
        es.append(sel[0] * e[j] + sel[1] * e[EPG + j] + sel[2] * e[2 * EPG + j] + sel[3] * e[3 * EPG + j])
    rank = []
    for j in range(EPG):
        rj = jnp.zeros((1, tl), F32)
        for i in range(EPG):
            if i == j:
                continue
            beats = (es[i] >= es[j]) if i < j else (es[i] > es[j])
            rj = rj + jnp.where(beats, 1.0, 0.0)
        rank.append(rj)
    v1 = jnp.maximum(jnp.maximum(es[0], es[1]), jnp.maximum(es[2], es[3]))
    v2 = sum(jnp.where(rank[j] == 1.0, es[j], 0.0) for j in range(EPG))
    tt = jnp.exp(v2 - v1)
    w1 = p_sel / (1.0 + tt)
    w2 = w1 * tt
    w = [jnp.where(rank[j] == 0.0, w1, jnp.where(rank[j] == 1.0, w2, 0.0)) for j in range(EPG)]
    top2 = [jnp.where(rank[j] < 2.0, 1.0, 0.0) for j in range(EPG)]
    mem = jnp.concatenate([sel[gi] * top2[j] for gi in range(N_GROUPS) for j in range(EPG)], axis=0)
    wts = jnp.concatenate([sel[gi] * w[j] for gi in range(N_GROUPS) for j in range(EPG)], axis=0)
    w_ref[...] = wts
    lane = lax.broadcasted_iota(jnp.int32, (N_EXPERTS, tl), 1) & (sbk - 1)
    c = mem
    k = 1
    while k < sbk:
        c = c + jnp.where(lane >= k, pltpu.roll(c, k, 1), 0.0)
        k *= 2
    pos_ref[...] = jnp.where(mem > 0.0, c - 1.0, -1.0)
    lane128 = lax.broadcasted_iota(jnp.int32, (N_EXPERTS, LANES), 1)
    cnt = jnp.zeros((N_EXPERTS, LANES), F32)
    for kb in range(tl // sbk):
        tot = jnp.sum(mem[:, kb * sbk:(kb + 1) * sbk], axis=1, keepdims=True)
        cnt = cnt + jnp.where(lane128 == kb, tot, 0.0)
    cnt_ref[...] = cnt


def _route(lg, *, tl, sbk):
    t = lg.shape[1]
    row = pl.BlockSpec((N_EXPERTS, tl), lambda i: (0, i))
    return pl.pallas_call(
        functools.partial(_route_kernel, sbk=sbk),
        grid=(t // tl,),
        in_specs=[pl.BlockSpec((32, tl), lambda i: (0, i))],
        out_specs=[row, row, pl.BlockSpec((N_EXPERTS, LANES), lambda i: (i, 0))],
        out_shape=[jax.ShapeDtypeStruct((N_EXPERTS, t), F32),
                   jax.ShapeDtypeStruct((N_EXPERTS, t), F32),
                   jax.ShapeDtypeStruct((t // tl * N_EXPERTS, LANES), F32)],
        compiler_params=_cparams("parallel"),
        name="route",
    )(lg)


def _slots_kernel(pos_ref, w_ref, first_ref, qrow_ref, qcol_ref):
    tl = pos_ref.shape[1]
    q0 = jnp.full((1, tl), -1.0, F32)
    q1 = jnp.full((1, tl), -1.0, F32)
    w0 = jnp.zeros((1, tl), F32)
    w1 = jnp.zeros((1, tl), F32)
    seen = jnp.zeros((1, tl), F32)
    for ex in range(N_EXPERTS):
        rk = pos_ref[ex:ex + 1, :]
        wt = w_ref[ex:ex + 1, :]
        m = jnp.where(rk >= 0.0, 1.0, 0.0)
        val = rk + first_ref[ex:ex + 1, :]
        first = (m * (1.0 - seen)) > 0.0
        second = (m * seen) > 0.0
        q0 = jnp.where(first, val, q0)
        w0 = jnp.where(first, wt, w0)
        q1 = jnp.where(second, val, q1)
        w1 = jnp.where(second, wt, w1)
        seen = seen + m
    qrow_ref[...] = jnp.concatenate([q0, q1, jnp.zeros((6, tl), F32)], axis=0)
    qcol_ref[...] = jnp.concatenate([q0, q1, w0, w1, jnp.zeros((LANES - 4, tl), F32)], axis=0).T


def _slots(pos, w, first, *, tl):
    t = pos.shape[1]
    row = pl.BlockSpec((N_EXPERTS, tl), lambda i: (0, i))
    return pl.pallas_call(
        _slots_kernel,
        grid=(t // tl,),
        in_specs=[row, row, row],
        out_specs=[pl.BlockSpec((8, tl), lambda i: (0, i)),
                   pl.BlockSpec((tl, LANES), lambda i: (i, 0))],
        out_shape=[jax.ShapeDtypeStruct((8, t), F32),
                   jax.ShapeDtypeStruct((t, LANES), F32)],
        compiler_params=_cparams("parallel"),
        name="slots",
    )(pos, w, first)


def _mlp(x, wg, wu, wd):
    gt = _dot(x, wg)
    a = ((gt * _sigmoid(gt)) * _dot(x, wu)).astype(BF16)
    return _dot(a, wd).astype(BF16)


def _moe_kernel(tab_ref, loc_ref, h_ref, qrow_ref, wg_ref, wu_ref, wd_ref,
                qcol_ref, x1_ref, gg2_ref, o_ref, xs_ref, *, sbk, slots, csb):
    i = pl.program_id(0)
    step = pl.program_id(1)
    nsb = h_ref.shape[0] // sbk
    eps = wg_ref.shape[0]
    exp_steps = N_EXPERTS // eps

    @pl.when(step == 0)
    def _():
        pid = lax.broadcasted_iota(jnp.int32, (slots, sbk), 0).astype(F32)

        def select(sb, carry):
            q0 = qrow_ref[0, pl.ds(sb, 1), :]
            q1 = qrow_ref[1, pl.ds(sb, 1), :]
            s = jnp.where(q0 == pid, 1.0, jnp.where(q1 == pid, 1.0, 0.0)).astype(BF16)
            row0 = pl.multiple_of(sb * sbk, sbk)
            xs_ref[sb, 0:slots, :] = _dot(s, h_ref[pl.ds(row0, sbk), :]).astype(BF16)
            xs_ref[sb, slots:slots + 32, :] = jnp.zeros((32, D_MODEL), BF16)
            return carry

        lax.fori_loop(0, nsb, select, 0)

    def run_pieces(k, base, n_pieces):
        locs = []
        for m in range(n_pieces):
            d = loc_ref[0, 0, base + m]
            locs.append((d // MOE_LOC, pl.multiple_of(d % MOE_LOC, 16)))
        x = jnp.concatenate([xs_ref[sb, pl.ds(off, 16), :] for sb, off in locs], axis=0)
        y = _mlp(x, wg_ref[k], wu_ref[k], wd_ref[k])
        for m, (sb, off) in enumerate(locs):
            xs_ref[sb, pl.ds(off, 16), :] = y[m * 16:(m + 1) * 16]

    @pl.when(jnp.logical_and(step >= 1, step <= exp_steps))
    def _():
        def expert(k, carry):
            e = _expert_slot((step - 1) * eps + k)
            total = tab_ref[i * N_EXPERTS + e]
            base = e * MOE_PIECES

            def full(t, c1):
                run_pieces(k, base + t * 32, 32)
                return c1

            n_full = (total + 3) // 32
            lax.fori_loop(0, n_full, full, 0)
            done = n_full * 32
            rest = total - done

            for lo, n in ((0, 8), (8, 12), (12, 16), (16, 20), (20, 24), (24, 28)):
                @pl.when(jnp.logical_and(rest > lo, rest <= n))
                def _():
                    run_pieces(k, base + done, n)

            return carry

        lax.fori_loop(0, eps, expert, 0)

    @pl.when(step > exp_steps)
    def _():
        lane = lax.broadcasted_iota(jnp.int32, (sbk, slots), 1).astype(F32)
        for k in range(csb):
            sb = (step - 1 - exp_steps) * csb + k
            rows = slice(k * sbk, (k + 1) * sbk)
            qc = qcol_ref[rows, :]
            wmat = (jnp.where(lane == qc[:, 0:1], qc[:, 2:3], 0.0)
                    + jnp.where(lane == qc[:, 1:2], qc[:, 3:4], 0.0)).astype(BF16)
            y = _dot(wmat, xs_ref[sb, 0:slots, :])
            o_ref[rows, :] = x1_ref[rows, :] + _rms(y) * gg2_ref[...]


def _moe(tab, loc, h2, qrow3, wg, wu, wd, qcol, x1, gg2, *, tb, sbk, slots, eps, csb):
    t = h2.shape[0]
    nsb = tb // sbk
    comb_steps = nsb // csb
    exp_steps = N_EXPERTS // eps
    wblk = lambda i, s, c: (jnp.where(s == 0, exp_steps - 1, jnp.minimum(s - 1, exp_steps - 1)), 0, 0)
    oblk = lambda i, s, c: (i * comb_steps + jnp.maximum(s - 1 - exp_steps, 0), 0)
    return pl.pallas_call(
        functools.partial(_moe_kernel, sbk=sbk, slots=slots, csb=csb),
        grid_spec=pltpu.PrefetchScalarGridSpec(
            num_scalar_prefetch=1,
            grid=(t // tb, 1 + exp_steps + comb_steps),
            in_specs=[pl.BlockSpec((1, 1, N_EXPERTS * MOE_PIECES), lambda i, s, c: (i, 0, 0), memory_space=pltpu.SMEM),
                      pl.BlockSpec((tb, D_MODEL), lambda i, s, c: (i, 0)),
                      pl.BlockSpec((8, nsb, sbk), lambda i, s, c: (0, i, 0)),
                      pl.BlockSpec((eps, D_MODEL, D_EXPERT), wblk),
                      pl.BlockSpec((eps, D_MODEL, D_EXPERT), wblk),
                      pl.BlockSpec((eps, D_EXPERT, D_MODEL), wblk),
                      pl.BlockSpec((csb * sbk, LANES), oblk),
                      pl.BlockSpec((csb * sbk, D_MODEL), oblk),
                      pl.BlockSpec((1, D_MODEL), lambda i, s, c: (0, 0))],
            out_specs=pl.BlockSpec((csb * sbk, D_MODEL), oblk),
            scratch_shapes=[pltpu.VMEM((nsb, slots + 32, D_MODEL), BF16)]),
        out_shape=jax.ShapeDtypeStruct((t, D_MODEL), F32),
        compiler_params=pltpu.CompilerParams(dimension_semantics=("parallel", "arbitrary"),
                                             vmem_limit_bytes=MOE_VMEM_LIMIT),
        name="moe",
    )(tab, loc, h2, qrow3, wg, wu, wd, qcol, x1, gg2)


def _slot_tables(cnt, *, nsb, slots):
    tiles = (cnt + 15) // 16
    first = (jnp.cumsum(tiles, axis=1) - tiles) * 16
    tl = tiles.reshape(-1, nsb, N_EXPERTS)
    end = jnp.cumsum(tl, axis=1)
    beg = end - tl
    q = jnp.arange(MOE_PIECES, dtype=jnp.int32)[None, None, None, :]
    inside = jnp.logical_and(q >= beg[..., None], q < end[..., None])
    sb = jnp.arange(nsb, dtype=jnp.int32)[None, :, None, None]
    code = sb * MOE_LOC + first.reshape(-1, nsb, N_EXPERTS)[..., None] + (q - beg[..., None]) * 16
    loc = jnp.sum(jnp.where(inside, code, 0), axis=1)
    total = end[:, -1, :]
    dump = slots + 16 * (q[0, 0] % 2)
    loc = jnp.where(q[0] < total[..., None], loc, dump)
    return first, total.reshape(-1), loc.reshape(loc.shape[0], 1, -1)


def _pos_tables(rows):
    quarter = D_MODEL // 4
    freq = 1.0 / (POS_BASE ** (np.arange(quarter, dtype=np.float64) / quarter))
    r = np.arange(rows, dtype=np.float64)[:, None] * freq
    cl = np.arange(GRID_W, dtype=np.float64)[:, None] * freq
    er = np.concatenate([np.sin(r), np.cos(r)], axis=-1).astype(np.float32)
    ec = np.concatenate([np.sin(cl), np.cos(cl)], axis=-1).astype(np.float32)
    return jnp.asarray(er[:, None, :]), jnp.asarray(ec[None, :, :])


def _dft_tables(t):
    n = np.arange(CHUNK, dtype=np.int64)
    prod = n[:, None] * n[None, :]
    ang = (prod % CHUNK).astype(np.float64) * (2.0 * np.pi / CHUNK)
    c, s = np.cos(ang), np.sin(ang)
    f1 = np.concatenate([c, s], axis=0)
    a2 = np.concatenate([np.concatenate([c, -s], axis=1), np.concatenate([s, c], axis=1)], axis=0)
    cs = np.concatenate([c, -s], axis=0)
    angw = prod.astype(np.float64) * (2.0 * np.pi / t)
    f32 = lambda a: jnp.asarray(a.astype(np.float32))
    return f32(f1), f32(a2), f32(cs), f32(np.cos(angw)[:, :, None]), f32(np.sin(angw)[:, :, None])


def _blockdiag(w):
    n = w.shape[0]
    size = n * QKV_BLOCK
    spread = np.tile(np.eye(QKV_BLOCK, dtype=np.float32), (1, n))
    rows = jnp.dot(w.reshape(size, QKV_BLOCK), jnp.asarray(spread), precision=lax.Precision.HIGHEST)
    blk = np.arange(size) // QKV_BLOCK
    mask = (blk[:, None] == blk[None, :]).astype(np.float32)
    return rows * jnp.asarray(mask)


def _gate_weights(w_f, b_f, w_b, b_b):
    w = jnp.concatenate([w_f[:, :HEADS], w_b[:, :HEADS], w_f[:, HEADS:], w_b[:, HEADS:]], axis=1).T
    b = jnp.concatenate([b_f[:HEADS], b_b[:HEADS], b_f[HEADS:], b_b[HEADS:]])
    return w.astype(BF16), b[:, None]


def kernel(x, c, ctx, c_ctx, w_ada, b_ada, g_pre_mix, g_post_mix, g_pre_ffn, g_post_ffn,
           w_in, conv_w, conv_b, w_q, w_k, w_v, w_if_fwd, b_if_fwd, w_if_bwd, b_if_bwd,
           mlstm_norm_w, mlstm_skip, w_fourier, w_out, w_router_group, b_router_group,
           w_router_expert, b_router_expert, w_gate, w_up, w_down):
    t = x.shape[1]
    rows = t // GRID_W

    c8 = jnp.concatenate([c, c_ctx[None, :], jnp.zeros((6, D_MODEL), F32)], axis=0)
    mod = _ada(c8, w_ada[0], b_ada[0][None, :])
    shift1, scale1, gate1, shift2, scale2, gate2 = [mod[0:1, k * D_MODEL:(k + 1) * D_MODEL] for k in range(N_MOD)]
    shift1c, scale1c = mod[1:2, 0:D_MODEL], mod[1:2, D_MODEL:2 * D_MODEL]
    g1 = g_pre_mix[0][None, :] * (1.0 + scale1)
    g1c = g_pre_mix[0][None, :] * (1.0 + scale1c)
    gg1 = g_post_mix[0][None, :] * gate1
    g2 = g_pre_ffn[0][None, :] * (1.0 + scale2)
    gg2 = g_post_ffn[0][None, :] * gate2

    er3, ec3 = _pos_tables(rows)
    x3 = x.reshape(rows, GRID_W, D_MODEL)
    ctx3 = ctx.reshape(CTX_LEN // GRID_W, GRID_W, D_MODEL)
    w_in_bf = w_in[0].astype(BF16)

    xm_l, z_l, u_l = _inproj(x3, er3, ec3, g1, shift1, w_in_bf, rows=16, add_pos=True)
    xm_c, _, _ = _inproj(ctx3, er3, ec3, g1c, shift1c, w_in_bf, rows=CTX_LEN // GRID_W, add_pos=False)

    wq = _blockdiag(w_q[0]).astype(BF16)
    wkt = _blockdiag(w_k[0]).T.astype(BF16)
    wv = _blockdiag(w_v[0]).astype(BF16)
    wi, bi = _gate_weights(w_if_fwd[0], b_if_fwd[0], w_if_bwd[0], b_if_bwd[0])
    wiq, wik, wiv = wi[:, :D_MLSTM], wi[:, D_MLSTM:2 * D_MLSTM], wi[:, 2 * D_MLSTM:]
    cb = conv_b[0][None, :]
    f1, a2, cs, cw3, sw3 = _dft_tables(t)
    n1 = t // CHUNK
    q_l, kt_l, v_l, act_l, gp_l, yc, ys = _feat(
        xm_l, conv_w[0], cb, wq, wkt, wv, wiq, wik, wiv, bi, tm=t // (n1 // FFT_ROWS),
        fft=(u_l.reshape(FGROUPS, n1, CHUNK, FCG), f1.astype(BF16), cw3, sw3))
    q_c, kt_c, v_c, _, gp_c = _feat(xm_c, conv_w[0], cb, wq, wkt, wv, wiq, wik, wiv, bi, tm=CTX_LEN)

    dr_l, gc_l = _gates(gp_l, tl=2048)
    dr_c, gc_c = _gates(gp_c, tl=CTX_LEN)

    mix = jnp.einsum('kc,gcd->gkd', cs, w_fourier[0], precision=lax.Precision.HIGHEST)
    mix = (mix * float(1.0 / np.sqrt(float(t) * FCG))).astype(BF16)

    c0 = jnp.zeros((2 * HEADS, DH, 2 * DH), F32)
    m0 = jnp.zeros((2 * HEADS, 8, LANES), F32)
    kt_c, kt_l = kt_c.reshape(HEADS, DH, CTX_LEN), kt_l.reshape(HEADS, DH, t)
    c_ctx_fin, m_ctx_fin = _mlstm(q_c, kt_c, v_c, gc_c, dr_c, c0, m0, cps=CTX_LEN // CHUNK, emit=False)
    hf, hb, _, _, wg_bf, wu_bf, wd_bf, yf = _mlstm(
        q_l, kt_l, v_l, gc_l, dr_l, c_ctx_fin, m_ctx_fin, cps=8, emit=True,
        casts=(w_gate[0], w_up[0], w_down[0]),
        fft=(yc.reshape(FGROUPS, t, FCG), ys.reshape(FGROUPS, t, FCG), a2.astype(BF16), mix))
    yf = yf.reshape(FGROUPS, t, FCG)

    wr = jnp.concatenate([w_router_group[0], w_router_expert[0],
                          jnp.zeros((D_MODEL, LANES - N_GROUPS - N_EXPERTS), F32)], axis=1)
    br = jnp.concatenate([b_router_group[0], b_router_expert[0],
                          jnp.zeros((LANES - N_GROUPS - N_EXPERTS,), F32)])[None, :]
    x1, h2, lg = _merge(hf, hb, act_l, z_l, yf, x3, er3, ec3,
                        mlstm_norm_w[0][None, :], mlstm_skip[0][None, :], w_out[0].astype(BF16),
                        gg1, g2, shift2, wr.astype(BF16), br, rows=16)
    pos, wts, cnt = _route(lg, tl=MOE_TB, sbk=MOE_SBK)
    nblk, nsb = t // MOE_TB, MOE_TB // MOE_SBK
    cnt = cnt.reshape(nblk, N_EXPERTS, LANES)[:, :, :nsb]
    cnt = jnp.transpose(cnt, (0, 2, 1)).reshape(nblk * nsb, N_EXPERTS).astype(jnp.int32)
    first, total, loc = _slot_tables(cnt, nsb=nsb, slots=MOE_SLOTS)
    first_rows = jnp.repeat(first.T.astype(F32), MOE_SBK, axis=1)
    qrow, qcol = _slots(pos, wts, first_rows, tl=MOE_TB)
    out = _moe(total, loc, h2, qrow.reshape(8, t // MOE_SBK, MOE_SBK), wg_bf, wu_bf, wd_bf,
               qcol, x1, gg2, tb=MOE_TB, sbk=MOE_SBK, slots=MOE_SLOTS, eps=MOE_EPS, csb=MOE_CSB)
    return out[None]
```
